```python
import jax, jax.numpy as jnp
from jax import lax
import numpy as np

D_MODEL = 1024
BATCH = 8
SEQ = 2048
DEPTH = 1
DEC_BATCH = 32
DEC_SEQ = 1
PAST_LEN = 8192
PAGE_SIZE = 128

HEAD_DIM = 64
D_ATTN = D_MODEL // 2
N_HEADS = D_ATTN // HEAD_DIM
D_CONV = D_MODEL - D_ATTN
CONV_WIDTH = 3
WIN_KEYS = 128
DILATIONS = (1, 4, 16)
MAX_WINDOW = WIN_KEYS * max(DILATIONS)
N_GROUPS = 4
EXPERTS_PER_GROUP = 8
N_EXPERTS = N_GROUPS * EXPERTS_PER_GROUP
TOP_K = 2
D_EXPERT = D_MODEL // 2
MOE_BLOCK = 128
EPS = 1e-6
NEG = -1e30

kernel_name = 'hymba_dilated_shortconv_hmoe_step'


def _rmsnorm(x, g):
    xf = x.astype(jnp.float32)
    y = xf * lax.rsqrt(jnp.mean(xf * xf, axis=-1, keepdims=True) + EPS) * g.astype(jnp.float32)
    return y.astype(x.dtype)


def _mix_in(h, w_in):
    proj = h @ w_in
    q, k, v, b, c, hv = jnp.split(
        proj, [D_ATTN, 2 * D_ATTN, 3 * D_ATTN, 3 * D_ATTN + D_CONV, 3 * D_ATTN + 2 * D_CONV], axis=-1)
    shp = h.shape[:-1] + (N_HEADS, HEAD_DIM)
    return q.reshape(shp), k.reshape(shp), v.reshape(shp), b, c * hv


def _combine(parts):
    m = jnp.stack([p[0] for p in parts])
    l = jnp.stack([p[1] for p in parts])
    acc = jnp.stack([p[2] for p in parts])
    w = jnp.exp(m - m.max(axis=0))
    return (w[..., None] * acc).sum(0) / (w * l).sum(0)[..., None]


def _branch_prompt(q, k, v, dil):
    B, S, H, Dh = q.shape
    span = dil * WIN_KEYS
    Sp = -(-S // span) * span
    nb = Sp // span

    def blocks(t):
        t = jnp.pad(t, ((0, 0), (0, Sp - S), (0, 0), (0, 0))).reshape(B, Sp // dil, dil, H, Dh)
        return t.transpose(0, 2, 1, 3, 4).reshape(B, dil, nb, WIN_KEYS, H, Dh)

    def with_prev(t):
        prev = jnp.pad(t[:, :, :-1], ((0, 0), (0, 0), (1, 0), (0, 0), (0, 0), (0, 0)))
        return jnp.concatenate([prev, t], axis=3)

    qb = blocks(q)
    kw, vw = with_prev(blocks(k)), with_prev(blocks(v))
    s = jnp.einsum('bgnrhd,bgnchd->bgnhrc', qb, kw,
                   preferred_element_type=jnp.float32) * (HEAD_DIM ** -0.5)
    r = jnp.arange(WIN_KEYS)[:, None]
    c = jnp.arange(2 * WIN_KEYS)[None, :]
    band = (c >= r) & (c <= r + WIN_KEYS)
    has_prev = (jnp.arange(nb) > 0)[:, None, None] | (c >= WIN_KEYS)[None]
    valid = band[None] & has_prev
    s = jnp.where(valid[None, None, :, None], s, NEG)
    m = s.max(-1)
    p = jnp.exp(s - m[..., None])
    l = p.sum(-1)
    acc = jnp.einsum('bgnhrc,bgnchd->bgnrhd', p, vw.astype(jnp.float32))
    m = m.transpose(0, 2, 4, 1, 3).reshape(B, Sp, H)[:, :S]
    l = l.transpose(0, 2, 4, 1, 3).reshape(B, Sp, H)[:, :S]
    acc = acc.transpose(0, 2, 3, 1, 4, 5).reshape(B, Sp, H, Dh)[:, :S]
    return m, l, acc


def _branch_sample(q, kk, vv, dil, w_buf):
    DS = q.shape[1]
    idx = (w_buf + jnp.arange(DS))[:, None] - dil * jnp.arange(WIN_KEYS + 1)[None, :]
    valid = idx >= 0
    idx = jnp.maximum(idx, 0)
    kg = jnp.take(kk, idx, axis=1)
    vg = jnp.take(vv, idx, axis=1)
    s = jnp.einsum('bjhd,bjihd->bjhi', q, kg,
                   preferred_element_type=jnp.float32) * (HEAD_DIM ** -0.5)
    s = jnp.where(valid[None, :, None, :], s, NEG)
    m = s.max(-1)
    p = jnp.exp(s - m[..., None])
    l = p.sum(-1)
    acc = jnp.einsum('bjhi,bjihd->bjhd', p, vg.astype(jnp.float32))
    return m, l, acc


def _short_conv(u, state, w):
    cat = jnp.concatenate([state.astype(u.dtype), u], axis=1)
    L = u.shape[1]
    z = cat[:, 0:L] * w[0]
    for j in range(1, CONV_WIDTH):
        z = z + cat[:, j:j + L] * w[j]
    return z, cat[:, -(CONV_WIDTH - 1):]


def _mix_out(attn, conv, g_a, g_c, w_out):
    a = attn.reshape(attn.shape[:-2] + (D_ATTN,))
    o = jnp.concatenate([_rmsnorm(a, g_a), _rmsnorm(conv, g_c)], axis=-1)
    return o @ w_out


def _moe(x, w_rg, b_rg, w_re, b_re, w_gate, w_up, w_down):
    T, D = x.shape
    lg = jnp.matmul(x, w_rg, preferred_element_type=jnp.float32) + b_rg
    p_group = jax.nn.softmax(lg, axis=-1)
    g_sel = jnp.argmax(lg, axis=-1)
    le = (jnp.matmul(x, w_re, preferred_element_type=jnp.float32) + b_re).reshape(
        T, N_GROUPS, EXPERTS_PER_GROUP)
    le_sel = jnp.take_along_axis(le, g_sel[:, None, None], axis=1)[:, 0]
    top_v, top_i = lax.top_k(le_sel, TOP_K)
    gates = jax.nn.softmax(top_v, axis=-1) * jnp.take_along_axis(p_group, g_sel[:, None], axis=1)
    e_flat = (g_sel[:, None] * EXPERTS_PER_GROUP + top_i).reshape(-1)
    A = T * TOP_K
    tok_flat = jnp.repeat(jnp.arange(T), TOP_K)
    order = jnp.argsort(e_flat)
    e_s, tok_s, gate_s = e_flat[order], tok_flat[order], gates.reshape(-1)[order]
    counts = jax.ops.segment_sum(jnp.ones((A,), jnp.int32), e_flat, num_segments=N_EXPERTS)
    start = jnp.cumsum(counts) - counts
    padded = (counts + MOE_BLOCK - 1) // MOE_BLOCK * MOE_BLOCK
    pend = jnp.cumsum(padded)
    pstart = pend - padded
    dest = pstart[e_s] + jnp.arange(A) - start[e_s]
    n_blocks = -(-A // MOE_BLOCK) + N_EXPERTS
    xbuf = jnp.zeros((n_blocks * MOE_BLOCK, D), x.dtype).at[dest].set(x[tok_s])
    block_e = jnp.minimum(
        jnp.searchsorted(pend, jnp.arange(n_blocks) * MOE_BLOCK, side='right'), N_EXPERTS - 1)

    def expert_block(args):
        xb, e = args
        hid = jax.nn.silu(xb @ w_gate[e]) * (xb @ w_up[e])
        return hid @ w_down[e]

    ybuf = lax.map(expert_block, (xbuf.reshape(n_blocks, MOE_BLOCK, D), block_e)).reshape(-1, D)
    out = jax.ops.segment_sum(gate_s[:, None] * ybuf[dest].astype(jnp.float32), tok_s, num_segments=T)
    return out.astype(x.dtype)


def setup_inputs(seed: int = 0) -> dict:
    key = jax.random.key(seed)
    ks = jax.random.split(key, 20)
    w_buf = min(MAX_WINDOW, PAST_LEN)

    def nrm(k, shape, scale):
        return jax.random.normal(k, shape, jnp.float32) * scale

    def gain(k, shape):
        return 1.0 + 0.01 * jax.random.normal(k, shape, jnp.float32)

    return {
        'x_prompt': nrm(ks[0], (BATCH, SEQ, D_MODEL), 1.0),
        'x_sample': nrm(ks[1], (DEC_BATCH, DEC_SEQ, D_MODEL), 1.0),
        'cache_k': nrm(ks[2], (DEPTH, DEC_BATCH, w_buf, N_HEADS, HEAD_DIM), 1.0),
        'cache_v': nrm(ks[3], (DEPTH, DEC_BATCH, w_buf, N_HEADS, HEAD_DIM), 1.0),
        'state_conv': nrm(ks[4], (DEPTH, DEC_BATCH, CONV_WIDTH - 1, D_CONV), 1.0),
        'norm_mix': gain(ks[5], (DEPTH, D_MODEL)),
        'w_in': nrm(ks[6], (DEPTH, D_MODEL, 3 * D_ATTN + 3 * D_CONV), D_MODEL ** -0.5),
        'conv_w': nrm(ks[7], (DEPTH, CONV_WIDTH, D_CONV), CONV_WIDTH ** -0.5),
        'norm_out_attn': gain(ks[8], (DEPTH, D_ATTN)),
        'norm_out_conv': gain(ks[9], (DEPTH, D_CONV)),
        'w_out': nrm(ks[10], (DEPTH, D_MODEL, D_MODEL), D_MODEL ** -0.5),
        'norm_ffn': gain(ks[11], (DEPTH, D_MODEL)),
        'w_router_group': nrm(ks[12], (DEPTH, D_MODEL, N_GROUPS), D_MODEL ** -0.5),
        'b_router_group': nrm(ks[13], (DEPTH, N_GROUPS), 0.01),
        'w_router_expert': nrm(ks[14], (DEPTH, D_MODEL, N_EXPERTS), D_MODEL ** -0.5),
        'b_router_expert': nrm(ks[15], (DEPTH, N_EXPERTS), 0.01),
        'w_gate': nrm(ks[16], (DEPTH, N_EXPERTS, D_MODEL, D_EXPERT), D_MODEL ** -0.5),
        'w_up': nrm(ks[17], (DEPTH, N_EXPERTS, D_MODEL, D_EXPERT), D_MODEL ** -0.5),
        'w_down': nrm(ks[18], (DEPTH, N_EXPERTS, D_EXPERT, D_MODEL), D_EXPERT ** -0.5),
        'norm_final': gain(ks[19], (D_MODEL,)),
    }


def reference(x_prompt, x_sample, cache_k, cache_v, state_conv, norm_mix, w_in, conv_w,
              norm_out_attn, norm_out_conv, w_out, norm_ffn, w_router_group, b_router_group,
              w_router_expert, b_router_expert, w_gate, w_up, w_down, norm_final):
    B, S, D = x_prompt.shape
    DB, DS, _ = x_sample.shape
    w_buf = cache_k.shape[2]
    w_keep = min(MAX_WINDOW, S)
    hp, hs = x_prompt, x_sample
    kp_l, vp_l, cp_l, ks_l, vs_l, cs_l = [], [], [], [], [], []
    for layer in range(DEPTH):
        qp, kp, vp, bp, up = _mix_in(_rmsnorm(hp, norm_mix[layer]), w_in[layer])
        qs, ks, vs, bs, us = _mix_in(_rmsnorm(hs, norm_mix[layer]), w_in[layer])

        attn_p = _combine([_branch_prompt(qp, kp, vp, d) for d in DILATIONS]).astype(qp.dtype)
        kk = jnp.concatenate([cache_k[layer].astype(ks.dtype), ks], axis=1)
        vv = jnp.concatenate([cache_v[layer].astype(vs.dtype), vs], axis=1)
        attn_s = _combine([_branch_sample(qs, kk, vv, d, w_buf) for d in DILATIONS]).astype(qs.dtype)

        conv_p, st_p = _short_conv(up, jnp.zeros((B, CONV_WIDTH - 1, D_CONV), up.dtype), conv_w[layer])
        conv_s, st_s = _short_conv(us, state_conv[layer], conv_w[layer])

        hp = hp + _mix_out(attn_p, bp * conv_p, norm_out_attn[layer], norm_out_conv[layer], w_out[layer])
        hs = hs + _mix_out(attn_s, bs * conv_s, norm_out_attn[layer], norm_out_conv[layer], w_out[layer])

        tok = jnp.concatenate([_rmsnorm(hp, norm_ffn[layer]).reshape(B * S, D),
                               _rmsnorm(hs, norm_ffn[layer]).reshape(DB * DS, D)], axis=0)
        f = _moe(tok, w_router_group[layer], b_router_group[layer], w_router_expert[layer],
                 b_router_expert[layer], w_gate[layer], w_up[layer], w_down[layer])
        hp = hp + f[:B * S].reshape(B, S, D)
        hs = hs + f[B * S:].reshape(DB, DS, D)

        kp_l.append(kp[:, S - w_keep:])
        vp_l.append(vp[:, S - w_keep:])
        cp_l.append(st_p)
        ks_l.append(ks)
        vs_l.append(vs)
        cs_l.append(st_s)
    y_prompt = _rmsnorm(hp, norm_final)
    y_sample = _rmsnorm(hs, norm_final)
    return (y_prompt, y_sample, jnp.stack(kp_l), jnp.stack(vp_l), jnp.stack(cp_l),
            jnp.stack(ks_l), jnp.stack(vs_l), jnp.stack(cs_l))
```

```python
import functools

import jax
import jax.numpy as jnp
from jax import lax
from jax.experimental import pallas as pl
from jax.experimental.pallas import tpu as pltpu

HEAD_DIM = 64
WIN_KEYS = 128
DILATIONS = (1, 4, 16)
CONV_WIDTH = 3
N_GROUPS = 4
EXPERTS_PER_GROUP = 8
N_EXPERTS = N_GROUPS * EXPERTS_PER_GROUP
EPS = 1e-6
NEG = -1e30

LANES = 128
ROW_TILE = 512
EXPERT_BLOCK = 256
VMEM_LIMIT = 56 * 1024 * 1024

F32 = jnp.float32
BF16 = jnp.bfloat16


def _rms(x, g):
    return x * lax.rsqrt(jnp.mean(x * x, axis=-1, keepdims=True) + EPS) * g


def _mix_in_kernel(x_ref, g_ref, w_ref, cw_ref, gc_ref, st0_ref, st1_ref,
                   q_ref, k_ref, v_ref, oc_ref, u_ref, carry_ref, *, d_attn, d_conv, sequential):
    x = x_ref[...]
    xb = _rms(x, g_ref[...]).astype(BF16)

    def proj(lo, width):
        return jnp.dot(xb, w_ref[:, lo:lo + width], preferred_element_type=F32)

    q_ref[...] = proj(0, d_attn)
    k_ref[...] = proj(d_attn, d_attn)
    v_ref[...] = proj(2 * d_attn, d_attn)
    gate = proj(3 * d_attn, d_conv)
    u = proj(3 * d_attn + d_conv, d_conv) * proj(3 * d_attn + 2 * d_conv, d_conv)
    u_ref[...] = u

    tm = x.shape[0]
    if sequential:
        @pl.when(pl.program_id(1) == 0)
        def _():
            carry_ref[...] = jnp.zeros_like(carry_ref)

        row = lax.broadcasted_iota(jnp.int32, u.shape, 0)
        prev1 = carry_ref[1:2, :]
        prev2 = carry_ref[0:1, :]
        u1 = jnp.where(row == 0, prev1, pltpu.roll(u, 1, axis=0))
        u2 = jnp.where(row == 0, prev2, jnp.where(row == 1, prev1, pltpu.roll(u, 2, axis=0)))
        carry_ref[0:2, :] = u[tm - 2:tm, :]
    else:
        u2 = st0_ref[...]
        u1 = st1_ref[...]
    z = u2 * cw_ref[0:1, :] + u1 * cw_ref[1:2, :] + u * cw_ref[2:3, :]
    oc_ref[...] = _rms(gate * z, gc_ref[...])


def _mix_in(x2d, norm_g, w_in_bf16, conv_w, norm_gc, st0, st1, *, seq_len, d_attn, d_conv):
    t, d = x2d.shape
    sequential = seq_len is not None
    tm = min(ROW_TILE, seq_len) if sequential else t
    if sequential:
        n_seq, per = t // seq_len, seq_len // tm
        grid = (n_seq, per)
        row_map = lambda b, s: (b * per + s, 0)
    else:
        grid = (1, 1)
        row_map = lambda b, s: (0, 0)
    const = lambda b, s: (0, 0)
    n_in = w_in_bf16.shape[1]
    row_spec = lambda width: pl.BlockSpec((tm, width), row_map)
    st_spec = pl.BlockSpec(st0.shape, const)
    outs = [jax.ShapeDtypeStruct((t, d_attn), F32)] * 3 + [jax.ShapeDtypeStruct((t, d_conv), F32)] * 2
    return pl.pallas_call(
        functools.partial(_mix_in_kernel, d_attn=d_attn, d_conv=d_conv, sequential=sequential),
        grid=grid,
        in_specs=[row_spec(d), pl.BlockSpec((1, d), const), pl.BlockSpec((d, n_in), const),
                  pl.BlockSpec((CONV_WIDTH, d_conv), const), pl.BlockSpec((1, d_conv), const),
                  st_spec, st_spec],
        out_specs=[row_spec(d_attn)] * 3 + [row_spec(d_conv)] * 2,
        out_shape=outs,
        scratch_shapes=[pltpu.VMEM((8, d_conv), F32)],
        compiler_params=pltpu.CompilerParams(
            dimension_semantics=("arbitrary", "arbitrary"), vmem_limit_bytes=VMEM_LIMIT),
        name="mix_in",
    )(x2d, norm_g, w_in_bf16, conv_w, norm_gc, st0, st1)


def _attn_prompt_kernel(q_ref, k_ref, v_ref, o_ref, m_s, l_s, a_s, *, seq_len):
    w = WIN_KEYS
    scale = HEAD_DIM ** -0.5
    r_i = lax.broadcasted_iota(jnp.int32, (w, w), 0)
    c_i = lax.broadcasted_iota(jnp.int32, (w, w), 1)
    mask_cur = c_i <= r_i
    mask_prev = c_i >= r_i
    first_head = lax.broadcasted_iota(jnp.int32, (w, 2 * HEAD_DIM), 1) < HEAD_DIM
    dn_t = (((1,), (1,)), ((), ()))

    def branch(q, kc, vc, kp, vp, prev_bias):
        ms, ls, accs = [], [], []
        for h in range(2):
            hs = slice(h * HEAD_DIM, (h + 1) * HEAD_DIM)
            qh = (q[:, hs] * scale).astype(BF16)
            s_c = lax.dot_general(qh, kc[:, hs].astype(BF16), dn_t, preferred_element_type=F32)
            s_c = jnp.where(mask_cur, s_c, NEG)
            m = jnp.max(s_c, axis=-1, keepdims=True)
            if kp is not None:
                s_p = lax.dot_general(qh, kp[:, hs].astype(BF16), dn_t, preferred_element_type=F32)
                s_p = jnp.where(mask_prev, s_p + prev_bias, NEG)
                m = jnp.maximum(m, jnp.max(s_p, axis=-1, keepdims=True))
            p_c = jnp.exp(s_c - m)
            l = jnp.sum(p_c, axis=-1, keepdims=True)
            acc = jnp.dot(p_c.astype(BF16), vc[:, hs].astype(BF16), preferred_element_type=F32)
            if kp is not None:
                p_p = jnp.exp(s_p - m)
                l = l + jnp.sum(p_p, axis=-1, keepdims=True)
                acc = acc + jnp.dot(p_p.astype(BF16), vp[:, hs].astype(BF16),
                                    preferred_element_type=F32)
            ms.append(m)
            ls.append(l)
            accs.append(acc)
        m2 = jnp.where(first_head, ms[0], ms[1])
        l2 = jnp.where(first_head, ls[0], ls[1])
        return m2, l2, jnp.concatenate(accs, axis=-1)

    def rows(start, dil):
        if dil == 1:
            return pl.ds(pl.multiple_of(start, w), w)
        return pl.ds(start, w, stride=dil)

    def run_branch(dil, first, last):
        span = dil * w
        nb = seq_len // span

        def body(it, carry):
            g = it // nb
            n = it % nb
            cur = rows(g + n * span, dil)
            q = q_ref[cur, :]
            kc = k_ref[cur, :]
            vc = v_ref[cur, :]
            if nb > 1:
                prev = rows(g + jnp.maximum(n - 1, 0) * span, dil)
                prev_bias = jnp.where(n > 0, 0.0, NEG).astype(F32)
                m_b, l_b, a_b = branch(q, kc, vc, k_ref[prev, :], v_ref[prev, :], prev_bias)
            else:
                m_b, l_b, a_b = branch(q, kc, vc, None, None, None)
            if not first:
                m_o = m_s[cur, :]
                m_n = jnp.maximum(m_o, m_b)
                w_o = jnp.exp(m_o - m_n)
                w_b = jnp.exp(m_b - m_n)
                l_b = w_o * l_s[cur, :] + w_b * l_b
                a_b = w_o * a_s[cur, :] + w_b * a_b
                m_b = m_n
            if last:
                o_ref[cur, :] = a_b / l_b
            else:
                m_s[cur, :] = m_b
                l_s[cur, :] = l_b
                a_s[cur, :] = a_b
            return carry

        lax.fori_loop(0, dil * nb, body, 0)

    order = sorted(DILATIONS, reverse=True)
    for i, dil in enumerate(order):
        run_branch(dil, i == 0, i == len(order) - 1)


def _attn_prompt(q, k, v, *, n_seq, seq_len):
    t, d_attn = q.shape
    pair = 2 * HEAD_DIM
    spec = pl.BlockSpec((seq_len, pair), lambda b, h: (b, h))
    return pl.pallas_call(
        functools.partial(_attn_prompt_kernel, seq_len=seq_len),
        grid=(n_seq, d_attn // pair),
        in_specs=[spec] * 3,
        out_specs=spec,
        out_shape=jax.ShapeDtypeStruct((t, d_attn), F32),
        scratch_shapes=[pltpu.VMEM((seq_len, pair), F32)] * 3,
        compiler_params=pltpu.CompilerParams(
            dimension_semantics=("arbitrary", "arbitrary"), vmem_limit_bytes=VMEM_LIMIT),
        name="attn_prompt",
    )(q, k, v)


def _attn_sample_kernel(q_ref, kn_ref, vn_ref, *refs):
    o_ref = refs[-1]
    slabs = refs[:-1]
    q = q_ref[...] * (HEAD_DIM ** -0.5)
    kn = kn_ref[...]
    vn = vn_ref[...]
    s_self = jnp.sum(q * kn, axis=-1, keepdims=True)
    stats = []
    for b in range(len(slabs) // 2):
        kc = slabs[2 * b][...]
        vc = slabs[2 * b + 1][...]
        s = jnp.sum(kc * q[None], axis=-1, keepdims=True)
        m = jnp.maximum(jnp.max(s, axis=0), s_self)
        p = jnp.exp(s - m[None])
        p_self = jnp.exp(s_self - m)
        l = jnp.sum(p, axis=0) + p_self
        acc = jnp.sum(p * vc, axis=0) + p_self * vn
        stats.append((m, l, acc))
    m_all = functools.reduce(jnp.maximum, [s[0] for s in stats])
    num = 0.0
    den = 0.0
    for m, l, acc in stats:
        wgt = jnp.exp(m - m_all)
        num = num + wgt * acc
        den = den + wgt * l
    o_ref[...] = num / den


def _attn_sample(q, k_new, v_new, cache_k, cache_v):
    db, w_buf, n_heads, dh = cache_k.shape
    w = WIN_KEYS
    head_spec = pl.BlockSpec((None, n_heads, dh), lambda b: (b, 0, 0))
    slabs, slab_specs = [], []
    for dil in DILATIONS:
        assert w_buf % (dil * w) == 0
        n_str = w_buf // dil
        last = n_str // w - 1
        for c in (cache_k, cache_v):
            slabs.append(c.reshape(db, n_str, dil, n_heads, dh))
            slab_specs.append(pl.BlockSpec((None, w, None, n_heads, dh),
                                           lambda b, last=last: (b, last, 0, 0, 0)))
    return pl.pallas_call(
        _attn_sample_kernel,
        grid=(db,),
        in_specs=[head_spec] * 3 + slab_specs,
        out_specs=head_spec,
        out_shape=jax.ShapeDtypeStruct((db, n_heads, dh), F32),
        compiler_params=pltpu.CompilerParams(
            dimension_semantics=("arbitrary",), vmem_limit_bytes=VMEM_LIMIT),
        name="attn_sample",
    )(q, k_new, v_new, *slabs)


R_E0, R_E1, R_G0, R_G1, R_RANK0, R_RANK1 = range(6)
ROUTER_LANE0 = N_GROUPS


def _mix_out_kernel(x_ref, a_ref, oc_ref, ga_ref, wo_ref, gf_ref, wr_ref, br_ref, cnt_in_ref,
                    h_ref, tok_ref, route_ref, cnt_ref):
    d_attn = a_ref.shape[1]
    tm = x_ref.shape[0]
    a = _rms(a_ref[...], ga_ref[...]).astype(BF16)
    mix = jnp.dot(a, wo_ref[0:d_attn, :], preferred_element_type=F32)
    mix = mix + jnp.dot(oc_ref[...].astype(BF16), wo_ref[d_attn:, :], preferred_element_type=F32)
    h = x_ref[...] + mix
    h_ref[...] = h
    tok = _rms(h, gf_ref[...])
    tok_ref[...] = tok

    logits = jnp.dot(tok, wr_ref[...], precision=lax.Precision.HIGHEST,
                     preferred_element_type=F32) + br_ref[...]
    lane = lax.broadcasted_iota(jnp.int32, logits.shape, 1)
    big = jnp.int32(LANES)
    neg_inf = jnp.float32(-jnp.inf)

    def top1(vals):
        best = jnp.max(vals, axis=-1, keepdims=True)
        idx = jnp.min(jnp.where(vals == best, lane, big), axis=-1, keepdims=True)
        return best, idx

    is_group = lane < N_GROUPS
    lg = jnp.where(is_group, logits, neg_inf)
    mg, g_sel = top1(lg)
    p_group = 1.0 / jnp.sum(jnp.where(is_group, jnp.exp(lg - mg), 0.0), axis=-1, keepdims=True)

    lo = ROUTER_LANE0 + g_sel * EXPERTS_PER_GROUP
    in_group = jnp.logical_and(lane >= lo, lane < lo + EXPERTS_PER_GROUP)
    le = jnp.where(in_group, logits, neg_inf)
    v1, i1 = top1(le)
    v2, i2 = top1(jnp.where(lane == i1, neg_inf, le))
    e2 = jnp.exp(v2 - v1)
    gate1 = p_group / (1.0 + e2)
    gate2 = p_group * e2 / (1.0 + e2)

    @pl.when(pl.program_id(0) == 0)
    def _():
        cnt_ref[...] = cnt_in_ref[...]

    oh1 = lane == i1
    oh2 = lane == i2
    both = jnp.where(jnp.logical_or(oh1, oh2), 1.0, 0.0)
    r_i = lax.broadcasted_iota(jnp.int32, (tm, tm), 0)
    c_i = lax.broadcasted_iota(jnp.int32, (tm, tm), 1)
    strict_lower = jnp.where(c_i < r_i, 1.0, 0.0).astype(BF16)
    before = jnp.dot(strict_lower, both.astype(BF16), preferred_element_type=F32) + cnt_ref[0:1, :]
    rank1 = jnp.sum(jnp.where(oh1, before, 0.0), axis=-1, keepdims=True)
    rank2 = jnp.sum(jnp.where(oh2, before, 0.0), axis=-1, keepdims=True)
    cnt_ref[0:1, :] = cnt_ref[0:1, :] + jnp.sum(both, axis=0, keepdims=True)

    rec = jnp.zeros(logits.shape, F32)
    for col, val in ((R_E0, (i1 - ROUTER_LANE0).astype(F32)), (R_E1, (i2 - ROUTER_LANE0).astype(F32)),
                     (R_G0, gate1), (R_G1, gate2), (R_RANK0, rank1), (R_RANK1, rank2)):
        rec = jnp.where(lane == col, val, rec)
    route_ref[...] = rec


def _mix_out(x2d, attn, oconv, norm_ga, w_out_bf16, norm_gf, w_router, b_router, cnt_in):
    t, d = x2d.shape
    d_attn, d_conv = attn.shape[1], oconv.shape[1]
    tm = min(ROW_TILE, t)
    row = lambda width: pl.BlockSpec((tm, width), lambda i: (i, 0))
    full = lambda arr: pl.BlockSpec(arr.shape, lambda i: (0, 0))
    return pl.pallas_call(
        _mix_out_kernel,
        grid=(t // tm,),
        in_specs=[row(d), row(d_attn), row(d_conv), full(norm_ga), full(w_out_bf16), full(norm_gf),
                  full(w_router), full(b_router), full(cnt_in)],
        out_specs=[row(d), row(d), row(LANES), pl.BlockSpec((8, LANES), lambda i: (0, 0))],
        out_shape=[jax.ShapeDtypeStruct((t, d), F32), jax.ShapeDtypeStruct((t, d), F32),
                   jax.ShapeDtypeStruct((t, LANES), F32), jax.ShapeDtypeStruct((8, LANES), F32)],
        compiler_params=pltpu.CompilerParams(
            dimension_semantics=("arbitrary",), vmem_limit_bytes=VMEM_LIMIT),
        name="mix_out",
    )(x2d, attn, oconv, norm_ga, w_out_bf16, norm_gf, w_router, b_router, cnt_in)


def _row_copy(src, src_row, dst, dst_row, sem):
    return pltpu.make_async_copy(src.at[pl.ds(src_row, 1)], dst.at[pl.ds(dst_row, 1)], sem)


def _dispatch_kernel(dest_ref, tok_ref, xbuf_in_ref, xbuf_ref, sem):
    del xbuf_in_ref
    tm = tok_ref.shape[0]
    base = pl.program_id(0) * tm

    def issue(r, carry):
        for j in range(2):
            _row_copy(tok_ref, r, xbuf_ref, dest_ref[2 * (base + r) + j], sem).start()
        return carry

    lax.fori_loop(0, tm, issue, 0)

    def drain(r, carry):
        for j in range(2):
            _row_copy(tok_ref, r, xbuf_ref, dest_ref[2 * (base + r) + j], sem).wait()
        return carry

    lax.fori_loop(0, tm, drain, 0)


def _dispatch(dest, tok, xbuf):
    t, d = tok.shape
    tm = min(ROW_TILE, t)
    return pl.pallas_call(
        _dispatch_kernel,
        grid_spec=pltpu.PrefetchScalarGridSpec(
            num_scalar_prefetch=1,
            grid=(t // tm,),
            in_specs=[pl.BlockSpec((tm, d), lambda i, dest: (i, 0)),
                      pl.BlockSpec(memory_space=pl.ANY)],
            out_specs=pl.BlockSpec(memory_space=pl.ANY),
            scratch_shapes=[pltpu.SemaphoreType.DMA(())],
        ),
        out_shape=jax.ShapeDtypeStruct(xbuf.shape, xbuf.dtype),
        input_output_aliases={2: 0},
        compiler_params=pltpu.CompilerParams(
            dimension_semantics=("arbitrary",), vmem_limit_bytes=VMEM_LIMIT),
        name="dispatch",
    )(dest, tok, xbuf)


def _experts_kernel(block_e_ref, n_used_ref, x_ref, wg_ref, wu_ref, wd_ref, y_ref):
    del block_e_ref
    b = pl.program_id(0)

    @pl.when(b < n_used_ref[0])
    def _():
        x = x_ref[...].astype(BF16)
        gate = jnp.dot(x, wg_ref[...].astype(BF16), preferred_element_type=F32)
        up = jnp.dot(x, wu_ref[...].astype(BF16), preferred_element_type=F32)
        hid = gate * (1.0 / (1.0 + jnp.exp(-gate))) * up
        y_ref[...] = jnp.dot(hid.astype(BF16), wd_ref[...].astype(BF16), preferred_element_type=F32)

    @pl.when(b >= n_used_ref[0])
    def _():
        y_ref[...] = jnp.zeros_like(y_ref)


def _experts(block_e, n_used, xbuf, w_gate, w_up, w_down):
    rows, d = xbuf.shape
    _, _, d_exp = w_gate.shape
    blk = EXPERT_BLOCK
    return pl.pallas_call(
        _experts_kernel,
        grid_spec=pltpu.PrefetchScalarGridSpec(
            num_scalar_prefetch=2,
            grid=(rows // blk,),
            in_specs=[pl.BlockSpec((blk, d), lambda b, be, nu: (b, 0)),
                      pl.BlockSpec((None, d, d_exp), lambda b, be, nu: (be[b], 0, 0)),
                      pl.BlockSpec((None, d, d_exp), lambda b, be, nu: (be[b], 0, 0)),
                      pl.BlockSpec((None, d_exp, d), lambda b, be, nu: (be[b], 0, 0))],
            out_specs=pl.BlockSpec((blk, d), lambda b, be, nu: (b, 0)),
        ),
        out_shape=jax.ShapeDtypeStruct((rows, d), F32),
        compiler_params=pltpu.CompilerParams(
            dimension_semantics=("arbitrary",), vmem_limit_bytes=VMEM_LIMIT),
        name="experts",
    )(block_e, n_used, xbuf, w_gate, w_up, w_down)


def _combine_kernel(dest_ref, h_ref, route_ref, gn_ref, ybuf_ref, o_ref, y0_ref, y1_ref, sem):
    tm = h_ref.shape[0]
    base = pl.program_id(0) * tm
    bufs = (y0_ref, y1_ref)

    def issue(r, carry):
        for j in range(2):
            _row_copy(ybuf_ref, dest_ref[2 * (base + r) + j], bufs[j], r, sem).start()
        return carry

    lax.fori_loop(0, tm, issue, 0)

    def drain(r, carry):
        for j in range(2):
            _row_copy(ybuf_ref, dest_ref[2 * (base + r) + j], bufs[j], r, sem).wait()
        return carry

    lax.fori_loop(0, tm, drain, 0)

    route = route_ref[...]
    g0 = route[:, R_G0:R_G0 + 1]
    g1 = route[:, R_G1:R_G1 + 1]
    f = g0 * y0_ref[...] + g1 * y1_ref[...]
    o_ref[...] = _rms(h_ref[...] + f, gn_ref[...])


def _combine(dest, h, route, norm_g, ybuf):
    t, d = h.shape
    tm = min(ROW_TILE, t)
    return pl.pallas_call(
        _combine_kernel,
        grid_spec=pltpu.PrefetchScalarGridSpec(
            num_scalar_prefetch=1,
            grid=(t // tm,),
            in_specs=[pl.BlockSpec((tm, d), lambda i, dest: (i, 0)),
                      pl.BlockSpec((tm, LANES), lambda i, dest: (i, 0)),
                      pl.BlockSpec((1, d), lambda i, dest: (0, 0)),
                      pl.BlockSpec(memory_space=pl.ANY)],
            out_specs=pl.BlockSpec((tm, d), lambda i, dest: (i, 0)),
            scratch_shapes=[pltpu.VMEM((tm, d), F32), pltpu.VMEM((tm, d), F32),
                            pltpu.SemaphoreType.DMA(())],
        ),
        out_shape=jax.ShapeDtypeStruct((t, d), F32),
        compiler_params=pltpu.CompilerParams(
            dimension_semantics=("arbitrary",), vmem_limit_bytes=VMEM_LIMIT),
        name="combine",
    )(dest, h, route, norm_g, ybuf)


def kernel(x_prompt, x_sample, cache_k, cache_v, state_conv, norm_mix, w_in, conv_w, norm_out_attn,
           norm_out_conv, w_out, norm_ffn, w_router_group, b_router_group, w_router_expert,
           b_router_expert, w_gate, w_up, w_down, norm_final):
    n_seq, seq_len, d = x_prompt.shape
    db, ds, _ = x_sample.shape
    depth = w_in.shape[0]
    _, _, w_buf, n_heads, dh = cache_k.shape
    d_attn = n_heads * dh
    d_conv = d - d_attn
    assert depth == 1 and ds == 1 and dh == HEAD_DIM
    assert seq_len % (max(DILATIONS) * WIN_KEYS) == 0 and seq_len <= max(DILATIONS) * WIN_KEYS
    layer = 0
    tp, ts = n_seq * seq_len, db

    xp = x_prompt.reshape(tp, d)
    xs = x_sample.reshape(ts, d)
    row = lambda vec: vec.reshape(1, -1)
    w_in_b = w_in[layer].astype(BF16)
    w_out_b = w_out[layer].astype(BF16)
    g_mix, g_oa, g_oc, g_ffn = (row(norm_mix[layer]), row(norm_out_attn[layer]),
                                row(norm_out_conv[layer]), row(norm_ffn[layer]))
    st0, st1 = state_conv[layer, :, 0, :], state_conv[layer, :, 1, :]

    mix_in = functools.partial(_mix_in, d_attn=d_attn, d_conv=d_conv)
    qp, kp, vp, ocp, up = mix_in(xp, g_mix, w_in_b, conv_w[layer], g_oc, st0, st1, seq_len=seq_len)
    qs, ks, vs, ocs, us = mix_in(xs, g_mix, w_in_b, conv_w[layer], g_oc, st0, st1, seq_len=None)

    attn_p = _attn_prompt(qp, kp, vp, n_seq=n_seq, seq_len=seq_len)
    heads = lambda a: a.reshape(ts, n_heads, dh)
    attn_s = _attn_sample(heads(qs), heads(ks), heads(vs), cache_k[layer], cache_v[layer])
    attn_s = attn_s.reshape(ts, d_attn)

    n_route = N_GROUPS + N_EXPERTS
    w_router = jnp.zeros((d, LANES), F32).at[:, :N_GROUPS].set(w_router_group[layer])
    w_router = w_router.at[:, N_GROUPS:n_route].set(w_router_expert[layer])
    b_router = jnp.zeros((1, LANES), F32).at[0, :N_GROUPS].set(b_router_group[layer])
    b_router = b_router.at[0, N_GROUPS:n_route].set(b_router_expert[layer])
    mix_out = functools.partial(_mix_out, norm_ga=g_oa, w_out_bf16=w_out_b, norm_gf=g_ffn,
                                w_router=w_router, b_router=b_router)
    h_p, tok_p, route_p, cnt_p = mix_out(xp, attn_p, ocp, cnt_in=jnp.zeros((8, LANES), F32))
    h_s, tok_s, route_s, cnt_s = mix_out(xs, attn_s, ocs, cnt_in=cnt_p)

    blk = EXPERT_BLOCK
    n_assign = 2 * (tp + ts)
    n_blocks = -(-n_assign // blk) + N_EXPERTS
    counts = cnt_s[0, ROUTER_LANE0:ROUTER_LANE0 + N_EXPERTS].astype(jnp.int32)
    padded = (counts + blk - 1) // blk * blk
    pend = jnp.cumsum(padded)
    pstart = pend - padded
    block_start = jnp.arange(n_blocks, dtype=jnp.int32) * blk
    block_e = jnp.sum((pend[None, :] <= block_start[:, None]).astype(jnp.int32), axis=1)
    block_e = jnp.minimum(block_e, N_EXPERTS - 1)
    n_used = (pend[-1:] // blk).astype(jnp.int32)

    def dests(route):
        e = route[:, R_E0:R_E1 + 1].astype(jnp.int32)
        rank = route[:, R_RANK0:R_RANK1 + 1].astype(jnp.int32)
        return (jnp.take(pstart, e) + rank).reshape(-1)

    dest_p, dest_s = dests(route_p), dests(route_s)
    xbuf = jnp.zeros((n_blocks * blk, d), F32)
    xbuf = _dispatch(dest_p, tok_p, xbuf)
    xbuf = _dispatch(dest_s, tok_s, xbuf)
    ybuf = _experts(block_e, n_used, xbuf, w_gate[layer], w_up[layer], w_down[layer])
    g_fin = row(norm_final)
    y_p = _combine(dest_p, h_p, route_p, g_fin, ybuf)
    y_s = _combine(dest_s, h_s, route_s, g_fin, ybuf)

    w_keep = min(max(DILATIONS) * WIN_KEYS, seq_len)
    kv5 = lambda a: a.reshape(n_seq, seq_len, n_heads, dh)[None, :, seq_len - w_keep:]
    conv_p = up.reshape(n_seq, seq_len, d_conv)[None, :, seq_len - (CONV_WIDTH - 1):]
    conv_s = jnp.stack([st1, us], axis=1)[None]
    kvs = lambda a: a.reshape(1, ts, 1, n_heads, dh)
    return (y_p.reshape(n_seq, seq_len, d), y_s.reshape(db, ds, d), kv5(kp), kv5(vp), conv_p,
            kvs(ks), kvs(vs), conv_s)
```

```python
import functools

import jax
import jax.numpy as jnp
from jax import lax
from jax.experimental import pallas as pl
from jax.experimental.pallas import tpu as pltpu

HEAD_DIM = 64
WIN_KEYS = 128
DILATIONS = (1, 4, 16)
CONV_WIDTH = 3
N_GROUPS = 4
EXPERTS_PER_GROUP = 8
N_EXPERTS = N_GROUPS * EXPERTS_PER_GROUP
EPS = 1e-6
NEG = -1e30

LANES = 128
ROW_TILE = 512
EXPERT_BLOCK = 256
ATTN_UNROLL = 4
VMEM_LIMIT = 56 * 1024 * 1024

F32 = jnp.float32
BF16 = jnp.bfloat16


def _rms(x, g):
    return x * lax.rsqrt(jnp.mean(x * x, axis=-1, keepdims=True) + EPS) * g


def _mix_in_kernel(*refs, d_attn, d_conv, sequential):
    if sequential:
        (x_ref, g_ref, w_ref, cw_ref, gc_ref,
         q_ref, k_ref, v_ref, kt_ref, vt_ref, oc_ref, st_ref, carry_ref) = refs
    else:
        (x_ref, g_ref, w_ref, cw_ref, gc_ref, st0_ref, st1_ref,
         q_ref, k_ref, v_ref, oc_ref, u_ref) = refs
    x = x_ref[...]
    xb = _rms(x, g_ref[...]).astype(BF16)

    def proj(lo, width):
        return jnp.dot(xb, w_ref[:, lo:lo + width], preferred_element_type=F32)

    q_ref[...] = proj(0, d_attn)
    k = proj(d_attn, d_attn)
    v = proj(2 * d_attn, d_attn)
    k_ref[...] = k
    v_ref[...] = v
    gate = proj(3 * d_attn, d_conv)
    u = proj(3 * d_attn + d_conv, d_conv) * proj(3 * d_attn + 2 * d_conv, d_conv)

    tm = x.shape[0]
    if sequential:
        kt_ref[...] = k.T
        vt_ref[...] = v.T

        @pl.when(pl.program_id(1) == 0)
        def _():
            carry_ref[...] = jnp.zeros_like(carry_ref)

        row = lax.broadcasted_iota(jnp.int32, u.shape, 0)
        prev1 = carry_ref[1:2, :]
        prev2 = carry_ref[0:1, :]
        u1 = jnp.where(row == 0, prev1, pltpu.roll(u, 1, axis=0))
        u2 = jnp.where(row == 0, prev2, jnp.where(row == 1, prev1, pltpu.roll(u, 2, axis=0)))
        carry_ref[0:2, :] = u[tm - 2:tm, :]
        st_ref[...] = u[tm - 2:tm, :]
    else:
        u_ref[...] = u
        u2 = st0_ref[...]
        u1 = st1_ref[...]
    z = u2 * cw_ref[0:1, :] + u1 * cw_ref[1:2, :] + u * cw_ref[2:3, :]
    oc_ref[...] = _rms(gate * z, gc_ref[...])


def _mix_in_call(kernel, grid, in_specs, out_specs, out_shape, scratch, args):
    return pl.pallas_call(
        kernel, grid=grid, in_specs=in_specs, out_specs=out_specs, out_shape=out_shape,
        scratch_shapes=scratch,
        compiler_params=pltpu.CompilerParams(
            dimension_semantics=("arbitrary",) * len(grid), vmem_limit_bytes=VMEM_LIMIT),
        name="mix_in",
    )(*args)


def _mix_in_prompt(x2d, norm_g, w_in_bf16, conv_w, norm_gc, *, seq_len, d_attn, d_conv):
    t, d = x2d.shape
    tm = min(ROW_TILE, seq_len)
    n_seq, per = t // seq_len, seq_len // tm
    const = lambda b, s: (0, 0)
    row = lambda width: pl.BlockSpec((tm, width), lambda b, s: (b * per + s, 0))
    col = pl.BlockSpec((None, d_attn, tm), lambda b, s: (b, 0, s))
    f32 = lambda *shape: jax.ShapeDtypeStruct(shape, F32)
    return _mix_in_call(
        functools.partial(_mix_in_kernel, d_attn=d_attn, d_conv=d_conv, sequential=True),
        (n_seq, per),
        [row(d), pl.BlockSpec((1, d), const), pl.BlockSpec(w_in_bf16.shape, const),
         pl.BlockSpec((CONV_WIDTH, d_conv), const), pl.BlockSpec((1, d_conv), const)],
        [row(d_attn)] * 3 + [col] * 2 + [row(d_conv),
                                         pl.BlockSpec((None, CONV_WIDTH - 1, d_conv),
                                                      lambda b, s: (b, 0, 0))],
        [f32(t, d_attn)] * 3 + [f32(n_seq, d_attn, seq_len)] * 2
        + [f32(t, d_conv), f32(n_seq, CONV_WIDTH - 1, d_conv)],
        [pltpu.VMEM((8, d_conv), F32)],
        (x2d, norm_g, w_in_bf16, conv_w, norm_gc))


def _mix_in_sample(x2d, norm_g, w_in_bf16, conv_w, norm_gc, st0, st1, *, d_attn, d_conv):
    t, d = x2d.shape
    full = lambda arr: pl.BlockSpec(arr.shape, lambda i: (0,) * arr.ndim)
    f32 = lambda *shape: jax.ShapeDtypeStruct(shape, F32)
    args = (x2d, norm_g, w_in_bf16, conv_w, norm_gc, st0, st1)
    outs = [f32(t, d_attn)] * 3 + [f32(t, d_conv)] * 2
    return _mix_in_call(
        functools.partial(_mix_in_kernel, d_attn=d_attn, d_conv=d_conv, sequential=False),
        (1,), [full(a) for a in args], [full(o) for o in outs], outs, [], args)


def _attn_prompt_kernel(q_ref, k_ref, v_ref, o_ref, m_s, l_s, a_s, *, seq_len):
    w = WIN_KEYS
    scale = HEAD_DIM ** -0.5
    r_i = lax.broadcasted_iota(jnp.int32, (2 * w, 2 * w), 0) & (w - 1)
    c_i = lax.broadcasted_iota(jnp.int32, (2 * w, 2 * w), 1)
    mask_cur = (lax.broadcasted_iota(jnp.int32, (2 * w, w), 1)
                <= lax.broadcasted_iota(jnp.int32, (2 * w, w), 0) & (w - 1))
    mask_both = jnp.logical_and(c_i >= r_i, c_i - w <= r_i)
    first_head = lax.broadcasted_iota(jnp.int32, (w, 2 * HEAD_DIM), 1) < HEAD_DIM
    dn_t = (((1,), (1,)), ((), ()))

    def rows(start, dil):
        if dil > 1:
            return pl.ds(start, w, stride=dil)
        return pl.ds(start if isinstance(start, int) else pl.multiple_of(start, w), w)

    def run_branch(dil, first, last):
        span = dil * w
        nb = seq_len // span

        def blocks(its, with_prev):
            mask = mask_both if with_prev else mask_cur
            cur, qs, ks, vs = [], [], [], []
            for it in its:
                g = it % dil
                n = it // dil
                c = rows(g + n * span, dil)
                cur.append(c)
                qb = (q_ref[c, :] * scale).astype(BF16)
                zero = jnp.zeros_like(qb)
                qs.append(jnp.concatenate([jnp.where(first_head, qb, zero),
                                           jnp.where(first_head, zero, qb)], axis=0))
                k = k_ref[c, :].astype(BF16)
                v = v_ref[c, :].astype(BF16)
                if with_prev:
                    p = rows(g + (n - 1) * span, dil)
                    k = jnp.concatenate([k_ref[p, :].astype(BF16), k], axis=0)
                    v = jnp.concatenate([v_ref[p, :].astype(BF16), v], axis=0)
                ks.append(k)
                vs.append(v)
            scores = [lax.dot_general(q, k, dn_t, preferred_element_type=F32)
                      for q, k in zip(qs, ks)]
            ms, ls, ps = [], [], []
            for s in scores:
                s = jnp.where(mask, s, NEG)
                m = jnp.max(s, axis=-1, keepdims=True)
                p = jnp.exp(s - m)
                ms.append(m)
                ls.append(jnp.sum(p, axis=-1, keepdims=True))
                ps.append(p.astype(BF16))
            accs = [jnp.dot(p, v, preferred_element_type=F32) for p, v in zip(ps, vs)]
            for c, m, l, acc in zip(cur, ms, ls, accs):
                m_b = jnp.where(first_head, m[:w], m[w:])
                l_b = jnp.where(first_head, l[:w], l[w:])
                a_b = jnp.where(first_head, acc[:w], acc[w:])
                if not first:
                    m_o = m_s[c, :]
                    m_n = jnp.maximum(m_o, m_b)
                    w_o = jnp.exp(m_o - m_n)
                    w_b = jnp.exp(m_b - m_n)
                    l_b = w_o * l_s[c, :] + w_b * l_b
                    a_b = w_o * a_s[c, :] + w_b * a_b
                    m_b = m_n
                if last:
                    o_ref[c, :] = a_b / l_b
                else:
                    m_s[c, :] = m_b
                    l_s[c, :] = l_b
                    a_s[c, :] = a_b

        def run(lo, hi, with_prev):
            u = ATTN_UNROLL
            trips = (hi - lo) // u

            def body(t, carry):
                blocks([lo + t * u + j for j in range(u)], with_prev)
                return carry

            if trips:
                lax.fori_loop(0, trips, body, 0)
            if lo + trips * u < hi:
                blocks(list(range(lo + trips * u, hi)), with_prev)

        run(0, dil, False)
        run(dil, dil * nb, True)

    order = sorted(DILATIONS, reverse=True)
    for i, dil in enumerate(order):
        run_branch(dil, i == 0, i == len(order) - 1)


def _attn_prompt(q, k, v, *, n_seq, seq_len):
    t, d_attn = q.shape
    pair = 2 * HEAD_DIM
    spec = pl.BlockSpec((seq_len, pair), lambda b, h: (b, h))
    return pl.pallas_call(
        functools.partial(_attn_prompt_kernel, seq_len=seq_len),
        grid=(n_seq, d_attn // pair),
        in_specs=[spec] * 3,
        out_specs=spec,
        out_shape=jax.ShapeDtypeStruct((t, d_attn), F32),
        scratch_shapes=[pltpu.VMEM((seq_len, pair), F32)] * 3,
        compiler_params=pltpu.CompilerParams(
            dimension_semantics=("arbitrary", "arbitrary"), vmem_limit_bytes=VMEM_LIMIT),
        name="attn_prompt",
    )(q, k, v)


def _attn_sample_kernel(q_ref, kn_ref, vn_ref, kt_ref, vt_ref, o_ref):
    w_buf = kt_ref.shape[-1]
    q = q_ref[...] * (HEAD_DIM ** -0.5)
    delta = w_buf - lax.broadcasted_iota(jnp.int32, (1, 1, w_buf), 2)
    cnt = jnp.zeros((1, 1, w_buf), F32)
    for dil in DILATIONS:
        assert dil & (dil - 1) == 0
        member = jnp.where(delta <= dil * WIN_KEYS, 1.0, 0.0)
        cnt = cnt + jnp.where((delta & (dil - 1)) == 0, member, 0.0)
    s = jnp.sum(q * kt_ref[...], axis=1, keepdims=True)
    s = jnp.where(cnt > 0.0, s, NEG)
    s_self = jnp.sum(q * kn_ref[...], axis=1, keepdims=True)
    m = jnp.maximum(jnp.max(s, axis=-1, keepdims=True), s_self)
    p = cnt * jnp.exp(s - m)
    p_self = len(DILATIONS) * jnp.exp(s_self - m)
    l = jnp.sum(p, axis=-1, keepdims=True) + p_self
    acc = jnp.sum(p * vt_ref[...], axis=-1, keepdims=True) + p_self * vn_ref[...]
    o_ref[...] = acc / l


def _attn_sample(q, k_new, v_new, cache_kt, cache_vt):
    db, n_heads, dh, w_buf = cache_kt.shape
    head_spec = pl.BlockSpec((None, n_heads, dh, 1), lambda b: (b, 0, 0, 0))
    cache_spec = pl.BlockSpec((None, n_heads, dh, w_buf), lambda b: (b, 0, 0, 0))
    return pl.pallas_call(
        _attn_sample_kernel,
        grid=(db,),
        in_specs=[head_spec] * 3 + [cache_spec] * 2,
        out_specs=head_spec,
        out_shape=jax.ShapeDtypeStruct((db, n_heads, dh, 1), F32),
        compiler_params=pltpu.CompilerParams(
            dimension_semantics=("arbitrary",), vmem_limit_bytes=VMEM_LIMIT),
        name="attn_sample",
    )(q, k_new, v_new, cache_kt, cache_vt)


R_E0, R_E1, R_G0, R_G1, R_RANK0, R_RANK1 = range(6)
ROUTER_LANE0 = N_GROUPS


def _mix_out_kernel(x_ref, a_ref, oc_ref, ga_ref, wo_ref, gf_ref, wr_ref, br_ref, cnt_in_ref,
                    h_ref, tok_ref, route_ref, cnt_ref):
    d_attn = a_ref.shape[1]
    tm = x_ref.shape[0]
    a = _rms(a_ref[...], ga_ref[...]).astype(BF16)
    mix = jnp.dot(a, wo_ref[0:d_attn, :], preferred_element_type=F32)
    mix = mix + jnp.dot(oc_ref[...].astype(BF16), wo_ref[d_attn:, :], preferred_element_type=F32)
    h = x_ref[...] + mix
    h_ref[...] = h
    tok = _rms(h, gf_ref[...])
    tok_ref[...] = tok

    logits = jnp.dot(tok, wr_ref[...], precision=lax.Precision.HIGHEST,
                     preferred_element_type=F32) + br_ref[...]
    lane = lax.broadcasted_iota(jnp.int32, logits.shape, 1)
    big = jnp.int32(LANES)
    neg_inf = jnp.float32(-jnp.inf)

    def top1(vals):
        best = jnp.max(vals, axis=-1, keepdims=True)
        idx = jnp.min(jnp.where(vals == best, lane, big), axis=-1, keepdims=True)
        return best, idx

    is_group = lane < N_GROUPS
    lg = jnp.where(is_group, logits, neg_inf)
    mg, g_sel = top1(lg)
    p_group = 1.0 / jnp.sum(jnp.where(is_group, jnp.exp(lg - mg), 0.0), axis=-1, keepdims=True)

    lo = ROUTER_LANE0 + g_sel * EXPERTS_PER_GROUP
    in_group = jnp.logical_and(lane >= lo, lane < lo + EXPERTS_PER_GROUP)
    le = jnp.where(in_group, logits, neg_inf)
    v1, i1 = top1(le)
    v2, i2 = top1(jnp.where(lane == i1, neg_inf, le))
    e2 = jnp.exp(v2 - v1)
    gate1 = p_group / (1.0 + e2)
    gate2 = p_group * e2 / (1.0 + e2)

    @pl.when(pl.program_id(0) == 0)
    def _():
        cnt_ref[...] = cnt_in_ref[...]

    oh1 = lane == i1
    oh2 = lane == i2
    both = jnp.where(jnp.logical_or(oh1, oh2), 1.0, 0.0)
    r_i = lax.broadcasted_iota(jnp.int32, (tm, tm), 0)
    c_i = lax.broadcasted_iota(jnp.int32, (tm, tm), 1)
    strict_lower = jnp.where(c_i < r_i, 1.0, 0.0).astype(BF16)
    before = jnp.dot(strict_lower, both.astype(BF16), preferred_element_type=F32) + cnt_ref[0:1, :]
    rank1 = jnp.sum(jnp.where(oh1, before, 0.0), axis=-1, keepdims=True)
    rank2 = jnp.sum(jnp.where(oh2, before, 0.0), axis=-1, keepdims=True)
    cnt_ref[0:1, :] = cnt_ref[0:1, :] + jnp.sum(both, axis=0, keepdims=True)

    rec = jnp.zeros(logits.shape, F32)
    for col, val in ((R_E0, (i1 - ROUTER_LANE0).astype(F32)), (R_E1, (i2 - ROUTER_LANE0).astype(F32)),
                     (R_G0, gate1), (R_G1, gate2), (R_RANK0, rank1), (R_RANK1, rank2)):
        rec = jnp.where(lane == col, val, rec)
    route_ref[...] = rec


def _mix_out(x2d, attn, oconv, norm_ga, w_out_bf16, norm_gf, w_router, b_router, cnt_in):
    t, d = x2d.shape
    d_attn, d_conv = attn.shape[1], oconv.shape[1]
    tm = min(ROW_TILE, t)
    row = lambda width: pl.BlockSpec((tm, width), lambda i: (i, 0))
    full = lambda arr: pl.BlockSpec(arr.shape, lambda i: (0, 0))
    return pl.pallas_call(
        _mix_out_kernel,
        grid=(t // tm,),
        in_specs=[row(d), row(d_attn), row(d_conv), full(norm_ga), full(w_out_bf16), full(norm_gf),
                  full(w_router), full(b_router), full(cnt_in)],
        out_specs=[row(d), row(d), row(LANES), pl.BlockSpec((8, LANES), lambda i: (0, 0))],
        out_shape=[jax.ShapeDtypeStruct((t, d), F32), jax.ShapeDtypeStruct((t, d), F32),
                   jax.ShapeDtypeStruct((t, LANES), F32), jax.ShapeDtypeStruct((8, LANES), F32)],
        compiler_params=pltpu.CompilerParams(
            dimension_semantics=("arbitrary",), vmem_limit_bytes=VMEM_LIMIT),
        name="mix_out",
    )(x2d, attn, oconv, norm_ga, w_out_bf16, norm_gf, w_router, b_router, cnt_in)


def _row_copy(src, src_row, dst, dst_row, sem):
    return pltpu.make_async_copy(src.at[pl.ds(src_row, 1)], dst.at[pl.ds(dst_row, 1)], sem)


def _dispatch_kernel(dest_ref, tok_ref, xbuf_in_ref, xbuf_ref, sem):
    del xbuf_in_ref
    tm = tok_ref.shape[0]
    base = pl.program_id(0) * tm

    def issue(r, carry):
        for j in range(2):
            _row_copy(tok_ref, r, xbuf_ref, dest_ref[2 * (base + r) + j], sem).start()
        return carry

    lax.fori_loop(0, tm, issue, 0)

    def drain(r, carry):
        for j in range(2):
            _row_copy(tok_ref, r, xbuf_ref, dest_ref[2 * (base + r) + j], sem).wait()
        return carry

    lax.fori_loop(0, tm, drain, 0)


def _dispatch(dest, tok, xbuf):
    t, d = tok.shape
    tm = min(ROW_TILE, t)
    return pl.pallas_call(
        _dispatch_kernel,
        grid_spec=pltpu.PrefetchScalarGridSpec(
            num_scalar_prefetch=1,
            grid=(t // tm,),
            in_specs=[pl.BlockSpec((tm, d), lambda i, dest: (i, 0)),
                      pl.BlockSpec(memory_space=pl.ANY)],
            out_specs=pl.BlockSpec(memory_space=pl.ANY),
            scratch_shapes=[pltpu.SemaphoreType.DMA(())],
        ),
        out_shape=jax.ShapeDtypeStruct(xbuf.shape, xbuf.dtype),
        input_output_aliases={2: 0},
        compiler_params=pltpu.CompilerParams(
            dimension_semantics=("arbitrary",), vmem_limit_bytes=VMEM_LIMIT),
        name="dispatch",
    )(dest, tok, xbuf)


def _experts_kernel(block_e_ref, n_used_ref, x_ref, wg_ref, wu_ref, wd_ref, y_ref):
    del block_e_ref
    b = pl.program_id(0)

    @pl.when(b < n_used_ref[0])
    def _():
        x = x_ref[...].astype(BF16)
        gate = jnp.dot(x, wg_ref[...].astype(BF16), preferred_element_type=F32)
        up = jnp.dot(x, wu_ref[...].astype(BF16), preferred_element_type=F32)
        hid = gate * (1.0 / (1.0 + jnp.exp(-gate))) * up
        y_ref[...] = jnp.dot(hid.astype(BF16), wd_ref[...].astype(BF16), preferred_element_type=F32)

    @pl.when(b >= n_used_ref[0])
    def _():
        y_ref[...] = jnp.zeros_like(y_ref)


def _experts(block_e, n_used, xbuf, w_gate, w_up, w_down):
    rows, d = xbuf.shape
    _, _, d_exp = w_gate.shape
    blk = EXPERT_BLOCK
    return pl.pallas_call(
        _experts_kernel,
        grid_spec=pltpu.PrefetchScalarGridSpec(
            num_scalar_prefetch=2,
            grid=(rows // blk,),
            in_specs=[pl.BlockSpec((blk, d), lambda b, be, nu: (b, 0)),
                      pl.BlockSpec((None, d, d_exp), lambda b, be, nu: (be[b], 0, 0)),
                      pl.BlockSpec((None, d, d_exp), lambda b, be, nu: (be[b], 0, 0)),
                      pl.BlockSpec((None, d_exp, d), lambda b, be, nu: (be[b], 0, 0))],
            out_specs=pl.BlockSpec((blk, d), lambda b, be, nu: (b, 0)),
        ),
        out_shape=jax.ShapeDtypeStruct((rows, d), F32),
        compiler_params=pltpu.CompilerParams(
            dimension_semantics=("arbitrary",), vmem_limit_bytes=VMEM_LIMIT),
        name="experts",
    )(block_e, n_used, xbuf, w_gate, w_up, w_down)


def _combine_kernel(dest_ref, h_ref, route_ref, gn_ref, ybuf_ref, o_ref, y0_ref, y1_ref, sem):
    tm = h_ref.shape[0]
    base = pl.program_id(0) * tm
    bufs = (y0_ref, y1_ref)

    def issue(r, carry):
        for j in range(2):
            _row_copy(ybuf_ref, dest_ref[2 * (base + r) + j], bufs[j], r, sem).start()
        return carry

    lax.fori_loop(0, tm, issue, 0)

    def drain(r, carry):
        for j in range(2):
            _row_copy(ybuf_ref, dest_ref[2 * (base + r) + j], bufs[j], r, sem).wait()
        return carry

    lax.fori_loop(0, tm, drain, 0)

    route = route_ref[...]
    g0 = route[:, R_G0:R_G0 + 1]
    g1 = route[:, R_G1:R_G1 + 1]
    f = g0 * y0_ref[...] + g1 * y1_ref[...]
    o_ref[...] = _rms(h_ref[...] + f, gn_ref[...])


def _combine(dest, h, route, norm_g, ybuf):
    t, d = h.shape
    tm = min(ROW_TILE, t)
    return pl.pallas_call(
        _combine_kernel,
        grid_spec=pltpu.PrefetchScalarGridSpec(
            num_scalar_prefetch=1,
            grid=(t // tm,),
            in_specs=[pl.BlockSpec((tm, d), lambda i, dest: (i, 0)),
                      pl.BlockSpec((tm, LANES), lambda i, dest: (i, 0)),
                      pl.BlockSpec((1, d), lambda i, dest: (0, 0)),
                      pl.BlockSpec(memory_space=pl.ANY)],
            out_specs=pl.BlockSpec((tm, d), lambda i, dest: (i, 0)),
            scratch_shapes=[pltpu.VMEM((tm, d), F32), pltpu.VMEM((tm, d), F32),
                            pltpu.SemaphoreType.DMA(())],
        ),
        out_shape=jax.ShapeDtypeStruct((t, d), F32),
        compiler_params=pltpu.CompilerParams(
            dimension_semantics=("arbitrary",), vmem_limit_bytes=VMEM_LIMIT),
        name="combine",
    )(dest, h, route, norm_g, ybuf)


def kernel(x_prompt, x_sample, cache_k, cache_v, state_conv, norm_mix, w_in, conv_w, norm_out_attn,
           norm_out_conv, w_out, norm_ffn, w_router_group, b_router_group, w_router_expert,
           b_router_expert, w_gate, w_up, w_down, norm_final):
    n_seq, seq_len, d = x_prompt.shape
    db, ds, _ = x_sample.shape
    depth = w_in.shape[0]
    _, _, w_buf, n_heads, dh = cache_k.shape
    d_attn = n_heads * dh
    d_conv = d - d_attn
    assert depth == 1 and ds == 1 and dh == HEAD_DIM
    assert seq_len % (max(DILATIONS) * WIN_KEYS) == 0 and seq_len <= max(DILATIONS) * WIN_KEYS
    layer = 0
    tp, ts = n_seq * seq_len, db

    xp = x_prompt.reshape(tp, d)
    xs = x_sample.reshape(ts, d)
    row = lambda vec: vec.reshape(1, -1)
    w_in_b = w_in[layer].astype(BF16)
    w_out_b = w_out[layer].astype(BF16)
    g_mix, g_oa, g_oc, g_ffn = (row(norm_mix[layer]), row(norm_out_attn[layer]),
                                row(norm_out_conv[layer]), row(norm_ffn[layer]))
    st0, st1 = state_conv[layer, :, 0, :], state_conv[layer, :, 1, :]

    qp, kp, vp, kp_t, vp_t, ocp, conv_p = _mix_in_prompt(
        xp, g_mix, w_in_b, conv_w[layer], g_oc, seq_len=seq_len, d_attn=d_attn, d_conv=d_conv)
    qs, ks, vs, ocs, us = _mix_in_sample(
        xs, g_mix, w_in_b, conv_w[layer], g_oc, st0, st1, d_attn=d_attn, d_conv=d_conv)

    attn_p = _attn_prompt(qp, kp, vp, n_seq=n_seq, seq_len=seq_len)
    heads = lambda a: a.reshape(ts, n_heads, dh, 1)
    positions_last = lambda c: jnp.transpose(c, (0, 2, 3, 1))
    attn_s = _attn_sample(heads(qs), heads(ks), heads(vs),
                          positions_last(cache_k[layer]), positions_last(cache_v[layer]))
    attn_s = attn_s.reshape(ts, d_attn)

    n_route = N_GROUPS + N_EXPERTS
    w_router = jnp.zeros((d, LANES), F32).at[:, :N_GROUPS].set(w_router_group[layer])
    w_router = w_router.at[:, N_GROUPS:n_route].set(w_router_expert[layer])
    b_router = jnp.zeros((1, LANES), F32).at[0, :N_GROUPS].set(b_router_group[layer])
    b_router = b_router.at[0, N_GROUPS:n_route].set(b_router_expert[layer])
    mix_out = functools.partial(_mix_out, norm_ga=g_oa, w_out_bf16=w_out_b, norm_gf=g_ffn,
                                w_router=w_router, b_router=b_router)
    h_p, tok_p, route_p, cnt_p = mix_out(xp, attn_p, ocp, cnt_in=jnp.zeros((8, LANES), F32))
    h_s, tok_s, route_s, cnt_s = mix_out(xs, attn_s, ocs, cnt_in=cnt_p)

    blk = EXPERT_BLOCK
    n_assign = 2 * (tp + ts)
    n_blocks = -(-n_assign // blk) + N_EXPERTS
    counts = cnt_s[0, ROUTER_LANE0:ROUTER_LANE0 + N_EXPERTS].astype(jnp.int32)
    padded = (counts + blk - 1) // blk * blk
    pend = jnp.cumsum(padded)
    pstart = pend - padded
    block_start = jnp.arange(n_blocks, dtype=jnp.int32) * blk
    block_e = jnp.sum((pend[None, :] <= block_start[:, None]).astype(jnp.int32), axis=1)
    block_e = jnp.minimum(block_e, N_EXPERTS - 1)
    n_used = (pend[-1:] // blk).astype(jnp.int32)

    def dests(route):
        e = route[:, R_E0:R_E1 + 1].astype(jnp.int32)
        rank = route[:, R_RANK0:R_RANK1 + 1].astype(jnp.int32)
        return (jnp.take(pstart, e) + rank).reshape(-1)

    dest_p, dest_s = dests(route_p), dests(route_s)
    xbuf = jnp.zeros((n_blocks * blk, d), F32)
    xbuf = _dispatch(dest_p, tok_p, xbuf)
    xbuf = _dispatch(dest_s, tok_s, xbuf)
    ybuf = _experts(block_e, n_used, xbuf, w_gate[layer], w_up[layer], w_down[layer])
    g_fin = row(norm_final)
    y_p = _combine(dest_p, h_p, route_p, g_fin, ybuf)
    y_s = _combine(dest_s, h_s, route_s, g_fin, ybuf)

    w_keep = min(max(DILATIONS) * WIN_KEYS, seq_len)
    kv5 = lambda a_t: jnp.transpose(a_t.reshape(n_seq, n_heads, dh, seq_len),
                                    (0, 3, 1, 2))[None, :, seq_len - w_keep:]
    conv_s = jnp.stack([st1, us], axis=1)[None]
    kvs = lambda a: a.reshape(1, ts, 1, n_heads, dh)
    return (y_p.reshape(n_seq, seq_len, d), y_s.reshape(db, ds, d), kv5(kp_t), kv5(vp_t),
            conv_p[None], kvs(ks), kvs(vs), conv_s)
```

```python
import functools

import jax
import jax.numpy as jnp
from jax import lax
from jax.experimental import pallas as pl
from jax.experimental.pallas import tpu as pltpu

HEAD_DIM = 64
WIN_KEYS = 128
DILATIONS = (1, 4, 16)
CONV_WIDTH = 3
N_GROUPS = 4
EXPERTS_PER_GROUP = 8
N_EXPERTS = N_GROUPS * EXPERTS_PER_GROUP
EPS = 1e-6
NEG = -1e30

LANES = 128
ROW_TILE = 512
EXPERT_BLOCK = 256
ATTN_UNROLL = 4
VMEM_LIMIT = 56 * 1024 * 1024

F32 = jnp.float32
BF16 = jnp.bfloat16


def _rms(x, g):
    return x * lax.rsqrt(jnp.mean(x * x, axis=-1, keepdims=True) + EPS) * g


def _mix_in_kernel(*refs, d_attn, d_conv, sequential):
    if sequential:
        (x_ref, g_ref, w_ref, cw_ref, gc_ref,
         q_ref, k_ref, v_ref, kt_ref, vt_ref, oc_ref, st_ref, carry_ref) = refs
    else:
        (x_ref, g_ref, w_ref, cw_ref, gc_ref, st0_ref, st1_ref,
         q_ref, k_ref, v_ref, oc_ref, u_ref) = refs
    x = x_ref[...]
    xb = _rms(x, g_ref[...]).astype(BF16)

    def proj(lo, width):
        return jnp.dot(xb, w_ref[:, lo:lo + width], preferred_element_type=F32)

    q_ref[...] = proj(0, d_attn)
    k = proj(d_attn, d_attn)
    v = proj(2 * d_attn, d_attn)
    k_ref[...] = k
    v_ref[...] = v
    gate = proj(3 * d_attn, d_conv)
    u = proj(3 * d_attn + d_conv, d_conv) * proj(3 * d_attn + 2 * d_conv, d_conv)

    tm = x.shape[0]
    if sequential:
        kt_ref[...] = k.T
        vt_ref[...] = v.T

        @pl.when(pl.program_id(1) == 0)
        def _():
            carry_ref[...] = jnp.zeros_like(carry_ref)

        row = lax.broadcasted_iota(jnp.int32, u.shape, 0)
        prev1 = carry_ref[1:2, :]
        prev2 = carry_ref[0:1, :]
        u1 = jnp.where(row == 0, prev1, pltpu.roll(u, 1, axis=0))
        u2 = jnp.where(row == 0, prev2, jnp.where(row == 1, prev1, pltpu.roll(u, 2, axis=0)))
        carry_ref[0:2, :] = u[tm - 2:tm, :]
        st_ref[...] = u[tm - 2:tm, :]
    else:
        u_ref[...] = u
        u2 = st0_ref[...]
        u1 = st1_ref[...]
    z = u2 * cw_ref[0:1, :] + u1 * cw_ref[1:2, :] + u * cw_ref[2:3, :]
    oc_ref[...] = _rms(gate * z, gc_ref[...])


def _mix_in_call(kernel, grid, in_specs, out_specs, out_shape, scratch, args):
    return pl.pallas_call(
        kernel, grid=grid, in_specs=in_specs, out_specs=out_specs, out_shape=out_shape,
        scratch_shapes=scratch,
        compiler_params=pltpu.CompilerParams(
            dimension_semantics=("arbitrary",) * len(grid), vmem_limit_bytes=VMEM_LIMIT),
        name="mix_in",
    )(*args)


def _mix_in_prompt(x2d, norm_g, w_in_bf16, conv_w, norm_gc, *, seq_len, d_attn, d_conv):
    t, d = x2d.shape
    tm = min(ROW_TILE, seq_len)
    n_seq, per = t // seq_len, seq_len // tm
    const = lambda b, s: (0, 0)
    row = lambda width: pl.BlockSpec((tm, width), lambda b, s: (b * per + s, 0))
    col = pl.BlockSpec((None, d_attn, tm), lambda b, s: (b, 0, s))
    f32 = lambda *shape: jax.ShapeDtypeStruct(shape, F32)
    return _mix_in_call(
        functools.partial(_mix_in_kernel, d_attn=d_attn, d_conv=d_conv, sequential=True),
        (n_seq, per),
        [row(d), pl.BlockSpec((1, d), const), pl.BlockSpec(w_in_bf16.shape, const),
         pl.BlockSpec((CONV_WIDTH, d_conv), const), pl.BlockSpec((1, d_conv), const)],
        [row(d_attn)] * 3 + [col] * 2 + [row(d_conv),
                                         pl.BlockSpec((None, CONV_WIDTH - 1, d_conv),
                                                      lambda b, s: (b, 0, 0))],
        [f32(t, d_attn)] * 3 + [f32(n_seq, d_attn, seq_len)] * 2
        + [f32(t, d_conv), f32(n_seq, CONV_WIDTH - 1, d_conv)],
        [pltpu.VMEM((8, d_conv), F32)],
        (x2d, norm_g, w_in_bf16, conv_w, norm_gc))


def _mix_in_sample(x2d, norm_g, w_in_bf16, conv_w, norm_gc, st0, st1, *, d_attn, d_conv):
    t, d = x2d.shape
    full = lambda arr: pl.BlockSpec(arr.shape, lambda i: (0,) * arr.ndim)
    f32 = lambda *shape: jax.ShapeDtypeStruct(shape, F32)
    args = (x2d, norm_g, w_in_bf16, conv_w, norm_gc, st0, st1)
    outs = [f32(t, d_attn)] * 3 + [f32(t, d_conv)] * 2
    return _mix_in_call(
        functools.partial(_mix_in_kernel, d_attn=d_attn, d_conv=d_conv, sequential=False),
        (1,), [full(a) for a in args], [full(o) for o in outs], outs, [], args)


def _attn_prompt_kernel(q_ref, k_ref, v_ref, o_ref, m_s, l_s, a_s, *, seq_len):
    w = WIN_KEYS
    scale = HEAD_DIM ** -0.5
    r_i = lax.broadcasted_iota(jnp.int32, (2 * w, 2 * w), 0) & (w - 1)
    c_i = lax.broadcasted_iota(jnp.int32, (2 * w, 2 * w), 1)
    mask_cur = (lax.broadcasted_iota(jnp.int32, (2 * w, w), 1)
                <= lax.broadcasted_iota(jnp.int32, (2 * w, w), 0) & (w - 1))
    mask_both = jnp.logical_and(c_i >= r_i, c_i - w <= r_i)
    first_head = lax.broadcasted_iota(jnp.int32, (w, 2 * HEAD_DIM), 1) < HEAD_DIM
    dn_t = (((1,), (1,)), ((), ()))

    def rows(start, dil):
        if dil > 1:
            return pl.ds(start, w, stride=dil)
        return pl.ds(start if isinstance(start, int) else pl.multiple_of(start, w), w)

    def run_branch(dil, first, last):
        span = dil * w
        nb = seq_len // span

        def blocks(its, with_prev):
            mask = mask_both if with_prev else mask_cur
            cur, qs, ks, vs = [], [], [], []
            for it in its:
                g = it % dil
                n = it // dil
                c = rows(g + n * span, dil)
                cur.append(c)
                qb = (q_ref[c, :] * scale).astype(BF16)
                zero = jnp.zeros_like(qb)
                qs.append(jnp.concatenate([jnp.where(first_head, qb, zero),
                                           jnp.where(first_head, zero, qb)], axis=0))
                k = k_ref[c, :].astype(BF16)
                v = v_ref[c, :].astype(BF16)
                if with_prev:
                    p = rows(g + (n - 1) * span, dil)
                    k = jnp.concatenate([k_ref[p, :].astype(BF16), k], axis=0)
                    v = jnp.concatenate([v_ref[p, :].astype(BF16), v], axis=0)
                ks.append(k)
                vs.append(v)
            scores = [lax.dot_general(q, k, dn_t, preferred_element_type=F32)
                      for q, k in zip(qs, ks)]
            ms, ls, ps = [], [], []
            for s in scores:
                s = jnp.where(mask, s, NEG)
                m = jnp.max(s, axis=-1, keepdims=True)
                p = jnp.exp(s - m)
                ms.append(m)
                ls.append(jnp.sum(p, axis=-1, keepdims=True))
                ps.append(p.astype(BF16))
            accs = [jnp.dot(p, v, preferred_element_type=F32) for p, v in zip(ps, vs)]
            for c, m, l, acc in zip(cur, ms, ls, accs):
                m_b = jnp.where(first_head, m[:w], m[w:])
                l_b = jnp.where(first_head, l[:w], l[w:])
                a_b = jnp.where(first_head, acc[:w], acc[w:])
                if not first:
                    m_o = m_s[c, :]
                    m_n = jnp.maximum(m_o, m_b)
                    w_o = jnp.exp(m_o - m_n)
                    w_b = jnp.exp(m_b - m_n)
                    l_b = w_o * l_s[c, :] + w_b * l_b
                    a_b = w_o * a_s[c, :] + w_b * a_b
                    m_b = m_n
                if last:
                    o_ref[c, :] = a_b / l_b
                else:
                    m_s[c, :] = m_b
                    l_s[c, :] = l_b
                    a_s[c, :] = a_b

        def run(lo, hi, with_prev):
            u = ATTN_UNROLL
            trips = (hi - lo) // u

            def body(t, carry):
                blocks([lo + t * u + j for j in range(u)], with_prev)
                return carry

            if trips:
                lax.fori_loop(0, trips, body, 0)
            if lo + trips * u < hi:
                blocks(list(range(lo + trips * u, hi)), with_prev)

        run(0, dil, False)
        run(dil, dil * nb, True)

    order = sorted(DILATIONS, reverse=True)
    for i, dil in enumerate(order):
        run_branch(dil, i == 0, i == len(order) - 1)


def _attn_prompt(q, k, v, *, n_seq, seq_len):
    t, d_attn = q.shape
    pair = 2 * HEAD_DIM
    spec = pl.BlockSpec((seq_len, pair), lambda b, h: (b, h))
    return pl.pallas_call(
        functools.partial(_attn_prompt_kernel, seq_len=seq_len),
        grid=(n_seq, d_attn // pair),
        in_specs=[spec] * 3,
        out_specs=spec,
        out_shape=jax.ShapeDtypeStruct((t, d_attn), F32),
        scratch_shapes=[pltpu.VMEM((seq_len, pair), F32)] * 3,
        compiler_params=pltpu.CompilerParams(
            dimension_semantics=("arbitrary", "arbitrary"), vmem_limit_bytes=VMEM_LIMIT),
        name="attn_prompt",
    )(q, k, v)


def _attn_sample_kernel(q_ref, kn_ref, vn_ref, kt_ref, vt_ref, o_ref):
    n_heads, dh, w_buf = kt_ref.shape
    delta = w_buf - lax.broadcasted_iota(jnp.int32, (1, w_buf), 1)
    cnt = jnp.zeros((1, w_buf), F32)
    for dil in DILATIONS:
        assert dil & (dil - 1) == 0
        member = jnp.where(delta <= dil * WIN_KEYS, 1.0, 0.0)
        cnt = cnt + jnp.where((delta & (dil - 1)) == 0, member, 0.0)
    eye = (lax.broadcasted_iota(jnp.int32, (dh, dh), 0)
           == lax.broadcasted_iota(jnp.int32, (dh, dh), 1))
    to_col = lambda r: jnp.sum(jnp.where(eye, r, 0.0), axis=1, keepdims=True)
    to_row = lambda c: jnp.sum(jnp.where(eye, c, 0.0), axis=0, keepdims=True)
    outs = []
    for h in range(n_heads):
        sl = slice(h * dh, (h + 1) * dh)
        q = q_ref[:, sl] * (HEAD_DIM ** -0.5)
        s_self = jnp.sum(q * kn_ref[:, sl], axis=1, keepdims=True)
        s = jnp.sum(to_col(q) * kt_ref[h], axis=0, keepdims=True)
        s = jnp.where(cnt > 0.0, s, NEG)
        m = jnp.maximum(jnp.max(s, axis=1, keepdims=True), s_self)
        p = cnt * jnp.exp(s - m)
        p_self = len(DILATIONS) * jnp.exp(s_self - m)
        l = jnp.sum(p, axis=1, keepdims=True) + p_self
        acc = jnp.sum(p * vt_ref[h], axis=1, keepdims=True)
        outs.append((to_row(acc) + p_self * vn_ref[:, sl]) / l)
    o_ref[...] = jnp.concatenate(outs, axis=1)


def _attn_sample(q, k_new, v_new, cache_kt, cache_vt):
    db, n_heads, dh, w_buf = cache_kt.shape
    head_spec = pl.BlockSpec((None, 1, n_heads * dh), lambda b: (b, 0, 0))
    cache_spec = pl.BlockSpec((None, n_heads, dh, w_buf), lambda b: (b, 0, 0, 0))
    return pl.pallas_call(
        _attn_sample_kernel,
        grid=(db,),
        in_specs=[head_spec] * 3 + [cache_spec] * 2,
        out_specs=head_spec,
        out_shape=jax.ShapeDtypeStruct((db, 1, n_heads * dh), F32),
        compiler_params=pltpu.CompilerParams(
            dimension_semantics=("arbitrary",), vmem_limit_bytes=VMEM_LIMIT),
        name="attn_sample",
    )(q, k_new, v_new, cache_kt, cache_vt)


R_E0, R_E1, R_G0, R_G1, R_POS0, R_POS1 = range(6)
ROUTER_LANE0 = N_GROUPS
CHUNK = 8
CHUNKS_PER_BLOCK = EXPERT_BLOCK // CHUNK


def _max_tile_chunks(tm):
    return (2 * tm + (CHUNK - 1) * N_EXPERTS) // CHUNK


def _local_rows(tm):
    return 2 * tm + N_EXPERTS * CHUNK


def _mix_out_kernel(x_ref, a_ref, oc_ref, ga_ref, wo_ref, gf_ref, wr_ref, br_ref,
                    h_ref, route_ref, xs_ref, cnt_ref):
    d_attn = a_ref.shape[1]
    tm, d = x_ref.shape
    a = _rms(a_ref[...], ga_ref[...]).astype(BF16)
    mix = jnp.dot(a, wo_ref[0:d_attn, :], preferred_element_type=F32)
    mix = mix + jnp.dot(oc_ref[...].astype(BF16), wo_ref[d_attn:, :], preferred_element_type=F32)
    h = x_ref[...] + mix
    h_ref[...] = h
    tok = _rms(h, gf_ref[...])

    logits = jnp.dot(tok, wr_ref[...], precision=lax.Precision.HIGHEST,
                     preferred_element_type=F32) + br_ref[...]
    lane = lax.broadcasted_iota(jnp.int32, logits.shape, 1)
    big = jnp.int32(LANES)
    neg_inf = jnp.float32(-jnp.inf)

    def top1(vals):
        best = jnp.max(vals, axis=-1, keepdims=True)
        idx = jnp.min(jnp.where(vals == best, lane, big), axis=-1, keepdims=True)
        return best, idx

    is_group = lane < N_GROUPS
    lg = jnp.where(is_group, logits, neg_inf)
    mg, g_sel = top1(lg)
    p_group = 1.0 / jnp.sum(jnp.where(is_group, jnp.exp(lg - mg), 0.0), axis=-1, keepdims=True)

    lo = ROUTER_LANE0 + g_sel * EXPERTS_PER_GROUP
    in_group = jnp.logical_and(lane >= lo, lane < lo + EXPERTS_PER_GROUP)
    le = jnp.where(in_group, logits, neg_inf)
    v1, i1 = top1(le)
    v2, i2 = top1(jnp.where(lane == i1, neg_inf, le))
    e2 = jnp.exp(v2 - v1)
    gate1 = p_group / (1.0 + e2)
    gate2 = p_group * e2 / (1.0 + e2)

    oh1 = lane == i1
    oh2 = lane == i2
    both = jnp.where(jnp.logical_or(oh1, oh2), 1.0, 0.0)
    r_i = lax.broadcasted_iota(jnp.int32, (tm, tm), 0)
    c_i = lax.broadcasted_iota(jnp.int32, (tm, tm), 1)
    strict_lower = jnp.where(c_i < r_i, 1.0, 0.0).astype(BF16)
    before = jnp.dot(strict_lower, both.astype(BF16), preferred_element_type=F32)
    chunks = jnp.floor((jnp.sum(both, axis=0, keepdims=True) + (CHUNK - 1)) * (1.0 / CHUNK))
    u_r = lax.broadcasted_iota(jnp.int32, (LANES, LANES), 0)
    u_c = lax.broadcasted_iota(jnp.int32, (LANES, LANES), 1)
    strict_upper = jnp.where(u_r < u_c, 1.0, 0.0).astype(BF16)
    chunks8 = jnp.broadcast_to(chunks, (8, LANES))
    first_row = CHUNK * jnp.dot(chunks8.astype(BF16), strict_upper,
                                preferred_element_type=F32)[0:1, :]
    pos = first_row + before
    pos1 = jnp.sum(jnp.where(oh1, pos, 0.0), axis=-1, keepdims=True)
    pos2 = jnp.sum(jnp.where(oh2, pos, 0.0), axis=-1, keepdims=True)
    cnt_ref[...] = jnp.where(lax.broadcasted_iota(jnp.int32, (8, LANES), 0) == 0, chunks8, 0.0)

    rec = jnp.zeros(logits.shape, F32)
    for col, val in ((R_E0, (i1 - ROUTER_LANE0).astype(F32)), (R_E1, (i2 - ROUTER_LANE0).astype(F32)),
                     (R_G0, gate1), (R_G1, gate2), (R_POS0, pos1), (R_POS1, pos2)):
        rec = jnp.where(lane == col, val, rec)
    route_ref[...] = rec

    rec_t = rec.T
    l1 = rec_t[R_POS0:R_POS0 + 1, :].astype(jnp.int32)
    l2 = rec_t[R_POS1:R_POS1 + 1, :].astype(jnp.int32)
    srow = lax.broadcasted_iota(jnp.int32, (xs_ref.shape[0], tm), 0)
    perm = jnp.where(srow == l1, 1.0, jnp.where(srow == l2, 1.0, 0.0)).astype(BF16)
    xs_ref[...] = jnp.dot(perm, tok.astype(BF16), preferred_element_type=F32)


def _mix_out(x2d, attn, oconv, norm_ga, w_out_bf16, norm_gf, w_router, b_router):
    t, d = x2d.shape
    d_attn, d_conv = attn.shape[1], oconv.shape[1]
    tm = min(ROW_TILE, t)
    nt = t // tm
    r_l = _local_rows(tm)
    row = lambda width: pl.BlockSpec((tm, width), lambda i: (i, 0))
    full = lambda arr: pl.BlockSpec(arr.shape, lambda i: (0, 0))
    return pl.pallas_call(
        _mix_out_kernel,
        grid=(nt,),
        in_specs=[row(d), row(d_attn), row(d_conv), full(norm_ga), full(w_out_bf16), full(norm_gf),
                  full(w_router), full(b_router)],
        out_specs=[row(d), row(LANES), pl.BlockSpec((r_l, d), lambda i: (i, 0)),
                   pl.BlockSpec((None, 8, LANES), lambda i: (i, 0, 0))],
        out_shape=[jax.ShapeDtypeStruct((t, d), F32), jax.ShapeDtypeStruct((t, LANES), F32),
                   jax.ShapeDtypeStruct((nt * r_l, d), F32),
                   jax.ShapeDtypeStruct((nt, 8, LANES), F32)],
        compiler_params=pltpu.CompilerParams(
            dimension_semantics=("arbitrary",), vmem_limit_bytes=VMEM_LIMIT),
        name="mix_out",
    )(x2d, attn, oconv, norm_ga, w_out_bf16, norm_gf, w_router, b_router)


def _sorted_layout(tile_chunks, tile_row0, max_local, n_blocks):
    nt, n_exp = tile_chunks.shape
    cpb = CHUNKS_PER_BLOCK
    i32 = jnp.int32
    seg = jnp.sum(tile_chunks, axis=0)
    padded = (seg + cpb - 1) // cpb * cpb
    pend = jnp.cumsum(padded)
    pstart = pend - padded
    tile_incl = jnp.cumsum(tile_chunks, axis=0)
    tile_excl = tile_incl - tile_chunks
    local_incl = jnp.cumsum(tile_chunks, axis=1)
    local_excl = local_incl - tile_chunks
    base = pstart[None, :] + tile_excl

    block_first = jnp.arange(n_blocks, dtype=i32) * cpb
    block_e = jnp.minimum(jnp.sum((pend[None, :] <= block_first[:, None]).astype(i32), axis=1),
                          n_exp - 1)
    n_used = (pend[-1:] // cpb).astype(i32)

    onehot_pick = lambda onehot, table: jnp.sum(jnp.where(onehot, table, 0), axis=-1)

    is_e = block_e[:, None] == jnp.arange(n_exp, dtype=i32)[None, :]
    of_expert = lambda table_te: onehot_pick(is_e[:, None, :], table_te[None, :, :])
    incl_b, cnt_b, lexcl_b = of_expert(tile_incl), of_expert(tile_chunks), of_expert(local_excl)
    q = (block_first - onehot_pick(is_e, pstart[None, :]))[:, None] + jnp.arange(cpb, dtype=i32)
    tile_q = jnp.minimum(jnp.sum((incl_b[:, None, :] <= q[:, :, None]).astype(i32), axis=2), nt - 1)
    is_t = tile_q[:, :, None] == jnp.arange(nt, dtype=i32)[None, None, :]
    of_tile = lambda table_bt: onehot_pick(is_t, table_bt[:, None, :])
    local_chunk = of_tile(lexcl_b) + q - of_tile(incl_b - cnt_b)
    in_run = jnp.logical_and(q >= 0, q < onehot_pick(is_e, seg[None, :])[:, None])
    src_row = jnp.where(in_run, of_tile(tile_row0[None, :]) + CHUNK * local_chunk, 0)
    src_row = src_row.reshape(-1).astype(i32)

    c = jnp.arange(max_local, dtype=i32)
    e_c = jnp.minimum(jnp.sum((local_incl[:, None, :] <= c[None, :, None]).astype(i32), axis=2),
                      n_exp - 1)
    is_ec = e_c[:, :, None] == jnp.arange(n_exp, dtype=i32)[None, None, :]
    of_run = lambda table_te: onehot_pick(is_ec, table_te[:, None, :])
    global_chunk = of_run(base) + c[None, :] - of_run(local_excl)
    tile_src = jnp.where(c[None, :] < local_incl[:, -1:], CHUNK * global_chunk, 0).astype(i32)
    return block_e.astype(i32), n_used, src_row, tile_src


def _experts_kernel(block_e_ref, n_used_ref, src_ref, xs_p_ref, xs_s_ref, wg_ref, wu_ref, wd_ref,
                    y_ref, xblk, sems):
    del block_e_ref
    b = pl.program_id(0)
    n_used = n_used_ref[0]
    n_prompt_rows = xs_p_ref.shape[0]
    cpb = CHUNKS_PER_BLOCK

    def chunk_copy(src_hbm, row, c, slot):
        return pltpu.make_async_copy(src_hbm.at[pl.ds(row, CHUNK)],
                                     xblk.at[slot, pl.ds(c * CHUNK, CHUNK)], sems.at[slot])

    def start_gather(blk, slot):
        for c in range(cpb):
            row = src_ref[blk * cpb + c]

            @pl.when(row < n_prompt_rows)
            def _():
                chunk_copy(xs_p_ref, pl.multiple_of(row, CHUNK), c, slot).start()

            @pl.when(row >= n_prompt_rows)
            def _():
                chunk_copy(xs_s_ref, pl.multiple_of(row - n_prompt_rows, CHUNK), c, slot).start()

    slot = b % 2

    @pl.when(jnp.logical_and(b == 0, n_used > 0))
    def _():
        start_gather(0, 0)

    @pl.when(b + 1 < n_used)
    def _():
        start_gather(b + 1, 1 - slot)

    @pl.when(b < n_used)
    def _():
        for c in range(cpb):
            chunk_copy(xs_p_ref, 0, c, slot).wait()
        x = xblk[slot].astype(BF16)
        gate = jnp.dot(x, wg_ref[...].astype(BF16), preferred_element_type=F32)
        up = jnp.dot(x, wu_ref[...].astype(BF16), preferred_element_type=F32)
        hid = gate * (1.0 / (1.0 + jnp.exp(-gate))) * up
        y_ref[...] = jnp.dot(hid.astype(BF16), wd_ref[...].astype(BF16), preferred_element_type=F32)

    @pl.when(b >= n_used)
    def _():
        y_ref[...] = jnp.zeros_like(y_ref)


def _experts(block_e, n_used, src_row, xs_p, xs_s, w_gate, w_up, w_down):
    n_blocks = block_e.shape[0]
    _, d, d_exp = w_gate.shape
    blk = EXPERT_BLOCK
    weight = lambda k, n: pl.BlockSpec((None, k, n), lambda b, be, nu, src: (be[b], 0, 0))
    any_spec = pl.BlockSpec(memory_space=pl.ANY)
    return pl.pallas_call(
        _experts_kernel,
        grid_spec=pltpu.PrefetchScalarGridSpec(
            num_scalar_prefetch=3,
            grid=(n_blocks,),
            in_specs=[any_spec, any_spec, weight(d, d_exp), weight(d, d_exp), weight(d_exp, d)],
            out_specs=pl.BlockSpec((blk, d), lambda b, be, nu, src: (b, 0)),
            scratch_shapes=[pltpu.VMEM((2, blk, d), F32), pltpu.SemaphoreType.DMA((2,))],
        ),
        out_shape=jax.ShapeDtypeStruct((n_blocks * blk, d), F32),
        compiler_params=pltpu.CompilerParams(
            dimension_semantics=("arbitrary",), vmem_limit_bytes=VMEM_LIMIT),
        name="experts",
    )(block_e, n_used, src_row, xs_p, xs_s, w_gate, w_up, w_down)


def _combine_kernel(src_ref, h_ref, route_ref, gn_ref, ybuf_ref, o_ref, yloc, sems):
    i = pl.program_id(0)
    tm = h_ref.shape[0]
    r_l = yloc.shape[1]
    n_chunks = r_l // CHUNK

    def chunk_copy(tile, c, slot):
        row = pl.multiple_of(src_ref[tile * n_chunks + c], CHUNK)
        return pltpu.make_async_copy(ybuf_ref.at[pl.ds(row, CHUNK)],
                                     yloc.at[slot, pl.ds(pl.multiple_of(c * CHUNK, CHUNK), CHUNK)],
                                     sems.at[slot])

    def start_gather(tile, slot):
        def body(c, carry):
            chunk_copy(tile, c, slot).start()
            return carry
        lax.fori_loop(0, n_chunks, body, 0)

    slot = i % 2

    @pl.when(i == 0)
    def _():
        start_gather(0, 0)

    @pl.when(i + 1 < pl.num_programs(0))
    def _():
        start_gather(i + 1, 1 - slot)

    def wait_body(c, carry):
        chunk_copy(i, c, slot).wait()
        return carry
    lax.fori_loop(0, n_chunks, wait_body, 0)

    route = route_ref[...]
    l0 = route[:, R_POS0:R_POS0 + 1].astype(jnp.int32)
    l1 = route[:, R_POS1:R_POS1 + 1].astype(jnp.int32)
    srow = lax.broadcasted_iota(jnp.int32, (tm, r_l), 1)
    gates = jnp.where(srow == l0, route[:, R_G0:R_G0 + 1],
                      jnp.where(srow == l1, route[:, R_G1:R_G1 + 1], 0.0)).astype(BF16)
    f = jnp.dot(gates, yloc[slot].astype(BF16), preferred_element_type=F32)
    o_ref[...] = _rms(h_ref[...] + f, gn_ref[...])


def _combine(tile_src, h, route, norm_g, ybuf):
    t, d = h.shape
    tm = min(ROW_TILE, t)
    r_l = _local_rows(tm)
    return pl.pallas_call(
        _combine_kernel,
        grid_spec=pltpu.PrefetchScalarGridSpec(
            num_scalar_prefetch=1,
            grid=(t // tm,),
            in_specs=[pl.BlockSpec((tm, d), lambda i, src: (i, 0)),
                      pl.BlockSpec((tm, LANES), lambda i, src: (i, 0)),
                      pl.BlockSpec((1, d), lambda i, src: (0, 0)),
                      pl.BlockSpec(memory_space=pl.ANY)],
            out_specs=pl.BlockSpec((tm, d), lambda i, src: (i, 0)),
            scratch_shapes=[pltpu.VMEM((2, r_l, d), F32),
                            pltpu.SemaphoreType.DMA((2,))],
        ),
        out_shape=jax.ShapeDtypeStruct((t, d), F32),
        compiler_params=pltpu.CompilerParams(
            dimension_semantics=("arbitrary",), vmem_limit_bytes=VMEM_LIMIT),
        name="combine",
    )(tile_src, h, route, norm_g, ybuf)


def kernel(x_prompt, x_sample, cache_k, cache_v, state_conv, norm_mix, w_in, conv_w, norm_out_attn,
           norm_out_conv, w_out, norm_ffn, w_router_group, b_router_group, w_router_expert,
           b_router_expert, w_gate, w_up, w_down, norm_final):
    n_seq, seq_len, d = x_prompt.shape
    db, ds, _ = x_sample.shape
    depth = w_in.shape[0]
    _, _, w_buf, n_heads, dh = cache_k.shape
    d_attn = n_heads * dh
    d_conv = d - d_attn
    assert depth == 1 and ds == 1 and dh == HEAD_DIM
    assert seq_len % (max(DILATIONS) * WIN_KEYS) == 0 and seq_len <= max(DILATIONS) * WIN_KEYS
    layer = 0
    tp, ts = n_seq * seq_len, db

    xp = x_prompt.reshape(tp, d)
    xs = x_sample.reshape(ts, d)
    row = lambda vec: vec.reshape(1, -1)
    w_in_b = w_in[layer].astype(BF16)
    w_out_b = w_out[layer].astype(BF16)
    g_mix, g_oa, g_oc, g_ffn = (row(norm_mix[layer]), row(norm_out_attn[layer]),
                                row(norm_out_conv[layer]), row(norm_ffn[layer]))
    st0, st1 = state_conv[layer, :, 0, :], state_conv[layer, :, 1, :]

    qp, kp, vp, kp_t, vp_t, ocp, conv_p = _mix_in_prompt(
        xp, g_mix, w_in_b, conv_w[layer], g_oc, seq_len=seq_len, d_attn=d_attn, d_conv=d_conv)
    qs, ks, vs, ocs, us = _mix_in_sample(
        xs, g_mix, w_in_b, conv_w[layer], g_oc, st0, st1, d_attn=d_attn, d_conv=d_conv)

    attn_p = _attn_prompt(qp, kp, vp, n_seq=n_seq, seq_len=seq_len)
    heads = lambda a: a.reshape(ts, 1, d_attn)
    positions_last = lambda c: jnp.transpose(c, (0, 2, 3, 1))
    attn_s = _attn_sample(heads(qs), heads(ks), heads(vs),
                          positions_last(cache_k[layer]), positions_last(cache_v[layer]))
    attn_s = attn_s.reshape(ts, d_attn)

    n_route = N_GROUPS + N_EXPERTS
    w_router = jnp.zeros((d, LANES), F32).at[:, :N_GROUPS].set(w_router_group[layer])
    w_router = w_router.at[:, N_GROUPS:n_route].set(w_router_expert[layer])
    b_router = jnp.zeros((1, LANES), F32).at[0, :N_GROUPS].set(b_router_group[layer])
    b_router = b_router.at[0, N_GROUPS:n_route].set(b_router_expert[layer])
    mix_out = functools.partial(_mix_out, norm_ga=g_oa, w_out_bf16=w_out_b, norm_gf=g_ffn,
                                w_router=w_router, b_router=b_router)
    h_p, route_p, xs_p, cnt_p = mix_out(xp, attn_p, ocp)
    h_s, route_s, xs_s, cnt_s = mix_out(xs, attn_s, ocs)

    tile_chunks = jnp.concatenate([cnt_p[:, 0, ROUTER_LANE0:n_route],
                                   cnt_s[:, 0, ROUTER_LANE0:n_route]], axis=0).astype(jnp.int32)
    ntp, nts = cnt_p.shape[0], cnt_s.shape[0]
    tm_p, tm_s = tp // ntp, ts // nts
    rl_p, rl_s = _local_rows(tm_p), _local_rows(tm_s)
    tile_row0 = jnp.concatenate([jnp.arange(ntp, dtype=jnp.int32) * rl_p,
                                 ntp * rl_p + jnp.arange(nts, dtype=jnp.int32) * rl_s])
    total_chunks = ntp * _max_tile_chunks(tm_p) + nts * _max_tile_chunks(tm_s)
    n_blocks = -(-(total_chunks + N_EXPERTS * (CHUNKS_PER_BLOCK - 1)) // CHUNKS_PER_BLOCK)
    block_e, n_used, src_row, tile_src = _sorted_layout(tile_chunks, tile_row0, rl_p // CHUNK,
                                                        n_blocks)
    ybuf = _experts(block_e, n_used, src_row, xs_p, xs_s, w_gate[layer], w_up[layer],
                    w_down[layer])
    g_fin = row(norm_final)
    y_p = _combine(tile_src[:ntp].reshape(-1), h_p, route_p, g_fin, ybuf)
    y_s = _combine(tile_src[ntp:, :rl_s // CHUNK].reshape(-1), h_s, route_s, g_fin, ybuf)

    w_keep = min(max(DILATIONS) * WIN_KEYS, seq_len)
    kv5 = lambda a_t: jnp.transpose(a_t.reshape(n_seq, n_heads, dh, seq_len),
                                    (0, 3, 1, 2))[None, :, seq_len - w_keep:]
    conv_s = jnp.stack([st1, us], axis=1)[None]
    kvs = lambda a: a.reshape(1, ts, 1, n_heads, dh)
    return (y_p.reshape(n_seq, seq_len, d), y_s.reshape(db, ds, d), kv5(kp_t), kv5(vp_t),
            conv_p[None], kvs(ks), kvs(vs), conv_s)
```

```python
import functools

import jax
import jax.numpy as jnp
from jax import lax
from jax.experimental import pallas as pl
from jax.experimental.pallas import tpu as pltpu

HEAD_DIM = 64
WIN_KEYS = 128
DILATIONS = (1, 4, 16)
CONV_WIDTH = 3
N_GROUPS = 4
EXPERTS_PER_GROUP = 8
N_EXPERTS = N_GROUPS * EXPERTS_PER_GROUP
EPS = 1e-6
NEG = -1e30
LOG2_E = 1.4426950408889634

LANES = 128
ROW_TILE = 512
EXPERT_BLOCK = 256
ATTN_UNROLL = 4
VMEM_LIMIT = 56 * 1024 * 1024

F32 = jnp.float32
BF16 = jnp.bfloat16


def _rms(x, g):
    return x * lax.rsqrt(jnp.mean(x * x, axis=-1, keepdims=True) + EPS) * g


def _mix_in_kernel(*refs, d_attn, d_conv, sequential):
    if sequential:
        (x_ref, g_ref, w_ref, cw_ref, gc_ref,
         q_ref, k_ref, v_ref, kt_ref, vt_ref, oc_ref, st_ref, carry_ref) = refs
    else:
        (x_ref, g_ref, w_ref, cw_ref, gc_ref, st0_ref, st1_ref,
         q_ref, k_ref, v_ref, oc_ref, u_ref) = refs
    x = x_ref[...]
    xb = _rms(x, g_ref[...]).astype(BF16)

    def proj(lo, width):
        return jnp.dot(xb, w_ref[:, lo:lo + width], preferred_element_type=F32)

    q_ref[...] = proj(0, d_attn)
    k = proj(d_attn, d_attn)
    v = proj(2 * d_attn, d_attn)
    k_ref[...] = k
    v_ref[...] = v
    gate = proj(3 * d_attn, d_conv)
    u = proj(3 * d_attn + d_conv, d_conv) * proj(3 * d_attn + 2 * d_conv, d_conv)

    tm = x.shape[0]
    if sequential:
        kt_ref[...] = k.T
        vt_ref[...] = v.T

        @pl.when(pl.program_id(1) == 0)
        def _():
            carry_ref[...] = jnp.zeros_like(carry_ref)

        row = lax.broadcasted_iota(jnp.int32, u.shape, 0)
        prev1 = carry_ref[1:2, :]
        prev2 = carry_ref[0:1, :]
        u1 = jnp.where(row == 0, prev1, pltpu.roll(u, 1, axis=0))
        u2 = jnp.where(row == 0, prev2, jnp.where(row == 1, prev1, pltpu.roll(u, 2, axis=0)))
        carry_ref[0:2, :] = u[tm - 2:tm, :]
        st_ref[...] = u[tm - 2:tm, :]
    else:
        u_ref[...] = u
        u2 = st0_ref[...]
        u1 = st1_ref[...]
    z = u2 * cw_ref[0:1, :] + u1 * cw_ref[1:2, :] + u * cw_ref[2:3, :]
    oc_ref[...] = _rms(gate * z, gc_ref[...])


def _mix_in_call(kernel, grid, in_specs, out_specs, out_shape, scratch, args):
    return pl.pallas_call(
        kernel, grid=grid, in_specs=in_specs, out_specs=out_specs, out_shape=out_shape,
        scratch_shapes=scratch,
        compiler_params=pltpu.CompilerParams(
            dimension_semantics=("arbitrary",) * len(grid), vmem_limit_bytes=VMEM_LIMIT),
        name="mix_in",
    )(*args)


def _mix_in_prompt(x2d, norm_g, w_in_bf16, conv_w, norm_gc, *, seq_len, d_attn, d_conv):
    t, d = x2d.shape
    tm = min(ROW_TILE, seq_len)
    n_seq, per = t // seq_len, seq_len // tm
    const = lambda b, s: (0, 0)
    row = lambda width: pl.BlockSpec((tm, width), lambda b, s: (b * per + s, 0))
    col = pl.BlockSpec((None, d_attn, tm), lambda b, s: (b, 0, s))
    f32 = lambda *shape: jax.ShapeDtypeStruct(shape, F32)
    return _mix_in_call(
        functools.partial(_mix_in_kernel, d_attn=d_attn, d_conv=d_conv, sequential=True),
        (n_seq, per),
        [row(d), pl.BlockSpec((1, d), const), pl.BlockSpec(w_in_bf16.shape, const),
         pl.BlockSpec((CONV_WIDTH, d_conv), const), pl.BlockSpec((1, d_conv), const)],
        [row(d_attn)] * 3 + [col] * 2 + [row(d_conv),
                                         pl.BlockSpec((None, CONV_WIDTH - 1, d_conv),
                                                      lambda b, s: (b, 0, 0))],
        [f32(t, d_attn)] * 3 + [f32(n_seq, d_attn, seq_len)] * 2
        + [f32(t, d_conv), f32(n_seq, CONV_WIDTH - 1, d_conv)],
        [pltpu.VMEM((8, d_conv), F32)],
        (x2d, norm_g, w_in_bf16, conv_w, norm_gc))


def _mix_in_sample(x2d, norm_g, w_in_bf16, conv_w, norm_gc, st0, st1, *, d_attn, d_conv):
    t, d = x2d.shape
    full = lambda arr: pl.BlockSpec(arr.shape, lambda i: (0,) * arr.ndim)
    f32 = lambda *shape: jax.ShapeDtypeStruct(shape, F32)
    args = (x2d, norm_g, w_in_bf16, conv_w, norm_gc, st0, st1)
    outs = [f32(t, d_attn)] * 3 + [f32(t, d_conv)] * 2
    return _mix_in_call(
        functools.partial(_mix_in_kernel, d_attn=d_attn, d_conv=d_conv, sequential=False),
        (1,), [full(a) for a in args], [full(o) for o in outs], outs, [], args)


def _attn_prompt_kernel(q_ref, k_ref, v_ref, o_ref, m_s, l_s, a_s, *, seq_len):
    w = WIN_KEYS
    scale = HEAD_DIM ** -0.5 * LOG2_E
    r_i = lax.broadcasted_iota(jnp.int32, (2 * w, 2 * w), 0) & (w - 1)
    c_i = lax.broadcasted_iota(jnp.int32, (2 * w, 2 * w), 1)
    mask_cur = (lax.broadcasted_iota(jnp.int32, (2 * w, w), 1)
                <= lax.broadcasted_iota(jnp.int32, (2 * w, w), 0) & (w - 1))
    mask_both = jnp.logical_and(c_i >= r_i, c_i - w <= r_i)
    first_head = lax.broadcasted_iota(jnp.int32, (w, 2 * HEAD_DIM), 1) < HEAD_DIM
    dn_t = (((1,), (1,)), ((), ()))

    def rows(start, dil):
        if dil > 1:
            return pl.ds(start, w, stride=dil)
        return pl.ds(start if isinstance(start, int) else pl.multiple_of(start, w), w)

    def run_branch(dil, first, last):
        span = dil * w
        nb = seq_len // span

        def blocks(its, with_prev):
            mask = mask_both if with_prev else mask_cur
            cur, qs, ks, vs = [], [], [], []
            for it in its:
                g = it % dil
                n = it // dil
                c = rows(g + n * span, dil)
                cur.append(c)
                qb = (q_ref[c, :] * scale).astype(BF16)
                zero = jnp.zeros_like(qb)
                qs.append(jnp.concatenate([jnp.where(first_head, qb, zero),
                                           jnp.where(first_head, zero, qb)], axis=0))
                k = k_ref[c, :].astype(BF16)
                v = v_ref[c, :].astype(BF16)
                if with_prev:
                    p = rows(g + (n - 1) * span, dil)
                    k = jnp.concatenate([k_ref[p, :].astype(BF16), k], axis=0)
                    v = jnp.concatenate([v_ref[p, :].astype(BF16), v], axis=0)
                ks.append(k)
                vs.append(v)
            scores = [lax.dot_general(q, k, dn_t, preferred_element_type=F32)
                      for q, k in zip(qs, ks)]
            ms, ps = [], []
            for s in scores:
                s = jnp.where(mask, s, NEG)
                m = jnp.max(s, axis=-1, keepdims=True)
                ms.append(m)
                ps.append(jnp.exp2(s - m).astype(BF16))
            ones = jnp.ones((ks[0].shape[0], 2 * HEAD_DIM), BF16)
            accs = [jnp.dot(p, jnp.concatenate([v, ones], axis=1), preferred_element_type=F32)
                    for p, v in zip(ps, vs)]
            for c, m, acc_l in zip(cur, ms, accs):
                acc, l = acc_l[:, :2 * HEAD_DIM], acc_l[:, 2 * HEAD_DIM:]
                m_b = jnp.where(first_head, m[:w], m[w:])
                l_b = jnp.where(first_head, l[:w], l[w:])
                a_b = jnp.where(first_head, acc[:w], acc[w:])
                if not first:
                    m_o = m_s[c, :]
                    m_n = jnp.maximum(m_o, m_b)
                    w_o = jnp.exp2(m_o - m_n)
                    w_b = jnp.exp2(m_b - m_n)
                    l_b = w_o * l_s[c, :] + w_b * l_b
                    a_b = w_o * a_s[c, :] + w_b * a_b
                    m_b = m_n
                if last:
                    o_ref[c, :] = a_b / l_b
                else:
                    m_s[c, :] = m_b
                    l_s[c, :] = l_b
                    a_s[c, :] = a_b

        def run(lo, hi, with_prev):
            u = ATTN_UNROLL
            trips = (hi - lo) // u

            def body(t, carry):
                blocks([lo + t * u + j for j in range(u)], with_prev)
                return carry

            if trips:
                lax.fori_loop(0, trips, body, 0)
            if lo + trips * u < hi:
                blocks(list(range(lo + trips * u, hi)), with_prev)

        run(0, dil, False)
        run(dil, dil * nb, True)

    order = sorted(DILATIONS, reverse=True)
    for i, dil in enumerate(order):
        run_branch(dil, i == 0, i == len(order) - 1)


def _attn_prompt(q, k, v, *, n_seq, seq_len):
    t, d_attn = q.shape
    pair = 2 * HEAD_DIM
    spec = pl.BlockSpec((seq_len, pair), lambda b, h: (b, h))
    return pl.pallas_call(
        functools.partial(_attn_prompt_kernel, seq_len=seq_len),
        grid=(n_seq, d_attn // pair),
        in_specs=[spec] * 3,
        out_specs=spec,
        out_shape=jax.ShapeDtypeStruct((t, d_attn), F32),
        scratch_shapes=[pltpu.VMEM((seq_len, pair), F32)] * 3,
        compiler_params=pltpu.CompilerParams(
            dimension_semantics=("arbitrary", "arbitrary"), vmem_limit_bytes=VMEM_LIMIT),
        name="attn_prompt",
    )(q, k, v)


def _attn_sample_kernel(q_ref, kn_ref, vn_ref, kt_ref, vt_ref, o_ref):
    n_heads, dh, w_buf = kt_ref.shape
    delta = w_buf - lax.broadcasted_iota(jnp.int32, (1, w_buf), 1)
    cnt = jnp.zeros((1, w_buf), F32)
    for dil in DILATIONS:
        assert dil & (dil - 1) == 0
        member = jnp.where(delta <= dil * WIN_KEYS, 1.0, 0.0)
        cnt = cnt + jnp.where((delta & (dil - 1)) == 0, member, 0.0)
    eye = (lax.broadcasted_iota(jnp.int32, (dh, dh), 0)
           == lax.broadcasted_iota(jnp.int32, (dh, dh), 1))
    to_col = lambda r: jnp.sum(jnp.where(eye, r, 0.0), axis=1, keepdims=True)
    to_row = lambda c: jnp.sum(jnp.where(eye, c, 0.0), axis=0, keepdims=True)
    outs = []
    for h in range(n_heads):
        sl = slice(h * dh, (h + 1) * dh)
        q = q_ref[:, sl] * (HEAD_DIM ** -0.5)
        s_self = jnp.sum(q * kn_ref[:, sl], axis=1, keepdims=True)
        s = jnp.sum(to_col(q) * kt_ref[h], axis=0, keepdims=True)
        s = jnp.where(cnt > 0.0, s, NEG)
        m = jnp.maximum(jnp.max(s, axis=1, keepdims=True), s_self)
        p = cnt * jnp.exp(s - m)
        p_self = len(DILATIONS) * jnp.exp(s_self - m)
        l = jnp.sum(p, axis=1, keepdims=True) + p_self
        acc = jnp.sum(p * vt_ref[h], axis=1, keepdims=True)
        outs.append((to_row(acc) + p_self * vn_ref[:, sl]) / l)
    o_ref[...] = jnp.concatenate(outs, axis=1)


def _attn_sample(q, k_new, v_new, cache_kt, cache_vt):
    db, n_heads, dh, w_buf = cache_kt.shape
    head_spec = pl.BlockSpec((None, 1, n_heads * dh), lambda b: (b, 0, 0))
    cache_spec = pl.BlockSpec((None, n_heads, dh, w_buf), lambda b: (b, 0, 0, 0))
    return pl.pallas_call(
        _attn_sample_kernel,
        grid=(db,),
        in_specs=[head_spec] * 3 + [cache_spec] * 2,
        out_specs=head_spec,
        out_shape=jax.ShapeDtypeStruct((db, 1, n_heads * dh), F32),
        compiler_params=pltpu.CompilerParams(
            dimension_semantics=("arbitrary",), vmem_limit_bytes=VMEM_LIMIT),
        name="attn_sample",
    )(q, k_new, v_new, cache_kt, cache_vt)


R_E0, R_E1, R_G0, R_G1, R_POS0, R_POS1 = range(6)
ROUTER_LANE0 = N_GROUPS
CHUNK = 8
CHUNKS_PER_BLOCK = EXPERT_BLOCK // CHUNK


def _max_tile_chunks(tm):
    return (2 * tm + (CHUNK - 1) * N_EXPERTS) // CHUNK


def _local_rows(tm):
    return 2 * tm + N_EXPERTS * CHUNK


def _mix_out_kernel(*refs, n_tiles, has_tail):
    if not has_tail:
        _mix_out_tile(*refs)
        return
    *tile_in, tail_ref, h_ref, route_ref, xs_ref, cnt_ref = refs

    @pl.when(pl.program_id(0) < n_tiles)
    def _():
        _mix_out_tile(*tile_in, h_ref, route_ref, xs_ref, cnt_ref)

    @pl.when(pl.program_id(0) == n_tiles)
    def _():
        rows = tail_ref.shape[0]
        xs_ref[0:rows, :] = tail_ref[...]
        xs_ref[rows:, :] = jnp.zeros((xs_ref.shape[0] - rows, xs_ref.shape[1]), F32)


def _mix_out_tile(x_ref, a_ref, oc_ref, ga_ref, wo_ref, gf_ref, wr_ref, br_ref,
                  h_ref, route_ref, xs_ref, cnt_ref):
    d_attn = a_ref.shape[1]
    tm, d = x_ref.shape
    a = _rms(a_ref[...], ga_ref[...]).astype(BF16)
    mix = jnp.dot(a, wo_ref[0:d_attn, :], preferred_element_type=F32)
    mix = mix + jnp.dot(oc_ref[...].astype(BF16), wo_ref[d_attn:, :], preferred_element_type=F32)
    h = x_ref[...] + mix
    h_ref[...] = h
    tok = _rms(h, gf_ref[...])

    tok_hi = tok.astype(BF16)
    tok_lo = (tok - tok_hi.astype(F32)).astype(BF16)
    hi_part = jnp.dot(tok_hi, wr_ref[...], preferred_element_type=F32)
    lo_part = jnp.dot(tok_lo, wr_ref[:, :LANES], preferred_element_type=F32)
    logits = hi_part[:, :LANES] + hi_part[:, LANES:] + lo_part + br_ref[...]
    lane = lax.broadcasted_iota(jnp.int32, logits.shape, 1)
    big = jnp.int32(LANES)
    neg_inf = jnp.float32(-jnp.inf)

    def top1(vals):
        best = jnp.max(vals, axis=-1, keepdims=True)
        idx = jnp.min(jnp.where(vals == best, lane, big), axis=-1, keepdims=True)
        return best, idx

    is_group = lane < N_GROUPS
    lg = jnp.where(is_group, logits, neg_inf)
    mg, g_sel = top1(lg)
    p_group = 1.0 / jnp.sum(jnp.where(is_group, jnp.exp(lg - mg), 0.0), axis=-1, keepdims=True)

    lo = ROUTER_LANE0 + g_sel * EXPERTS_PER_GROUP
    in_group = jnp.logical_and(lane >= lo, lane < lo + EXPERTS_PER_GROUP)
    le = jnp.where(in_group, logits, neg_inf)
    v1, i1 = top1(le)
    v2, i2 = top1(jnp.where(lane == i1, neg_inf, le))
    e2 = jnp.exp(v2 - v1)
    gate1 = p_group / (1.0 + e2)
    gate2 = p_group * e2 / (1.0 + e2)

    oh1 = lane == i1
    oh2 = lane == i2
    both = jnp.where(jnp.logical_or(oh1, oh2), 1.0, 0.0)
    r_i = lax.broadcasted_iota(jnp.int32, (tm, tm), 0)
    c_i = lax.broadcasted_iota(jnp.int32, (tm, tm), 1)
    strict_lower = jnp.where(c_i < r_i, 1.0, 0.0).astype(BF16)
    before = jnp.dot(strict_lower, both.astype(BF16), preferred_element_type=F32)
    chunks = jnp.floor((jnp.sum(both, axis=0, keepdims=True) + (CHUNK - 1)) * (1.0 / CHUNK))
    u_r = lax.broadcasted_iota(jnp.int32, (LANES, LANES), 0)
    u_c = lax.broadcasted_iota(jnp.int32, (LANES, LANES), 1)
    strict_upper = jnp.where(u_r < u_c, 1.0, 0.0).astype(BF16)
    chunks8 = jnp.broadcast_to(chunks, (8, LANES))
    first_row = CHUNK * jnp.dot(chunks8.astype(BF16), strict_upper,
                                preferred_element_type=F32)[0:1, :]
    pos = first_row + before
    pos1 = jnp.sum(jnp.where(oh1, pos, 0.0), axis=-1, keepdims=True)
    pos2 = jnp.sum(jnp.where(oh2, pos, 0.0), axis=-1, keepdims=True)
    cnt_ref[...] = jnp.where(lax.broadcasted_iota(jnp.int32, (8, LANES), 0) == 0, chunks8, 0.0)

    rec = jnp.zeros(logits.shape, F32)
    for col, val in ((R_E0, (i1 - ROUTER_LANE0).astype(F32)), (R_E1, (i2 - ROUTER_LANE0).astype(F32)),
                     (R_G0, gate1), (R_G1, gate2), (R_POS0, pos1), (R_POS1, pos2)):
        rec = jnp.where(lane == col, val, rec)
    route_ref[...] = rec

    rec_t = rec.T
    l1 = rec_t[R_POS0:R_POS0 + 1, :].astype(jnp.int32)
    l2 = rec_t[R_POS1:R_POS1 + 1, :].astype(jnp.int32)
    srow = lax.broadcasted_iota(jnp.int32, (xs_ref.shape[0], tm), 0)
    perm = jnp.where(srow == l1, 1.0, jnp.where(srow == l2, 1.0, 0.0)).astype(BF16)
    xs_ref[...] = jnp.dot(perm, tok_hi, preferred_element_type=F32)


def _mix_out(x2d, attn, oconv, norm_ga, w_out_bf16, norm_gf, w_router, b_router, tail=None):
    t, d = x2d.shape
    d_attn, d_conv = attn.shape[1], oconv.shape[1]
    tm = min(ROW_TILE, t)
    nt = t // tm
    r_l = _local_rows(tm)
    has_tail = tail is not None
    tile = lambda i: jnp.minimum(i, nt - 1)
    row = lambda width: pl.BlockSpec((tm, width), lambda i: (tile(i), 0))
    full = lambda arr: pl.BlockSpec(arr.shape, lambda i: (0, 0))
    args = [x2d, attn, oconv, norm_ga, w_out_bf16, norm_gf, w_router, b_router]
    in_specs = [row(d), row(d_attn), row(d_conv)] + [full(a) for a in args[3:]]
    if has_tail:
        assert tail.shape[0] <= r_l and tail.shape[1] == d
        args.append(tail)
        in_specs.append(full(tail))
    return pl.pallas_call(
        functools.partial(_mix_out_kernel, n_tiles=nt, has_tail=has_tail),
        grid=(nt + has_tail,),
        in_specs=in_specs,
        out_specs=[row(d), row(LANES), pl.BlockSpec((r_l, d), lambda i: (i, 0)),
                   pl.BlockSpec((None, 8, LANES), lambda i: (tile(i), 0, 0))],
        out_shape=[jax.ShapeDtypeStruct((t, d), F32), jax.ShapeDtypeStruct((t, LANES), F32),
                   jax.ShapeDtypeStruct(((nt + has_tail) * r_l, d), F32),
                   jax.ShapeDtypeStruct((nt, 8, LANES), F32)],
        compiler_params=pltpu.CompilerParams(
            dimension_semantics=("arbitrary",), vmem_limit_bytes=VMEM_LIMIT),
        name="mix_out",
    )(*args)


def _sorted_layout(tile_chunks, tile_row0, max_local, n_blocks):
    nt, n_exp = tile_chunks.shape
    cpb = CHUNKS_PER_BLOCK
    i32 = jnp.int32
    seg = jnp.sum(tile_chunks, axis=0)
    padded = (seg + cpb - 1) // cpb * cpb
    pend = jnp.cumsum(padded)
    pstart = pend - padded
    tile_incl = jnp.cumsum(tile_chunks, axis=0)
    tile_excl = tile_incl - tile_chunks
    local_incl = jnp.cumsum(tile_chunks, axis=1)
    local_excl = local_incl - tile_chunks
    base = pstart[None, :] + tile_excl

    block_first = jnp.arange(n_blocks, dtype=i32) * cpb
    block_e = jnp.minimum(jnp.sum((pend[None, :] <= block_first[:, None]).astype(i32), axis=1),
                          n_exp - 1)
    n_used = (pend[-1:] // cpb).astype(i32)

    onehot_pick = lambda onehot, table: jnp.sum(jnp.where(onehot, table, 0), axis=-1)

    is_e = block_e[:, None] == jnp.arange(n_exp, dtype=i32)[None, :]
    of_expert = lambda table_te: onehot_pick(is_e[:, None, :], table_te[None, :, :])
    incl_b, cnt_b, lexcl_b = of_expert(tile_incl), of_expert(tile_chunks), of_expert(local_excl)
    q = (block_first - onehot_pick(is_e, pstart[None, :]))[:, None] + jnp.arange(cpb, dtype=i32)
    tile_q = jnp.minimum(jnp.sum((incl_b[:, None, :] <= q[:, :, None]).astype(i32), axis=2), nt - 1)
    is_t = tile_q[:, :, None] == jnp.arange(nt, dtype=i32)[None, None, :]
    of_tile = lambda table_bt: onehot_pick(is_t, table_bt[:, None, :])
    local_chunk = of_tile(lexcl_b) + q - of_tile(incl_b - cnt_b)
    in_run = jnp.logical_and(q >= 0, q < onehot_pick(is_e, seg[None, :])[:, None])
    src_row = jnp.where(in_run, of_tile(tile_row0[None, :]) + CHUNK * local_chunk, 0)
    src_row = src_row.reshape(-1).astype(i32)

    c = jnp.arange(max_local, dtype=i32)
    e_c = jnp.minimum(jnp.sum((local_incl[:, None, :] <= c[None, :, None]).astype(i32), axis=2),
                      n_exp - 1)
    is_ec = e_c[:, :, None] == jnp.arange(n_exp, dtype=i32)[None, None, :]
    of_run = lambda table_te: onehot_pick(is_ec, table_te[:, None, :])
    global_chunk = of_run(base) + c[None, :] - of_run(local_excl)
    tile_src = jnp.where(c[None, :] < local_incl[:, -1:], CHUNK * global_chunk, 0).astype(i32)
    return block_e.astype(i32), n_used, src_row, tile_src


def _chunk_gather(src_ref, hbm_ref, buf, sems, item, slot, n_chunks, *, wait):
    for c in range(n_chunks):
        row = 0 if wait else pl.multiple_of(src_ref[item * n_chunks + c], CHUNK)
        copy = pltpu.make_async_copy(hbm_ref.at[pl.ds(row, CHUNK)],
                                     buf.at[slot, pl.ds(c * CHUNK, CHUNK)], sems.at[slot])
        if wait:
            copy.wait()
        else:
            copy.start()


def _prefetched(gather, step, n_steps, body):
    slot = step % 2

    @pl.when(step == 0)
    def _():
        gather(0, 0, wait=False)

    gather(step, slot, wait=True)
    body(slot, lambda: gather(jnp.minimum(step + 1, n_steps - 1), 1 - slot, wait=False))

    @pl.when(step == n_steps - 1)
    def _():
        gather(step, 1 - slot, wait=True)


def _experts_kernel(block_e_ref, n_used_ref, src_ref, xs_ref, wg_ref, wu_ref, wd_ref,
                    y_ref, xblk, sems):
    del block_e_ref
    b = pl.program_id(0)
    gather = functools.partial(_chunk_gather, src_ref, xs_ref, xblk, sems,
                               n_chunks=CHUNKS_PER_BLOCK)

    def body(slot, start_next):
        @pl.when(b < n_used_ref[0])
        def _():
            x = xblk[slot].astype(BF16)
            start_next()
            gate = jnp.dot(x, wg_ref[...].astype(BF16), preferred_element_type=F32)
            up = jnp.dot(x, wu_ref[...].astype(BF16), preferred_element_type=F32)
            hid = gate * (1.0 / (1.0 + jnp.exp(-gate))) * up
            y_ref[...] = jnp.dot(hid.astype(BF16), wd_ref[...].astype(BF16),
                                 preferred_element_type=F32)

        @pl.when(b >= n_used_ref[0])
        def _():
            start_next()
            y_ref[...] = jnp.zeros_like(y_ref)

    _prefetched(gather, b, pl.num_programs(0), body)


def _experts(block_e, n_used, src_row, xs, w_gate, w_up, w_down):
    n_blocks = block_e.shape[0]
    _, d, d_exp = w_gate.shape
    blk = EXPERT_BLOCK
    weight = lambda k, n: pl.BlockSpec((None, k, n), lambda b, be, nu, src: (be[b], 0, 0))
    return pl.pallas_call(
        _experts_kernel,
        grid_spec=pltpu.PrefetchScalarGridSpec(
            num_scalar_prefetch=3,
            grid=(n_blocks,),
            in_specs=[pl.BlockSpec(memory_space=pl.ANY),
                      weight(d, d_exp), weight(d, d_exp), weight(d_exp, d)],
            out_specs=pl.BlockSpec((blk, d), lambda b, be, nu, src: (b, 0)),
            scratch_shapes=[pltpu.VMEM((2, blk, d), F32), pltpu.SemaphoreType.DMA((2,))],
        ),
        out_shape=jax.ShapeDtypeStruct((n_blocks * blk, d), F32),
        compiler_params=pltpu.CompilerParams(
            dimension_semantics=("arbitrary",), vmem_limit_bytes=VMEM_LIMIT),
        name="experts",
    )(block_e, n_used, src_row, xs, w_gate, w_up, w_down)


def _combine_kernel(src_ref, h_ref, route_ref, gn_ref, ybuf_ref, o_ref, yloc, sems):
    tm = h_ref.shape[0]
    r_l = yloc.shape[1]
    gather = functools.partial(_chunk_gather, src_ref, ybuf_ref, yloc, sems,
                               n_chunks=r_l // CHUNK)

    def body(slot, start_next):
        y = yloc[slot].astype(BF16)
        start_next()
        route = route_ref[...]
        l0 = route[:, R_POS0:R_POS0 + 1].astype(jnp.int32)
        l1 = route[:, R_POS1:R_POS1 + 1].astype(jnp.int32)
        srow = lax.broadcasted_iota(jnp.int32, (tm, r_l), 1)
        gates = jnp.where(srow == l0, route[:, R_G0:R_G0 + 1],
                          jnp.where(srow == l1, route[:, R_G1:R_G1 + 1], 0.0)).astype(BF16)
        f = jnp.dot(gates, y, preferred_element_type=F32)
        o_ref[...] = _rms(h_ref[...] + f, gn_ref[...])

    _prefetched(gather, pl.program_id(0), pl.num_programs(0), body)


def _combine(tile_src, h, route, norm_g, ybuf):
    t, d = h.shape
    tm = min(ROW_TILE, t)
    r_l = _local_rows(tm)
    return pl.pallas_call(
        _combine_kernel,
        grid_spec=pltpu.PrefetchScalarGridSpec(
            num_scalar_prefetch=1,
            grid=(t // tm,),
            in_specs=[pl.BlockSpec((tm, d), lambda i, src: (i, 0)),
                      pl.BlockSpec((tm, LANES), lambda i, src: (i, 0)),
                      pl.BlockSpec((1, d), lambda i, src: (0, 0)),
                      pl.BlockSpec(memory_space=pl.ANY)],
            out_specs=pl.BlockSpec((tm, d), lambda i, src: (i, 0)),
            scratch_shapes=[pltpu.VMEM((2, r_l, d), F32),
                            pltpu.SemaphoreType.DMA((2,))],
        ),
        out_shape=jax.ShapeDtypeStruct((t, d), F32),
        compiler_params=pltpu.CompilerParams(
            dimension_semantics=("arbitrary",), vmem_limit_bytes=VMEM_LIMIT),
        name="combine",
    )(tile_src, h, route, norm_g, ybuf)


def kernel(x_prompt, x_sample, cache_k, cache_v, state_conv, norm_mix, w_in, conv_w, norm_out_attn,
           norm_out_conv, w_out, norm_ffn, w_router_group, b_router_group, w_router_expert,
           b_router_expert, w_gate, w_up, w_down, norm_final):
    n_seq, seq_len, d = x_prompt.shape
    db, ds, _ = x_sample.shape
    depth = w_in.shape[0]
    _, _, w_buf, n_heads, dh = cache_k.shape
    d_attn = n_heads * dh
    d_conv = d - d_attn
    assert depth == 1 and ds == 1 and dh == HEAD_DIM
    assert seq_len % (max(DILATIONS) * WIN_KEYS) == 0 and seq_len <= max(DILATIONS) * WIN_KEYS
    layer = 0
    tp, ts = n_seq * seq_len, db

    xp = x_prompt.reshape(tp, d)
    xs = x_sample.reshape(ts, d)
    row = lambda vec: vec.reshape(1, -1)
    w_in_b = w_in[layer].astype(BF16)
    w_out_b = w_out[layer].astype(BF16)
    g_mix, g_oa, g_oc, g_ffn = (row(norm_mix[layer]), row(norm_out_attn[layer]),
                                row(norm_out_conv[layer]), row(norm_ffn[layer]))
    st0, st1 = state_conv[layer, :, 0, :], state_conv[layer, :, 1, :]

    qp, kp, vp, kp_t, vp_t, ocp, conv_p = _mix_in_prompt(
        xp, g_mix, w_in_b, conv_w[layer], g_oc, seq_len=seq_len, d_attn=d_attn, d_conv=d_conv)
    qs, ks, vs, ocs, us = _mix_in_sample(
        xs, g_mix, w_in_b, conv_w[layer], g_oc, st0, st1, d_attn=d_attn, d_conv=d_conv)

    attn_p = _attn_prompt(qp, kp, vp, n_seq=n_seq, seq_len=seq_len)
    heads = lambda a: a.reshape(ts, 1, d_attn)
    positions_last = lambda c: jnp.transpose(c, (0, 2, 3, 1))
    attn_s = _attn_sample(heads(qs), heads(ks), heads(vs),
                          positions_last(cache_k[layer]), positions_last(cache_v[layer]))
    attn_s = attn_s.reshape(ts, d_attn)

    n_route = N_GROUPS + N_EXPERTS
    w_router = jnp.zeros((d, LANES), F32).at[:, :N_GROUPS].set(w_router_group[layer])
    w_router = w_router.at[:, N_GROUPS:n_route].set(w_router_expert[layer])
    b_router = jnp.zeros((1, LANES), F32).at[0, :N_GROUPS].set(b_router_group[layer])
    b_router = b_router.at[0, N_GROUPS:n_route].set(b_router_expert[layer])
    w_router_hi = w_router.astype(BF16)
    w_router_lo = (w_router - w_router_hi.astype(F32)).astype(BF16)
    mix_out = functools.partial(_mix_out, norm_ga=g_oa, w_out_bf16=w_out_b, norm_gf=g_ffn,
                                w_router=jnp.concatenate([w_router_hi, w_router_lo], axis=1),
                                b_router=b_router)
    h_s, route_s, xs_s, cnt_s = mix_out(xs, attn_s, ocs)
    assert cnt_s.shape[0] == 1
    h_p, route_p, xs_all, cnt_p = mix_out(xp, attn_p, ocp, tail=xs_s)

    tile_chunks = jnp.concatenate([cnt_p[:, 0, ROUTER_LANE0:n_route],
                                   cnt_s[:, 0, ROUTER_LANE0:n_route]], axis=0).astype(jnp.int32)
    ntp, nts = cnt_p.shape[0], cnt_s.shape[0]
    tm_p, tm_s = tp // ntp, ts // nts
    rl_p, rl_s = _local_rows(tm_p), _local_rows(tm_s)
    tile_row0 = jnp.arange(ntp + nts, dtype=jnp.int32) * rl_p
    total_chunks = ntp * _max_tile_chunks(tm_p) + nts * _max_tile_chunks(tm_s)
    n_blocks = -(-(total_chunks + N_EXPERTS * (CHUNKS_PER_BLOCK - 1)) // CHUNKS_PER_BLOCK)
    block_e, n_used, src_row, tile_src = _sorted_layout(tile_chunks, tile_row0, rl_p // CHUNK,
                                                        n_blocks)
    ybuf = _experts(block_e, n_used, src_row, xs_all, w_gate[layer], w_up[layer], w_down[layer])
    g_fin = row(norm_final)
    y_p = _combine(tile_src[:ntp].reshape(-1), h_p, route_p, g_fin, ybuf)
    y_s = _combine(tile_src[ntp:, :rl_s // CHUNK].reshape(-1), h_s, route_s, g_fin, ybuf)

    w_keep = min(max(DILATIONS) * WIN_KEYS, seq_len)
    kv5 = lambda a_t: jnp.transpose(a_t.reshape(n_seq, n_heads, dh, seq_len),
                                    (0, 3, 1, 2))[None, :, seq_len - w_keep:]
    conv_s = jnp.stack([st1, us], axis=1)[None]
    kvs = lambda a: a.reshape(1, ts, 1, n_heads, dh)
    return (y_p.reshape(n_seq, seq_len, d), y_s.reshape(db, ds, d), kv5(kp_t), kv5(vp_t),
            conv_p[None], kvs(ks), kvs(vs), conv_s)
```

```python
import functools

import jax
import jax.numpy as jnp
from jax import lax
from jax.experimental import pallas as pl
from jax.experimental.pallas import tpu as pltpu

HEAD_DIM = 64
WIN_KEYS = 128
DILATIONS = (1, 4, 16)
CONV_WIDTH = 3
N_GROUPS = 4
EXPERTS_PER_GROUP = 8
N_EXPERTS = N_GROUPS * EXPERTS_PER_GROUP
EPS = 1e-6
NEG = -1e30
LOG2_E = 1.4426950408889634

LANES = 128
ROW_TILE = 512
EXPERT_BLOCK = 256
ATTN_UNROLL = 8
VMEM_LIMIT = 56 * 1024 * 1024

F32 = jnp.float32
BF16 = jnp.bfloat16


def _rms(x, g):
    return x * lax.rsqrt(jnp.mean(x * x, axis=-1, keepdims=True) + EPS) * g


def _mix_in_kernel(*refs, d_attn, d_conv, sequential):
    if sequential:
        (x_ref, g_ref, w_ref, cw_ref, gc_ref,
         q_ref, k_ref, v_ref, kt_ref, vt_ref, oc_ref, st_ref, carry_ref) = refs
    else:
        (x_ref, g_ref, w_ref, cw_ref, gc_ref, st0_ref, st1_ref,
         q_ref, k_ref, v_ref, oc_ref, u_ref) = refs
    x = x_ref[...]
    xb = _rms(x, g_ref[...]).astype(BF16)

    def proj(lo, width):
        return jnp.dot(xb, w_ref[:, lo:lo + width], preferred_element_type=F32)

    q_ref[...] = proj(0, d_attn)
    k = proj(d_attn, d_attn)
    v = proj(2 * d_attn, d_attn)
    k_ref[...] = k
    v_ref[...] = v
    gate = proj(3 * d_attn, d_conv)
    u = proj(3 * d_attn + d_conv, d_conv) * proj(3 * d_attn + 2 * d_conv, d_conv)

    tm = x.shape[0]
    if sequential:
        kt_ref[...] = k.T
        vt_ref[...] = v.T

        @pl.when(pl.program_id(1) == 0)
        def _():
            carry_ref[...] = jnp.zeros_like(carry_ref)

        row = lax.broadcasted_iota(jnp.int32, u.shape, 0)
        prev1 = carry_ref[1:2, :]
        prev2 = carry_ref[0:1, :]
        u1 = jnp.where(row == 0, prev1, pltpu.roll(u, 1, axis=0))
        u2 = jnp.where(row == 0, prev2, jnp.where(row == 1, prev1, pltpu.roll(u, 2, axis=0)))
        carry_ref[0:2, :] = u[tm - 2:tm, :]
        st_ref[...] = u[tm - 2:tm, :]
    else:
        u_ref[...] = u
        u2 = st0_ref[...]
        u1 = st1_ref[...]
    z = u2 * cw_ref[0:1, :] + u1 * cw_ref[1:2, :] + u * cw_ref[2:3, :]
    oc_ref[...] = _rms(gate * z, gc_ref[...])


def _mix_in_call(kernel, grid, in_specs, out_specs, out_shape, scratch, args):
    return pl.pallas_call(
        kernel, grid=grid, in_specs=in_specs, out_specs=out_specs, out_shape=out_shape,
        scratch_shapes=scratch,
        compiler_params=pltpu.CompilerParams(
            dimension_semantics=("arbitrary",) * len(grid), vmem_limit_bytes=VMEM_LIMIT),
        name="mix_in",
    )(*args)


def _mix_in_prompt(x2d, norm_g, w_in_bf16, conv_w, norm_gc, *, seq_len, d_attn, d_conv):
    t, d = x2d.shape
    tm = min(ROW_TILE, seq_len)
    n_seq, per = t // seq_len, seq_len // tm
    const = lambda b, s: (0, 0)
    row = lambda width: pl.BlockSpec((tm, width), lambda b, s: (b * per + s, 0))
    col = pl.BlockSpec((None, d_attn, tm), lambda b, s: (b, 0, s))
    f32 = lambda *shape: jax.ShapeDtypeStruct(shape, F32)
    return _mix_in_call(
        functools.partial(_mix_in_kernel, d_attn=d_attn, d_conv=d_conv, sequential=True),
        (n_seq, per),
        [row(d), pl.BlockSpec((1, d), const), pl.BlockSpec(w_in_bf16.shape, const),
         pl.BlockSpec((CONV_WIDTH, d_conv), const), pl.BlockSpec((1, d_conv), const)],
        [row(d_attn)] * 3 + [col] * 2 + [row(d_conv),
                                         pl.BlockSpec((None, CONV_WIDTH - 1, d_conv),
                                                      lambda b, s: (b, 0, 0))],
        [f32(t, d_attn)] * 3 + [f32(n_seq, d_attn, seq_len)] * 2
        + [f32(t, d_conv), f32(n_seq, CONV_WIDTH - 1, d_conv)],
        [pltpu.VMEM((8, d_conv), F32)],
        (x2d, norm_g, w_in_bf16, conv_w, norm_gc))


def _mix_in_sample(x2d, norm_g, w_in_bf16, conv_w, norm_gc, st0, st1, *, d_attn, d_conv):
    t, d = x2d.shape
    full = lambda arr: pl.BlockSpec(arr.shape, lambda i: (0,) * arr.ndim)
    f32 = lambda *shape: jax.ShapeDtypeStruct(shape, F32)
    args = (x2d, norm_g, w_in_bf16, conv_w, norm_gc, st0, st1)
    outs = [f32(t, d_attn)] * 3 + [f32(t, d_conv)] * 2
    return _mix_in_call(
        functools.partial(_mix_in_kernel, d_attn=d_attn, d_conv=d_conv, sequential=False),
        (1,), [full(a) for a in args], [full(o) for o in outs], outs, [], args)


def _attn_prompt_kernel(q_ref, k_ref, v_ref, o_ref, m_s, l_s, a_s, *, seq_len):
    w = WIN_KEYS
    scale = HEAD_DIM ** -0.5 * LOG2_E
    r_i = lax.broadcasted_iota(jnp.int32, (2 * w, 2 * w), 0) & (w - 1)
    c_i = lax.broadcasted_iota(jnp.int32, (2 * w, 2 * w), 1)
    mask_cur = (lax.broadcasted_iota(jnp.int32, (2 * w, w), 1)
                <= lax.broadcasted_iota(jnp.int32, (2 * w, w), 0) & (w - 1))
    mask_both = jnp.logical_and(c_i >= r_i, c_i - w <= r_i)
    first_head = lax.broadcasted_iota(jnp.int32, (w, 2 * HEAD_DIM), 1) < HEAD_DIM
    dn_t = (((1,), (1,)), ((), ()))

    def rows(start, dil):
        if dil > 1:
            return pl.ds(start, w, stride=dil)
        return pl.ds(start if isinstance(start, int) else pl.multiple_of(start, w), w)

    def run_branch(dil, first, last):
        span = dil * w
        nb = seq_len // span

        def blocks(its, with_prev):
            mask = mask_both if with_prev else mask_cur
            cur, qs, ks, vs = [], [], [], []
            for it in its:
                g = it % dil
                n = it // dil
                c = rows(g + n * span, dil)
                cur.append(c)
                qb = (q_ref[c, :] * scale).astype(BF16)
                zero = jnp.zeros_like(qb)
                qs.append(jnp.concatenate([jnp.where(first_head, qb, zero),
                                           jnp.where(first_head, zero, qb)], axis=0))
                k = k_ref[c, :].astype(BF16)
                v = v_ref[c, :].astype(BF16)
                if with_prev:
                    p = rows(g + (n - 1) * span, dil)
                    k = jnp.concatenate([k_ref[p, :].astype(BF16), k], axis=0)
                    v = jnp.concatenate([v_ref[p, :].astype(BF16), v], axis=0)
                ks.append(k)
                vs.append(v)
            scores = [lax.dot_general(q, k, dn_t, preferred_element_type=F32)
                      for q, k in zip(qs, ks)]
            ms, ps = [], []
            for s in scores:
                s = jnp.where(mask, s, NEG)
                m = jnp.max(s, axis=-1, keepdims=True)
                ms.append(m)
                ps.append(jnp.exp2(s - m).astype(BF16))
            ones = jnp.ones((ks[0].shape[0], 2 * HEAD_DIM), BF16)
            accs = [jnp.dot(p, jnp.concatenate([v, ones], axis=1), preferred_element_type=F32)
                    for p, v in zip(ps, vs)]
            for c, m, acc_l in zip(cur, ms, accs):
                acc, l = acc_l[:, :2 * HEAD_DIM], acc_l[:, 2 * HEAD_DIM:]
                m_b = jnp.where(first_head, m[:w], m[w:])
                l_b = jnp.where(first_head, l[:w], l[w:])
                a_b = jnp.where(first_head, acc[:w], acc[w:])
                if not first:
                    m_o = m_s[c, :]
                    m_n = jnp.maximum(m_o, m_b)
                    w_o = jnp.exp2(m_o - m_n)
                    w_b = jnp.exp2(m_b - m_n)
                    l_b = w_o * l_s[c, :] + w_b * l_b
                    a_b = w_o * a_s[c, :] + w_b * a_b
                    m_b = m_n
                if last:
                    o_ref[c, :] = a_b / l_b
                else:
                    m_s[c, :] = m_b
                    l_s[c, :] = l_b
                    a_s[c, :] = a_b

        def run(lo, hi, with_prev):
            u = ATTN_UNROLL
            trips = (hi - lo) // u

            def body(t, carry):
                blocks([lo + t * u + j for j in range(u)], with_prev)
                return carry

            if trips:
                lax.fori_loop(0, trips, body, 0)
            if lo + trips * u < hi:
                blocks(list(range(lo + trips * u, hi)), with_prev)

        run(0, dil, False)
        run(dil, dil * nb, True)

    order = sorted(DILATIONS, reverse=True)
    for i, dil in enumerate(order):
        run_branch(dil, i == 0, i == len(order) - 1)


def _attn_prompt(q, k, v, *, n_seq, seq_len):
    t, d_attn = q.shape
    pair = 2 * HEAD_DIM
    spec = pl.BlockSpec((seq_len, pair), lambda b, h: (b, h))
    return pl.pallas_call(
        functools.partial(_attn_prompt_kernel, seq_len=seq_len),
        grid=(n_seq, d_attn // pair),
        in_specs=[spec] * 3,
        out_specs=spec,
        out_shape=jax.ShapeDtypeStruct((t, d_attn), F32),
        scratch_shapes=[pltpu.VMEM((seq_len, pair), F32)] * 3,
        compiler_params=pltpu.CompilerParams(
            dimension_semantics=("arbitrary", "arbitrary"), vmem_limit_bytes=VMEM_LIMIT),
        name="attn_prompt",
    )(q, k, v)


def _attn_sample_kernel(q_ref, kn_ref, vn_ref, kt_ref, vt_ref, o_ref):
    n_heads, dh, w_buf = kt_ref.shape
    delta = w_buf - lax.broadcasted_iota(jnp.int32, (1, w_buf), 1)
    cnt = jnp.zeros((1, w_buf), F32)
    for dil in DILATIONS:
        assert dil & (dil - 1) == 0
        member = jnp.where(delta <= dil * WIN_KEYS, 1.0, 0.0)
        cnt = cnt + jnp.where((delta & (dil - 1)) == 0, member, 0.0)
    eye = (lax.broadcasted_iota(jnp.int32, (dh, dh), 0)
           == lax.broadcasted_iota(jnp.int32, (dh, dh), 1))
    to_col = lambda r: jnp.sum(jnp.where(eye, r, 0.0), axis=1, keepdims=True)
    to_row = lambda c: jnp.sum(jnp.where(eye, c, 0.0), axis=0, keepdims=True)
    outs = []
    for h in range(n_heads):
        sl = slice(h * dh, (h + 1) * dh)
        q = q_ref[:, sl] * (HEAD_DIM ** -0.5)
        s_self = jnp.sum(q * kn_ref[:, sl], axis=1, keepdims=True)
        s = jnp.sum(to_col(q) * kt_ref[h], axis=0, keepdims=True)
        s = jnp.where(cnt > 0.0, s, NEG)
        m = jnp.maximum(jnp.max(s, axis=1, keepdims=True), s_self)
        p = cnt * jnp.exp(s - m)
        p_self = len(DILATIONS) * jnp.exp(s_self - m)
        l = jnp.sum(p, axis=1, keepdims=True) + p_self
        acc = jnp.sum(p * vt_ref[h], axis=1, keepdims=True)
        outs.append((to_row(acc) + p_self * vn_ref[:, sl]) / l)
    o_ref[...] = jnp.concatenate(outs, axis=1)


def _attn_sample(q, k_new, v_new, cache_kt, cache_vt):
    db, n_heads, dh, w_buf = cache_kt.shape
    head_spec = pl.BlockSpec((None, 1, n_heads * dh), lambda b: (b, 0, 0))
    cache_spec = pl.BlockSpec((None, n_heads, dh, w_buf), lambda b: (b, 0, 0, 0))
    return pl.pallas_call(
        _attn_sample_kernel,
        grid=(db,),
        in_specs=[head_spec] * 3 + [cache_spec] * 2,
        out_specs=head_spec,
        out_shape=jax.ShapeDtypeStruct((db, 1, n_heads * dh), F32),
        compiler_params=pltpu.CompilerParams(
            dimension_semantics=("arbitrary",), vmem_limit_bytes=VMEM_LIMIT),
        name="attn_sample",
    )(q, k_new, v_new, cache_kt, cache_vt)


R_E0, R_E1, R_G0, R_G1, R_POS0, R_POS1 = range(6)
ROUTER_LANE0 = N_GROUPS
CHUNK = 8
CHUNKS_PER_BLOCK = EXPERT_BLOCK // CHUNK


def _max_tile_chunks(tm):
    return (2 * tm + (CHUNK - 1) * N_EXPERTS) // CHUNK


def _local_rows(tm):
    return 2 * tm + N_EXPERTS * CHUNK


def _mix_out_kernel(*refs, n_tiles, has_tail):
    if not has_tail:
        _mix_out_tile(*refs)
        return
    *tile_in, tail_ref, h_ref, route_ref, xs_ref, cnt_ref = refs

    @pl.when(pl.program_id(0) < n_tiles)
    def _():
        _mix_out_tile(*tile_in, h_ref, route_ref, xs_ref, cnt_ref)

    @pl.when(pl.program_id(0) == n_tiles)
    def _():
        rows = tail_ref.shape[0]
        xs_ref[0:rows, :] = tail_ref[...]
        xs_ref[rows:, :] = jnp.zeros((xs_ref.shape[0] - rows, xs_ref.shape[1]), F32)


def _mix_out_tile(x_ref, a_ref, oc_ref, ga_ref, wo_ref, gf_ref, wr_ref, br_ref,
                  h_ref, route_ref, xs_ref, cnt_ref):
    d_attn = a_ref.shape[1]
    tm, d = x_ref.shape
    a = _rms(a_ref[...], ga_ref[...]).astype(BF16)
    mix = jnp.dot(a, wo_ref[0:d_attn, :], preferred_element_type=F32)
    mix = mix + jnp.dot(oc_ref[...].astype(BF16), wo_ref[d_attn:, :], preferred_element_type=F32)
    h = x_ref[...] + mix
    h_ref[...] = h
    tok = _rms(h, gf_ref[...])

    tok_hi = tok.astype(BF16)
    tok_lo = (tok - tok_hi.astype(F32)).astype(BF16)
    hi_part = jnp.dot(tok_hi, wr_ref[...], preferred_element_type=F32)
    lo_part = jnp.dot(tok_lo, wr_ref[:, :LANES], preferred_element_type=F32)
    logits = hi_part[:, :LANES] + hi_part[:, LANES:] + lo_part + br_ref[...]
    lane = lax.broadcasted_iota(jnp.int32, logits.shape, 1)
    big = jnp.int32(LANES)
    neg_inf = jnp.float32(-jnp.inf)

    def top1(vals):
        best = jnp.max(vals, axis=-1, keepdims=True)
        idx = jnp.min(jnp.where(vals == best, lane, big), axis=-1, keepdims=True)
        return best, idx

    is_group = lane < N_GROUPS
    lg = jnp.where(is_group, logits, neg_inf)
    mg, g_sel = top1(lg)
    p_group = 1.0 / jnp.sum(jnp.where(is_group, jnp.exp(lg - mg), 0.0), axis=-1, keepdims=True)

    lo = ROUTER_LANE0 + g_sel * EXPERTS_PER_GROUP
    in_group = jnp.logical_and(lane >= lo, lane < lo + EXPERTS_PER_GROUP)
    le = jnp.where(in_group, logits, neg_inf)
    v1, i1 = top1(le)
    v2, i2 = top1(jnp.where(lane == i1, neg_inf, le))
    e2 = jnp.exp(v2 - v1)
    gate1 = p_group / (1.0 + e2)
    gate2 = p_group * e2 / (1.0 + e2)

    oh1 = lane == i1
    oh2 = lane == i2
    both = jnp.where(jnp.logical_or(oh1, oh2), 1.0, 0.0)
    r_i = lax.broadcasted_iota(jnp.int32, (tm, tm), 0)
    c_i = lax.broadcasted_iota(jnp.int32, (tm, tm), 1)
    strict_lower = jnp.where(c_i < r_i, 1.0, 0.0).astype(BF16)
    before = jnp.dot(strict_lower, both.astype(BF16), preferred_element_type=F32)
    chunks = jnp.floor((jnp.sum(both, axis=0, keepdims=True) + (CHUNK - 1)) * (1.0 / CHUNK))
    u_r = lax.broadcasted_iota(jnp.int32, (LANES, LANES), 0)
    u_c = lax.broadcasted_iota(jnp.int32, (LANES, LANES), 1)
    strict_upper = jnp.where(u_r < u_c, 1.0, 0.0).astype(BF16)
    chunks8 = jnp.broadcast_to(chunks, (8, LANES))
    first_row = CHUNK * jnp.dot(chunks8.astype(BF16), strict_upper,
                                preferred_element_type=F32)[0:1, :]
    pos = first_row + before
    pos1 = jnp.sum(jnp.where(oh1, pos, 0.0), axis=-1, keepdims=True)
    pos2 = jnp.sum(jnp.where(oh2, pos, 0.0), axis=-1, keepdims=True)
    cnt_ref[...] = jnp.where(lax.broadcasted_iota(jnp.int32, (8, LANES), 0) == 0, chunks8, 0.0)

    rec = jnp.zeros(logits.shape, F32)
    for col, val in ((R_E0, (i1 - ROUTER_LANE0).astype(F32)), (R_E1, (i2 - ROUTER_LANE0).astype(F32)),
                     (R_G0, gate1), (R_G1, gate2), (R_POS0, pos1), (R_POS1, pos2)):
        rec = jnp.where(lane == col, val, rec)
    route_ref[...] = rec

    rec_t = rec.T
    l1 = rec_t[R_POS0:R_POS0 + 1, :].astype(jnp.int32)
    l2 = rec_t[R_POS1:R_POS1 + 1, :].astype(jnp.int32)
    srow = lax.broadcasted_iota(jnp.int32, (xs_ref.shape[0], tm), 0)
    perm = jnp.where(srow == l1, 1.0, jnp.where(srow == l2, 1.0, 0.0)).astype(BF16)
    xs_ref[...] = jnp.dot(perm, tok_hi, preferred_element_type=F32)


def _mix_out(x2d, attn, oconv, norm_ga, w_out_bf16, norm_gf, w_router, b_router, tail=None):
    t, d = x2d.shape
    d_attn, d_conv = attn.shape[1], oconv.shape[1]
    tm = min(ROW_TILE, t)
    nt = t // tm
    r_l = _local_rows(tm)
    has_tail = tail is not None
    tile = lambda i: jnp.minimum(i, nt - 1)
    row = lambda width: pl.BlockSpec((tm, width), lambda i: (tile(i), 0))
    full = lambda arr: pl.BlockSpec(arr.shape, lambda i: (0, 0))
    args = [x2d, attn, oconv, norm_ga, w_out_bf16, norm_gf, w_router, b_router]
    in_specs = [row(d), row(d_attn), row(d_conv)] + [full(a) for a in args[3:]]
    if has_tail:
        assert tail.shape[0] <= r_l and tail.shape[1] == d
        args.append(tail)
        in_specs.append(full(tail))
    return pl.pallas_call(
        functools.partial(_mix_out_kernel, n_tiles=nt, has_tail=has_tail),
        grid=(nt + has_tail,),
        in_specs=in_specs,
        out_specs=[row(d), row(LANES), pl.BlockSpec((r_l, d), lambda i: (i, 0)),
                   pl.BlockSpec((None, 8, LANES), lambda i: (tile(i), 0, 0))],
        out_shape=[jax.ShapeDtypeStruct((t, d), F32), jax.ShapeDtypeStruct((t, LANES), F32),
                   jax.ShapeDtypeStruct(((nt + has_tail) * r_l, d), F32),
                   jax.ShapeDtypeStruct((nt, 8, LANES), F32)],
        compiler_params=pltpu.CompilerParams(
            dimension_semantics=("arbitrary",), vmem_limit_bytes=VMEM_LIMIT),
        name="mix_out",
    )(*args)


def _sorted_layout(tile_chunks, tile_row0, max_local, n_blocks):
    nt, n_exp = tile_chunks.shape
    cpb = CHUNKS_PER_BLOCK
    i32 = jnp.int32
    seg = jnp.sum(tile_chunks, axis=0)
    padded = (seg + cpb - 1) // cpb * cpb
    pend = jnp.cumsum(padded)
    pstart = pend - padded
    tile_incl = jnp.cumsum(tile_chunks, axis=0)
    tile_excl = tile_incl - tile_chunks
    local_incl = jnp.cumsum(tile_chunks, axis=1)
    local_excl = local_incl - tile_chunks
    base = pstart[None, :] + tile_excl

    block_first = jnp.arange(n_blocks, dtype=i32) * cpb
    block_e = jnp.minimum(jnp.sum((pend[None, :] <= block_first[:, None]).astype(i32), axis=1),
                          n_exp - 1)
    n_used = (pend[-1:] // cpb).astype(i32)

    onehot_pick = lambda onehot, table: jnp.sum(jnp.where(onehot, table, 0), axis=-1)

    is_e = block_e[:, None] == jnp.arange(n_exp, dtype=i32)[None, :]
    of_expert = lambda table_te: onehot_pick(is_e[:, None, :], table_te[None, :, :])
    incl_b, cnt_b, lexcl_b = of_expert(tile_incl), of_expert(tile_chunks), of_expert(local_excl)
    q = (block_first - onehot_pick(is_e, pstart[None, :]))[:, None] + jnp.arange(cpb, dtype=i32)
    tile_q = jnp.minimum(jnp.sum((incl_b[:, None, :] <= q[:, :, None]).astype(i32), axis=2), nt - 1)
    is_t = tile_q[:, :, None] == jnp.arange(nt, dtype=i32)[None, None, :]
    of_tile = lambda table_bt: onehot_pick(is_t, table_bt[:, None, :])
    local_chunk = of_tile(lexcl_b) + q - of_tile(incl_b - cnt_b)
    in_run = jnp.logical_and(q >= 0, q < onehot_pick(is_e, seg[None, :])[:, None])
    src_row = jnp.where(in_run, of_tile(tile_row0[None, :]) + CHUNK * local_chunk, 0)
    src_row = src_row.reshape(-1).astype(i32)

    c = jnp.arange(max_local, dtype=i32)
    e_c = jnp.minimum(jnp.sum((local_incl[:, None, :] <= c[None, :, None]).astype(i32), axis=2),
                      n_exp - 1)
    is_ec = e_c[:, :, None] == jnp.arange(n_exp, dtype=i32)[None, None, :]
    of_run = lambda table_te: onehot_pick(is_ec, table_te[:, None, :])
    global_chunk = of_run(base) + c[None, :] - of_run(local_excl)
    tile_src = jnp.where(c[None, :] < local_incl[:, -1:], CHUNK * global_chunk, 0).astype(i32)
    return block_e.astype(i32), n_used, src_row, tile_src


def _chunk_gather(src_ref, hbm_ref, buf, sems, item, slot, n_chunks, *, wait):
    for c in range(n_chunks):
        row = 0 if wait else pl.multiple_of(src_ref[item * n_chunks + c], CHUNK)
        copy = pltpu.make_async_copy(hbm_ref.at[pl.ds(row, CHUNK)],
                                     buf.at[slot, pl.ds(c * CHUNK, CHUNK)], sems.at[slot])
        if wait:
            copy.wait()
        else:
            copy.start()


def _prefetched(gather, step, n_steps, body):
    slot = step % 2

    @pl.when(step == 0)
    def _():
        gather(0, 0, wait=False)

    gather(jnp.minimum(step + 1, n_steps - 1), 1 - slot, wait=False)
    gather(step, slot, wait=True)
    body(slot)

    @pl.when(step == n_steps - 1)
    def _():
        gather(step, 1 - slot, wait=True)


def _experts_kernel(block_e_ref, n_used_ref, src_ref, xs_ref, wg_ref, wu_ref, wd_ref,
                    y_ref, xblk, sems):
    del block_e_ref
    b = pl.program_id(0)
    gather = functools.partial(_chunk_gather, src_ref, xs_ref, xblk, sems,
                               n_chunks=CHUNKS_PER_BLOCK)

    def body(slot):
        @pl.when(b < n_used_ref[0])
        def _():
            x = xblk[slot].astype(BF16)
            gate = jnp.dot(x, wg_ref[...].astype(BF16), preferred_element_type=F32)
            up = jnp.dot(x, wu_ref[...].astype(BF16), preferred_element_type=F32)
            hid = gate * (1.0 / (1.0 + jnp.exp(-gate))) * up
            y_ref[...] = jnp.dot(hid.astype(BF16), wd_ref[...].astype(BF16),
                                 preferred_element_type=F32)

        @pl.when(b >= n_used_ref[0])
        def _():
            y_ref[...] = jnp.zeros_like(y_ref)

    _prefetched(gather, b, pl.num_programs(0), body)


def _experts(block_e, n_used, src_row, xs, w_gate, w_up, w_down):
    n_blocks = block_e.shape[0]
    _, d, d_exp = w_gate.shape
    blk = EXPERT_BLOCK
    weight = lambda k, n: pl.BlockSpec((None, k, n), lambda b, be, nu, src: (be[b], 0, 0))
    return pl.pallas_call(
        _experts_kernel,
        grid_spec=pltpu.PrefetchScalarGridSpec(
            num_scalar_prefetch=3,
            grid=(n_blocks,),
            in_specs=[pl.BlockSpec(memory_space=pl.ANY),
                      weight(d, d_exp), weight(d, d_exp), weight(d_exp, d)],
            out_specs=pl.BlockSpec((blk, d), lambda b, be, nu, src: (b, 0)),
            scratch_shapes=[pltpu.VMEM((2, blk, d), F32), pltpu.SemaphoreType.DMA((2,))],
        ),
        out_shape=jax.ShapeDtypeStruct((n_blocks * blk, d), F32),
        compiler_params=pltpu.CompilerParams(
            dimension_semantics=("arbitrary",), vmem_limit_bytes=VMEM_LIMIT),
        name="experts",
    )(block_e, n_used, src_row, xs, w_gate, w_up, w_down)


def _combine_kernel(src_ref, h_ref, route_ref, gn_ref, ybuf_ref, o_ref, yloc, sems):
    tm = h_ref.shape[0]
    r_l = yloc.shape[1]
    gather = functools.partial(_chunk_gather, src_ref, ybuf_ref, yloc, sems,
                               n_chunks=r_l // CHUNK)

    def body(slot):
        y = yloc[slot].astype(BF16)
        route = route_ref[...]
        l0 = route[:, R_POS0:R_POS0 + 1].astype(jnp.int32)
        l1 = route[:, R_POS1:R_POS1 + 1].astype(jnp.int32)
        srow = lax.broadcasted_iota(jnp.int32, (tm, r_l), 1)
        gates = jnp.where(srow == l0, route[:, R_G0:R_G0 + 1],
                          jnp.where(srow == l1, route[:, R_G1:R_G1 + 1], 0.0)).astype(BF16)
        f = jnp.dot(gates, y, preferred_element_type=F32)
        o_ref[...] = _rms(h_ref[...] + f, gn_ref[...])

    _prefetched(gather, pl.program_id(0), pl.num_programs(0), body)


def _combine(tile_src, h, route, norm_g, ybuf):
    t, d = h.shape
    tm = min(ROW_TILE, t)
    r_l = _local_rows(tm)
    return pl.pallas_call(
        _combine_kernel,
        grid_spec=pltpu.PrefetchScalarGridSpec(
            num_scalar_prefetch=1,
            grid=(t // tm,),
            in_specs=[pl.BlockSpec((tm, d), lambda i, src: (i, 0)),
                      pl.BlockSpec((tm, LANES), lambda i, src: (i, 0)),
                      pl.BlockSpec((1, d), lambda i, src: (0, 0)),
                      pl.BlockSpec(memory_space=pl.ANY)],
            out_specs=pl.BlockSpec((tm, d), lambda i, src: (i, 0)),
            scratch_shapes=[pltpu.VMEM((2, r_l, d), F32),
                            pltpu.SemaphoreType.DMA((2,))],
        ),
        out_shape=jax.ShapeDtypeStruct((t, d), F32),
        compiler_params=pltpu.CompilerParams(
            dimension_semantics=("arbitrary",), vmem_limit_bytes=VMEM_LIMIT),
        name="combine",
    )(tile_src, h, route, norm_g, ybuf)


def kernel(x_prompt, x_sample, cache_k, cache_v, state_conv, norm_mix, w_in, conv_w, norm_out_attn,
           norm_out_conv, w_out, norm_ffn, w_router_group, b_router_group, w_router_expert,
           b_router_expert, w_gate, w_up, w_down, norm_final):
    n_seq, seq_len, d = x_prompt.shape
    db, ds, _ = x_sample.shape
    depth = w_in.shape[0]
    _, _, w_buf, n_heads, dh = cache_k.shape
    d_attn = n_heads * dh
    d_conv = d - d_attn
    assert depth == 1 and ds == 1 and dh == HEAD_DIM
    assert seq_len % (max(DILATIONS) * WIN_KEYS) == 0 and seq_len <= max(DILATIONS) * WIN_KEYS
    layer = 0
    tp, ts = n_seq * seq_len, db

    xp = x_prompt.reshape(tp, d)
    xs = x_sample.reshape(ts, d)
    row = lambda vec: vec.reshape(1, -1)
    w_in_b = w_in[layer].astype(BF16)
    w_out_b = w_out[layer].astype(BF16)
    g_mix, g_oa, g_oc, g_ffn = (row(norm_mix[layer]), row(norm_out_attn[layer]),
                                row(norm_out_conv[layer]), row(norm_ffn[layer]))
    st0, st1 = state_conv[layer, :, 0, :], state_conv[layer, :, 1, :]

    qp, kp, vp, kp_t, vp_t, ocp, conv_p = _mix_in_prompt(
        xp, g_mix, w_in_b, conv_w[layer], g_oc, seq_len=seq_len, d_attn=d_attn, d_conv=d_conv)
    qs, ks, vs, ocs, us = _mix_in_sample(
        xs, g_mix, w_in_b, conv_w[layer], g_oc, st0, st1, d_attn=d_attn, d_conv=d_conv)

    attn_p = _attn_prompt(qp, kp, vp, n_seq=n_seq, seq_len=seq_len)
    heads = lambda a: a.reshape(ts, 1, d_attn)
    positions_last = lambda c: jnp.transpose(c, (0, 2, 3, 1))
    attn_s = _attn_sample(heads(qs), heads(ks), heads(vs),
                          positions_last(cache_k[layer]), positions_last(cache_v[layer]))
    attn_s = attn_s.reshape(ts, d_attn)

    n_route = N_GROUPS + N_EXPERTS
    w_router = jnp.zeros((d, LANES), F32).at[:, :N_GROUPS].set(w_router_group[layer])
    w_router = w_router.at[:, N_GROUPS:n_route].set(w_router_expert[layer])
    b_router = jnp.zeros((1, LANES), F32).at[0, :N_GROUPS].set(b_router_group[layer])
    b_router = b_router.at[0, N_GROUPS:n_route].set(b_router_expert[layer])
    w_router_hi = w_router.astype(BF16)
    w_router_lo = (w_router - w_router_hi.astype(F32)).astype(BF16)
    mix_out = functools.partial(_mix_out, norm_ga=g_oa, w_out_bf16=w_out_b, norm_gf=g_ffn,
                                w_router=jnp.concatenate([w_router_hi, w_router_lo], axis=1),
                                b_router=b_router)
    h_s, route_s, xs_s, cnt_s = mix_out(xs, attn_s, ocs)
    assert cnt_s.shape[0] == 1
    h_p, route_p, xs_all, cnt_p = mix_out(xp, attn_p, ocp, tail=xs_s)

    tile_chunks = jnp.concatenate([cnt_p[:, 0, ROUTER_LANE0:n_route],
                                   cnt_s[:, 0, ROUTER_LANE0:n_route]], axis=0).astype(jnp.int32)
    ntp, nts = cnt_p.shape[0], cnt_s.shape[0]
    tm_p, tm_s = tp // ntp, ts // nts
    rl_p, rl_s = _local_rows(tm_p), _local_rows(tm_s)
    tile_row0 = jnp.arange(ntp + nts, dtype=jnp.int32) * rl_p
    total_chunks = ntp * _max_tile_chunks(tm_p) + nts * _max_tile_chunks(tm_s)
    n_blocks = -(-(total_chunks + N_EXPERTS * (CHUNKS_PER_BLOCK - 1)) // CHUNKS_PER_BLOCK)
    block_e, n_used, src_row, tile_src = _sorted_layout(tile_chunks, tile_row0, rl_p // CHUNK,
                                                        n_blocks)
    ybuf = _experts(block_e, n_used, src_row, xs_all, w_gate[layer], w_up[layer], w_down[layer])
    g_fin = row(norm_final)
    y_p = _combine(tile_src[:ntp].reshape(-1), h_p, route_p, g_fin, ybuf)
    y_s = _combine(tile_src[ntp:, :rl_s // CHUNK].reshape(-1), h_s, route_s, g_fin, ybuf)

    w_keep = min(max(DILATIONS) * WIN_KEYS, seq_len)
    kv5 = lambda a_t: jnp.transpose(a_t.reshape(n_seq, n_heads, dh, seq_len),
                                    (0, 3, 1, 2))[None, :, seq_len - w_keep:]
    conv_s = jnp.stack([st1, us], axis=1)[None]
    kvs = lambda a: a.reshape(1, ts, 1, n_heads, dh)
    return (y_p.reshape(n_seq, seq_len, d), y_s.reshape(db, ds, d), kv5(kp_t), kv5(vp_t),
            conv_p[None], kvs(ks), kvs(vs), conv_s)
```

```python
import functools

import jax
import jax.numpy as jnp
from jax import lax
from jax.experimental import pallas as pl
from jax.experimental.pallas import tpu as pltpu

HEAD_DIM = 64
WIN_KEYS = 128
DILATIONS = (1, 4, 16)
CONV_WIDTH = 3
N_GROUPS = 4
EXPERTS_PER_GROUP = 8
N_EXPERTS = N_GROUPS * EXPERTS_PER_GROUP
EPS = 1e-6
NEG = -1e30
LOG2_E = 1.4426950408889634

LANES = 128
ROW_TILE = 512
EXPERT_BLOCK = 512
ATTN_UNROLL = 8
VMEM_LIMIT = 56 * 1024 * 1024

F32 = jnp.float32
BF16 = jnp.bfloat16


def _rms(x, g):
    return x * lax.rsqrt(jnp.mean(x * x, axis=-1, keepdims=True) + EPS) * g


def _mix_in_kernel(*refs, d_attn, d_conv, sequential):
    if sequential:
        (x_ref, g_ref, w_ref, cw_ref, gc_ref,
         q_ref, k_ref, v_ref, kt_ref, vt_ref, oc_ref, st_ref, carry_ref) = refs
    else:
        (x_ref, g_ref, w_ref, cw_ref, gc_ref, st0_ref, st1_ref,
         q_ref, k_ref, v_ref, oc_ref, u_ref) = refs
    x = x_ref[...]
    xb = _rms(x, g_ref[...]).astype(BF16)

    def proj(lo, width):
        return jnp.dot(xb, w_ref[:, lo:lo + width], preferred_element_type=F32)

    q_ref[...] = proj(0, d_attn)
    k = proj(d_attn, d_attn)
    v = proj(2 * d_attn, d_attn)
    k_ref[...] = k
    v_ref[...] = v
    gate = proj(3 * d_attn, d_conv)
    u = proj(3 * d_attn + d_conv, d_conv) * proj(3 * d_attn + 2 * d_conv, d_conv)

    tm = x.shape[0]
    if sequential:
        kt_ref[...] = k.T
        vt_ref[...] = v.T

        @pl.when(pl.program_id(1) == 0)
        def _():
            carry_ref[...] = jnp.zeros_like(carry_ref)

        row = lax.broadcasted_iota(jnp.int32, u.shape, 0)
        prev1 = carry_ref[1:2, :]
        prev2 = carry_ref[0:1, :]
        u1 = jnp.where(row == 0, prev1, pltpu.roll(u, 1, axis=0))
        u2 = jnp.where(row == 0, prev2, jnp.where(row == 1, prev1, pltpu.roll(u, 2, axis=0)))
        carry_ref[0:2, :] = u[tm - 2:tm, :]
        st_ref[...] = u[tm - 2:tm, :]
    else:
        u_ref[...] = u
        u2 = st0_ref[...]
        u1 = st1_ref[...]
    z = u2 * cw_ref[0:1, :] + u1 * cw_ref[1:2, :] + u * cw_ref[2:3, :]
    oc_ref[...] = _rms(gate * z, gc_ref[...])


def _mix_in_call(kernel, grid, in_specs, out_specs, out_shape, scratch, args):
    return pl.pallas_call(
        kernel, grid=grid, in_specs=in_specs, out_specs=out_specs, out_shape=out_shape,
        scratch_shapes=scratch,
        compiler_params=pltpu.CompilerParams(
            dimension_semantics=("arbitrary",) * len(grid), vmem_limit_bytes=VMEM_LIMIT),
        name="mix_in",
    )(*args)


def _mix_in_prompt(x2d, norm_g, w_in_bf16, conv_w, norm_gc, *, seq_len, d_attn, d_conv):
    t, d = x2d.shape
    tm = min(ROW_TILE, seq_len)
    n_seq, per = t // seq_len, seq_len // tm
    const = lambda b, s: (0, 0)
    row = lambda width: pl.BlockSpec((tm, width), lambda b, s: (b * per + s, 0))
    col = pl.BlockSpec((None, d_attn, tm), lambda b, s: (b, 0, s))
    f32 = lambda *shape: jax.ShapeDtypeStruct(shape, F32)
    return _mix_in_call(
        functools.partial(_mix_in_kernel, d_attn=d_attn, d_conv=d_conv, sequential=True),
        (n_seq, per),
        [row(d), pl.BlockSpec((1, d), const), pl.BlockSpec(w_in_bf16.shape, const),
         pl.BlockSpec((CONV_WIDTH, d_conv), const), pl.BlockSpec((1, d_conv), const)],
        [row(d_attn)] * 3 + [col] * 2 + [row(d_conv),
                                         pl.BlockSpec((None, CONV_WIDTH - 1, d_conv),
                                                      lambda b, s: (b, 0, 0))],
        [f32(t, d_attn)] * 3 + [f32(n_seq, d_attn, seq_len)] * 2
        + [f32(t, d_conv), f32(n_seq, CONV_WIDTH - 1, d_conv)],
        [pltpu.VMEM((8, d_conv), F32)],
        (x2d, norm_g, w_in_bf16, conv_w, norm_gc))


def _mix_in_sample(x2d, norm_g, w_in_bf16, conv_w, norm_gc, st0, st1, *, d_attn, d_conv):
    t, d = x2d.shape
    full = lambda arr: pl.BlockSpec(arr.shape, lambda i: (0,) * arr.ndim)
    f32 = lambda *shape: jax.ShapeDtypeStruct(shape, F32)
    args = (x2d, norm_g, w_in_bf16, conv_w, norm_gc, st0, st1)
    outs = [f32(t, d_attn)] * 3 + [f32(t, d_conv)] * 2
    return _mix_in_call(
        functools.partial(_mix_in_kernel, d_attn=d_attn, d_conv=d_conv, sequential=False),
        (1,), [full(a) for a in args], [full(o) for o in outs], outs, [], args)


def _attn_prompt_kernel(q_ref, k_ref, v_ref, o_ref, m_s, l_s, a_s, *, seq_len):
    w = WIN_KEYS
    scale = HEAD_DIM ** -0.5 * LOG2_E
    r_i = lax.broadcasted_iota(jnp.int32, (2 * w, 2 * w), 0) & (w - 1)
    c_i = lax.broadcasted_iota(jnp.int32, (2 * w, 2 * w), 1)
    mask_cur = (lax.broadcasted_iota(jnp.int32, (2 * w, w), 1)
                <= lax.broadcasted_iota(jnp.int32, (2 * w, w), 0) & (w - 1))
    mask_both = jnp.logical_and(c_i >= r_i, c_i - w <= r_i)
    first_head = lax.broadcasted_iota(jnp.int32, (w, 2 * HEAD_DIM), 1) < HEAD_DIM
    dn_t = (((1,), (1,)), ((), ()))

    def rows(start, dil):
        if dil > 1:
            return pl.ds(start, w, stride=dil)
        return pl.ds(start if isinstance(start, int) else pl.multiple_of(start, w), w)

    def run_branch(dil, first, last):
        span = dil * w
        nb = seq_len // span

        def blocks(its, with_prev):
            mask = mask_both if with_prev else mask_cur
            cur, qs, ks, vs = [], [], [], []
            for it in its:
                g = it % dil
                n = it // dil
                c = rows(g + n * span, dil)
                cur.append(c)
                qb = (q_ref[c, :] * scale).astype(BF16)
                zero = jnp.zeros_like(qb)
                qs.append(jnp.concatenate([jnp.where(first_head, qb, zero),
                                           jnp.where(first_head, zero, qb)], axis=0))
                k = k_ref[c, :].astype(BF16)
                v = v_ref[c, :].astype(BF16)
                if with_prev:
                    p = rows(g + (n - 1) * span, dil)
                    k = jnp.concatenate([k_ref[p, :].astype(BF16), k], axis=0)
                    v = jnp.concatenate([v_ref[p, :].astype(BF16), v], axis=0)
                ks.append(k)
                vs.append(v)
            scores = [lax.dot_general(q, k, dn_t, preferred_element_type=F32)
                      for q, k in zip(qs, ks)]
            ms, ps = [], []
            for s in scores:
                s = jnp.where(mask, s, NEG)
                m = jnp.max(s, axis=-1, keepdims=True)
                ms.append(m)
                ps.append(jnp.exp2(s - m).astype(BF16))
            ones = jnp.ones((ks[0].shape[0], 2 * HEAD_DIM), BF16)
            accs = [jnp.dot(p, jnp.concatenate([v, ones], axis=1), preferred_element_type=F32)
                    for p, v in zip(ps, vs)]
            for c, m, acc_l in zip(cur, ms, accs):
                acc, l = acc_l[:, :2 * HEAD_DIM], acc_l[:, 2 * HEAD_DIM:]
                m_b = jnp.where(first_head, m[:w], m[w:])
                l_b = jnp.where(first_head, l[:w], l[w:])
                a_b = jnp.where(first_head, acc[:w], acc[w:])
                if not first:
                    m_o = m_s[c, :]
                    m_n = jnp.maximum(m_o, m_b)
                    w_o = jnp.exp2(m_o - m_n)
                    w_b = jnp.exp2(m_b - m_n)
                    l_b = w_o * l_s[c, :] + w_b * l_b
                    a_b = w_o * a_s[c, :] + w_b * a_b
                    m_b = m_n
                if last:
                    o_ref[c, :] = a_b / l_b
                else:
                    m_s[c, :] = m_b
                    l_s[c, :] = l_b
                    a_s[c, :] = a_b

        def run(lo, hi, with_prev):
            u = ATTN_UNROLL
            trips = (hi - lo) // u

            def body(t, carry):
                blocks([lo + t * u + j for j in range(u)], with_prev)
                return carry

            if trips:
                lax.fori_loop(0, trips, body, 0)
            if lo + trips * u < hi:
                blocks(list(range(lo + trips * u, hi)), with_prev)

        run(0, dil, False)
        run(dil, dil * nb, True)

    order = sorted(DILATIONS, reverse=True)
    for i, dil in enumerate(order):
        run_branch(dil, i == 0, i == len(order) - 1)


def _attn_prompt(q, k, v, *, n_seq, seq_len):
    t, d_attn = q.shape
    pair = 2 * HEAD_DIM
    spec = pl.BlockSpec((seq_len, pair), lambda b, h: (b, h))
    return pl.pallas_call(
        functools.partial(_attn_prompt_kernel, seq_len=seq_len),
        grid=(n_seq, d_attn // pair),
        in_specs=[spec] * 3,
        out_specs=spec,
        out_shape=jax.ShapeDtypeStruct((t, d_attn), F32),
        scratch_shapes=[pltpu.VMEM((seq_len, pair), F32)] * 3,
        compiler_params=pltpu.CompilerParams(
            dimension_semantics=("arbitrary", "arbitrary"), vmem_limit_bytes=VMEM_LIMIT),
        name="attn_prompt",
    )(q, k, v)


def _attn_sample_kernel(q_ref, kn_ref, vn_ref, kt_ref, vt_ref, o_ref):
    n_heads, dh, w_buf = kt_ref.shape
    delta = w_buf - lax.broadcasted_iota(jnp.int32, (1, w_buf), 1)
    cnt = jnp.zeros((1, w_buf), F32)
    for dil in DILATIONS:
        assert dil & (dil - 1) == 0
        member = jnp.where(delta <= dil * WIN_KEYS, 1.0, 0.0)
        cnt = cnt + jnp.where((delta & (dil - 1)) == 0, member, 0.0)
    eye = (lax.broadcasted_iota(jnp.int32, (dh, dh), 0)
           == lax.broadcasted_iota(jnp.int32, (dh, dh), 1))
    to_col = lambda r: jnp.sum(jnp.where(eye, r, 0.0), axis=1, keepdims=True)
    to_row = lambda c: jnp.sum(jnp.where(eye, c, 0.0), axis=0, keepdims=True)
    outs = []
    for h in range(n_heads):
        sl = slice(h * dh, (h + 1) * dh)
        q = q_ref[:, sl] * (HEAD_DIM ** -0.5)
        s_self = jnp.sum(q * kn_ref[:, sl], axis=1, keepdims=True)
        s = jnp.sum(to_col(q) * kt_ref[h], axis=0, keepdims=True)
        s = jnp.where(cnt > 0.0, s, NEG)
        m = jnp.maximum(jnp.max(s, axis=1, keepdims=True), s_self)
        p = cnt * jnp.exp(s - m)
        p_self = len(DILATIONS) * jnp.exp(s_self - m)
        l = jnp.sum(p, axis=1, keepdims=True) + p_self
        acc = jnp.sum(p * vt_ref[h], axis=1, keepdims=True)
        outs.append((to_row(acc) + p_self * vn_ref[:, sl]) / l)
    o_ref[...] = jnp.concatenate(outs, axis=1)


def _attn_sample(q, k_new, v_new, cache_kt, cache_vt):
    db, n_heads, dh, w_buf = cache_kt.shape
    head_spec = pl.BlockSpec((None, 1, n_heads * dh), lambda b: (b, 0, 0))
    cache_spec = pl.BlockSpec((None, n_heads, dh, w_buf), lambda b: (b, 0, 0, 0))
    return pl.pallas_call(
        _attn_sample_kernel,
        grid=(db,),
        in_specs=[head_spec] * 3 + [cache_spec] * 2,
        out_specs=head_spec,
        out_shape=jax.ShapeDtypeStruct((db, 1, n_heads * dh), F32),
        compiler_params=pltpu.CompilerParams(
            dimension_semantics=("arbitrary",), vmem_limit_bytes=VMEM_LIMIT),
        name="attn_sample",
    )(q, k_new, v_new, cache_kt, cache_vt)


R_E0, R_E1, R_G0, R_G1, R_POS0, R_POS1 = range(6)
ROUTER_LANE0 = N_GROUPS
CHUNK = 8
CHUNKS_PER_BLOCK = EXPERT_BLOCK // CHUNK


def _max_tile_chunks(tm):
    return (2 * tm + (CHUNK - 1) * N_EXPERTS) // CHUNK


def _local_rows(tm):
    return 2 * tm + N_EXPERTS * CHUNK


def _mix_out_kernel(*refs, n_tiles, has_tail):
    if not has_tail:
        _mix_out_tile(*refs)
        return
    *tile_in, tail_ref, h_ref, route_ref, xs_ref, cnt_ref = refs

    @pl.when(pl.program_id(0) < n_tiles)
    def _():
        _mix_out_tile(*tile_in, h_ref, route_ref, xs_ref, cnt_ref)

    @pl.when(pl.program_id(0) == n_tiles)
    def _():
        rows = tail_ref.shape[0]
        xs_ref[0:rows, :] = tail_ref[...]
        xs_ref[rows:, :] = jnp.zeros((xs_ref.shape[0] - rows, xs_ref.shape[1]), F32)


def _mix_out_tile(x_ref, a_ref, oc_ref, ga_ref, wo_ref, gf_ref, wr_ref, br_ref,
                  h_ref, route_ref, xs_ref, cnt_ref):
    d_attn = a_ref.shape[1]
    tm, d = x_ref.shape
    a = _rms(a_ref[...], ga_ref[...]).astype(BF16)
    mix = jnp.dot(a, wo_ref[0:d_attn, :], preferred_element_type=F32)
    mix = mix + jnp.dot(oc_ref[...].astype(BF16), wo_ref[d_attn:, :], preferred_element_type=F32)
    h = x_ref[...] + mix
    h_ref[...] = h
    tok = _rms(h, gf_ref[...])

    tok_hi = tok.astype(BF16)
    tok_lo = (tok - tok_hi.astype(F32)).astype(BF16)
    hi_part = jnp.dot(tok_hi, wr_ref[...], preferred_element_type=F32)
    lo_part = jnp.dot(tok_lo, wr_ref[:, :LANES], preferred_element_type=F32)
    logits = hi_part[:, :LANES] + hi_part[:, LANES:] + lo_part + br_ref[...]
    lane = lax.broadcasted_iota(jnp.int32, logits.shape, 1)
    big = jnp.int32(LANES)
    neg_inf = jnp.float32(-jnp.inf)

    def top1(vals):
        best = jnp.max(vals, axis=-1, keepdims=True)
        idx = jnp.min(jnp.where(vals == best, lane, big), axis=-1, keepdims=True)
        return best, idx

    is_group = lane < N_GROUPS
    lg = jnp.where(is_group, logits, neg_inf)
    mg, g_sel = top1(lg)
    p_group = 1.0 / jnp.sum(jnp.where(is_group, jnp.exp(lg - mg), 0.0), axis=-1, keepdims=True)

    lo = ROUTER_LANE0 + g_sel * EXPERTS_PER_GROUP
    in_group = jnp.logical_and(lane >= lo, lane < lo + EXPERTS_PER_GROUP)
    le = jnp.where(in_group, logits, neg_inf)
    v1, i1 = top1(le)
    v2, i2 = top1(jnp.where(lane == i1, neg_inf, le))
    e2 = jnp.exp(v2 - v1)
    gate1 = p_group / (1.0 + e2)
    gate2 = p_group * e2 / (1.0 + e2)

    oh1 = lane == i1
    oh2 = lane == i2
    both = jnp.where(jnp.logical_or(oh1, oh2), 1.0, 0.0)
    r_i = lax.broadcasted_iota(jnp.int32, (tm, tm), 0)
    c_i = lax.broadcasted_iota(jnp.int32, (tm, tm), 1)
    strict_lower = jnp.where(c_i < r_i, 1.0, 0.0).astype(BF16)
    before = jnp.dot(strict_lower, both.astype(BF16), preferred_element_type=F32)
    chunks = jnp.floor((jnp.sum(both, axis=0, keepdims=True) + (CHUNK - 1)) * (1.0 / CHUNK))
    u_r = lax.broadcasted_iota(jnp.int32, (LANES, LANES), 0)
    u_c = lax.broadcasted_iota(jnp.int32, (LANES, LANES), 1)
    strict_upper = jnp.where(u_r < u_c, 1.0, 0.0).astype(BF16)
    chunks8 = jnp.broadcast_to(chunks, (8, LANES))
    first_row = CHUNK * jnp.dot(chunks8.astype(BF16), strict_upper,
                                preferred_element_type=F32)[0:1, :]
    pos = first_row + before
    pos1 = jnp.sum(jnp.where(oh1, pos, 0.0), axis=-1, keepdims=True)
    pos2 = jnp.sum(jnp.where(oh2, pos, 0.0), axis=-1, keepdims=True)
    cnt_ref[...] = jnp.where(lax.broadcasted_iota(jnp.int32, (8, LANES), 0) == 0, chunks8, 0.0)

    rec = jnp.zeros(logits.shape, F32)
    for col, val in ((R_E0, (i1 - ROUTER_LANE0).astype(F32)), (R_E1, (i2 - ROUTER_LANE0).astype(F32)),
                     (R_G0, gate1), (R_G1, gate2), (R_POS0, pos1), (R_POS1, pos2)):
        rec = jnp.where(lane == col, val, rec)
    route_ref[...] = rec

    rec_t = rec.T
    l1 = rec_t[R_POS0:R_POS0 + 1, :].astype(jnp.int32)
    l2 = rec_t[R_POS1:R_POS1 + 1, :].astype(jnp.int32)
    srow = lax.broadcasted_iota(jnp.int32, (xs_ref.shape[0], tm), 0)
    perm = jnp.where(srow == l1, 1.0, jnp.where(srow == l2, 1.0, 0.0)).astype(BF16)
    xs_ref[...] = jnp.dot(perm, tok_hi, preferred_element_type=F32)


def _mix_out(x2d, attn, oconv, norm_ga, w_out_bf16, norm_gf, w_router, b_router, tail=None):
    t, d = x2d.shape
    d_attn, d_conv = attn.shape[1], oconv.shape[1]
    tm = min(ROW_TILE, t)
    nt = t // tm
    r_l = _local_rows(tm)
    has_tail = tail is not None
    tile = lambda i: jnp.minimum(i, nt - 1)
    row = lambda width: pl.BlockSpec((tm, width), lambda i: (tile(i), 0))
    full = lambda arr: pl.BlockSpec(arr.shape, lambda i: (0, 0))
    args = [x2d, attn, oconv, norm_ga, w_out_bf16, norm_gf, w_router, b_router]
    in_specs = [row(d), row(d_attn), row(d_conv)] + [full(a) for a in args[3:]]
    if has_tail:
        assert tail.shape[0] <= r_l and tail.shape[1] == d
        args.append(tail)
        in_specs.append(full(tail))
    return pl.pallas_call(
        functools.partial(_mix_out_kernel, n_tiles=nt, has_tail=has_tail),
        grid=(nt + has_tail,),
        in_specs=in_specs,
        out_specs=[row(d), row(LANES), pl.BlockSpec((r_l, d), lambda i: (i, 0)),
                   pl.BlockSpec((None, 8, LANES), lambda i: (tile(i), 0, 0))],
        out_shape=[jax.ShapeDtypeStruct((t, d), F32), jax.ShapeDtypeStruct((t, LANES), F32),
                   jax.ShapeDtypeStruct(((nt + has_tail) * r_l, d), F32),
                   jax.ShapeDtypeStruct((nt, 8, LANES), F32)],
        compiler_params=pltpu.CompilerParams(
            dimension_semantics=("arbitrary",), vmem_limit_bytes=VMEM_LIMIT),
        name="mix_out",
    )(*args)


def _sorted_layout(tile_chunks, tile_row0, max_local, n_blocks):
    nt, n_exp = tile_chunks.shape
    cpb = CHUNKS_PER_BLOCK
    i32 = jnp.int32
    seg = jnp.sum(tile_chunks, axis=0)
    padded = (seg + cpb - 1) // cpb * cpb
    pend = jnp.cumsum(padded)
    pstart = pend - padded
    tile_incl = jnp.cumsum(tile_chunks, axis=0)
    tile_excl = tile_incl - tile_chunks
    local_incl = jnp.cumsum(tile_chunks, axis=1)
    local_excl = local_incl - tile_chunks
    base = pstart[None, :] + tile_excl

    block_first = jnp.arange(n_blocks, dtype=i32) * cpb
    block_e = jnp.minimum(jnp.sum((pend[None, :] <= block_first[:, None]).astype(i32), axis=1),
                          n_exp - 1)
    n_used = (pend[-1:] // cpb).astype(i32)

    onehot_pick = lambda onehot, table: jnp.sum(jnp.where(onehot, table, 0), axis=-1)

    is_e = block_e[:, None] == jnp.arange(n_exp, dtype=i32)[None, :]
    of_expert = lambda table_te: onehot_pick(is_e[:, None, :], table_te[None, :, :])
    incl_b, cnt_b, lexcl_b = of_expert(tile_incl), of_expert(tile_chunks), of_expert(local_excl)
    q = (block_first - onehot_pick(is_e, pstart[None, :]))[:, None] + jnp.arange(cpb, dtype=i32)
    tile_q = jnp.minimum(jnp.sum((incl_b[:, None, :] <= q[:, :, None]).astype(i32), axis=2), nt - 1)
    is_t = tile_q[:, :, None] == jnp.arange(nt, dtype=i32)[None, None, :]
    of_tile = lambda table_bt: onehot_pick(is_t, table_bt[:, None, :])
    local_chunk = of_tile(lexcl_b) + q - of_tile(incl_b - cnt_b)
    in_run = jnp.logical_and(q >= 0, q < onehot_pick(is_e, seg[None, :])[:, None])
    src_row = jnp.where(in_run, of_tile(tile_row0[None, :]) + CHUNK * local_chunk, 0)
    src_row = src_row.reshape(-1).astype(i32)

    c = jnp.arange(max_local, dtype=i32)
    e_c = jnp.minimum(jnp.sum((local_incl[:, None, :] <= c[None, :, None]).astype(i32), axis=2),
                      n_exp - 1)
    is_ec = e_c[:, :, None] == jnp.arange(n_exp, dtype=i32)[None, None, :]
    of_run = lambda table_te: onehot_pick(is_ec, table_te[:, None, :])
    global_chunk = of_run(base) + c[None, :] - of_run(local_excl)
    tile_src = jnp.where(c[None, :] < local_incl[:, -1:], CHUNK * global_chunk, 0).astype(i32)
    return block_e.astype(i32), n_used, src_row, tile_src


def _chunk_gather(src_ref, hbm_ref, buf, sems, item, slot, n_chunks, *, wait):
    for c in range(n_chunks):
        row = 0 if wait else pl.multiple_of(src_ref[item * n_chunks + c], CHUNK)
        copy = pltpu.make_async_copy(hbm_ref.at[pl.ds(row, CHUNK)],
                                     buf.at[slot, pl.ds(c * CHUNK, CHUNK)], sems.at[slot])
        if wait:
            copy.wait()
        else:
            copy.start()


def _prefetched(gather, step, n_items, body):
    slot = step % 2

    @pl.when(jnp.logical_and(step == 0, n_items > 0))
    def _():
        gather(0, 0, wait=False)

    @pl.when(step + 1 < n_items)
    def _():
        gather(step + 1, 1 - slot, wait=False)

    body(slot, lambda: gather(step, slot, wait=True))


def _experts_kernel(block_e_ref, n_used_ref, src_ref, xs_ref, wg_ref, wu_ref, wd_ref,
                    y_ref, xblk, sems):
    del block_e_ref
    b = pl.program_id(0)
    gather = functools.partial(_chunk_gather, src_ref, xs_ref, xblk, sems,
                               n_chunks=CHUNKS_PER_BLOCK)

    def body(slot, wait_current):
        @pl.when(b < n_used_ref[0])
        def _():
            wait_current()
            x = xblk[slot].astype(BF16)
            gate = jnp.dot(x, wg_ref[...].astype(BF16), preferred_element_type=F32)
            up = jnp.dot(x, wu_ref[...].astype(BF16), preferred_element_type=F32)
            hid = gate * (1.0 / (1.0 + jnp.exp(-gate))) * up
            y_ref[...] = jnp.dot(hid.astype(BF16), wd_ref[...].astype(BF16),
                                 preferred_element_type=F32)

        @pl.when(b >= n_used_ref[0])
        def _():
            y_ref[...] = jnp.zeros_like(y_ref)

    _prefetched(gather, b, n_used_ref[0], body)


def _experts(block_e, n_used, src_row, xs, w_gate, w_up, w_down):
    n_blocks = block_e.shape[0]
    _, d, d_exp = w_gate.shape
    blk = EXPERT_BLOCK
    weight = lambda k, n: pl.BlockSpec((None, k, n), lambda b, be, nu, src: (be[b], 0, 0))
    return pl.pallas_call(
        _experts_kernel,
        grid_spec=pltpu.PrefetchScalarGridSpec(
            num_scalar_prefetch=3,
            grid=(n_blocks,),
            in_specs=[pl.BlockSpec(memory_space=pl.ANY),
                      weight(d, d_exp), weight(d, d_exp), weight(d_exp, d)],
            out_specs=pl.BlockSpec((blk, d), lambda b, be, nu, src: (b, 0)),
            scratch_shapes=[pltpu.VMEM((2, blk, d), F32), pltpu.SemaphoreType.DMA((2,))],
        ),
        out_shape=jax.ShapeDtypeStruct((n_blocks * blk, d), F32),
        compiler_params=pltpu.CompilerParams(
            dimension_semantics=("arbitrary",), vmem_limit_bytes=VMEM_LIMIT),
        name="experts",
    )(block_e, n_used, src_row, xs, w_gate, w_up, w_down)


def _combine_kernel(src_ref, h_ref, route_ref, gn_ref, ybuf_ref, o_ref, yloc, sems):
    tm = h_ref.shape[0]
    r_l = yloc.shape[1]
    gather = functools.partial(_chunk_gather, src_ref, ybuf_ref, yloc, sems,
                               n_chunks=r_l // CHUNK)

    def body(slot, wait_current):
        wait_current()
        y = yloc[slot].astype(BF16)
        route = route_ref[...]
        l0 = route[:, R_POS0:R_POS0 + 1].astype(jnp.int32)
        l1 = route[:, R_POS1:R_POS1 + 1].astype(jnp.int32)
        srow = lax.broadcasted_iota(jnp.int32, (tm, r_l), 1)
        gates = jnp.where(srow == l0, route[:, R_G0:R_G0 + 1],
                          jnp.where(srow == l1, route[:, R_G1:R_G1 + 1], 0.0)).astype(BF16)
        f = jnp.dot(gates, y, preferred_element_type=F32)
        o_ref[...] = _rms(h_ref[...] + f, gn_ref[...])

    _prefetched(gather, pl.program_id(0), pl.num_programs(0), body)


def _combine(tile_src, h, route, norm_g, ybuf):
    t, d = h.shape
    tm = min(ROW_TILE, t)
    r_l = _local_rows(tm)
    return pl.pallas_call(
        _combine_kernel,
        grid_spec=pltpu.PrefetchScalarGridSpec(
            num_scalar_prefetch=1,
            grid=(t // tm,),
            in_specs=[pl.BlockSpec((tm, d), lambda i, src: (i, 0)),
                      pl.BlockSpec((tm, LANES), lambda i, src: (i, 0)),
                      pl.BlockSpec((1, d), lambda i, src: (0, 0)),
                      pl.BlockSpec(memory_space=pl.ANY)],
            out_specs=pl.BlockSpec((tm, d), lambda i, src: (i, 0)),
            scratch_shapes=[pltpu.VMEM((2, r_l, d), F32),
                            pltpu.SemaphoreType.DMA((2,))],
        ),
        out_shape=jax.ShapeDtypeStruct((t, d), F32),
        compiler_params=pltpu.CompilerParams(
            dimension_semantics=("arbitrary",), vmem_limit_bytes=VMEM_LIMIT),
        name="combine",
    )(tile_src, h, route, norm_g, ybuf)


def kernel(x_prompt, x_sample, cache_k, cache_v, state_conv, norm_mix, w_in, conv_w, norm_out_attn,
           norm_out_conv, w_out, norm_ffn, w_router_group, b_router_group, w_router_expert,
           b_router_expert, w_gate, w_up, w_down, norm_final):
    n_seq, seq_len, d = x_prompt.shape
    db, ds, _ = x_sample.shape
    depth = w_in.shape[0]
    _, _, w_buf, n_heads, dh = cache_k.shape
    d_attn = n_heads * dh
    d_conv = d - d_attn
    assert depth == 1 and ds == 1 and dh == HEAD_DIM
    assert seq_len % (max(DILATIONS) * WIN_KEYS) == 0 and seq_len <= max(DILATIONS) * WIN_KEYS
    layer = 0
    tp, ts = n_seq * seq_len, db

    xp = x_prompt.reshape(tp, d)
    xs = x_sample.reshape(ts, d)
    row = lambda vec: vec.reshape(1, -1)
    w_in_b = w_in[layer].astype(BF16)
    w_out_b = w_out[layer].astype(BF16)
    g_mix, g_oa, g_oc, g_ffn = (row(norm_mix[layer]), row(norm_out_attn[layer]),
                                row(norm_out_conv[layer]), row(norm_ffn[layer]))
    st0, st1 = state_conv[layer, :, 0, :], state_conv[layer, :, 1, :]

    qp, kp, vp, kp_t, vp_t, ocp, conv_p = _mix_in_prompt(
        xp, g_mix, w_in_b, conv_w[layer], g_oc, seq_len=seq_len, d_attn=d_attn, d_conv=d_conv)
    qs, ks, vs, ocs, us = _mix_in_sample(
        xs, g_mix, w_in_b, conv_w[layer], g_oc, st0, st1, d_attn=d_attn, d_conv=d_conv)

    attn_p = _attn_prompt(qp, kp, vp, n_seq=n_seq, seq_len=seq_len)
    heads = lambda a: a.reshape(ts, 1, d_attn)
    positions_last = lambda c: jnp.transpose(c, (0, 2, 3, 1))
    attn_s = _attn_sample(heads(qs), heads(ks), heads(vs),
                          positions_last(cache_k[layer]), positions_last(cache_v[layer]))
    attn_s = attn_s.reshape(ts, d_attn)

    n_route = N_GROUPS + N_EXPERTS
    w_router = jnp.zeros((d, LANES), F32).at[:, :N_GROUPS].set(w_router_group[layer])
    w_router = w_router.at[:, N_GROUPS:n_route].set(w_router_expert[layer])
    b_router = jnp.zeros((1, LANES), F32).at[0, :N_GROUPS].set(b_router_group[layer])
    b_router = b_router.at[0, N_GROUPS:n_route].set(b_router_expert[layer])
    w_router_hi = w_router.astype(BF16)
    w_router_lo = (w_router - w_router_hi.astype(F32)).astype(BF16)
    mix_out = functools.partial(_mix_out, norm_ga=g_oa, w_out_bf16=w_out_b, norm_gf=g_ffn,
                                w_router=jnp.concatenate([w_router_hi, w_router_lo], axis=1),
                                b_router=b_router)
    h_s, route_s, xs_s, cnt_s = mix_out(xs, attn_s, ocs)
    assert cnt_s.shape[0] == 1
    h_p, route_p, xs_all, cnt_p = mix_out(xp, attn_p, ocp, tail=xs_s)

    tile_chunks = jnp.concatenate([cnt_p[:, 0, ROUTER_LANE0:n_route],
                                   cnt_s[:, 0, ROUTER_LANE0:n_route]], axis=0).astype(jnp.int32)
    ntp, nts = cnt_p.shape[0], cnt_s.shape[0]
    tm_p, tm_s = tp // ntp, ts // nts
    rl_p, rl_s = _local_rows(tm_p), _local_rows(tm_s)
    tile_row0 = jnp.arange(ntp + nts, dtype=jnp.int32) * rl_p
    total_chunks = ntp * _max_tile_chunks(tm_p) + nts * _max_tile_chunks(tm_s)
    n_blocks = -(-(total_chunks + N_EXPERTS * (CHUNKS_PER_BLOCK - 1)) // CHUNKS_PER_BLOCK)
    block_e, n_used, src_row, tile_src = _sorted_layout(tile_chunks, tile_row0, rl_p // CHUNK,
                                                        n_blocks)
    ybuf = _experts(block_e, n_used, src_row, xs_all, w_gate[layer], w_up[layer], w_down[layer])
    g_fin = row(norm_final)
    y_p = _combine(tile_src[:ntp].reshape(-1), h_p, route_p, g_fin, ybuf)
    y_s = _combine(tile_src[ntp:, :rl_s // CHUNK].reshape(-1), h_s, route_s, g_fin, ybuf)

    w_keep = min(max(DILATIONS) * WIN_KEYS, seq_len)
    kv5 = lambda a_t: jnp.transpose(a_t.reshape(n_seq, n_heads, dh, seq_len),
                                    (0, 3, 1, 2))[None, :, seq_len - w_keep:]
    conv_s = jnp.stack([st1, us], axis=1)[None]
    kvs = lambda a: a.reshape(1, ts, 1, n_heads, dh)
    return (y_p.reshape(n_seq, seq_len, d), y_s.reshape(db, ds, d), kv5(kp_t), kv5(vp_t),
            conv_p[None], kvs(ks), kvs(vs), conv_s)
```

```python
import functools

import jax
import jax.numpy as jnp
from jax import lax
from jax.experimental import pallas as pl
from jax.experimental.pallas import tpu as pltpu

HEAD_DIM = 64
WIN_KEYS = 128
DILATIONS = (1, 4, 16)
CONV_WIDTH = 3
N_GROUPS = 4
EXPERTS_PER_GROUP = 8
N_EXPERTS = N_GROUPS * EXPERTS_PER_GROUP
EPS = 1e-6
NEG = -1e30
LOG2_E = 1.4426950408889634

LANES = 128
ROW_TILE = 512
EXPERT_BLOCK = 256
ATTN_UNROLL = 8
VMEM_LIMIT = 56 * 1024 * 1024

F32 = jnp.float32
BF16 = jnp.bfloat16


def _rms(x, g):
    return x * lax.rsqrt(jnp.mean(x * x, axis=-1, keepdims=True) + EPS) * g


def _mix_in_kernel(*refs, d_attn, d_conv, sequential):
    if sequential:
        (x_ref, g_ref, w_ref, cw_ref, gc_ref,
         q_ref, k_ref, v_ref, kt_ref, vt_ref, oc_ref, st_ref, carry_ref) = refs
    else:
        (x_ref, g_ref, w_ref, cw_ref, gc_ref, st0_ref, st1_ref,
         q_ref, k_ref, v_ref, oc_ref, u_ref) = refs
    x = x_ref[...]
    xb = _rms(x, g_ref[...]).astype(BF16)

    def proj(lo, width):
        return jnp.dot(xb, w_ref[:, lo:lo + width], preferred_element_type=F32)

    q_ref[...] = proj(0, d_attn)
    k = proj(d_attn, d_attn)
    v = proj(2 * d_attn, d_attn)
    k_ref[...] = k
    v_ref[...] = v
    gate = proj(3 * d_attn, d_conv)
    u = proj(3 * d_attn + d_conv, d_conv) * proj(3 * d_attn + 2 * d_conv, d_conv)

    tm = x.shape[0]
    if sequential:
        kt_ref[...] = k.T
        vt_ref[...] = v.T

        @pl.when(pl.program_id(1) == 0)
        def _():
            carry_ref[...] = jnp.zeros_like(carry_ref)

        row = lax.broadcasted_iota(jnp.int32, u.shape, 0)
        prev1 = carry_ref[1:2, :]
        prev2 = carry_ref[0:1, :]
        u1 = jnp.where(row == 0, prev1, pltpu.roll(u, 1, axis=0))
        u2 = jnp.where(row == 0, prev2, jnp.where(row == 1, prev1, pltpu.roll(u, 2, axis=0)))
        carry_ref[0:2, :] = u[tm - 2:tm, :]
        st_ref[...] = u[tm - 2:tm, :]
    else:
        u_ref[...] = u
        u2 = st0_ref[...]
        u1 = st1_ref[...]
    z = u2 * cw_ref[0:1, :] + u1 * cw_ref[1:2, :] + u * cw_ref[2:3, :]
    oc_ref[...] = _rms(gate * z, gc_ref[...])


def _mix_in_call(kernel, grid, in_specs, out_specs, out_shape, scratch, args):
    return pl.pallas_call(
        kernel, grid=grid, in_specs=in_specs, out_specs=out_specs, out_shape=out_shape,
        scratch_shapes=scratch,
        compiler_params=pltpu.CompilerParams(
            dimension_semantics=("arbitrary",) * len(grid), vmem_limit_bytes=VMEM_LIMIT),
        name="mix_in",
    )(*args)


def _mix_in_prompt(x2d, norm_g, w_in_bf16, conv_w, norm_gc, *, seq_len, d_attn, d_conv):
    t, d = x2d.shape
    tm = min(ROW_TILE, seq_len)
    n_seq, per = t // seq_len, seq_len // tm
    const = lambda b, s: (0, 0)
    row = lambda width: pl.BlockSpec((tm, width), lambda b, s: (b * per + s, 0))
    col = pl.BlockSpec((None, d_attn, tm), lambda b, s: (b, 0, s))
    f32 = lambda *shape: jax.ShapeDtypeStruct(shape, F32)
    return _mix_in_call(
        functools.partial(_mix_in_kernel, d_attn=d_attn, d_conv=d_conv, sequential=True),
        (n_seq, per),
        [row(d), pl.BlockSpec((1, d), const), pl.BlockSpec(w_in_bf16.shape, const),
         pl.BlockSpec((CONV_WIDTH, d_conv), const), pl.BlockSpec((1, d_conv), const)],
        [row(d_attn)] * 3 + [col] * 2 + [row(d_conv),
                                         pl.BlockSpec((None, CONV_WIDTH - 1, d_conv),
                                                      lambda b, s: (b, 0, 0))],
        [f32(t, d_attn)] * 3 + [f32(n_seq, d_attn, seq_len)] * 2
        + [f32(t, d_conv), f32(n_seq, CONV_WIDTH - 1, d_conv)],
        [pltpu.VMEM((8, d_conv), F32)],
        (x2d, norm_g, w_in_bf16, conv_w, norm_gc))


def _mix_in_sample(x2d, norm_g, w_in_bf16, conv_w, norm_gc, st0, st1, *, d_attn, d_conv):
    t, d = x2d.shape
    full = lambda arr: pl.BlockSpec(arr.shape, lambda i: (0,) * arr.ndim)
    f32 = lambda *shape: jax.ShapeDtypeStruct(shape, F32)
    args = (x2d, norm_g, w_in_bf16, conv_w, norm_gc, st0, st1)
    outs = [f32(t, d_attn)] * 3 + [f32(t, d_conv)] * 2
    return _mix_in_call(
        functools.partial(_mix_in_kernel, d_attn=d_attn, d_conv=d_conv, sequential=False),
        (1,), [full(a) for a in args], [full(o) for o in outs], outs, [], args)


def _attn_prompt_kernel(q_ref, k_ref, v_ref, o_ref, m_s, l_s, a_s, *, seq_len):
    w = WIN_KEYS
    scale = HEAD_DIM ** -0.5 * LOG2_E
    r_i = lax.broadcasted_iota(jnp.int32, (2 * w, 2 * w), 0) & (w - 1)
    c_i = lax.broadcasted_iota(jnp.int32, (2 * w, 2 * w), 1)
    mask_cur = (lax.broadcasted_iota(jnp.int32, (2 * w, w), 1)
                <= lax.broadcasted_iota(jnp.int32, (2 * w, w), 0) & (w - 1))
    mask_both = jnp.logical_and(c_i >= r_i, c_i - w <= r_i)
    first_head = lax.broadcasted_iota(jnp.int32, (w, 2 * HEAD_DIM), 1) < HEAD_DIM
    dn_t = (((1,), (1,)), ((), ()))

    def rows(start, dil):
        if dil > 1:
            return pl.ds(start, w, stride=dil)
        return pl.ds(start if isinstance(start, int) else pl.multiple_of(start, w), w)

    def run_branch(dil, first, last):
        span = dil * w
        nb = seq_len // span

        def blocks(its, with_prev):
            mask = mask_both if with_prev else mask_cur
            cur, qs, ks, vs = [], [], [], []
            for it in its:
                g = it % dil
                n = it // dil
                c = rows(g + n * span, dil)
                cur.append(c)
                qb = (q_ref[c, :] * scale).astype(BF16)
                zero = jnp.zeros_like(qb)
                qs.append(jnp.concatenate([jnp.where(first_head, qb, zero),
                                           jnp.where(first_head, zero, qb)], axis=0))
                k = k_ref[c, :].astype(BF16)
                v = v_ref[c, :].astype(BF16)
                if with_prev:
                    p = rows(g + (n - 1) * span, dil)
                    k = jnp.concatenate([k_ref[p, :].astype(BF16), k], axis=0)
                    v = jnp.concatenate([v_ref[p, :].astype(BF16), v], axis=0)
                ks.append(k)
                vs.append(v)
            scores = [lax.dot_general(q, k, dn_t, preferred_element_type=F32)
                      for q, k in zip(qs, ks)]
            ms, ps = [], []
            for s in scores:
                s = jnp.where(mask, s, NEG)
                m = jnp.max(s, axis=-1, keepdims=True)
                ms.append(m)
                ps.append(jnp.exp2(s - m).astype(BF16))
            ones = jnp.ones((ks[0].shape[0], 2 * HEAD_DIM), BF16)
            accs = [jnp.dot(p, jnp.concatenate([v, ones], axis=1), preferred_element_type=F32)
                    for p, v in zip(ps, vs)]
            for c, m, acc_l in zip(cur, ms, accs):
                acc, l = acc_l[:, :2 * HEAD_DIM], acc_l[:, 2 * HEAD_DIM:]
                m_b = jnp.where(first_head, m[:w], m[w:])
                l_b = jnp.where(first_head, l[:w], l[w:])
                a_b = jnp.where(first_head, acc[:w], acc[w:])
                if not first:
                    m_o = m_s[c, :]
                    m_n = jnp.maximum(m_o, m_b)
                    w_o = jnp.exp2(m_o - m_n)
                    w_b = jnp.exp2(m_b - m_n)
                    l_b = w_o * l_s[c, :] + w_b * l_b
                    a_b = w_o * a_s[c, :] + w_b * a_b
                    m_b = m_n
                if last:
                    o_ref[c, :] = a_b / l_b
                else:
                    m_s[c, :] = m_b
                    l_s[c, :] = l_b
                    a_s[c, :] = a_b

        def run(lo, hi, with_prev):
            u = ATTN_UNROLL
            trips = (hi - lo) // u

            def body(t, carry):
                blocks([lo + t * u + j for j in range(u)], with_prev)
                return carry

            if trips:
                lax.fori_loop(0, trips, body, 0)
            if lo + trips * u < hi:
                blocks(list(range(lo + trips * u, hi)), with_prev)

        run(0, dil, False)
        run(dil, dil * nb, True)

    order = sorted(DILATIONS, reverse=True)
    for i, dil in enumerate(order):
        run_branch(dil, i == 0, i == len(order) - 1)


def _attn_prompt(q, k, v, *, n_seq, seq_len):
    t, d_attn = q.shape
    pair = 2 * HEAD_DIM
    spec = pl.BlockSpec((seq_len, pair), lambda b, h: (b, h))
    return pl.pallas_call(
        functools.partial(_attn_prompt_kernel, seq_len=seq_len),
        grid=(n_seq, d_attn // pair),
        in_specs=[spec] * 3,
        out_specs=spec,
        out_shape=jax.ShapeDtypeStruct((t, d_attn), F32),
        scratch_shapes=[pltpu.VMEM((seq_len, pair), F32)] * 3,
        compiler_params=pltpu.CompilerParams(
            dimension_semantics=("arbitrary", "arbitrary"), vmem_limit_bytes=VMEM_LIMIT),
        name="attn_prompt",
    )(q, k, v)


def _attn_sample_kernel(q_ref, kn_ref, vn_ref, kt_ref, vt_ref, o_ref):
    n_heads, dh, w_buf = kt_ref.shape
    delta = w_buf - lax.broadcasted_iota(jnp.int32, (1, w_buf), 1)
    cnt = jnp.zeros((1, w_buf), F32)
    for dil in DILATIONS:
        assert dil & (dil - 1) == 0
        member = jnp.where(delta <= dil * WIN_KEYS, 1.0, 0.0)
        cnt = cnt + jnp.where((delta & (dil - 1)) == 0, member, 0.0)
    eye = (lax.broadcasted_iota(jnp.int32, (dh, dh), 0)
           == lax.broadcasted_iota(jnp.int32, (dh, dh), 1))
    to_col = lambda r: jnp.sum(jnp.where(eye, r, 0.0), axis=1, keepdims=True)
    to_row = lambda c: jnp.sum(jnp.where(eye, c, 0.0), axis=0, keepdims=True)
    outs = []
    for h in range(n_heads):
        sl = slice(h * dh, (h + 1) * dh)
        q = q_ref[:, sl] * (HEAD_DIM ** -0.5)
        s_self = jnp.sum(q * kn_ref[:, sl], axis=1, keepdims=True)
        s = jnp.sum(to_col(q) * kt_ref[h], axis=0, keepdims=True)
        s = jnp.where(cnt > 0.0, s, NEG)
        m = jnp.maximum(jnp.max(s, axis=1, keepdims=True), s_self)
        p = cnt * jnp.exp(s - m)
        p_self = len(DILATIONS) * jnp.exp(s_self - m)
        l = jnp.sum(p, axis=1, keepdims=True) + p_self
        acc = jnp.sum(p * vt_ref[h], axis=1, keepdims=True)
        outs.append((to_row(acc) + p_self * vn_ref[:, sl]) / l)
    o_ref[...] = jnp.concatenate(outs, axis=1)


def _attn_sample(q, k_new, v_new, cache_kt, cache_vt):
    db, n_heads, dh, w_buf = cache_kt.shape
    head_spec = pl.BlockSpec((None, 1, n_heads * dh), lambda b: (b, 0, 0))
    cache_spec = pl.BlockSpec((None, n_heads, dh, w_buf), lambda b: (b, 0, 0, 0))
    return pl.pallas_call(
        _attn_sample_kernel,
        grid=(db,),
        in_specs=[head_spec] * 3 + [cache_spec] * 2,
        out_specs=head_spec,
        out_shape=jax.ShapeDtypeStruct((db, 1, n_heads * dh), F32),
        compiler_params=pltpu.CompilerParams(
            dimension_semantics=("arbitrary",), vmem_limit_bytes=VMEM_LIMIT),
        name="attn_sample",
    )(q, k_new, v_new, cache_kt, cache_vt)


R_E0, R_E1, R_G0, R_G1, R_POS0, R_POS1 = range(6)
ROUTER_LANE0 = N_GROUPS
CHUNK = 16
CHUNKS_PER_BLOCK = EXPERT_BLOCK // CHUNK


def _max_tile_chunks(tm):
    return (2 * tm + (CHUNK - 1) * N_EXPERTS) // CHUNK


def _local_rows(tm):
    return 2 * tm + N_EXPERTS * CHUNK


def _mix_out_kernel(*refs, n_tiles, has_tail):
    if not has_tail:
        _mix_out_tile(*refs)
        return
    *tile_in, tail_ref, h_ref, route_ref, xs_ref, cnt_ref = refs

    @pl.when(pl.program_id(0) < n_tiles)
    def _():
        _mix_out_tile(*tile_in, h_ref, route_ref, xs_ref, cnt_ref)

    @pl.when(pl.program_id(0) == n_tiles)
    def _():
        rows = tail_ref.shape[0]
        xs_ref[0:rows, :] = tail_ref[...]
        xs_ref[rows:, :] = jnp.zeros((xs_ref.shape[0] - rows, xs_ref.shape[1]), xs_ref.dtype)


def _mix_out_tile(x_ref, a_ref, oc_ref, ga_ref, wo_ref, gf_ref, wr_ref, br_ref,
                  h_ref, route_ref, xs_ref, cnt_ref):
    d_attn = a_ref.shape[1]
    tm, d = x_ref.shape
    a = _rms(a_ref[...], ga_ref[...]).astype(BF16)
    mix = jnp.dot(a, wo_ref[0:d_attn, :], preferred_element_type=F32)
    mix = mix + jnp.dot(oc_ref[...].astype(BF16), wo_ref[d_attn:, :], preferred_element_type=F32)
    h = x_ref[...] + mix
    h_ref[...] = h
    tok = _rms(h, gf_ref[...])

    tok_hi = tok.astype(BF16)
    tok_lo = (tok - tok_hi.astype(F32)).astype(BF16)
    hi_part = jnp.dot(tok_hi, wr_ref[...], preferred_element_type=F32)
    lo_part = jnp.dot(tok_lo, wr_ref[:, :LANES], preferred_element_type=F32)
    logits = hi_part[:, :LANES] + hi_part[:, LANES:] + lo_part + br_ref[...]
    lane = lax.broadcasted_iota(jnp.int32, logits.shape, 1)
    big = jnp.int32(LANES)
    neg_inf = jnp.float32(-jnp.inf)

    def top1(vals):
        best = jnp.max(vals, axis=-1, keepdims=True)
        idx = jnp.min(jnp.where(vals == best, lane, big), axis=-1, keepdims=True)
        return best, idx

    is_group = lane < N_GROUPS
    lg = jnp.where(is_group, logits, neg_inf)
    mg, g_sel = top1(lg)
    p_group = 1.0 / jnp.sum(jnp.where(is_group, jnp.exp(lg - mg), 0.0), axis=-1, keepdims=True)

    lo = ROUTER_LANE0 + g_sel * EXPERTS_PER_GROUP
    in_group = jnp.logical_and(lane >= lo, lane < lo + EXPERTS_PER_GROUP)
    le = jnp.where(in_group, logits, neg_inf)
    v1, i1 = top1(le)
    v2, i2 = top1(jnp.where(lane == i1, neg_inf, le))
    e2 = jnp.exp(v2 - v1)
    gate1 = p_group / (1.0 + e2)
    gate2 = p_group * e2 / (1.0 + e2)

    oh1 = lane == i1
    oh2 = lane == i2
    both = jnp.where(jnp.logical_or(oh1, oh2), 1.0, 0.0)
    r_i = lax.broadcasted_iota(jnp.int32, (tm, tm), 0)
    c_i = lax.broadcasted_iota(jnp.int32, (tm, tm), 1)
    strict_lower = jnp.where(c_i < r_i, 1.0, 0.0).astype(BF16)
    before = jnp.dot(strict_lower, both.astype(BF16), preferred_element_type=F32)
    chunks = jnp.floor((jnp.sum(both, axis=0, keepdims=True) + (CHUNK - 1)) * (1.0 / CHUNK))
    u_r = lax.broadcasted_iota(jnp.int32, (LANES, LANES), 0)
    u_c = lax.broadcasted_iota(jnp.int32, (LANES, LANES), 1)
    strict_upper = jnp.where(u_r < u_c, 1.0, 0.0).astype(BF16)
    chunks8 = jnp.broadcast_to(chunks, (8, LANES))
    first_row = CHUNK * jnp.dot(chunks8.astype(BF16), strict_upper,
                                preferred_element_type=F32)[0:1, :]
    pos = first_row + before
    pos1 = jnp.sum(jnp.where(oh1, pos, 0.0), axis=-1, keepdims=True)
    pos2 = jnp.sum(jnp.where(oh2, pos, 0.0), axis=-1, keepdims=True)
    cnt_ref[...] = jnp.where(lax.broadcasted_iota(jnp.int32, (8, LANES), 0) == 0, chunks8, 0.0)

    rec = jnp.zeros(logits.shape, F32)
    for col, val in ((R_E0, (i1 - ROUTER_LANE0).astype(F32)), (R_E1, (i2 - ROUTER_LANE0).astype(F32)),
                     (R_G0, gate1), (R_G1, gate2), (R_POS0, pos1), (R_POS1, pos2)):
        rec = jnp.where(lane == col, val, rec)
    route_ref[...] = rec

    rec_t = rec.T
    l1 = rec_t[R_POS0:R_POS0 + 1, :].astype(jnp.int32)
    l2 = rec_t[R_POS1:R_POS1 + 1, :].astype(jnp.int32)
    srow = lax.broadcasted_iota(jnp.int32, (xs_ref.shape[0], tm), 0)
    perm = jnp.where(srow == l1, 1.0, jnp.where(srow == l2, 1.0, 0.0)).astype(BF16)
    xs_ref[...] = jnp.dot(perm, tok_hi, preferred_element_type=F32).astype(BF16)


def _mix_out(x2d, attn, oconv, norm_ga, w_out_bf16, norm_gf, w_router, b_router, tail=None):
    t, d = x2d.shape
    d_attn, d_conv = attn.shape[1], oconv.shape[1]
    tm = min(ROW_TILE, t)
    nt = t // tm
    r_l = _local_rows(tm)
    has_tail = tail is not None
    tile = lambda i: jnp.minimum(i, nt - 1)
    row = lambda width: pl.BlockSpec((tm, width), lambda i: (tile(i), 0))
    full = lambda arr: pl.BlockSpec(arr.shape, lambda i: (0, 0))
    args = [x2d, attn, oconv, norm_ga, w_out_bf16, norm_gf, w_router, b_router]
    in_specs = [row(d), row(d_attn), row(d_conv)] + [full(a) for a in args[3:]]
    if has_tail:
        assert tail.shape[0] <= r_l and tail.shape[1] == d
        args.append(tail)
        in_specs.append(full(tail))
    return pl.pallas_call(
        functools.partial(_mix_out_kernel, n_tiles=nt, has_tail=has_tail),
        grid=(nt + has_tail,),
        in_specs=in_specs,
        out_specs=[row(d), row(LANES), pl.BlockSpec((r_l, d), lambda i: (i, 0)),
                   pl.BlockSpec((None, 8, LANES), lambda i: (tile(i), 0, 0))],
        out_shape=[jax.ShapeDtypeStruct((t, d), F32), jax.ShapeDtypeStruct((t, LANES), F32),
                   jax.ShapeDtypeStruct(((nt + has_tail) * r_l, d), BF16),
                   jax.ShapeDtypeStruct((nt, 8, LANES), F32)],
        compiler_params=pltpu.CompilerParams(
            dimension_semantics=("arbitrary",), vmem_limit_bytes=VMEM_LIMIT),
        name="mix_out",
    )(*args)


def _sorted_layout(tile_chunks, tile_row0, max_local, n_blocks):
    nt, n_exp = tile_chunks.shape
    cpb = CHUNKS_PER_BLOCK
    i32 = jnp.int32
    seg = jnp.sum(tile_chunks, axis=0)
    padded = (seg + cpb - 1) // cpb * cpb
    pend = jnp.cumsum(padded)
    pstart = pend - padded
    tile_incl = jnp.cumsum(tile_chunks, axis=0)
    tile_excl = tile_incl - tile_chunks
    local_incl = jnp.cumsum(tile_chunks, axis=1)
    local_excl = local_incl - tile_chunks
    base = pstart[None, :] + tile_excl

    block_first = jnp.arange(n_blocks, dtype=i32) * cpb
    block_e = jnp.minimum(jnp.sum((pend[None, :] <= block_first[:, None]).astype(i32), axis=1),
                          n_exp - 1)
    n_used = (pend[-1:] // cpb).astype(i32)

    onehot_pick = lambda onehot, table: jnp.sum(jnp.where(onehot, table, 0), axis=-1)

    is_e = block_e[:, None] == jnp.arange(n_exp, dtype=i32)[None, :]
    of_expert = lambda table_te: onehot_pick(is_e[:, None, :], table_te[None, :, :])
    incl_b, cnt_b, lexcl_b = of_expert(tile_incl), of_expert(tile_chunks), of_expert(local_excl)
    q = (block_first - onehot_pick(is_e, pstart[None, :]))[:, None] + jnp.arange(cpb, dtype=i32)
    tile_q = jnp.minimum(jnp.sum((incl_b[:, None, :] <= q[:, :, None]).astype(i32), axis=2), nt - 1)
    is_t = tile_q[:, :, None] == jnp.arange(nt, dtype=i32)[None, None, :]
    of_tile = lambda table_bt: onehot_pick(is_t, table_bt[:, None, :])
    local_chunk = of_tile(lexcl_b) + q - of_tile(incl_b - cnt_b)
    in_run = jnp.logical_and(q >= 0, q < onehot_pick(is_e, seg[None, :])[:, None])
    src_row = jnp.where(in_run, of_tile(tile_row0[None, :]) + CHUNK * local_chunk, 0)
    src_row = src_row.reshape(-1).astype(i32)

    c = jnp.arange(max_local, dtype=i32)
    e_c = jnp.minimum(jnp.sum((local_incl[:, None, :] <= c[None, :, None]).astype(i32), axis=2),
                      n_exp - 1)
    is_ec = e_c[:, :, None] == jnp.arange(n_exp, dtype=i32)[None, None, :]
    of_run = lambda table_te: onehot_pick(is_ec, table_te[:, None, :])
    global_chunk = of_run(base) + c[None, :] - of_run(local_excl)
    tile_src = jnp.where(c[None, :] < local_incl[:, -1:], CHUNK * global_chunk, 0).astype(i32)
    return block_e.astype(i32), n_used, src_row, tile_src


def _chunk_gather(src_ref, hbm_ref, buf, sems, item, slot, n_chunks, *, wait):
    for c in range(n_chunks):
        row = 0 if wait else pl.multiple_of(src_ref[item * n_chunks + c], CHUNK)
        copy = pltpu.make_async_copy(hbm_ref.at[pl.ds(row, CHUNK)],
                                     buf.at[slot, pl.ds(c * CHUNK, CHUNK)], sems.at[slot])
        if wait:
            copy.wait()
        else:
            copy.start()


def _prefetched(gather, step, n_items, body):
    slot = step % 2

    @pl.when(jnp.logical_and(step == 0, n_items > 0))
    def _():
        gather(0, 0, wait=False)

    @pl.when(step + 1 < n_items)
    def _():
        gather(step + 1, 1 - slot, wait=False)

    body(slot, lambda: gather(step, slot, wait=True))


def _experts_kernel(block_e_ref, n_used_ref, src_ref, xs_ref, wg_ref, wu_ref, wd_ref,
                    y_ref, xblk, sems):
    del block_e_ref
    b = pl.program_id(0)
    gather = functools.partial(_chunk_gather, src_ref, xs_ref, xblk, sems,
                               n_chunks=CHUNKS_PER_BLOCK)

    def body(slot, wait_current):
        @pl.when(b < n_used_ref[0])
        def _():
            wait_current()
            x = xblk[slot]
            gate = jnp.dot(x, wg_ref[...].astype(BF16), preferred_element_type=F32)
            up = jnp.dot(x, wu_ref[...].astype(BF16), preferred_element_type=F32)
            hid = gate * (1.0 / (1.0 + jnp.exp(-gate))) * up
            y_ref[...] = jnp.dot(hid.astype(BF16), wd_ref[...].astype(BF16),
                                 preferred_element_type=F32).astype(BF16)

        @pl.when(b >= n_used_ref[0])
        def _():
            y_ref[...] = jnp.zeros_like(y_ref)

    _prefetched(gather, b, n_used_ref[0], body)


def _experts(block_e, n_used, src_row, xs, w_gate, w_up, w_down):
    n_blocks = block_e.shape[0]
    _, d, d_exp = w_gate.shape
    blk = EXPERT_BLOCK
    weight = lambda k, n: pl.BlockSpec((None, k, n), lambda b, be, nu, src: (be[b], 0, 0))
    return pl.pallas_call(
        _experts_kernel,
        grid_spec=pltpu.PrefetchScalarGridSpec(
            num_scalar_prefetch=3,
            grid=(n_blocks,),
            in_specs=[pl.BlockSpec(memory_space=pl.ANY),
                      weight(d, d_exp), weight(d, d_exp), weight(d_exp, d)],
            out_specs=pl.BlockSpec((blk, d), lambda b, be, nu, src: (b, 0)),
            scratch_shapes=[pltpu.VMEM((2, blk, d), BF16), pltpu.SemaphoreType.DMA((2,))],
        ),
        out_shape=jax.ShapeDtypeStruct((n_blocks * blk, d), BF16),
        compiler_params=pltpu.CompilerParams(
            dimension_semantics=("arbitrary",), vmem_limit_bytes=VMEM_LIMIT),
        name="experts",
    )(block_e, n_used, src_row, xs, w_gate, w_up, w_down)


def _combine_kernel(src_ref, h_ref, route_ref, gn_ref, ybuf_ref, o_ref, yloc, sems):
    tm = h_ref.shape[0]
    r_l = yloc.shape[1]
    gather = functools.partial(_chunk_gather, src_ref, ybuf_ref, yloc, sems,
                               n_chunks=r_l // CHUNK)

    def body(slot, wait_current):
        wait_current()
        y = yloc[slot]
        route = route_ref[...]
        l0 = route[:, R_POS0:R_POS0 + 1].astype(jnp.int32)
        l1 = route[:, R_POS1:R_POS1 + 1].astype(jnp.int32)
        srow = lax.broadcasted_iota(jnp.int32, (tm, r_l), 1)
        gates = jnp.where(srow == l0, route[:, R_G0:R_G0 + 1],
                          jnp.where(srow == l1, route[:, R_G1:R_G1 + 1], 0.0)).astype(BF16)
        f = jnp.dot(gates, y, preferred_element_type=F32)
        o_ref[...] = _rms(h_ref[...] + f, gn_ref[...])

    _prefetched(gather, pl.program_id(0), pl.num_programs(0), body)


def _combine(tile_src, h, route, norm_g, ybuf):
    t, d = h.shape
    tm = min(ROW_TILE, t)
    r_l = _local_rows(tm)
    return pl.pallas_call(
        _combine_kernel,
        grid_spec=pltpu.PrefetchScalarGridSpec(
            num_scalar_prefetch=1,
            grid=(t // tm,),
            in_specs=[pl.BlockSpec((tm, d), lambda i, src: (i, 0)),
                      pl.BlockSpec((tm, LANES), lambda i, src: (i, 0)),
                      pl.BlockSpec((1, d), lambda i, src: (0, 0)),
                      pl.BlockSpec(memory_space=pl.ANY)],
            out_specs=pl.BlockSpec((tm, d), lambda i, src: (i, 0)),
            scratch_shapes=[pltpu.VMEM((2, r_l, d), BF16),
                            pltpu.SemaphoreType.DMA((2,))],
        ),
        out_shape=jax.ShapeDtypeStruct((t, d), F32),
        compiler_params=pltpu.CompilerParams(
            dimension_semantics=("arbitrary",), vmem_limit_bytes=VMEM_LIMIT),
        name="combine",
    )(tile_src, h, route, norm_g, ybuf)


def kernel(x_prompt, x_sample, cache_k, cache_v, state_conv, norm_mix, w_in, conv_w, norm_out_attn,
           norm_out_conv, w_out, norm_ffn, w_router_group, b_router_group, w_router_expert,
           b_router_expert, w_gate, w_up, w_down, norm_final):
    n_seq, seq_len, d = x_prompt.shape
    db, ds, _ = x_sample.shape
    depth = w_in.shape[0]
    _, _, w_buf, n_heads, dh = cache_k.shape
    d_attn = n_heads * dh
    d_conv = d - d_attn
    assert depth == 1 and ds == 1 and dh == HEAD_DIM
    assert seq_len % (max(DILATIONS) * WIN_KEYS) == 0 and seq_len <= max(DILATIONS) * WIN_KEYS
    layer = 0
    tp, ts = n_seq * seq_len, db

    xp = x_prompt.reshape(tp, d)
    xs = x_sample.reshape(ts, d)
    row = lambda vec: vec.reshape(1, -1)
    w_in_b = w_in[layer].astype(BF16)
    w_out_b = w_out[layer].astype(BF16)
    g_mix, g_oa, g_oc, g_ffn = (row(norm_mix[layer]), row(norm_out_attn[layer]),
                                row(norm_out_conv[layer]), row(norm_ffn[layer]))
    st0, st1 = state_conv[layer, :, 0, :], state_conv[layer, :, 1, :]

    qp, kp, vp, kp_t, vp_t, ocp, conv_p = _mix_in_prompt(
        xp, g_mix, w_in_b, conv_w[layer], g_oc, seq_len=seq_len, d_attn=d_attn, d_conv=d_conv)
    qs, ks, vs, ocs, us = _mix_in_sample(
        xs, g_mix, w_in_b, conv_w[layer], g_oc, st0, st1, d_attn=d_attn, d_conv=d_conv)

    attn_p = _attn_prompt(qp, kp, vp, n_seq=n_seq, seq_len=seq_len)
    heads = lambda a: a.reshape(ts, 1, d_attn)
    positions_last = lambda c: jnp.transpose(c, (0, 2, 3, 1))
    attn_s = _attn_sample(heads(qs), heads(ks), heads(vs),
                          positions_last(cache_k[layer]), positions_last(cache_v[layer]))
    attn_s = attn_s.reshape(ts, d_attn)

    n_route = N_GROUPS + N_EXPERTS
    w_router = jnp.zeros((d, LANES), F32).at[:, :N_GROUPS].set(w_router_group[layer])
    w_router = w_router.at[:, N_GROUPS:n_route].set(w_router_expert[layer])
    b_router = jnp.zeros((1, LANES), F32).at[0, :N_GROUPS].set(b_router_group[layer])
    b_router = b_router.at[0, N_GROUPS:n_route].set(b_router_expert[layer])
    w_router_hi = w_router.astype(BF16)
    w_router_lo = (w_router - w_router_hi.astype(F32)).astype(BF16)
    mix_out = functools.partial(_mix_out, norm_ga=g_oa, w_out_bf16=w_out_b, norm_gf=g_ffn,
                                w_router=jnp.concatenate([w_router_hi, w_router_lo], axis=1),
                                b_router=b_router)
    h_s, route_s, xs_s, cnt_s = mix_out(xs, attn_s, ocs)
    assert cnt_s.shape[0] == 1
    h_p, route_p, xs_all, cnt_p = mix_out(xp, attn_p, ocp, tail=xs_s)

    tile_chunks = jnp.concatenate([cnt_p[:, 0, ROUTER_LANE0:n_route],
                                   cnt_s[:, 0, ROUTER_LANE0:n_route]], axis=0).astype(jnp.int32)
    ntp, nts = cnt_p.shape[0], cnt_s.shape[0]
    tm_p, tm_s = tp // ntp, ts // nts
    rl_p, rl_s = _local_rows(tm_p), _local_rows(tm_s)
    tile_row0 = jnp.arange(ntp + nts, dtype=jnp.int32) * rl_p
    total_chunks = ntp * _max_tile_chunks(tm_p) + nts * _max_tile_chunks(tm_s)
    n_blocks = -(-(total_chunks + N_EXPERTS * (CHUNKS_PER_BLOCK - 1)) // CHUNKS_PER_BLOCK)
    block_e, n_used, src_row, tile_src = _sorted_layout(tile_chunks, tile_row0, rl_p // CHUNK,
                                                        n_blocks)
    ybuf = _experts(block_e, n_used, src_row, xs_all, w_gate[layer], w_up[layer], w_down[layer])
    g_fin = row(norm_final)
    y_p = _combine(tile_src[:ntp].reshape(-1), h_p, route_p, g_fin, ybuf)
    y_s = _combine(tile_src[ntp:, :rl_s // CHUNK].reshape(-1), h_s, route_s, g_fin, ybuf)

    w_keep = min(max(DILATIONS) * WIN_KEYS, seq_len)
    kv5 = lambda a_t: jnp.transpose(a_t.reshape(n_seq, n_heads, dh, seq_len),
                                    (0, 3, 1, 2))[None, :, seq_len - w_keep:]
    conv_s = jnp.stack([st1, us], axis=1)[None]
    kvs = lambda a: a.reshape(1, ts, 1, n_heads, dh)
    return (y_p.reshape(n_seq, seq_len, d), y_s.reshape(db, ds, d), kv5(kp_t), kv5(vp_t),
            conv_p[None], kvs(ks), kvs(vs), conv_s)
```

```python
import functools

import jax
import jax.numpy as jnp
from jax import lax
from jax.experimental import pallas as pl
from jax.experimental.pallas import tpu as pltpu

HEAD_DIM = 64
WIN_KEYS = 128
DILATIONS = (1, 4, 16)
CONV_WIDTH = 3
N_GROUPS = 4
EXPERTS_PER_GROUP = 8
N_EXPERTS = N_GROUPS * EXPERTS_PER_GROUP
EPS = 1e-6
NEG = -1e30
LOG2_E = 1.4426950408889634

LANES = 128
ROW_TILE = 512
EXPERT_BLOCK = 256
ATTN_UNROLL = 8
VMEM_LIMIT = 56 * 1024 * 1024

F32 = jnp.float32
BF16 = jnp.bfloat16


def _rms(x, g):
    return x * lax.rsqrt(jnp.mean(x * x, axis=-1, keepdims=True) + EPS) * g


def _mix_in_kernel(*refs, d_attn, d_conv, sequential):
    if sequential:
        (x_ref, g_ref, w_ref, cw_ref, gc_ref,
         q_ref, k_ref, v_ref, kt_ref, vt_ref, oc_ref, st_ref, carry_ref) = refs
    else:
        (x_ref, g_ref, w_ref, cw_ref, gc_ref, st0_ref, st1_ref,
         q_ref, k_ref, v_ref, oc_ref, u_ref) = refs
    x = x_ref[...]
    xb = _rms(x, g_ref[...]).astype(BF16)

    def proj(lo, width):
        return jnp.dot(xb, w_ref[:, lo:lo + width], preferred_element_type=F32)

    q_ref[...] = proj(0, d_attn)
    k = proj(d_attn, d_attn)
    v = proj(2 * d_attn, d_attn)
    k_ref[...] = k
    v_ref[...] = v
    gate = proj(3 * d_attn, d_conv)
    u = proj(3 * d_attn + d_conv, d_conv) * proj(3 * d_attn + 2 * d_conv, d_conv)

    tm = x.shape[0]
    if sequential:
        kt_ref[...] = k.T
        vt_ref[...] = v.T

        @pl.when(pl.program_id(1) == 0)
        def _():
            carry_ref[...] = jnp.zeros_like(carry_ref)

        row = lax.broadcasted_iota(jnp.int32, u.shape, 0)
        prev1 = carry_ref[1:2, :]
        prev2 = carry_ref[0:1, :]
        u1 = jnp.where(row == 0, prev1, pltpu.roll(u, 1, axis=0))
        u2 = jnp.where(row == 0, prev2, jnp.where(row == 1, prev1, pltpu.roll(u, 2, axis=0)))
        carry_ref[0:2, :] = u[tm - 2:tm, :]
        st_ref[...] = u[tm - 2:tm, :]
    else:
        u_ref[...] = u
        u2 = st0_ref[...]
        u1 = st1_ref[...]
    z = u2 * cw_ref[0:1, :] + u1 * cw_ref[1:2, :] + u * cw_ref[2:3, :]
    oc_ref[...] = _rms(gate * z, gc_ref[...])


def _mix_in_call(kernel, grid, in_specs, out_specs, out_shape, scratch, args):
    return pl.pallas_call(
        kernel, grid=grid, in_specs=in_specs, out_specs=out_specs, out_shape=out_shape,
        scratch_shapes=scratch,
        compiler_params=pltpu.CompilerParams(
            dimension_semantics=("arbitrary",) * len(grid), vmem_limit_bytes=VMEM_LIMIT),
        name="mix_in",
    )(*args)


def _mix_in_prompt(x2d, norm_g, w_in_bf16, conv_w, norm_gc, *, seq_len, d_attn, d_conv):
    t, d = x2d.shape
    tm = min(ROW_TILE, seq_len)
    n_seq, per = t // seq_len, seq_len // tm
    const = lambda b, s: (0, 0)
    row = lambda width: pl.BlockSpec((tm, width), lambda b, s: (b * per + s, 0))
    col = pl.BlockSpec((None, d_attn, tm), lambda b, s: (b, 0, s))
    f32 = lambda *shape: jax.ShapeDtypeStruct(shape, F32)
    return _mix_in_call(
        functools.partial(_mix_in_kernel, d_attn=d_attn, d_conv=d_conv, sequential=True),
        (n_seq, per),
        [row(d), pl.BlockSpec((1, d), const), pl.BlockSpec(w_in_bf16.shape, const),
         pl.BlockSpec((CONV_WIDTH, d_conv), const), pl.BlockSpec((1, d_conv), const)],
        [row(d_attn)] * 3 + [col] * 2 + [row(d_conv),
                                         pl.BlockSpec((None, CONV_WIDTH - 1, d_conv),
                                                      lambda b, s: (b, 0, 0))],
        [f32(t, d_attn)] * 3 + [f32(n_seq, d_attn, seq_len)] * 2
        + [f32(t, d_conv), f32(n_seq, CONV_WIDTH - 1, d_conv)],
        [pltpu.VMEM((8, d_conv), F32)],
        (x2d, norm_g, w_in_bf16, conv_w, norm_gc))


def _mix_in_sample(x2d, norm_g, w_in_bf16, conv_w, norm_gc, st0, st1, *, d_attn, d_conv):
    t, d = x2d.shape
    full = lambda arr: pl.BlockSpec(arr.shape, lambda i: (0,) * arr.ndim)
    f32 = lambda *shape: jax.ShapeDtypeStruct(shape, F32)
    args = (x2d, norm_g, w_in_bf16, conv_w, norm_gc, st0, st1)
    outs = [f32(t, d_attn)] * 3 + [f32(t, d_conv)] * 2
    return _mix_in_call(
        functools.partial(_mix_in_kernel, d_attn=d_attn, d_conv=d_conv, sequential=False),
        (1,), [full(a) for a in args], [full(o) for o in outs], outs, [], args)


def _attn_prompt_kernel(q_ref, k_ref, v_ref, o_ref, m_s, l_s, a_s, *, seq_len):
    w = WIN_KEYS
    scale = HEAD_DIM ** -0.5 * LOG2_E
    r_i = lax.broadcasted_iota(jnp.int32, (2 * w, 2 * w), 0) & (w - 1)
    c_i = lax.broadcasted_iota(jnp.int32, (2 * w, 2 * w), 1)
    mask_cur = (lax.broadcasted_iota(jnp.int32, (2 * w, w), 1)
                <= lax.broadcasted_iota(jnp.int32, (2 * w, w), 0) & (w - 1))
    mask_both = jnp.logical_and(c_i >= r_i, c_i - w <= r_i)
    first_head = lax.broadcasted_iota(jnp.int32, (w, 2 * HEAD_DIM), 1) < HEAD_DIM
    dn_t = (((1,), (1,)), ((), ()))

    def rows(start, dil):
        if dil > 1:
            return pl.ds(start, w, stride=dil)
        return pl.ds(start if isinstance(start, int) else pl.multiple_of(start, w), w)

    def run_branch(dil, first, last):
        span = dil * w
        nb = seq_len // span

        def blocks(its, with_prev):
            mask = mask_both if with_prev else mask_cur
            cur, qs, ks, vs = [], [], [], []
            for it in its:
                g = it % dil
                n = it // dil
                c = rows(g + n * span, dil)
                cur.append(c)
                qb = (q_ref[c, :] * scale).astype(BF16)
                zero = jnp.zeros_like(qb)
                qs.append(jnp.concatenate([jnp.where(first_head, qb, zero),
                                           jnp.where(first_head, zero, qb)], axis=0))
                k = k_ref[c, :].astype(BF16)
                v = v_ref[c, :].astype(BF16)
                if with_prev:
                    p = rows(g + (n - 1) * span, dil)
                    k = jnp.concatenate([k_ref[p, :].astype(BF16), k], axis=0)
                    v = jnp.concatenate([v_ref[p, :].astype(BF16), v], axis=0)
                ks.append(k)
                vs.append(v)
            scores = [lax.dot_general(q, k, dn_t, preferred_element_type=F32)
                      for q, k in zip(qs, ks)]
            ms, ps = [], []
            for s in scores:
                s = jnp.where(mask, s, NEG)
                m = jnp.max(s, axis=-1, keepdims=True)
                ms.append(m)
                ps.append(jnp.exp2(s - m).astype(BF16))
            ones = jnp.ones((ks[0].shape[0], 2 * HEAD_DIM), BF16)
            accs = [jnp.dot(p, jnp.concatenate([v, ones], axis=1), preferred_element_type=F32)
                    for p, v in zip(ps, vs)]
            for c, m, acc_l in zip(cur, ms, accs):
                acc, l = acc_l[:, :2 * HEAD_DIM], acc_l[:, 2 * HEAD_DIM:]
                m_b = jnp.where(first_head, m[:w], m[w:])
                l_b = jnp.where(first_head, l[:w], l[w:])
                a_b = jnp.where(first_head, acc[:w], acc[w:])
                if not first:
                    m_o = m_s[c, :]
                    m_n = jnp.maximum(m_o, m_b)
                    w_o = jnp.exp2(m_o - m_n)
                    w_b = jnp.exp2(m_b - m_n)
                    l_b = w_o * l_s[c, :] + w_b * l_b
                    a_b = w_o * a_s[c, :] + w_b * a_b
                    m_b = m_n
                if last:
                    o_ref[c, :] = a_b / l_b
                else:
                    m_s[c, :] = m_b
                    l_s[c, :] = l_b
                    a_s[c, :] = a_b

        def run(lo, hi, with_prev):
            u = ATTN_UNROLL
            trips = (hi - lo) // u

            def body(t, carry):
                blocks([lo + t * u + j for j in range(u)], with_prev)
                return carry

            if trips:
                lax.fori_loop(0, trips, body, 0)
            if lo + trips * u < hi:
                blocks(list(range(lo + trips * u, hi)), with_prev)

        run(0, dil, False)
        run(dil, dil * nb, True)

    order = sorted(DILATIONS, reverse=True)
    for i, dil in enumerate(order):
        run_branch(dil, i == 0, i == len(order) - 1)


def _attn_prompt(q, k, v, *, n_seq, seq_len):
    t, d_attn = q.shape
    pair = 2 * HEAD_DIM
    spec = pl.BlockSpec((seq_len, pair), lambda b, h: (b, h))
    return pl.pallas_call(
        functools.partial(_attn_prompt_kernel, seq_len=seq_len),
        grid=(n_seq, d_attn // pair),
        in_specs=[spec] * 3,
        out_specs=spec,
        out_shape=jax.ShapeDtypeStruct((t, d_attn), F32),
        scratch_shapes=[pltpu.VMEM((seq_len, pair), F32)] * 3,
        compiler_params=pltpu.CompilerParams(
            dimension_semantics=("arbitrary", "arbitrary"), vmem_limit_bytes=VMEM_LIMIT),
        name="attn_prompt",
    )(q, k, v)


def _attn_sample_kernel(q_ref, kn_ref, vn_ref, kt_ref, vt_ref, o_ref):
    n_heads, dh, w_buf = kt_ref.shape
    delta = w_buf - lax.broadcasted_iota(jnp.int32, (1, w_buf), 1)
    cnt = jnp.zeros((1, w_buf), F32)
    for dil in DILATIONS:
        assert dil & (dil - 1) == 0
        member = jnp.where(delta <= dil * WIN_KEYS, 1.0, 0.0)
        cnt = cnt + jnp.where((delta & (dil - 1)) == 0, member, 0.0)
    eye = (lax.broadcasted_iota(jnp.int32, (dh, dh), 0)
           == lax.broadcasted_iota(jnp.int32, (dh, dh), 1))
    to_col = lambda r: jnp.sum(jnp.where(eye, r, 0.0), axis=1, keepdims=True)
    to_row = lambda c: jnp.sum(jnp.where(eye, c, 0.0), axis=0, keepdims=True)
    outs = []
    for h in range(n_heads):
        sl = slice(h * dh, (h + 1) * dh)
        q = q_ref[:, sl] * (HEAD_DIM ** -0.5)
        s_self = jnp.sum(q * kn_ref[:, sl], axis=1, keepdims=True)
        s = jnp.sum(to_col(q) * kt_ref[h], axis=0, keepdims=True)
        s = jnp.where(cnt > 0.0, s, NEG)
        m = jnp.maximum(jnp.max(s, axis=1, keepdims=True), s_self)
        p = cnt * jnp.exp(s - m)
        p_self = len(DILATIONS) * jnp.exp(s_self - m)
        l = jnp.sum(p, axis=1, keepdims=True) + p_self
        acc = jnp.sum(p * vt_ref[h], axis=1, keepdims=True)
        outs.append((to_row(acc) + p_self * vn_ref[:, sl]) / l)
    o_ref[...] = jnp.concatenate(outs, axis=1)


def _attn_sample(q, k_new, v_new, cache_kt, cache_vt):
    db, n_heads, dh, w_buf = cache_kt.shape
    head_spec = pl.BlockSpec((None, 1, n_heads * dh), lambda b: (b, 0, 0))
    cache_spec = pl.BlockSpec((None, n_heads, dh, w_buf), lambda b: (b, 0, 0, 0))
    return pl.pallas_call(
        _attn_sample_kernel,
        grid=(db,),
        in_specs=[head_spec] * 3 + [cache_spec] * 2,
        out_specs=head_spec,
        out_shape=jax.ShapeDtypeStruct((db, 1, n_heads * dh), F32),
        compiler_params=pltpu.CompilerParams(
            dimension_semantics=("arbitrary",), vmem_limit_bytes=VMEM_LIMIT),
        name="attn_sample",
    )(q, k_new, v_new, cache_kt, cache_vt)


R_E0, R_E1, R_G0, R_G1, R_POS0, R_POS1 = range(6)
ROUTER_LANE0 = N_GROUPS
CHUNK = 16
CHUNKS_PER_BLOCK = EXPERT_BLOCK // CHUNK


def _max_tile_chunks(tm):
    return (2 * tm + (CHUNK - 1) * N_EXPERTS) // CHUNK


def _local_rows(tm):
    return 2 * tm + N_EXPERTS * CHUNK


def _mix_out_kernel(*refs, n_tiles, has_tail):
    if not has_tail:
        _mix_out_tile(*refs)
        return
    *tile_in, tail_ref, h_ref, route_ref, xs_ref, cnt_ref = refs

    @pl.when(pl.program_id(0) < n_tiles)
    def _():
        _mix_out_tile(*tile_in, h_ref, route_ref, xs_ref, cnt_ref)

    @pl.when(pl.program_id(0) == n_tiles)
    def _():
        rows = tail_ref.shape[0]
        xs_ref[0:rows, :] = tail_ref[...]
        xs_ref[rows:, :] = jnp.zeros((xs_ref.shape[0] - rows, xs_ref.shape[1]), xs_ref.dtype)


def _mix_out_tile(x_ref, a_ref, oc_ref, ga_ref, wo_ref, gf_ref, wr_ref, br_ref,
                  h_ref, route_ref, xs_ref, cnt_ref):
    d_attn = a_ref.shape[1]
    tm, d = x_ref.shape
    a = _rms(a_ref[...], ga_ref[...]).astype(BF16)
    mix = jnp.dot(a, wo_ref[0:d_attn, :], preferred_element_type=F32)
    mix = mix + jnp.dot(oc_ref[...].astype(BF16), wo_ref[d_attn:, :], preferred_element_type=F32)
    h = x_ref[...] + mix
    h_ref[...] = h
    tok = _rms(h, gf_ref[...])

    tok_hi = tok.astype(BF16)
    tok_lo = (tok - tok_hi.astype(F32)).astype(BF16)
    hi_part = jnp.dot(tok_hi, wr_ref[...], preferred_element_type=F32)
    lo_part = jnp.dot(tok_lo, wr_ref[:, :LANES], preferred_element_type=F32)
    logits = hi_part[:, :LANES] + hi_part[:, LANES:] + lo_part + br_ref[...]
    lane = lax.broadcasted_iota(jnp.int32, logits.shape, 1)
    big = jnp.int32(LANES)
    neg_inf = jnp.float32(-jnp.inf)

    def top1(vals):
        best = jnp.max(vals, axis=-1, keepdims=True)
        idx = jnp.min(jnp.where(vals == best, lane, big), axis=-1, keepdims=True)
        return best, idx

    is_group = lane < N_GROUPS
    lg = jnp.where(is_group, logits, neg_inf)
    mg, g_sel = top1(lg)
    p_group = 1.0 / jnp.sum(jnp.where(is_group, jnp.exp(lg - mg), 0.0), axis=-1, keepdims=True)

    lo = ROUTER_LANE0 + g_sel * EXPERTS_PER_GROUP
    in_group = jnp.logical_and(lane >= lo, lane < lo + EXPERTS_PER_GROUP)
    le = jnp.where(in_group, logits, neg_inf)
    v1, i1 = top1(le)
    v2, i2 = top1(jnp.where(lane == i1, neg_inf, le))
    e2 = jnp.exp(v2 - v1)
    gate1 = p_group / (1.0 + e2)
    gate2 = p_group * e2 / (1.0 + e2)

    oh1 = lane == i1
    oh2 = lane == i2
    both = jnp.where(jnp.logical_or(oh1, oh2), 1.0, 0.0)
    r_i = lax.broadcasted_iota(jnp.int32, (tm, tm), 0)
    c_i = lax.broadcasted_iota(jnp.int32, (tm, tm), 1)
    strict_lower = jnp.where(c_i < r_i, 1.0, 0.0).astype(BF16)
    before = jnp.dot(strict_lower, both.astype(BF16), preferred_element_type=F32)
    chunks = jnp.floor((jnp.sum(both, axis=0, keepdims=True) + (CHUNK - 1)) * (1.0 / CHUNK))
    u_r = lax.broadcasted_iota(jnp.int32, (LANES, LANES), 0)
    u_c = lax.broadcasted_iota(jnp.int32, (LANES, LANES), 1)
    strict_upper = jnp.where(u_r < u_c, 1.0, 0.0).astype(BF16)
    chunks8 = jnp.broadcast_to(chunks, (8, LANES))
    first_row = CHUNK * jnp.dot(chunks8.astype(BF16), strict_upper,
                                preferred_element_type=F32)[0:1, :]
    pos = first_row + before
    pos1 = jnp.sum(jnp.where(oh1, pos, 0.0), axis=-1, keepdims=True)
    pos2 = jnp.sum(jnp.where(oh2, pos, 0.0), axis=-1, keepdims=True)
    cnt_ref[...] = jnp.where(lax.broadcasted_iota(jnp.int32, (8, LANES), 0) == 0, chunks8, 0.0)

    rec = jnp.zeros(logits.shape, F32)
    for col, val in ((R_E0, (i1 - ROUTER_LANE0).astype(F32)), (R_E1, (i2 - ROUTER_LANE0).astype(F32)),
                     (R_G0, gate1), (R_G1, gate2), (R_POS0, pos1), (R_POS1, pos2)):
        rec = jnp.where(lane == col, val, rec)
    route_ref[...] = rec

    rec_t = rec.T
    l1 = rec_t[R_POS0:R_POS0 + 1, :].astype(jnp.int32)
    l2 = rec_t[R_POS1:R_POS1 + 1, :].astype(jnp.int32)
    srow = lax.broadcasted_iota(jnp.int32, (xs_ref.shape[0], tm), 0)
    perm = jnp.where(srow == l1, 1.0, jnp.where(srow == l2, 1.0, 0.0)).astype(BF16)
    xs_ref[...] = jnp.dot(perm, tok_hi, preferred_element_type=F32).astype(BF16)


def _mix_out(x2d, attn, oconv, norm_ga, w_out_bf16, norm_gf, w_router, b_router, tail=None):
    t, d = x2d.shape
    d_attn, d_conv = attn.shape[1], oconv.shape[1]
    tm = min(ROW_TILE, t)
    nt = t // tm
    r_l = _local_rows(tm)
    has_tail = tail is not None
    tile = lambda i: jnp.minimum(i, nt - 1)
    row = lambda width: pl.BlockSpec((tm, width), lambda i: (tile(i), 0))
    full = lambda arr: pl.BlockSpec(arr.shape, lambda i: (0, 0))
    args = [x2d, attn, oconv, norm_ga, w_out_bf16, norm_gf, w_router, b_router]
    in_specs = [row(d), row(d_attn), row(d_conv)] + [full(a) for a in args[3:]]
    if has_tail:
        assert tail.shape[0] <= r_l and tail.shape[1] == d
        args.append(tail)
        in_specs.append(full(tail))
    return pl.pallas_call(
        functools.partial(_mix_out_kernel, n_tiles=nt, has_tail=has_tail),
        grid=(nt + has_tail,),
        in_specs=in_specs,
        out_specs=[row(d), row(LANES), pl.BlockSpec((r_l, d), lambda i: (i, 0)),
                   pl.BlockSpec((None, 8, LANES), lambda i: (tile(i), 0, 0))],
        out_shape=[jax.ShapeDtypeStruct((t, d), F32), jax.ShapeDtypeStruct((t, LANES), F32),
                   jax.ShapeDtypeStruct(((nt + has_tail) * r_l, d), BF16),
                   jax.ShapeDtypeStruct((nt, 8, LANES), F32)],
        compiler_params=pltpu.CompilerParams(
            dimension_semantics=("arbitrary",), vmem_limit_bytes=VMEM_LIMIT),
        name="mix_out",
    )(*args)


def _sorted_layout(tile_chunks, tile_row0, max_local, n_blocks):
    nt, n_exp = tile_chunks.shape
    cpb = CHUNKS_PER_BLOCK
    i32 = jnp.int32
    seg = jnp.sum(tile_chunks, axis=0)
    padded = (seg + cpb - 1) // cpb * cpb
    pend = jnp.cumsum(padded)
    pstart = pend - padded
    tile_incl = jnp.cumsum(tile_chunks, axis=0)
    tile_excl = tile_incl - tile_chunks
    local_incl = jnp.cumsum(tile_chunks, axis=1)
    local_excl = local_incl - tile_chunks
    base = pstart[None, :] + tile_excl

    block_first = jnp.arange(n_blocks, dtype=i32) * cpb
    block_e = jnp.minimum(jnp.sum((pend[None, :] <= block_first[:, None]).astype(i32), axis=1),
                          n_exp - 1)
    n_used = (pend[-1:] // cpb).astype(i32)

    onehot_pick = lambda onehot, table: jnp.sum(jnp.where(onehot, table, 0), axis=-1)

    is_e = block_e[:, None] == jnp.arange(n_exp, dtype=i32)[None, :]
    of_expert = lambda table_te: onehot_pick(is_e[:, None, :], table_te[None, :, :])
    incl_b, cnt_b, lexcl_b = of_expert(tile_incl), of_expert(tile_chunks), of_expert(local_excl)
    q = (block_first - onehot_pick(is_e, pstart[None, :]))[:, None] + jnp.arange(cpb, dtype=i32)
    tile_q = jnp.minimum(jnp.sum((incl_b[:, None, :] <= q[:, :, None]).astype(i32), axis=2), nt - 1)
    is_t = tile_q[:, :, None] == jnp.arange(nt, dtype=i32)[None, None, :]
    of_tile = lambda table_bt: onehot_pick(is_t, table_bt[:, None, :])
    local_chunk = of_tile(lexcl_b) + q - of_tile(incl_b - cnt_b)
    in_run = jnp.logical_and(q >= 0, q < onehot_pick(is_e, seg[None, :])[:, None])
    src_row = jnp.where(in_run, of_tile(tile_row0[None, :]) + CHUNK * local_chunk, 0)
    src_row = src_row.reshape(-1).astype(i32)

    c = jnp.arange(max_local, dtype=i32)
    e_c = jnp.minimum(jnp.sum((local_incl[:, None, :] <= c[None, :, None]).astype(i32), axis=2),
                      n_exp - 1)
    is_ec = e_c[:, :, None] == jnp.arange(n_exp, dtype=i32)[None, None, :]
    of_run = lambda table_te: onehot_pick(is_ec, table_te[:, None, :])
    global_chunk = of_run(base) + c[None, :] - of_run(local_excl)
    tile_src = jnp.where(c[None, :] < local_incl[:, -1:], CHUNK * global_chunk, 0).astype(i32)
    return block_e.astype(i32), n_used, src_row, tile_src


def _chunk_gather(src_ref, hbm_ref, buf, sems, item, slot, n_chunks, *, wait):
    for c in range(n_chunks):
        row = 0 if wait else pl.multiple_of(src_ref[item * n_chunks + c], CHUNK)
        copy = pltpu.make_async_copy(hbm_ref.at[pl.ds(row, CHUNK)],
                                     buf.at[slot, pl.ds(c * CHUNK, CHUNK)], sems.at[slot])
        if wait:
            copy.wait()
        else:
            copy.start()


def _prefetched(gather, step, n_items, body):
    slot = step % 2

    @pl.when(jnp.logical_and(step == 0, n_items > 0))
    def _():
        gather(0, 0, wait=False)

    @pl.when(step + 1 < n_items)
    def _():
        gather(step + 1, 1 - slot, wait=False)

    body(slot, lambda: gather(step, slot, wait=True))


def _experts_kernel(block_e_ref, n_used_ref, src_ref, xs_ref, wg_ref, wu_ref, wd_ref,
                    y_ref, xblk, sems, wg_b, wu_b, wd_b):
    b = pl.program_id(0)
    gather = functools.partial(_chunk_gather, src_ref, xs_ref, xblk, sems,
                               n_chunks=CHUNKS_PER_BLOCK)

    new_expert = jnp.logical_or(b == 0, block_e_ref[b] != block_e_ref[jnp.maximum(b - 1, 0)])

    @pl.when(jnp.logical_and(new_expert, b < n_used_ref[0]))
    def _():
        wg_b[...] = wg_ref[...].astype(BF16)
        wu_b[...] = wu_ref[...].astype(BF16)
        wd_b[...] = wd_ref[...].astype(BF16)

    def body(slot, wait_current):
        @pl.when(b < n_used_ref[0])
        def _():
            wait_current()
            x = xblk[slot]
            gate = jnp.dot(x, wg_b[...], preferred_element_type=F32)
            up = jnp.dot(x, wu_b[...], preferred_element_type=F32)
            hid = gate * (1.0 / (1.0 + jnp.exp(-gate))) * up
            y_ref[...] = jnp.dot(hid.astype(BF16), wd_b[...],
                                 preferred_element_type=F32).astype(BF16)

        @pl.when(b >= n_used_ref[0])
        def _():
            y_ref[...] = jnp.zeros_like(y_ref)

    _prefetched(gather, b, n_used_ref[0], body)


def _experts(block_e, n_used, src_row, xs, w_gate, w_up, w_down):
    n_blocks = block_e.shape[0]
    _, d, d_exp = w_gate.shape
    blk = EXPERT_BLOCK
    weight = lambda k, n: pl.BlockSpec((None, k, n), lambda b, be, nu, src: (be[b], 0, 0))
    return pl.pallas_call(
        _experts_kernel,
        grid_spec=pltpu.PrefetchScalarGridSpec(
            num_scalar_prefetch=3,
            grid=(n_blocks,),
            in_specs=[pl.BlockSpec(memory_space=pl.ANY),
                      weight(d, d_exp), weight(d, d_exp), weight(d_exp, d)],
            out_specs=pl.BlockSpec((blk, d), lambda b, be, nu, src: (b, 0)),
            scratch_shapes=[pltpu.VMEM((2, blk, d), BF16), pltpu.SemaphoreType.DMA((2,)),
                            pltpu.VMEM((d, d_exp), BF16), pltpu.VMEM((d, d_exp), BF16),
                            pltpu.VMEM((d_exp, d), BF16)],
        ),
        out_shape=jax.ShapeDtypeStruct((n_blocks * blk, d), BF16),
        compiler_params=pltpu.CompilerParams(
            dimension_semantics=("arbitrary",), vmem_limit_bytes=VMEM_LIMIT),
        name="experts",
    )(block_e, n_used, src_row, xs, w_gate, w_up, w_down)


def _combine_kernel(src_ref, h_ref, route_ref, gn_ref, ybuf_ref, o_ref, yloc, sems):
    tm = h_ref.shape[0]
    r_l = yloc.shape[1]
    gather = functools.partial(_chunk_gather, src_ref, ybuf_ref, yloc, sems,
                               n_chunks=r_l // CHUNK)

    def body(slot, wait_current):
        wait_current()
        y = yloc[slot]
        route = route_ref[...]
        l0 = route[:, R_POS0:R_POS0 + 1].astype(jnp.int32)
        l1 = route[:, R_POS1:R_POS1 + 1].astype(jnp.int32)
        srow = lax.broadcasted_iota(jnp.int32, (tm, r_l), 1)
        gates = jnp.where(srow == l0, route[:, R_G0:R_G0 + 1],
                          jnp.where(srow == l1, route[:, R_G1:R_G1 + 1], 0.0)).astype(BF16)
        f = jnp.dot(gates, y, preferred_element_type=F32)
        o_ref[...] = _rms(h_ref[...] + f, gn_ref[...])

    _prefetched(gather, pl.program_id(0), pl.num_programs(0), body)


def _combine(tile_src, h, route, norm_g, ybuf):
    t, d = h.shape
    tm = min(ROW_TILE, t)
    r_l = _local_rows(tm)
    return pl.pallas_call(
        _combine_kernel,
        grid_spec=pltpu.PrefetchScalarGridSpec(
            num_scalar_prefetch=1,
            grid=(t // tm,),
            in_specs=[pl.BlockSpec((tm, d), lambda i, src: (i, 0)),
                      pl.BlockSpec((tm, LANES), lambda i, src: (i, 0)),
                      pl.BlockSpec((1, d), lambda i, src: (0, 0)),
                      pl.BlockSpec(memory_space=pl.ANY)],
            out_specs=pl.BlockSpec((tm, d), lambda i, src: (i, 0)),
            scratch_shapes=[pltpu.VMEM((2, r_l, d), BF16),
                            pltpu.SemaphoreType.DMA((2,))],
        ),
        out_shape=jax.ShapeDtypeStruct((t, d), F32),
        compiler_params=pltpu.CompilerParams(
            dimension_semantics=("arbitrary",), vmem_limit_bytes=VMEM_LIMIT),
        name="combine",
    )(tile_src, h, route, norm_g, ybuf)


def kernel(x_prompt, x_sample, cache_k, cache_v, state_conv, norm_mix, w_in, conv_w, norm_out_attn,
           norm_out_conv, w_out, norm_ffn, w_router_group, b_router_group, w_router_expert,
           b_router_expert, w_gate, w_up, w_down, norm_final):
    n_seq, seq_len, d = x_prompt.shape
    db, ds, _ = x_sample.shape
    depth = w_in.shape[0]
    _, _, w_buf, n_heads, dh = cache_k.shape
    d_attn = n_heads * dh
    d_conv = d - d_attn
    assert depth == 1 and ds == 1 and dh == HEAD_DIM
    assert seq_len % (max(DILATIONS) * WIN_KEYS) == 0 and seq_len <= max(DILATIONS) * WIN_KEYS
    layer = 0
    tp, ts = n_seq * seq_len, db

    xp = x_prompt.reshape(tp, d)
    xs = x_sample.reshape(ts, d)
    row = lambda vec: vec.reshape(1, -1)
    w_in_b = w_in[layer].astype(BF16)
    w_out_b = w_out[layer].astype(BF16)
    g_mix, g_oa, g_oc, g_ffn = (row(norm_mix[layer]), row(norm_out_attn[layer]),
                                row(norm_out_conv[layer]), row(norm_ffn[layer]))
    st0, st1 = state_conv[layer, :, 0, :], state_conv[layer, :, 1, :]

    qp, kp, vp, kp_t, vp_t, ocp, conv_p = _mix_in_prompt(
        xp, g_mix, w_in_b, conv_w[layer], g_oc, seq_len=seq_len, d_attn=d_attn, d_conv=d_conv)
    qs, ks, vs, ocs, us = _mix_in_sample(
        xs, g_mix, w_in_b, conv_w[layer], g_oc, st0, st1, d_attn=d_attn, d_conv=d_conv)

    attn_p = _attn_prompt(qp, kp, vp, n_seq=n_seq, seq_len=seq_len)
    heads = lambda a: a.reshape(ts, 1, d_attn)
    positions_last = lambda c: jnp.transpose(c, (0, 2, 3, 1))
    attn_s = _attn_sample(heads(qs), heads(ks), heads(vs),
                          positions_last(cache_k[layer]), positions_last(cache_v[layer]))
    attn_s = attn_s.reshape(ts, d_attn)

    n_route = N_GROUPS + N_EXPERTS
    w_router = jnp.zeros((d, LANES), F32).at[:, :N_GROUPS].set(w_router_group[layer])
    w_router = w_router.at[:, N_GROUPS:n_route].set(w_router_expert[layer])
    b_router = jnp.zeros((1, LANES), F32).at[0, :N_GROUPS].set(b_router_group[layer])
    b_router = b_router.at[0, N_GROUPS:n_route].set(b_router_expert[layer])
    w_router_hi = w_router.astype(BF16)
    w_router_lo = (w_router - w_router_hi.astype(F32)).astype(BF16)
    mix_out = functools.partial(_mix_out, norm_ga=g_oa, w_out_bf16=w_out_b, norm_gf=g_ffn,
                                w_router=jnp.concatenate([w_router_hi, w_router_lo], axis=1),
                                b_router=b_router)
    h_s, route_s, xs_s, cnt_s = mix_out(xs, attn_s, ocs)
    assert cnt_s.shape[0] == 1
    h_p, route_p, xs_all, cnt_p = mix_out(xp, attn_p, ocp, tail=xs_s)

    tile_chunks = jnp.concatenate([cnt_p[:, 0, ROUTER_LANE0:n_route],
                                   cnt_s[:, 0, ROUTER_LANE0:n_route]], axis=0).astype(jnp.int32)
    ntp, nts = cnt_p.shape[0], cnt_s.shape[0]
    tm_p, tm_s = tp // ntp, ts // nts
    rl_p, rl_s = _local_rows(tm_p), _local_rows(tm_s)
    tile_row0 = jnp.arange(ntp + nts, dtype=jnp.int32) * rl_p
    total_chunks = ntp * _max_tile_chunks(tm_p) + nts * _max_tile_chunks(tm_s)
    n_blocks = -(-(total_chunks + N_EXPERTS * (CHUNKS_PER_BLOCK - 1)) // CHUNKS_PER_BLOCK)
    block_e, n_used, src_row, tile_src = _sorted_layout(tile_chunks, tile_row0, rl_p // CHUNK,
                                                        n_blocks)
    ybuf = _experts(block_e, n_used, src_row, xs_all, w_gate[layer], w_up[layer], w_down[layer])
    g_fin = row(norm_final)
    y_p = _combine(tile_src[:ntp].reshape(-1), h_p, route_p, g_fin, ybuf)
    y_s = _combine(tile_src[ntp:, :rl_s // CHUNK].reshape(-1), h_s, route_s, g_fin, ybuf)

    w_keep = min(max(DILATIONS) * WIN_KEYS, seq_len)
    kv5 = lambda a_t: jnp.transpose(a_t.reshape(n_seq, n_heads, dh, seq_len),
                                    (0, 3, 1, 2))[None, :, seq_len - w_keep:]
    conv_s = jnp.stack([st1, us], axis=1)[None]
    kvs = lambda a: a.reshape(1, ts, 1, n_heads, dh)
    return (y_p.reshape(n_seq, seq_len, d), y_s.reshape(db, ds, d), kv5(kp_t), kv5(vp_t),
            conv_p[None], kvs(ks), kvs(vs), conv_s)
```

```python
import functools

import jax
import jax.numpy as jnp
from jax import lax
from jax.experimental import pallas as pl
from jax.experimental.pallas import tpu as pltpu

HEAD_DIM = 64
WIN_KEYS = 128
DILATIONS = (1, 4, 16)
CONV_WIDTH = 3
N_GROUPS = 4
EXPERTS_PER_GROUP = 8
N_EXPERTS = N_GROUPS * EXPERTS_PER_GROUP
EPS = 1e-6
NEG = -1e30
LOG2_E = 1.4426950408889634

LANES = 128
ROW_TILE = 512
EXPERT_BLOCK = 256
ATTN_UNROLL = 8
VMEM_LIMIT = 56 * 1024 * 1024

F32 = jnp.float32
BF16 = jnp.bfloat16


def _rms(x, g):
    return x * lax.rsqrt(jnp.mean(x * x, axis=-1, keepdims=True) + EPS) * g


def _mix_in_kernel(*refs, d_attn, d_conv, sequential):
    if sequential:
        (x_ref, g_ref, w_ref, cw_ref, gc_ref,
         q_ref, k_ref, v_ref, kt_ref, vt_ref, oc_ref, st_ref, carry_ref) = refs
    else:
        (x_ref, g_ref, w_ref, cw_ref, gc_ref, st0_ref, st1_ref,
         q_ref, k_ref, v_ref, oc_ref, u_ref) = refs
    x = x_ref[...]
    xb = _rms(x, g_ref[...]).astype(BF16)

    def proj(lo, width):
        return jnp.dot(xb, w_ref[:, lo:lo + width], preferred_element_type=F32)

    q_ref[...] = proj(0, d_attn)
    k = proj(d_attn, d_attn)
    v = proj(2 * d_attn, d_attn)
    k_ref[...] = k
    v_ref[...] = v
    gate = proj(3 * d_attn, d_conv)
    u = proj(3 * d_attn + d_conv, d_conv) * proj(3 * d_attn + 2 * d_conv, d_conv)

    tm = x.shape[0]
    if sequential:
        kt_ref[...] = k.T
        vt_ref[...] = v.T

        @pl.when(pl.program_id(1) == 0)
        def _():
            carry_ref[...] = jnp.zeros_like(carry_ref)

        row = lax.broadcasted_iota(jnp.int32, u.shape, 0)
        prev1 = carry_ref[1:2, :]
        prev2 = carry_ref[0:1, :]
        u1 = jnp.where(row == 0, prev1, pltpu.roll(u, 1, axis=0))
        u2 = jnp.where(row == 0, prev2, jnp.where(row == 1, prev1, pltpu.roll(u, 2, axis=0)))
        carry_ref[0:2, :] = u[tm - 2:tm, :]
        st_ref[...] = u[tm - 2:tm, :]
    else:
        u_ref[...] = u
        u2 = st0_ref[...]
        u1 = st1_ref[...]
    z = u2 * cw_ref[0:1, :] + u1 * cw_ref[1:2, :] + u * cw_ref[2:3, :]
    oc_ref[...] = _rms(gate * z, gc_ref[...])


def _mix_in_call(kernel, grid, in_specs, out_specs, out_shape, scratch, args):
    return pl.pallas_call(
        kernel, grid=grid, in_specs=in_specs, out_specs=out_specs, out_shape=out_shape,
        scratch_shapes=scratch,
        compiler_params=pltpu.CompilerParams(
            dimension_semantics=("arbitrary",) * len(grid), vmem_limit_bytes=VMEM_LIMIT),
        name="mix_in",
    )(*args)


def _mix_in_prompt(x2d, norm_g, w_in_bf16, conv_w, norm_gc, *, seq_len, d_attn, d_conv):
    t, d = x2d.shape
    tm = min(ROW_TILE, seq_len)
    n_seq, per = t // seq_len, seq_len // tm
    const = lambda b, s: (0, 0)
    row = lambda width: pl.BlockSpec((tm, width), lambda b, s: (b * per + s, 0))
    col = pl.BlockSpec((None, d_attn, tm), lambda b, s: (b, 0, s))
    f32 = lambda *shape: jax.ShapeDtypeStruct(shape, F32)
    return _mix_in_call(
        functools.partial(_mix_in_kernel, d_attn=d_attn, d_conv=d_conv, sequential=True),
        (n_seq, per),
        [row(d), pl.BlockSpec((1, d), const), pl.BlockSpec(w_in_bf16.shape, const),
         pl.BlockSpec((CONV_WIDTH, d_conv), const), pl.BlockSpec((1, d_conv), const)],
        [row(d_attn)] * 3 + [col] * 2 + [row(d_conv),
                                         pl.BlockSpec((None, CONV_WIDTH - 1, d_conv),
                                                      lambda b, s: (b, 0, 0))],
        [f32(t, d_attn)] * 3 + [f32(n_seq, d_attn, seq_len)] * 2
        + [f32(t, d_conv), f32(n_seq, CONV_WIDTH - 1, d_conv)],
        [pltpu.VMEM((8, d_conv), F32)],
        (x2d, norm_g, w_in_bf16, conv_w, norm_gc))


def _mix_in_sample(x2d, norm_g, w_in_bf16, conv_w, norm_gc, st0, st1, *, d_attn, d_conv):
    t, d = x2d.shape
    full = lambda arr: pl.BlockSpec(arr.shape, lambda i: (0,) * arr.ndim)
    f32 = lambda *shape: jax.ShapeDtypeStruct(shape, F32)
    args = (x2d, norm_g, w_in_bf16, conv_w, norm_gc, st0, st1)
    outs = [f32(t, d_attn)] * 3 + [f32(t, d_conv)] * 2
    return _mix_in_call(
        functools.partial(_mix_in_kernel, d_attn=d_attn, d_conv=d_conv, sequential=False),
        (1,), [full(a) for a in args], [full(o) for o in outs], outs, [], args)


def _attn_prompt_kernel(q_ref, k_ref, v_ref, o_ref, m_s, l_s, a_s, *, seq_len):
    w = WIN_KEYS
    scale = HEAD_DIM ** -0.5 * LOG2_E
    r_i = lax.broadcasted_iota(jnp.int32, (2 * w, 2 * w), 0) & (w - 1)
    c_i = lax.broadcasted_iota(jnp.int32, (2 * w, 2 * w), 1)
    mask_cur = (lax.broadcasted_iota(jnp.int32, (2 * w, w), 1)
                <= lax.broadcasted_iota(jnp.int32, (2 * w, w), 0) & (w - 1))
    mask_both = jnp.logical_and(c_i >= r_i, c_i - w <= r_i)
    first_head = lax.broadcasted_iota(jnp.int32, (w, 2 * HEAD_DIM), 1) < HEAD_DIM
    dn_t = (((1,), (1,)), ((), ()))

    def rows(start, dil):
        if dil > 1:
            return pl.ds(start, w, stride=dil)
        return pl.ds(start if isinstance(start, int) else pl.multiple_of(start, w), w)

    def run_branch(dil, first, last):
        span = dil * w
        nb = seq_len // span

        def blocks(its, with_prev):
            mask = mask_both if with_prev else mask_cur
            cur, qs, ks, vs = [], [], [], []
            for it in its:
                g = it % dil
                n = it // dil
                c = rows(g + n * span, dil)
                cur.append(c)
                qb = (q_ref[c, :] * scale).astype(BF16)
                zero = jnp.zeros_like(qb)
                qs.append(jnp.concatenate([jnp.where(first_head, qb, zero),
                                           jnp.where(first_head, zero, qb)], axis=0))
                k = k_ref[c, :].astype(BF16)
                v = v_ref[c, :].astype(BF16)
                if with_prev:
                    p = rows(g + (n - 1) * span, dil)
                    k = jnp.concatenate([k_ref[p, :].astype(BF16), k], axis=0)
                    v = jnp.concatenate([v_ref[p, :].astype(BF16), v], axis=0)
                ks.append(k)
                vs.append(v)
            scores = [lax.dot_general(q, k, dn_t, preferred_element_type=F32)
                      for q, k in zip(qs, ks)]
            ms, ps = [], []
            for s in scores:
                s = jnp.where(mask, s, NEG)
                m = jnp.max(s, axis=-1, keepdims=True)
                ms.append(m)
                ps.append(jnp.exp2(s - m).astype(BF16))
            ones = jnp.ones((ks[0].shape[0], 2 * HEAD_DIM), BF16)
            accs = [jnp.dot(p, jnp.concatenate([v, ones], axis=1), preferred_element_type=F32)
                    for p, v in zip(ps, vs)]
            for c, m, acc_l in zip(cur, ms, accs):
                acc, l = acc_l[:, :2 * HEAD_DIM], acc_l[:, 2 * HEAD_DIM:]
                m_b = jnp.where(first_head, m[:w], m[w:])
                l_b = jnp.where(first_head, l[:w], l[w:])
                a_b = jnp.where(first_head, acc[:w], acc[w:])
                if not first:
                    m_o = m_s[c, :]
                    m_n = jnp.maximum(m_o, m_b)
                    w_o = jnp.exp2(m_o - m_n)
                    w_b = jnp.exp2(m_b - m_n)
                    l_b = w_o * l_s[c, :] + w_b * l_b
                    a_b = w_o * a_s[c, :] + w_b * a_b
                    m_b = m_n
                if last:
                    o_ref[c, :] = a_b / l_b
                else:
                    m_s[c, :] = m_b
                    l_s[c, :] = l_b
                    a_s[c, :] = a_b

        def run(lo, hi, with_prev):
            u = ATTN_UNROLL
            trips = (hi - lo) // u

            def body(t, carry):
                blocks([lo + t * u + j for j in range(u)], with_prev)
                return carry

            if trips:
                lax.fori_loop(0, trips, body, 0)
            if lo + trips * u < hi:
                blocks(list(range(lo + trips * u, hi)), with_prev)

        run(0, dil, False)
        run(dil, dil * nb, True)

    order = sorted(DILATIONS, reverse=True)
    for i, dil in enumerate(order):
        run_branch(dil, i == 0, i == len(order) - 1)


def _attn_prompt(q, k, v, *, n_seq, seq_len):
    t, d_attn = q.shape
    pair = 2 * HEAD_DIM
    spec = pl.BlockSpec((seq_len, pair), lambda b, h: (b, h))
    return pl.pallas_call(
        functools.partial(_attn_prompt_kernel, seq_len=seq_len),
        grid=(n_seq, d_attn // pair),
        in_specs=[spec] * 3,
        out_specs=spec,
        out_shape=jax.ShapeDtypeStruct((t, d_attn), F32),
        scratch_shapes=[pltpu.VMEM((seq_len, pair), F32)] * 3,
        compiler_params=pltpu.CompilerParams(
            dimension_semantics=("arbitrary", "arbitrary"), vmem_limit_bytes=VMEM_LIMIT),
        name="attn_prompt",
    )(q, k, v)


def _attn_sample_kernel(q_ref, kn_ref, vn_ref, kt_ref, vt_ref, o_ref):
    n_heads, dh, w_buf = kt_ref.shape
    delta = w_buf - lax.broadcasted_iota(jnp.int32, (1, w_buf), 1)
    cnt = jnp.zeros((1, w_buf), F32)
    for dil in DILATIONS:
        assert dil & (dil - 1) == 0
        member = jnp.where(delta <= dil * WIN_KEYS, 1.0, 0.0)
        cnt = cnt + jnp.where((delta & (dil - 1)) == 0, member, 0.0)
    eye = (lax.broadcasted_iota(jnp.int32, (dh, dh), 0)
           == lax.broadcasted_iota(jnp.int32, (dh, dh), 1))
    to_col = lambda r: jnp.sum(jnp.where(eye, r, 0.0), axis=1, keepdims=True)
    to_row = lambda c: jnp.sum(jnp.where(eye, c, 0.0), axis=0, keepdims=True)
    outs = []
    for h in range(n_heads):
        sl = slice(h * dh, (h + 1) * dh)
        q = q_ref[:, sl] * (HEAD_DIM ** -0.5)
        s_self = jnp.sum(q * kn_ref[:, sl], axis=1, keepdims=True)
        s = jnp.sum(to_col(q) * kt_ref[h], axis=0, keepdims=True)
        s = jnp.where(cnt > 0.0, s, NEG)
        m = jnp.maximum(jnp.max(s, axis=1, keepdims=True), s_self)
        p = cnt * jnp.exp(s - m)
        p_self = len(DILATIONS) * jnp.exp(s_self - m)
        l = jnp.sum(p, axis=1, keepdims=True) + p_self
        acc = jnp.sum(p * vt_ref[h], axis=1, keepdims=True)
        outs.append((to_row(acc) + p_self * vn_ref[:, sl]) / l)
    o_ref[...] = jnp.concatenate(outs, axis=1)


def _attn_sample(q, k_new, v_new, cache_kt, cache_vt):
    db, n_heads, dh, w_buf = cache_kt.shape
    head_spec = pl.BlockSpec((None, 1, n_heads * dh), lambda b: (b, 0, 0))
    cache_spec = pl.BlockSpec((None, n_heads, dh, w_buf), lambda b: (b, 0, 0, 0))
    return pl.pallas_call(
        _attn_sample_kernel,
        grid=(db,),
        in_specs=[head_spec] * 3 + [cache_spec] * 2,
        out_specs=head_spec,
        out_shape=jax.ShapeDtypeStruct((db, 1, n_heads * dh), F32),
        compiler_params=pltpu.CompilerParams(
            dimension_semantics=("arbitrary",), vmem_limit_bytes=VMEM_LIMIT),
        name="attn_sample",
    )(q, k_new, v_new, cache_kt, cache_vt)


R_E0, R_E1, R_G0, R_G1, R_POS0, R_POS1 = range(6)
ROUTER_LANE0 = N_GROUPS
CHUNK = 16
CHUNKS_PER_BLOCK = EXPERT_BLOCK // CHUNK


def _max_tile_chunks(tm):
    return (2 * tm + (CHUNK - 1) * N_EXPERTS) // CHUNK


def _local_rows(tm):
    return 2 * tm + N_EXPERTS * CHUNK


def _mix_out_kernel(*refs, n_tiles, has_tail):
    if not has_tail:
        _mix_out_tile(*refs)
        return
    *tile_in, tail_ref, h_ref, route_ref, xs_ref, cnt_ref = refs

    @pl.when(pl.program_id(0) < n_tiles)
    def _():
        _mix_out_tile(*tile_in, h_ref, route_ref, xs_ref, cnt_ref)

    @pl.when(pl.program_id(0) == n_tiles)
    def _():
        rows = tail_ref.shape[0]
        xs_ref[0:rows, :] = tail_ref[...]
        xs_ref[rows:, :] = jnp.zeros((xs_ref.shape[0] - rows, xs_ref.shape[1]), xs_ref.dtype)


def _mix_out_tile(x_ref, a_ref, oc_ref, ga_ref, wo_ref, gf_ref, wr_ref, br_ref,
                  h_ref, route_ref, xs_ref, cnt_ref):
    d_attn = a_ref.shape[1]
    tm, d = x_ref.shape
    a = _rms(a_ref[...], ga_ref[...]).astype(BF16)
    mix = jnp.dot(a, wo_ref[0:d_attn, :], preferred_element_type=F32)
    mix = mix + jnp.dot(oc_ref[...].astype(BF16), wo_ref[d_attn:, :], preferred_element_type=F32)
    h = x_ref[...] + mix
    h_ref[...] = h
    tok = _rms(h, gf_ref[...])

    tok_hi = tok.astype(BF16)
    tok_lo = (tok - tok_hi.astype(F32)).astype(BF16)
    hi_part = jnp.dot(tok_hi, wr_ref[...], preferred_element_type=F32)
    lo_part = jnp.dot(tok_lo, wr_ref[:, :LANES], preferred_element_type=F32)
    logits = hi_part[:, :LANES] + hi_part[:, LANES:] + lo_part + br_ref[...]
    lane = lax.broadcasted_iota(jnp.int32, logits.shape, 1)
    big = jnp.int32(LANES)
    neg_inf = jnp.float32(-jnp.inf)

    def top1(vals):
        best = jnp.max(vals, axis=-1, keepdims=True)
        idx = jnp.min(jnp.where(vals == best, lane, big), axis=-1, keepdims=True)
        return best, idx

    is_group = lane < N_GROUPS
    lg = jnp.where(is_group, logits, neg_inf)
    mg, g_sel = top1(lg)
    p_group = 1.0 / jnp.sum(jnp.where(is_group, jnp.exp(lg - mg), 0.0), axis=-1, keepdims=True)

    lo = ROUTER_LANE0 + g_sel * EXPERTS_PER_GROUP
    in_group = jnp.logical_and(lane >= lo, lane < lo + EXPERTS_PER_GROUP)
    le = jnp.where(in_group, logits, neg_inf)
    v1, i1 = top1(le)
    v2, i2 = top1(jnp.where(lane == i1, neg_inf, le))
    e2 = jnp.exp(v2 - v1)
    gate1 = p_group / (1.0 + e2)
    gate2 = p_group * e2 / (1.0 + e2)

    oh1 = lane == i1
    oh2 = lane == i2
    both = jnp.where(jnp.logical_or(oh1, oh2), 1.0, 0.0)
    r_i = lax.broadcasted_iota(jnp.int32, (tm, tm), 0)
    c_i = lax.broadcasted_iota(jnp.int32, (tm, tm), 1)
    strict_lower = jnp.where(c_i < r_i, 1.0, 0.0).astype(BF16)
    before = jnp.dot(strict_lower, both.astype(BF16), preferred_element_type=F32)
    chunks = jnp.floor((jnp.sum(both, axis=0, keepdims=True) + (CHUNK - 1)) * (1.0 / CHUNK))
    u_r = lax.broadcasted_iota(jnp.int32, (LANES, LANES), 0)
    u_c = lax.broadcasted_iota(jnp.int32, (LANES, LANES), 1)
    strict_upper = jnp.where(u_r < u_c, 1.0, 0.0).astype(BF16)
    chunks8 = jnp.broadcast_to(chunks, (8, LANES))
    first_row = CHUNK * jnp.dot(chunks8.astype(BF16), strict_upper,
                                preferred_element_type=F32)[0:1, :]
    pos = first_row + before
    pos1 = jnp.sum(jnp.where(oh1, pos, 0.0), axis=-1, keepdims=True)
    pos2 = jnp.sum(jnp.where(oh2, pos, 0.0), axis=-1, keepdims=True)
    cnt_ref[...] = jnp.where(lax.broadcasted_iota(jnp.int32, (8, LANES), 0) == 0, chunks8, 0.0)

    rec = jnp.zeros(logits.shape, F32)
    for col, val in ((R_E0, (i1 - ROUTER_LANE0).astype(F32)), (R_E1, (i2 - ROUTER_LANE0).astype(F32)),
                     (R_G0, gate1), (R_G1, gate2), (R_POS0, pos1), (R_POS1, pos2)):
        rec = jnp.where(lane == col, val, rec)
    route_ref[...] = rec

    rec_t = rec.T
    l1 = rec_t[R_POS0:R_POS0 + 1, :].astype(jnp.int32)
    l2 = rec_t[R_POS1:R_POS1 + 1, :].astype(jnp.int32)
    srow = lax.broadcasted_iota(jnp.int32, (xs_ref.shape[0], tm), 0)
    perm = jnp.where(srow == l1, 1.0, jnp.where(srow == l2, 1.0, 0.0)).astype(BF16)
    xs_ref[...] = jnp.dot(perm, tok_hi, preferred_element_type=F32).astype(BF16)


def _mix_out(x2d, attn, oconv, norm_ga, w_out_bf16, norm_gf, w_router, b_router, tail=None):
    t, d = x2d.shape
    d_attn, d_conv = attn.shape[1], oconv.shape[1]
    tm = min(ROW_TILE, t)
    nt = t // tm
    r_l = _local_rows(tm)
    has_tail = tail is not None
    tile = lambda i: jnp.minimum(i, nt - 1)
    row = lambda width: pl.BlockSpec((tm, width), lambda i: (tile(i), 0))
    full = lambda arr: pl.BlockSpec(arr.shape, lambda i: (0, 0))
    args = [x2d, attn, oconv, norm_ga, w_out_bf16, norm_gf, w_router, b_router]
    in_specs = [row(d), row(d_attn), row(d_conv)] + [full(a) for a in args[3:]]
    if has_tail:
        assert tail.shape[0] <= r_l and tail.shape[1] == d
        args.append(tail)
        in_specs.append(full(tail))
    return pl.pallas_call(
        functools.partial(_mix_out_kernel, n_tiles=nt, has_tail=has_tail),
        grid=(nt + has_tail,),
        in_specs=in_specs,
        out_specs=[row(d), row(LANES), pl.BlockSpec((r_l, d), lambda i: (i, 0)),
                   pl.BlockSpec((None, 8, LANES), lambda i: (tile(i), 0, 0))],
        out_shape=[jax.ShapeDtypeStruct((t, d), F32), jax.ShapeDtypeStruct((t, LANES), F32),
                   jax.ShapeDtypeStruct(((nt + has_tail) * r_l, d), BF16),
                   jax.ShapeDtypeStruct((nt, 8, LANES), F32)],
        compiler_params=pltpu.CompilerParams(
            dimension_semantics=("arbitrary",), vmem_limit_bytes=VMEM_LIMIT),
        name="mix_out",
    )(*args)


def _sorted_layout(tile_chunks, tile_row0, max_local, n_blocks):
    nt, n_exp = tile_chunks.shape
    cpb = CHUNKS_PER_BLOCK
    i32 = jnp.int32
    seg = jnp.sum(tile_chunks, axis=0)
    padded = (seg + cpb - 1) // cpb * cpb
    pend = jnp.cumsum(padded)
    pstart = pend - padded
    tile_incl = jnp.cumsum(tile_chunks, axis=0)
    tile_excl = tile_incl - tile_chunks
    local_incl = jnp.cumsum(tile_chunks, axis=1)
    local_excl = local_incl - tile_chunks
    base = pstart[None, :] + tile_excl

    block_first = jnp.arange(n_blocks, dtype=i32) * cpb
    block_e = jnp.minimum(jnp.sum((pend[None, :] <= block_first[:, None]).astype(i32), axis=1),
                          n_exp - 1)
    n_used = (pend[-1:] // cpb).astype(i32)

    onehot_pick = lambda onehot, table: jnp.sum(jnp.where(onehot, table, 0), axis=-1)

    is_e = block_e[:, None] == jnp.arange(n_exp, dtype=i32)[None, :]
    of_expert = lambda table_te: onehot_pick(is_e[:, None, :], table_te[None, :, :])
    incl_b, cnt_b, lexcl_b = of_expert(tile_incl), of_expert(tile_chunks), of_expert(local_excl)
    q = (block_first - onehot_pick(is_e, pstart[None, :]))[:, None] + jnp.arange(cpb, dtype=i32)
    tile_q = jnp.minimum(jnp.sum((incl_b[:, None, :] <= q[:, :, None]).astype(i32), axis=2), nt - 1)
    is_t = tile_q[:, :, None] == jnp.arange(nt, dtype=i32)[None, None, :]
    of_tile = lambda table_bt: onehot_pick(is_t, table_bt[:, None, :])
    local_chunk = of_tile(lexcl_b) + q - of_tile(incl_b - cnt_b)
    in_run = jnp.logical_and(q >= 0, q < onehot_pick(is_e, seg[None, :])[:, None])
    src_row = jnp.where(in_run, of_tile(tile_row0[None, :]) + CHUNK * local_chunk, 0)
    src_row = src_row.reshape(-1).astype(i32)

    c = jnp.arange(max_local, dtype=i32)
    e_c = jnp.minimum(jnp.sum((local_incl[:, None, :] <= c[None, :, None]).astype(i32), axis=2),
                      n_exp - 1)
    is_ec = e_c[:, :, None] == jnp.arange(n_exp, dtype=i32)[None, None, :]
    of_run = lambda table_te: onehot_pick(is_ec, table_te[:, None, :])
    global_chunk = of_run(base) + c[None, :] - of_run(local_excl)
    tile_src = jnp.where(c[None, :] < local_incl[:, -1:], CHUNK * global_chunk, 0).astype(i32)
    e_ids = jnp.arange(n_exp, dtype=i32)
    later = jnp.logical_and(seg[None, :] > 0, e_ids[None, :] > e_ids[:, None])
    next_e = jnp.min(jnp.where(later, e_ids[None, :], n_exp), axis=1)
    next_e = jnp.where(next_e < n_exp, next_e, -1).astype(i32)
    return block_e.astype(i32), n_used, src_row, tile_src, next_e


def _chunk_gather(src_ref, hbm_ref, buf, sems, item, slot, n_chunks, *, wait):
    for c in range(n_chunks):
        row = 0 if wait else pl.multiple_of(src_ref[item * n_chunks + c], CHUNK)
        copy = pltpu.make_async_copy(hbm_ref.at[pl.ds(row, CHUNK)],
                                     buf.at[slot, pl.ds(c * CHUNK, CHUNK)], sems.at[slot])
        if wait:
            copy.wait()
        else:
            copy.start()


def _prefetched(gather, step, n_items, body):
    slot = step % 2

    @pl.when(jnp.logical_and(step == 0, n_items > 0))
    def _():
        gather(0, 0, wait=False)

    @pl.when(step + 1 < n_items)
    def _():
        gather(step + 1, 1 - slot, wait=False)

    body(slot, lambda: gather(step, slot, wait=True))


def _experts_kernel(block_e_ref, n_used_ref, src_ref, next_e_ref, xs_ref, wg_hbm, wu_hbm, wd_hbm,
                    y_ref, xblk, sems, wg_f, wu_f, wd_f, wsems, wg_b, wu_b, wd_b, run_ref):
    b = pl.program_id(0)
    gather = functools.partial(_chunk_gather, src_ref, xs_ref, xblk, sems,
                               n_chunks=CHUNKS_PER_BLOCK)
    e = block_e_ref[b]
    new_expert = jnp.logical_or(b == 0, e != block_e_ref[jnp.maximum(b - 1, 0)])

    def weight_copies(expert, slot):
        return [pltpu.make_async_copy(hbm.at[expert], stage.at[slot], wsems.at[slot])
                for hbm, stage in ((wg_hbm, wg_f), (wu_hbm, wu_f), (wd_hbm, wd_f))]

    @pl.when(jnp.logical_and(new_expert, b < n_used_ref[0]))
    def _():
        @pl.when(b == 0)
        def _():
            run_ref[0] = 0
            for copy in weight_copies(e, 0):
                copy.start()

        @pl.when(b > 0)
        def _():
            run_ref[0] = run_ref[0] + 1

        slot = run_ref[0] % 2
        nxt = next_e_ref[e]

        @pl.when(nxt >= 0)
        def _():
            for copy in weight_copies(nxt, 1 - slot):
                copy.start()

        for copy in weight_copies(e, slot):
            copy.wait()
        wg_b[...] = wg_f[slot].astype(BF16)
        wu_b[...] = wu_f[slot].astype(BF16)
        wd_b[...] = wd_f[slot].astype(BF16)

    def body(slot, wait_current):
        @pl.when(b < n_used_ref[0])
        def _():
            wait_current()
            x = xblk[slot]
            gate = jnp.dot(x, wg_b[...], preferred_element_type=F32)
            up = jnp.dot(x, wu_b[...], preferred_element_type=F32)
            hid = gate * (1.0 / (1.0 + jnp.exp(-gate))) * up
            y_ref[...] = jnp.dot(hid.astype(BF16), wd_b[...],
                                 preferred_element_type=F32).astype(BF16)

        @pl.when(b >= n_used_ref[0])
        def _():
            y_ref[...] = jnp.zeros_like(y_ref)

    _prefetched(gather, b, n_used_ref[0], body)


def _experts(block_e, n_used, src_row, next_e, xs, w_gate, w_up, w_down):
    n_blocks = block_e.shape[0]
    _, d, d_exp = w_gate.shape
    blk = EXPERT_BLOCK
    any_spec = pl.BlockSpec(memory_space=pl.ANY)
    return pl.pallas_call(
        _experts_kernel,
        grid_spec=pltpu.PrefetchScalarGridSpec(
            num_scalar_prefetch=4,
            grid=(n_blocks,),
            in_specs=[any_spec] * 4,
            out_specs=pl.BlockSpec((blk, d), lambda b, be, nu, src, nxt: (b, 0)),
            scratch_shapes=[pltpu.VMEM((2, blk, d), BF16), pltpu.SemaphoreType.DMA((2,)),
                            pltpu.VMEM((2, d, d_exp), F32), pltpu.VMEM((2, d, d_exp), F32),
                            pltpu.VMEM((2, d_exp, d), F32), pltpu.SemaphoreType.DMA((2,)),
                            pltpu.VMEM((d, d_exp), BF16), pltpu.VMEM((d, d_exp), BF16),
                            pltpu.VMEM((d_exp, d), BF16), pltpu.SMEM((1,), jnp.int32)],
        ),
        out_shape=jax.ShapeDtypeStruct((n_blocks * blk, d), BF16),
        compiler_params=pltpu.CompilerParams(
            dimension_semantics=("arbitrary",), vmem_limit_bytes=VMEM_LIMIT),
        name="experts",
    )(block_e, n_used, src_row, next_e, xs, w_gate, w_up, w_down)


def _combine_kernel(src_ref, h_ref, route_ref, gn_ref, ybuf_ref, o_ref, yloc, sems):
    tm = h_ref.shape[0]
    r_l = yloc.shape[1]
    gather = functools.partial(_chunk_gather, src_ref, ybuf_ref, yloc, sems,
                               n_chunks=r_l // CHUNK)

    def body(slot, wait_current):
        wait_current()
        y = yloc[slot]
        route = route_ref[...]
        l0 = route[:, R_POS0:R_POS0 + 1].astype(jnp.int32)
        l1 = route[:, R_POS1:R_POS1 + 1].astype(jnp.int32)
        srow = lax.broadcasted_iota(jnp.int32, (tm, r_l), 1)
        gates = jnp.where(srow == l0, route[:, R_G0:R_G0 + 1],
                          jnp.where(srow == l1, route[:, R_G1:R_G1 + 1], 0.0)).astype(BF16)
        f = jnp.dot(gates, y, preferred_element_type=F32)
        o_ref[...] = _rms(h_ref[...] + f, gn_ref[...])

    _prefetched(gather, pl.program_id(0), pl.num_programs(0), body)


def _combine(tile_src, h, route, norm_g, ybuf):
    t, d = h.shape
    tm = min(ROW_TILE, t)
    r_l = _local_rows(tm)
    return pl.pallas_call(
        _combine_kernel,
        grid_spec=pltpu.PrefetchScalarGridSpec(
            num_scalar_prefetch=1,
            grid=(t // tm,),
            in_specs=[pl.BlockSpec((tm, d), lambda i, src: (i, 0)),
                      pl.BlockSpec((tm, LANES), lambda i, src: (i, 0)),
                      pl.BlockSpec((1, d), lambda i, src: (0, 0)),
                      pl.BlockSpec(memory_space=pl.ANY)],
            out_specs=pl.BlockSpec((tm, d), lambda i, src: (i, 0)),
            scratch_shapes=[pltpu.VMEM((2, r_l, d), BF16),
                            pltpu.SemaphoreType.DMA((2,))],
        ),
        out_shape=jax.ShapeDtypeStruct((t, d), F32),
        compiler_params=pltpu.CompilerParams(
            dimension_semantics=("arbitrary",), vmem_limit_bytes=VMEM_LIMIT),
        name="combine",
    )(tile_src, h, route, norm_g, ybuf)


def kernel(x_prompt, x_sample, cache_k, cache_v, state_conv, norm_mix, w_in, conv_w, norm_out_attn,
           norm_out_conv, w_out, norm_ffn, w_router_group, b_router_group, w_router_expert,
           b_router_expert, w_gate, w_up, w_down, norm_final):
    n_seq, seq_len, d = x_prompt.shape
    db, ds, _ = x_sample.shape
    depth = w_in.shape[0]
    _, _, w_buf, n_heads, dh = cache_k.shape
    d_attn = n_heads * dh
    d_conv = d - d_attn
    assert depth == 1 and ds == 1 and dh == HEAD_DIM
    assert seq_len % (max(DILATIONS) * WIN_KEYS) == 0 and seq_len <= max(DILATIONS) * WIN_KEYS
    layer = 0
    tp, ts = n_seq * seq_len, db

    xp = x_prompt.reshape(tp, d)
    xs = x_sample.reshape(ts, d)
    row = lambda vec: vec.reshape(1, -1)
    w_in_b = w_in[layer].astype(BF16)
    w_out_b = w_out[layer].astype(BF16)
    g_mix, g_oa, g_oc, g_ffn = (row(norm_mix[layer]), row(norm_out_attn[layer]),
                                row(norm_out_conv[layer]), row(norm_ffn[layer]))
    st0, st1 = state_conv[layer, :, 0, :], state_conv[layer, :, 1, :]

    qp, kp, vp, kp_t, vp_t, ocp, conv_p = _mix_in_prompt(
        xp, g_mix, w_in_b, conv_w[layer], g_oc, seq_len=seq_len, d_attn=d_attn, d_conv=d_conv)
    qs, ks, vs, ocs, us = _mix_in_sample(
        xs, g_mix, w_in_b, conv_w[layer], g_oc, st0, st1, d_attn=d_attn, d_conv=d_conv)

    attn_p = _attn_prompt(qp, kp, vp, n_seq=n_seq, seq_len=seq_len)
    heads = lambda a: a.reshape(ts, 1, d_attn)
    positions_last = lambda c: jnp.transpose(c, (0, 2, 3, 1))
    attn_s = _attn_sample(heads(qs), heads(ks), heads(vs),
                          positions_last(cache_k[layer]), positions_last(cache_v[layer]))
    attn_s = attn_s.reshape(ts, d_attn)

    n_route = N_GROUPS + N_EXPERTS
    w_router = jnp.zeros((d, LANES), F32).at[:, :N_GROUPS].set(w_router_group[layer])
    w_router = w_router.at[:, N_GROUPS:n_route].set(w_router_expert[layer])
    b_router = jnp.zeros((1, LANES), F32).at[0, :N_GROUPS].set(b_router_group[layer])
    b_router = b_router.at[0, N_GROUPS:n_route].set(b_router_expert[layer])
    w_router_hi = w_router.astype(BF16)
    w_router_lo = (w_router - w_router_hi.astype(F32)).astype(BF16)
    mix_out = functools.partial(_mix_out, norm_ga=g_oa, w_out_bf16=w_out_b, norm_gf=g_ffn,
                                w_router=jnp.concatenate([w_router_hi, w_router_lo], axis=1),
                                b_router=b_router)
    h_s, route_s, xs_s, cnt_s = mix_out(xs, attn_s, ocs)
    assert cnt_s.shape[0] == 1
    h_p, route_p, xs_all, cnt_p = mix_out(xp, attn_p, ocp, tail=xs_s)

    tile_chunks = jnp.concatenate([cnt_p[:, 0, ROUTER_LANE0:n_route],
                                   cnt_s[:, 0, ROUTER_LANE0:n_route]], axis=0).astype(jnp.int32)
    ntp, nts = cnt_p.shape[0], cnt_s.shape[0]
    tm_p, tm_s = tp // ntp, ts // nts
    rl_p, rl_s = _local_rows(tm_p), _local_rows(tm_s)
    tile_row0 = jnp.arange(ntp + nts, dtype=jnp.int32) * rl_p
    total_chunks = ntp * _max_tile_chunks(tm_p) + nts * _max_tile_chunks(tm_s)
    n_blocks = -(-(total_chunks + N_EXPERTS * (CHUNKS_PER_BLOCK - 1)) // CHUNKS_PER_BLOCK)
    block_e, n_used, src_row, tile_src, next_e = _sorted_layout(
        tile_chunks, tile_row0, rl_p // CHUNK, n_blocks)
    ybuf = _experts(block_e, n_used, src_row, next_e, xs_all, w_gate[layer], w_up[layer],
                    w_down[layer])
    g_fin = row(norm_final)
    y_p = _combine(tile_src[:ntp].reshape(-1), h_p, route_p, g_fin, ybuf)
    y_s = _combine(tile_src[ntp:, :rl_s // CHUNK].reshape(-1), h_s, route_s, g_fin, ybuf)

    w_keep = min(max(DILATIONS) * WIN_KEYS, seq_len)
    kv5 = lambda a_t: jnp.transpose(a_t.reshape(n_seq, n_heads, dh, seq_len),
                                    (0, 3, 1, 2))[None, :, seq_len - w_keep:]
    conv_s = jnp.stack([st1, us], axis=1)[None]
    kvs = lambda a: a.reshape(1, ts, 1, n_heads, dh)
    return (y_p.reshape(n_seq, seq_len, d), y_s.reshape(db, ds, d), kv5(kp_t), kv5(vp_t),
            conv_p[None], kvs(ks), kvs(vs), conv_s)
```

```python
import functools

import jax
import jax.numpy as jnp
from jax import lax
from jax.experimental import pallas as pl
from jax.experimental.pallas import tpu as pltpu

HEAD_DIM = 64
WIN_KEYS = 128
DILATIONS = (1, 4, 16)
CONV_WIDTH = 3
N_GROUPS = 4
EXPERTS_PER_GROUP = 8
N_EXPERTS = N_GROUPS * EXPERTS_PER_GROUP
EPS = 1e-6
NEG = -1e30
LOG2_E = 1.4426950408889634

LANES = 128
ROW_TILE = 512
EXPERT_BLOCK = 256
ATTN_UNROLL = 8
VMEM_LIMIT = 56 * 1024 * 1024

F32 = jnp.float32
BF16 = jnp.bfloat16


def _rms(x, g):
    return x * lax.rsqrt(jnp.mean(x * x, axis=-1, keepdims=True) + EPS) * g


def _mix_in_kernel(*refs, d_attn, d_conv, sequential):
    if sequential:
        (x_ref, g_ref, w_ref, cw_ref, gc_ref,
         q_ref, k_ref, v_ref, kt_ref, vt_ref, oc_ref, st_ref, carry_ref) = refs
    else:
        (x_ref, g_ref, w_ref, cw_ref, gc_ref, st0_ref, st1_ref,
         q_ref, k_ref, v_ref, oc_ref, u_ref) = refs
    x = x_ref[...]
    xb = _rms(x, g_ref[...]).astype(BF16)

    def proj(lo, width):
        return jnp.dot(xb, w_ref[:, lo:lo + width], preferred_element_type=F32)

    q_ref[...] = proj(0, d_attn)
    k = proj(d_attn, d_attn)
    v = proj(2 * d_attn, d_attn)
    k_ref[...] = k
    v_ref[...] = v
    gate = proj(3 * d_attn, d_conv)
    u = proj(3 * d_attn + d_conv, d_conv) * proj(3 * d_attn + 2 * d_conv, d_conv)

    tm = x.shape[0]
    if sequential:
        kt_ref[...] = k.T
        vt_ref[...] = v.T

        @pl.when(pl.program_id(1) == 0)
        def _():
            carry_ref[...] = jnp.zeros_like(carry_ref)

        row = lax.broadcasted_iota(jnp.int32, u.shape, 0)
        prev1 = carry_ref[1:2, :]
        prev2 = carry_ref[0:1, :]
        u1 = jnp.where(row == 0, prev1, pltpu.roll(u, 1, axis=0))
        u2 = jnp.where(row == 0, prev2, jnp.where(row == 1, prev1, pltpu.roll(u, 2, axis=0)))
        carry_ref[0:2, :] = u[tm - 2:tm, :]
        st_ref[...] = u[tm - 2:tm, :]
    else:
        u_ref[...] = u
        u2 = st0_ref[...]
        u1 = st1_ref[...]
    z = u2 * cw_ref[0:1, :] + u1 * cw_ref[1:2, :] + u * cw_ref[2:3, :]
    oc_ref[...] = _rms(gate * z, gc_ref[...])


def _mix_in_call(kernel, grid, in_specs, out_specs, out_shape, scratch, args):
    return pl.pallas_call(
        kernel, grid=grid, in_specs=in_specs, out_specs=out_specs, out_shape=out_shape,
        scratch_shapes=scratch,
        compiler_params=pltpu.CompilerParams(
            dimension_semantics=("arbitrary",) * len(grid), vmem_limit_bytes=VMEM_LIMIT),
        name="mix_in",
    )(*args)


def _mix_in_prompt(x2d, norm_g, w_in_bf16, conv_w, norm_gc, *, seq_len, d_attn, d_conv):
    t, d = x2d.shape
    tm = min(ROW_TILE, seq_len)
    n_seq, per = t // seq_len, seq_len // tm
    const = lambda b, s: (0, 0)
    row = lambda width: pl.BlockSpec((tm, width), lambda b, s: (b * per + s, 0))
    col = pl.BlockSpec((None, d_attn, tm), lambda b, s: (b, 0, s))
    f32 = lambda *shape: jax.ShapeDtypeStruct(shape, F32)
    return _mix_in_call(
        functools.partial(_mix_in_kernel, d_attn=d_attn, d_conv=d_conv, sequential=True),
        (n_seq, per),
        [row(d), pl.BlockSpec((1, d), const), pl.BlockSpec(w_in_bf16.shape, const),
         pl.BlockSpec((CONV_WIDTH, d_conv), const), pl.BlockSpec((1, d_conv), const)],
        [row(d_attn)] * 3 + [col] * 2 + [row(d_conv),
                                         pl.BlockSpec((None, CONV_WIDTH - 1, d_conv),
                                                      lambda b, s: (b, 0, 0))],
        [f32(t, d_attn)] * 3 + [f32(n_seq, d_attn, seq_len)] * 2
        + [f32(t, d_conv), f32(n_seq, CONV_WIDTH - 1, d_conv)],
        [pltpu.VMEM((8, d_conv), F32)],
        (x2d, norm_g, w_in_bf16, conv_w, norm_gc))


def _mix_in_sample(x2d, norm_g, w_in_bf16, conv_w, norm_gc, st0, st1, *, d_attn, d_conv):
    t, d = x2d.shape
    full = lambda arr: pl.BlockSpec(arr.shape, lambda i: (0,) * arr.ndim)
    f32 = lambda *shape: jax.ShapeDtypeStruct(shape, F32)
    args = (x2d, norm_g, w_in_bf16, conv_w, norm_gc, st0, st1)
    outs = [f32(t, d_attn)] * 3 + [f32(t, d_conv)] * 2
    return _mix_in_call(
        functools.partial(_mix_in_kernel, d_attn=d_attn, d_conv=d_conv, sequential=False),
        (1,), [full(a) for a in args], [full(o) for o in outs], outs, [], args)


def _attn_prompt_kernel(q_ref, k_ref, v_ref, o_ref, m_s, l_s, a_s, *, seq_len):
    w = WIN_KEYS
    scale = HEAD_DIM ** -0.5 * LOG2_E
    r_i = lax.broadcasted_iota(jnp.int32, (2 * w, 2 * w), 0) & (w - 1)
    c_i = lax.broadcasted_iota(jnp.int32, (2 * w, 2 * w), 1)
    mask_cur = (lax.broadcasted_iota(jnp.int32, (2 * w, w), 1)
                <= lax.broadcasted_iota(jnp.int32, (2 * w, w), 0) & (w - 1))
    mask_both = jnp.logical_and(c_i >= r_i, c_i - w <= r_i)
    first_head = lax.broadcasted_iota(jnp.int32, (w, 2 * HEAD_DIM), 1) < HEAD_DIM
    dn_t = (((1,), (1,)), ((), ()))

    def rows(start, dil):
        if dil > 1:
            return pl.ds(start, w, stride=dil)
        return pl.ds(start if isinstance(start, int) else pl.multiple_of(start, w), w)

    def run_branch(dil, first, last):
        span = dil * w
        nb = seq_len // span

        def blocks(its, with_prev):
            mask = mask_both if with_prev else mask_cur
            cur, qs, ks, vs = [], [], [], []
            for it in its:
                g = it % dil
                n = it // dil
                c = rows(g + n * span, dil)
                cur.append(c)
                qb = (q_ref[c, :] * scale).astype(BF16)
                zero = jnp.zeros_like(qb)
                qs.append(jnp.concatenate([jnp.where(first_head, qb, zero),
                                           jnp.where(first_head, zero, qb)], axis=0))
                k = k_ref[c, :].astype(BF16)
                v = v_ref[c, :].astype(BF16)
                if with_prev:
                    p = rows(g + (n - 1) * span, dil)
                    k = jnp.concatenate([k_ref[p, :].astype(BF16), k], axis=0)
                    v = jnp.concatenate([v_ref[p, :].astype(BF16), v], axis=0)
                ks.append(k)
                vs.append(v)
            scores = [lax.dot_general(q, k, dn_t, preferred_element_type=F32)
                      for q, k in zip(qs, ks)]
            ms, ps = [], []
            for s in scores:
                s = jnp.where(mask, s, NEG)
                m = jnp.max(s, axis=-1, keepdims=True)
                ms.append(m)
                ps.append(jnp.exp2(s - m).astype(BF16))
            ones = jnp.ones((ks[0].shape[0], 2 * HEAD_DIM), BF16)
            accs = [jnp.dot(p, jnp.concatenate([v, ones], axis=1), preferred_element_type=F32)
                    for p, v in zip(ps, vs)]
            for c, m, acc_l in zip(cur, ms, accs):
                acc, l = acc_l[:, :2 * HEAD_DIM], acc_l[:, 2 * HEAD_DIM:]
                m_b = jnp.where(first_head, m[:w], m[w:])
                l_b = jnp.where(first_head, l[:w], l[w:])
                a_b = jnp.where(first_head, acc[:w], acc[w:])
                if not first:
                    m_o = m_s[c, :]
                    m_n = jnp.maximum(m_o, m_b)
                    w_o = jnp.exp2(m_o - m_n)
                    w_b = jnp.exp2(m_b - m_n)
                    l_b = w_o * l_s[c, :] + w_b * l_b
                    a_b = w_o * a_s[c, :] + w_b * a_b
                    m_b = m_n
                if last:
                    o_ref[c, :] = a_b / l_b
                else:
                    m_s[c, :] = m_b
                    l_s[c, :] = l_b
                    a_s[c, :] = a_b

        def run(lo, hi, with_prev):
            u = ATTN_UNROLL
            trips = (hi - lo) // u

            def body(t, carry):
                blocks([lo + t * u + j for j in range(u)], with_prev)
                return carry

            if trips:
                lax.fori_loop(0, trips, body, 0)
            if lo + trips * u < hi:
                blocks(list(range(lo + trips * u, hi)), with_prev)

        run(0, dil, False)
        run(dil, dil * nb, True)

    order = sorted(DILATIONS, reverse=True)
    for i, dil in enumerate(order):
        run_branch(dil, i == 0, i == len(order) - 1)


def _attn_prompt(q, k, v, *, n_seq, seq_len):
    t, d_attn = q.shape
    pair = 2 * HEAD_DIM
    spec = pl.BlockSpec((seq_len, pair), lambda b, h: (b, h))
    return pl.pallas_call(
        functools.partial(_attn_prompt_kernel, seq_len=seq_len),
        grid=(n_seq, d_attn // pair),
        in_specs=[spec] * 3,
        out_specs=spec,
        out_shape=jax.ShapeDtypeStruct((t, d_attn), F32),
        scratch_shapes=[pltpu.VMEM((seq_len, pair), F32)] * 3,
        compiler_params=pltpu.CompilerParams(
            dimension_semantics=("arbitrary", "arbitrary"), vmem_limit_bytes=VMEM_LIMIT),
        name="attn_prompt",
    )(q, k, v)


def _attn_sample_kernel(q_ref, kn_ref, vn_ref, kt_ref, vt_ref, o_ref):
    n_heads, dh, w_buf = kt_ref.shape
    delta = w_buf - lax.broadcasted_iota(jnp.int32, (1, w_buf), 1)
    cnt = jnp.zeros((1, w_buf), F32)
    for dil in DILATIONS:
        assert dil & (dil - 1) == 0
        member = jnp.where(delta <= dil * WIN_KEYS, 1.0, 0.0)
        cnt = cnt + jnp.where((delta & (dil - 1)) == 0, member, 0.0)
    eye = (lax.broadcasted_iota(jnp.int32, (dh, dh), 0)
           == lax.broadcasted_iota(jnp.int32, (dh, dh), 1))
    to_col = lambda r: jnp.sum(jnp.where(eye, r, 0.0), axis=1, keepdims=True)
    to_row = lambda c: jnp.sum(jnp.where(eye, c, 0.0), axis=0, keepdims=True)
    outs = []
    for h in range(n_heads):
        sl = slice(h * dh, (h + 1) * dh)
        q = q_ref[:, sl] * (HEAD_DIM ** -0.5)
        s_self = jnp.sum(q * kn_ref[:, sl], axis=1, keepdims=True)
        s = jnp.sum(to_col(q) * kt_ref[h], axis=0, keepdims=True)
        s = jnp.where(cnt > 0.0, s, NEG)
        m = jnp.maximum(jnp.max(s, axis=1, keepdims=True), s_self)
        p = cnt * jnp.exp(s - m)
        p_self = len(DILATIONS) * jnp.exp(s_self - m)
        l = jnp.sum(p, axis=1, keepdims=True) + p_self
        acc = jnp.sum(p * vt_ref[h], axis=1, keepdims=True)
        outs.append((to_row(acc) + p_self * vn_ref[:, sl]) / l)
    o_ref[...] = jnp.concatenate(outs, axis=1)


def _attn_sample(q, k_new, v_new, cache_kt, cache_vt):
    db, n_heads, dh, w_buf = cache_kt.shape
    head_spec = pl.BlockSpec((None, 1, n_heads * dh), lambda b: (b, 0, 0))
    cache_spec = pl.BlockSpec((None, n_heads, dh, w_buf), lambda b: (b, 0, 0, 0))
    return pl.pallas_call(
        _attn_sample_kernel,
        grid=(db,),
        in_specs=[head_spec] * 3 + [cache_spec] * 2,
        out_specs=head_spec,
        out_shape=jax.ShapeDtypeStruct((db, 1, n_heads * dh), F32),
        compiler_params=pltpu.CompilerParams(
            dimension_semantics=("arbitrary",), vmem_limit_bytes=VMEM_LIMIT),
        name="attn_sample",
    )(q, k_new, v_new, cache_kt, cache_vt)


R_E0, R_E1, R_G0, R_G1, R_POS0, R_POS1 = range(6)
ROUTER_LANE0 = N_GROUPS
CHUNK = 16
CHUNKS_PER_BLOCK = EXPERT_BLOCK // CHUNK


def _max_tile_chunks(tm):
    return (2 * tm + (CHUNK - 1) * N_EXPERTS) // CHUNK


def _local_rows(tm):
    return 2 * tm + N_EXPERTS * CHUNK


def _mix_out_kernel(*refs, n_tiles, has_tail):
    if not has_tail:
        _mix_out_tile(*refs)
        return
    *tile_in, tail_ref, h_ref, route_ref, xs_ref, cnt_ref = refs

    @pl.when(pl.program_id(0) < n_tiles)
    def _():
        _mix_out_tile(*tile_in, h_ref, route_ref, xs_ref, cnt_ref)

    @pl.when(pl.program_id(0) == n_tiles)
    def _():
        rows = tail_ref.shape[0]
        xs_ref[0:rows, :] = tail_ref[...]
        xs_ref[rows:, :] = jnp.zeros((xs_ref.shape[0] - rows, xs_ref.shape[1]), xs_ref.dtype)


def _mix_out_tile(x_ref, a_ref, oc_ref, ga_ref, wo_ref, gf_ref, wr_ref, br_ref,
                  h_ref, route_ref, xs_ref, cnt_ref):
    d_attn = a_ref.shape[1]
    tm, d = x_ref.shape
    a = _rms(a_ref[...], ga_ref[...]).astype(BF16)
    mix = jnp.dot(a, wo_ref[0:d_attn, :], preferred_element_type=F32)
    mix = mix + jnp.dot(oc_ref[...].astype(BF16), wo_ref[d_attn:, :], preferred_element_type=F32)
    h = x_ref[...] + mix
    h_ref[...] = h
    tok = _rms(h, gf_ref[...])

    tok_hi = tok.astype(BF16)
    tok_lo = (tok - tok_hi.astype(F32)).astype(BF16)
    hi_part = jnp.dot(tok_hi, wr_ref[...], preferred_element_type=F32)
    lo_part = jnp.dot(tok_lo, wr_ref[:, :LANES], preferred_element_type=F32)
    logits = hi_part[:, :LANES] + hi_part[:, LANES:] + lo_part + br_ref[...]
    lane = lax.broadcasted_iota(jnp.int32, logits.shape, 1)
    big = jnp.int32(LANES)
    neg_inf = jnp.float32(-jnp.inf)

    def top1(vals):
        best = jnp.max(vals, axis=-1, keepdims=True)
        idx = jnp.min(jnp.where(vals == best, lane, big), axis=-1, keepdims=True)
        return best, idx

    is_group = lane < N_GROUPS
    lg = jnp.where(is_group, logits, neg_inf)
    mg, g_sel = top1(lg)
    p_group = 1.0 / jnp.sum(jnp.where(is_group, jnp.exp(lg - mg), 0.0), axis=-1, keepdims=True)

    lo = ROUTER_LANE0 + g_sel * EXPERTS_PER_GROUP
    in_group = jnp.logical_and(lane >= lo, lane < lo + EXPERTS_PER_GROUP)
    le = jnp.where(in_group, logits, neg_inf)
    v1, i1 = top1(le)
    v2, i2 = top1(jnp.where(lane == i1, neg_inf, le))
    e2 = jnp.exp(v2 - v1)
    gate1 = p_group / (1.0 + e2)
    gate2 = p_group * e2 / (1.0 + e2)

    oh1 = lane == i1
    oh2 = lane == i2
    both = jnp.where(jnp.logical_or(oh1, oh2), 1.0, 0.0)
    r_i = lax.broadcasted_iota(jnp.int32, (tm, tm), 0)
    c_i = lax.broadcasted_iota(jnp.int32, (tm, tm), 1)
    strict_lower = jnp.where(c_i < r_i, 1.0, 0.0).astype(BF16)
    before = jnp.dot(strict_lower, both.astype(BF16), preferred_element_type=F32)
    chunks = jnp.floor((jnp.sum(both, axis=0, keepdims=True) + (CHUNK - 1)) * (1.0 / CHUNK))
    u_r = lax.broadcasted_iota(jnp.int32, (LANES, LANES), 0)
    u_c = lax.broadcasted_iota(jnp.int32, (LANES, LANES), 1)
    strict_upper = jnp.where(u_r < u_c, 1.0, 0.0).astype(BF16)
    chunks8 = jnp.broadcast_to(chunks, (8, LANES))
    first_row = CHUNK * jnp.dot(chunks8.astype(BF16), strict_upper,
                                preferred_element_type=F32)[0:1, :]
    pos = first_row + before
    pos1 = jnp.sum(jnp.where(oh1, pos, 0.0), axis=-1, keepdims=True)
    pos2 = jnp.sum(jnp.where(oh2, pos, 0.0), axis=-1, keepdims=True)
    cnt_ref[...] = jnp.where(lax.broadcasted_iota(jnp.int32, (8, LANES), 0) == 0, chunks8, 0.0)

    rec = jnp.zeros(logits.shape, F32)
    for col, val in ((R_E0, (i1 - ROUTER_LANE0).astype(F32)), (R_E1, (i2 - ROUTER_LANE0).astype(F32)),
                     (R_G0, gate1), (R_G1, gate2), (R_POS0, pos1), (R_POS1, pos2)):
        rec = jnp.where(lane == col, val, rec)
    route_ref[...] = rec

    rec_t = rec.T
    l1 = rec_t[R_POS0:R_POS0 + 1, :].astype(jnp.int32)
    l2 = rec_t[R_POS1:R_POS1 + 1, :].astype(jnp.int32)
    srow = lax.broadcasted_iota(jnp.int32, (xs_ref.shape[0], tm), 0)
    perm = jnp.where(srow == l1, 1.0, jnp.where(srow == l2, 1.0, 0.0)).astype(BF16)
    xs_ref[...] = jnp.dot(perm, tok_hi, preferred_element_type=F32).astype(BF16)


def _mix_out(x2d, attn, oconv, norm_ga, w_out_bf16, norm_gf, w_router, b_router, tail=None):
    t, d = x2d.shape
    d_attn, d_conv = attn.shape[1], oconv.shape[1]
    tm = min(ROW_TILE, t)
    nt = t // tm
    r_l = _local_rows(tm)
    has_tail = tail is not None
    tile = lambda i: jnp.minimum(i, nt - 1)
    row = lambda width: pl.BlockSpec((tm, width), lambda i: (tile(i), 0))
    full = lambda arr: pl.BlockSpec(arr.shape, lambda i: (0, 0))
    args = [x2d, attn, oconv, norm_ga, w_out_bf16, norm_gf, w_router, b_router]
    in_specs = [row(d), row(d_attn), row(d_conv)] + [full(a) for a in args[3:]]
    if has_tail:
        assert tail.shape[0] <= r_l and tail.shape[1] == d
        args.append(tail)
        in_specs.append(full(tail))
    return pl.pallas_call(
        functools.partial(_mix_out_kernel, n_tiles=nt, has_tail=has_tail),
        grid=(nt + has_tail,),
        in_specs=in_specs,
        out_specs=[row(d), row(LANES), pl.BlockSpec((r_l, d), lambda i: (i, 0)),
                   pl.BlockSpec((None, 8, LANES), lambda i: (tile(i), 0, 0))],
        out_shape=[jax.ShapeDtypeStruct((t, d), F32), jax.ShapeDtypeStruct((t, LANES), F32),
                   jax.ShapeDtypeStruct(((nt + has_tail) * r_l, d), BF16),
                   jax.ShapeDtypeStruct((nt, 8, LANES), F32)],
        compiler_params=pltpu.CompilerParams(
            dimension_semantics=("arbitrary",), vmem_limit_bytes=VMEM_LIMIT),
        name="mix_out",
    )(*args)


def _sorted_layout(tile_chunks, tile_row0, max_local, n_blocks):
    nt, n_exp = tile_chunks.shape
    cpb = CHUNKS_PER_BLOCK
    i32 = jnp.int32
    seg = jnp.sum(tile_chunks, axis=0)
    padded = (seg + cpb - 1) // cpb * cpb
    pend = jnp.cumsum(padded)
    pstart = pend - padded
    tile_incl = jnp.cumsum(tile_chunks, axis=0)
    tile_excl = tile_incl - tile_chunks
    local_incl = jnp.cumsum(tile_chunks, axis=1)
    local_excl = local_incl - tile_chunks
    base = pstart[None, :] + tile_excl

    block_first = jnp.arange(n_blocks, dtype=i32) * cpb
    block_e = jnp.minimum(jnp.sum((pend[None, :] <= block_first[:, None]).astype(i32), axis=1),
                          n_exp - 1)
    n_used = (pend[-1:] // cpb).astype(i32)

    onehot_pick = lambda onehot, table: jnp.sum(jnp.where(onehot, table, 0), axis=-1)

    is_e = block_e[:, None] == jnp.arange(n_exp, dtype=i32)[None, :]
    of_expert = lambda table_te: onehot_pick(is_e[:, None, :], table_te[None, :, :])
    incl_b, cnt_b, lexcl_b = of_expert(tile_incl), of_expert(tile_chunks), of_expert(local_excl)
    q = (block_first - onehot_pick(is_e, pstart[None, :]))[:, None] + jnp.arange(cpb, dtype=i32)
    tile_q = jnp.minimum(jnp.sum((incl_b[:, None, :] <= q[:, :, None]).astype(i32), axis=2), nt - 1)
    is_t = tile_q[:, :, None] == jnp.arange(nt, dtype=i32)[None, None, :]
    of_tile = lambda table_bt: onehot_pick(is_t, table_bt[:, None, :])
    local_chunk = of_tile(lexcl_b) + q - of_tile(incl_b - cnt_b)
    in_run = jnp.logical_and(q >= 0, q < onehot_pick(is_e, seg[None, :])[:, None])
    src_row = jnp.where(in_run, of_tile(tile_row0[None, :]) + CHUNK * local_chunk, 0)
    src_row = src_row.reshape(-1).astype(i32)

    c = jnp.arange(max_local, dtype=i32)
    e_c = jnp.minimum(jnp.sum((local_incl[:, None, :] <= c[None, :, None]).astype(i32), axis=2),
                      n_exp - 1)
    is_ec = e_c[:, :, None] == jnp.arange(n_exp, dtype=i32)[None, None, :]
    of_run = lambda table_te: onehot_pick(is_ec, table_te[:, None, :])
    global_chunk = of_run(base) + c[None, :] - of_run(local_excl)
    tile_src = jnp.where(c[None, :] < local_incl[:, -1:], CHUNK * global_chunk, 0).astype(i32)
    e_ids = jnp.arange(n_exp, dtype=i32)
    later = jnp.logical_and(seg[None, :] > 0, e_ids[None, :] > e_ids[:, None])
    next_e = jnp.min(jnp.where(later, e_ids[None, :], n_exp), axis=1)
    next_e = jnp.where(next_e < n_exp, next_e, -1).astype(i32)
    return block_e.astype(i32), n_used, src_row, tile_src, next_e


def _chunk_gather(src_ref, hbm_ref, buf, sems, item, slot, n_chunks, *, wait):
    for c in range(n_chunks):
        row = 0 if wait else pl.multiple_of(src_ref[item * n_chunks + c], CHUNK)
        copy = pltpu.make_async_copy(hbm_ref.at[pl.ds(row, CHUNK)],
                                     buf.at[slot, pl.ds(c * CHUNK, CHUNK)], sems.at[slot])
        if wait:
            copy.wait()
        else:
            copy.start()


def _prefetched(gather, step, n_items, body):
    slot = step % 2

    @pl.when(jnp.logical_and(step == 0, n_items > 0))
    def _():
        gather(0, 0, wait=False)

    @pl.when(step + 1 < n_items)
    def _():
        gather(step + 1, 1 - slot, wait=False)

    body(slot, lambda: gather(step, slot, wait=True))


def _experts_kernel(block_e_ref, n_used_ref, src_ref, next_e_ref, xs_ref, wg_hbm, wu_hbm, wd_hbm,
                    y_hbm, xblk, sems, ybuf, ysems, wg_f, wu_f, wd_f, wsems, wg_b, wu_b, wd_b,
                    run_ref):
    rows = EXPERT_BLOCK
    n_blocks = y_hbm.shape[0] // rows
    n_used = n_used_ref[0]
    gather = functools.partial(_chunk_gather, src_ref, xs_ref, xblk, sems,
                               n_chunks=CHUNKS_PER_BLOCK)

    def weight_copies(expert, slot):
        return [pltpu.make_async_copy(hbm.at[expert], stage.at[slot], wsems.at[slot])
                for hbm, stage in ((wg_hbm, wg_f), (wu_hbm, wu_f), (wd_hbm, wd_f))]

    def y_copy(blk, slot):
        start = blk * rows if isinstance(blk, int) else pl.multiple_of(blk * rows, rows)
        return pltpu.make_async_copy(ybuf.at[slot], y_hbm.at[pl.ds(start, rows)], ysems.at[slot])

    def block(b, carry):
        e = block_e_ref[b]
        new_expert = jnp.logical_or(b == 0, e != block_e_ref[jnp.maximum(b - 1, 0)])

        @pl.when(jnp.logical_and(new_expert, b < n_used))
        def _():
            @pl.when(b == 0)
            def _():
                run_ref[0] = 0
                for copy in weight_copies(e, 0):
                    copy.start()

            @pl.when(b > 0)
            def _():
                run_ref[0] = run_ref[0] + 1

            slot = run_ref[0] % 2
            nxt = next_e_ref[e]

            @pl.when(nxt >= 0)
            def _():
                for copy in weight_copies(nxt, 1 - slot):
                    copy.start()

            for copy in weight_copies(e, slot):
                copy.wait()
            wg_b[...] = wg_f[slot].astype(BF16)
            wu_b[...] = wu_f[slot].astype(BF16)
            wd_b[...] = wd_f[slot].astype(BF16)

        def body(slot, wait_current):
            @pl.when(b >= 2)
            def _():
                y_copy(b - 2, slot).wait()

            @pl.when(b < n_used)
            def _():
                wait_current()
                x = xblk[slot]
                gate = jnp.dot(x, wg_b[...], preferred_element_type=F32)
                up = jnp.dot(x, wu_b[...], preferred_element_type=F32)
                hid = gate * (1.0 / (1.0 + jnp.exp(-gate))) * up
                ybuf[slot] = jnp.dot(hid.astype(BF16), wd_b[...],
                                     preferred_element_type=F32).astype(BF16)

            @pl.when(b >= n_used)
            def _():
                ybuf[slot] = jnp.zeros(ybuf.shape[1:], ybuf.dtype)

            y_copy(b, slot).start()

        _prefetched(gather, b, n_used, body)
        return carry

    lax.fori_loop(0, n_blocks, block, 0)
    for blk in range(max(n_blocks - 2, 0), n_blocks):
        y_copy(blk, blk % 2).wait()


def _experts(block_e, n_used, src_row, next_e, xs, w_gate, w_up, w_down):
    n_blocks = block_e.shape[0]
    _, d, d_exp = w_gate.shape
    blk = EXPERT_BLOCK
    any_spec = pl.BlockSpec(memory_space=pl.ANY)
    return pl.pallas_call(
        _experts_kernel,
        grid_spec=pltpu.PrefetchScalarGridSpec(
            num_scalar_prefetch=4,
            grid=(1,),
            in_specs=[any_spec] * 4,
            out_specs=any_spec,
            scratch_shapes=[pltpu.VMEM((2, blk, d), BF16), pltpu.SemaphoreType.DMA((2,)),
                            pltpu.VMEM((2, blk, d), BF16), pltpu.SemaphoreType.DMA((2,)),
                            pltpu.VMEM((2, d, d_exp), F32), pltpu.VMEM((2, d, d_exp), F32),
                            pltpu.VMEM((2, d_exp, d), F32), pltpu.SemaphoreType.DMA((2,)),
                            pltpu.VMEM((d, d_exp), BF16), pltpu.VMEM((d, d_exp), BF16),
                            pltpu.VMEM((d_exp, d), BF16), pltpu.SMEM((1,), jnp.int32)],
        ),
        out_shape=jax.ShapeDtypeStruct((n_blocks * blk, d), BF16),
        compiler_params=pltpu.CompilerParams(
            dimension_semantics=("arbitrary",), vmem_limit_bytes=VMEM_LIMIT),
        name="experts",
    )(block_e, n_used, src_row, next_e, xs, w_gate, w_up, w_down)


def _combine_kernel(src_ref, h_ref, route_ref, gn_ref, ybuf_ref, o_ref, yloc, sems):
    tm = h_ref.shape[0]
    r_l = yloc.shape[1]
    gather = functools.partial(_chunk_gather, src_ref, ybuf_ref, yloc, sems,
                               n_chunks=r_l // CHUNK)

    def body(slot, wait_current):
        wait_current()
        y = yloc[slot]
        route = route_ref[...]
        l0 = route[:, R_POS0:R_POS0 + 1].astype(jnp.int32)
        l1 = route[:, R_POS1:R_POS1 + 1].astype(jnp.int32)
        srow = lax.broadcasted_iota(jnp.int32, (tm, r_l), 1)
        gates = jnp.where(srow == l0, route[:, R_G0:R_G0 + 1],
                          jnp.where(srow == l1, route[:, R_G1:R_G1 + 1], 0.0)).astype(BF16)
        f = jnp.dot(gates, y, preferred_element_type=F32)
        o_ref[...] = _rms(h_ref[...] + f, gn_ref[...])

    _prefetched(gather, pl.program_id(0), pl.num_programs(0), body)


def _combine(tile_src, h, route, norm_g, ybuf):
    t, d = h.shape
    tm = min(ROW_TILE, t)
    r_l = _local_rows(tm)
    return pl.pallas_call(
        _combine_kernel,
        grid_spec=pltpu.PrefetchScalarGridSpec(
            num_scalar_prefetch=1,
            grid=(t // tm,),
            in_specs=[pl.BlockSpec((tm, d), lambda i, src: (i, 0)),
                      pl.BlockSpec((tm, LANES), lambda i, src: (i, 0)),
                      pl.BlockSpec((1, d), lambda i, src: (0, 0)),
                      pl.BlockSpec(memory_space=pl.ANY)],
            out_specs=pl.BlockSpec((tm, d), lambda i, src: (i, 0)),
            scratch_shapes=[pltpu.VMEM((2, r_l, d), BF16),
                            pltpu.SemaphoreType.DMA((2,))],
        ),
        out_shape=jax.ShapeDtypeStruct((t, d), F32),
        compiler_params=pltpu.CompilerParams(
            dimension_semantics=("arbitrary",), vmem_limit_bytes=VMEM_LIMIT),
        name="combine",
    )(tile_src, h, route, norm_g, ybuf)


def kernel(x_prompt, x_sample, cache_k, cache_v, state_conv, norm_mix, w_in, conv_w, norm_out_attn,
           norm_out_conv, w_out, norm_ffn, w_router_group, b_router_group, w_router_expert,
           b_router_expert, w_gate, w_up, w_down, norm_final):
    n_seq, seq_len, d = x_prompt.shape
    db, ds, _ = x_sample.shape
    depth = w_in.shape[0]
    _, _, w_buf, n_heads, dh = cache_k.shape
    d_attn = n_heads * dh
    d_conv = d - d_attn
    assert depth == 1 and ds == 1 and dh == HEAD_DIM
    assert seq_len % (max(DILATIONS) * WIN_KEYS) == 0 and seq_len <= max(DILATIONS) * WIN_KEYS
    layer = 0
    tp, ts = n_seq * seq_len, db

    xp = x_prompt.reshape(tp, d)
    xs = x_sample.reshape(ts, d)
    row = lambda vec: vec.reshape(1, -1)
    w_in_b = w_in[layer].astype(BF16)
    w_out_b = w_out[layer].astype(BF16)
    g_mix, g_oa, g_oc, g_ffn = (row(norm_mix[layer]), row(norm_out_attn[layer]),
                                row(norm_out_conv[layer]), row(norm_ffn[layer]))
    st0, st1 = state_conv[layer, :, 0, :], state_conv[layer, :, 1, :]

    qp, kp, vp, kp_t, vp_t, ocp, conv_p = _mix_in_prompt(
        xp, g_mix, w_in_b, conv_w[layer], g_oc, seq_len=seq_len, d_attn=d_attn, d_conv=d_conv)
    qs, ks, vs, ocs, us = _mix_in_sample(
        xs, g_mix, w_in_b, conv_w[layer], g_oc, st0, st1, d_attn=d_attn, d_conv=d_conv)

    attn_p = _attn_prompt(qp, kp, vp, n_seq=n_seq, seq_len=seq_len)
    heads = lambda a: a.reshape(ts, 1, d_attn)
    positions_last = lambda c: jnp.transpose(c, (0, 2, 3, 1))
    attn_s = _attn_sample(heads(qs), heads(ks), heads(vs),
                          positions_last(cache_k[layer]), positions_last(cache_v[layer]))
    attn_s = attn_s.reshape(ts, d_attn)

    n_route = N_GROUPS + N_EXPERTS
    w_router = jnp.zeros((d, LANES), F32).at[:, :N_GROUPS].set(w_router_group[layer])
    w_router = w_router.at[:, N_GROUPS:n_route].set(w_router_expert[layer])
    b_router = jnp.zeros((1, LANES), F32).at[0, :N_GROUPS].set(b_router_group[layer])
    b_router = b_router.at[0, N_GROUPS:n_route].set(b_router_expert[layer])
    w_router_hi = w_router.astype(BF16)
    w_router_lo = (w_router - w_router_hi.astype(F32)).astype(BF16)
    mix_out = functools.partial(_mix_out, norm_ga=g_oa, w_out_bf16=w_out_b, norm_gf=g_ffn,
                                w_router=jnp.concatenate([w_router_hi, w_router_lo], axis=1),
                                b_router=b_router)
    h_s, route_s, xs_s, cnt_s = mix_out(xs, attn_s, ocs)
    assert cnt_s.shape[0] == 1
    h_p, route_p, xs_all, cnt_p = mix_out(xp, attn_p, ocp, tail=xs_s)

    tile_chunks = jnp.concatenate([cnt_p[:, 0, ROUTER_LANE0:n_route],
                                   cnt_s[:, 0, ROUTER_LANE0:n_route]], axis=0).astype(jnp.int32)
    ntp, nts = cnt_p.shape[0], cnt_s.shape[0]
    tm_p, tm_s = tp // ntp, ts // nts
    rl_p, rl_s = _local_rows(tm_p), _local_rows(tm_s)
    tile_row0 = jnp.arange(ntp + nts, dtype=jnp.int32) * rl_p
    total_chunks = ntp * _max_tile_chunks(tm_p) + nts * _max_tile_chunks(tm_s)
    n_blocks = -(-(total_chunks + N_EXPERTS * (CHUNKS_PER_BLOCK - 1)) // CHUNKS_PER_BLOCK)
    block_e, n_used, src_row, tile_src, next_e = _sorted_layout(
        tile_chunks, tile_row0, rl_p // CHUNK, n_blocks)
    ybuf = _experts(block_e, n_used, src_row, next_e, xs_all, w_gate[layer], w_up[layer],
                    w_down[layer])
    g_fin = row(norm_final)
    y_p = _combine(tile_src[:ntp].reshape(-1), h_p, route_p, g_fin, ybuf)
    y_s = _combine(tile_src[ntp:, :rl_s // CHUNK].reshape(-1), h_s, route_s, g_fin, ybuf)

    w_keep = min(max(DILATIONS) * WIN_KEYS, seq_len)
    kv5 = lambda a_t: jnp.transpose(a_t.reshape(n_seq, n_heads, dh, seq_len),
                                    (0, 3, 1, 2))[None, :, seq_len - w_keep:]
    conv_s = jnp.stack([st1, us], axis=1)[None]
    kvs = lambda a: a.reshape(1, ts, 1, n_heads, dh)
    return (y_p.reshape(n_seq, seq_len, d), y_s.reshape(db, ds, d), kv5(kp_t), kv5(vp_t),
            conv_p[None], kvs(ks), kvs(vs), conv_s)
```

```python
import functools

import jax
import jax.numpy as jnp
from jax import lax
from jax.experimental import pallas as pl
from jax.experimental.pallas import tpu as pltpu

HEAD_DIM = 64
WIN_KEYS = 128
DILATIONS = (1, 4, 16)
CONV_WIDTH = 3
N_GROUPS = 4
EXPERTS_PER_GROUP = 8
N_EXPERTS = N_GROUPS * EXPERTS_PER_GROUP
EPS = 1e-6
NEG = -1e30
LOG2_E = 1.4426950408889634

LANES = 128
ROW_TILE = 512
EXPERT_BLOCK = 256
WEIGHT_SLABS = 8
ATTN_UNROLL = 8
VMEM_LIMIT = 56 * 1024 * 1024

F32 = jnp.float32
BF16 = jnp.bfloat16


def _rms(x, g):
    return x * lax.rsqrt(jnp.mean(x * x, axis=-1, keepdims=True) + EPS) * g


def _mix_in_kernel(*refs, d_attn, d_conv, sequential):
    if sequential:
        (x_ref, g_ref, w_ref, cw_ref, gc_ref,
         q_ref, k_ref, v_ref, kt_ref, vt_ref, oc_ref, st_ref, carry_ref) = refs
    else:
        (x_ref, g_ref, w_ref, cw_ref, gc_ref, st0_ref, st1_ref,
         q_ref, k_ref, v_ref, oc_ref, u_ref) = refs
    x = x_ref[...]
    xb = _rms(x, g_ref[...]).astype(BF16)

    def proj(lo, width):
        return jnp.dot(xb, w_ref[:, lo:lo + width], preferred_element_type=F32)

    q_ref[...] = proj(0, d_attn)
    k = proj(d_attn, d_attn)
    v = proj(2 * d_attn, d_attn)
    k_ref[...] = k
    v_ref[...] = v
    gate = proj(3 * d_attn, d_conv)
    u = proj(3 * d_attn + d_conv, d_conv) * proj(3 * d_attn + 2 * d_conv, d_conv)

    tm = x.shape[0]
    if sequential:
        kt_ref[...] = k.T
        vt_ref[...] = v.T

        @pl.when(pl.program_id(1) == 0)
        def _():
            carry_ref[...] = jnp.zeros_like(carry_ref)

        row = lax.broadcasted_iota(jnp.int32, u.shape, 0)
        prev1 = carry_ref[1:2, :]
        prev2 = carry_ref[0:1, :]
        u1 = jnp.where(row == 0, prev1, pltpu.roll(u, 1, axis=0))
        u2 = jnp.where(row == 0, prev2, jnp.where(row == 1, prev1, pltpu.roll(u, 2, axis=0)))
        carry_ref[0:2, :] = u[tm - 2:tm, :]
        st_ref[...] = u[tm - 2:tm, :]
    else:
        u_ref[...] = u
        u2 = st0_ref[...]
        u1 = st1_ref[...]
    z = u2 * cw_ref[0:1, :] + u1 * cw_ref[1:2, :] + u * cw_ref[2:3, :]
    oc_ref[...] = _rms(gate * z, gc_ref[...])


def _mix_in_call(kernel, grid, in_specs, out_specs, out_shape, scratch, args):
    return pl.pallas_call(
        kernel, grid=grid, in_specs=in_specs, out_specs=out_specs, out_shape=out_shape,
        scratch_shapes=scratch,
        compiler_params=pltpu.CompilerParams(
            dimension_semantics=("arbitrary",) * len(grid), vmem_limit_bytes=VMEM_LIMIT),
        name="mix_in",
    )(*args)


def _mix_in_prompt(x2d, norm_g, w_in_bf16, conv_w, norm_gc, *, seq_len, d_attn, d_conv):
    t, d = x2d.shape
    tm = min(ROW_TILE, seq_len)
    n_seq, per = t // seq_len, seq_len // tm
    const = lambda b, s: (0, 0)
    row = lambda width: pl.BlockSpec((tm, width), lambda b, s: (b * per + s, 0))
    col = pl.BlockSpec((None, d_attn, tm), lambda b, s: (b, 0, s))
    f32 = lambda *shape: jax.ShapeDtypeStruct(shape, F32)
    return _mix_in_call(
        functools.partial(_mix_in_kernel, d_attn=d_attn, d_conv=d_conv, sequential=True),
        (n_seq, per),
        [row(d), pl.BlockSpec((1, d), const), pl.BlockSpec(w_in_bf16.shape, const),
         pl.BlockSpec((CONV_WIDTH, d_conv), const), pl.BlockSpec((1, d_conv), const)],
        [row(d_attn)] * 3 + [col] * 2 + [row(d_conv),
                                         pl.BlockSpec((None, CONV_WIDTH - 1, d_conv),
                                                      lambda b, s: (b, 0, 0))],
        [f32(t, d_attn)] * 3 + [f32(n_seq, d_attn, seq_len)] * 2
        + [f32(t, d_conv), f32(n_seq, CONV_WIDTH - 1, d_conv)],
        [pltpu.VMEM((8, d_conv), F32)],
        (x2d, norm_g, w_in_bf16, conv_w, norm_gc))


def _mix_in_sample(x2d, norm_g, w_in_bf16, conv_w, norm_gc, st0, st1, *, d_attn, d_conv):
    t, d = x2d.shape
    full = lambda arr: pl.BlockSpec(arr.shape, lambda i: (0,) * arr.ndim)
    f32 = lambda *shape: jax.ShapeDtypeStruct(shape, F32)
    args = (x2d, norm_g, w_in_bf16, conv_w, norm_gc, st0, st1)
    outs = [f32(t, d_attn)] * 3 + [f32(t, d_conv)] * 2
    return _mix_in_call(
        functools.partial(_mix_in_kernel, d_attn=d_attn, d_conv=d_conv, sequential=False),
        (1,), [full(a) for a in args], [full(o) for o in outs], outs, [], args)


def _attn_prompt_kernel(q_ref, k_ref, v_ref, o_ref, m_s, l_s, a_s, *, seq_len):
    w = WIN_KEYS
    scale = HEAD_DIM ** -0.5 * LOG2_E
    r_i = lax.broadcasted_iota(jnp.int32, (2 * w, 2 * w), 0) & (w - 1)
    c_i = lax.broadcasted_iota(jnp.int32, (2 * w, 2 * w), 1)
    mask_cur = (lax.broadcasted_iota(jnp.int32, (2 * w, w), 1)
                <= lax.broadcasted_iota(jnp.int32, (2 * w, w), 0) & (w - 1))
    mask_both = jnp.logical_and(c_i >= r_i, c_i - w <= r_i)
    first_head = lax.broadcasted_iota(jnp.int32, (w, 2 * HEAD_DIM), 1) < HEAD_DIM
    dn_t = (((1,), (1,)), ((), ()))

    def rows(start, dil):
        if dil > 1:
            return pl.ds(start, w, stride=dil)
        return pl.ds(start if isinstance(start, int) else pl.multiple_of(start, w), w)

    def run_branch(dil, first, last):
        span = dil * w
        nb = seq_len // span

        def blocks(its, with_prev):
            mask = mask_both if with_prev else mask_cur
            cur, qs, ks, vs = [], [], [], []
            for it in its:
                g = it % dil
                n = it // dil
                c = rows(g + n * span, dil)
                cur.append(c)
                qb = (q_ref[c, :] * scale).astype(BF16)
                zero = jnp.zeros_like(qb)
                qs.append(jnp.concatenate([jnp.where(first_head, qb, zero),
                                           jnp.where(first_head, zero, qb)], axis=0))
                k = k_ref[c, :].astype(BF16)
                v = v_ref[c, :].astype(BF16)
                if with_prev:
                    p = rows(g + (n - 1) * span, dil)
                    k = jnp.concatenate([k_ref[p, :].astype(BF16), k], axis=0)
                    v = jnp.concatenate([v_ref[p, :].astype(BF16), v], axis=0)
                ks.append(k)
                vs.append(v)
            scores = [lax.dot_general(q, k, dn_t, preferred_element_type=F32)
                      for q, k in zip(qs, ks)]
            ms, ps = [], []
            for s in scores:
                s = jnp.where(mask, s, NEG)
                m = jnp.max(s, axis=-1, keepdims=True)
                ms.append(m)
                ps.append(jnp.exp2(s - m).astype(BF16))
            ones = jnp.ones((ks[0].shape[0], 2 * HEAD_DIM), BF16)
            accs = [jnp.dot(p, jnp.concatenate([v, ones], axis=1), preferred_element_type=F32)
                    for p, v in zip(ps, vs)]
            for c, m, acc_l in zip(cur, ms, accs):
                acc, l = acc_l[:, :2 * HEAD_DIM], acc_l[:, 2 * HEAD_DIM:]
                m_b = jnp.where(first_head, m[:w], m[w:])
                l_b = jnp.where(first_head, l[:w], l[w:])
                a_b = jnp.where(first_head, acc[:w], acc[w:])
                if not first:
                    m_o = m_s[c, :]
                    m_n = jnp.maximum(m_o, m_b)
                    w_o = jnp.exp2(m_o - m_n)
                    w_b = jnp.exp2(m_b - m_n)
                    l_b = w_o * l_s[c, :] + w_b * l_b
                    a_b = w_o * a_s[c, :] + w_b * a_b
                    m_b = m_n
                if last:
                    o_ref[c, :] = a_b / l_b
                else:
                    m_s[c, :] = m_b
                    l_s[c, :] = l_b
                    a_s[c, :] = a_b

        def run(lo, hi, with_prev):
            u = ATTN_UNROLL
            trips = (hi - lo) // u

            def body(t, carry):
                blocks([lo + t * u + j for j in range(u)], with_prev)
                return carry

            if trips:
                lax.fori_loop(0, trips, body, 0)
            if lo + trips * u < hi:
                blocks(list(range(lo + trips * u, hi)), with_prev)

        run(0, dil, False)
        run(dil, dil * nb, True)

    order = sorted(DILATIONS, reverse=True)
    for i, dil in enumerate(order):
        run_branch(dil, i == 0, i == len(order) - 1)


def _attn_prompt(q, k, v, *, n_seq, seq_len):
    t, d_attn = q.shape
    pair = 2 * HEAD_DIM
    spec = pl.BlockSpec((seq_len, pair), lambda b, h: (b, h))
    return pl.pallas_call(
        functools.partial(_attn_prompt_kernel, seq_len=seq_len),
        grid=(n_seq, d_attn // pair),
        in_specs=[spec] * 3,
        out_specs=spec,
        out_shape=jax.ShapeDtypeStruct((t, d_attn), F32),
        scratch_shapes=[pltpu.VMEM((seq_len, pair), F32)] * 3,
        compiler_params=pltpu.CompilerParams(
            dimension_semantics=("arbitrary", "arbitrary"), vmem_limit_bytes=VMEM_LIMIT),
        name="attn_prompt",
    )(q, k, v)


def _attn_sample_kernel(q_ref, kn_ref, vn_ref, kt_ref, vt_ref, o_ref):
    n_heads, dh, w_buf = kt_ref.shape
    delta = w_buf - lax.broadcasted_iota(jnp.int32, (1, w_buf), 1)
    cnt = jnp.zeros((1, w_buf), F32)
    for dil in DILATIONS:
        assert dil & (dil - 1) == 0
        member = jnp.where(delta <= dil * WIN_KEYS, 1.0, 0.0)
        cnt = cnt + jnp.where((delta & (dil - 1)) == 0, member, 0.0)
    eye = (lax.broadcasted_iota(jnp.int32, (dh, dh), 0)
           == lax.broadcasted_iota(jnp.int32, (dh, dh), 1))
    to_col = lambda r: jnp.sum(jnp.where(eye, r, 0.0), axis=1, keepdims=True)
    to_row = lambda c: jnp.sum(jnp.where(eye, c, 0.0), axis=0, keepdims=True)
    outs = []
    for h in range(n_heads):
        sl = slice(h * dh, (h + 1) * dh)
        q = q_ref[:, sl] * (HEAD_DIM ** -0.5)
        s_self = jnp.sum(q * kn_ref[:, sl], axis=1, keepdims=True)
        s = jnp.sum(to_col(q) * kt_ref[h], axis=0, keepdims=True)
        s = jnp.where(cnt > 0.0, s, NEG)
        m = jnp.maximum(jnp.max(s, axis=1, keepdims=True), s_self)
        p = cnt * jnp.exp(s - m)
        p_self = len(DILATIONS) * jnp.exp(s_self - m)
        l = jnp.sum(p, axis=1, keepdims=True) + p_self
        acc = jnp.sum(p * vt_ref[h], axis=1, keepdims=True)
        outs.append((to_row(acc) + p_self * vn_ref[:, sl]) / l)
    o_ref[...] = jnp.concatenate(outs, axis=1)


def _attn_sample(q, k_new, v_new, cache_kt, cache_vt):
    db, n_heads, dh, w_buf = cache_kt.shape
    head_spec = pl.BlockSpec((None, 1, n_heads * dh), lambda b: (b, 0, 0))
    cache_spec = pl.BlockSpec((None, n_heads, dh, w_buf), lambda b: (b, 0, 0, 0))
    return pl.pallas_call(
        _attn_sample_kernel,
        grid=(db,),
        in_specs=[head_spec] * 3 + [cache_spec] * 2,
        out_specs=head_spec,
        out_shape=jax.ShapeDtypeStruct((db, 1, n_heads * dh), F32),
        compiler_params=pltpu.CompilerParams(
            dimension_semantics=("arbitrary",), vmem_limit_bytes=VMEM_LIMIT),
        name="attn_sample",
    )(q, k_new, v_new, cache_kt, cache_vt)


R_E0, R_E1, R_G0, R_G1, R_POS0, R_POS1 = range(6)
ROUTER_LANE0 = N_GROUPS
CHUNK = 16
CHUNKS_PER_BLOCK = EXPERT_BLOCK // CHUNK


def _max_tile_chunks(tm):
    return (2 * tm + (CHUNK - 1) * N_EXPERTS) // CHUNK


def _local_rows(tm):
    return 2 * tm + N_EXPERTS * CHUNK


def _mix_out_kernel(*refs, n_tiles, has_tail):
    if not has_tail:
        _mix_out_tile(*refs)
        return
    *tile_in, tail_ref, h_ref, route_ref, xs_ref, cnt_ref = refs

    @pl.when(pl.program_id(0) < n_tiles)
    def _():
        _mix_out_tile(*tile_in, h_ref, route_ref, xs_ref, cnt_ref)

    @pl.when(pl.program_id(0) == n_tiles)
    def _():
        rows = tail_ref.shape[0]
        xs_ref[0:rows, :] = tail_ref[...]
        xs_ref[rows:, :] = jnp.zeros((xs_ref.shape[0] - rows, xs_ref.shape[1]), xs_ref.dtype)


def _mix_out_tile(x_ref, a_ref, oc_ref, ga_ref, wo_ref, gf_ref, wr_ref, br_ref,
                  h_ref, route_ref, xs_ref, cnt_ref):
    d_attn = a_ref.shape[1]
    tm, d = x_ref.shape
    a = _rms(a_ref[...], ga_ref[...]).astype(BF16)
    mix = jnp.dot(a, wo_ref[0:d_attn, :], preferred_element_type=F32)
    mix = mix + jnp.dot(oc_ref[...].astype(BF16), wo_ref[d_attn:, :], preferred_element_type=F32)
    h = x_ref[...] + mix
    h_ref[...] = h
    tok = _rms(h, gf_ref[...])

    tok_hi = tok.astype(BF16)
    tok_lo = (tok - tok_hi.astype(F32)).astype(BF16)
    hi_part = jnp.dot(tok_hi, wr_ref[...], preferred_element_type=F32)
    lo_part = jnp.dot(tok_lo, wr_ref[:, :LANES], preferred_element_type=F32)
    logits = hi_part[:, :LANES] + hi_part[:, LANES:] + lo_part + br_ref[...]
    lane = lax.broadcasted_iota(jnp.int32, logits.shape, 1)
    big = jnp.int32(LANES)
    neg_inf = jnp.float32(-jnp.inf)

    def top1(vals):
        best = jnp.max(vals, axis=-1, keepdims=True)
        idx = jnp.min(jnp.where(vals == best, lane, big), axis=-1, keepdims=True)
        return best, idx

    is_group = lane < N_GROUPS
    lg = jnp.where(is_group, logits, neg_inf)
    mg, g_sel = top1(lg)
    p_group = 1.0 / jnp.sum(jnp.where(is_group, jnp.exp(lg - mg), 0.0), axis=-1, keepdims=True)

    lo = ROUTER_LANE0 + g_sel * EXPERTS_PER_GROUP
    in_group = jnp.logical_and(lane >= lo, lane < lo + EXPERTS_PER_GROUP)
    le = jnp.where(in_group, logits, neg_inf)
    v1, i1 = top1(le)
    v2, i2 = top1(jnp.where(lane == i1, neg_inf, le))
    e2 = jnp.exp(v2 - v1)
    gate1 = p_group / (1.0 + e2)
    gate2 = p_group * e2 / (1.0 + e2)

    oh1 = lane == i1
    oh2 = lane == i2
    both = jnp.where(jnp.logical_or(oh1, oh2), 1.0, 0.0)
    r_i = lax.broadcasted_iota(jnp.int32, (tm, tm), 0)
    c_i = lax.broadcasted_iota(jnp.int32, (tm, tm), 1)
    strict_lower = jnp.where(c_i < r_i, 1.0, 0.0).astype(BF16)
    before = jnp.dot(strict_lower, both.astype(BF16), preferred_element_type=F32)
    chunks = jnp.floor((jnp.sum(both, axis=0, keepdims=True) + (CHUNK - 1)) * (1.0 / CHUNK))
    u_r = lax.broadcasted_iota(jnp.int32, (LANES, LANES), 0)
    u_c = lax.broadcasted_iota(jnp.int32, (LANES, LANES), 1)
    strict_upper = jnp.where(u_r < u_c, 1.0, 0.0).astype(BF16)
    chunks8 = jnp.broadcast_to(chunks, (8, LANES))
    first_row = CHUNK * jnp.dot(chunks8.astype(BF16), strict_upper,
                                preferred_element_type=F32)[0:1, :]
    pos = first_row + before
    pos1 = jnp.sum(jnp.where(oh1, pos, 0.0), axis=-1, keepdims=True)
    pos2 = jnp.sum(jnp.where(oh2, pos, 0.0), axis=-1, keepdims=True)
    cnt_ref[...] = jnp.where(lax.broadcasted_iota(jnp.int32, (8, LANES), 0) == 0, chunks8, 0.0)

    rec = jnp.zeros(logits.shape, F32)
    for col, val in ((R_E0, (i1 - ROUTER_LANE0).astype(F32)), (R_E1, (i2 - ROUTER_LANE0).astype(F32)),
                     (R_G0, gate1), (R_G1, gate2), (R_POS0, pos1), (R_POS1, pos2)):
        rec = jnp.where(lane == col, val, rec)
    route_ref[...] = rec

    rec_t = rec.T
    l1 = rec_t[R_POS0:R_POS0 + 1, :].astype(jnp.int32)
    l2 = rec_t[R_POS1:R_POS1 + 1, :].astype(jnp.int32)
    srow = lax.broadcasted_iota(jnp.int32, (xs_ref.shape[0], tm), 0)
    perm = jnp.where(srow == l1, 1.0, jnp.where(srow == l2, 1.0, 0.0)).astype(BF16)
    xs_ref[...] = jnp.dot(perm, tok_hi, preferred_element_type=F32).astype(BF16)


def _mix_out(x2d, attn, oconv, norm_ga, w_out_bf16, norm_gf, w_router, b_router, tail=None):
    t, d = x2d.shape
    d_attn, d_conv = attn.shape[1], oconv.shape[1]
    tm = min(ROW_TILE, t)
    nt = t // tm
    r_l = _local_rows(tm)
    has_tail = tail is not None
    tile = lambda i: jnp.minimum(i, nt - 1)
    row = lambda width: pl.BlockSpec((tm, width), lambda i: (tile(i), 0))
    full = lambda arr: pl.BlockSpec(arr.shape, lambda i: (0, 0))
    args = [x2d, attn, oconv, norm_ga, w_out_bf16, norm_gf, w_router, b_router]
    in_specs = [row(d), row(d_attn), row(d_conv)] + [full(a) for a in args[3:]]
    if has_tail:
        assert tail.shape[0] <= r_l and tail.shape[1] == d
        args.append(tail)
        in_specs.append(full(tail))
    return pl.pallas_call(
        functools.partial(_mix_out_kernel, n_tiles=nt, has_tail=has_tail),
        grid=(nt + has_tail,),
        in_specs=in_specs,
        out_specs=[row(d), row(LANES), pl.BlockSpec((r_l, d), lambda i: (i, 0)),
                   pl.BlockSpec((None, 8, LANES), lambda i: (tile(i), 0, 0))],
        out_shape=[jax.ShapeDtypeStruct((t, d), F32), jax.ShapeDtypeStruct((t, LANES), F32),
                   jax.ShapeDtypeStruct(((nt + has_tail) * r_l, d), BF16),
                   jax.ShapeDtypeStruct((nt, 8, LANES), F32)],
        compiler_params=pltpu.CompilerParams(
            dimension_semantics=("arbitrary",), vmem_limit_bytes=VMEM_LIMIT),
        name="mix_out",
    )(*args)


def _sorted_layout(tile_chunks, tile_row0, max_local, n_blocks):
    nt, n_exp = tile_chunks.shape
    cpb = CHUNKS_PER_BLOCK
    i32 = jnp.int32
    seg = jnp.sum(tile_chunks, axis=0)
    padded = (seg + cpb - 1) // cpb * cpb
    pend = jnp.cumsum(padded)
    pstart = pend - padded
    tile_incl = jnp.cumsum(tile_chunks, axis=0)
    tile_excl = tile_incl - tile_chunks
    local_incl = jnp.cumsum(tile_chunks, axis=1)
    local_excl = local_incl - tile_chunks
    base = pstart[None, :] + tile_excl

    block_first = jnp.arange(n_blocks, dtype=i32) * cpb
    block_e = jnp.minimum(jnp.sum((pend[None, :] <= block_first[:, None]).astype(i32), axis=1),
                          n_exp - 1)
    n_used = (pend[-1:] // cpb).astype(i32)

    onehot_pick = lambda onehot, table: jnp.sum(jnp.where(onehot, table, 0), axis=-1)

    is_e = block_e[:, None] == jnp.arange(n_exp, dtype=i32)[None, :]
    of_expert = lambda table_te: onehot_pick(is_e[:, None, :], table_te[None, :, :])
    incl_b, cnt_b, lexcl_b = of_expert(tile_incl), of_expert(tile_chunks), of_expert(local_excl)
    q = (block_first - onehot_pick(is_e, pstart[None, :]))[:, None] + jnp.arange(cpb, dtype=i32)
    tile_q = jnp.minimum(jnp.sum((incl_b[:, None, :] <= q[:, :, None]).astype(i32), axis=2), nt - 1)
    is_t = tile_q[:, :, None] == jnp.arange(nt, dtype=i32)[None, None, :]
    of_tile = lambda table_bt: onehot_pick(is_t, table_bt[:, None, :])
    local_chunk = of_tile(lexcl_b) + q - of_tile(incl_b - cnt_b)
    in_run = jnp.logical_and(q >= 0, q < onehot_pick(is_e, seg[None, :])[:, None])
    src_row = jnp.where(in_run, of_tile(tile_row0[None, :]) + CHUNK * local_chunk, 0)
    src_row = src_row.reshape(-1).astype(i32)

    c = jnp.arange(max_local, dtype=i32)
    e_c = jnp.minimum(jnp.sum((local_incl[:, None, :] <= c[None, :, None]).astype(i32), axis=2),
                      n_exp - 1)
    is_ec = e_c[:, :, None] == jnp.arange(n_exp, dtype=i32)[None, None, :]
    of_run = lambda table_te: onehot_pick(is_ec, table_te[:, None, :])
    global_chunk = of_run(base) + c[None, :] - of_run(local_excl)
    tile_src = jnp.where(c[None, :] < local_incl[:, -1:], CHUNK * global_chunk, 0).astype(i32)
    e_ids = jnp.arange(n_exp, dtype=i32)
    later = jnp.logical_and(seg[None, :] > 0, e_ids[None, :] > e_ids[:, None])
    next_e = jnp.min(jnp.where(later, e_ids[None, :], n_exp), axis=1)
    next_e = jnp.where(next_e < n_exp, next_e, -1).astype(i32)
    return block_e.astype(i32), n_used, src_row, tile_src, next_e


def _chunk_gather(src_ref, hbm_ref, buf, sems, item, slot, n_chunks, *, wait):
    for c in range(n_chunks):
        row = 0 if wait else pl.multiple_of(src_ref[item * n_chunks + c], CHUNK)
        copy = pltpu.make_async_copy(hbm_ref.at[pl.ds(row, CHUNK)],
                                     buf.at[slot, pl.ds(c * CHUNK, CHUNK)], sems.at[slot])
        if wait:
            copy.wait()
        else:
            copy.start()


def _prefetched(gather, step, n_items, body):
    slot = step % 2

    @pl.when(jnp.logical_and(step == 0, n_items > 0))
    def _():
        gather(0, 0, wait=False)

    @pl.when(step + 1 < n_items)
    def _():
        gather(step + 1, 1 - slot, wait=False)

    body(slot, lambda: gather(step, slot, wait=True))


def _experts_kernel(block_e_ref, n_used_ref, src_ref, next_e_ref, xs_ref, wg_hbm, wu_hbm, wd_hbm,
                    y_hbm, xblk, sems, ybuf, ysems, wg_f, wu_f, wd_f, wsems, wg_b, wu_b, wd_b,
                    run_ref):
    rows = EXPERT_BLOCK
    n_blocks = y_hbm.shape[0] // rows
    n_used = n_used_ref[0]
    gather = functools.partial(_chunk_gather, src_ref, xs_ref, xblk, sems,
                               n_chunks=CHUNKS_PER_BLOCK)

    def weight_copies(expert, slot):
        copies = []
        for hbm, stage in ((wg_hbm, wg_f), (wu_hbm, wu_f), (wd_hbm, wd_f)):
            slab = hbm.shape[1] // WEIGHT_SLABS
            for i in range(WEIGHT_SLABS):
                rows_i = pl.ds(i * slab, slab)
                copies.append(pltpu.make_async_copy(hbm.at[expert, rows_i], stage.at[slot, rows_i],
                                                    wsems.at[slot]))
        return copies

    def y_copy(blk, slot):
        start = blk * rows if isinstance(blk, int) else pl.multiple_of(blk * rows, rows)
        return pltpu.make_async_copy(ybuf.at[slot], y_hbm.at[pl.ds(start, rows)], ysems.at[slot])

    def block(b, carry):
        e = block_e_ref[b]
        new_expert = jnp.logical_or(b == 0, e != block_e_ref[jnp.maximum(b - 1, 0)])

        @pl.when(jnp.logical_and(new_expert, b < n_used))
        def _():
            @pl.when(b == 0)
            def _():
                run_ref[0] = 0
                for copy in weight_copies(e, 0):
                    copy.start()

            @pl.when(b > 0)
            def _():
                run_ref[0] = run_ref[0] + 1

            slot = run_ref[0] % 2
            nxt = next_e_ref[e]

            @pl.when(nxt >= 0)
            def _():
                for copy in weight_copies(nxt, 1 - slot):
                    copy.start()

            for copy in weight_copies(e, slot):
                copy.wait()
            wg_b[...] = wg_f[slot].astype(BF16)
            wu_b[...] = wu_f[slot].astype(BF16)
            wd_b[...] = wd_f[slot].astype(BF16)

        def body(slot, wait_current):
            @pl.when(b >= 2)
            def _():
                y_copy(b - 2, slot).wait()

            @pl.when(b < n_used)
            def _():
                wait_current()
                x = xblk[slot]
                gate = jnp.dot(x, wg_b[...], preferred_element_type=F32)
                up = jnp.dot(x, wu_b[...], preferred_element_type=F32)
                hid = gate * (1.0 / (1.0 + jnp.exp(-gate))) * up
                ybuf[slot] = jnp.dot(hid.astype(BF16), wd_b[...],
                                     preferred_element_type=F32).astype(BF16)

            @pl.when(b >= n_used)
            def _():
                ybuf[slot] = jnp.zeros(ybuf.shape[1:], ybuf.dtype)

            y_copy(b, slot).start()

        _prefetched(gather, b, n_used, body)
        return carry

    lax.fori_loop(0, n_blocks, block, 0)
    for blk in range(max(n_blocks - 2, 0), n_blocks):
        y_copy(blk, blk % 2).wait()


def _experts(block_e, n_used, src_row, next_e, xs, w_gate, w_up, w_down):
    n_blocks = block_e.shape[0]
    _, d, d_exp = w_gate.shape
    blk = EXPERT_BLOCK
    any_spec = pl.BlockSpec(memory_space=pl.ANY)
    return pl.pallas_call(
        _experts_kernel,
        grid_spec=pltpu.PrefetchScalarGridSpec(
            num_scalar_prefetch=4,
            grid=(1,),
            in_specs=[any_spec] * 4,
            out_specs=any_spec,
            scratch_shapes=[pltpu.VMEM((2, blk, d), BF16), pltpu.SemaphoreType.DMA((2,)),
                            pltpu.VMEM((2, blk, d), BF16), pltpu.SemaphoreType.DMA((2,)),
                            pltpu.VMEM((2, d, d_exp), F32), pltpu.VMEM((2, d, d_exp), F32),
                            pltpu.VMEM((2, d_exp, d), F32), pltpu.SemaphoreType.DMA((2,)),
                            pltpu.VMEM((d, d_exp), BF16), pltpu.VMEM((d, d_exp), BF16),
                            pltpu.VMEM((d_exp, d), BF16), pltpu.SMEM((1,), jnp.int32)],
        ),
        out_shape=jax.ShapeDtypeStruct((n_blocks * blk, d), BF16),
        compiler_params=pltpu.CompilerParams(
            dimension_semantics=("arbitrary",), vmem_limit_bytes=VMEM_LIMIT),
        name="experts",
    )(block_e, n_used, src_row, next_e, xs, w_gate, w_up, w_down)


def _combine_kernel(src_ref, h_ref, route_ref, gn_ref, ybuf_ref, o_ref, yloc, sems):
    tm = h_ref.shape[0]
    r_l = yloc.shape[1]
    gather = functools.partial(_chunk_gather, src_ref, ybuf_ref, yloc, sems,
                               n_chunks=r_l // CHUNK)

    def body(slot, wait_current):
        wait_current()
        y = yloc[slot]
        route = route_ref[...]
        l0 = route[:, R_POS0:R_POS0 + 1].astype(jnp.int32)
        l1 = route[:, R_POS1:R_POS1 + 1].astype(jnp.int32)
        srow = lax.broadcasted_iota(jnp.int32, (tm, r_l), 1)
        gates = jnp.where(srow == l0, route[:, R_G0:R_G0 + 1],
                          jnp.where(srow == l1, route[:, R_G1:R_G1 + 1], 0.0)).astype(BF16)
        f = jnp.dot(gates, y, preferred_element_type=F32)
        o_ref[...] = _rms(h_ref[...] + f, gn_ref[...])

    _prefetched(gather, pl.program_id(0), pl.num_programs(0), body)


def _combine(tile_src, h, route, norm_g, ybuf):
    t, d = h.shape
    tm = min(ROW_TILE, t)
    r_l = _local_rows(tm)
    return pl.pallas_call(
        _combine_kernel,
        grid_spec=pltpu.PrefetchScalarGridSpec(
            num_scalar_prefetch=1,
            grid=(t // tm,),
            in_specs=[pl.BlockSpec((tm, d), lambda i, src: (i, 0)),
                      pl.BlockSpec((tm, LANES), lambda i, src: (i, 0)),
                      pl.BlockSpec((1, d), lambda i, src: (0, 0)),
                      pl.BlockSpec(memory_space=pl.ANY)],
            out_specs=pl.BlockSpec((tm, d), lambda i, src: (i, 0)),
            scratch_shapes=[pltpu.VMEM((2, r_l, d), BF16),
                            pltpu.SemaphoreType.DMA((2,))],
        ),
        out_shape=jax.ShapeDtypeStruct((t, d), F32),
        compiler_params=pltpu.CompilerParams(
            dimension_semantics=("arbitrary",), vmem_limit_bytes=VMEM_LIMIT),
        name="combine",
    )(tile_src, h, route, norm_g, ybuf)


def kernel(x_prompt, x_sample, cache_k, cache_v, state_conv, norm_mix, w_in, conv_w, norm_out_attn,
           norm_out_conv, w_out, norm_ffn, w_router_group, b_router_group, w_router_expert,
           b_router_expert, w_gate, w_up, w_down, norm_final):
    n_seq, seq_len, d = x_prompt.shape
    db, ds, _ = x_sample.shape
    depth = w_in.shape[0]
    _, _, w_buf, n_heads, dh = cache_k.shape
    d_attn = n_heads * dh
    d_conv = d - d_attn
    assert depth == 1 and ds == 1 and dh == HEAD_DIM
    assert seq_len % (max(DILATIONS) * WIN_KEYS) == 0 and seq_len <= max(DILATIONS) * WIN_KEYS
    layer = 0
    tp, ts = n_seq * seq_len, db

    xp = x_prompt.reshape(tp, d)
    xs = x_sample.reshape(ts, d)
    row = lambda vec: vec.reshape(1, -1)
    w_in_b = w_in[layer].astype(BF16)
    w_out_b = w_out[layer].astype(BF16)
    g_mix, g_oa, g_oc, g_ffn = (row(norm_mix[layer]), row(norm_out_attn[layer]),
                                row(norm_out_conv[layer]), row(norm_ffn[layer]))
    st0, st1 = state_conv[layer, :, 0, :], state_conv[layer, :, 1, :]

    qp, kp, vp, kp_t, vp_t, ocp, conv_p = _mix_in_prompt(
        xp, g_mix, w_in_b, conv_w[layer], g_oc, seq_len=seq_len, d_attn=d_attn, d_conv=d_conv)
    qs, ks, vs, ocs, us = _mix_in_sample(
        xs, g_mix, w_in_b, conv_w[layer], g_oc, st0, st1, d_attn=d_attn, d_conv=d_conv)

    attn_p = _attn_prompt(qp, kp, vp, n_seq=n_seq, seq_len=seq_len)
    heads = lambda a: a.reshape(ts, 1, d_attn)
    positions_last = lambda c: jnp.transpose(c, (0, 2, 3, 1))
    attn_s = _attn_sample(heads(qs), heads(ks), heads(vs),
                          positions_last(cache_k[layer]), positions_last(cache_v[layer]))
    attn_s = attn_s.reshape(ts, d_attn)

    n_route = N_GROUPS + N_EXPERTS
    w_router = jnp.zeros((d, LANES), F32).at[:, :N_GROUPS].set(w_router_group[layer])
    w_router = w_router.at[:, N_GROUPS:n_route].set(w_router_expert[layer])
    b_router = jnp.zeros((1, LANES), F32).at[0, :N_GROUPS].set(b_router_group[layer])
    b_router = b_router.at[0, N_GROUPS:n_route].set(b_router_expert[layer])
    w_router_hi = w_router.astype(BF16)
    w_router_lo = (w_router - w_router_hi.astype(F32)).astype(BF16)
    mix_out = functools.partial(_mix_out, norm_ga=g_oa, w_out_bf16=w_out_b, norm_gf=g_ffn,
                                w_router=jnp.concatenate([w_router_hi, w_router_lo], axis=1),
                                b_router=b_router)
    h_s, route_s, xs_s, cnt_s = mix_out(xs, attn_s, ocs)
    assert cnt_s.shape[0] == 1
    h_p, route_p, xs_all, cnt_p = mix_out(xp, attn_p, ocp, tail=xs_s)

    tile_chunks = jnp.concatenate([cnt_p[:, 0, ROUTER_LANE0:n_route],
                                   cnt_s[:, 0, ROUTER_LANE0:n_route]], axis=0).astype(jnp.int32)
    ntp, nts = cnt_p.shape[0], cnt_s.shape[0]
    tm_p, tm_s = tp // ntp, ts // nts
    rl_p, rl_s = _local_rows(tm_p), _local_rows(tm_s)
    tile_row0 = jnp.arange(ntp + nts, dtype=jnp.int32) * rl_p
    total_chunks = ntp * _max_tile_chunks(tm_p) + nts * _max_tile_chunks(tm_s)
    n_blocks = -(-(total_chunks + N_EXPERTS * (CHUNKS_PER_BLOCK - 1)) // CHUNKS_PER_BLOCK)
    block_e, n_used, src_row, tile_src, next_e = _sorted_layout(
        tile_chunks, tile_row0, rl_p // CHUNK, n_blocks)
    ybuf = _experts(block_e, n_used, src_row, next_e, xs_all, w_gate[layer], w_up[layer],
                    w_down[layer])
    g_fin = row(norm_final)
    y_p = _combine(tile_src[:ntp].reshape(-1), h_p, route_p, g_fin, ybuf)
    y_s = _combine(tile_src[ntp:, :rl_s // CHUNK].reshape(-1), h_s, route_s, g_fin, ybuf)

    w_keep = min(max(DILATIONS) * WIN_KEYS, seq_len)
    kv5 = lambda a_t: jnp.transpose(a_t.reshape(n_seq, n_heads, dh, seq_len),
                                    (0, 3, 1, 2))[None, :, seq_len - w_keep:]
    conv_s = jnp.stack([st1, us], axis=1)[None]
    kvs = lambda a: a.reshape(1, ts, 1, n_heads, dh)
    return (y_p.reshape(n_seq, seq_len, d), y_s.reshape(db, ds, d), kv5(kp_t), kv5(vp_t),
            conv_p[None], kvs(ks), kvs(vs), conv_s)
```

```python
import functools

import jax
import jax.numpy as jnp
from jax import lax
from jax.experimental import pallas as pl
from jax.experimental.pallas import tpu as pltpu

HEAD_DIM = 64
WIN_KEYS = 128
DILATIONS = (1, 4, 16)
CONV_WIDTH = 3
N_GROUPS = 4
EXPERTS_PER_GROUP = 8
N_EXPERTS = N_GROUPS * EXPERTS_PER_GROUP
EPS = 1e-6
NEG = -1e30
LOG2_E = 1.4426950408889634

LANES = 128
ROW_TILE = 512
EXPERT_BLOCK = 128
WEIGHT_SLABS = 8
ATTN_UNROLL = 8
VMEM_LIMIT = 56 * 1024 * 1024

F32 = jnp.float32
BF16 = jnp.bfloat16


def _rms(x, g):
    return x * lax.rsqrt(jnp.mean(x * x, axis=-1, keepdims=True) + EPS) * g


def _mix_in_kernel(*refs, d_attn, d_conv, sequential):
    if sequential:
        (x_ref, g_ref, w_ref, cw_ref, gc_ref,
         q_ref, k_ref, v_ref, kt_ref, vt_ref, oc_ref, st_ref, carry_ref) = refs
    else:
        (x_ref, g_ref, w_ref, cw_ref, gc_ref, st0_ref, st1_ref,
         q_ref, k_ref, v_ref, oc_ref, u_ref) = refs
    x = x_ref[...]
    xb = _rms(x, g_ref[...]).astype(BF16)

    def proj(lo, width):
        return jnp.dot(xb, w_ref[:, lo:lo + width], preferred_element_type=F32)

    q_ref[...] = proj(0, d_attn)
    k = proj(d_attn, d_attn)
    v = proj(2 * d_attn, d_attn)
    k_ref[...] = k
    v_ref[...] = v
    gate = proj(3 * d_attn, d_conv)
    u = proj(3 * d_attn + d_conv, d_conv) * proj(3 * d_attn + 2 * d_conv, d_conv)

    tm = x.shape[0]
    if sequential:
        kt_ref[...] = k.T
        vt_ref[...] = v.T

        @pl.when(pl.program_id(1) == 0)
        def _():
            carry_ref[...] = jnp.zeros_like(carry_ref)

        row = lax.broadcasted_iota(jnp.int32, u.shape, 0)
        prev1 = carry_ref[1:2, :]
        prev2 = carry_ref[0:1, :]
        u1 = jnp.where(row == 0, prev1, pltpu.roll(u, 1, axis=0))
        u2 = jnp.where(row == 0, prev2, jnp.where(row == 1, prev1, pltpu.roll(u, 2, axis=0)))
        carry_ref[0:2, :] = u[tm - 2:tm, :]
        st_ref[...] = u[tm - 2:tm, :]
    else:
        u_ref[...] = u
        u2 = st0_ref[...]
        u1 = st1_ref[...]
    z = u2 * cw_ref[0:1, :] + u1 * cw_ref[1:2, :] + u * cw_ref[2:3, :]
    oc_ref[...] = _rms(gate * z, gc_ref[...])


def _mix_in_call(kernel, grid, in_specs, out_specs, out_shape, scratch, args):
    return pl.pallas_call(
        kernel, grid=grid, in_specs=in_specs, out_specs=out_specs, out_shape=out_shape,
        scratch_shapes=scratch,
        compiler_params=pltpu.CompilerParams(
            dimension_semantics=("arbitrary",) * len(grid), vmem_limit_bytes=VMEM_LIMIT),
        name="mix_in",
    )(*args)


def _mix_in_prompt(x2d, norm_g, w_in_bf16, conv_w, norm_gc, *, seq_len, d_attn, d_conv):
    t, d = x2d.shape
    tm = min(ROW_TILE, seq_len)
    n_seq, per = t // seq_len, seq_len // tm
    const = lambda b, s: (0, 0)
    row = lambda width: pl.BlockSpec((tm, width), lambda b, s: (b * per + s, 0))
    col = pl.BlockSpec((None, d_attn, tm), lambda b, s: (b, 0, s))
    f32 = lambda *shape: jax.ShapeDtypeStruct(shape, F32)
    return _mix_in_call(
        functools.partial(_mix_in_kernel, d_attn=d_attn, d_conv=d_conv, sequential=True),
        (n_seq, per),
        [row(d), pl.BlockSpec((1, d), const), pl.BlockSpec(w_in_bf16.shape, const),
         pl.BlockSpec((CONV_WIDTH, d_conv), const), pl.BlockSpec((1, d_conv), const)],
        [row(d_attn)] * 3 + [col] * 2 + [row(d_conv),
                                         pl.BlockSpec((None, CONV_WIDTH - 1, d_conv),
                                                      lambda b, s: (b, 0, 0))],
        [f32(t, d_attn)] * 3 + [f32(n_seq, d_attn, seq_len)] * 2
        + [f32(t, d_conv), f32(n_seq, CONV_WIDTH - 1, d_conv)],
        [pltpu.VMEM((8, d_conv), F32)],
        (x2d, norm_g, w_in_bf16, conv_w, norm_gc))


def _mix_in_sample(x2d, norm_g, w_in_bf16, conv_w, norm_gc, st0, st1, *, d_attn, d_conv):
    t, d = x2d.shape
    full = lambda arr: pl.BlockSpec(arr.shape, lambda i: (0,) * arr.ndim)
    f32 = lambda *shape: jax.ShapeDtypeStruct(shape, F32)
    args = (x2d, norm_g, w_in_bf16, conv_w, norm_gc, st0, st1)
    outs = [f32(t, d_attn)] * 3 + [f32(t, d_conv)] * 2
    return _mix_in_call(
        functools.partial(_mix_in_kernel, d_attn=d_attn, d_conv=d_conv, sequential=False),
        (1,), [full(a) for a in args], [full(o) for o in outs], outs, [], args)


def _attn_prompt_kernel(q_ref, k_ref, v_ref, o_ref, m_s, l_s, a_s, *, seq_len):
    w = WIN_KEYS
    scale = HEAD_DIM ** -0.5 * LOG2_E
    r_i = lax.broadcasted_iota(jnp.int32, (2 * w, 2 * w), 0) & (w - 1)
    c_i = lax.broadcasted_iota(jnp.int32, (2 * w, 2 * w), 1)
    mask_cur = (lax.broadcasted_iota(jnp.int32, (2 * w, w), 1)
                <= lax.broadcasted_iota(jnp.int32, (2 * w, w), 0) & (w - 1))
    mask_both = jnp.logical_and(c_i >= r_i, c_i - w <= r_i)
    first_head = lax.broadcasted_iota(jnp.int32, (w, 2 * HEAD_DIM), 1) < HEAD_DIM
    dn_t = (((1,), (1,)), ((), ()))

    def rows(start, dil):
        if dil > 1:
            return pl.ds(start, w, stride=dil)
        return pl.ds(start if isinstance(start, int) else pl.multiple_of(start, w), w)

    def run_branch(dil, first, last):
        span = dil * w
        nb = seq_len // span

        def blocks(its, with_prev):
            mask = mask_both if with_prev else mask_cur
            cur, qs, ks, vs = [], [], [], []
            for it in its:
                g = it % dil
                n = it // dil
                c = rows(g + n * span, dil)
                cur.append(c)
                qb = (q_ref[c, :] * scale).astype(BF16)
                zero = jnp.zeros_like(qb)
                qs.append(jnp.concatenate([jnp.where(first_head, qb, zero),
                                           jnp.where(first_head, zero, qb)], axis=0))
                k = k_ref[c, :].astype(BF16)
                v = v_ref[c, :].astype(BF16)
                if with_prev:
                    p = rows(g + (n - 1) * span, dil)
                    k = jnp.concatenate([k_ref[p, :].astype(BF16), k], axis=0)
                    v = jnp.concatenate([v_ref[p, :].astype(BF16), v], axis=0)
                ks.append(k)
                vs.append(v)
            scores = [lax.dot_general(q, k, dn_t, preferred_element_type=F32)
                      for q, k in zip(qs, ks)]
            ms, ps = [], []
            for s in scores:
                s = jnp.where(mask, s, NEG)
                m = jnp.max(s, axis=-1, keepdims=True)
                ms.append(m)
                ps.append(jnp.exp2(s - m).astype(BF16))
            ones = jnp.ones((ks[0].shape[0], 2 * HEAD_DIM), BF16)
            accs = [jnp.dot(p, jnp.concatenate([v, ones], axis=1), preferred_element_type=F32)
                    for p, v in zip(ps, vs)]
            for c, m, acc_l in zip(cur, ms, accs):
                acc, l = acc_l[:, :2 * HEAD_DIM], acc_l[:, 2 * HEAD_DIM:]
                m_b = jnp.where(first_head, m[:w], m[w:])
                l_b = jnp.where(first_head, l[:w], l[w:])
                a_b = jnp.where(first_head, acc[:w], acc[w:])
                if not first:
                    m_o = m_s[c, :]
                    m_n = jnp.maximum(m_o, m_b)
                    w_o = jnp.exp2(m_o - m_n)
                    w_b = jnp.exp2(m_b - m_n)
                    l_b = w_o * l_s[c, :] + w_b * l_b
                    a_b = w_o * a_s[c, :] + w_b * a_b
                    m_b = m_n
                if last:
                    o_ref[c, :] = a_b / l_b
                else:
                    m_s[c, :] = m_b
                    l_s[c, :] = l_b
                    a_s[c, :] = a_b

        def run(lo, hi, with_prev):
            u = ATTN_UNROLL
            trips = (hi - lo) // u

            def body(t, carry):
                blocks([lo + t * u + j for j in range(u)], with_prev)
                return carry

            if trips:
                lax.fori_loop(0, trips, body, 0)
            if lo + trips * u < hi:
                blocks(list(range(lo + trips * u, hi)), with_prev)

        run(0, dil, False)
        run(dil, dil * nb, True)

    order = sorted(DILATIONS, reverse=True)
    for i, dil in enumerate(order):
        run_branch(dil, i == 0, i == len(order) - 1)


def _attn_prompt(q, k, v, *, n_seq, seq_len):
    t, d_attn = q.shape
    pair = 2 * HEAD_DIM
    spec = pl.BlockSpec((seq_len, pair), lambda b, h: (b, h))
    return pl.pallas_call(
        functools.partial(_attn_prompt_kernel, seq_len=seq_len),
        grid=(n_seq, d_attn // pair),
        in_specs=[spec] * 3,
        out_specs=spec,
        out_shape=jax.ShapeDtypeStruct((t, d_attn), F32),
        scratch_shapes=[pltpu.VMEM((seq_len, pair), F32)] * 3,
        compiler_params=pltpu.CompilerParams(
            dimension_semantics=("arbitrary", "arbitrary"), vmem_limit_bytes=VMEM_LIMIT),
        name="attn_prompt",
    )(q, k, v)


def _attn_sample_kernel(q_ref, kn_ref, vn_ref, kt_ref, vt_ref, o_ref):
    n_heads, dh, w_buf = kt_ref.shape
    delta = w_buf - lax.broadcasted_iota(jnp.int32, (1, w_buf), 1)
    cnt = jnp.zeros((1, w_buf), F32)
    for dil in DILATIONS:
        assert dil & (dil - 1) == 0
        member = jnp.where(delta <= dil * WIN_KEYS, 1.0, 0.0)
        cnt = cnt + jnp.where((delta & (dil - 1)) == 0, member, 0.0)
    eye = (lax.broadcasted_iota(jnp.int32, (dh, dh), 0)
           == lax.broadcasted_iota(jnp.int32, (dh, dh), 1))
    to_col = lambda r: jnp.sum(jnp.where(eye, r, 0.0), axis=1, keepdims=True)
    to_row = lambda c: jnp.sum(jnp.where(eye, c, 0.0), axis=0, keepdims=True)
    outs = []
    for h in range(n_heads):
        sl = slice(h * dh, (h + 1) * dh)
        q = q_ref[:, sl] * (HEAD_DIM ** -0.5)
        s_self = jnp.sum(q * kn_ref[:, sl], axis=1, keepdims=True)
        s = jnp.sum(to_col(q) * kt_ref[h], axis=0, keepdims=True)
        s = jnp.where(cnt > 0.0, s, NEG)
        m = jnp.maximum(jnp.max(s, axis=1, keepdims=True), s_self)
        p = cnt * jnp.exp(s - m)
        p_self = len(DILATIONS) * jnp.exp(s_self - m)
        l = jnp.sum(p, axis=1, keepdims=True) + p_self
        acc = jnp.sum(p * vt_ref[h], axis=1, keepdims=True)
        outs.append((to_row(acc) + p_self * vn_ref[:, sl]) / l)
    o_ref[...] = jnp.concatenate(outs, axis=1)


def _attn_sample(q, k_new, v_new, cache_kt, cache_vt):
    db, n_heads, dh, w_buf = cache_kt.shape
    head_spec = pl.BlockSpec((None, 1, n_heads * dh), lambda b: (b, 0, 0))
    cache_spec = pl.BlockSpec((None, n_heads, dh, w_buf), lambda b: (b, 0, 0, 0))
    return pl.pallas_call(
        _attn_sample_kernel,
        grid=(db,),
        in_specs=[head_spec] * 3 + [cache_spec] * 2,
        out_specs=head_spec,
        out_shape=jax.ShapeDtypeStruct((db, 1, n_heads * dh), F32),
        compiler_params=pltpu.CompilerParams(
            dimension_semantics=("arbitrary",), vmem_limit_bytes=VMEM_LIMIT),
        name="attn_sample",
    )(q, k_new, v_new, cache_kt, cache_vt)


R_E0, R_E1, R_G0, R_G1, R_POS0, R_POS1 = range(6)
ROUTER_LANE0 = N_GROUPS
CHUNK = 16
CHUNKS_PER_BLOCK = EXPERT_BLOCK // CHUNK


def _max_tile_chunks(tm):
    return (2 * tm + (CHUNK - 1) * N_EXPERTS) // CHUNK


def _local_rows(tm):
    return 2 * tm + N_EXPERTS * CHUNK


def _mix_out_kernel(*refs, n_tiles, has_tail):
    if not has_tail:
        _mix_out_tile(*refs)
        return
    *tile_in, tail_ref, h_ref, route_ref, xs_ref, cnt_ref = refs

    @pl.when(pl.program_id(0) < n_tiles)
    def _():
        _mix_out_tile(*tile_in, h_ref, route_ref, xs_ref, cnt_ref)

    @pl.when(pl.program_id(0) == n_tiles)
    def _():
        rows = tail_ref.shape[0]
        xs_ref[0:rows, :] = tail_ref[...]
        xs_ref[rows:, :] = jnp.zeros((xs_ref.shape[0] - rows, xs_ref.shape[1]), xs_ref.dtype)


def _mix_out_tile(x_ref, a_ref, oc_ref, ga_ref, wo_ref, gf_ref, wr_ref, br_ref,
                  h_ref, route_ref, xs_ref, cnt_ref):
    d_attn = a_ref.shape[1]
    tm, d = x_ref.shape
    a = _rms(a_ref[...], ga_ref[...]).astype(BF16)
    mix = jnp.dot(a, wo_ref[0:d_attn, :], preferred_element_type=F32)
    mix = mix + jnp.dot(oc_ref[...].astype(BF16), wo_ref[d_attn:, :], preferred_element_type=F32)
    h = x_ref[...] + mix
    h_ref[...] = h
    tok = _rms(h, gf_ref[...])

    tok_hi = tok.astype(BF16)
    tok_lo = (tok - tok_hi.astype(F32)).astype(BF16)
    hi_part = jnp.dot(tok_hi, wr_ref[...], preferred_element_type=F32)
    lo_part = jnp.dot(tok_lo, wr_ref[:, :LANES], preferred_element_type=F32)
    logits = hi_part[:, :LANES] + hi_part[:, LANES:] + lo_part + br_ref[...]
    lane = lax.broadcasted_iota(jnp.int32, logits.shape, 1)
    big = jnp.int32(LANES)
    neg_inf = jnp.float32(-jnp.inf)

    def top1(vals):
        best = jnp.max(vals, axis=-1, keepdims=True)
        idx = jnp.min(jnp.where(vals == best, lane, big), axis=-1, keepdims=True)
        return best, idx

    is_group = lane < N_GROUPS
    lg = jnp.where(is_group, logits, neg_inf)
    mg, g_sel = top1(lg)
    p_group = 1.0 / jnp.sum(jnp.where(is_group, jnp.exp(lg - mg), 0.0), axis=-1, keepdims=True)

    lo = ROUTER_LANE0 + g_sel * EXPERTS_PER_GROUP
    in_group = jnp.logical_and(lane >= lo, lane < lo + EXPERTS_PER_GROUP)
    le = jnp.where(in_group, logits, neg_inf)
    v1, i1 = top1(le)
    v2, i2 = top1(jnp.where(lane == i1, neg_inf, le))
    e2 = jnp.exp(v2 - v1)
    gate1 = p_group / (1.0 + e2)
    gate2 = p_group * e2 / (1.0 + e2)

    oh1 = lane == i1
    oh2 = lane == i2
    both = jnp.where(jnp.logical_or(oh1, oh2), 1.0, 0.0)
    r_i = lax.broadcasted_iota(jnp.int32, (tm, tm), 0)
    c_i = lax.broadcasted_iota(jnp.int32, (tm, tm), 1)
    strict_lower = jnp.where(c_i < r_i, 1.0, 0.0).astype(BF16)
    before = jnp.dot(strict_lower, both.astype(BF16), preferred_element_type=F32)
    chunks = jnp.floor((jnp.sum(both, axis=0, keepdims=True) + (CHUNK - 1)) * (1.0 / CHUNK))
    u_r = lax.broadcasted_iota(jnp.int32, (LANES, LANES), 0)
    u_c = lax.broadcasted_iota(jnp.int32, (LANES, LANES), 1)
    strict_upper = jnp.where(u_r < u_c, 1.0, 0.0).astype(BF16)
    chunks8 = jnp.broadcast_to(chunks, (8, LANES))
    first_row = CHUNK * jnp.dot(chunks8.astype(BF16), strict_upper,
                                preferred_element_type=F32)[0:1, :]
    pos = first_row + before
    pos1 = jnp.sum(jnp.where(oh1, pos, 0.0), axis=-1, keepdims=True)
    pos2 = jnp.sum(jnp.where(oh2, pos, 0.0), axis=-1, keepdims=True)
    cnt_ref[...] = jnp.where(lax.broadcasted_iota(jnp.int32, (8, LANES), 0) == 0, chunks8, 0.0)

    rec = jnp.zeros(logits.shape, F32)
    for col, val in ((R_E0, (i1 - ROUTER_LANE0).astype(F32)), (R_E1, (i2 - ROUTER_LANE0).astype(F32)),
                     (R_G0, gate1), (R_G1, gate2), (R_POS0, pos1), (R_POS1, pos2)):
        rec = jnp.where(lane == col, val, rec)
    route_ref[...] = rec

    rec_t = rec.T
    l1 = rec_t[R_POS0:R_POS0 + 1, :].astype(jnp.int32)
    l2 = rec_t[R_POS1:R_POS1 + 1, :].astype(jnp.int32)
    srow = lax.broadcasted_iota(jnp.int32, (xs_ref.shape[0], tm), 0)
    perm = jnp.where(srow == l1, 1.0, jnp.where(srow == l2, 1.0, 0.0)).astype(BF16)
    xs_ref[...] = jnp.dot(perm, tok_hi, preferred_element_type=F32).astype(BF16)


def _mix_out(x2d, attn, oconv, norm_ga, w_out_bf16, norm_gf, w_router, b_router, tail=None):
    t, d = x2d.shape
    d_attn, d_conv = attn.shape[1], oconv.shape[1]
    tm = min(ROW_TILE, t)
    nt = t // tm
    r_l = _local_rows(tm)
    has_tail = tail is not None
    tile = lambda i: jnp.minimum(i, nt - 1)
    row = lambda width: pl.BlockSpec((tm, width), lambda i: (tile(i), 0))
    full = lambda arr: pl.BlockSpec(arr.shape, lambda i: (0, 0))
    args = [x2d, attn, oconv, norm_ga, w_out_bf16, norm_gf, w_router, b_router]
    in_specs = [row(d), row(d_attn), row(d_conv)] + [full(a) for a in args[3:]]
    if has_tail:
        assert tail.shape[0] <= r_l and tail.shape[1] == d
        args.append(tail)
        in_specs.append(full(tail))
    return pl.pallas_call(
        functools.partial(_mix_out_kernel, n_tiles=nt, has_tail=has_tail),
        grid=(nt + has_tail,),
        in_specs=in_specs,
        out_specs=[row(d), row(LANES), pl.BlockSpec((r_l, d), lambda i: (i, 0)),
                   pl.BlockSpec((None, 8, LANES), lambda i: (tile(i), 0, 0))],
        out_shape=[jax.ShapeDtypeStruct((t, d), F32), jax.ShapeDtypeStruct((t, LANES), F32),
                   jax.ShapeDtypeStruct(((nt + has_tail) * r_l, d), BF16),
                   jax.ShapeDtypeStruct((nt, 8, LANES), F32)],
        compiler_params=pltpu.CompilerParams(
            dimension_semantics=("arbitrary",), vmem_limit_bytes=VMEM_LIMIT),
        name="mix_out",
    )(*args)


def _sorted_layout(tile_chunks, tile_row0, max_local, n_blocks):
    nt, n_exp = tile_chunks.shape
    cpb = CHUNKS_PER_BLOCK
    i32 = jnp.int32
    seg = jnp.sum(tile_chunks, axis=0)
    padded = (seg + cpb - 1) // cpb * cpb
    pend = jnp.cumsum(padded)
    pstart = pend - padded
    tile_incl = jnp.cumsum(tile_chunks, axis=0)
    tile_excl = tile_incl - tile_chunks
    local_incl = jnp.cumsum(tile_chunks, axis=1)
    local_excl = local_incl - tile_chunks
    base = pstart[None, :] + tile_excl

    block_first = jnp.arange(n_blocks, dtype=i32) * cpb
    block_e = jnp.minimum(jnp.sum((pend[None, :] <= block_first[:, None]).astype(i32), axis=1),
                          n_exp - 1)
    n_used = (pend[-1:] // cpb).astype(i32)

    onehot_pick = lambda onehot, table: jnp.sum(jnp.where(onehot, table, 0), axis=-1)

    is_e = block_e[:, None] == jnp.arange(n_exp, dtype=i32)[None, :]
    of_expert = lambda table_te: onehot_pick(is_e[:, None, :], table_te[None, :, :])
    incl_b, cnt_b, lexcl_b = of_expert(tile_incl), of_expert(tile_chunks), of_expert(local_excl)
    q = (block_first - onehot_pick(is_e, pstart[None, :]))[:, None] + jnp.arange(cpb, dtype=i32)
    tile_q = jnp.minimum(jnp.sum((incl_b[:, None, :] <= q[:, :, None]).astype(i32), axis=2), nt - 1)
    is_t = tile_q[:, :, None] == jnp.arange(nt, dtype=i32)[None, None, :]
    of_tile = lambda table_bt: onehot_pick(is_t, table_bt[:, None, :])
    local_chunk = of_tile(lexcl_b) + q - of_tile(incl_b - cnt_b)
    in_run = jnp.logical_and(q >= 0, q < onehot_pick(is_e, seg[None, :])[:, None])
    src_row = jnp.where(in_run, of_tile(tile_row0[None, :]) + CHUNK * local_chunk, 0)
    src_row = src_row.reshape(-1).astype(i32)

    c = jnp.arange(max_local, dtype=i32)
    e_c = jnp.minimum(jnp.sum((local_incl[:, None, :] <= c[None, :, None]).astype(i32), axis=2),
                      n_exp - 1)
    is_ec = e_c[:, :, None] == jnp.arange(n_exp, dtype=i32)[None, None, :]
    of_run = lambda table_te: onehot_pick(is_ec, table_te[:, None, :])
    global_chunk = of_run(base) + c[None, :] - of_run(local_excl)
    tile_src = jnp.where(c[None, :] < local_incl[:, -1:], CHUNK * global_chunk, 0).astype(i32)
    e_ids = jnp.arange(n_exp, dtype=i32)
    later = jnp.logical_and(seg[None, :] > 0, e_ids[None, :] > e_ids[:, None])
    next_e = jnp.min(jnp.where(later, e_ids[None, :], n_exp), axis=1)
    next_e = jnp.where(next_e < n_exp, next_e, -1).astype(i32)
    return block_e.astype(i32), n_used, src_row, tile_src, next_e


def _chunk_gather(src_ref, hbm_ref, buf, sems, item, slot, n_chunks, *, wait):
    for c in range(n_chunks):
        row = 0 if wait else pl.multiple_of(src_ref[item * n_chunks + c], CHUNK)
        copy = pltpu.make_async_copy(hbm_ref.at[pl.ds(row, CHUNK)],
                                     buf.at[slot, pl.ds(c * CHUNK, CHUNK)], sems.at[slot])
        if wait:
            copy.wait()
        else:
            copy.start()


def _prefetched(gather, step, n_items, body):
    slot = step % 2

    @pl.when(jnp.logical_and(step == 0, n_items > 0))
    def _():
        gather(0, 0, wait=False)

    @pl.when(step + 1 < n_items)
    def _():
        gather(step + 1, 1 - slot, wait=False)

    body(slot, lambda: gather(step, slot, wait=True))


def _experts_kernel(block_e_ref, n_used_ref, src_ref, next_e_ref, xs_ref, wg_hbm, wu_hbm, wd_hbm,
                    y_hbm, xblk, sems, ybuf, ysems, wg_f, wu_f, wd_f, wsems, wg_b, wu_b, wd_b,
                    run_ref):
    rows = EXPERT_BLOCK
    n_blocks = y_hbm.shape[0] // rows
    n_used = n_used_ref[0]
    gather = functools.partial(_chunk_gather, src_ref, xs_ref, xblk, sems,
                               n_chunks=CHUNKS_PER_BLOCK)

    def weight_copies(expert, slot):
        copies = []
        for hbm, stage in ((wg_hbm, wg_f), (wu_hbm, wu_f), (wd_hbm, wd_f)):
            slab = hbm.shape[1] // WEIGHT_SLABS
            for i in range(WEIGHT_SLABS):
                rows_i = pl.ds(i * slab, slab)
                copies.append(pltpu.make_async_copy(hbm.at[expert, rows_i], stage.at[slot, rows_i],
                                                    wsems.at[slot]))
        return copies

    def y_copy(blk, slot):
        start = blk * rows if isinstance(blk, int) else pl.multiple_of(blk * rows, rows)
        return pltpu.make_async_copy(ybuf.at[slot], y_hbm.at[pl.ds(start, rows)], ysems.at[slot])

    def block(b, carry):
        e = block_e_ref[b]
        new_expert = jnp.logical_or(b == 0, e != block_e_ref[jnp.maximum(b - 1, 0)])

        @pl.when(jnp.logical_and(new_expert, b < n_used))
        def _():
            @pl.when(b == 0)
            def _():
                run_ref[0] = 0
                for copy in weight_copies(e, 0):
                    copy.start()

            @pl.when(b > 0)
            def _():
                run_ref[0] = run_ref[0] + 1

            slot = run_ref[0] % 2
            nxt = next_e_ref[e]

            @pl.when(nxt >= 0)
            def _():
                for copy in weight_copies(nxt, 1 - slot):
                    copy.start()

            for copy in weight_copies(e, slot):
                copy.wait()
            wg_b[...] = wg_f[slot].astype(BF16)
            wu_b[...] = wu_f[slot].astype(BF16)
            wd_b[...] = wd_f[slot].astype(BF16)

        def body(slot, wait_current):
            @pl.when(b >= 2)
            def _():
                y_copy(b - 2, slot).wait()

            @pl.when(b < n_used)
            def _():
                wait_current()
                x = xblk[slot]
                gate = jnp.dot(x, wg_b[...], preferred_element_type=F32)
                up = jnp.dot(x, wu_b[...], preferred_element_type=F32)
                hid = gate * (1.0 / (1.0 + jnp.exp(-gate))) * up
                ybuf[slot] = jnp.dot(hid.astype(BF16), wd_b[...],
                                     preferred_element_type=F32).astype(BF16)

            @pl.when(b >= n_used)
            def _():
                ybuf[slot] = jnp.zeros(ybuf.shape[1:], ybuf.dtype)

            y_copy(b, slot).start()

        _prefetched(gather, b, n_used, body)
        return carry

    lax.fori_loop(0, n_blocks, block, 0)
    for blk in range(max(n_blocks - 2, 0), n_blocks):
        y_copy(blk, blk % 2).wait()


def _experts(block_e, n_used, src_row, next_e, xs, w_gate, w_up, w_down):
    n_blocks = block_e.shape[0]
    _, d, d_exp = w_gate.shape
    blk = EXPERT_BLOCK
    any_spec = pl.BlockSpec(memory_space=pl.ANY)
    return pl.pallas_call(
        _experts_kernel,
        grid_spec=pltpu.PrefetchScalarGridSpec(
            num_scalar_prefetch=4,
            grid=(1,),
            in_specs=[any_spec] * 4,
            out_specs=any_spec,
            scratch_shapes=[pltpu.VMEM((2, blk, d), BF16), pltpu.SemaphoreType.DMA((2,)),
                            pltpu.VMEM((2, blk, d), BF16), pltpu.SemaphoreType.DMA((2,)),
                            pltpu.VMEM((2, d, d_exp), F32), pltpu.VMEM((2, d, d_exp), F32),
                            pltpu.VMEM((2, d_exp, d), F32), pltpu.SemaphoreType.DMA((2,)),
                            pltpu.VMEM((d, d_exp), BF16), pltpu.VMEM((d, d_exp), BF16),
                            pltpu.VMEM((d_exp, d), BF16), pltpu.SMEM((1,), jnp.int32)],
        ),
        out_shape=jax.ShapeDtypeStruct((n_blocks * blk, d), BF16),
        compiler_params=pltpu.CompilerParams(
            dimension_semantics=("arbitrary",), vmem_limit_bytes=VMEM_LIMIT),
        name="experts",
    )(block_e, n_used, src_row, next_e, xs, w_gate, w_up, w_down)


def _combine_kernel(src_ref, h_ref, route_ref, gn_ref, ybuf_ref, o_ref, yloc, sems):
    tm = h_ref.shape[0]
    r_l = yloc.shape[1]
    gather = functools.partial(_chunk_gather, src_ref, ybuf_ref, yloc, sems,
                               n_chunks=r_l // CHUNK)

    def body(slot, wait_current):
        wait_current()
        y = yloc[slot]
        route = route_ref[...]
        l0 = route[:, R_POS0:R_POS0 + 1].astype(jnp.int32)
        l1 = route[:, R_POS1:R_POS1 + 1].astype(jnp.int32)
        srow = lax.broadcasted_iota(jnp.int32, (tm, r_l), 1)
        gates = jnp.where(srow == l0, route[:, R_G0:R_G0 + 1],
                          jnp.where(srow == l1, route[:, R_G1:R_G1 + 1], 0.0)).astype(BF16)
        f = jnp.dot(gates, y, preferred_element_type=F32)
        o_ref[...] = _rms(h_ref[...] + f, gn_ref[...])

    _prefetched(gather, pl.program_id(0), pl.num_programs(0), body)


def _combine(tile_src, h, route, norm_g, ybuf):
    t, d = h.shape
    tm = min(ROW_TILE, t)
    r_l = _local_rows(tm)
    return pl.pallas_call(
        _combine_kernel,
        grid_spec=pltpu.PrefetchScalarGridSpec(
            num_scalar_prefetch=1,
            grid=(t // tm,),
            in_specs=[pl.BlockSpec((tm, d), lambda i, src: (i, 0)),
                      pl.BlockSpec((tm, LANES), lambda i, src: (i, 0)),
                      pl.BlockSpec((1, d), lambda i, src: (0, 0)),
                      pl.BlockSpec(memory_space=pl.ANY)],
            out_specs=pl.BlockSpec((tm, d), lambda i, src: (i, 0)),
            scratch_shapes=[pltpu.VMEM((2, r_l, d), BF16),
                            pltpu.SemaphoreType.DMA((2,))],
        ),
        out_shape=jax.ShapeDtypeStruct((t, d), F32),
        compiler_params=pltpu.CompilerParams(
            dimension_semantics=("arbitrary",), vmem_limit_bytes=VMEM_LIMIT),
        name="combine",
    )(tile_src, h, route, norm_g, ybuf)


def kernel(x_prompt, x_sample, cache_k, cache_v, state_conv, norm_mix, w_in, conv_w, norm_out_attn,
           norm_out_conv, w_out, norm_ffn, w_router_group, b_router_group, w_router_expert,
           b_router_expert, w_gate, w_up, w_down, norm_final):
    n_seq, seq_len, d = x_prompt.shape
    db, ds, _ = x_sample.shape
    depth = w_in.shape[0]
    _, _, w_buf, n_heads, dh = cache_k.shape
    d_attn = n_heads * dh
    d_conv = d - d_attn
    assert depth == 1 and ds == 1 and dh == HEAD_DIM
    assert seq_len % (max(DILATIONS) * WIN_KEYS) == 0 and seq_len <= max(DILATIONS) * WIN_KEYS
    layer = 0
    tp, ts = n_seq * seq_len, db

    xp = x_prompt.reshape(tp, d)
    xs = x_sample.reshape(ts, d)
    row = lambda vec: vec.reshape(1, -1)
    w_in_b = w_in[layer].astype(BF16)
    w_out_b = w_out[layer].astype(BF16)
    g_mix, g_oa, g_oc, g_ffn = (row(norm_mix[layer]), row(norm_out_attn[layer]),
                                row(norm_out_conv[layer]), row(norm_ffn[layer]))
    st0, st1 = state_conv[layer, :, 0, :], state_conv[layer, :, 1, :]

    qp, kp, vp, kp_t, vp_t, ocp, conv_p = _mix_in_prompt(
        xp, g_mix, w_in_b, conv_w[layer], g_oc, seq_len=seq_len, d_attn=d_attn, d_conv=d_conv)
    qs, ks, vs, ocs, us = _mix_in_sample(
        xs, g_mix, w_in_b, conv_w[layer], g_oc, st0, st1, d_attn=d_attn, d_conv=d_conv)

    attn_p = _attn_prompt(qp, kp, vp, n_seq=n_seq, seq_len=seq_len)
    heads = lambda a: a.reshape(ts, 1, d_attn)
    positions_last = lambda c: jnp.transpose(c, (0, 2, 3, 1))
    attn_s = _attn_sample(heads(qs), heads(ks), heads(vs),
                          positions_last(cache_k[layer]), positions_last(cache_v[layer]))
    attn_s = attn_s.reshape(ts, d_attn)

    n_route = N_GROUPS + N_EXPERTS
    w_router = jnp.zeros((d, LANES), F32).at[:, :N_GROUPS].set(w_router_group[layer])
    w_router = w_router.at[:, N_GROUPS:n_route].set(w_router_expert[layer])
    b_router = jnp.zeros((1, LANES), F32).at[0, :N_GROUPS].set(b_router_group[layer])
    b_router = b_router.at[0, N_GROUPS:n_route].set(b_router_expert[layer])
    w_router_hi = w_router.astype(BF16)
    w_router_lo = (w_router - w_router_hi.astype(F32)).astype(BF16)
    mix_out = functools.partial(_mix_out, norm_ga=g_oa, w_out_bf16=w_out_b, norm_gf=g_ffn,
                                w_router=jnp.concatenate([w_router_hi, w_router_lo], axis=1),
                                b_router=b_router)
    h_s, route_s, xs_s, cnt_s = mix_out(xs, attn_s, ocs)
    assert cnt_s.shape[0] == 1
    h_p, route_p, xs_all, cnt_p = mix_out(xp, attn_p, ocp, tail=xs_s)

    tile_chunks = jnp.concatenate([cnt_p[:, 0, ROUTER_LANE0:n_route],
                                   cnt_s[:, 0, ROUTER_LANE0:n_route]], axis=0).astype(jnp.int32)
    ntp, nts = cnt_p.shape[0], cnt_s.shape[0]
    tm_p, tm_s = tp // ntp, ts // nts
    rl_p, rl_s = _local_rows(tm_p), _local_rows(tm_s)
    tile_row0 = jnp.arange(ntp + nts, dtype=jnp.int32) * rl_p
    total_chunks = ntp * _max_tile_chunks(tm_p) + nts * _max_tile_chunks(tm_s)
    n_blocks = -(-(total_chunks + N_EXPERTS * (CHUNKS_PER_BLOCK - 1)) // CHUNKS_PER_BLOCK)
    block_e, n_used, src_row, tile_src, next_e = _sorted_layout(
        tile_chunks, tile_row0, rl_p // CHUNK, n_blocks)
    ybuf = _experts(block_e, n_used, src_row, next_e, xs_all, w_gate[layer], w_up[layer],
                    w_down[layer])
    g_fin = row(norm_final)
    y_p = _combine(tile_src[:ntp].reshape(-1), h_p, route_p, g_fin, ybuf)
    y_s = _combine(tile_src[ntp:, :rl_s // CHUNK].reshape(-1), h_s, route_s, g_fin, ybuf)

    w_keep = min(max(DILATIONS) * WIN_KEYS, seq_len)
    kv5 = lambda a_t: jnp.transpose(a_t.reshape(n_seq, n_heads, dh, seq_len),
                                    (0, 3, 1, 2))[None, :, seq_len - w_keep:]
    conv_s = jnp.stack([st1, us], axis=1)[None]
    kvs = lambda a: a.reshape(1, ts, 1, n_heads, dh)
    return (y_p.reshape(n_seq, seq_len, d), y_s.reshape(db, ds, d), kv5(kp_t), kv5(vp_t),
            conv_p[None], kvs(ks), kvs(vs), conv_s)
```

```python
import functools

import jax
import jax.numpy as jnp
from jax import lax
from jax.experimental import pallas as pl
from jax.experimental.pallas import tpu as pltpu

HEAD_DIM = 64
WIN_KEYS = 128
DILATIONS = (1, 4, 16)
CONV_WIDTH = 3
N_GROUPS = 4
EXPERTS_PER_GROUP = 8
N_EXPERTS = N_GROUPS * EXPERTS_PER_GROUP
EPS = 1e-6
NEG = -1e30
LOG2_E = 1.4426950408889634

LANES = 128
ROW_TILE = 512
EXPERT_BLOCK = 512
WEIGHT_SLABS = 8
ATTN_UNROLL = 8
VMEM_LIMIT = 56 * 1024 * 1024

F32 = jnp.float32
BF16 = jnp.bfloat16


def _rms(x, g):
    return x * lax.rsqrt(jnp.mean(x * x, axis=-1, keepdims=True) + EPS) * g


def _mix_in_kernel(*refs, d_attn, d_conv, sequential):
    if sequential:
        (x_ref, g_ref, w_ref, cw_ref, gc_ref,
         q_ref, k_ref, v_ref, kt_ref, vt_ref, oc_ref, st_ref, carry_ref) = refs
    else:
        (x_ref, g_ref, w_ref, cw_ref, gc_ref, st0_ref, st1_ref,
         q_ref, k_ref, v_ref, oc_ref, u_ref) = refs
    x = x_ref[...]
    xb = _rms(x, g_ref[...]).astype(BF16)

    def proj(lo, width):
        return jnp.dot(xb, w_ref[:, lo:lo + width], preferred_element_type=F32)

    q_ref[...] = proj(0, d_attn)
    k = proj(d_attn, d_attn)
    v = proj(2 * d_attn, d_attn)
    k_ref[...] = k
    v_ref[...] = v
    gate = proj(3 * d_attn, d_conv)
    u = proj(3 * d_attn + d_conv, d_conv) * proj(3 * d_attn + 2 * d_conv, d_conv)

    tm = x.shape[0]
    if sequential:
        kt_ref[...] = k.T
        vt_ref[...] = v.T

        @pl.when(pl.program_id(1) == 0)
        def _():
            carry_ref[...] = jnp.zeros_like(carry_ref)

        row = lax.broadcasted_iota(jnp.int32, u.shape, 0)
        prev1 = carry_ref[1:2, :]
        prev2 = carry_ref[0:1, :]
        u1 = jnp.where(row == 0, prev1, pltpu.roll(u, 1, axis=0))
        u2 = jnp.where(row == 0, prev2, jnp.where(row == 1, prev1, pltpu.roll(u, 2, axis=0)))
        carry_ref[0:2, :] = u[tm - 2:tm, :]
        st_ref[...] = u[tm - 2:tm, :]
    else:
        u_ref[...] = u
        u2 = st0_ref[...]
        u1 = st1_ref[...]
    z = u2 * cw_ref[0:1, :] + u1 * cw_ref[1:2, :] + u * cw_ref[2:3, :]
    oc_ref[...] = _rms(gate * z, gc_ref[...])


def _mix_in_call(kernel, grid, in_specs, out_specs, out_shape, scratch, args):
    return pl.pallas_call(
        kernel, grid=grid, in_specs=in_specs, out_specs=out_specs, out_shape=out_shape,
        scratch_shapes=scratch,
        compiler_params=pltpu.CompilerParams(
            dimension_semantics=("arbitrary",) * len(grid), vmem_limit_bytes=VMEM_LIMIT),
        name="mix_in",
    )(*args)


def _mix_in_prompt(x2d, norm_g, w_in_bf16, conv_w, norm_gc, *, seq_len, d_attn, d_conv):
    t, d = x2d.shape
    tm = min(ROW_TILE, seq_len)
    n_seq, per = t // seq_len, seq_len // tm
    const = lambda b, s: (0, 0)
    row = lambda width: pl.BlockSpec((tm, width), lambda b, s: (b * per + s, 0))
    col = pl.BlockSpec((None, d_attn, tm), lambda b, s: (b, 0, s))
    f32 = lambda *shape: jax.ShapeDtypeStruct(shape, F32)
    return _mix_in_call(
        functools.partial(_mix_in_kernel, d_attn=d_attn, d_conv=d_conv, sequential=True),
        (n_seq, per),
        [row(d), pl.BlockSpec((1, d), const), pl.BlockSpec(w_in_bf16.shape, const),
         pl.BlockSpec((CONV_WIDTH, d_conv), const), pl.BlockSpec((1, d_conv), const)],
        [row(d_attn)] * 3 + [col] * 2 + [row(d_conv),
                                         pl.BlockSpec((None, CONV_WIDTH - 1, d_conv),
                                                      lambda b, s: (b, 0, 0))],
        [f32(t, d_attn)] * 3 + [f32(n_seq, d_attn, seq_len)] * 2
        + [f32(t, d_conv), f32(n_seq, CONV_WIDTH - 1, d_conv)],
        [pltpu.VMEM((8, d_conv), F32)],
        (x2d, norm_g, w_in_bf16, conv_w, norm_gc))


def _mix_in_sample(x2d, norm_g, w_in_bf16, conv_w, norm_gc, st0, st1, *, d_attn, d_conv):
    t, d = x2d.shape
    full = lambda arr: pl.BlockSpec(arr.shape, lambda i: (0,) * arr.ndim)
    f32 = lambda *shape: jax.ShapeDtypeStruct(shape, F32)
    args = (x2d, norm_g, w_in_bf16, conv_w, norm_gc, st0, st1)
    outs = [f32(t, d_attn)] * 3 + [f32(t, d_conv)] * 2
    return _mix_in_call(
        functools.partial(_mix_in_kernel, d_attn=d_attn, d_conv=d_conv, sequential=False),
        (1,), [full(a) for a in args], [full(o) for o in outs], outs, [], args)


def _attn_prompt_kernel(q_ref, k_ref, v_ref, o_ref, m_s, l_s, a_s, *, seq_len):
    w = WIN_KEYS
    scale = HEAD_DIM ** -0.5 * LOG2_E
    r_i = lax.broadcasted_iota(jnp.int32, (2 * w, 2 * w), 0) & (w - 1)
    c_i = lax.broadcasted_iota(jnp.int32, (2 * w, 2 * w), 1)
    mask_cur = (lax.broadcasted_iota(jnp.int32, (2 * w, w), 1)
                <= lax.broadcasted_iota(jnp.int32, (2 * w, w), 0) & (w - 1))
    mask_both = jnp.logical_and(c_i >= r_i, c_i - w <= r_i)
    first_head = lax.broadcasted_iota(jnp.int32, (w, 2 * HEAD_DIM), 1) < HEAD_DIM
    dn_t = (((1,), (1,)), ((), ()))

    def rows(start, dil):
        if dil > 1:
            return pl.ds(start, w, stride=dil)
        return pl.ds(start if isinstance(start, int) else pl.multiple_of(start, w), w)

    def run_branch(dil, first, last):
        span = dil * w
        nb = seq_len // span

        def blocks(its, with_prev):
            mask = mask_both if with_prev else mask_cur
            cur, qs, ks, vs = [], [], [], []
            for it in its:
                g = it % dil
                n = it // dil
                c = rows(g + n * span, dil)
                cur.append(c)
                qb = (q_ref[c, :] * scale).astype(BF16)
                zero = jnp.zeros_like(qb)
                qs.append(jnp.concatenate([jnp.where(first_head, qb, zero),
                                           jnp.where(first_head, zero, qb)], axis=0))
                k = k_ref[c, :].astype(BF16)
                v = v_ref[c, :].astype(BF16)
                if with_prev:
                    p = rows(g + (n - 1) * span, dil)
                    k = jnp.concatenate([k_ref[p, :].astype(BF16), k], axis=0)
                    v = jnp.concatenate([v_ref[p, :].astype(BF16), v], axis=0)
                ks.append(k)
                vs.append(v)
            scores = [lax.dot_general(q, k, dn_t, preferred_element_type=F32)
                      for q, k in zip(qs, ks)]
            ms, ps = [], []
            for s in scores:
                s = jnp.where(mask, s, NEG)
                m = jnp.max(s, axis=-1, keepdims=True)
                ms.append(m)
                ps.append(jnp.exp2(s - m).astype(BF16))
            ones = jnp.ones((ks[0].shape[0], 2 * HEAD_DIM), BF16)
            accs = [jnp.dot(p, jnp.concatenate([v, ones], axis=1), preferred_element_type=F32)
                    for p, v in zip(ps, vs)]
            for c, m, acc_l in zip(cur, ms, accs):
                acc, l = acc_l[:, :2 * HEAD_DIM], acc_l[:, 2 * HEAD_DIM:]
                m_b = jnp.where(first_head, m[:w], m[w:])
                l_b = jnp.where(first_head, l[:w], l[w:])
                a_b = jnp.where(first_head, acc[:w], acc[w:])
                if not first:
                    m_o = m_s[c, :]
                    m_n = jnp.maximum(m_o, m_b)
                    w_o = jnp.exp2(m_o - m_n)
                    w_b = jnp.exp2(m_b - m_n)
                    l_b = w_o * l_s[c, :] + w_b * l_b
                    a_b = w_o * a_s[c, :] + w_b * a_b
                    m_b = m_n
                if last:
                    o_ref[c, :] = a_b / l_b
                else:
                    m_s[c, :] = m_b
                    l_s[c, :] = l_b
                    a_s[c, :] = a_b

        def run(lo, hi, with_prev):
            u = ATTN_UNROLL
            trips = (hi - lo) // u

            def body(t, carry):
                blocks([lo + t * u + j for j in range(u)], with_prev)
                return carry

            if trips:
                lax.fori_loop(0, trips, body, 0)
            if lo + trips * u < hi:
                blocks(list(range(lo + trips * u, hi)), with_prev)

        run(0, dil, False)
        run(dil, dil * nb, True)

    order = sorted(DILATIONS, reverse=True)
    for i, dil in enumerate(order):
        run_branch(dil, i == 0, i == len(order) - 1)


def _attn_prompt(q, k, v, *, n_seq, seq_len):
    t, d_attn = q.shape
    pair = 2 * HEAD_DIM
    spec = pl.BlockSpec((seq_len, pair), lambda b, h: (b, h))
    return pl.pallas_call(
        functools.partial(_attn_prompt_kernel, seq_len=seq_len),
        grid=(n_seq, d_attn // pair),
        in_specs=[spec] * 3,
        out_specs=spec,
        out_shape=jax.ShapeDtypeStruct((t, d_attn), F32),
        scratch_shapes=[pltpu.VMEM((seq_len, pair), F32)] * 3,
        compiler_params=pltpu.CompilerParams(
            dimension_semantics=("arbitrary", "arbitrary"), vmem_limit_bytes=VMEM_LIMIT),
        name="attn_prompt",
    )(q, k, v)


def _attn_sample_kernel(q_ref, kn_ref, vn_ref, kt_ref, vt_ref, o_ref):
    n_heads, dh, w_buf = kt_ref.shape
    delta = w_buf - lax.broadcasted_iota(jnp.int32, (1, w_buf), 1)
    cnt = jnp.zeros((1, w_buf), F32)
    for dil in DILATIONS:
        assert dil & (dil - 1) == 0
        member = jnp.where(delta <= dil * WIN_KEYS, 1.0, 0.0)
        cnt = cnt + jnp.where((delta & (dil - 1)) == 0, member, 0.0)
    eye = (lax.broadcasted_iota(jnp.int32, (dh, dh), 0)
           == lax.broadcasted_iota(jnp.int32, (dh, dh), 1))
    to_col = lambda r: jnp.sum(jnp.where(eye, r, 0.0), axis=1, keepdims=True)
    to_row = lambda c: jnp.sum(jnp.where(eye, c, 0.0), axis=0, keepdims=True)
    outs = []
    for h in range(n_heads):
        sl = slice(h * dh, (h + 1) * dh)
        q = q_ref[:, sl] * (HEAD_DIM ** -0.5)
        s_self = jnp.sum(q * kn_ref[:, sl], axis=1, keepdims=True)
        s = jnp.sum(to_col(q) * kt_ref[h], axis=0, keepdims=True)
        s = jnp.where(cnt > 0.0, s, NEG)
        m = jnp.maximum(jnp.max(s, axis=1, keepdims=True), s_self)
        p = cnt * jnp.exp(s - m)
        p_self = len(DILATIONS) * jnp.exp(s_self - m)
        l = jnp.sum(p, axis=1, keepdims=True) + p_self
        acc = jnp.sum(p * vt_ref[h], axis=1, keepdims=True)
        outs.append((to_row(acc) + p_self * vn_ref[:, sl]) / l)
    o_ref[...] = jnp.concatenate(outs, axis=1)


def _attn_sample(q, k_new, v_new, cache_kt, cache_vt):
    db, n_heads, dh, w_buf = cache_kt.shape
    head_spec = pl.BlockSpec((None, 1, n_heads * dh), lambda b: (b, 0, 0))
    cache_spec = pl.BlockSpec((None, n_heads, dh, w_buf), lambda b: (b, 0, 0, 0))
    return pl.pallas_call(
        _attn_sample_kernel,
        grid=(db,),
        in_specs=[head_spec] * 3 + [cache_spec] * 2,
        out_specs=head_spec,
        out_shape=jax.ShapeDtypeStruct((db, 1, n_heads * dh), F32),
        compiler_params=pltpu.CompilerParams(
            dimension_semantics=("arbitrary",), vmem_limit_bytes=VMEM_LIMIT),
        name="attn_sample",
    )(q, k_new, v_new, cache_kt, cache_vt)


R_E0, R_E1, R_G0, R_G1, R_POS0, R_POS1 = range(6)
ROUTER_LANE0 = N_GROUPS
CHUNK = 16
CHUNKS_PER_BLOCK = EXPERT_BLOCK // CHUNK


def _max_tile_chunks(tm):
    return (2 * tm + (CHUNK - 1) * N_EXPERTS) // CHUNK


def _local_rows(tm):
    return 2 * tm + N_EXPERTS * CHUNK


def _mix_out_kernel(*refs, n_tiles, has_tail):
    if not has_tail:
        _mix_out_tile(*refs)
        return
    *tile_in, tail_ref, h_ref, route_ref, xs_ref, cnt_ref = refs

    @pl.when(pl.program_id(0) < n_tiles)
    def _():
        _mix_out_tile(*tile_in, h_ref, route_ref, xs_ref, cnt_ref)

    @pl.when(pl.program_id(0) == n_tiles)
    def _():
        rows = tail_ref.shape[0]
        xs_ref[0:rows, :] = tail_ref[...]
        xs_ref[rows:, :] = jnp.zeros((xs_ref.shape[0] - rows, xs_ref.shape[1]), xs_ref.dtype)


def _mix_out_tile(x_ref, a_ref, oc_ref, ga_ref, wo_ref, gf_ref, wr_ref, br_ref,
                  h_ref, route_ref, xs_ref, cnt_ref):
    d_attn = a_ref.shape[1]
    tm, d = x_ref.shape
    a = _rms(a_ref[...], ga_ref[...]).astype(BF16)
    mix = jnp.dot(a, wo_ref[0:d_attn, :], preferred_element_type=F32)
    mix = mix + jnp.dot(oc_ref[...].astype(BF16), wo_ref[d_attn:, :], preferred_element_type=F32)
    h = x_ref[...] + mix
    h_ref[...] = h
    tok = _rms(h, gf_ref[...])

    tok_hi = tok.astype(BF16)
    tok_lo = (tok - tok_hi.astype(F32)).astype(BF16)
    hi_part = jnp.dot(tok_hi, wr_ref[...], preferred_element_type=F32)
    lo_part = jnp.dot(tok_lo, wr_ref[:, :LANES], preferred_element_type=F32)
    logits = hi_part[:, :LANES] + hi_part[:, LANES:] + lo_part + br_ref[...]
    lane = lax.broadcasted_iota(jnp.int32, logits.shape, 1)
    big = jnp.int32(LANES)
    neg_inf = jnp.float32(-jnp.inf)

    def top1(vals):
        best = jnp.max(vals, axis=-1, keepdims=True)
        idx = jnp.min(jnp.where(vals == best, lane, big), axis=-1, keepdims=True)
        return best, idx

    is_group = lane < N_GROUPS
    lg = jnp.where(is_group, logits, neg_inf)
    mg, g_sel = top1(lg)
    p_group = 1.0 / jnp.sum(jnp.where(is_group, jnp.exp(lg - mg), 0.0), axis=-1, keepdims=True)

    lo = ROUTER_LANE0 + g_sel * EXPERTS_PER_GROUP
    in_group = jnp.logical_and(lane >= lo, lane < lo + EXPERTS_PER_GROUP)
    le = jnp.where(in_group, logits, neg_inf)
    v1, i1 = top1(le)
    v2, i2 = top1(jnp.where(lane == i1, neg_inf, le))
    e2 = jnp.exp(v2 - v1)
    gate1 = p_group / (1.0 + e2)
    gate2 = p_group * e2 / (1.0 + e2)

    oh1 = lane == i1
    oh2 = lane == i2
    both = jnp.where(jnp.logical_or(oh1, oh2), 1.0, 0.0)
    r_i = lax.broadcasted_iota(jnp.int32, (tm, tm), 0)
    c_i = lax.broadcasted_iota(jnp.int32, (tm, tm), 1)
    strict_lower = jnp.where(c_i < r_i, 1.0, 0.0).astype(BF16)
    before = jnp.dot(strict_lower, both.astype(BF16), preferred_element_type=F32)
    chunks = jnp.floor((jnp.sum(both, axis=0, keepdims=True) + (CHUNK - 1)) * (1.0 / CHUNK))
    u_r = lax.broadcasted_iota(jnp.int32, (LANES, LANES), 0)
    u_c = lax.broadcasted_iota(jnp.int32, (LANES, LANES), 1)
    strict_upper = jnp.where(u_r < u_c, 1.0, 0.0).astype(BF16)
    chunks8 = jnp.broadcast_to(chunks, (8, LANES))
    first_row = CHUNK * jnp.dot(chunks8.astype(BF16), strict_upper,
                                preferred_element_type=F32)[0:1, :]
    pos = first_row + before
    pos1 = jnp.sum(jnp.where(oh1, pos, 0.0), axis=-1, keepdims=True)
    pos2 = jnp.sum(jnp.where(oh2, pos, 0.0), axis=-1, keepdims=True)
    cnt_ref[...] = jnp.where(lax.broadcasted_iota(jnp.int32, (8, LANES), 0) == 0, chunks8, 0.0)

    rec = jnp.zeros(logits.shape, F32)
    for col, val in ((R_E0, (i1 - ROUTER_LANE0).astype(F32)), (R_E1, (i2 - ROUTER_LANE0).astype(F32)),
                     (R_G0, gate1), (R_G1, gate2), (R_POS0, pos1), (R_POS1, pos2)):
        rec = jnp.where(lane == col, val, rec)
    route_ref[...] = rec

    rec_t = rec.T
    l1 = rec_t[R_POS0:R_POS0 + 1, :].astype(jnp.int32)
    l2 = rec_t[R_POS1:R_POS1 + 1, :].astype(jnp.int32)
    srow = lax.broadcasted_iota(jnp.int32, (xs_ref.shape[0], tm), 0)
    perm = jnp.where(srow == l1, 1.0, jnp.where(srow == l2, 1.0, 0.0)).astype(BF16)
    xs_ref[...] = jnp.dot(perm, tok_hi, preferred_element_type=F32).astype(BF16)


def _mix_out(x2d, attn, oconv, norm_ga, w_out_bf16, norm_gf, w_router, b_router, tail=None):
    t, d = x2d.shape
    d_attn, d_conv = attn.shape[1], oconv.shape[1]
    tm = min(ROW_TILE, t)
    nt = t // tm
    r_l = _local_rows(tm)
    has_tail = tail is not None
    tile = lambda i: jnp.minimum(i, nt - 1)
    row = lambda width: pl.BlockSpec((tm, width), lambda i: (tile(i), 0))
    full = lambda arr: pl.BlockSpec(arr.shape, lambda i: (0, 0))
    args = [x2d, attn, oconv, norm_ga, w_out_bf16, norm_gf, w_router, b_router]
    in_specs = [row(d), row(d_attn), row(d_conv)] + [full(a) for a in args[3:]]
    if has_tail:
        assert tail.shape[0] <= r_l and tail.shape[1] == d
        args.append(tail)
        in_specs.append(full(tail))
    return pl.pallas_call(
        functools.partial(_mix_out_kernel, n_tiles=nt, has_tail=has_tail),
        grid=(nt + has_tail,),
        in_specs=in_specs,
        out_specs=[row(d), row(LANES), pl.BlockSpec((r_l, d), lambda i: (i, 0)),
                   pl.BlockSpec((None, 8, LANES), lambda i: (tile(i), 0, 0))],
        out_shape=[jax.ShapeDtypeStruct((t, d), F32), jax.ShapeDtypeStruct((t, LANES), F32),
                   jax.ShapeDtypeStruct(((nt + has_tail) * r_l, d), BF16),
                   jax.ShapeDtypeStruct((nt, 8, LANES), F32)],
        compiler_params=pltpu.CompilerParams(
            dimension_semantics=("arbitrary",), vmem_limit_bytes=VMEM_LIMIT),
        name="mix_out",
    )(*args)


def _sorted_layout(tile_chunks, tile_row0, max_local, n_blocks):
    nt, n_exp = tile_chunks.shape
    cpb = CHUNKS_PER_BLOCK
    i32 = jnp.int32
    seg = jnp.sum(tile_chunks, axis=0)
    padded = (seg + cpb - 1) // cpb * cpb
    pend = jnp.cumsum(padded)
    pstart = pend - padded
    tile_incl = jnp.cumsum(tile_chunks, axis=0)
    tile_excl = tile_incl - tile_chunks
    local_incl = jnp.cumsum(tile_chunks, axis=1)
    local_excl = local_incl - tile_chunks
    base = pstart[None, :] + tile_excl

    block_first = jnp.arange(n_blocks, dtype=i32) * cpb
    block_e = jnp.minimum(jnp.sum((pend[None, :] <= block_first[:, None]).astype(i32), axis=1),
                          n_exp - 1)
    n_used = (pend[-1:] // cpb).astype(i32)

    onehot_pick = lambda onehot, table: jnp.sum(jnp.where(onehot, table, 0), axis=-1)

    is_e = block_e[:, None] == jnp.arange(n_exp, dtype=i32)[None, :]
    of_expert = lambda table_te: onehot_pick(is_e[:, None, :], table_te[None, :, :])
    incl_b, cnt_b, lexcl_b = of_expert(tile_incl), of_expert(tile_chunks), of_expert(local_excl)
    q = (block_first - onehot_pick(is_e, pstart[None, :]))[:, None] + jnp.arange(cpb, dtype=i32)
    tile_q = jnp.minimum(jnp.sum((incl_b[:, None, :] <= q[:, :, None]).astype(i32), axis=2), nt - 1)
    is_t = tile_q[:, :, None] == jnp.arange(nt, dtype=i32)[None, None, :]
    of_tile = lambda table_bt: onehot_pick(is_t, table_bt[:, None, :])
    local_chunk = of_tile(lexcl_b) + q - of_tile(incl_b - cnt_b)
    in_run = jnp.logical_and(q >= 0, q < onehot_pick(is_e, seg[None, :])[:, None])
    src_row = jnp.where(in_run, of_tile(tile_row0[None, :]) + CHUNK * local_chunk, 0)
    src_row = src_row.reshape(-1).astype(i32)

    c = jnp.arange(max_local, dtype=i32)
    e_c = jnp.minimum(jnp.sum((local_incl[:, None, :] <= c[None, :, None]).astype(i32), axis=2),
                      n_exp - 1)
    is_ec = e_c[:, :, None] == jnp.arange(n_exp, dtype=i32)[None, None, :]
    of_run = lambda table_te: onehot_pick(is_ec, table_te[:, None, :])
    global_chunk = of_run(base) + c[None, :] - of_run(local_excl)
    tile_src = jnp.where(c[None, :] < local_incl[:, -1:], CHUNK * global_chunk, 0).astype(i32)
    e_ids = jnp.arange(n_exp, dtype=i32)
    later = jnp.logical_and(seg[None, :] > 0, e_ids[None, :] > e_ids[:, None])
    next_e = jnp.min(jnp.where(later, e_ids[None, :], n_exp), axis=1)
    next_e = jnp.where(next_e < n_exp, next_e, -1).astype(i32)
    return block_e.astype(i32), n_used, src_row, tile_src, next_e


def _chunk_gather(src_ref, hbm_ref, buf, sems, item, slot, n_chunks, *, wait):
    for c in range(n_chunks):
        row = 0 if wait else pl.multiple_of(src_ref[item * n_chunks + c], CHUNK)
        copy = pltpu.make_async_copy(hbm_ref.at[pl.ds(row, CHUNK)],
                                     buf.at[slot, pl.ds(c * CHUNK, CHUNK)], sems.at[slot])
        if wait:
            copy.wait()
        else:
            copy.start()


def _prefetched(gather, step, n_items, body):
    slot = step % 2

    @pl.when(jnp.logical_and(step == 0, n_items > 0))
    def _():
        gather(0, 0, wait=False)

    @pl.when(step + 1 < n_items)
    def _():
        gather(step + 1, 1 - slot, wait=False)

    body(slot, lambda: gather(step, slot, wait=True))


def _experts_kernel(block_e_ref, n_used_ref, src_ref, next_e_ref, xs_ref, wg_hbm, wu_hbm, wd_hbm,
                    y_hbm, xblk, sems, ybuf, ysems, wg_f, wu_f, wd_f, wsems, wg_b, wu_b, wd_b,
                    run_ref):
    rows = EXPERT_BLOCK
    n_blocks = y_hbm.shape[0] // rows
    n_used = n_used_ref[0]
    gather = functools.partial(_chunk_gather, src_ref, xs_ref, xblk, sems,
                               n_chunks=CHUNKS_PER_BLOCK)

    def weight_copies(expert, slot):
        copies = []
        for hbm, stage in ((wg_hbm, wg_f), (wu_hbm, wu_f), (wd_hbm, wd_f)):
            slab = hbm.shape[1] // WEIGHT_SLABS
            for i in range(WEIGHT_SLABS):
                rows_i = pl.ds(i * slab, slab)
                copies.append(pltpu.make_async_copy(hbm.at[expert, rows_i], stage.at[slot, rows_i],
                                                    wsems.at[slot]))
        return copies

    def y_copy(blk, slot):
        start = blk * rows if isinstance(blk, int) else pl.multiple_of(blk * rows, rows)
        return pltpu.make_async_copy(ybuf.at[slot], y_hbm.at[pl.ds(start, rows)], ysems.at[slot])

    def block(b, carry):
        e = block_e_ref[b]
        new_expert = jnp.logical_or(b == 0, e != block_e_ref[jnp.maximum(b - 1, 0)])

        @pl.when(jnp.logical_and(new_expert, b < n_used))
        def _():
            @pl.when(b == 0)
            def _():
                run_ref[0] = 0
                for copy in weight_copies(e, 0):
                    copy.start()

            @pl.when(b > 0)
            def _():
                run_ref[0] = run_ref[0] + 1

            slot = run_ref[0] % 2
            nxt = next_e_ref[e]

            @pl.when(nxt >= 0)
            def _():
                for copy in weight_copies(nxt, 1 - slot):
                    copy.start()

            for copy in weight_copies(e, slot):
                copy.wait()
            wg_b[...] = wg_f[slot].astype(BF16)
            wu_b[...] = wu_f[slot].astype(BF16)
            wd_b[...] = wd_f[slot].astype(BF16)

        def body(slot, wait_current):
            @pl.when(b >= 2)
            def _():
                y_copy(b - 2, slot).wait()

            @pl.when(b < n_used)
            def _():
                wait_current()
                x = xblk[slot]
                gate = jnp.dot(x, wg_b[...], preferred_element_type=F32)
                up = jnp.dot(x, wu_b[...], preferred_element_type=F32)
                hid = gate * (1.0 / (1.0 + jnp.exp(-gate))) * up
                ybuf[slot] = jnp.dot(hid.astype(BF16), wd_b[...],
                                     preferred_element_type=F32).astype(BF16)

            @pl.when(b >= n_used)
            def _():
                ybuf[slot] = jnp.zeros(ybuf.shape[1:], ybuf.dtype)

            y_copy(b, slot).start()

        _prefetched(gather, b, n_used, body)
        return carry

    lax.fori_loop(0, n_blocks, block, 0)
    for blk in range(max(n_blocks - 2, 0), n_blocks):
        y_copy(blk, blk % 2).wait()


def _experts(block_e, n_used, src_row, next_e, xs, w_gate, w_up, w_down):
    n_blocks = block_e.shape[0]
    _, d, d_exp = w_gate.shape
    blk = EXPERT_BLOCK
    any_spec = pl.BlockSpec(memory_space=pl.ANY)
    return pl.pallas_call(
        _experts_kernel,
        grid_spec=pltpu.PrefetchScalarGridSpec(
            num_scalar_prefetch=4,
            grid=(1,),
            in_specs=[any_spec] * 4,
            out_specs=any_spec,
            scratch_shapes=[pltpu.VMEM((2, blk, d), BF16), pltpu.SemaphoreType.DMA((2,)),
                            pltpu.VMEM((2, blk, d), BF16), pltpu.SemaphoreType.DMA((2,)),
                            pltpu.VMEM((2, d, d_exp), F32), pltpu.VMEM((2, d, d_exp), F32),
                            pltpu.VMEM((2, d_exp, d), F32), pltpu.SemaphoreType.DMA((2,)),
                            pltpu.VMEM((d, d_exp), BF16), pltpu.VMEM((d, d_exp), BF16),
                            pltpu.VMEM((d_exp, d), BF16), pltpu.SMEM((1,), jnp.int32)],
        ),
        out_shape=jax.ShapeDtypeStruct((n_blocks * blk, d), BF16),
        compiler_params=pltpu.CompilerParams(
            dimension_semantics=("arbitrary",), vmem_limit_bytes=VMEM_LIMIT),
        name="experts",
    )(block_e, n_used, src_row, next_e, xs, w_gate, w_up, w_down)


def _combine_kernel(src_ref, h_ref, route_ref, gn_ref, ybuf_ref, o_ref, yloc, sems):
    tm = h_ref.shape[0]
    r_l = yloc.shape[1]
    gather = functools.partial(_chunk_gather, src_ref, ybuf_ref, yloc, sems,
                               n_chunks=r_l // CHUNK)

    def body(slot, wait_current):
        wait_current()
        y = yloc[slot]
        route = route_ref[...]
        l0 = route[:, R_POS0:R_POS0 + 1].astype(jnp.int32)
        l1 = route[:, R_POS1:R_POS1 + 1].astype(jnp.int32)
        srow = lax.broadcasted_iota(jnp.int32, (tm, r_l), 1)
        gates = jnp.where(srow == l0, route[:, R_G0:R_G0 + 1],
                          jnp.where(srow == l1, route[:, R_G1:R_G1 + 1], 0.0)).astype(BF16)
        f = jnp.dot(gates, y, preferred_element_type=F32)
        o_ref[...] = _rms(h_ref[...] + f, gn_ref[...])

    _prefetched(gather, pl.program_id(0), pl.num_programs(0), body)


def _combine(tile_src, h, route, norm_g, ybuf):
    t, d = h.shape
    tm = min(ROW_TILE, t)
    r_l = _local_rows(tm)
    return pl.pallas_call(
        _combine_kernel,
        grid_spec=pltpu.PrefetchScalarGridSpec(
            num_scalar_prefetch=1,
            grid=(t // tm,),
            in_specs=[pl.BlockSpec((tm, d), lambda i, src: (i, 0)),
                      pl.BlockSpec((tm, LANES), lambda i, src: (i, 0)),
                      pl.BlockSpec((1, d), lambda i, src: (0, 0)),
                      pl.BlockSpec(memory_space=pl.ANY)],
            out_specs=pl.BlockSpec((tm, d), lambda i, src: (i, 0)),
            scratch_shapes=[pltpu.VMEM((2, r_l, d), BF16),
                            pltpu.SemaphoreType.DMA((2,))],
        ),
        out_shape=jax.ShapeDtypeStruct((t, d), F32),
        compiler_params=pltpu.CompilerParams(
            dimension_semantics=("arbitrary",), vmem_limit_bytes=VMEM_LIMIT),
        name="combine",
    )(tile_src, h, route, norm_g, ybuf)


def kernel(x_prompt, x_sample, cache_k, cache_v, state_conv, norm_mix, w_in, conv_w, norm_out_attn,
           norm_out_conv, w_out, norm_ffn, w_router_group, b_router_group, w_router_expert,
           b_router_expert, w_gate, w_up, w_down, norm_final):
    n_seq, seq_len, d = x_prompt.shape
    db, ds, _ = x_sample.shape
    depth = w_in.shape[0]
    _, _, w_buf, n_heads, dh = cache_k.shape
    d_attn = n_heads * dh
    d_conv = d - d_attn
    assert depth == 1 and ds == 1 and dh == HEAD_DIM
    assert seq_len % (max(DILATIONS) * WIN_KEYS) == 0 and seq_len <= max(DILATIONS) * WIN_KEYS
    layer = 0
    tp, ts = n_seq * seq_len, db

    xp = x_prompt.reshape(tp, d)
    xs = x_sample.reshape(ts, d)
    row = lambda vec: vec.reshape(1, -1)
    w_in_b = w_in[layer].astype(BF16)
    w_out_b = w_out[layer].astype(BF16)
    g_mix, g_oa, g_oc, g_ffn = (row(norm_mix[layer]), row(norm_out_attn[layer]),
                                row(norm_out_conv[layer]), row(norm_ffn[layer]))
    st0, st1 = state_conv[layer, :, 0, :], state_conv[layer, :, 1, :]

    qp, kp, vp, kp_t, vp_t, ocp, conv_p = _mix_in_prompt(
        xp, g_mix, w_in_b, conv_w[layer], g_oc, seq_len=seq_len, d_attn=d_attn, d_conv=d_conv)
    qs, ks, vs, ocs, us = _mix_in_sample(
        xs, g_mix, w_in_b, conv_w[layer], g_oc, st0, st1, d_attn=d_attn, d_conv=d_conv)

    attn_p = _attn_prompt(qp, kp, vp, n_seq=n_seq, seq_len=seq_len)
    heads = lambda a: a.reshape(ts, 1, d_attn)
    positions_last = lambda c: jnp.transpose(c, (0, 2, 3, 1))
    attn_s = _attn_sample(heads(qs), heads(ks), heads(vs),
                          positions_last(cache_k[layer]), positions_last(cache_v[layer]))
    attn_s = attn_s.reshape(ts, d_attn)

    n_route = N_GROUPS + N_EXPERTS
    w_router = jnp.zeros((d, LANES), F32).at[:, :N_GROUPS].set(w_router_group[layer])
    w_router = w_router.at[:, N_GROUPS:n_route].set(w_router_expert[layer])
    b_router = jnp.zeros((1, LANES), F32).at[0, :N_GROUPS].set(b_router_group[layer])
    b_router = b_router.at[0, N_GROUPS:n_route].set(b_router_expert[layer])
    w_router_hi = w_router.astype(BF16)
    w_router_lo = (w_router - w_router_hi.astype(F32)).astype(BF16)
    mix_out = functools.partial(_mix_out, norm_ga=g_oa, w_out_bf16=w_out_b, norm_gf=g_ffn,
                                w_router=jnp.concatenate([w_router_hi, w_router_lo], axis=1),
                                b_router=b_router)
    h_s, route_s, xs_s, cnt_s = mix_out(xs, attn_s, ocs)
    assert cnt_s.shape[0] == 1
    h_p, route_p, xs_all, cnt_p = mix_out(xp, attn_p, ocp, tail=xs_s)

    tile_chunks = jnp.concatenate([cnt_p[:, 0, ROUTER_LANE0:n_route],
                                   cnt_s[:, 0, ROUTER_LANE0:n_route]], axis=0).astype(jnp.int32)
    ntp, nts = cnt_p.shape[0], cnt_s.shape[0]
    tm_p, tm_s = tp // ntp, ts // nts
    rl_p, rl_s = _local_rows(tm_p), _local_rows(tm_s)
    tile_row0 = jnp.arange(ntp + nts, dtype=jnp.int32) * rl_p
    total_chunks = ntp * _max_tile_chunks(tm_p) + nts * _max_tile_chunks(tm_s)
    n_blocks = -(-(total_chunks + N_EXPERTS * (CHUNKS_PER_BLOCK - 1)) // CHUNKS_PER_BLOCK)
    block_e, n_used, src_row, tile_src, next_e = _sorted_layout(
        tile_chunks, tile_row0, rl_p // CHUNK, n_blocks)
    ybuf = _experts(block_e, n_used, src_row, next_e, xs_all, w_gate[layer], w_up[layer],
                    w_down[layer])
    g_fin = row(norm_final)
    y_p = _combine(tile_src[:ntp].reshape(-1), h_p, route_p, g_fin, ybuf)
    y_s = _combine(tile_src[ntp:, :rl_s // CHUNK].reshape(-1), h_s, route_s, g_fin, ybuf)

    w_keep = min(max(DILATIONS) * WIN_KEYS, seq_len)
    kv5 = lambda a_t: jnp.transpose(a_t.reshape(n_seq, n_heads, dh, seq_len),
                                    (0, 3, 1, 2))[None, :, seq_len - w_keep:]
    conv_s = jnp.stack([st1, us], axis=1)[None]
    kvs = lambda a: a.reshape(1, ts, 1, n_heads, dh)
    return (y_p.reshape(n_seq, seq_len, d), y_s.reshape(db, ds, d), kv5(kp_t), kv5(vp_t),
            conv_p[None], kvs(ks), kvs(vs), conv_s)
```

```python
import functools

import jax
import jax.numpy as jnp
from jax import lax
from jax.experimental import pallas as pl
from jax.experimental.pallas import tpu as pltpu

HEAD_DIM = 64
WIN_KEYS = 128
DILATIONS = (1, 4, 16)
CONV_WIDTH = 3
N_GROUPS = 4
EXPERTS_PER_GROUP = 8
N_EXPERTS = N_GROUPS * EXPERTS_PER_GROUP
EPS = 1e-6
NEG = -1e30
LOG2_E = 1.4426950408889634

LANES = 128
ROW_TILE = 512
EXPERT_BLOCK = 512
ATTN_LAG = 3
WEIGHT_SLABS = 8
ATTN_UNROLL = 8
VMEM_LIMIT = 56 * 1024 * 1024

F32 = jnp.float32
BF16 = jnp.bfloat16


def _rms(x, g):
    return x * lax.rsqrt(jnp.mean(x * x, axis=-1, keepdims=True) + EPS) * g


def _mix_in_kernel(*refs, d_attn, d_conv, sequential):
    if sequential:
        (x_ref, g_ref, w_ref, cw_ref, gc_ref,
         q_ref, k_ref, v_ref, kt_ref, vt_ref, oc_ref, st_ref, carry_ref) = refs
    else:
        (x_ref, g_ref, w_ref, cw_ref, gc_ref, st0_ref, st1_ref,
         q_ref, k_ref, v_ref, oc_ref, u_ref) = refs
    x = x_ref[...]
    xb = _rms(x, g_ref[...]).astype(BF16)

    def proj(lo, width):
        return jnp.dot(xb, w_ref[:, lo:lo + width], preferred_element_type=F32)

    q_ref[...] = proj(0, d_attn)
    k = proj(d_attn, d_attn)
    v = proj(2 * d_attn, d_attn)
    k_ref[...] = k
    v_ref[...] = v
    gate = proj(3 * d_attn, d_conv)
    u = proj(3 * d_attn + d_conv, d_conv) * proj(3 * d_attn + 2 * d_conv, d_conv)

    tm = x.shape[0]
    if sequential:
        kt_ref[...] = k.T
        vt_ref[...] = v.T

        @pl.when(pl.program_id(1) == 0)
        def _():
            carry_ref[...] = jnp.zeros_like(carry_ref)

        row = lax.broadcasted_iota(jnp.int32, u.shape, 0)
        prev1 = carry_ref[1:2, :]
        prev2 = carry_ref[0:1, :]
        u1 = jnp.where(row == 0, prev1, pltpu.roll(u, 1, axis=0))
        u2 = jnp.where(row == 0, prev2, jnp.where(row == 1, prev1, pltpu.roll(u, 2, axis=0)))
        carry_ref[0:2, :] = u[tm - 2:tm, :]
        st_ref[...] = u[tm - 2:tm, :]
    else:
        u_ref[...] = u
        u2 = st0_ref[...]
        u1 = st1_ref[...]
    z = u2 * cw_ref[0:1, :] + u1 * cw_ref[1:2, :] + u * cw_ref[2:3, :]
    oc_ref[...] = _rms(gate * z, gc_ref[...])


def _mix_in_call(kernel, grid, in_specs, out_specs, out_shape, scratch, args):
    return pl.pallas_call(
        kernel, grid=grid, in_specs=in_specs, out_specs=out_specs, out_shape=out_shape,
        scratch_shapes=scratch,
        compiler_params=pltpu.CompilerParams(
            dimension_semantics=("arbitrary",) * len(grid), vmem_limit_bytes=VMEM_LIMIT),
        name="mix_in",
    )(*args)


def _mix_in_prompt(x2d, norm_g, w_in_bf16, conv_w, norm_gc, *, seq_len, d_attn, d_conv):
    t, d = x2d.shape
    tm = min(ROW_TILE, seq_len)
    n_seq, per = t // seq_len, seq_len // tm
    const = lambda b, s: (0, 0)
    row = lambda width: pl.BlockSpec((tm, width), lambda b, s: (b * per + s, 0))
    col = pl.BlockSpec((None, d_attn, tm), lambda b, s: (b, 0, s))
    f32 = lambda *shape: jax.ShapeDtypeStruct(shape, F32)
    return _mix_in_call(
        functools.partial(_mix_in_kernel, d_attn=d_attn, d_conv=d_conv, sequential=True),
        (n_seq, per),
        [row(d), pl.BlockSpec((1, d), const), pl.BlockSpec(w_in_bf16.shape, const),
         pl.BlockSpec((CONV_WIDTH, d_conv), const), pl.BlockSpec((1, d_conv), const)],
        [row(d_attn)] * 3 + [col] * 2 + [row(d_conv),
                                         pl.BlockSpec((None, CONV_WIDTH - 1, d_conv),
                                                      lambda b, s: (b, 0, 0))],
        [f32(t, d_attn)] * 3 + [f32(n_seq, d_attn, seq_len)] * 2
        + [f32(t, d_conv), f32(n_seq, CONV_WIDTH - 1, d_conv)],
        [pltpu.VMEM((8, d_conv), F32)],
        (x2d, norm_g, w_in_bf16, conv_w, norm_gc))


def _mix_in_sample(x2d, norm_g, w_in_bf16, conv_w, norm_gc, st0, st1, *, d_attn, d_conv):
    t, d = x2d.shape
    full = lambda arr: pl.BlockSpec(arr.shape, lambda i: (0,) * arr.ndim)
    f32 = lambda *shape: jax.ShapeDtypeStruct(shape, F32)
    args = (x2d, norm_g, w_in_bf16, conv_w, norm_gc, st0, st1)
    outs = [f32(t, d_attn)] * 3 + [f32(t, d_conv)] * 2
    return _mix_in_call(
        functools.partial(_mix_in_kernel, d_attn=d_attn, d_conv=d_conv, sequential=False),
        (1,), [full(a) for a in args], [full(o) for o in outs], outs, [], args)


def _attn_prompt_kernel(q_ref, k_ref, v_ref, o_ref, m_s, l_s, a_s, *, seq_len):
    w = WIN_KEYS
    scale = HEAD_DIM ** -0.5 * LOG2_E
    r_i = lax.broadcasted_iota(jnp.int32, (2 * w, 2 * w), 0) & (w - 1)
    c_i = lax.broadcasted_iota(jnp.int32, (2 * w, 2 * w), 1)
    mask_cur = (lax.broadcasted_iota(jnp.int32, (2 * w, w), 1)
                <= lax.broadcasted_iota(jnp.int32, (2 * w, w), 0) & (w - 1))
    mask_both = jnp.logical_and(c_i >= r_i, c_i - w <= r_i)
    first_head = lax.broadcasted_iota(jnp.int32, (w, 2 * HEAD_DIM), 1) < HEAD_DIM
    dn_t = (((1,), (1,)), ((), ()))

    def rows(start, dil):
        if dil > 1:
            return pl.ds(start, w, stride=dil)
        return pl.ds(start if isinstance(start, int) else pl.multiple_of(start, w), w)

    def run_branch(dil, first, last):
        span = dil * w
        nb = seq_len // span

        def blocks(its, with_prev):
            mask = mask_both if with_prev else mask_cur

            def issue_scores(it):
                g = it % dil
                n = it // dil
                c = rows(g + n * span, dil)
                qb = (q_ref[c, :] * scale).astype(BF16)
                zero = jnp.zeros_like(qb)
                q = jnp.concatenate([jnp.where(first_head, qb, zero),
                                     jnp.where(first_head, zero, qb)], axis=0)
                k = k_ref[c, :].astype(BF16)
                v = v_ref[c, :].astype(BF16)
                if with_prev:
                    p = rows(g + (n - 1) * span, dil)
                    k = jnp.concatenate([k_ref[p, :].astype(BF16), k], axis=0)
                    v = jnp.concatenate([v_ref[p, :].astype(BF16), v], axis=0)
                return c, lax.dot_general(q, k, dn_t, preferred_element_type=F32), v

            def finish(c, s, v):
                s = jnp.where(mask, s, NEG)
                m = jnp.max(s, axis=-1, keepdims=True)
                p = jnp.exp2(s - m).astype(BF16)
                ones = jnp.ones((v.shape[0], 2 * HEAD_DIM), BF16)
                acc_l = jnp.dot(p, jnp.concatenate([v, ones], axis=1), preferred_element_type=F32)
                acc, l = acc_l[:, :2 * HEAD_DIM], acc_l[:, 2 * HEAD_DIM:]
                m_b = jnp.where(first_head, m[:w], m[w:])
                l_b = jnp.where(first_head, l[:w], l[w:])
                a_b = jnp.where(first_head, acc[:w], acc[w:])
                if not first:
                    m_o = m_s[c, :]
                    m_n = jnp.maximum(m_o, m_b)
                    w_o = jnp.exp2(m_o - m_n)
                    w_b = jnp.exp2(m_b - m_n)
                    l_b = w_o * l_s[c, :] + w_b * l_b
                    a_b = w_o * a_s[c, :] + w_b * a_b
                    m_b = m_n
                if last:
                    o_ref[c, :] = a_b / l_b
                else:
                    m_s[c, :] = m_b
                    l_s[c, :] = l_b
                    a_s[c, :] = a_b

            in_flight = []
            for i in range(len(its) + ATTN_LAG):
                if i < len(its):
                    in_flight.append(issue_scores(its[i]))
                if i >= ATTN_LAG:
                    finish(*in_flight.pop(0))

        def run(lo, hi, with_prev):
            u = ATTN_UNROLL
            trips = (hi - lo) // u

            def body(t, carry):
                blocks([lo + t * u + j for j in range(u)], with_prev)
                return carry

            if trips:
                lax.fori_loop(0, trips, body, 0)
            if lo + trips * u < hi:
                blocks(list(range(lo + trips * u, hi)), with_prev)

        run(0, dil, False)
        run(dil, dil * nb, True)

    order = sorted(DILATIONS, reverse=True)
    for i, dil in enumerate(order):
        run_branch(dil, i == 0, i == len(order) - 1)


def _attn_sample_kernel(q_ref, kn_ref, vn_ref, kt_ref, vt_ref, o_ref):
    n_heads, dh, w_buf = kt_ref.shape
    delta = w_buf - lax.broadcasted_iota(jnp.int32, (1, w_buf), 1)
    cnt = jnp.zeros((1, w_buf), F32)
    for dil in DILATIONS:
        assert dil & (dil - 1) == 0
        member = jnp.where(delta <= dil * WIN_KEYS, 1.0, 0.0)
        cnt = cnt + jnp.where((delta & (dil - 1)) == 0, member, 0.0)
    eye = (lax.broadcasted_iota(jnp.int32, (dh, dh), 0)
           == lax.broadcasted_iota(jnp.int32, (dh, dh), 1))
    to_col = lambda r: jnp.sum(jnp.where(eye, r, 0.0), axis=1, keepdims=True)
    to_row = lambda c: jnp.sum(jnp.where(eye, c, 0.0), axis=0, keepdims=True)
    outs = []
    for h in range(n_heads):
        sl = slice(h * dh, (h + 1) * dh)
        q = q_ref[:, sl] * (HEAD_DIM ** -0.5)
        s_self = jnp.sum(q * kn_ref[:, sl], axis=1, keepdims=True)
        s = jnp.sum(to_col(q) * kt_ref[h], axis=0, keepdims=True)
        s = jnp.where(cnt > 0.0, s, NEG)
        m = jnp.maximum(jnp.max(s, axis=1, keepdims=True), s_self)
        p = cnt * jnp.exp(s - m)
        p_self = len(DILATIONS) * jnp.exp(s_self - m)
        l = jnp.sum(p, axis=1, keepdims=True) + p_self
        acc = jnp.sum(p * vt_ref[h], axis=1, keepdims=True)
        outs.append((to_row(acc) + p_self * vn_ref[:, sl]) / l)
    o_ref[...] = jnp.concatenate(outs, axis=1)


def _attn_kernel(q_ref, k_ref, v_ref, qs_ref, kn_ref, vn_ref, kt_ref, vt_ref, o_ref, os_ref,
                 m_s, l_s, a_s, *, seq_len):
    _attn_sample_kernel(qs_ref, kn_ref, vn_ref, kt_ref, vt_ref, os_ref)
    _attn_prompt_kernel(q_ref, k_ref, v_ref, o_ref, m_s, l_s, a_s, seq_len=seq_len)


def _attention(q, k, v, qs, k_new, v_new, cache_kt, cache_vt, *, n_seq, seq_len):
    t, d_attn = q.shape
    db, n_heads, dh, w_buf = cache_kt.shape
    pair = 2 * HEAD_DIM
    pairs = d_attn // pair
    assert db == n_seq * pairs, "one sample sequence per prompt grid step"
    spec = pl.BlockSpec((seq_len, pair), lambda b, h: (b, h))
    head_spec = pl.BlockSpec((None, 1, d_attn), lambda b, h: (b * pairs + h, 0, 0))
    cache_spec = pl.BlockSpec((None, n_heads, dh, w_buf), lambda b, h: (b * pairs + h, 0, 0, 0))
    return pl.pallas_call(
        functools.partial(_attn_kernel, seq_len=seq_len),
        grid=(n_seq, pairs),
        in_specs=[spec] * 3 + [head_spec] * 3 + [cache_spec] * 2,
        out_specs=[spec, head_spec],
        out_shape=[jax.ShapeDtypeStruct((t, d_attn), F32),
                   jax.ShapeDtypeStruct((db, 1, d_attn), F32)],
        scratch_shapes=[pltpu.VMEM((seq_len, pair), F32)] * 3,
        compiler_params=pltpu.CompilerParams(
            dimension_semantics=("arbitrary", "arbitrary"), vmem_limit_bytes=VMEM_LIMIT),
        name="attention",
    )(q, k, v, qs, k_new, v_new, cache_kt, cache_vt)


R_E0, R_E1, R_G0, R_G1, R_POS0, R_POS1 = range(6)
ROUTER_LANE0 = N_GROUPS
CHUNK = 16
CHUNKS_PER_BLOCK = EXPERT_BLOCK // CHUNK


def _max_tile_chunks(tm):
    return (2 * tm + (CHUNK - 1) * N_EXPERTS) // CHUNK


def _local_rows(tm):
    return 2 * tm + N_EXPERTS * CHUNK


def _mix_out_kernel(*refs, n_tiles, has_tail):
    if not has_tail:
        _mix_out_tile(*refs)
        return
    *tile_in, tail_ref, h_ref, route_ref, xs_ref, cnt_ref = refs

    @pl.when(pl.program_id(0) < n_tiles)
    def _():
        _mix_out_tile(*tile_in, h_ref, route_ref, xs_ref, cnt_ref)

    @pl.when(pl.program_id(0) == n_tiles)
    def _():
        rows = tail_ref.shape[0]
        xs_ref[0:rows, :] = tail_ref[...]
        xs_ref[rows:, :] = jnp.zeros((xs_ref.shape[0] - rows, xs_ref.shape[1]), xs_ref.dtype)


def _mix_out_tile(x_ref, a_ref, oc_ref, ga_ref, wo_ref, gf_ref, wr_ref, br_ref,
                  h_ref, route_ref, xs_ref, cnt_ref):
    d_attn = a_ref.shape[1]
    tm, d = x_ref.shape
    a = _rms(a_ref[...], ga_ref[...]).astype(BF16)
    mix = jnp.dot(a, wo_ref[0:d_attn, :], preferred_element_type=F32)
    mix = mix + jnp.dot(oc_ref[...].astype(BF16), wo_ref[d_attn:, :], preferred_element_type=F32)
    h = x_ref[...] + mix
    h_ref[...] = h
    tok = _rms(h, gf_ref[...])

    tok_hi = tok.astype(BF16)
    tok_lo = (tok - tok_hi.astype(F32)).astype(BF16)
    hi_part = jnp.dot(tok_hi, wr_ref[...], preferred_element_type=F32)
    lo_part = jnp.dot(tok_lo, wr_ref[:, :LANES], preferred_element_type=F32)
    logits = hi_part[:, :LANES] + hi_part[:, LANES:] + lo_part + br_ref[...]
    lane = lax.broadcasted_iota(jnp.int32, logits.shape, 1)
    big = jnp.int32(LANES)
    neg_inf = jnp.float32(-jnp.inf)

    def top1(vals):
        best = jnp.max(vals, axis=-1, keepdims=True)
        idx = jnp.min(jnp.where(vals == best, lane, big), axis=-1, keepdims=True)
        return best, idx

    is_group = lane < N_GROUPS
    lg = jnp.where(is_group, logits, neg_inf)
    mg, g_sel = top1(lg)
    p_group = 1.0 / jnp.sum(jnp.where(is_group, jnp.exp(lg - mg), 0.0), axis=-1, keepdims=True)

    lo = ROUTER_LANE0 + g_sel * EXPERTS_PER_GROUP
    in_group = jnp.logical_and(lane >= lo, lane < lo + EXPERTS_PER_GROUP)
    le = jnp.where(in_group, logits, neg_inf)
    v1, i1 = top1(le)
    v2, i2 = top1(jnp.where(lane == i1, neg_inf, le))
    e2 = jnp.exp(v2 - v1)
    gate1 = p_group / (1.0 + e2)
    gate2 = p_group * e2 / (1.0 + e2)

    oh1 = lane == i1
    oh2 = lane == i2
    both = jnp.where(jnp.logical_or(oh1, oh2), 1.0, 0.0)
    r_i = lax.broadcasted_iota(jnp.int32, (tm, tm), 0)
    c_i = lax.broadcasted_iota(jnp.int32, (tm, tm), 1)
    strict_lower = jnp.where(c_i < r_i, 1.0, 0.0).astype(BF16)
    before = jnp.dot(strict_lower, both.astype(BF16), preferred_element_type=F32)
    chunks = jnp.floor((jnp.sum(both, axis=0, keepdims=True) + (CHUNK - 1)) * (1.0 / CHUNK))
    u_r = lax.broadcasted_iota(jnp.int32, (LANES, LANES), 0)
    u_c = lax.broadcasted_iota(jnp.int32, (LANES, LANES), 1)
    strict_upper = jnp.where(u_r < u_c, 1.0, 0.0).astype(BF16)
    chunks8 = jnp.broadcast_to(chunks, (8, LANES))
    first_row = CHUNK * jnp.dot(chunks8.astype(BF16), strict_upper,
                                preferred_element_type=F32)[0:1, :]
    pos = first_row + before
    pos1 = jnp.sum(jnp.where(oh1, pos, 0.0), axis=-1, keepdims=True)
    pos2 = jnp.sum(jnp.where(oh2, pos, 0.0), axis=-1, keepdims=True)
    cnt_ref[...] = jnp.where(lax.broadcasted_iota(jnp.int32, (8, LANES), 0) == 0, chunks8, 0.0)

    rec = jnp.zeros(logits.shape, F32)
    for col, val in ((R_E0, (i1 - ROUTER_LANE0).astype(F32)), (R_E1, (i2 - ROUTER_LANE0).astype(F32)),
                     (R_G0, gate1), (R_G1, gate2), (R_POS0, pos1), (R_POS1, pos2)):
        rec = jnp.where(lane == col, val, rec)
    route_ref[...] = rec

    rec_t = rec.T
    l1 = rec_t[R_POS0:R_POS0 + 1, :].astype(jnp.int32)
    l2 = rec_t[R_POS1:R_POS1 + 1, :].astype(jnp.int32)
    srow = lax.broadcasted_iota(jnp.int32, (xs_ref.shape[0], tm), 0)
    perm = jnp.where(srow == l1, 1.0, jnp.where(srow == l2, 1.0, 0.0)).astype(BF16)
    xs_ref[...] = jnp.dot(perm, tok_hi, preferred_element_type=F32).astype(BF16)


def _mix_out(x2d, attn, oconv, norm_ga, w_out_bf16, norm_gf, w_router, b_router, tail=None):
    t, d = x2d.shape
    d_attn, d_conv = attn.shape[1], oconv.shape[1]
    tm = min(ROW_TILE, t)
    nt = t // tm
    r_l = _local_rows(tm)
    has_tail = tail is not None
    tile = lambda i: jnp.minimum(i, nt - 1)
    row = lambda width: pl.BlockSpec((tm, width), lambda i: (tile(i), 0))
    full = lambda arr: pl.BlockSpec(arr.shape, lambda i: (0, 0))
    args = [x2d, attn, oconv, norm_ga, w_out_bf16, norm_gf, w_router, b_router]
    in_specs = [row(d), row(d_attn), row(d_conv)] + [full(a) for a in args[3:]]
    if has_tail:
        assert tail.shape[0] <= r_l and tail.shape[1] == d
        args.append(tail)
        in_specs.append(full(tail))
    return pl.pallas_call(
        functools.partial(_mix_out_kernel, n_tiles=nt, has_tail=has_tail),
        grid=(nt + has_tail,),
        in_specs=in_specs,
        out_specs=[row(d), row(LANES), pl.BlockSpec((r_l, d), lambda i: (i, 0)),
                   pl.BlockSpec((None, 8, LANES), lambda i: (tile(i), 0, 0))],
        out_shape=[jax.ShapeDtypeStruct((t, d), F32), jax.ShapeDtypeStruct((t, LANES), F32),
                   jax.ShapeDtypeStruct(((nt + has_tail) * r_l, d), BF16),
                   jax.ShapeDtypeStruct((nt, 8, LANES), F32)],
        compiler_params=pltpu.CompilerParams(
            dimension_semantics=("arbitrary",), vmem_limit_bytes=VMEM_LIMIT),
        name="mix_out",
    )(*args)


def _sorted_layout(tile_chunks, tile_row0, max_local, n_blocks):
    nt, n_exp = tile_chunks.shape
    cpb = CHUNKS_PER_BLOCK
    i32 = jnp.int32
    seg = jnp.sum(tile_chunks, axis=0)
    padded = (seg + cpb - 1) // cpb * cpb
    pend = jnp.cumsum(padded)
    pstart = pend - padded
    tile_incl = jnp.cumsum(tile_chunks, axis=0)
    tile_excl = tile_incl - tile_chunks
    local_incl = jnp.cumsum(tile_chunks, axis=1)
    local_excl = local_incl - tile_chunks
    base = pstart[None, :] + tile_excl

    block_first = jnp.arange(n_blocks, dtype=i32) * cpb
    block_e = jnp.minimum(jnp.sum((pend[None, :] <= block_first[:, None]).astype(i32), axis=1),
                          n_exp - 1)
    n_used = (pend[-1:] // cpb).astype(i32)

    onehot_pick = lambda onehot, table: jnp.sum(jnp.where(onehot, table, 0), axis=-1)

    is_e = block_e[:, None] == jnp.arange(n_exp, dtype=i32)[None, :]
    of_expert = lambda table_te: onehot_pick(is_e[:, None, :], table_te[None, :, :])
    incl_b, cnt_b, lexcl_b = of_expert(tile_incl), of_expert(tile_chunks), of_expert(local_excl)
    q = (block_first - onehot_pick(is_e, pstart[None, :]))[:, None] + jnp.arange(cpb, dtype=i32)
    tile_q = jnp.minimum(jnp.sum((incl_b[:, None, :] <= q[:, :, None]).astype(i32), axis=2), nt - 1)
    is_t = tile_q[:, :, None] == jnp.arange(nt, dtype=i32)[None, None, :]
    of_tile = lambda table_bt: onehot_pick(is_t, table_bt[:, None, :])
    local_chunk = of_tile(lexcl_b) + q - of_tile(incl_b - cnt_b)
    in_run = jnp.logical_and(q >= 0, q < onehot_pick(is_e, seg[None, :])[:, None])
    src_row = jnp.where(in_run, of_tile(tile_row0[None, :]) + CHUNK * local_chunk, 0)
    src_row = src_row.reshape(-1).astype(i32)

    c = jnp.arange(max_local, dtype=i32)
    e_c = jnp.minimum(jnp.sum((local_incl[:, None, :] <= c[None, :, None]).astype(i32), axis=2),
                      n_exp - 1)
    is_ec = e_c[:, :, None] == jnp.arange(n_exp, dtype=i32)[None, None, :]
    of_run = lambda table_te: onehot_pick(is_ec, table_te[:, None, :])
    global_chunk = of_run(base) + c[None, :] - of_run(local_excl)
    tile_src = jnp.where(c[None, :] < local_incl[:, -1:], CHUNK * global_chunk, 0).astype(i32)
    e_ids = jnp.arange(n_exp, dtype=i32)
    later = jnp.logical_and(seg[None, :] > 0, e_ids[None, :] > e_ids[:, None])
    next_e = jnp.min(jnp.where(later, e_ids[None, :], n_exp), axis=1)
    next_e = jnp.where(next_e < n_exp, next_e, -1).astype(i32)
    return block_e.astype(i32), n_used, src_row, tile_src, next_e


def _chunk_gather(src_ref, hbm_ref, buf, sems, item, slot, n_chunks, *, wait):
    for c in range(n_chunks):
        row = 0 if wait else pl.multiple_of(src_ref[item * n_chunks + c], CHUNK)
        copy = pltpu.make_async_copy(hbm_ref.at[pl.ds(row, CHUNK)],
                                     buf.at[slot, pl.ds(c * CHUNK, CHUNK)], sems.at[slot])
        if wait:
            copy.wait()
        else:
            copy.start()


def _prefetched(gather, step, n_items, body):
    slot = step % 2

    @pl.when(jnp.logical_and(step == 0, n_items > 0))
    def _():
        gather(0, 0, wait=False)

    @pl.when(step + 1 < n_items)
    def _():
        gather(step + 1, 1 - slot, wait=False)

    body(slot, lambda: gather(step, slot, wait=True))


def _experts_kernel(block_e_ref, n_used_ref, src_ref, next_e_ref, xs_ref, wg_hbm, wu_hbm, wd_hbm,
                    y_hbm, xblk, sems, ybuf, ysems, wg_f, wu_f, wd_f, wsems, wg_b, wu_b, wd_b,
                    run_ref):
    rows = EXPERT_BLOCK
    n_blocks = y_hbm.shape[0] // rows
    n_used = n_used_ref[0]
    gather = functools.partial(_chunk_gather, src_ref, xs_ref, xblk, sems,
                               n_chunks=CHUNKS_PER_BLOCK)

    def weight_copies(expert, slot):
        copies = []
        for hbm, stage in ((wg_hbm, wg_f), (wu_hbm, wu_f), (wd_hbm, wd_f)):
            slab = hbm.shape[1] // WEIGHT_SLABS
            for i in range(WEIGHT_SLABS):
                rows_i = pl.ds(i * slab, slab)
                copies.append(pltpu.make_async_copy(hbm.at[expert, rows_i], stage.at[slot, rows_i],
                                                    wsems.at[slot]))
        return copies

    def y_copy(blk, slot):
        start = blk * rows if isinstance(blk, int) else pl.multiple_of(blk * rows, rows)
        return pltpu.make_async_copy(ybuf.at[slot], y_hbm.at[pl.ds(start, rows)], ysems.at[slot])

    def block(b, carry):
        e = block_e_ref[b]
        new_expert = jnp.logical_or(b == 0, e != block_e_ref[jnp.maximum(b - 1, 0)])

        @pl.when(jnp.logical_and(new_expert, b < n_used))
        def _():
            @pl.when(b == 0)
            def _():
                run_ref[0] = 0
                for copy in weight_copies(e, 0):
                    copy.start()

            @pl.when(b > 0)
            def _():
                run_ref[0] = run_ref[0] + 1

            slot = run_ref[0] % 2
            nxt = next_e_ref[e]

            @pl.when(nxt >= 0)
            def _():
                for copy in weight_copies(nxt, 1 - slot):
                    copy.start()

            for copy in weight_copies(e, slot):
                copy.wait()
            wg_b[...] = wg_f[slot].astype(BF16)
            wu_b[...] = wu_f[slot].astype(BF16)
            wd_b[...] = wd_f[slot].astype(BF16)

        def body(slot, wait_current):
            @pl.when(b >= 2)
            def _():
                y_copy(b - 2, slot).wait()

            @pl.when(b < n_used)
            def _():
                wait_current()
                x = xblk[slot]
                gate = jnp.dot(x, wg_b[...], preferred_element_type=F32)
                up = jnp.dot(x, wu_b[...], preferred_element_type=F32)
                hid = gate * (1.0 / (1.0 + jnp.exp(-gate))) * up
                ybuf[slot] = jnp.dot(hid.astype(BF16), wd_b[...],
                                     preferred_element_type=F32).astype(BF16)

            @pl.when(b >= n_used)
            def _():
                ybuf[slot] = jnp.zeros(ybuf.shape[1:], ybuf.dtype)

            y_copy(b, slot).start()

        _prefetched(gather, b, n_used, body)
        return carry

    lax.fori_loop(0, n_blocks, block, 0)
    for blk in range(max(n_blocks - 2, 0), n_blocks):
        y_copy(blk, blk % 2).wait()


def _experts(block_e, n_used, src_row, next_e, xs, w_gate, w_up, w_down):
    n_blocks = block_e.shape[0]
    _, d, d_exp = w_gate.shape
    blk = EXPERT_BLOCK
    any_spec = pl.BlockSpec(memory_space=pl.ANY)
    return pl.pallas_call(
        _experts_kernel,
        grid_spec=pltpu.PrefetchScalarGridSpec(
            num_scalar_prefetch=4,
            grid=(1,),
            in_specs=[any_spec] * 4,
            out_specs=any_spec,
            scratch_shapes=[pltpu.VMEM((2, blk, d), BF16), pltpu.SemaphoreType.DMA((2,)),
                            pltpu.VMEM((2, blk, d), BF16), pltpu.SemaphoreType.DMA((2,)),
                            pltpu.VMEM((2, d, d_exp), F32), pltpu.VMEM((2, d, d_exp), F32),
                            pltpu.VMEM((2, d_exp, d), F32), pltpu.SemaphoreType.DMA((2,)),
                            pltpu.VMEM((d, d_exp), BF16), pltpu.VMEM((d, d_exp), BF16),
                            pltpu.VMEM((d_exp, d), BF16), pltpu.SMEM((1,), jnp.int32)],
        ),
        out_shape=jax.ShapeDtypeStruct((n_blocks * blk, d), BF16),
        compiler_params=pltpu.CompilerParams(
            dimension_semantics=("arbitrary",), vmem_limit_bytes=VMEM_LIMIT),
        name="experts",
    )(block_e, n_used, src_row, next_e, xs, w_gate, w_up, w_down)


def _combine_kernel(src_ref, h_ref, route_ref, gn_ref, ybuf_ref, o_ref, yloc, sems):
    tm = h_ref.shape[0]
    r_l = yloc.shape[1]
    gather = functools.partial(_chunk_gather, src_ref, ybuf_ref, yloc, sems,
                               n_chunks=r_l // CHUNK)

    def body(slot, wait_current):
        wait_current()
        y = yloc[slot]
        route = route_ref[...]
        l0 = route[:, R_POS0:R_POS0 + 1].astype(jnp.int32)
        l1 = route[:, R_POS1:R_POS1 + 1].astype(jnp.int32)
        srow = lax.broadcasted_iota(jnp.int32, (tm, r_l), 1)
        gates = jnp.where(srow == l0, route[:, R_G0:R_G0 + 1],
                          jnp.where(srow == l1, route[:, R_G1:R_G1 + 1], 0.0)).astype(BF16)
        f = jnp.dot(gates, y, preferred_element_type=F32)
        o_ref[...] = _rms(h_ref[...] + f, gn_ref[...])

    _prefetched(gather, pl.program_id(0), pl.num_programs(0), body)


def _combine(tile_src, h, route, norm_g, ybuf):
    t, d = h.shape
    tm = min(ROW_TILE, t)
    r_l = _local_rows(tm)
    return pl.pallas_call(
        _combine_kernel,
        grid_spec=pltpu.PrefetchScalarGridSpec(
            num_scalar_prefetch=1,
            grid=(t // tm,),
            in_specs=[pl.BlockSpec((tm, d), lambda i, src: (i, 0)),
                      pl.BlockSpec((tm, LANES), lambda i, src: (i, 0)),
                      pl.BlockSpec((1, d), lambda i, src: (0, 0)),
                      pl.BlockSpec(memory_space=pl.ANY)],
            out_specs=pl.BlockSpec((tm, d), lambda i, src: (i, 0)),
            scratch_shapes=[pltpu.VMEM((2, r_l, d), BF16),
                            pltpu.SemaphoreType.DMA((2,))],
        ),
        out_shape=jax.ShapeDtypeStruct((t, d), F32),
        compiler_params=pltpu.CompilerParams(
            dimension_semantics=("arbitrary",), vmem_limit_bytes=VMEM_LIMIT),
        name="combine",
    )(tile_src, h, route, norm_g, ybuf)


def kernel(x_prompt, x_sample, cache_k, cache_v, state_conv, norm_mix, w_in, conv_w, norm_out_attn,
           norm_out_conv, w_out, norm_ffn, w_router_group, b_router_group, w_router_expert,
           b_router_expert, w_gate, w_up, w_down, norm_final):
    n_seq, seq_len, d = x_prompt.shape
    db, ds, _ = x_sample.shape
    depth = w_in.shape[0]
    _, _, w_buf, n_heads, dh = cache_k.shape
    d_attn = n_heads * dh
    d_conv = d - d_attn
    assert depth == 1 and ds == 1 and dh == HEAD_DIM
    assert seq_len % (max(DILATIONS) * WIN_KEYS) == 0 and seq_len <= max(DILATIONS) * WIN_KEYS
    layer = 0
    tp, ts = n_seq * seq_len, db

    xp = x_prompt.reshape(tp, d)
    xs = x_sample.reshape(ts, d)
    row = lambda vec: vec.reshape(1, -1)
    w_in_b = w_in[layer].astype(BF16)
    w_out_b = w_out[layer].astype(BF16)
    g_mix, g_oa, g_oc, g_ffn = (row(norm_mix[layer]), row(norm_out_attn[layer]),
                                row(norm_out_conv[layer]), row(norm_ffn[layer]))
    st0, st1 = state_conv[layer, :, 0, :], state_conv[layer, :, 1, :]

    qp, kp, vp, kp_t, vp_t, ocp, conv_p = _mix_in_prompt(
        xp, g_mix, w_in_b, conv_w[layer], g_oc, seq_len=seq_len, d_attn=d_attn, d_conv=d_conv)
    qs, ks, vs, ocs, us = _mix_in_sample(
        xs, g_mix, w_in_b, conv_w[layer], g_oc, st0, st1, d_attn=d_attn, d_conv=d_conv)

    heads = lambda a: a.reshape(ts, 1, d_attn)
    positions_last = lambda c: jnp.transpose(c, (0, 2, 3, 1))
    attn_p, attn_s = _attention(qp, kp, vp, heads(qs), heads(ks), heads(vs),
                                positions_last(cache_k[layer]), positions_last(cache_v[layer]),
                                n_seq=n_seq, seq_len=seq_len)
    attn_s = attn_s.reshape(ts, d_attn)

    n_route = N_GROUPS + N_EXPERTS
    w_router = jnp.zeros((d, LANES), F32).at[:, :N_GROUPS].set(w_router_group[layer])
    w_router = w_router.at[:, N_GROUPS:n_route].set(w_router_expert[layer])
    b_router = jnp.zeros((1, LANES), F32).at[0, :N_GROUPS].set(b_router_group[layer])
    b_router = b_router.at[0, N_GROUPS:n_route].set(b_router_expert[layer])
    w_router_hi = w_router.astype(BF16)
    w_router_lo = (w_router - w_router_hi.astype(F32)).astype(BF16)
    mix_out = functools.partial(_mix_out, norm_ga=g_oa, w_out_bf16=w_out_b, norm_gf=g_ffn,
                                w_router=jnp.concatenate([w_router_hi, w_router_lo], axis=1),
                                b_router=b_router)
    h_s, route_s, xs_s, cnt_s = mix_out(xs, attn_s, ocs)
    assert cnt_s.shape[0] == 1
    h_p, route_p, xs_all, cnt_p = mix_out(xp, attn_p, ocp, tail=xs_s)

    tile_chunks = jnp.concatenate([cnt_p[:, 0, ROUTER_LANE0:n_route],
                                   cnt_s[:, 0, ROUTER_LANE0:n_route]], axis=0).astype(jnp.int32)
    ntp, nts = cnt_p.shape[0], cnt_s.shape[0]
    tm_p, tm_s = tp // ntp, ts // nts
    rl_p, rl_s = _local_rows(tm_p), _local_rows(tm_s)
    tile_row0 = jnp.arange(ntp + nts, dtype=jnp.int32) * rl_p
    total_chunks = ntp * _max_tile_chunks(tm_p) + nts * _max_tile_chunks(tm_s)
    n_blocks = -(-(total_chunks + N_EXPERTS * (CHUNKS_PER_BLOCK - 1)) // CHUNKS_PER_BLOCK)
    block_e, n_used, src_row, tile_src, next_e = _sorted_layout(
        tile_chunks, tile_row0, rl_p // CHUNK, n_blocks)
    ybuf = _experts(block_e, n_used, src_row, next_e, xs_all, w_gate[layer], w_up[layer],
                    w_down[layer])
    g_fin = row(norm_final)
    y_p = _combine(tile_src[:ntp].reshape(-1), h_p, route_p, g_fin, ybuf)
    y_s = _combine(tile_src[ntp:, :rl_s // CHUNK].reshape(-1), h_s, route_s, g_fin, ybuf)

    w_keep = min(max(DILATIONS) * WIN_KEYS, seq_len)
    kv5 = lambda a_t: jnp.transpose(a_t.reshape(n_seq, n_heads, dh, seq_len),
                                    (0, 3, 1, 2))[None, :, seq_len - w_keep:]
    conv_s = jnp.stack([st1, us], axis=1)[None]
    kvs = lambda a: a.reshape(1, ts, 1, n_heads, dh)
    return (y_p.reshape(n_seq, seq_len, d), y_s.reshape(db, ds, d), kv5(kp_t), kv5(vp_t),
            conv_p[None], kvs(ks), kvs(vs), conv_s)
```

```python
import functools

import jax
import jax.numpy as jnp
from jax import lax
from jax.experimental import pallas as pl
from jax.experimental.pallas import tpu as pltpu

HEAD_DIM = 64
WIN_KEYS = 128
DILATIONS = (1, 4, 16)
CONV_WIDTH = 3
N_GROUPS = 4
EXPERTS_PER_GROUP = 8
N_EXPERTS = N_GROUPS * EXPERTS_PER_GROUP
EPS = 1e-6
NEG = -1e30
LOG2_E = 1.4426950408889634

LANES = 128
ROW_TILE = 512
MIX_IN_TILE = 1024
EXPERT_BLOCK = 512
ATTN_LAG = 3
WEIGHT_SLABS = 8
ATTN_UNROLL = 8
VMEM_LIMIT = 56 * 1024 * 1024

F32 = jnp.float32
BF16 = jnp.bfloat16


def _rms(x, g):
    return x * lax.rsqrt(jnp.mean(x * x, axis=-1, keepdims=True) + EPS) * g


def _mix_in_kernel(*refs, d_attn, d_conv, sequential):
    if sequential:
        (x_ref, g_ref, w_ref, cw_ref, gc_ref,
         q_ref, k_ref, v_ref, kt_ref, vt_ref, oc_ref, st_ref, carry_ref) = refs
    else:
        (x_ref, g_ref, w_ref, cw_ref, gc_ref, st0_ref, st1_ref,
         q_ref, k_ref, v_ref, oc_ref, u_ref) = refs
    x = x_ref[...]
    xb = _rms(x, g_ref[...]).astype(BF16)

    def proj(lo, width):
        return jnp.dot(xb, w_ref[:, lo:lo + width], preferred_element_type=F32)

    q_ref[...] = proj(0, d_attn)
    k = proj(d_attn, d_attn)
    v = proj(2 * d_attn, d_attn)
    k_ref[...] = k
    v_ref[...] = v
    gate = proj(3 * d_attn, d_conv)
    u = proj(3 * d_attn + d_conv, d_conv) * proj(3 * d_attn + 2 * d_conv, d_conv)

    tm = x.shape[0]
    if sequential:
        kt_ref[...] = k.T
        vt_ref[...] = v.T

        @pl.when(pl.program_id(1) == 0)
        def _():
            carry_ref[...] = jnp.zeros_like(carry_ref)

        row = lax.broadcasted_iota(jnp.int32, u.shape, 0)
        prev1 = carry_ref[1:2, :]
        prev2 = carry_ref[0:1, :]
        u1 = jnp.where(row == 0, prev1, pltpu.roll(u, 1, axis=0))
        u2 = jnp.where(row == 0, prev2, jnp.where(row == 1, prev1, pltpu.roll(u, 2, axis=0)))
        carry_ref[0:2, :] = u[tm - 2:tm, :]
        st_ref[...] = u[tm - 2:tm, :]
    else:
        u_ref[...] = u
        u2 = st0_ref[...]
        u1 = st1_ref[...]
    z = u2 * cw_ref[0:1, :] + u1 * cw_ref[1:2, :] + u * cw_ref[2:3, :]
    oc_ref[...] = _rms(gate * z, gc_ref[...])


def _mix_in_call(kernel, grid, in_specs, out_specs, out_shape, scratch, args):
    return pl.pallas_call(
        kernel, grid=grid, in_specs=in_specs, out_specs=out_specs, out_shape=out_shape,
        scratch_shapes=scratch,
        compiler_params=pltpu.CompilerParams(
            dimension_semantics=("arbitrary",) * len(grid), vmem_limit_bytes=VMEM_LIMIT),
        name="mix_in",
    )(*args)


def _mix_in_prompt(x2d, norm_g, w_in_bf16, conv_w, norm_gc, *, seq_len, d_attn, d_conv):
    t, d = x2d.shape
    tm = min(MIX_IN_TILE, seq_len)
    n_seq, per = t // seq_len, seq_len // tm
    const = lambda b, s: (0, 0)
    row = lambda width: pl.BlockSpec((tm, width), lambda b, s: (b * per + s, 0))
    col = pl.BlockSpec((None, d_attn, tm), lambda b, s: (b, 0, s))
    f32 = lambda *shape: jax.ShapeDtypeStruct(shape, F32)
    return _mix_in_call(
        functools.partial(_mix_in_kernel, d_attn=d_attn, d_conv=d_conv, sequential=True),
        (n_seq, per),
        [row(d), pl.BlockSpec((1, d), const),
         pl.BlockSpec(w_in_bf16.shape, const, pipeline_mode=pl.Buffered(1)),
         pl.BlockSpec((CONV_WIDTH, d_conv), const), pl.BlockSpec((1, d_conv), const)],
        [row(d_attn)] * 3 + [col] * 2 + [row(d_conv),
                                         pl.BlockSpec((None, CONV_WIDTH - 1, d_conv),
                                                      lambda b, s: (b, 0, 0))],
        [f32(t, d_attn)] * 3 + [f32(n_seq, d_attn, seq_len)] * 2
        + [f32(t, d_conv), f32(n_seq, CONV_WIDTH - 1, d_conv)],
        [pltpu.VMEM((8, d_conv), F32)],
        (x2d, norm_g, w_in_bf16, conv_w, norm_gc))


def _mix_in_sample(x2d, norm_g, w_in_bf16, conv_w, norm_gc, st0, st1, *, d_attn, d_conv):
    t, d = x2d.shape
    full = lambda arr: pl.BlockSpec(arr.shape, lambda i: (0,) * arr.ndim)
    f32 = lambda *shape: jax.ShapeDtypeStruct(shape, F32)
    args = (x2d, norm_g, w_in_bf16, conv_w, norm_gc, st0, st1)
    outs = [f32(t, d_attn)] * 3 + [f32(t, d_conv)] * 2
    return _mix_in_call(
        functools.partial(_mix_in_kernel, d_attn=d_attn, d_conv=d_conv, sequential=False),
        (1,), [full(a) for a in args], [full(o) for o in outs], outs, [], args)


def _attn_prompt_kernel(q_ref, k_ref, v_ref, o_ref, m_s, l_s, a_s, *, seq_len):
    w = WIN_KEYS
    scale = HEAD_DIM ** -0.5 * LOG2_E
    r_i = lax.broadcasted_iota(jnp.int32, (2 * w, 2 * w), 0) & (w - 1)
    c_i = lax.broadcasted_iota(jnp.int32, (2 * w, 2 * w), 1)
    mask_cur = (lax.broadcasted_iota(jnp.int32, (2 * w, w), 1)
                <= lax.broadcasted_iota(jnp.int32, (2 * w, w), 0) & (w - 1))
    mask_both = jnp.logical_and(c_i >= r_i, c_i - w <= r_i)
    first_head = lax.broadcasted_iota(jnp.int32, (w, 2 * HEAD_DIM), 1) < HEAD_DIM
    dn_t = (((1,), (1,)), ((), ()))

    def rows(start, dil):
        if dil > 1:
            return pl.ds(start, w, stride=dil)
        return pl.ds(start if isinstance(start, int) else pl.multiple_of(start, w), w)

    def run_branch(dil, first, last):
        span = dil * w
        nb = seq_len // span

        def blocks(its, with_prev):
            mask = mask_both if with_prev else mask_cur

            def issue_scores(it):
                g = it % dil
                n = it // dil
                c = rows(g + n * span, dil)
                qb = (q_ref[c, :] * scale).astype(BF16)
                zero = jnp.zeros_like(qb)
                q = jnp.concatenate([jnp.where(first_head, qb, zero),
                                     jnp.where(first_head, zero, qb)], axis=0)
                k = k_ref[c, :].astype(BF16)
                v = v_ref[c, :].astype(BF16)
                if with_prev:
                    p = rows(g + (n - 1) * span, dil)
                    k = jnp.concatenate([k_ref[p, :].astype(BF16), k], axis=0)
                    v = jnp.concatenate([v_ref[p, :].astype(BF16), v], axis=0)
                return c, lax.dot_general(q, k, dn_t, preferred_element_type=F32), v

            def finish(c, s, v):
                s = jnp.where(mask, s, NEG)
                m = jnp.max(s, axis=-1, keepdims=True)
                p = jnp.exp2(s - m).astype(BF16)
                ones = jnp.ones((v.shape[0], 2 * HEAD_DIM), BF16)
                acc_l = jnp.dot(p, jnp.concatenate([v, ones], axis=1), preferred_element_type=F32)
                acc, l = acc_l[:, :2 * HEAD_DIM], acc_l[:, 2 * HEAD_DIM:]
                m_b = jnp.where(first_head, m[:w], m[w:])
                l_b = jnp.where(first_head, l[:w], l[w:])
                a_b = jnp.where(first_head, acc[:w], acc[w:])
                if not first:
                    m_o = m_s[c, :]
                    m_n = jnp.maximum(m_o, m_b)
                    w_o = jnp.exp2(m_o - m_n)
                    w_b = jnp.exp2(m_b - m_n)
                    l_b = w_o * l_s[c, :] + w_b * l_b
                    a_b = w_o * a_s[c, :] + w_b * a_b
                    m_b = m_n
                if last:
                    o_ref[c, :] = a_b / l_b
                else:
                    m_s[c, :] = m_b
                    l_s[c, :] = l_b
                    a_s[c, :] = a_b

            in_flight = []
            for i in range(len(its) + ATTN_LAG):
                if i < len(its):
                    in_flight.append(issue_scores(its[i]))
                if i >= ATTN_LAG:
                    finish(*in_flight.pop(0))

        def run(lo, hi, with_prev):
            u = ATTN_UNROLL
            trips = (hi - lo) // u

            def body(t, carry):
                blocks([lo + t * u + j for j in range(u)], with_prev)
                return carry

            if trips:
                lax.fori_loop(0, trips, body, 0)
            if lo + trips * u < hi:
                blocks(list(range(lo + trips * u, hi)), with_prev)

        run(0, dil, False)
        run(dil, dil * nb, True)

    order = sorted(DILATIONS, reverse=True)
    for i, dil in enumerate(order):
        run_branch(dil, i == 0, i == len(order) - 1)


def _attn_sample_kernel(q_ref, kn_ref, vn_ref, kt_ref, vt_ref, o_ref):
    n_heads, dh, w_buf = kt_ref.shape
    delta = w_buf - lax.broadcasted_iota(jnp.int32, (1, w_buf), 1)
    cnt = jnp.zeros((1, w_buf), F32)
    for dil in DILATIONS:
        assert dil & (dil - 1) == 0
        member = jnp.where(delta <= dil * WIN_KEYS, 1.0, 0.0)
        cnt = cnt + jnp.where((delta & (dil - 1)) == 0, member, 0.0)
    eye = (lax.broadcasted_iota(jnp.int32, (dh, dh), 0)
           == lax.broadcasted_iota(jnp.int32, (dh, dh), 1))
    to_col = lambda r: jnp.sum(jnp.where(eye, r, 0.0), axis=1, keepdims=True)
    to_row = lambda c: jnp.sum(jnp.where(eye, c, 0.0), axis=0, keepdims=True)
    outs = []
    for h in range(n_heads):
        sl = slice(h * dh, (h + 1) * dh)
        q = q_ref[:, sl] * (HEAD_DIM ** -0.5)
        s_self = jnp.sum(q * kn_ref[:, sl], axis=1, keepdims=True)
        s = jnp.sum(to_col(q) * kt_ref[h], axis=0, keepdims=True)
        s = jnp.where(cnt > 0.0, s, NEG)
        m = jnp.maximum(jnp.max(s, axis=1, keepdims=True), s_self)
        p = cnt * jnp.exp(s - m)
        p_self = len(DILATIONS) * jnp.exp(s_self - m)
        l = jnp.sum(p, axis=1, keepdims=True) + p_self
        acc = jnp.sum(p * vt_ref[h], axis=1, keepdims=True)
        outs.append((to_row(acc) + p_self * vn_ref[:, sl]) / l)
    o_ref[...] = jnp.concatenate(outs, axis=1)


def _attn_kernel(q_ref, k_ref, v_ref, qs_ref, kn_ref, vn_ref, kt_ref, vt_ref, o_ref, os_ref,
                 m_s, l_s, a_s, *, seq_len):
    _attn_sample_kernel(qs_ref, kn_ref, vn_ref, kt_ref, vt_ref, os_ref)
    _attn_prompt_kernel(q_ref, k_ref, v_ref, o_ref, m_s, l_s, a_s, seq_len=seq_len)


def _attention(q, k, v, qs, k_new, v_new, cache_kt, cache_vt, *, n_seq, seq_len):
    t, d_attn = q.shape
    db, n_heads, dh, w_buf = cache_kt.shape
    pair = 2 * HEAD_DIM
    pairs = d_attn // pair
    assert db == n_seq * pairs, "one sample sequence per prompt grid step"
    spec = pl.BlockSpec((seq_len, pair), lambda b, h: (b, h))
    head_spec = pl.BlockSpec((None, 1, d_attn), lambda b, h: (b * pairs + h, 0, 0))
    cache_spec = pl.BlockSpec((None, n_heads, dh, w_buf), lambda b, h: (b * pairs + h, 0, 0, 0))
    return pl.pallas_call(
        functools.partial(_attn_kernel, seq_len=seq_len),
        grid=(n_seq, pairs),
        in_specs=[spec] * 3 + [head_spec] * 3 + [cache_spec] * 2,
        out_specs=[spec, head_spec],
        out_shape=[jax.ShapeDtypeStruct((t, d_attn), F32),
                   jax.ShapeDtypeStruct((db, 1, d_attn), F32)],
        scratch_shapes=[pltpu.VMEM((seq_len, pair), F32)] * 3,
        compiler_params=pltpu.CompilerParams(
            dimension_semantics=("arbitrary", "arbitrary"), vmem_limit_bytes=VMEM_LIMIT),
        name="attention",
    )(q, k, v, qs, k_new, v_new, cache_kt, cache_vt)


R_E0, R_E1, R_G0, R_G1, R_POS0, R_POS1 = range(6)
ROUTER_LANE0 = N_GROUPS
CHUNK = 16
CHUNKS_PER_BLOCK = EXPERT_BLOCK // CHUNK


def _max_tile_chunks(tm):
    return (2 * tm + (CHUNK - 1) * N_EXPERTS) // CHUNK


def _local_rows(tm):
    return 2 * tm + N_EXPERTS * CHUNK


def _mix_out_kernel(*refs, n_tiles, has_tail):
    if not has_tail:
        _mix_out_tile(*refs)
        return
    *tile_in, tail_ref, h_ref, route_ref, xs_ref, cnt_ref = refs

    @pl.when(pl.program_id(0) < n_tiles)
    def _():
        _mix_out_tile(*tile_in, h_ref, route_ref, xs_ref, cnt_ref)

    @pl.when(pl.program_id(0) == n_tiles)
    def _():
        rows = tail_ref.shape[0]
        xs_ref[0:rows, :] = tail_ref[...]
        xs_ref[rows:, :] = jnp.zeros((xs_ref.shape[0] - rows, xs_ref.shape[1]), xs_ref.dtype)


def _mix_out_tile(x_ref, a_ref, oc_ref, ga_ref, wo_ref, gf_ref, wr_ref, br_ref,
                  h_ref, route_ref, xs_ref, cnt_ref):
    d_attn = a_ref.shape[1]
    tm, d = x_ref.shape
    a = _rms(a_ref[...], ga_ref[...]).astype(BF16)
    mix = jnp.dot(a, wo_ref[0:d_attn, :], preferred_element_type=F32)
    mix = mix + jnp.dot(oc_ref[...].astype(BF16), wo_ref[d_attn:, :], preferred_element_type=F32)
    h = x_ref[...] + mix
    h_ref[...] = h
    tok = _rms(h, gf_ref[...])

    tok_hi = tok.astype(BF16)
    tok_lo = (tok - tok_hi.astype(F32)).astype(BF16)
    hi_part = jnp.dot(tok_hi, wr_ref[...], preferred_element_type=F32)
    lo_part = jnp.dot(tok_lo, wr_ref[:, :LANES], preferred_element_type=F32)
    logits = hi_part[:, :LANES] + hi_part[:, LANES:] + lo_part + br_ref[...]
    lane = lax.broadcasted_iota(jnp.int32, logits.shape, 1)
    big = jnp.int32(LANES)
    neg_inf = jnp.float32(-jnp.inf)

    def top1(vals):
        best = jnp.max(vals, axis=-1, keepdims=True)
        idx = jnp.min(jnp.where(vals == best, lane, big), axis=-1, keepdims=True)
        return best, idx

    is_group = lane < N_GROUPS
    lg = jnp.where(is_group, logits, neg_inf)
    mg, g_sel = top1(lg)
    p_group = 1.0 / jnp.sum(jnp.where(is_group, jnp.exp(lg - mg), 0.0), axis=-1, keepdims=True)

    lo = ROUTER_LANE0 + g_sel * EXPERTS_PER_GROUP
    in_group = jnp.logical_and(lane >= lo, lane < lo + EXPERTS_PER_GROUP)
    le = jnp.where(in_group, logits, neg_inf)
    v1, i1 = top1(le)
    v2, i2 = top1(jnp.where(lane == i1, neg_inf, le))
    e2 = jnp.exp(v2 - v1)
    gate1 = p_group / (1.0 + e2)
    gate2 = p_group * e2 / (1.0 + e2)

    oh1 = lane == i1
    oh2 = lane == i2
    both = jnp.where(jnp.logical_or(oh1, oh2), 1.0, 0.0)
    r_i = lax.broadcasted_iota(jnp.int32, (tm, tm), 0)
    c_i = lax.broadcasted_iota(jnp.int32, (tm, tm), 1)
    strict_lower = jnp.where(c_i < r_i, 1.0, 0.0).astype(BF16)
    before = jnp.dot(strict_lower, both.astype(BF16), preferred_element_type=F32)
    chunks = jnp.floor((jnp.sum(both, axis=0, keepdims=True) + (CHUNK - 1)) * (1.0 / CHUNK))
    u_r = lax.broadcasted_iota(jnp.int32, (LANES, LANES), 0)
    u_c = lax.broadcasted_iota(jnp.int32, (LANES, LANES), 1)
    strict_upper = jnp.where(u_r < u_c, 1.0, 0.0).astype(BF16)
    chunks8 = jnp.broadcast_to(chunks, (8, LANES))
    first_row = CHUNK * jnp.dot(chunks8.astype(BF16), strict_upper,
                                preferred_element_type=F32)[0:1, :]
    pos = first_row + before
    pos1 = jnp.sum(jnp.where(oh1, pos, 0.0), axis=-1, keepdims=True)
    pos2 = jnp.sum(jnp.where(oh2, pos, 0.0), axis=-1, keepdims=True)
    cnt_ref[...] = jnp.where(lax.broadcasted_iota(jnp.int32, (8, LANES), 0) == 0, chunks8, 0.0)

    rec = jnp.zeros(logits.shape, F32)
    for col, val in ((R_E0, (i1 - ROUTER_LANE0).astype(F32)), (R_E1, (i2 - ROUTER_LANE0).astype(F32)),
                     (R_G0, gate1), (R_G1, gate2), (R_POS0, pos1), (R_POS1, pos2)):
        rec = jnp.where(lane == col, val, rec)
    route_ref[...] = rec

    rec_t = rec.T
    l1 = rec_t[R_POS0:R_POS0 + 1, :].astype(jnp.int32)
    l2 = rec_t[R_POS1:R_POS1 + 1, :].astype(jnp.int32)
    srow = lax.broadcasted_iota(jnp.int32, (xs_ref.shape[0], tm), 0)
    perm = jnp.where(srow == l1, 1.0, jnp.where(srow == l2, 1.0, 0.0)).astype(BF16)
    xs_ref[...] = jnp.dot(perm, tok_hi, preferred_element_type=F32).astype(BF16)


def _mix_out(x2d, attn, oconv, norm_ga, w_out_bf16, norm_gf, w_router, b_router, tail=None):
    t, d = x2d.shape
    d_attn, d_conv = attn.shape[1], oconv.shape[1]
    tm = min(ROW_TILE, t)
    nt = t // tm
    r_l = _local_rows(tm)
    has_tail = tail is not None
    tile = lambda i: jnp.minimum(i, nt - 1)
    row = lambda width: pl.BlockSpec((tm, width), lambda i: (tile(i), 0))
    full = lambda arr: pl.BlockSpec(arr.shape, lambda i: (0, 0))
    args = [x2d, attn, oconv, norm_ga, w_out_bf16, norm_gf, w_router, b_router]
    in_specs = [row(d), row(d_attn), row(d_conv)] + [full(a) for a in args[3:]]
    if has_tail:
        assert tail.shape[0] <= r_l and tail.shape[1] == d
        args.append(tail)
        in_specs.append(full(tail))
    return pl.pallas_call(
        functools.partial(_mix_out_kernel, n_tiles=nt, has_tail=has_tail),
        grid=(nt + has_tail,),
        in_specs=in_specs,
        out_specs=[row(d), row(LANES), pl.BlockSpec((r_l, d), lambda i: (i, 0)),
                   pl.BlockSpec((None, 8, LANES), lambda i: (tile(i), 0, 0))],
        out_shape=[jax.ShapeDtypeStruct((t, d), F32), jax.ShapeDtypeStruct((t, LANES), F32),
                   jax.ShapeDtypeStruct(((nt + has_tail) * r_l, d), BF16),
                   jax.ShapeDtypeStruct((nt, 8, LANES), F32)],
        compiler_params=pltpu.CompilerParams(
            dimension_semantics=("arbitrary",), vmem_limit_bytes=VMEM_LIMIT),
        name="mix_out",
    )(*args)


def _sorted_layout(tile_chunks, tile_row0, max_local, n_blocks):
    nt, n_exp = tile_chunks.shape
    cpb = CHUNKS_PER_BLOCK
    i32 = jnp.int32
    seg = jnp.sum(tile_chunks, axis=0)
    padded = (seg + cpb - 1) // cpb * cpb
    pend = jnp.cumsum(padded)
    pstart = pend - padded
    tile_incl = jnp.cumsum(tile_chunks, axis=0)
    tile_excl = tile_incl - tile_chunks
    local_incl = jnp.cumsum(tile_chunks, axis=1)
    local_excl = local_incl - tile_chunks
    base = pstart[None, :] + tile_excl

    block_first = jnp.arange(n_blocks, dtype=i32) * cpb
    block_e = jnp.minimum(jnp.sum((pend[None, :] <= block_first[:, None]).astype(i32), axis=1),
                          n_exp - 1)
    n_used = (pend[-1:] // cpb).astype(i32)

    onehot_pick = lambda onehot, table: jnp.sum(jnp.where(onehot, table, 0), axis=-1)

    is_e = block_e[:, None] == jnp.arange(n_exp, dtype=i32)[None, :]
    of_expert = lambda table_te: onehot_pick(is_e[:, None, :], table_te[None, :, :])
    incl_b, cnt_b, lexcl_b = of_expert(tile_incl), of_expert(tile_chunks), of_expert(local_excl)
    q = (block_first - onehot_pick(is_e, pstart[None, :]))[:, None] + jnp.arange(cpb, dtype=i32)
    tile_q = jnp.minimum(jnp.sum((incl_b[:, None, :] <= q[:, :, None]).astype(i32), axis=2), nt - 1)
    is_t = tile_q[:, :, None] == jnp.arange(nt, dtype=i32)[None, None, :]
    of_tile = lambda table_bt: onehot_pick(is_t, table_bt[:, None, :])
    local_chunk = of_tile(lexcl_b) + q - of_tile(incl_b - cnt_b)
    in_run = jnp.logical_and(q >= 0, q < onehot_pick(is_e, seg[None, :])[:, None])
    src_row = jnp.where(in_run, of_tile(tile_row0[None, :]) + CHUNK * local_chunk, 0)
    src_row = src_row.reshape(-1).astype(i32)

    c = jnp.arange(max_local, dtype=i32)
    e_c = jnp.minimum(jnp.sum((local_incl[:, None, :] <= c[None, :, None]).astype(i32), axis=2),
                      n_exp - 1)
    is_ec = e_c[:, :, None] == jnp.arange(n_exp, dtype=i32)[None, None, :]
    of_run = lambda table_te: onehot_pick(is_ec, table_te[:, None, :])
    global_chunk = of_run(base) + c[None, :] - of_run(local_excl)
    tile_src = jnp.where(c[None, :] < local_incl[:, -1:], CHUNK * global_chunk, 0).astype(i32)
    e_ids = jnp.arange(n_exp, dtype=i32)
    later = jnp.logical_and(seg[None, :] > 0, e_ids[None, :] > e_ids[:, None])
    next_e = jnp.min(jnp.where(later, e_ids[None, :], n_exp), axis=1)
    next_e = jnp.where(next_e < n_exp, next_e, -1).astype(i32)
    return block_e.astype(i32), n_used, src_row, tile_src, next_e


def _chunk_gather(src_ref, hbm_ref, buf, sems, item, slot, n_chunks, *, wait):
    for c in range(n_chunks):
        row = 0 if wait else pl.multiple_of(src_ref[item * n_chunks + c], CHUNK)
        copy = pltpu.make_async_copy(hbm_ref.at[pl.ds(row, CHUNK)],
                                     buf.at[slot, pl.ds(c * CHUNK, CHUNK)], sems.at[slot])
        if wait:
            copy.wait()
        else:
            copy.start()


def _prefetched(gather, step, n_items, body):
    slot = step % 2

    @pl.when(jnp.logical_and(step == 0, n_items > 0))
    def _():
        gather(0, 0, wait=False)

    @pl.when(step + 1 < n_items)
    def _():
        gather(step + 1, 1 - slot, wait=False)

    body(slot, lambda: gather(step, slot, wait=True))


def _experts_kernel(block_e_ref, n_used_ref, src_ref, next_e_ref, xs_ref, wg_hbm, wu_hbm, wd_hbm,
                    y_hbm, xblk, sems, ybuf, ysems, wg_f, wu_f, wd_f, wsems, wg_b, wu_b, wd_b,
                    run_ref):
    rows = EXPERT_BLOCK
    n_blocks = y_hbm.shape[0] // rows
    n_used = n_used_ref[0]
    gather = functools.partial(_chunk_gather, src_ref, xs_ref, xblk, sems,
                               n_chunks=CHUNKS_PER_BLOCK)

    def weight_copies(expert, slot):
        copies = []
        for hbm, stage in ((wg_hbm, wg_f), (wu_hbm, wu_f), (wd_hbm, wd_f)):
            slab = hbm.shape[1] // WEIGHT_SLABS
            for i in range(WEIGHT_SLABS):
                rows_i = pl.ds(i * slab, slab)
                copies.append(pltpu.make_async_copy(hbm.at[expert, rows_i], stage.at[slot, rows_i],
                                                    wsems.at[slot]))
        return copies

    def y_copy(blk, slot):
        start = blk * rows if isinstance(blk, int) else pl.multiple_of(blk * rows, rows)
        return pltpu.make_async_copy(ybuf.at[slot], y_hbm.at[pl.ds(start, rows)], ysems.at[slot])

    def block(b, carry):
        e = block_e_ref[b]
        new_expert = jnp.logical_or(b == 0, e != block_e_ref[jnp.maximum(b - 1, 0)])

        @pl.when(jnp.logical_and(new_expert, b < n_used))
        def _():
            @pl.when(b == 0)
            def _():
                run_ref[0] = 0
                for copy in weight_copies(e, 0):
                    copy.start()

            @pl.when(b > 0)
            def _():
                run_ref[0] = run_ref[0] + 1

            slot = run_ref[0] % 2
            nxt = next_e_ref[e]

            @pl.when(nxt >= 0)
            def _():
                for copy in weight_copies(nxt, 1 - slot):
                    copy.start()

            for copy in weight_copies(e, slot):
                copy.wait()
            wg_b[...] = wg_f[slot].astype(BF16)
            wu_b[...] = wu_f[slot].astype(BF16)
            wd_b[...] = wd_f[slot].astype(BF16)

        def body(slot, wait_current):
            @pl.when(b >= 2)
            def _():
                y_copy(b - 2, slot).wait()

            @pl.when(b < n_used)
            def _():
                wait_current()
                x = xblk[slot]
                gate = jnp.dot(x, wg_b[...], preferred_element_type=F32)
                up = jnp.dot(x, wu_b[...], preferred_element_type=F32)
                hid = gate * (1.0 / (1.0 + jnp.exp(-gate))) * up
                ybuf[slot] = jnp.dot(hid.astype(BF16), wd_b[...],
                                     preferred_element_type=F32).astype(BF16)

            @pl.when(b >= n_used)
            def _():
                ybuf[slot] = jnp.zeros(ybuf.shape[1:], ybuf.dtype)

            y_copy(b, slot).start()

        _prefetched(gather, b, n_used, body)
        return carry

    lax.fori_loop(0, n_blocks, block, 0)
    for blk in range(max(n_blocks - 2, 0), n_blocks):
        y_copy(blk, blk % 2).wait()


def _experts(block_e, n_used, src_row, next_e, xs, w_gate, w_up, w_down):
    n_blocks = block_e.shape[0]
    _, d, d_exp = w_gate.shape
    blk = EXPERT_BLOCK
    any_spec = pl.BlockSpec(memory_space=pl.ANY)
    return pl.pallas_call(
        _experts_kernel,
        grid_spec=pltpu.PrefetchScalarGridSpec(
            num_scalar_prefetch=4,
            grid=(1,),
            in_specs=[any_spec] * 4,
            out_specs=any_spec,
            scratch_shapes=[pltpu.VMEM((2, blk, d), BF16), pltpu.SemaphoreType.DMA((2,)),
                            pltpu.VMEM((2, blk, d), BF16), pltpu.SemaphoreType.DMA((2,)),
                            pltpu.VMEM((2, d, d_exp), F32), pltpu.VMEM((2, d, d_exp), F32),
                            pltpu.VMEM((2, d_exp, d), F32), pltpu.SemaphoreType.DMA((2,)),
                            pltpu.VMEM((d, d_exp), BF16), pltpu.VMEM((d, d_exp), BF16),
                            pltpu.VMEM((d_exp, d), BF16), pltpu.SMEM((1,), jnp.int32)],
        ),
        out_shape=jax.ShapeDtypeStruct((n_blocks * blk, d), BF16),
        compiler_params=pltpu.CompilerParams(
            dimension_semantics=("arbitrary",), vmem_limit_bytes=VMEM_LIMIT),
        name="experts",
    )(block_e, n_used, src_row, next_e, xs, w_gate, w_up, w_down)


def _combine_kernel(src_ref, h_ref, route_ref, gn_ref, ybuf_ref, o_ref, yloc, sems):
    tm = h_ref.shape[0]
    r_l = yloc.shape[1]
    gather = functools.partial(_chunk_gather, src_ref, ybuf_ref, yloc, sems,
                               n_chunks=r_l // CHUNK)

    def body(slot, wait_current):
        wait_current()
        y = yloc[slot]
        route = route_ref[...]
        l0 = route[:, R_POS0:R_POS0 + 1].astype(jnp.int32)
        l1 = route[:, R_POS1:R_POS1 + 1].astype(jnp.int32)
        srow = lax.broadcasted_iota(jnp.int32, (tm, r_l), 1)
        gates = jnp.where(srow == l0, route[:, R_G0:R_G0 + 1],
                          jnp.where(srow == l1, route[:, R_G1:R_G1 + 1], 0.0)).astype(BF16)
        f = jnp.dot(gates, y, preferred_element_type=F32)
        o_ref[...] = _rms(h_ref[...] + f, gn_ref[...])

    _prefetched(gather, pl.program_id(0), pl.num_programs(0), body)


def _combine(tile_src, h, route, norm_g, ybuf):
    t, d = h.shape
    tm = min(ROW_TILE, t)
    r_l = _local_rows(tm)
    return pl.pallas_call(
        _combine_kernel,
        grid_spec=pltpu.PrefetchScalarGridSpec(
            num_scalar_prefetch=1,
            grid=(t // tm,),
            in_specs=[pl.BlockSpec((tm, d), lambda i, src: (i, 0)),
                      pl.BlockSpec((tm, LANES), lambda i, src: (i, 0)),
                      pl.BlockSpec((1, d), lambda i, src: (0, 0)),
                      pl.BlockSpec(memory_space=pl.ANY)],
            out_specs=pl.BlockSpec((tm, d), lambda i, src: (i, 0)),
            scratch_shapes=[pltpu.VMEM((2, r_l, d), BF16),
                            pltpu.SemaphoreType.DMA((2,))],
        ),
        out_shape=jax.ShapeDtypeStruct((t, d), F32),
        compiler_params=pltpu.CompilerParams(
            dimension_semantics=("arbitrary",), vmem_limit_bytes=VMEM_LIMIT),
        name="combine",
    )(tile_src, h, route, norm_g, ybuf)


def kernel(x_prompt, x_sample, cache_k, cache_v, state_conv, norm_mix, w_in, conv_w, norm_out_attn,
           norm_out_conv, w_out, norm_ffn, w_router_group, b_router_group, w_router_expert,
           b_router_expert, w_gate, w_up, w_down, norm_final):
    n_seq, seq_len, d = x_prompt.shape
    db, ds, _ = x_sample.shape
    depth = w_in.shape[0]
    _, _, w_buf, n_heads, dh = cache_k.shape
    d_attn = n_heads * dh
    d_conv = d - d_attn
    assert depth == 1 and ds == 1 and dh == HEAD_DIM
    assert seq_len % (max(DILATIONS) * WIN_KEYS) == 0 and seq_len <= max(DILATIONS) * WIN_KEYS
    layer = 0
    tp, ts = n_seq * seq_len, db

    xp = x_prompt.reshape(tp, d)
    xs = x_sample.reshape(ts, d)
    row = lambda vec: vec.reshape(1, -1)
    w_in_b = w_in[layer].astype(BF16)
    w_out_b = w_out[layer].astype(BF16)
    g_mix, g_oa, g_oc, g_ffn = (row(norm_mix[layer]), row(norm_out_attn[layer]),
                                row(norm_out_conv[layer]), row(norm_ffn[layer]))
    st0, st1 = state_conv[layer, :, 0, :], state_conv[layer, :, 1, :]

    qp, kp, vp, kp_t, vp_t, ocp, conv_p = _mix_in_prompt(
        xp, g_mix, w_in_b, conv_w[layer], g_oc, seq_len=seq_len, d_attn=d_attn, d_conv=d_conv)
    qs, ks, vs, ocs, us = _mix_in_sample(
        xs, g_mix, w_in_b, conv_w[layer], g_oc, st0, st1, d_attn=d_attn, d_conv=d_conv)

    heads = lambda a: a.reshape(ts, 1, d_attn)
    positions_last = lambda c: jnp.transpose(c, (0, 2, 3, 1))
    attn_p, attn_s = _attention(qp, kp, vp, heads(qs), heads(ks), heads(vs),
                                positions_last(cache_k[layer]), positions_last(cache_v[layer]),
                                n_seq=n_seq, seq_len=seq_len)
    attn_s = attn_s.reshape(ts, d_attn)

    n_route = N_GROUPS + N_EXPERTS
    w_router = jnp.zeros((d, LANES), F32).at[:, :N_GROUPS].set(w_router_group[layer])
    w_router = w_router.at[:, N_GROUPS:n_route].set(w_router_expert[layer])
    b_router = jnp.zeros((1, LANES), F32).at[0, :N_GROUPS].set(b_router_group[layer])
    b_router = b_router.at[0, N_GROUPS:n_route].set(b_router_expert[layer])
    w_router_hi = w_router.astype(BF16)
    w_router_lo = (w_router - w_router_hi.astype(F32)).astype(BF16)
    mix_out = functools.partial(_mix_out, norm_ga=g_oa, w_out_bf16=w_out_b, norm_gf=g_ffn,
                                w_router=jnp.concatenate([w_router_hi, w_router_lo], axis=1),
                                b_router=b_router)
    h_s, route_s, xs_s, cnt_s = mix_out(xs, attn_s, ocs)
    assert cnt_s.shape[0] == 1
    h_p, route_p, xs_all, cnt_p = mix_out(xp, attn_p, ocp, tail=xs_s)

    tile_chunks = jnp.concatenate([cnt_p[:, 0, ROUTER_LANE0:n_route],
                                   cnt_s[:, 0, ROUTER_LANE0:n_route]], axis=0).astype(jnp.int32)
    ntp, nts = cnt_p.shape[0], cnt_s.shape[0]
    tm_p, tm_s = tp // ntp, ts // nts
    rl_p, rl_s = _local_rows(tm_p), _local_rows(tm_s)
    tile_row0 = jnp.arange(ntp + nts, dtype=jnp.int32) * rl_p
    total_chunks = ntp * _max_tile_chunks(tm_p) + nts * _max_tile_chunks(tm_s)
    n_blocks = -(-(total_chunks + N_EXPERTS * (CHUNKS_PER_BLOCK - 1)) // CHUNKS_PER_BLOCK)
    block_e, n_used, src_row, tile_src, next_e = _sorted_layout(
        tile_chunks, tile_row0, rl_p // CHUNK, n_blocks)
    ybuf = _experts(block_e, n_used, src_row, next_e, xs_all, w_gate[layer], w_up[layer],
                    w_down[layer])
    g_fin = row(norm_final)
    y_p = _combine(tile_src[:ntp].reshape(-1), h_p, route_p, g_fin, ybuf)
    y_s = _combine(tile_src[ntp:, :rl_s // CHUNK].reshape(-1), h_s, route_s, g_fin, ybuf)

    w_keep = min(max(DILATIONS) * WIN_KEYS, seq_len)
    kv5 = lambda a_t: jnp.transpose(a_t.reshape(n_seq, n_heads, dh, seq_len),
                                    (0, 3, 1, 2))[None, :, seq_len - w_keep:]
    conv_s = jnp.stack([st1, us], axis=1)[None]
    kvs = lambda a: a.reshape(1, ts, 1, n_heads, dh)
    return (y_p.reshape(n_seq, seq_len, d), y_s.reshape(db, ds, d), kv5(kp_t), kv5(vp_t),
            conv_p[None], kvs(ks), kvs(vs), conv_s)
```

```python
import functools

import jax
import jax.numpy as jnp
from jax import lax
from jax.experimental import pallas as pl
from jax.experimental.pallas import tpu as pltpu

HEAD_DIM = 64
WIN_KEYS = 128
DILATIONS = (1, 4, 16)
CONV_WIDTH = 3
N_GROUPS = 4
EXPERTS_PER_GROUP = 8
N_EXPERTS = N_GROUPS * EXPERTS_PER_GROUP
EPS = 1e-6
NEG = -1e30
LOG2_E = 1.4426950408889634

LANES = 128
ROW_TILE = 512
MIX_IN_TILE = 1024
EXPERT_BLOCK = 512
ATTN_LAG = 3
WEIGHT_SLABS = 8
ATTN_UNROLL = 8
VMEM_LIMIT = 56 * 1024 * 1024

F32 = jnp.float32
BF16 = jnp.bfloat16


def _rms(x, g):
    return x * lax.rsqrt(jnp.mean(x * x, axis=-1, keepdims=True) + EPS) * g


def _mix_in_kernel(*refs, d_attn, d_conv, sequential):
    if sequential:
        (x_ref, g_ref, w_ref, cw_ref, gc_ref,
         q_ref, k_ref, v_ref, kt_ref, vt_ref, oc_ref, st_ref, carry_ref) = refs
    else:
        (x_ref, g_ref, w_ref, cw_ref, gc_ref, st0_ref, st1_ref,
         q_ref, k_ref, v_ref, oc_ref, u_ref) = refs
    x = x_ref[...]
    xb = _rms(x, g_ref[...]).astype(BF16)

    def proj(lo, width):
        return jnp.dot(xb, w_ref[:, lo:lo + width], preferred_element_type=F32)

    q_ref[...] = proj(0, d_attn)
    k = proj(d_attn, d_attn)
    v = proj(2 * d_attn, d_attn)
    k_ref[...] = k
    v_ref[...] = v
    gate = proj(3 * d_attn, d_conv)
    u = proj(3 * d_attn + d_conv, d_conv) * proj(3 * d_attn + 2 * d_conv, d_conv)

    tm = x.shape[0]
    if sequential:
        kt_ref[...] = k.T
        vt_ref[...] = v.T

        @pl.when(pl.program_id(1) == 0)
        def _():
            carry_ref[...] = jnp.zeros_like(carry_ref)

        row = lax.broadcasted_iota(jnp.int32, u.shape, 0)
        prev1 = carry_ref[1:2, :]
        prev2 = carry_ref[0:1, :]
        u1 = jnp.where(row == 0, prev1, pltpu.roll(u, 1, axis=0))
        u2 = jnp.where(row == 0, prev2, jnp.where(row == 1, prev1, pltpu.roll(u, 2, axis=0)))
        carry_ref[0:2, :] = u[tm - 2:tm, :]
        st_ref[...] = u[tm - 2:tm, :]
    else:
        u_ref[...] = u
        u2 = st0_ref[...]
        u1 = st1_ref[...]
    z = u2 * cw_ref[0:1, :] + u1 * cw_ref[1:2, :] + u * cw_ref[2:3, :]
    oc_ref[...] = _rms(gate * z, gc_ref[...])


def _mix_in_call(kernel, grid, in_specs, out_specs, out_shape, scratch, args):
    return pl.pallas_call(
        kernel, grid=grid, in_specs=in_specs, out_specs=out_specs, out_shape=out_shape,
        scratch_shapes=scratch,
        compiler_params=pltpu.CompilerParams(
            dimension_semantics=("arbitrary",) * len(grid), vmem_limit_bytes=VMEM_LIMIT),
        name="mix_in",
    )(*args)


def _mix_in_prompt(x2d, norm_g, w_in_bf16, conv_w, norm_gc, *, seq_len, d_attn, d_conv):
    t, d = x2d.shape
    tm = min(MIX_IN_TILE, seq_len)
    n_seq, per = t // seq_len, seq_len // tm
    const = lambda b, s: (0, 0)
    row = lambda width: pl.BlockSpec((tm, width), lambda b, s: (b * per + s, 0))
    col = pl.BlockSpec((None, d_attn, tm), lambda b, s: (b, 0, s))
    f32 = lambda *shape: jax.ShapeDtypeStruct(shape, F32)
    return _mix_in_call(
        functools.partial(_mix_in_kernel, d_attn=d_attn, d_conv=d_conv, sequential=True),
        (n_seq, per),
        [row(d), pl.BlockSpec((1, d), const),
         pl.BlockSpec(w_in_bf16.shape, const, pipeline_mode=pl.Buffered(1)),
         pl.BlockSpec((CONV_WIDTH, d_conv), const), pl.BlockSpec((1, d_conv), const)],
        [row(d_attn)] * 3 + [col] * 2 + [row(d_conv),
                                         pl.BlockSpec((None, CONV_WIDTH - 1, d_conv),
                                                      lambda b, s: (b, 0, 0))],
        [f32(t, d_attn)] * 3 + [f32(n_seq, d_attn, seq_len)] * 2
        + [f32(t, d_conv), f32(n_seq, CONV_WIDTH - 1, d_conv)],
        [pltpu.VMEM((8, d_conv), F32)],
        (x2d, norm_g, w_in_bf16, conv_w, norm_gc))


def _mix_in_sample(x2d, norm_g, w_in_bf16, conv_w, norm_gc, st0, st1, *, d_attn, d_conv):
    t, d = x2d.shape
    full = lambda arr: pl.BlockSpec(arr.shape, lambda i: (0,) * arr.ndim)
    f32 = lambda *shape: jax.ShapeDtypeStruct(shape, F32)
    args = (x2d, norm_g, w_in_bf16, conv_w, norm_gc, st0, st1)
    outs = [f32(t, d_attn)] * 3 + [f32(t, d_conv)] * 2
    return _mix_in_call(
        functools.partial(_mix_in_kernel, d_attn=d_attn, d_conv=d_conv, sequential=False),
        (1,), [full(a) for a in args], [full(o) for o in outs], outs, [], args)


def _attn_prompt_kernel(q_ref, k_ref, v_ref, o_ref, m_s, l_s, a_s, *, seq_len):
    w = WIN_KEYS
    scale = HEAD_DIM ** -0.5 * LOG2_E
    r_i = lax.broadcasted_iota(jnp.int32, (2 * w, 2 * w), 0) & (w - 1)
    c_i = lax.broadcasted_iota(jnp.int32, (2 * w, 2 * w), 1)
    mask_cur = (lax.broadcasted_iota(jnp.int32, (2 * w, w), 1)
                <= lax.broadcasted_iota(jnp.int32, (2 * w, w), 0) & (w - 1))
    mask_both = jnp.logical_and(c_i >= r_i, c_i - w <= r_i)
    first_head = lax.broadcasted_iota(jnp.int32, (w, 2 * HEAD_DIM), 1) < HEAD_DIM
    dn_t = (((1,), (1,)), ((), ()))

    def rows(start, dil):
        if dil > 1:
            return pl.ds(start, w, stride=dil)
        return pl.ds(start if isinstance(start, int) else pl.multiple_of(start, w), w)

    def run_branch(dil, first, last):
        span = dil * w
        nb = seq_len // span

        def blocks(its, with_prev):
            mask = mask_both if with_prev else mask_cur

            def issue_scores(it):
                g = it % dil
                n = it // dil
                c = rows(g + n * span, dil)
                qb = (q_ref[c, :] * scale).astype(BF16)
                zero = jnp.zeros_like(qb)
                q = jnp.concatenate([jnp.where(first_head, qb, zero),
                                     jnp.where(first_head, zero, qb)], axis=0)
                k = k_ref[c, :].astype(BF16)
                v = v_ref[c, :].astype(BF16)
                if with_prev:
                    p = rows(g + (n - 1) * span, dil)
                    k = jnp.concatenate([k_ref[p, :].astype(BF16), k], axis=0)
                    v = jnp.concatenate([v_ref[p, :].astype(BF16), v], axis=0)
                return c, lax.dot_general(q, k, dn_t, preferred_element_type=F32), v

            def finish(c, s, v):
                s = jnp.where(mask, s, NEG)
                m = jnp.max(s, axis=-1, keepdims=True)
                p = jnp.exp2(s - m).astype(BF16)
                ones = jnp.ones((v.shape[0], 2 * HEAD_DIM), BF16)
                acc_l = jnp.dot(p, jnp.concatenate([v, ones], axis=1), preferred_element_type=F32)
                acc, l = acc_l[:, :2 * HEAD_DIM], acc_l[:, 2 * HEAD_DIM:]
                m_b = jnp.where(first_head, m[:w], m[w:])
                l_b = jnp.where(first_head, l[:w], l[w:])
                a_b = jnp.where(first_head, acc[:w], acc[w:])
                if not first:
                    m_o = m_s[c, :]
                    m_n = jnp.maximum(m_o, m_b)
                    w_o = jnp.exp2(m_o - m_n)
                    w_b = jnp.exp2(m_b - m_n)
                    l_b = w_o * l_s[c, :] + w_b * l_b
                    a_b = w_o * a_s[c, :] + w_b * a_b
                    m_b = m_n
                if last:
                    o_ref[c, :] = a_b / l_b
                else:
                    m_s[c, :] = m_b
                    l_s[c, :] = l_b
                    a_s[c, :] = a_b

            in_flight = []
            for i in range(len(its) + ATTN_LAG):
                if i < len(its):
                    in_flight.append(issue_scores(its[i]))
                if i >= ATTN_LAG:
                    finish(*in_flight.pop(0))

        def run(lo, hi, with_prev):
            u = ATTN_UNROLL
            trips = (hi - lo) // u

            def body(t, carry):
                blocks([lo + t * u + j for j in range(u)], with_prev)
                return carry

            if trips:
                lax.fori_loop(0, trips, body, 0)
            if lo + trips * u < hi:
                blocks(list(range(lo + trips * u, hi)), with_prev)

        run(0, dil, False)
        run(dil, dil * nb, True)

    order = sorted(DILATIONS, reverse=True)
    for i, dil in enumerate(order):
        run_branch(dil, i == 0, i == len(order) - 1)


def _attn_sample_kernel(q_ref, kn_ref, vn_ref, kt_ref, vt_ref, o_ref):
    n_heads, dh, w_buf = kt_ref.shape
    delta = w_buf - lax.broadcasted_iota(jnp.int32, (1, w_buf), 1)
    cnt = jnp.zeros((1, w_buf), F32)
    for dil in DILATIONS:
        assert dil & (dil - 1) == 0
        member = jnp.where(delta <= dil * WIN_KEYS, 1.0, 0.0)
        cnt = cnt + jnp.where((delta & (dil - 1)) == 0, member, 0.0)
    eye = (lax.broadcasted_iota(jnp.int32, (dh, dh), 0)
           == lax.broadcasted_iota(jnp.int32, (dh, dh), 1))
    to_col = lambda r: jnp.sum(jnp.where(eye, r, 0.0), axis=1, keepdims=True)
    to_row = lambda c: jnp.sum(jnp.where(eye, c, 0.0), axis=0, keepdims=True)
    outs = []
    for h in range(n_heads):
        sl = slice(h * dh, (h + 1) * dh)
        q = q_ref[:, sl] * (HEAD_DIM ** -0.5)
        s_self = jnp.sum(q * kn_ref[:, sl], axis=1, keepdims=True)
        s = jnp.sum(to_col(q) * kt_ref[h], axis=0, keepdims=True)
        s = jnp.where(cnt > 0.0, s, NEG)
        m = jnp.maximum(jnp.max(s, axis=1, keepdims=True), s_self)
        p = cnt * jnp.exp(s - m)
        p_self = len(DILATIONS) * jnp.exp(s_self - m)
        l = jnp.sum(p, axis=1, keepdims=True) + p_self
        acc = jnp.sum(p * vt_ref[h], axis=1, keepdims=True)
        outs.append((to_row(acc) + p_self * vn_ref[:, sl]) / l)
    o_ref[...] = jnp.concatenate(outs, axis=1)


def _attn_kernel(q_ref, k_ref, v_ref, qs_ref, kn_ref, vn_ref, kt_ref, vt_ref, o_ref, os_ref,
                 m_s, l_s, a_s, *, seq_len):
    _attn_sample_kernel(qs_ref, kn_ref, vn_ref, kt_ref, vt_ref, os_ref)
    _attn_prompt_kernel(q_ref, k_ref, v_ref, o_ref, m_s, l_s, a_s, seq_len=seq_len)


def _attention(q, k, v, qs, k_new, v_new, cache_kt, cache_vt, *, n_seq, seq_len):
    t, d_attn = q.shape
    db, n_heads, dh, w_buf = cache_kt.shape
    pair = 2 * HEAD_DIM
    pairs = d_attn // pair
    assert db == n_seq * pairs, "one sample sequence per prompt grid step"
    spec = pl.BlockSpec((seq_len, pair), lambda b, h: (b, h))
    head_spec = pl.BlockSpec((None, 1, d_attn), lambda b, h: (b * pairs + h, 0, 0))
    cache_spec = pl.BlockSpec((None, n_heads, dh, w_buf), lambda b, h: (b * pairs + h, 0, 0, 0))
    return pl.pallas_call(
        functools.partial(_attn_kernel, seq_len=seq_len),
        grid=(n_seq, pairs),
        in_specs=[spec] * 3 + [head_spec] * 3 + [cache_spec] * 2,
        out_specs=[spec, head_spec],
        out_shape=[jax.ShapeDtypeStruct((t, d_attn), F32),
                   jax.ShapeDtypeStruct((db, 1, d_attn), F32)],
        scratch_shapes=[pltpu.VMEM((seq_len, pair), F32)] * 3,
        compiler_params=pltpu.CompilerParams(
            dimension_semantics=("arbitrary", "arbitrary"), vmem_limit_bytes=VMEM_LIMIT),
        name="attention",
    )(q, k, v, qs, k_new, v_new, cache_kt, cache_vt)


R_E0, R_E1, R_G0, R_G1, R_POS0, R_POS1 = range(6)
ROUTER_LANE0 = N_GROUPS
CHUNK = 16
CHUNKS_PER_BLOCK = EXPERT_BLOCK // CHUNK


def _max_tile_chunks(tm):
    return (2 * tm + (CHUNK - 1) * N_EXPERTS) // CHUNK


def _local_rows(tm):
    return 2 * tm + N_EXPERTS * CHUNK


def _mix_out_kernel(*refs, n_tiles, has_tail):
    if has_tail:
        *tile_in, tail_ref, h_ref, route_ref, xs_ref, cnt_ref, tok_s, pos_s = refs
    else:
        *tile_in, h_ref, route_ref, xs_ref, cnt_ref, tok_s, pos_s = refs
    i = pl.program_id(0)
    route = lambda slot, *between: _route_tile(*tile_in, h_ref, route_ref, cnt_ref,
                                               tok_s.at[slot], pos_s.at[slot], *between)
    sort = lambda slot: _sort_tile(tok_s.at[slot], pos_s.at[slot], xs_ref)

    @pl.when(i == 0)
    def _():
        route(0)

    for parity in range(2):
        @pl.when(jnp.logical_and(jnp.logical_and(i >= 1, i < n_tiles), i % 2 == parity))
        def _():
            route(parity, lambda: sort(1 - parity))

    @pl.when(i == n_tiles)
    def _():
        sort((n_tiles - 1) % 2)

    if has_tail:
        @pl.when(i == n_tiles + 1)
        def _():
            rows = tail_ref.shape[0]
            xs_ref[0:rows, :] = tail_ref[...]
            xs_ref[rows:, :] = jnp.zeros((xs_ref.shape[0] - rows, xs_ref.shape[1]), xs_ref.dtype)


def _sort_tile(tok_ref, pos_ref, xs_ref):
    tm = tok_ref.shape[0]
    l1 = pos_ref[R_POS0:R_POS0 + 1, :].astype(jnp.int32)
    l2 = pos_ref[R_POS1:R_POS1 + 1, :].astype(jnp.int32)
    srow = lax.broadcasted_iota(jnp.int32, (xs_ref.shape[0], tm), 0)
    perm = jnp.where(srow == l1, 1.0, jnp.where(srow == l2, 1.0, 0.0)).astype(BF16)
    xs_ref[...] = jnp.dot(perm, tok_ref[...], preferred_element_type=F32).astype(BF16)


def _route_tile(x_ref, a_ref, oc_ref, ga_ref, wo_ref, gf_ref, wr_ref, br_ref,
                h_ref, route_ref, cnt_ref, tok_ref, pos_ref, after_projections=lambda: None):
    d_attn = a_ref.shape[1]
    tm, d = x_ref.shape
    a = _rms(a_ref[...], ga_ref[...]).astype(BF16)
    mix = jnp.dot(a, wo_ref[0:d_attn, :], preferred_element_type=F32)
    mix = mix + jnp.dot(oc_ref[...].astype(BF16), wo_ref[d_attn:, :], preferred_element_type=F32)
    h = x_ref[...] + mix
    h_ref[...] = h
    tok = _rms(h, gf_ref[...])

    tok_hi = tok.astype(BF16)
    tok_lo = (tok - tok_hi.astype(F32)).astype(BF16)
    hi_part = jnp.dot(tok_hi, wr_ref[...], preferred_element_type=F32)
    lo_part = jnp.dot(tok_lo, wr_ref[:, :LANES], preferred_element_type=F32)
    logits = hi_part[:, :LANES] + hi_part[:, LANES:] + lo_part + br_ref[...]
    after_projections()
    lane = lax.broadcasted_iota(jnp.int32, logits.shape, 1)
    big = jnp.int32(LANES)
    neg_inf = jnp.float32(-jnp.inf)

    def top1(vals):
        best = jnp.max(vals, axis=-1, keepdims=True)
        idx = jnp.min(jnp.where(vals == best, lane, big), axis=-1, keepdims=True)
        return best, idx

    is_group = lane < N_GROUPS
    lg = jnp.where(is_group, logits, neg_inf)
    mg, g_sel = top1(lg)
    p_group = 1.0 / jnp.sum(jnp.where(is_group, jnp.exp(lg - mg), 0.0), axis=-1, keepdims=True)

    lo = ROUTER_LANE0 + g_sel * EXPERTS_PER_GROUP
    in_group = jnp.logical_and(lane >= lo, lane < lo + EXPERTS_PER_GROUP)
    le = jnp.where(in_group, logits, neg_inf)
    v1, i1 = top1(le)
    v2, i2 = top1(jnp.where(lane == i1, neg_inf, le))
    e2 = jnp.exp(v2 - v1)
    gate1 = p_group / (1.0 + e2)
    gate2 = p_group * e2 / (1.0 + e2)

    oh1 = lane == i1
    oh2 = lane == i2
    both = jnp.where(jnp.logical_or(oh1, oh2), 1.0, 0.0)
    r_i = lax.broadcasted_iota(jnp.int32, (tm, tm), 0)
    c_i = lax.broadcasted_iota(jnp.int32, (tm, tm), 1)
    strict_lower = jnp.where(c_i < r_i, 1.0, 0.0).astype(BF16)
    before = jnp.dot(strict_lower, both.astype(BF16), preferred_element_type=F32)
    chunks = jnp.floor((jnp.sum(both, axis=0, keepdims=True) + (CHUNK - 1)) * (1.0 / CHUNK))
    u_r = lax.broadcasted_iota(jnp.int32, (LANES, LANES), 0)
    u_c = lax.broadcasted_iota(jnp.int32, (LANES, LANES), 1)
    strict_upper = jnp.where(u_r < u_c, 1.0, 0.0).astype(BF16)
    chunks8 = jnp.broadcast_to(chunks, (8, LANES))
    first_row = CHUNK * jnp.dot(chunks8.astype(BF16), strict_upper,
                                preferred_element_type=F32)[0:1, :]
    pos = first_row + before
    pos1 = jnp.sum(jnp.where(oh1, pos, 0.0), axis=-1, keepdims=True)
    pos2 = jnp.sum(jnp.where(oh2, pos, 0.0), axis=-1, keepdims=True)
    cnt_ref[...] = jnp.where(lax.broadcasted_iota(jnp.int32, (8, LANES), 0) == 0, chunks8, 0.0)

    rec = jnp.zeros(logits.shape, F32)
    for col, val in ((R_E0, (i1 - ROUTER_LANE0).astype(F32)), (R_E1, (i2 - ROUTER_LANE0).astype(F32)),
                     (R_G0, gate1), (R_G1, gate2), (R_POS0, pos1), (R_POS1, pos2)):
        rec = jnp.where(lane == col, val, rec)
    route_ref[...] = rec
    tok_ref[...] = tok_hi
    pos_ref[...] = rec.T[0:pos_ref.shape[0], :]


def _mix_out(x2d, attn, oconv, norm_ga, w_out_bf16, norm_gf, w_router, b_router, tail=None):
    t, d = x2d.shape
    d_attn, d_conv = attn.shape[1], oconv.shape[1]
    tm = min(ROW_TILE, t)
    nt = t // tm
    r_l = _local_rows(tm)
    has_tail = tail is not None
    tile = lambda i: jnp.minimum(i, nt - 1)
    sorted_block = lambda i: jnp.minimum(jnp.maximum(i - 1, 0), nt - 1 + has_tail)
    row = lambda width: pl.BlockSpec((tm, width), lambda i: (tile(i), 0))
    full = lambda arr: pl.BlockSpec(arr.shape, lambda i: (0, 0))
    args = [x2d, attn, oconv, norm_ga, w_out_bf16, norm_gf, w_router, b_router]
    in_specs = [row(d), row(d_attn), row(d_conv)] + [full(a) for a in args[3:]]
    if has_tail:
        assert tail.shape[0] <= r_l and tail.shape[1] == d
        args.append(tail)
        in_specs.append(full(tail))
    return pl.pallas_call(
        functools.partial(_mix_out_kernel, n_tiles=nt, has_tail=has_tail),
        grid=(nt + 1 + has_tail,),
        in_specs=in_specs,
        out_specs=[row(d), row(LANES), pl.BlockSpec((r_l, d), lambda i: (sorted_block(i), 0)),
                   pl.BlockSpec((None, 8, LANES), lambda i: (tile(i), 0, 0))],
        out_shape=[jax.ShapeDtypeStruct((t, d), F32), jax.ShapeDtypeStruct((t, LANES), F32),
                   jax.ShapeDtypeStruct(((nt + has_tail) * r_l, d), BF16),
                   jax.ShapeDtypeStruct((nt, 8, LANES), F32)],
        scratch_shapes=[pltpu.VMEM((2, tm, d), BF16), pltpu.VMEM((2, 8, tm), F32)],
        compiler_params=pltpu.CompilerParams(
            dimension_semantics=("arbitrary",), vmem_limit_bytes=VMEM_LIMIT),
        name="mix_out",
    )(*args)


def _sorted_layout(tile_chunks, tile_row0, max_local, n_blocks):
    nt, n_exp = tile_chunks.shape
    cpb = CHUNKS_PER_BLOCK
    i32 = jnp.int32
    seg = jnp.sum(tile_chunks, axis=0)
    padded = (seg + cpb - 1) // cpb * cpb
    pend = jnp.cumsum(padded)
    pstart = pend - padded
    tile_incl = jnp.cumsum(tile_chunks, axis=0)
    tile_excl = tile_incl - tile_chunks
    local_incl = jnp.cumsum(tile_chunks, axis=1)
    local_excl = local_incl - tile_chunks
    base = pstart[None, :] + tile_excl

    block_first = jnp.arange(n_blocks, dtype=i32) * cpb
    block_e = jnp.minimum(jnp.sum((pend[None, :] <= block_first[:, None]).astype(i32), axis=1),
                          n_exp - 1)
    n_used = (pend[-1:] // cpb).astype(i32)

    onehot_pick = lambda onehot, table: jnp.sum(jnp.where(onehot, table, 0), axis=-1)

    is_e = block_e[:, None] == jnp.arange(n_exp, dtype=i32)[None, :]
    of_expert = lambda table_te: onehot_pick(is_e[:, None, :], table_te[None, :, :])
    incl_b, cnt_b, lexcl_b = of_expert(tile_incl), of_expert(tile_chunks), of_expert(local_excl)
    q = (block_first - onehot_pick(is_e, pstart[None, :]))[:, None] + jnp.arange(cpb, dtype=i32)
    tile_q = jnp.minimum(jnp.sum((incl_b[:, None, :] <= q[:, :, None]).astype(i32), axis=2), nt - 1)
    is_t = tile_q[:, :, None] == jnp.arange(nt, dtype=i32)[None, None, :]
    of_tile = lambda table_bt: onehot_pick(is_t, table_bt[:, None, :])
    local_chunk = of_tile(lexcl_b) + q - of_tile(incl_b - cnt_b)
    in_run = jnp.logical_and(q >= 0, q < onehot_pick(is_e, seg[None, :])[:, None])
    src_row = jnp.where(in_run, of_tile(tile_row0[None, :]) + CHUNK * local_chunk, 0)
    src_row = src_row.reshape(-1).astype(i32)

    c = jnp.arange(max_local, dtype=i32)
    e_c = jnp.minimum(jnp.sum((local_incl[:, None, :] <= c[None, :, None]).astype(i32), axis=2),
                      n_exp - 1)
    is_ec = e_c[:, :, None] == jnp.arange(n_exp, dtype=i32)[None, None, :]
    of_run = lambda table_te: onehot_pick(is_ec, table_te[:, None, :])
    global_chunk = of_run(base) + c[None, :] - of_run(local_excl)
    tile_src = jnp.where(c[None, :] < local_incl[:, -1:], CHUNK * global_chunk, 0).astype(i32)
    e_ids = jnp.arange(n_exp, dtype=i32)
    later = jnp.logical_and(seg[None, :] > 0, e_ids[None, :] > e_ids[:, None])
    next_e = jnp.min(jnp.where(later, e_ids[None, :], n_exp), axis=1)
    next_e = jnp.where(next_e < n_exp, next_e, -1).astype(i32)
    return block_e.astype(i32), n_used, src_row, tile_src, next_e


def _chunk_gather(src_ref, hbm_ref, buf, sems, item, slot, n_chunks, *, wait):
    for c in range(n_chunks):
        row = 0 if wait else pl.multiple_of(src_ref[item * n_chunks + c], CHUNK)
        copy = pltpu.make_async_copy(hbm_ref.at[pl.ds(row, CHUNK)],
                                     buf.at[slot, pl.ds(c * CHUNK, CHUNK)], sems.at[slot])
        if wait:
            copy.wait()
        else:
            copy.start()


def _prefetched(gather, step, n_items, body):
    slot = step % 2

    @pl.when(jnp.logical_and(step == 0, n_items > 0))
    def _():
        gather(0, 0, wait=False)

    @pl.when(step + 1 < n_items)
    def _():
        gather(step + 1, 1 - slot, wait=False)

    body(slot, lambda: gather(step, slot, wait=True))


def _experts_kernel(block_e_ref, n_used_ref, src_ref, next_e_ref, xs_ref, wg_hbm, wu_hbm, wd_hbm,
                    y_hbm, xblk, sems, ybuf, ysems, wg_f, wu_f, wd_f, wsems, wg_b, wu_b, wd_b,
                    run_ref):
    rows = EXPERT_BLOCK
    n_blocks = y_hbm.shape[0] // rows
    n_used = n_used_ref[0]
    gather = functools.partial(_chunk_gather, src_ref, xs_ref, xblk, sems,
                               n_chunks=CHUNKS_PER_BLOCK)

    def weight_copies(expert, slot):
        copies = []
        for hbm, stage in ((wg_hbm, wg_f), (wu_hbm, wu_f), (wd_hbm, wd_f)):
            slab = hbm.shape[1] // WEIGHT_SLABS
            for i in range(WEIGHT_SLABS):
                rows_i = pl.ds(i * slab, slab)
                copies.append(pltpu.make_async_copy(hbm.at[expert, rows_i], stage.at[slot, rows_i],
                                                    wsems.at[slot]))
        return copies

    def y_copy(blk, slot):
        start = blk * rows if isinstance(blk, int) else pl.multiple_of(blk * rows, rows)
        return pltpu.make_async_copy(ybuf.at[slot], y_hbm.at[pl.ds(start, rows)], ysems.at[slot])

    def block(b, carry):
        e = block_e_ref[b]
        new_expert = jnp.logical_or(b == 0, e != block_e_ref[jnp.maximum(b - 1, 0)])

        @pl.when(jnp.logical_and(new_expert, b < n_used))
        def _():
            @pl.when(b == 0)
            def _():
                run_ref[0] = 0
                for copy in weight_copies(e, 0):
                    copy.start()

            @pl.when(b > 0)
            def _():
                run_ref[0] = run_ref[0] + 1

            slot = run_ref[0] % 2
            nxt = next_e_ref[e]

            @pl.when(nxt >= 0)
            def _():
                for copy in weight_copies(nxt, 1 - slot):
                    copy.start()

            for copy in weight_copies(e, slot):
                copy.wait()
            wg_b[...] = wg_f[slot].astype(BF16)
            wu_b[...] = wu_f[slot].astype(BF16)
            wd_b[...] = wd_f[slot].astype(BF16)

        def body(slot, wait_current):
            @pl.when(b >= 2)
            def _():
                y_copy(b - 2, slot).wait()

            @pl.when(b < n_used)
            def _():
                wait_current()
                x = xblk[slot]
                gate = jnp.dot(x, wg_b[...], preferred_element_type=F32)
                up = jnp.dot(x, wu_b[...], preferred_element_type=F32)
                hid = gate * (1.0 / (1.0 + jnp.exp(-gate))) * up
                ybuf[slot] = jnp.dot(hid.astype(BF16), wd_b[...],
                                     preferred_element_type=F32).astype(BF16)

            @pl.when(b >= n_used)
            def _():
                ybuf[slot] = jnp.zeros(ybuf.shape[1:], ybuf.dtype)

            y_copy(b, slot).start()

        _prefetched(gather, b, n_used, body)
        return carry

    lax.fori_loop(0, n_blocks, block, 0)
    for blk in range(max(n_blocks - 2, 0), n_blocks):
        y_copy(blk, blk % 2).wait()


def _experts(block_e, n_used, src_row, next_e, xs, w_gate, w_up, w_down):
    n_blocks = block_e.shape[0]
    _, d, d_exp = w_gate.shape
    blk = EXPERT_BLOCK
    any_spec = pl.BlockSpec(memory_space=pl.ANY)
    return pl.pallas_call(
        _experts_kernel,
        grid_spec=pltpu.PrefetchScalarGridSpec(
            num_scalar_prefetch=4,
            grid=(1,),
            in_specs=[any_spec] * 4,
            out_specs=any_spec,
            scratch_shapes=[pltpu.VMEM((2, blk, d), BF16), pltpu.SemaphoreType.DMA((2,)),
                            pltpu.VMEM((2, blk, d), BF16), pltpu.SemaphoreType.DMA((2,)),
                            pltpu.VMEM((2, d, d_exp), F32), pltpu.VMEM((2, d, d_exp), F32),
                            pltpu.VMEM((2, d_exp, d), F32), pltpu.SemaphoreType.DMA((2,)),
                            pltpu.VMEM((d, d_exp), BF16), pltpu.VMEM((d, d_exp), BF16),
                            pltpu.VMEM((d_exp, d), BF16), pltpu.SMEM((1,), jnp.int32)],
        ),
        out_shape=jax.ShapeDtypeStruct((n_blocks * blk, d), BF16),
        compiler_params=pltpu.CompilerParams(
            dimension_semantics=("arbitrary",), vmem_limit_bytes=VMEM_LIMIT),
        name="experts",
    )(block_e, n_used, src_row, next_e, xs, w_gate, w_up, w_down)


def _combine_kernel(src_ref, h_ref, route_ref, gn_ref, ybuf_ref, o_ref, yloc, sems):
    tm = h_ref.shape[0]
    r_l = yloc.shape[1]
    gather = functools.partial(_chunk_gather, src_ref, ybuf_ref, yloc, sems,
                               n_chunks=r_l // CHUNK)

    def body(slot, wait_current):
        wait_current()
        y = yloc[slot]
        route = route_ref[...]
        l0 = route[:, R_POS0:R_POS0 + 1].astype(jnp.int32)
        l1 = route[:, R_POS1:R_POS1 + 1].astype(jnp.int32)
        srow = lax.broadcasted_iota(jnp.int32, (tm, r_l), 1)
        gates = jnp.where(srow == l0, route[:, R_G0:R_G0 + 1],
                          jnp.where(srow == l1, route[:, R_G1:R_G1 + 1], 0.0)).astype(BF16)
        f = jnp.dot(gates, y, preferred_element_type=F32)
        o_ref[...] = _rms(h_ref[...] + f, gn_ref[...])

    _prefetched(gather, pl.program_id(0), pl.num_programs(0), body)


def _combine(tile_src, h, route, norm_g, ybuf):
    t, d = h.shape
    tm = min(ROW_TILE, t)
    r_l = _local_rows(tm)
    return pl.pallas_call(
        _combine_kernel,
        grid_spec=pltpu.PrefetchScalarGridSpec(
            num_scalar_prefetch=1,
            grid=(t // tm,),
            in_specs=[pl.BlockSpec((tm, d), lambda i, src: (i, 0)),
                      pl.BlockSpec((tm, LANES), lambda i, src: (i, 0)),
                      pl.BlockSpec((1, d), lambda i, src: (0, 0)),
                      pl.BlockSpec(memory_space=pl.ANY)],
            out_specs=pl.BlockSpec((tm, d), lambda i, src: (i, 0)),
            scratch_shapes=[pltpu.VMEM((2, r_l, d), BF16),
                            pltpu.SemaphoreType.DMA((2,))],
        ),
        out_shape=jax.ShapeDtypeStruct((t, d), F32),
        compiler_params=pltpu.CompilerParams(
            dimension_semantics=("arbitrary",), vmem_limit_bytes=VMEM_LIMIT),
        name="combine",
    )(tile_src, h, route, norm_g, ybuf)


def kernel(x_prompt, x_sample, cache_k, cache_v, state_conv, norm_mix, w_in, conv_w, norm_out_attn,
           norm_out_conv, w_out, norm_ffn, w_router_group, b_router_group, w_router_expert,
           b_router_expert, w_gate, w_up, w_down, norm_final):
    n_seq, seq_len, d = x_prompt.shape
    db, ds, _ = x_sample.shape
    depth = w_in.shape[0]
    _, _, w_buf, n_heads, dh = cache_k.shape
    d_attn = n_heads * dh
    d_conv = d - d_attn
    assert depth == 1 and ds == 1 and dh == HEAD_DIM
    assert seq_len % (max(DILATIONS) * WIN_KEYS) == 0 and seq_len <= max(DILATIONS) * WIN_KEYS
    layer = 0
    tp, ts = n_seq * seq_len, db

    xp = x_prompt.reshape(tp, d)
    xs = x_sample.reshape(ts, d)
    row = lambda vec: vec.reshape(1, -1)
    w_in_b = w_in[layer].astype(BF16)
    w_out_b = w_out[layer].astype(BF16)
    g_mix, g_oa, g_oc, g_ffn = (row(norm_mix[layer]), row(norm_out_attn[layer]),
                                row(norm_out_conv[layer]), row(norm_ffn[layer]))
    st0, st1 = state_conv[layer, :, 0, :], state_conv[layer, :, 1, :]

    qp, kp, vp, kp_t, vp_t, ocp, conv_p = _mix_in_prompt(
        xp, g_mix, w_in_b, conv_w[layer], g_oc, seq_len=seq_len, d_attn=d_attn, d_conv=d_conv)
    qs, ks, vs, ocs, us = _mix_in_sample(
        xs, g_mix, w_in_b, conv_w[layer], g_oc, st0, st1, d_attn=d_attn, d_conv=d_conv)

    heads = lambda a: a.reshape(ts, 1, d_attn)
    positions_last = lambda c: jnp.transpose(c, (0, 2, 3, 1))
    attn_p, attn_s = _attention(qp, kp, vp, heads(qs), heads(ks), heads(vs),
                                positions_last(cache_k[layer]), positions_last(cache_v[layer]),
                                n_seq=n_seq, seq_len=seq_len)
    attn_s = attn_s.reshape(ts, d_attn)

    n_route = N_GROUPS + N_EXPERTS
    w_router = jnp.zeros((d, LANES), F32).at[:, :N_GROUPS].set(w_router_group[layer])
    w_router = w_router.at[:, N_GROUPS:n_route].set(w_router_expert[layer])
    b_router = jnp.zeros((1, LANES), F32).at[0, :N_GROUPS].set(b_router_group[layer])
    b_router = b_router.at[0, N_GROUPS:n_route].set(b_router_expert[layer])
    w_router_hi = w_router.astype(BF16)
    w_router_lo = (w_router - w_router_hi.astype(F32)).astype(BF16)
    mix_out = functools.partial(_mix_out, norm_ga=g_oa, w_out_bf16=w_out_b, norm_gf=g_ffn,
                                w_router=jnp.concatenate([w_router_hi, w_router_lo], axis=1),
                                b_router=b_router)
    h_s, route_s, xs_s, cnt_s = mix_out(xs, attn_s, ocs)
    assert cnt_s.shape[0] == 1
    h_p, route_p, xs_all, cnt_p = mix_out(xp, attn_p, ocp, tail=xs_s)

    tile_chunks = jnp.concatenate([cnt_p[:, 0, ROUTER_LANE0:n_route],
                                   cnt_s[:, 0, ROUTER_LANE0:n_route]], axis=0).astype(jnp.int32)
    ntp, nts = cnt_p.shape[0], cnt_s.shape[0]
    tm_p, tm_s = tp // ntp, ts // nts
    rl_p, rl_s = _local_rows(tm_p), _local_rows(tm_s)
    tile_row0 = jnp.arange(ntp + nts, dtype=jnp.int32) * rl_p
    total_chunks = ntp * _max_tile_chunks(tm_p) + nts * _max_tile_chunks(tm_s)
    n_blocks = -(-(total_chunks + N_EXPERTS * (CHUNKS_PER_BLOCK - 1)) // CHUNKS_PER_BLOCK)
    block_e, n_used, src_row, tile_src, next_e = _sorted_layout(
        tile_chunks, tile_row0, rl_p // CHUNK, n_blocks)
    ybuf = _experts(block_e, n_used, src_row, next_e, xs_all, w_gate[layer], w_up[layer],
                    w_down[layer])
    g_fin = row(norm_final)
    y_p = _combine(tile_src[:ntp].reshape(-1), h_p, route_p, g_fin, ybuf)
    y_s = _combine(tile_src[ntp:, :rl_s // CHUNK].reshape(-1), h_s, route_s, g_fin, ybuf)

    w_keep = min(max(DILATIONS) * WIN_KEYS, seq_len)
    kv5 = lambda a_t: jnp.transpose(a_t.reshape(n_seq, n_heads, dh, seq_len),
                                    (0, 3, 1, 2))[None, :, seq_len - w_keep:]
    conv_s = jnp.stack([st1, us], axis=1)[None]
    kvs = lambda a: a.reshape(1, ts, 1, n_heads, dh)
    return (y_p.reshape(n_seq, seq_len, d), y_s.reshape(db, ds, d), kv5(kp_t), kv5(vp_t),
            conv_p[None], kvs(ks), kvs(vs), conv_s)
```

```python
import functools

import jax
import jax.numpy as jnp
from jax import lax
from jax.experimental import pallas as pl
from jax.experimental.pallas import tpu as pltpu

HEAD_DIM = 64
WIN_KEYS = 128
DILATIONS = (1, 4, 16)
CONV_WIDTH = 3
N_GROUPS = 4
EXPERTS_PER_GROUP = 8
N_EXPERTS = N_GROUPS * EXPERTS_PER_GROUP
EPS = 1e-6
NEG = -1e30
LOG2_E = 1.4426950408889634

LANES = 128
ROW_TILE = 512
MIX_IN_TILE = 1024
EXPERT_BLOCK = 512
EXPERT_ROW_STEP = 128
ATTN_LAG = 3
WEIGHT_SLABS = 8
ATTN_UNROLL = 8
VMEM_LIMIT = 56 * 1024 * 1024

F32 = jnp.float32
BF16 = jnp.bfloat16


def _rms(x, g):
    return x * lax.rsqrt(jnp.mean(x * x, axis=-1, keepdims=True) + EPS) * g


def _mix_in_kernel(*refs, d_attn, d_conv, sequential):
    if sequential:
        (x_ref, g_ref, w_ref, cw_ref, gc_ref,
         q_ref, k_ref, v_ref, kt_ref, vt_ref, oc_ref, st_ref, carry_ref) = refs
    else:
        (x_ref, g_ref, w_ref, cw_ref, gc_ref, st0_ref, st1_ref,
         q_ref, k_ref, v_ref, oc_ref, u_ref) = refs
    x = x_ref[...]
    xb = _rms(x, g_ref[...]).astype(BF16)

    def proj(lo, width):
        return jnp.dot(xb, w_ref[:, lo:lo + width], preferred_element_type=F32)

    q_ref[...] = proj(0, d_attn)
    k = proj(d_attn, d_attn)
    v = proj(2 * d_attn, d_attn)
    k_ref[...] = k
    v_ref[...] = v
    gate = proj(3 * d_attn, d_conv)
    u = proj(3 * d_attn + d_conv, d_conv) * proj(3 * d_attn + 2 * d_conv, d_conv)

    tm = x.shape[0]
    if sequential:
        kt_ref[...] = k.T
        vt_ref[...] = v.T

        @pl.when(pl.program_id(1) == 0)
        def _():
            carry_ref[...] = jnp.zeros_like(carry_ref)

        row = lax.broadcasted_iota(jnp.int32, u.shape, 0)
        prev1 = carry_ref[1:2, :]
        prev2 = carry_ref[0:1, :]
        u1 = jnp.where(row == 0, prev1, pltpu.roll(u, 1, axis=0))
        u2 = jnp.where(row == 0, prev2, jnp.where(row == 1, prev1, pltpu.roll(u, 2, axis=0)))
        carry_ref[0:2, :] = u[tm - 2:tm, :]
        st_ref[...] = u[tm - 2:tm, :]
    else:
        u_ref[...] = u
        u2 = st0_ref[...]
        u1 = st1_ref[...]
    z = u2 * cw_ref[0:1, :] + u1 * cw_ref[1:2, :] + u * cw_ref[2:3, :]
    oc_ref[...] = _rms(gate * z, gc_ref[...])


def _mix_in_call(kernel, grid, in_specs, out_specs, out_shape, scratch, args):
    return pl.pallas_call(
        kernel, grid=grid, in_specs=in_specs, out_specs=out_specs, out_shape=out_shape,
        scratch_shapes=scratch,
        compiler_params=pltpu.CompilerParams(
            dimension_semantics=("arbitrary",) * len(grid), vmem_limit_bytes=VMEM_LIMIT),
        name="mix_in",
    )(*args)


def _mix_in_prompt(x2d, norm_g, w_in_bf16, conv_w, norm_gc, *, seq_len, d_attn, d_conv):
    t, d = x2d.shape
    tm = min(MIX_IN_TILE, seq_len)
    n_seq, per = t // seq_len, seq_len // tm
    const = lambda b, s: (0, 0)
    row = lambda width: pl.BlockSpec((tm, width), lambda b, s: (b * per + s, 0))
    col = pl.BlockSpec((None, d_attn, tm), lambda b, s: (b, 0, s))
    f32 = lambda *shape: jax.ShapeDtypeStruct(shape, F32)
    return _mix_in_call(
        functools.partial(_mix_in_kernel, d_attn=d_attn, d_conv=d_conv, sequential=True),
        (n_seq, per),
        [row(d), pl.BlockSpec((1, d), const),
         pl.BlockSpec(w_in_bf16.shape, const, pipeline_mode=pl.Buffered(1)),
         pl.BlockSpec((CONV_WIDTH, d_conv), const), pl.BlockSpec((1, d_conv), const)],
        [row(d_attn)] * 3 + [col] * 2 + [row(d_conv),
                                         pl.BlockSpec((None, CONV_WIDTH - 1, d_conv),
                                                      lambda b, s: (b, 0, 0))],
        [f32(t, d_attn)] * 3 + [f32(n_seq, d_attn, seq_len)] * 2
        + [f32(t, d_conv), f32(n_seq, CONV_WIDTH - 1, d_conv)],
        [pltpu.VMEM((8, d_conv), F32)],
        (x2d, norm_g, w_in_bf16, conv_w, norm_gc))


def _mix_in_sample(x2d, norm_g, w_in_bf16, conv_w, norm_gc, st0, st1, *, d_attn, d_conv):
    t, d = x2d.shape
    full = lambda arr: pl.BlockSpec(arr.shape, lambda i: (0,) * arr.ndim)
    f32 = lambda *shape: jax.ShapeDtypeStruct(shape, F32)
    args = (x2d, norm_g, w_in_bf16, conv_w, norm_gc, st0, st1)
    outs = [f32(t, d_attn)] * 3 + [f32(t, d_conv)] * 2
    return _mix_in_call(
        functools.partial(_mix_in_kernel, d_attn=d_attn, d_conv=d_conv, sequential=False),
        (1,), [full(a) for a in args], [full(o) for o in outs], outs, [], args)


def _attn_prompt_kernel(q_ref, k_ref, v_ref, o_ref, m_s, l_s, a_s, *, seq_len):
    w = WIN_KEYS
    scale = HEAD_DIM ** -0.5 * LOG2_E
    r_i = lax.broadcasted_iota(jnp.int32, (2 * w, 2 * w), 0) & (w - 1)
    c_i = lax.broadcasted_iota(jnp.int32, (2 * w, 2 * w), 1)
    mask_cur = (lax.broadcasted_iota(jnp.int32, (2 * w, w), 1)
                <= lax.broadcasted_iota(jnp.int32, (2 * w, w), 0) & (w - 1))
    mask_both = jnp.logical_and(c_i >= r_i, c_i - w <= r_i)
    first_head = lax.broadcasted_iota(jnp.int32, (w, 2 * HEAD_DIM), 1) < HEAD_DIM
    dn_t = (((1,), (1,)), ((), ()))

    def rows(start, dil):
        if dil > 1:
            return pl.ds(start, w, stride=dil)
        return pl.ds(start if isinstance(start, int) else pl.multiple_of(start, w), w)

    def run_branch(dil, first, last):
        span = dil * w
        nb = seq_len // span

        def blocks(its, with_prev):
            mask = mask_both if with_prev else mask_cur

            def issue_scores(it):
                g = it % dil
                n = it // dil
                c = rows(g + n * span, dil)
                qb = (q_ref[c, :] * scale).astype(BF16)
                zero = jnp.zeros_like(qb)
                q = jnp.concatenate([jnp.where(first_head, qb, zero),
                                     jnp.where(first_head, zero, qb)], axis=0)
                k = k_ref[c, :].astype(BF16)
                v = v_ref[c, :].astype(BF16)
                if with_prev:
                    p = rows(g + (n - 1) * span, dil)
                    k = jnp.concatenate([k_ref[p, :].astype(BF16), k], axis=0)
                    v = jnp.concatenate([v_ref[p, :].astype(BF16), v], axis=0)
                return c, lax.dot_general(q, k, dn_t, preferred_element_type=F32), v

            def finish(c, s, v):
                s = jnp.where(mask, s, NEG)
                m = jnp.max(s, axis=-1, keepdims=True)
                p = jnp.exp2(s - m).astype(BF16)
                ones = jnp.ones((v.shape[0], 2 * HEAD_DIM), BF16)
                acc_l = jnp.dot(p, jnp.concatenate([v, ones], axis=1), preferred_element_type=F32)
                acc, l = acc_l[:, :2 * HEAD_DIM], acc_l[:, 2 * HEAD_DIM:]
                m_b = jnp.where(first_head, m[:w], m[w:])
                l_b = jnp.where(first_head, l[:w], l[w:])
                a_b = jnp.where(first_head, acc[:w], acc[w:])
                if not first:
                    m_o = m_s[c, :]
                    m_n = jnp.maximum(m_o, m_b)
                    w_o = jnp.exp2(m_o - m_n)
                    w_b = jnp.exp2(m_b - m_n)
                    l_b = w_o * l_s[c, :] + w_b * l_b
                    a_b = w_o * a_s[c, :] + w_b * a_b
                    m_b = m_n
                if last:
                    o_ref[c, :] = a_b / l_b
                else:
                    m_s[c, :] = m_b
                    l_s[c, :] = l_b
                    a_s[c, :] = a_b

            in_flight = []
            for i in range(len(its) + ATTN_LAG):
                if i < len(its):
                    in_flight.append(issue_scores(its[i]))
                if i >= ATTN_LAG:
                    finish(*in_flight.pop(0))

        def run(lo, hi, with_prev):
            u = ATTN_UNROLL
            trips = (hi - lo) // u

            def body(t, carry):
                blocks([lo + t * u + j for j in range(u)], with_prev)
                return carry

            if trips:
                lax.fori_loop(0, trips, body, 0)
            if lo + trips * u < hi:
                blocks(list(range(lo + trips * u, hi)), with_prev)

        run(0, dil, False)
        run(dil, dil * nb, True)

    order = sorted(DILATIONS, reverse=True)
    for i, dil in enumerate(order):
        run_branch(dil, i == 0, i == len(order) - 1)


def _attn_sample_kernel(q_ref, kn_ref, vn_ref, kt_ref, vt_ref, o_ref):
    n_heads, dh, w_buf = kt_ref.shape
    delta = w_buf - lax.broadcasted_iota(jnp.int32, (1, w_buf), 1)
    cnt = jnp.zeros((1, w_buf), F32)
    for dil in DILATIONS:
        assert dil & (dil - 1) == 0
        member = jnp.where(delta <= dil * WIN_KEYS, 1.0, 0.0)
        cnt = cnt + jnp.where((delta & (dil - 1)) == 0, member, 0.0)
    eye = (lax.broadcasted_iota(jnp.int32, (dh, dh), 0)
           == lax.broadcasted_iota(jnp.int32, (dh, dh), 1))
    to_col = lambda r: jnp.sum(jnp.where(eye, r, 0.0), axis=1, keepdims=True)
    to_row = lambda c: jnp.sum(jnp.where(eye, c, 0.0), axis=0, keepdims=True)
    outs = []
    for h in range(n_heads):
        sl = slice(h * dh, (h + 1) * dh)
        q = q_ref[:, sl] * (HEAD_DIM ** -0.5)
        s_self = jnp.sum(q * kn_ref[:, sl], axis=1, keepdims=True)
        s = jnp.sum(to_col(q) * kt_ref[h], axis=0, keepdims=True)
        s = jnp.where(cnt > 0.0, s, NEG)
        m = jnp.maximum(jnp.max(s, axis=1, keepdims=True), s_self)
        p = cnt * jnp.exp(s - m)
        p_self = len(DILATIONS) * jnp.exp(s_self - m)
        l = jnp.sum(p, axis=1, keepdims=True) + p_self
        acc = jnp.sum(p * vt_ref[h], axis=1, keepdims=True)
        outs.append((to_row(acc) + p_self * vn_ref[:, sl]) / l)
    o_ref[...] = jnp.concatenate(outs, axis=1)


def _attn_kernel(q_ref, k_ref, v_ref, qs_ref, kn_ref, vn_ref, kt_ref, vt_ref, o_ref, os_ref,
                 m_s, l_s, a_s, *, seq_len):
    _attn_sample_kernel(qs_ref, kn_ref, vn_ref, kt_ref, vt_ref, os_ref)
    _attn_prompt_kernel(q_ref, k_ref, v_ref, o_ref, m_s, l_s, a_s, seq_len=seq_len)


def _attention(q, k, v, qs, k_new, v_new, cache_kt, cache_vt, *, n_seq, seq_len):
    t, d_attn = q.shape
    db, n_heads, dh, w_buf = cache_kt.shape
    pair = 2 * HEAD_DIM
    pairs = d_attn // pair
    assert db == n_seq * pairs, "one sample sequence per prompt grid step"
    spec = pl.BlockSpec((seq_len, pair), lambda b, h: (b, h))
    head_spec = pl.BlockSpec((None, 1, d_attn), lambda b, h: (b * pairs + h, 0, 0))
    cache_spec = pl.BlockSpec((None, n_heads, dh, w_buf), lambda b, h: (b * pairs + h, 0, 0, 0))
    return pl.pallas_call(
        functools.partial(_attn_kernel, seq_len=seq_len),
        grid=(n_seq, pairs),
        in_specs=[spec] * 3 + [head_spec] * 3 + [cache_spec] * 2,
        out_specs=[spec, head_spec],
        out_shape=[jax.ShapeDtypeStruct((t, d_attn), F32),
                   jax.ShapeDtypeStruct((db, 1, d_attn), F32)],
        scratch_shapes=[pltpu.VMEM((seq_len, pair), F32)] * 3,
        compiler_params=pltpu.CompilerParams(
            dimension_semantics=("arbitrary", "arbitrary"), vmem_limit_bytes=VMEM_LIMIT),
        name="attention",
    )(q, k, v, qs, k_new, v_new, cache_kt, cache_vt)


R_E0, R_E1, R_G0, R_G1, R_POS0, R_POS1 = range(6)
ROUTER_LANE0 = N_GROUPS
CHUNK = 16
CHUNKS_PER_BLOCK = EXPERT_BLOCK // CHUNK


def _max_tile_chunks(tm):
    return (2 * tm + (CHUNK - 1) * N_EXPERTS) // CHUNK


def _local_rows(tm):
    return 2 * tm + N_EXPERTS * CHUNK


def _mix_out_kernel(*refs, n_tiles, has_tail):
    if has_tail:
        *tile_in, tail_ref, h_ref, route_ref, xs_ref, cnt_ref, tok_s, pos_s = refs
    else:
        *tile_in, h_ref, route_ref, xs_ref, cnt_ref, tok_s, pos_s = refs
    i = pl.program_id(0)
    route = lambda slot, *between: _route_tile(*tile_in, h_ref, route_ref, cnt_ref,
                                               tok_s.at[slot], pos_s.at[slot], *between)
    sort = lambda slot: _sort_tile(tok_s.at[slot], pos_s.at[slot], xs_ref)

    @pl.when(i == 0)
    def _():
        route(0)

    for parity in range(2):
        @pl.when(jnp.logical_and(jnp.logical_and(i >= 1, i < n_tiles), i % 2 == parity))
        def _():
            route(parity, lambda: sort(1 - parity))

    @pl.when(i == n_tiles)
    def _():
        sort((n_tiles - 1) % 2)

    if has_tail:
        @pl.when(i == n_tiles + 1)
        def _():
            rows = tail_ref.shape[0]
            xs_ref[0:rows, :] = tail_ref[...]
            xs_ref[rows:, :] = jnp.zeros((xs_ref.shape[0] - rows, xs_ref.shape[1]), xs_ref.dtype)


def _sort_tile(tok_ref, pos_ref, xs_ref):
    tm = tok_ref.shape[0]
    l1 = pos_ref[R_POS0:R_POS0 + 1, :].astype(jnp.int32)
    l2 = pos_ref[R_POS1:R_POS1 + 1, :].astype(jnp.int32)
    srow = lax.broadcasted_iota(jnp.int32, (xs_ref.shape[0], tm), 0)
    perm = jnp.where(srow == l1, 1.0, jnp.where(srow == l2, 1.0, 0.0)).astype(BF16)
    xs_ref[...] = jnp.dot(perm, tok_ref[...], preferred_element_type=F32).astype(BF16)


def _route_tile(x_ref, a_ref, oc_ref, ga_ref, wo_ref, gf_ref, wr_ref, br_ref,
                h_ref, route_ref, cnt_ref, tok_ref, pos_ref, after_projections=lambda: None):
    d_attn = a_ref.shape[1]
    tm, d = x_ref.shape
    a = _rms(a_ref[...], ga_ref[...]).astype(BF16)
    mix = jnp.dot(a, wo_ref[0:d_attn, :], preferred_element_type=F32)
    mix = mix + jnp.dot(oc_ref[...].astype(BF16), wo_ref[d_attn:, :], preferred_element_type=F32)
    h = x_ref[...] + mix
    h_ref[...] = h
    tok = _rms(h, gf_ref[...])

    tok_hi = tok.astype(BF16)
    tok_lo = (tok - tok_hi.astype(F32)).astype(BF16)
    hi_part = jnp.dot(tok_hi, wr_ref[...], preferred_element_type=F32)
    lo_part = jnp.dot(tok_lo, wr_ref[:, :LANES], preferred_element_type=F32)
    logits = hi_part[:, :LANES] + hi_part[:, LANES:] + lo_part + br_ref[...]
    after_projections()
    lane = lax.broadcasted_iota(jnp.int32, logits.shape, 1)
    big = jnp.int32(LANES)
    neg_inf = jnp.float32(-jnp.inf)

    def top1(vals):
        best = jnp.max(vals, axis=-1, keepdims=True)
        idx = jnp.min(jnp.where(vals == best, lane, big), axis=-1, keepdims=True)
        return best, idx

    is_group = lane < N_GROUPS
    lg = jnp.where(is_group, logits, neg_inf)
    mg, g_sel = top1(lg)
    p_group = 1.0 / jnp.sum(jnp.where(is_group, jnp.exp(lg - mg), 0.0), axis=-1, keepdims=True)

    lo = ROUTER_LANE0 + g_sel * EXPERTS_PER_GROUP
    in_group = jnp.logical_and(lane >= lo, lane < lo + EXPERTS_PER_GROUP)
    le = jnp.where(in_group, logits, neg_inf)
    v1, i1 = top1(le)
    v2, i2 = top1(jnp.where(lane == i1, neg_inf, le))
    e2 = jnp.exp(v2 - v1)
    gate1 = p_group / (1.0 + e2)
    gate2 = p_group * e2 / (1.0 + e2)

    oh1 = lane == i1
    oh2 = lane == i2
    both = jnp.where(jnp.logical_or(oh1, oh2), 1.0, 0.0)
    r_i = lax.broadcasted_iota(jnp.int32, (tm, tm), 0)
    c_i = lax.broadcasted_iota(jnp.int32, (tm, tm), 1)
    strict_lower = jnp.where(c_i < r_i, 1.0, 0.0).astype(BF16)
    before = jnp.dot(strict_lower, both.astype(BF16), preferred_element_type=F32)
    chunks = jnp.floor((jnp.sum(both, axis=0, keepdims=True) + (CHUNK - 1)) * (1.0 / CHUNK))
    u_r = lax.broadcasted_iota(jnp.int32, (LANES, LANES), 0)
    u_c = lax.broadcasted_iota(jnp.int32, (LANES, LANES), 1)
    strict_upper = jnp.where(u_r < u_c, 1.0, 0.0).astype(BF16)
    chunks8 = jnp.broadcast_to(chunks, (8, LANES))
    first_row = CHUNK * jnp.dot(chunks8.astype(BF16), strict_upper,
                                preferred_element_type=F32)[0:1, :]
    pos = first_row + before
    pos1 = jnp.sum(jnp.where(oh1, pos, 0.0), axis=-1, keepdims=True)
    pos2 = jnp.sum(jnp.where(oh2, pos, 0.0), axis=-1, keepdims=True)
    cnt_ref[...] = jnp.where(lax.broadcasted_iota(jnp.int32, (8, LANES), 0) == 0, chunks8, 0.0)

    rec = jnp.zeros(logits.shape, F32)
    for col, val in ((R_E0, (i1 - ROUTER_LANE0).astype(F32)), (R_E1, (i2 - ROUTER_LANE0).astype(F32)),
                     (R_G0, gate1), (R_G1, gate2), (R_POS0, pos1), (R_POS1, pos2)):
        rec = jnp.where(lane == col, val, rec)
    route_ref[...] = rec
    tok_ref[...] = tok_hi
    pos_ref[...] = rec.T[0:pos_ref.shape[0], :]


def _mix_out(x2d, attn, oconv, norm_ga, w_out_bf16, norm_gf, w_router, b_router, tail=None):
    t, d = x2d.shape
    d_attn, d_conv = attn.shape[1], oconv.shape[1]
    tm = min(ROW_TILE, t)
    nt = t // tm
    r_l = _local_rows(tm)
    has_tail = tail is not None
    tile = lambda i: jnp.minimum(i, nt - 1)
    sorted_block = lambda i: jnp.minimum(jnp.maximum(i - 1, 0), nt - 1 + has_tail)
    row = lambda width: pl.BlockSpec((tm, width), lambda i: (tile(i), 0))
    full = lambda arr: pl.BlockSpec(arr.shape, lambda i: (0, 0))
    args = [x2d, attn, oconv, norm_ga, w_out_bf16, norm_gf, w_router, b_router]
    in_specs = [row(d), row(d_attn), row(d_conv)] + [full(a) for a in args[3:]]
    if has_tail:
        assert tail.shape[0] <= r_l and tail.shape[1] == d
        args.append(tail)
        in_specs.append(full(tail))
    return pl.pallas_call(
        functools.partial(_mix_out_kernel, n_tiles=nt, has_tail=has_tail),
        grid=(nt + 1 + has_tail,),
        in_specs=in_specs,
        out_specs=[row(d), row(LANES), pl.BlockSpec((r_l, d), lambda i: (sorted_block(i), 0)),
                   pl.BlockSpec((None, 8, LANES), lambda i: (tile(i), 0, 0))],
        out_shape=[jax.ShapeDtypeStruct((t, d), F32), jax.ShapeDtypeStruct((t, LANES), F32),
                   jax.ShapeDtypeStruct(((nt + has_tail) * r_l, d), BF16),
                   jax.ShapeDtypeStruct((nt, 8, LANES), F32)],
        scratch_shapes=[pltpu.VMEM((2, tm, d), BF16), pltpu.VMEM((2, 8, tm), F32)],
        compiler_params=pltpu.CompilerParams(
            dimension_semantics=("arbitrary",), vmem_limit_bytes=VMEM_LIMIT),
        name="mix_out",
    )(*args)


def _sorted_layout(tile_chunks, tile_row0, max_local, n_blocks):
    nt, n_exp = tile_chunks.shape
    cpb = CHUNKS_PER_BLOCK
    i32 = jnp.int32
    seg = jnp.sum(tile_chunks, axis=0)
    padded = (seg + cpb - 1) // cpb * cpb
    pend = jnp.cumsum(padded)
    pstart = pend - padded
    tile_incl = jnp.cumsum(tile_chunks, axis=0)
    tile_excl = tile_incl - tile_chunks
    local_incl = jnp.cumsum(tile_chunks, axis=1)
    local_excl = local_incl - tile_chunks
    base = pstart[None, :] + tile_excl

    block_first = jnp.arange(n_blocks, dtype=i32) * cpb
    block_e = jnp.minimum(jnp.sum((pend[None, :] <= block_first[:, None]).astype(i32), axis=1),
                          n_exp - 1)
    n_used = (pend[-1:] // cpb).astype(i32)

    onehot_pick = lambda onehot, table: jnp.sum(jnp.where(onehot, table, 0), axis=-1)

    is_e = block_e[:, None] == jnp.arange(n_exp, dtype=i32)[None, :]
    of_expert = lambda table_te: onehot_pick(is_e[:, None, :], table_te[None, :, :])
    incl_b, cnt_b, lexcl_b = of_expert(tile_incl), of_expert(tile_chunks), of_expert(local_excl)
    q = (block_first - onehot_pick(is_e, pstart[None, :]))[:, None] + jnp.arange(cpb, dtype=i32)
    tile_q = jnp.minimum(jnp.sum((incl_b[:, None, :] <= q[:, :, None]).astype(i32), axis=2), nt - 1)
    is_t = tile_q[:, :, None] == jnp.arange(nt, dtype=i32)[None, None, :]
    of_tile = lambda table_bt: onehot_pick(is_t, table_bt[:, None, :])
    local_chunk = of_tile(lexcl_b) + q - of_tile(incl_b - cnt_b)
    seg_b = onehot_pick(is_e, seg[None, :])
    block_rows = (CHUNK * jnp.clip(seg_b - q[:, 0], 0, cpb)).astype(i32)
    in_run = jnp.logical_and(q >= 0, q < seg_b[:, None])
    src_row = jnp.where(in_run, of_tile(tile_row0[None, :]) + CHUNK * local_chunk, 0)
    src_row = src_row.reshape(-1).astype(i32)

    c = jnp.arange(max_local, dtype=i32)
    e_c = jnp.minimum(jnp.sum((local_incl[:, None, :] <= c[None, :, None]).astype(i32), axis=2),
                      n_exp - 1)
    is_ec = e_c[:, :, None] == jnp.arange(n_exp, dtype=i32)[None, None, :]
    of_run = lambda table_te: onehot_pick(is_ec, table_te[:, None, :])
    global_chunk = of_run(base) + c[None, :] - of_run(local_excl)
    tile_src = jnp.where(c[None, :] < local_incl[:, -1:], CHUNK * global_chunk, 0).astype(i32)
    e_ids = jnp.arange(n_exp, dtype=i32)
    later = jnp.logical_and(seg[None, :] > 0, e_ids[None, :] > e_ids[:, None])
    next_e = jnp.min(jnp.where(later, e_ids[None, :], n_exp), axis=1)
    next_e = jnp.where(next_e < n_exp, next_e, -1).astype(i32)
    return block_e.astype(i32), n_used, src_row, tile_src, next_e, block_rows


def _chunk_gather(src_ref, hbm_ref, buf, sems, item, slot, n_chunks, *, wait):
    for c in range(n_chunks):
        row = 0 if wait else pl.multiple_of(src_ref[item * n_chunks + c], CHUNK)
        copy = pltpu.make_async_copy(hbm_ref.at[pl.ds(row, CHUNK)],
                                     buf.at[slot, pl.ds(c * CHUNK, CHUNK)], sems.at[slot])
        if wait:
            copy.wait()
        else:
            copy.start()


def _prefetched(gather, step, n_items, body):
    slot = step % 2

    @pl.when(jnp.logical_and(step == 0, n_items > 0))
    def _():
        gather(0, 0, wait=False)

    @pl.when(step + 1 < n_items)
    def _():
        gather(step + 1, 1 - slot, wait=False)

    body(slot, lambda: gather(step, slot, wait=True))


def _experts_kernel(block_e_ref, n_used_ref, src_ref, next_e_ref, block_rows_ref, xs_ref, wg_hbm,
                    wu_hbm, wd_hbm, y_hbm, xblk, sems, ybuf, ysems, wg_f, wu_f, wd_f, wsems, wg_b, wu_b, wd_b,
                    run_ref):
    rows = EXPERT_BLOCK
    n_blocks = y_hbm.shape[0] // rows
    n_used = n_used_ref[0]
    gather = functools.partial(_chunk_gather, src_ref, xs_ref, xblk, sems,
                               n_chunks=CHUNKS_PER_BLOCK)

    def weight_copies(expert, slot):
        copies = []
        for hbm, stage in ((wg_hbm, wg_f), (wu_hbm, wu_f), (wd_hbm, wd_f)):
            slab = hbm.shape[1] // WEIGHT_SLABS
            for i in range(WEIGHT_SLABS):
                rows_i = pl.ds(i * slab, slab)
                copies.append(pltpu.make_async_copy(hbm.at[expert, rows_i], stage.at[slot, rows_i],
                                                    wsems.at[slot]))
        return copies

    def y_copy(blk, slot):
        start = blk * rows if isinstance(blk, int) else pl.multiple_of(blk * rows, rows)
        return pltpu.make_async_copy(ybuf.at[slot], y_hbm.at[pl.ds(start, rows)], ysems.at[slot])

    def block(b, carry):
        e = block_e_ref[b]
        new_expert = jnp.logical_or(b == 0, e != block_e_ref[jnp.maximum(b - 1, 0)])

        @pl.when(jnp.logical_and(new_expert, b < n_used))
        def _():
            @pl.when(b == 0)
            def _():
                run_ref[0] = 0
                for copy in weight_copies(e, 0):
                    copy.start()

            @pl.when(b > 0)
            def _():
                run_ref[0] = run_ref[0] + 1

            slot = run_ref[0] % 2
            nxt = next_e_ref[e]

            @pl.when(nxt >= 0)
            def _():
                for copy in weight_copies(nxt, 1 - slot):
                    copy.start()

            for copy in weight_copies(e, slot):
                copy.wait()
            wg_b[...] = wg_f[slot].astype(BF16)
            wu_b[...] = wu_f[slot].astype(BF16)
            wd_b[...] = wd_f[slot].astype(BF16)

        def body(slot, wait_current):
            @pl.when(b >= 2)
            def _():
                y_copy(b - 2, slot).wait()

            @pl.when(b < n_used)
            def _():
                wait_current()

            valid = block_rows_ref[b]
            for m in range(EXPERT_ROW_STEP, rows + 1, EXPERT_ROW_STEP):
                @pl.when(jnp.logical_and(valid > m - EXPERT_ROW_STEP, valid <= m))
                def _():
                    x = xblk[slot, 0:m, :]
                    gate = jnp.dot(x, wg_b[...], preferred_element_type=F32)
                    up = jnp.dot(x, wu_b[...], preferred_element_type=F32)
                    hid = gate * (1.0 / (1.0 + jnp.exp(-gate))) * up
                    ybuf[slot, 0:m, :] = jnp.dot(hid.astype(BF16), wd_b[...],
                                                 preferred_element_type=F32).astype(BF16)
                    if m < rows:
                        ybuf[slot, m:rows, :] = jnp.zeros((rows - m, ybuf.shape[2]), ybuf.dtype)

            @pl.when(valid == 0)
            def _():
                ybuf[slot] = jnp.zeros(ybuf.shape[1:], ybuf.dtype)

            y_copy(b, slot).start()

        _prefetched(gather, b, n_used, body)
        return carry

    lax.fori_loop(0, n_blocks, block, 0)
    for blk in range(max(n_blocks - 2, 0), n_blocks):
        y_copy(blk, blk % 2).wait()


def _experts(block_e, n_used, src_row, next_e, block_rows, xs, w_gate, w_up, w_down):
    n_blocks = block_e.shape[0]
    _, d, d_exp = w_gate.shape
    blk = EXPERT_BLOCK
    any_spec = pl.BlockSpec(memory_space=pl.ANY)
    return pl.pallas_call(
        _experts_kernel,
        grid_spec=pltpu.PrefetchScalarGridSpec(
            num_scalar_prefetch=5,
            grid=(1,),
            in_specs=[any_spec] * 4,
            out_specs=any_spec,
            scratch_shapes=[pltpu.VMEM((2, blk, d), BF16), pltpu.SemaphoreType.DMA((2,)),
                            pltpu.VMEM((2, blk, d), BF16), pltpu.SemaphoreType.DMA((2,)),
                            pltpu.VMEM((2, d, d_exp), F32), pltpu.VMEM((2, d, d_exp), F32),
                            pltpu.VMEM((2, d_exp, d), F32), pltpu.SemaphoreType.DMA((2,)),
                            pltpu.VMEM((d, d_exp), BF16), pltpu.VMEM((d, d_exp), BF16),
                            pltpu.VMEM((d_exp, d), BF16), pltpu.SMEM((1,), jnp.int32)],
        ),
        out_shape=jax.ShapeDtypeStruct((n_blocks * blk, d), BF16),
        compiler_params=pltpu.CompilerParams(
            dimension_semantics=("arbitrary",), vmem_limit_bytes=VMEM_LIMIT),
        name="experts",
    )(block_e, n_used, src_row, next_e, block_rows, xs, w_gate, w_up, w_down)


def _combine_kernel(src_ref, h_ref, route_ref, gn_ref, ybuf_ref, o_ref, yloc, sems):
    tm = h_ref.shape[0]
    r_l = yloc.shape[1]
    gather = functools.partial(_chunk_gather, src_ref, ybuf_ref, yloc, sems,
                               n_chunks=r_l // CHUNK)

    def body(slot, wait_current):
        wait_current()
        y = yloc[slot]
        route = route_ref[...]
        l0 = route[:, R_POS0:R_POS0 + 1].astype(jnp.int32)
        l1 = route[:, R_POS1:R_POS1 + 1].astype(jnp.int32)
        srow = lax.broadcasted_iota(jnp.int32, (tm, r_l), 1)
        gates = jnp.where(srow == l0, route[:, R_G0:R_G0 + 1],
                          jnp.where(srow == l1, route[:, R_G1:R_G1 + 1], 0.0)).astype(BF16)
        f = jnp.dot(gates, y, preferred_element_type=F32)
        o_ref[...] = _rms(h_ref[...] + f, gn_ref[...])

    _prefetched(gather, pl.program_id(0), pl.num_programs(0), body)


def _combine(tile_src, h, route, norm_g, ybuf):
    t, d = h.shape
    tm = min(ROW_TILE, t)
    r_l = _local_rows(tm)
    return pl.pallas_call(
        _combine_kernel,
        grid_spec=pltpu.PrefetchScalarGridSpec(
            num_scalar_prefetch=1,
            grid=(t // tm,),
            in_specs=[pl.BlockSpec((tm, d), lambda i, src: (i, 0)),
                      pl.BlockSpec((tm, LANES), lambda i, src: (i, 0)),
                      pl.BlockSpec((1, d), lambda i, src: (0, 0)),
                      pl.BlockSpec(memory_space=pl.ANY)],
            out_specs=pl.BlockSpec((tm, d), lambda i, src: (i, 0)),
            scratch_shapes=[pltpu.VMEM((2, r_l, d), BF16),
                            pltpu.SemaphoreType.DMA((2,))],
        ),
        out_shape=jax.ShapeDtypeStruct((t, d), F32),
        compiler_params=pltpu.CompilerParams(
            dimension_semantics=("arbitrary",), vmem_limit_bytes=VMEM_LIMIT),
        name="combine",
    )(tile_src, h, route, norm_g, ybuf)


def kernel(x_prompt, x_sample, cache_k, cache_v, state_conv, norm_mix, w_in, conv_w, norm_out_attn,
           norm_out_conv, w_out, norm_ffn, w_router_group, b_router_group, w_router_expert,
           b_router_expert, w_gate, w_up, w_down, norm_final):
    n_seq, seq_len, d = x_prompt.shape
    db, ds, _ = x_sample.shape
    depth = w_in.shape[0]
    _, _, w_buf, n_heads, dh = cache_k.shape
    d_attn = n_heads * dh
    d_conv = d - d_attn
    assert depth == 1 and ds == 1 and dh == HEAD_DIM
    assert seq_len % (max(DILATIONS) * WIN_KEYS) == 0 and seq_len <= max(DILATIONS) * WIN_KEYS
    layer = 0
    tp, ts = n_seq * seq_len, db

    xp = x_prompt.reshape(tp, d)
    xs = x_sample.reshape(ts, d)
    row = lambda vec: vec.reshape(1, -1)
    w_in_b = w_in[layer].astype(BF16)
    w_out_b = w_out[layer].astype(BF16)
    g_mix, g_oa, g_oc, g_ffn = (row(norm_mix[layer]), row(norm_out_attn[layer]),
                                row(norm_out_conv[layer]), row(norm_ffn[layer]))
    st0, st1 = state_conv[layer, :, 0, :], state_conv[layer, :, 1, :]

    qp, kp, vp, kp_t, vp_t, ocp, conv_p = _mix_in_prompt(
        xp, g_mix, w_in_b, conv_w[layer], g_oc, seq_len=seq_len, d_attn=d_attn, d_conv=d_conv)
    qs, ks, vs, ocs, us = _mix_in_sample(
        xs, g_mix, w_in_b, conv_w[layer], g_oc, st0, st1, d_attn=d_attn, d_conv=d_conv)

    heads = lambda a: a.reshape(ts, 1, d_attn)
    positions_last = lambda c: jnp.transpose(c, (0, 2, 3, 1))
    attn_p, attn_s = _attention(qp, kp, vp, heads(qs), heads(ks), heads(vs),
                                positions_last(cache_k[layer]), positions_last(cache_v[layer]),
                                n_seq=n_seq, seq_len=seq_len)
    attn_s = attn_s.reshape(ts, d_attn)

    n_route = N_GROUPS + N_EXPERTS
    w_router = jnp.zeros((d, LANES), F32).at[:, :N_GROUPS].set(w_router_group[layer])
    w_router = w_router.at[:, N_GROUPS:n_route].set(w_router_expert[layer])
    b_router = jnp.zeros((1, LANES), F32).at[0, :N_GROUPS].set(b_router_group[layer])
    b_router = b_router.at[0, N_GROUPS:n_route].set(b_router_expert[layer])
    w_router_hi = w_router.astype(BF16)
    w_router_lo = (w_router - w_router_hi.astype(F32)).astype(BF16)
    mix_out = functools.partial(_mix_out, norm_ga=g_oa, w_out_bf16=w_out_b, norm_gf=g_ffn,
                                w_router=jnp.concatenate([w_router_hi, w_router_lo], axis=1),
                                b_router=b_router)
    h_s, route_s, xs_s, cnt_s = mix_out(xs, attn_s, ocs)
    assert cnt_s.shape[0] == 1
    h_p, route_p, xs_all, cnt_p = mix_out(xp, attn_p, ocp, tail=xs_s)

    tile_chunks = jnp.concatenate([cnt_p[:, 0, ROUTER_LANE0:n_route],
                                   cnt_s[:, 0, ROUTER_LANE0:n_route]], axis=0).astype(jnp.int32)
    ntp, nts = cnt_p.shape[0], cnt_s.shape[0]
    tm_p, tm_s = tp // ntp, ts // nts
    rl_p, rl_s = _local_rows(tm_p), _local_rows(tm_s)
    tile_row0 = jnp.arange(ntp + nts, dtype=jnp.int32) * rl_p
    total_chunks = ntp * _max_tile_chunks(tm_p) + nts * _max_tile_chunks(tm_s)
    n_blocks = -(-(total_chunks + N_EXPERTS * (CHUNKS_PER_BLOCK - 1)) // CHUNKS_PER_BLOCK)
    block_e, n_used, src_row, tile_src, next_e, block_rows = _sorted_layout(
        tile_chunks, tile_row0, rl_p // CHUNK, n_blocks)
    ybuf = _experts(block_e, n_used, src_row, next_e, block_rows, xs_all, w_gate[layer],
                    w_up[layer], w_down[layer])
    g_fin = row(norm_final)
    y_p = _combine(tile_src[:ntp].reshape(-1), h_p, route_p, g_fin, ybuf)
    y_s = _combine(tile_src[ntp:, :rl_s // CHUNK].reshape(-1), h_s, route_s, g_fin, ybuf)

    w_keep = min(max(DILATIONS) * WIN_KEYS, seq_len)
    kv5 = lambda a_t: jnp.transpose(a_t.reshape(n_seq, n_heads, dh, seq_len),
                                    (0, 3, 1, 2))[None, :, seq_len - w_keep:]
    conv_s = jnp.stack([st1, us], axis=1)[None]
    kvs = lambda a: a.reshape(1, ts, 1, n_heads, dh)
    return (y_p.reshape(n_seq, seq_len, d), y_s.reshape(db, ds, d), kv5(kp_t), kv5(vp_t),
            conv_p[None], kvs(ks), kvs(vs), conv_s)
```

```python
import functools

import jax
import jax.numpy as jnp
from jax import lax
from jax.experimental import pallas as pl
from jax.experimental.pallas import tpu as pltpu

HEAD_DIM = 64
WIN_KEYS = 128
DILATIONS = (1, 4, 16)
CONV_WIDTH = 3
N_GROUPS = 4
EXPERTS_PER_GROUP = 8
N_EXPERTS = N_GROUPS * EXPERTS_PER_GROUP
EPS = 1e-6
NEG = -1e30
LOG2_E = 1.4426950408889634

LANES = 128
ROW_TILE = 512
MIX_IN_TILE = 1024
EXPERT_BLOCK = 1024
EXPERT_ROW_STEP = 128
ATTN_LAG = 3
WEIGHT_SLABS = 8
ATTN_UNROLL = 8
VMEM_LIMIT = 56 * 1024 * 1024

F32 = jnp.float32
BF16 = jnp.bfloat16


def _rms(x, g):
    return x * lax.rsqrt(jnp.mean(x * x, axis=-1, keepdims=True) + EPS) * g


def _mix_in_kernel(*refs, d_attn, d_conv, sequential):
    if sequential:
        (x_ref, g_ref, w_ref, cw_ref, gc_ref,
         q_ref, k_ref, v_ref, kt_ref, vt_ref, oc_ref, st_ref, carry_ref) = refs
    else:
        (x_ref, g_ref, w_ref, cw_ref, gc_ref, st0_ref, st1_ref,
         q_ref, k_ref, v_ref, oc_ref, u_ref) = refs
    x = x_ref[...]
    xb = _rms(x, g_ref[...]).astype(BF16)

    def proj(lo, width):
        return jnp.dot(xb, w_ref[:, lo:lo + width], preferred_element_type=F32)

    q_ref[...] = proj(0, d_attn)
    k = proj(d_attn, d_attn)
    v = proj(2 * d_attn, d_attn)
    k_ref[...] = k
    v_ref[...] = v
    gate = proj(3 * d_attn, d_conv)
    u = proj(3 * d_attn + d_conv, d_conv) * proj(3 * d_attn + 2 * d_conv, d_conv)

    tm = x.shape[0]
    if sequential:
        kt_ref[...] = k.T
        vt_ref[...] = v.T

        @pl.when(pl.program_id(1) == 0)
        def _():
            carry_ref[...] = jnp.zeros_like(carry_ref)

        row = lax.broadcasted_iota(jnp.int32, u.shape, 0)
        prev1 = carry_ref[1:2, :]
        prev2 = carry_ref[0:1, :]
        u1 = jnp.where(row == 0, prev1, pltpu.roll(u, 1, axis=0))
        u2 = jnp.where(row == 0, prev2, jnp.where(row == 1, prev1, pltpu.roll(u, 2, axis=0)))
        carry_ref[0:2, :] = u[tm - 2:tm, :]
        st_ref[...] = u[tm - 2:tm, :]
    else:
        u_ref[...] = u
        u2 = st0_ref[...]
        u1 = st1_ref[...]
    z = u2 * cw_ref[0:1, :] + u1 * cw_ref[1:2, :] + u * cw_ref[2:3, :]
    oc_ref[...] = _rms(gate * z, gc_ref[...])


def _mix_in_call(kernel, grid, in_specs, out_specs, out_shape, scratch, args):
    return pl.pallas_call(
        kernel, grid=grid, in_specs=in_specs, out_specs=out_specs, out_shape=out_shape,
        scratch_shapes=scratch,
        compiler_params=pltpu.CompilerParams(
            dimension_semantics=("arbitrary",) * len(grid), vmem_limit_bytes=VMEM_LIMIT),
        name="mix_in",
    )(*args)


def _mix_in_prompt(x2d, norm_g, w_in_bf16, conv_w, norm_gc, *, seq_len, d_attn, d_conv):
    t, d = x2d.shape
    tm = min(MIX_IN_TILE, seq_len)
    n_seq, per = t // seq_len, seq_len // tm
    const = lambda b, s: (0, 0)
    row = lambda width: pl.BlockSpec((tm, width), lambda b, s: (b * per + s, 0))
    col = pl.BlockSpec((None, d_attn, tm), lambda b, s: (b, 0, s))
    f32 = lambda *shape: jax.ShapeDtypeStruct(shape, F32)
    return _mix_in_call(
        functools.partial(_mix_in_kernel, d_attn=d_attn, d_conv=d_conv, sequential=True),
        (n_seq, per),
        [row(d), pl.BlockSpec((1, d), const),
         pl.BlockSpec(w_in_bf16.shape, const, pipeline_mode=pl.Buffered(1)),
         pl.BlockSpec((CONV_WIDTH, d_conv), const), pl.BlockSpec((1, d_conv), const)],
        [row(d_attn)] * 3 + [col] * 2 + [row(d_conv),
                                         pl.BlockSpec((None, CONV_WIDTH - 1, d_conv),
                                                      lambda b, s: (b, 0, 0))],
        [f32(t, d_attn)] * 3 + [f32(n_seq, d_attn, seq_len)] * 2
        + [f32(t, d_conv), f32(n_seq, CONV_WIDTH - 1, d_conv)],
        [pltpu.VMEM((8, d_conv), F32)],
        (x2d, norm_g, w_in_bf16, conv_w, norm_gc))


def _mix_in_sample(x2d, norm_g, w_in_bf16, conv_w, norm_gc, st0, st1, *, d_attn, d_conv):
    t, d = x2d.shape
    full = lambda arr: pl.BlockSpec(arr.shape, lambda i: (0,) * arr.ndim)
    f32 = lambda *shape: jax.ShapeDtypeStruct(shape, F32)
    args = (x2d, norm_g, w_in_bf16, conv_w, norm_gc, st0, st1)
    outs = [f32(t, d_attn)] * 3 + [f32(t, d_conv)] * 2
    return _mix_in_call(
        functools.partial(_mix_in_kernel, d_attn=d_attn, d_conv=d_conv, sequential=False),
        (1,), [full(a) for a in args], [full(o) for o in outs], outs, [], args)


def _attn_prompt_kernel(q_ref, k_ref, v_ref, o_ref, m_s, l_s, a_s, *, seq_len):
    w = WIN_KEYS
    scale = HEAD_DIM ** -0.5 * LOG2_E
    r_i = lax.broadcasted_iota(jnp.int32, (2 * w, 2 * w), 0) & (w - 1)
    c_i = lax.broadcasted_iota(jnp.int32, (2 * w, 2 * w), 1)
    mask_cur = (lax.broadcasted_iota(jnp.int32, (2 * w, w), 1)
                <= lax.broadcasted_iota(jnp.int32, (2 * w, w), 0) & (w - 1))
    mask_both = jnp.logical_and(c_i >= r_i, c_i - w <= r_i)
    first_head = lax.broadcasted_iota(jnp.int32, (w, 2 * HEAD_DIM), 1) < HEAD_DIM
    dn_t = (((1,), (1,)), ((), ()))

    def rows(start, dil):
        if dil > 1:
            return pl.ds(start, w, stride=dil)
        return pl.ds(start if isinstance(start, int) else pl.multiple_of(start, w), w)

    def run_branch(dil, first, last):
        span = dil * w
        nb = seq_len // span

        def blocks(its, with_prev):
            mask = mask_both if with_prev else mask_cur

            def issue_scores(it):
                g = it % dil
                n = it // dil
                c = rows(g + n * span, dil)
                qb = (q_ref[c, :] * scale).astype(BF16)
                zero = jnp.zeros_like(qb)
                q = jnp.concatenate([jnp.where(first_head, qb, zero),
                                     jnp.where(first_head, zero, qb)], axis=0)
                k = k_ref[c, :].astype(BF16)
                v = v_ref[c, :].astype(BF16)
                if with_prev:
                    p = rows(g + (n - 1) * span, dil)
                    k = jnp.concatenate([k_ref[p, :].astype(BF16), k], axis=0)
                    v = jnp.concatenate([v_ref[p, :].astype(BF16), v], axis=0)
                return c, lax.dot_general(q, k, dn_t, preferred_element_type=F32), v

            def finish(c, s, v):
                s = jnp.where(mask, s, NEG)
                m = jnp.max(s, axis=-1, keepdims=True)
                p = jnp.exp2(s - m).astype(BF16)
                ones = jnp.ones((v.shape[0], 2 * HEAD_DIM), BF16)
                acc_l = jnp.dot(p, jnp.concatenate([v, ones], axis=1), preferred_element_type=F32)
                acc, l = acc_l[:, :2 * HEAD_DIM], acc_l[:, 2 * HEAD_DIM:]
                m_b = jnp.where(first_head, m[:w], m[w:])
                l_b = jnp.where(first_head, l[:w], l[w:])
                a_b = jnp.where(first_head, acc[:w], acc[w:])
                if not first:
                    m_o = m_s[c, :]
                    m_n = jnp.maximum(m_o, m_b)
                    w_o = jnp.exp2(m_o - m_n)
                    w_b = jnp.exp2(m_b - m_n)
                    l_b = w_o * l_s[c, :] + w_b * l_b
                    a_b = w_o * a_s[c, :] + w_b * a_b
                    m_b = m_n
                if last:
                    o_ref[c, :] = a_b / l_b
                else:
                    m_s[c, :] = m_b
                    l_s[c, :] = l_b
                    a_s[c, :] = a_b

            in_flight = []
            for i in range(len(its) + ATTN_LAG):
                if i < len(its):
                    in_flight.append(issue_scores(its[i]))
                if i >= ATTN_LAG:
                    finish(*in_flight.pop(0))

        def run(lo, hi, with_prev):
            u = ATTN_UNROLL
            trips = (hi - lo) // u

            def body(t, carry):
                blocks([lo + t * u + j for j in range(u)], with_prev)
                return carry

            if trips:
                lax.fori_loop(0, trips, body, 0)
            if lo + trips * u < hi:
                blocks(list(range(lo + trips * u, hi)), with_prev)

        run(0, dil, False)
        run(dil, dil * nb, True)

    order = sorted(DILATIONS, reverse=True)
    for i, dil in enumerate(order):
        run_branch(dil, i == 0, i == len(order) - 1)


def _attn_sample_kernel(q_ref, kn_ref, vn_ref, kt_ref, vt_ref, o_ref):
    n_heads, dh, w_buf = kt_ref.shape
    delta = w_buf - lax.broadcasted_iota(jnp.int32, (1, w_buf), 1)
    cnt = jnp.zeros((1, w_buf), F32)
    for dil in DILATIONS:
        assert dil & (dil - 1) == 0
        member = jnp.where(delta <= dil * WIN_KEYS, 1.0, 0.0)
        cnt = cnt + jnp.where((delta & (dil - 1)) == 0, member, 0.0)
    eye = (lax.broadcasted_iota(jnp.int32, (dh, dh), 0)
           == lax.broadcasted_iota(jnp.int32, (dh, dh), 1))
    to_col = lambda r: jnp.sum(jnp.where(eye, r, 0.0), axis=1, keepdims=True)
    to_row = lambda c: jnp.sum(jnp.where(eye, c, 0.0), axis=0, keepdims=True)
    outs = []
    for h in range(n_heads):
        sl = slice(h * dh, (h + 1) * dh)
        q = q_ref[:, sl] * (HEAD_DIM ** -0.5)
        s_self = jnp.sum(q * kn_ref[:, sl], axis=1, keepdims=True)
        s = jnp.sum(to_col(q) * kt_ref[h], axis=0, keepdims=True)
        s = jnp.where(cnt > 0.0, s, NEG)
        m = jnp.maximum(jnp.max(s, axis=1, keepdims=True), s_self)
        p = cnt * jnp.exp(s - m)
        p_self = len(DILATIONS) * jnp.exp(s_self - m)
        l = jnp.sum(p, axis=1, keepdims=True) + p_self
        acc = jnp.sum(p * vt_ref[h], axis=1, keepdims=True)
        outs.append((to_row(acc) + p_self * vn_ref[:, sl]) / l)
    o_ref[...] = jnp.concatenate(outs, axis=1)


def _attn_kernel(q_ref, k_ref, v_ref, qs_ref, kn_ref, vn_ref, kt_ref, vt_ref, o_ref, os_ref,
                 m_s, l_s, a_s, *, seq_len):
    _attn_sample_kernel(qs_ref, kn_ref, vn_ref, kt_ref, vt_ref, os_ref)
    _attn_prompt_kernel(q_ref, k_ref, v_ref, o_ref, m_s, l_s, a_s, seq_len=seq_len)


def _attention(q, k, v, qs, k_new, v_new, cache_kt, cache_vt, *, n_seq, seq_len):
    t, d_attn = q.shape
    db, n_heads, dh, w_buf = cache_kt.shape
    pair = 2 * HEAD_DIM
    pairs = d_attn // pair
    assert db == n_seq * pairs, "one sample sequence per prompt grid step"
    spec = pl.BlockSpec((seq_len, pair), lambda b, h: (b, h))
    head_spec = pl.BlockSpec((None, 1, d_attn), lambda b, h: (b * pairs + h, 0, 0))
    cache_spec = pl.BlockSpec((None, n_heads, dh, w_buf), lambda b, h: (b * pairs + h, 0, 0, 0))
    return pl.pallas_call(
        functools.partial(_attn_kernel, seq_len=seq_len),
        grid=(n_seq, pairs),
        in_specs=[spec] * 3 + [head_spec] * 3 + [cache_spec] * 2,
        out_specs=[spec, head_spec],
        out_shape=[jax.ShapeDtypeStruct((t, d_attn), F32),
                   jax.ShapeDtypeStruct((db, 1, d_attn), F32)],
        scratch_shapes=[pltpu.VMEM((seq_len, pair), F32)] * 3,
        compiler_params=pltpu.CompilerParams(
            dimension_semantics=("arbitrary", "arbitrary"), vmem_limit_bytes=VMEM_LIMIT),
        name="attention",
    )(q, k, v, qs, k_new, v_new, cache_kt, cache_vt)


R_E0, R_E1, R_G0, R_G1, R_POS0, R_POS1 = range(6)
ROUTER_LANE0 = N_GROUPS
CHUNK = 16
CHUNKS_PER_BLOCK = EXPERT_BLOCK // CHUNK


def _max_tile_chunks(tm):
    return (2 * tm + (CHUNK - 1) * N_EXPERTS) // CHUNK


def _local_rows(tm):
    return 2 * tm + N_EXPERTS * CHUNK


def _mix_out_kernel(*refs, n_tiles, has_tail):
    if has_tail:
        *tile_in, tail_ref, h_ref, route_ref, xs_ref, cnt_ref, tok_s, pos_s = refs
    else:
        *tile_in, h_ref, route_ref, xs_ref, cnt_ref, tok_s, pos_s = refs
    i = pl.program_id(0)
    route = lambda slot, *between: _route_tile(*tile_in, h_ref, route_ref, cnt_ref,
                                               tok_s.at[slot], pos_s.at[slot], *between)
    sort = lambda slot: _sort_tile(tok_s.at[slot], pos_s.at[slot], xs_ref)

    @pl.when(i == 0)
    def _():
        route(0)

    for parity in range(2):
        @pl.when(jnp.logical_and(jnp.logical_and(i >= 1, i < n_tiles), i % 2 == parity))
        def _():
            route(parity, lambda: sort(1 - parity))

    @pl.when(i == n_tiles)
    def _():
        sort((n_tiles - 1) % 2)

    if has_tail:
        @pl.when(i == n_tiles + 1)
        def _():
            rows = tail_ref.shape[0]
            xs_ref[0:rows, :] = tail_ref[...]
            xs_ref[rows:, :] = jnp.zeros((xs_ref.shape[0] - rows, xs_ref.shape[1]), xs_ref.dtype)


def _sort_tile(tok_ref, pos_ref, xs_ref):
    tm = tok_ref.shape[0]
    l1 = pos_ref[R_POS0:R_POS0 + 1, :].astype(jnp.int32)
    l2 = pos_ref[R_POS1:R_POS1 + 1, :].astype(jnp.int32)
    srow = lax.broadcasted_iota(jnp.int32, (xs_ref.shape[0], tm), 0)
    perm = jnp.where(srow == l1, 1.0, jnp.where(srow == l2, 1.0, 0.0)).astype(BF16)
    xs_ref[...] = jnp.dot(perm, tok_ref[...], preferred_element_type=F32).astype(BF16)


def _route_tile(x_ref, a_ref, oc_ref, ga_ref, wo_ref, gf_ref, wr_ref, br_ref,
                h_ref, route_ref, cnt_ref, tok_ref, pos_ref, after_projections=lambda: None):
    d_attn = a_ref.shape[1]
    tm, d = x_ref.shape
    a = _rms(a_ref[...], ga_ref[...]).astype(BF16)
    mix = jnp.dot(a, wo_ref[0:d_attn, :], preferred_element_type=F32)
    mix = mix + jnp.dot(oc_ref[...].astype(BF16), wo_ref[d_attn:, :], preferred_element_type=F32)
    h = x_ref[...] + mix
    h_ref[...] = h
    tok = _rms(h, gf_ref[...])

    tok_hi = tok.astype(BF16)
    tok_lo = (tok - tok_hi.astype(F32)).astype(BF16)
    hi_part = jnp.dot(tok_hi, wr_ref[...], preferred_element_type=F32)
    lo_part = jnp.dot(tok_lo, wr_ref[:, :LANES], preferred_element_type=F32)
    logits = hi_part[:, :LANES] + hi_part[:, LANES:] + lo_part + br_ref[...]
    after_projections()
    lane = lax.broadcasted_iota(jnp.int32, logits.shape, 1)
    big = jnp.int32(LANES)
    neg_inf = jnp.float32(-jnp.inf)

    def top1(vals):
        best = jnp.max(vals, axis=-1, keepdims=True)
        idx = jnp.min(jnp.where(vals == best, lane, big), axis=-1, keepdims=True)
        return best, idx

    is_group = lane < N_GROUPS
    lg = jnp.where(is_group, logits, neg_inf)
    mg, g_sel = top1(lg)
    p_group = 1.0 / jnp.sum(jnp.where(is_group, jnp.exp(lg - mg), 0.0), axis=-1, keepdims=True)

    lo = ROUTER_LANE0 + g_sel * EXPERTS_PER_GROUP
    in_group = jnp.logical_and(lane >= lo, lane < lo + EXPERTS_PER_GROUP)
    le = jnp.where(in_group, logits, neg_inf)
    v1, i1 = top1(le)
    v2, i2 = top1(jnp.where(lane == i1, neg_inf, le))
    e2 = jnp.exp(v2 - v1)
    gate1 = p_group / (1.0 + e2)
    gate2 = p_group * e2 / (1.0 + e2)

    oh1 = lane == i1
    oh2 = lane == i2
    both = jnp.where(jnp.logical_or(oh1, oh2), 1.0, 0.0)
    r_i = lax.broadcasted_iota(jnp.int32, (tm, tm), 0)
    c_i = lax.broadcasted_iota(jnp.int32, (tm, tm), 1)
    strict_lower = jnp.where(c_i < r_i, 1.0, 0.0).astype(BF16)
    before = jnp.dot(strict_lower, both.astype(BF16), preferred_element_type=F32)
    chunks = jnp.floor((jnp.sum(both, axis=0, keepdims=True) + (CHUNK - 1)) * (1.0 / CHUNK))
    u_r = lax.broadcasted_iota(jnp.int32, (LANES, LANES), 0)
    u_c = lax.broadcasted_iota(jnp.int32, (LANES, LANES), 1)
    strict_upper = jnp.where(u_r < u_c, 1.0, 0.0).astype(BF16)
    chunks8 = jnp.broadcast_to(chunks, (8, LANES))
    first_row = CHUNK * jnp.dot(chunks8.astype(BF16), strict_upper,
                                preferred_element_type=F32)[0:1, :]
    pos = first_row + before
    pos1 = jnp.sum(jnp.where(oh1, pos, 0.0), axis=-1, keepdims=True)
    pos2 = jnp.sum(jnp.where(oh2, pos, 0.0), axis=-1, keepdims=True)
    cnt_ref[...] = jnp.where(lax.broadcasted_iota(jnp.int32, (8, LANES), 0) == 0, chunks8, 0.0)

    rec = jnp.zeros(logits.shape, F32)
    for col, val in ((R_E0, (i1 - ROUTER_LANE0).astype(F32)), (R_E1, (i2 - ROUTER_LANE0).astype(F32)),
                     (R_G0, gate1), (R_G1, gate2), (R_POS0, pos1), (R_POS1, pos2)):
        rec = jnp.where(lane == col, val, rec)
    route_ref[...] = rec
    tok_ref[...] = tok_hi
    pos_ref[...] = rec.T[0:pos_ref.shape[0], :]


def _mix_out(x2d, attn, oconv, norm_ga, w_out_bf16, norm_gf, w_router, b_router, tail=None):
    t, d = x2d.shape
    d_attn, d_conv = attn.shape[1], oconv.shape[1]
    tm = min(ROW_TILE, t)
    nt = t // tm
    r_l = _local_rows(tm)
    has_tail = tail is not None
    tile = lambda i: jnp.minimum(i, nt - 1)
    sorted_block = lambda i: jnp.minimum(jnp.maximum(i - 1, 0), nt - 1 + has_tail)
    row = lambda width: pl.BlockSpec((tm, width), lambda i: (tile(i), 0))
    full = lambda arr: pl.BlockSpec(arr.shape, lambda i: (0, 0))
    args = [x2d, attn, oconv, norm_ga, w_out_bf16, norm_gf, w_router, b_router]
    in_specs = [row(d), row(d_attn), row(d_conv)] + [full(a) for a in args[3:]]
    if has_tail:
        assert tail.shape[0] <= r_l and tail.shape[1] == d
        args.append(tail)
        in_specs.append(full(tail))
    return pl.pallas_call(
        functools.partial(_mix_out_kernel, n_tiles=nt, has_tail=has_tail),
        grid=(nt + 1 + has_tail,),
        in_specs=in_specs,
        out_specs=[row(d), row(LANES), pl.BlockSpec((r_l, d), lambda i: (sorted_block(i), 0)),
                   pl.BlockSpec((None, 8, LANES), lambda i: (tile(i), 0, 0))],
        out_shape=[jax.ShapeDtypeStruct((t, d), F32), jax.ShapeDtypeStruct((t, LANES), F32),
                   jax.ShapeDtypeStruct(((nt + has_tail) * r_l, d), BF16),
                   jax.ShapeDtypeStruct((nt, 8, LANES), F32)],
        scratch_shapes=[pltpu.VMEM((2, tm, d), BF16), pltpu.VMEM((2, 8, tm), F32)],
        compiler_params=pltpu.CompilerParams(
            dimension_semantics=("arbitrary",), vmem_limit_bytes=VMEM_LIMIT),
        name="mix_out",
    )(*args)


def _sorted_layout(tile_chunks, tile_row0, max_local, n_blocks):
    nt, n_exp = tile_chunks.shape
    cpb = CHUNKS_PER_BLOCK
    i32 = jnp.int32
    seg = jnp.sum(tile_chunks, axis=0)
    padded = (seg + cpb - 1) // cpb * cpb
    pend = jnp.cumsum(padded)
    pstart = pend - padded
    tile_incl = jnp.cumsum(tile_chunks, axis=0)
    tile_excl = tile_incl - tile_chunks
    local_incl = jnp.cumsum(tile_chunks, axis=1)
    local_excl = local_incl - tile_chunks
    base = pstart[None, :] + tile_excl

    block_first = jnp.arange(n_blocks, dtype=i32) * cpb
    block_e = jnp.minimum(jnp.sum((pend[None, :] <= block_first[:, None]).astype(i32), axis=1),
                          n_exp - 1)
    n_used = (pend[-1:] // cpb).astype(i32)

    onehot_pick = lambda onehot, table: jnp.sum(jnp.where(onehot, table, 0), axis=-1)

    is_e = block_e[:, None] == jnp.arange(n_exp, dtype=i32)[None, :]
    of_expert = lambda table_te: onehot_pick(is_e[:, None, :], table_te[None, :, :])
    incl_b, cnt_b, lexcl_b = of_expert(tile_incl), of_expert(tile_chunks), of_expert(local_excl)
    q = (block_first - onehot_pick(is_e, pstart[None, :]))[:, None] + jnp.arange(cpb, dtype=i32)
    tile_q = jnp.minimum(jnp.sum((incl_b[:, None, :] <= q[:, :, None]).astype(i32), axis=2), nt - 1)
    is_t = tile_q[:, :, None] == jnp.arange(nt, dtype=i32)[None, None, :]
    of_tile = lambda table_bt: onehot_pick(is_t, table_bt[:, None, :])
    local_chunk = of_tile(lexcl_b) + q - of_tile(incl_b - cnt_b)
    seg_b = onehot_pick(is_e, seg[None, :])
    block_rows = (CHUNK * jnp.clip(seg_b - q[:, 0], 0, cpb)).astype(i32)
    in_run = jnp.logical_and(q >= 0, q < seg_b[:, None])
    src_row = jnp.where(in_run, of_tile(tile_row0[None, :]) + CHUNK * local_chunk, 0)
    src_row = src_row.reshape(-1).astype(i32)

    c = jnp.arange(max_local, dtype=i32)
    e_c = jnp.minimum(jnp.sum((local_incl[:, None, :] <= c[None, :, None]).astype(i32), axis=2),
                      n_exp - 1)
    is_ec = e_c[:, :, None] == jnp.arange(n_exp, dtype=i32)[None, None, :]
    of_run = lambda table_te: onehot_pick(is_ec, table_te[:, None, :])
    global_chunk = of_run(base) + c[None, :] - of_run(local_excl)
    tile_src = jnp.where(c[None, :] < local_incl[:, -1:], CHUNK * global_chunk, 0).astype(i32)
    e_ids = jnp.arange(n_exp, dtype=i32)
    later = jnp.logical_and(seg[None, :] > 0, e_ids[None, :] > e_ids[:, None])
    next_e = jnp.min(jnp.where(later, e_ids[None, :], n_exp), axis=1)
    next_e = jnp.where(next_e < n_exp, next_e, -1).astype(i32)
    return block_e.astype(i32), n_used, src_row, tile_src, next_e, block_rows


def _chunk_gather(src_ref, hbm_ref, buf, sems, item, slot, n_chunks, *, wait):
    for c in range(n_chunks):
        row = 0 if wait else pl.multiple_of(src_ref[item * n_chunks + c], CHUNK)
        copy = pltpu.make_async_copy(hbm_ref.at[pl.ds(row, CHUNK)],
                                     buf.at[slot, pl.ds(c * CHUNK, CHUNK)], sems.at[slot])
        if wait:
            copy.wait()
        else:
            copy.start()


def _prefetched(gather, step, n_items, body):
    slot = step % 2

    @pl.when(jnp.logical_and(step == 0, n_items > 0))
    def _():
        gather(0, 0, wait=False)

    @pl.when(step + 1 < n_items)
    def _():
        gather(step + 1, 1 - slot, wait=False)

    body(slot, lambda: gather(step, slot, wait=True))


def _experts_kernel(block_e_ref, n_used_ref, src_ref, next_e_ref, block_rows_ref, xs_ref, wg_hbm,
                    wu_hbm, wd_hbm, y_hbm, xblk, sems, ybuf, ysems, wg_f, wu_f, wd_f, wsems, wg_b, wu_b, wd_b,
                    run_ref):
    rows = EXPERT_BLOCK
    n_blocks = y_hbm.shape[0] // rows
    n_used = n_used_ref[0]
    gather = functools.partial(_chunk_gather, src_ref, xs_ref, xblk, sems,
                               n_chunks=CHUNKS_PER_BLOCK)

    def weight_copies(expert, slot):
        copies = []
        for hbm, stage in ((wg_hbm, wg_f), (wu_hbm, wu_f), (wd_hbm, wd_f)):
            slab = hbm.shape[1] // WEIGHT_SLABS
            for i in range(WEIGHT_SLABS):
                rows_i = pl.ds(i * slab, slab)
                copies.append(pltpu.make_async_copy(hbm.at[expert, rows_i], stage.at[slot, rows_i],
                                                    wsems.at[slot]))
        return copies

    def y_copy(blk, slot):
        start = blk * rows if isinstance(blk, int) else pl.multiple_of(blk * rows, rows)
        return pltpu.make_async_copy(ybuf.at[slot], y_hbm.at[pl.ds(start, rows)], ysems.at[slot])

    def block(b, carry):
        e = block_e_ref[b]
        new_expert = jnp.logical_or(b == 0, e != block_e_ref[jnp.maximum(b - 1, 0)])

        @pl.when(jnp.logical_and(new_expert, b < n_used))
        def _():
            @pl.when(b == 0)
            def _():
                run_ref[0] = 0
                for copy in weight_copies(e, 0):
                    copy.start()

            @pl.when(b > 0)
            def _():
                run_ref[0] = run_ref[0] + 1

            slot = run_ref[0] % 2
            nxt = next_e_ref[e]

            @pl.when(nxt >= 0)
            def _():
                for copy in weight_copies(nxt, 1 - slot):
                    copy.start()

            for copy in weight_copies(e, slot):
                copy.wait()
            wg_b[...] = wg_f[slot].astype(BF16)
            wu_b[...] = wu_f[slot].astype(BF16)
            wd_b[...] = wd_f[slot].astype(BF16)

        def body(slot, wait_current):
            @pl.when(b >= 2)
            def _():
                y_copy(b - 2, slot).wait()

            @pl.when(b < n_used)
            def _():
                wait_current()

            valid = block_rows_ref[b]
            for m in range(EXPERT_ROW_STEP, rows + 1, EXPERT_ROW_STEP):
                @pl.when(jnp.logical_and(valid > m - EXPERT_ROW_STEP, valid <= m))
                def _():
                    x = xblk[slot, 0:m, :]
                    gate = jnp.dot(x, wg_b[...], preferred_element_type=F32)
                    up = jnp.dot(x, wu_b[...], preferred_element_type=F32)
                    hid = gate * (1.0 / (1.0 + jnp.exp(-gate))) * up
                    ybuf[slot, 0:m, :] = jnp.dot(hid.astype(BF16), wd_b[...],
                                                 preferred_element_type=F32).astype(BF16)
                    if m < rows:
                        ybuf[slot, m:rows, :] = jnp.zeros((rows - m, ybuf.shape[2]), ybuf.dtype)

            @pl.when(valid == 0)
            def _():
                ybuf[slot] = jnp.zeros(ybuf.shape[1:], ybuf.dtype)

            y_copy(b, slot).start()

        _prefetched(gather, b, n_used, body)
        return carry

    lax.fori_loop(0, n_blocks, block, 0)
    for blk in range(max(n_blocks - 2, 0), n_blocks):
        y_copy(blk, blk % 2).wait()


def _experts(block_e, n_used, src_row, next_e, block_rows, xs, w_gate, w_up, w_down):
    n_blocks = block_e.shape[0]
    _, d, d_exp = w_gate.shape
    blk = EXPERT_BLOCK
    any_spec = pl.BlockSpec(memory_space=pl.ANY)
    return pl.pallas_call(
        _experts_kernel,
        grid_spec=pltpu.PrefetchScalarGridSpec(
            num_scalar_prefetch=5,
            grid=(1,),
            in_specs=[any_spec] * 4,
            out_specs=any_spec,
            scratch_shapes=[pltpu.VMEM((2, blk, d), BF16), pltpu.SemaphoreType.DMA((2,)),
                            pltpu.VMEM((2, blk, d), BF16), pltpu.SemaphoreType.DMA((2,)),
                            pltpu.VMEM((2, d, d_exp), F32), pltpu.VMEM((2, d, d_exp), F32),
                            pltpu.VMEM((2, d_exp, d), F32), pltpu.SemaphoreType.DMA((2,)),
                            pltpu.VMEM((d, d_exp), BF16), pltpu.VMEM((d, d_exp), BF16),
                            pltpu.VMEM((d_exp, d), BF16), pltpu.SMEM((1,), jnp.int32)],
        ),
        out_shape=jax.ShapeDtypeStruct((n_blocks * blk, d), BF16),
        compiler_params=pltpu.CompilerParams(
            dimension_semantics=("arbitrary",), vmem_limit_bytes=VMEM_LIMIT),
        name="experts",
    )(block_e, n_used, src_row, next_e, block_rows, xs, w_gate, w_up, w_down)


def _combine_kernel(src_ref, h_ref, route_ref, gn_ref, ybuf_ref, o_ref, yloc, sems):
    tm = h_ref.shape[0]
    r_l = yloc.shape[1]
    gather = functools.partial(_chunk_gather, src_ref, ybuf_ref, yloc, sems,
                               n_chunks=r_l // CHUNK)

    def body(slot, wait_current):
        wait_current()
        y = yloc[slot]
        route = route_ref[...]
        l0 = route[:, R_POS0:R_POS0 + 1].astype(jnp.int32)
        l1 = route[:, R_POS1:R_POS1 + 1].astype(jnp.int32)
        srow = lax.broadcasted_iota(jnp.int32, (tm, r_l), 1)
        gates = jnp.where(srow == l0, route[:, R_G0:R_G0 + 1],
                          jnp.where(srow == l1, route[:, R_G1:R_G1 + 1], 0.0)).astype(BF16)
        f = jnp.dot(gates, y, preferred_element_type=F32)
        o_ref[...] = _rms(h_ref[...] + f, gn_ref[...])

    _prefetched(gather, pl.program_id(0), pl.num_programs(0), body)


def _combine(tile_src, h, route, norm_g, ybuf):
    t, d = h.shape
    tm = min(ROW_TILE, t)
    r_l = _local_rows(tm)
    return pl.pallas_call(
        _combine_kernel,
        grid_spec=pltpu.PrefetchScalarGridSpec(
            num_scalar_prefetch=1,
            grid=(t // tm,),
            in_specs=[pl.BlockSpec((tm, d), lambda i, src: (i, 0)),
                      pl.BlockSpec((tm, LANES), lambda i, src: (i, 0)),
                      pl.BlockSpec((1, d), lambda i, src: (0, 0)),
                      pl.BlockSpec(memory_space=pl.ANY)],
            out_specs=pl.BlockSpec((tm, d), lambda i, src: (i, 0)),
            scratch_shapes=[pltpu.VMEM((2, r_l, d), BF16),
                            pltpu.SemaphoreType.DMA((2,))],
        ),
        out_shape=jax.ShapeDtypeStruct((t, d), F32),
        compiler_params=pltpu.CompilerParams(
            dimension_semantics=("arbitrary",), vmem_limit_bytes=VMEM_LIMIT),
        name="combine",
    )(tile_src, h, route, norm_g, ybuf)


def kernel(x_prompt, x_sample, cache_k, cache_v, state_conv, norm_mix, w_in, conv_w, norm_out_attn,
           norm_out_conv, w_out, norm_ffn, w_router_group, b_router_group, w_router_expert,
           b_router_expert, w_gate, w_up, w_down, norm_final):
    n_seq, seq_len, d = x_prompt.shape
    db, ds, _ = x_sample.shape
    depth = w_in.shape[0]
    _, _, w_buf, n_heads, dh = cache_k.shape
    d_attn = n_heads * dh
    d_conv = d - d_attn
    assert depth == 1 and ds == 1 and dh == HEAD_DIM
    assert seq_len % (max(DILATIONS) * WIN_KEYS) == 0 and seq_len <= max(DILATIONS) * WIN_KEYS
    layer = 0
    tp, ts = n_seq * seq_len, db

    xp = x_prompt.reshape(tp, d)
    xs = x_sample.reshape(ts, d)
    row = lambda vec: vec.reshape(1, -1)
    w_in_b = w_in[layer].astype(BF16)
    w_out_b = w_out[layer].astype(BF16)
    g_mix, g_oa, g_oc, g_ffn = (row(norm_mix[layer]), row(norm_out_attn[layer]),
                                row(norm_out_conv[layer]), row(norm_ffn[layer]))
    st0, st1 = state_conv[layer, :, 0, :], state_conv[layer, :, 1, :]

    qp, kp, vp, kp_t, vp_t, ocp, conv_p = _mix_in_prompt(
        xp, g_mix, w_in_b, conv_w[layer], g_oc, seq_len=seq_len, d_attn=d_attn, d_conv=d_conv)
    qs, ks, vs, ocs, us = _mix_in_sample(
        xs, g_mix, w_in_b, conv_w[layer], g_oc, st0, st1, d_attn=d_attn, d_conv=d_conv)

    heads = lambda a: a.reshape(ts, 1, d_attn)
    positions_last = lambda c: jnp.transpose(c, (0, 2, 3, 1))
    attn_p, attn_s = _attention(qp, kp, vp, heads(qs), heads(ks), heads(vs),
                                positions_last(cache_k[layer]), positions_last(cache_v[layer]),
                                n_seq=n_seq, seq_len=seq_len)
    attn_s = attn_s.reshape(ts, d_attn)

    n_route = N_GROUPS + N_EXPERTS
    w_router = jnp.zeros((d, LANES), F32).at[:, :N_GROUPS].set(w_router_group[layer])
    w_router = w_router.at[:, N_GROUPS:n_route].set(w_router_expert[layer])
    b_router = jnp.zeros((1, LANES), F32).at[0, :N_GROUPS].set(b_router_group[layer])
    b_router = b_router.at[0, N_GROUPS:n_route].set(b_router_expert[layer])
    w_router_hi = w_router.astype(BF16)
    w_router_lo = (w_router - w_router_hi.astype(F32)).astype(BF16)
    mix_out = functools.partial(_mix_out, norm_ga=g_oa, w_out_bf16=w_out_b, norm_gf=g_ffn,
                                w_router=jnp.concatenate([w_router_hi, w_router_lo], axis=1),
                                b_router=b_router)
    h_s, route_s, xs_s, cnt_s = mix_out(xs, attn_s, ocs)
    assert cnt_s.shape[0] == 1
    h_p, route_p, xs_all, cnt_p = mix_out(xp, attn_p, ocp, tail=xs_s)

    tile_chunks = jnp.concatenate([cnt_p[:, 0, ROUTER_LANE0:n_route],
                                   cnt_s[:, 0, ROUTER_LANE0:n_route]], axis=0).astype(jnp.int32)
    ntp, nts = cnt_p.shape[0], cnt_s.shape[0]
    tm_p, tm_s = tp // ntp, ts // nts
    rl_p, rl_s = _local_rows(tm_p), _local_rows(tm_s)
    tile_row0 = jnp.arange(ntp + nts, dtype=jnp.int32) * rl_p
    total_chunks = ntp * _max_tile_chunks(tm_p) + nts * _max_tile_chunks(tm_s)
    n_blocks = -(-(total_chunks + N_EXPERTS * (CHUNKS_PER_BLOCK - 1)) // CHUNKS_PER_BLOCK)
    block_e, n_used, src_row, tile_src, next_e, block_rows = _sorted_layout(
        tile_chunks, tile_row0, rl_p // CHUNK, n_blocks)
    ybuf = _experts(block_e, n_used, src_row, next_e, block_rows, xs_all, w_gate[layer],
                    w_up[layer], w_down[layer])
    g_fin = row(norm_final)
    y_p = _combine(tile_src[:ntp].reshape(-1), h_p, route_p, g_fin, ybuf)
    y_s = _combine(tile_src[ntp:, :rl_s // CHUNK].reshape(-1), h_s, route_s, g_fin, ybuf)

    w_keep = min(max(DILATIONS) * WIN_KEYS, seq_len)
    kv5 = lambda a_t: jnp.transpose(a_t.reshape(n_seq, n_heads, dh, seq_len),
                                    (0, 3, 1, 2))[None, :, seq_len - w_keep:]
    conv_s = jnp.stack([st1, us], axis=1)[None]
    kvs = lambda a: a.reshape(1, ts, 1, n_heads, dh)
    return (y_p.reshape(n_seq, seq_len, d), y_s.reshape(db, ds, d), kv5(kp_t), kv5(vp_t),
            conv_p[None], kvs(ks), kvs(vs), conv_s)
```

```python
import functools

import jax
import jax.numpy as jnp
from jax import lax
from jax.experimental import pallas as pl
from jax.experimental.pallas import tpu as pltpu

HEAD_DIM = 64
WIN_KEYS = 128
DILATIONS = (1, 4, 16)
CONV_WIDTH = 3
N_GROUPS = 4
EXPERTS_PER_GROUP = 8
N_EXPERTS = N_GROUPS * EXPERTS_PER_GROUP
EPS = 1e-6
NEG = -1e30
LOG2_E = 1.4426950408889634

LANES = 128
ROW_TILE = 512
MIX_IN_TILE = 1024
EXPERT_BLOCK = 512
EXPERT_ROW_STEP = 128
ATTN_LAG = 3
ATTN_UNROLL = 8
VMEM_LIMIT = 56 * 1024 * 1024

F32 = jnp.float32
BF16 = jnp.bfloat16


def _rms(x, g):
    return x * lax.rsqrt(jnp.mean(x * x, axis=-1, keepdims=True) + EPS) * g


def _mix_in_kernel(*refs, d_attn, d_conv, sequential):
    if sequential:
        (x_ref, g_ref, w_ref, cw_ref, gc_ref,
         q_ref, k_ref, v_ref, kt_ref, vt_ref, oc_ref, st_ref, carry_ref) = refs
    else:
        (x_ref, g_ref, w_ref, cw_ref, gc_ref, st0_ref, st1_ref,
         q_ref, k_ref, v_ref, oc_ref, u_ref) = refs
    x = x_ref[...]
    xb = _rms(x, g_ref[...]).astype(BF16)

    def proj(lo, width):
        return jnp.dot(xb, w_ref[:, lo:lo + width], preferred_element_type=F32)

    q_ref[...] = proj(0, d_attn)
    k = proj(d_attn, d_attn)
    v = proj(2 * d_attn, d_attn)
    k_ref[...] = k
    v_ref[...] = v
    gate = proj(3 * d_attn, d_conv)
    u = proj(3 * d_attn + d_conv, d_conv) * proj(3 * d_attn + 2 * d_conv, d_conv)

    tm = x.shape[0]
    if sequential:
        kt_ref[...] = k.T
        vt_ref[...] = v.T

        @pl.when(pl.program_id(1) == 0)
        def _():
            carry_ref[...] = jnp.zeros_like(carry_ref)

        row = lax.broadcasted_iota(jnp.int32, u.shape, 0)
        prev1 = carry_ref[1:2, :]
        prev2 = carry_ref[0:1, :]
        u1 = jnp.where(row == 0, prev1, pltpu.roll(u, 1, axis=0))
        u2 = jnp.where(row == 0, prev2, jnp.where(row == 1, prev1, pltpu.roll(u, 2, axis=0)))
        carry_ref[0:2, :] = u[tm - 2:tm, :]
        st_ref[...] = u[tm - 2:tm, :]
    else:
        u_ref[...] = u
        u2 = st0_ref[...]
        u1 = st1_ref[...]
    z = u2 * cw_ref[0:1, :] + u1 * cw_ref[1:2, :] + u * cw_ref[2:3, :]
    oc_ref[...] = _rms(gate * z, gc_ref[...])


def _mix_in_call(kernel, grid, in_specs, out_specs, out_shape, scratch, args):
    return pl.pallas_call(
        kernel, grid=grid, in_specs=in_specs, out_specs=out_specs, out_shape=out_shape,
        scratch_shapes=scratch,
        compiler_params=pltpu.CompilerParams(
            dimension_semantics=("arbitrary",) * len(grid), vmem_limit_bytes=VMEM_LIMIT),
        name="mix_in",
    )(*args)


def _mix_in_prompt(x2d, norm_g, w_in_bf16, conv_w, norm_gc, *, seq_len, d_attn, d_conv):
    t, d = x2d.shape
    tm = min(MIX_IN_TILE, seq_len)
    n_seq, per = t // seq_len, seq_len // tm
    const = lambda b, s: (0, 0)
    row = lambda width: pl.BlockSpec((tm, width), lambda b, s: (b * per + s, 0))
    col = pl.BlockSpec((None, d_attn, tm), lambda b, s: (b, 0, s))
    f32 = lambda *shape: jax.ShapeDtypeStruct(shape, F32)
    return _mix_in_call(
        functools.partial(_mix_in_kernel, d_attn=d_attn, d_conv=d_conv, sequential=True),
        (n_seq, per),
        [row(d), pl.BlockSpec((1, d), const),
         pl.BlockSpec(w_in_bf16.shape, const, pipeline_mode=pl.Buffered(1)),
         pl.BlockSpec((CONV_WIDTH, d_conv), const), pl.BlockSpec((1, d_conv), const)],
        [row(d_attn)] * 3 + [col] * 2 + [row(d_conv),
                                         pl.BlockSpec((None, CONV_WIDTH - 1, d_conv),
                                                      lambda b, s: (b, 0, 0))],
        [f32(t, d_attn)] * 3 + [f32(n_seq, d_attn, seq_len)] * 2
        + [f32(t, d_conv), f32(n_seq, CONV_WIDTH - 1, d_conv)],
        [pltpu.VMEM((8, d_conv), F32)],
        (x2d, norm_g, w_in_bf16, conv_w, norm_gc))


def _mix_in_sample(x2d, norm_g, w_in_bf16, conv_w, norm_gc, st0, st1, *, d_attn, d_conv):
    t, d = x2d.shape
    full = lambda arr: pl.BlockSpec(arr.shape, lambda i: (0,) * arr.ndim)
    f32 = lambda *shape: jax.ShapeDtypeStruct(shape, F32)
    args = (x2d, norm_g, w_in_bf16, conv_w, norm_gc, st0, st1)
    outs = [f32(t, d_attn)] * 3 + [f32(t, d_conv)] * 2
    return _mix_in_call(
        functools.partial(_mix_in_kernel, d_attn=d_attn, d_conv=d_conv, sequential=False),
        (1,), [full(a) for a in args], [full(o) for o in outs], outs, [], args)


def _attn_prompt_kernel(q_ref, k_ref, v_ref, o_ref, m_s, l_s, a_s, *, seq_len):
    w = WIN_KEYS
    scale = HEAD_DIM ** -0.5 * LOG2_E
    r_i = lax.broadcasted_iota(jnp.int32, (2 * w, 2 * w), 0) & (w - 1)
    c_i = lax.broadcasted_iota(jnp.int32, (2 * w, 2 * w), 1)
    mask_cur = (lax.broadcasted_iota(jnp.int32, (2 * w, w), 1)
                <= lax.broadcasted_iota(jnp.int32, (2 * w, w), 0) & (w - 1))
    mask_both = jnp.logical_and(c_i >= r_i, c_i - w <= r_i)
    first_head = lax.broadcasted_iota(jnp.int32, (w, 2 * HEAD_DIM), 1) < HEAD_DIM
    dn_t = (((1,), (1,)), ((), ()))

    def rows(start, dil):
        if dil > 1:
            return pl.ds(start, w, stride=dil)
        return pl.ds(start if isinstance(start, int) else pl.multiple_of(start, w), w)

    def run_branch(dil, first, last):
        span = dil * w
        nb = seq_len // span

        def blocks(its, with_prev):
            mask = mask_both if with_prev else mask_cur

            def issue_scores(it):
                g = it % dil
                n = it // dil
                c = rows(g + n * span, dil)
                qb = (q_ref[c, :] * scale).astype(BF16)
                zero = jnp.zeros_like(qb)
                q = jnp.concatenate([jnp.where(first_head, qb, zero),
                                     jnp.where(first_head, zero, qb)], axis=0)
                k = k_ref[c, :].astype(BF16)
                v = v_ref[c, :].astype(BF16)
                if with_prev:
                    p = rows(g + (n - 1) * span, dil)
                    k = jnp.concatenate([k_ref[p, :].astype(BF16), k], axis=0)
                    v = jnp.concatenate([v_ref[p, :].astype(BF16), v], axis=0)
                return c, lax.dot_general(q, k, dn_t, preferred_element_type=F32), v

            def finish(c, s, v):
                s = jnp.where(mask, s, NEG)
                m = jnp.max(s, axis=-1, keepdims=True)
                p = jnp.exp2(s - m).astype(BF16)
                ones = jnp.ones((v.shape[0], 2 * HEAD_DIM), BF16)
                acc_l = jnp.dot(p, jnp.concatenate([v, ones], axis=1), preferred_element_type=F32)
                acc, l = acc_l[:, :2 * HEAD_DIM], acc_l[:, 2 * HEAD_DIM:]
                m_b = jnp.where(first_head, m[:w], m[w:])
                l_b = jnp.where(first_head, l[:w], l[w:])
                a_b = jnp.where(first_head, acc[:w], acc[w:])
                if not first:
                    m_o = m_s[c, :]
                    m_n = jnp.maximum(m_o, m_b)
                    w_o = jnp.exp2(m_o - m_n)
                    w_b = jnp.exp2(m_b - m_n)
                    l_b = w_o * l_s[c, :] + w_b * l_b
                    a_b = w_o * a_s[c, :] + w_b * a_b
                    m_b = m_n
                if last:
                    o_ref[c, :] = a_b / l_b
                else:
                    m_s[c, :] = m_b
                    l_s[c, :] = l_b
                    a_s[c, :] = a_b

            in_flight = []
            for i in range(len(its) + ATTN_LAG):
                if i < len(its):
                    in_flight.append(issue_scores(its[i]))
                if i >= ATTN_LAG:
                    finish(*in_flight.pop(0))

        def run(lo, hi, with_prev):
            u = ATTN_UNROLL
            trips = (hi - lo) // u

            def body(t, carry):
                blocks([lo + t * u + j for j in range(u)], with_prev)
                return carry

            if trips:
                lax.fori_loop(0, trips, body, 0)
            if lo + trips * u < hi:
                blocks(list(range(lo + trips * u, hi)), with_prev)

        run(0, dil, False)
        run(dil, dil * nb, True)

    order = sorted(DILATIONS, reverse=True)
    for i, dil in enumerate(order):
        run_branch(dil, i == 0, i == len(order) - 1)


def _attn_sample_kernel(q_ref, kn_ref, vn_ref, kt_ref, vt_ref, o_ref):
    n_heads, dh, w_buf = kt_ref.shape
    delta = w_buf - lax.broadcasted_iota(jnp.int32, (1, w_buf), 1)
    cnt = jnp.zeros((1, w_buf), F32)
    for dil in DILATIONS:
        assert dil & (dil - 1) == 0
        member = jnp.where(delta <= dil * WIN_KEYS, 1.0, 0.0)
        cnt = cnt + jnp.where((delta & (dil - 1)) == 0, member, 0.0)
    eye = (lax.broadcasted_iota(jnp.int32, (dh, dh), 0)
           == lax.broadcasted_iota(jnp.int32, (dh, dh), 1))
    to_col = lambda r: jnp.sum(jnp.where(eye, r, 0.0), axis=1, keepdims=True)
    to_row = lambda c: jnp.sum(jnp.where(eye, c, 0.0), axis=0, keepdims=True)
    outs = []
    for h in range(n_heads):
        sl = slice(h * dh, (h + 1) * dh)
        q = q_ref[:, sl] * (HEAD_DIM ** -0.5)
        s_self = jnp.sum(q * kn_ref[:, sl], axis=1, keepdims=True)
        s = jnp.sum(to_col(q) * kt_ref[h], axis=0, keepdims=True)
        s = jnp.where(cnt > 0.0, s, NEG)
        m = jnp.maximum(jnp.max(s, axis=1, keepdims=True), s_self)
        p = cnt * jnp.exp(s - m)
        p_self = len(DILATIONS) * jnp.exp(s_self - m)
        l = jnp.sum(p, axis=1, keepdims=True) + p_self
        acc = jnp.sum(p * vt_ref[h], axis=1, keepdims=True)
        outs.append((to_row(acc) + p_self * vn_ref[:, sl]) / l)
    o_ref[...] = jnp.concatenate(outs, axis=1)


def _attn_kernel(q_ref, k_ref, v_ref, qs_ref, kn_ref, vn_ref, kt_ref, vt_ref, o_ref, os_ref,
                 m_s, l_s, a_s, *, seq_len):
    _attn_sample_kernel(qs_ref, kn_ref, vn_ref, kt_ref, vt_ref, os_ref)
    _attn_prompt_kernel(q_ref, k_ref, v_ref, o_ref, m_s, l_s, a_s, seq_len=seq_len)


def _attention(q, k, v, qs, k_new, v_new, cache_kt, cache_vt, *, n_seq, seq_len):
    t, d_attn = q.shape
    db, n_heads, dh, w_buf = cache_kt.shape
    pair = 2 * HEAD_DIM
    pairs = d_attn // pair
    assert db == n_seq * pairs, "one sample sequence per prompt grid step"
    spec = pl.BlockSpec((seq_len, pair), lambda b, h: (b, h))
    head_spec = pl.BlockSpec((None, 1, d_attn), lambda b, h: (b * pairs + h, 0, 0))
    cache_spec = pl.BlockSpec((None, n_heads, dh, w_buf), lambda b, h: (b * pairs + h, 0, 0, 0))
    return pl.pallas_call(
        functools.partial(_attn_kernel, seq_len=seq_len),
        grid=(n_seq, pairs),
        in_specs=[spec] * 3 + [head_spec] * 3 + [cache_spec] * 2,
        out_specs=[spec, head_spec],
        out_shape=[jax.ShapeDtypeStruct((t, d_attn), F32),
                   jax.ShapeDtypeStruct((db, 1, d_attn), F32)],
        scratch_shapes=[pltpu.VMEM((seq_len, pair), F32)] * 3,
        compiler_params=pltpu.CompilerParams(
            dimension_semantics=("arbitrary", "arbitrary"), vmem_limit_bytes=VMEM_LIMIT),
        name="attention",
    )(q, k, v, qs, k_new, v_new, cache_kt, cache_vt)


R_E0, R_E1, R_G0, R_G1, R_POS0, R_POS1 = range(6)
ROUTER_LANE0 = N_GROUPS
CHUNK = 16
CHUNKS_PER_BLOCK = EXPERT_BLOCK // CHUNK


def _max_tile_chunks(tm):
    return (2 * tm + (CHUNK - 1) * N_EXPERTS) // CHUNK


def _local_rows(tm):
    return 2 * tm + N_EXPERTS * CHUNK


def _mix_out_kernel(*refs, n_tiles, has_tail):
    if has_tail:
        *tile_in, tail_ref, h_ref, route_ref, xs_ref, cnt_ref, tok_s, pos_s = refs
    else:
        *tile_in, h_ref, route_ref, xs_ref, cnt_ref, tok_s, pos_s = refs
    i = pl.program_id(0)
    route = lambda slot, *between: _route_tile(*tile_in, h_ref, route_ref, cnt_ref,
                                               tok_s.at[slot], pos_s.at[slot], *between)
    sort = lambda slot: _sort_tile(tok_s.at[slot], pos_s.at[slot], xs_ref)

    @pl.when(i == 0)
    def _():
        route(0)

    @pl.when(jnp.logical_and(i >= 1, i < n_tiles))
    def _():
        route(i % 2, lambda: sort(1 - i % 2))

    @pl.when(i == n_tiles)
    def _():
        sort((n_tiles - 1) % 2)

    if has_tail:
        @pl.when(i == n_tiles + 1)
        def _():
            rows = tail_ref.shape[0]
            xs_ref[0:rows, :] = tail_ref[...]
            xs_ref[rows:, :] = jnp.zeros((xs_ref.shape[0] - rows, xs_ref.shape[1]), xs_ref.dtype)


def _sort_tile(tok_ref, pos_ref, xs_ref):
    tm = tok_ref.shape[0]
    l1 = pos_ref[R_POS0:R_POS0 + 1, :].astype(jnp.int32)
    l2 = pos_ref[R_POS1:R_POS1 + 1, :].astype(jnp.int32)
    srow = lax.broadcasted_iota(jnp.int32, (xs_ref.shape[0], tm), 0)
    perm = jnp.where(srow == l1, 1.0, jnp.where(srow == l2, 1.0, 0.0)).astype(BF16)
    xs_ref[...] = jnp.dot(perm, tok_ref[...], preferred_element_type=F32).astype(BF16)


def _route_tile(x_ref, a_ref, oc_ref, ga_ref, wo_ref, gf_ref, wr_ref, br_ref,
                h_ref, route_ref, cnt_ref, tok_ref, pos_ref, after_projections=lambda: None):
    d_attn = a_ref.shape[1]
    tm, d = x_ref.shape
    a = _rms(a_ref[...], ga_ref[...]).astype(BF16)
    mix = jnp.dot(a, wo_ref[0:d_attn, :], preferred_element_type=F32)
    mix = mix + jnp.dot(oc_ref[...].astype(BF16), wo_ref[d_attn:, :], preferred_element_type=F32)
    h = x_ref[...] + mix
    h_ref[...] = h
    tok = _rms(h, gf_ref[...])

    tok_hi = tok.astype(BF16)
    tok_lo = (tok - tok_hi.astype(F32)).astype(BF16)
    hi_part = jnp.dot(tok_hi, wr_ref[...], preferred_element_type=F32)
    lo_part = jnp.dot(tok_lo, wr_ref[:, :LANES], preferred_element_type=F32)
    logits = hi_part[:, :LANES] + hi_part[:, LANES:] + lo_part + br_ref[...]
    after_projections()
    lane = lax.broadcasted_iota(jnp.int32, logits.shape, 1)
    big = jnp.int32(LANES)
    neg_inf = jnp.float32(-jnp.inf)

    def top1(vals):
        best = jnp.max(vals, axis=-1, keepdims=True)
        idx = jnp.min(jnp.where(vals == best, lane, big), axis=-1, keepdims=True)
        return best, idx

    is_group = lane < N_GROUPS
    lg = jnp.where(is_group, logits, neg_inf)
    mg, g_sel = top1(lg)
    p_group = 1.0 / jnp.sum(jnp.where(is_group, jnp.exp(lg - mg), 0.0), axis=-1, keepdims=True)

    lo = ROUTER_LANE0 + g_sel * EXPERTS_PER_GROUP
    in_group = jnp.logical_and(lane >= lo, lane < lo + EXPERTS_PER_GROUP)
    le = jnp.where(in_group, logits, neg_inf)
    v1, i1 = top1(le)
    v2, i2 = top1(jnp.where(lane == i1, neg_inf, le))
    e2 = jnp.exp(v2 - v1)
    gate1 = p_group / (1.0 + e2)
    gate2 = p_group * e2 / (1.0 + e2)

    oh1 = lane == i1
    oh2 = lane == i2
    both = jnp.where(jnp.logical_or(oh1, oh2), 1.0, 0.0)
    r_i = lax.broadcasted_iota(jnp.int32, (tm, tm), 0)
    c_i = lax.broadcasted_iota(jnp.int32, (tm, tm), 1)
    strict_lower = jnp.where(c_i < r_i, 1.0, 0.0).astype(BF16)
    before = jnp.dot(strict_lower, both.astype(BF16), preferred_element_type=F32)
    chunks = jnp.floor((jnp.sum(both, axis=0, keepdims=True) + (CHUNK - 1)) * (1.0 / CHUNK))
    u_r = lax.broadcasted_iota(jnp.int32, (LANES, LANES), 0)
    u_c = lax.broadcasted_iota(jnp.int32, (LANES, LANES), 1)
    strict_upper = jnp.where(u_r < u_c, 1.0, 0.0).astype(BF16)
    chunks8 = jnp.broadcast_to(chunks, (8, LANES))
    first_row = CHUNK * jnp.dot(chunks8.astype(BF16), strict_upper,
                                preferred_element_type=F32)[0:1, :]
    pos = first_row + before
    pos1 = jnp.sum(jnp.where(oh1, pos, 0.0), axis=-1, keepdims=True)
    pos2 = jnp.sum(jnp.where(oh2, pos, 0.0), axis=-1, keepdims=True)
    cnt_ref[...] = jnp.where(lax.broadcasted_iota(jnp.int32, (8, LANES), 0) == 0, chunks8, 0.0)

    rec = jnp.zeros(logits.shape, F32)
    for col, val in ((R_E0, (i1 - ROUTER_LANE0).astype(F32)), (R_E1, (i2 - ROUTER_LANE0).astype(F32)),
                     (R_G0, gate1), (R_G1, gate2), (R_POS0, pos1), (R_POS1, pos2)):
        rec = jnp.where(lane == col, val, rec)
    route_ref[...] = rec
    tok_ref[...] = tok_hi
    pos_ref[...] = rec.T[0:pos_ref.shape[0], :]


def _mix_out(x2d, attn, oconv, norm_ga, w_out_bf16, norm_gf, w_router, b_router, tail=None):
    t, d = x2d.shape
    d_attn, d_conv = attn.shape[1], oconv.shape[1]
    tm = min(ROW_TILE, t)
    nt = t // tm
    r_l = _local_rows(tm)
    has_tail = tail is not None
    tile = lambda i: jnp.minimum(i, nt - 1)
    sorted_block = lambda i: jnp.minimum(jnp.maximum(i - 1, 0), nt - 1 + has_tail)
    row = lambda width: pl.BlockSpec((tm, width), lambda i: (tile(i), 0))
    full = lambda arr: pl.BlockSpec(arr.shape, lambda i: (0, 0))
    args = [x2d, attn, oconv, norm_ga, w_out_bf16, norm_gf, w_router, b_router]
    in_specs = [row(d), row(d_attn), row(d_conv)] + [full(a) for a in args[3:]]
    if has_tail:
        assert tail.shape[0] <= r_l and tail.shape[1] == d
        args.append(tail)
        in_specs.append(full(tail))
    return pl.pallas_call(
        functools.partial(_mix_out_kernel, n_tiles=nt, has_tail=has_tail),
        grid=(nt + 1 + has_tail,),
        in_specs=in_specs,
        out_specs=[row(d), row(LANES), pl.BlockSpec((r_l, d), lambda i: (sorted_block(i), 0)),
                   pl.BlockSpec((None, 8, LANES), lambda i: (tile(i), 0, 0))],
        out_shape=[jax.ShapeDtypeStruct((t, d), F32), jax.ShapeDtypeStruct((t, LANES), F32),
                   jax.ShapeDtypeStruct(((nt + has_tail) * r_l, d), BF16),
                   jax.ShapeDtypeStruct((nt, 8, LANES), F32)],
        scratch_shapes=[pltpu.VMEM((2, tm, d), BF16), pltpu.VMEM((2, 8, tm), F32)],
        compiler_params=pltpu.CompilerParams(
            dimension_semantics=("arbitrary",), vmem_limit_bytes=VMEM_LIMIT),
        name="mix_out",
    )(*args)


def _sorted_layout(tile_chunks, tile_row0, max_local, n_blocks):
    nt, n_exp = tile_chunks.shape
    cpb = CHUNKS_PER_BLOCK
    i32 = jnp.int32
    seg = jnp.sum(tile_chunks, axis=0)
    padded = (seg + cpb - 1) // cpb * cpb
    pend = jnp.cumsum(padded)
    pstart = pend - padded
    tile_incl = jnp.cumsum(tile_chunks, axis=0)
    tile_excl = tile_incl - tile_chunks
    local_incl = jnp.cumsum(tile_chunks, axis=1)
    local_excl = local_incl - tile_chunks
    base = pstart[None, :] + tile_excl

    block_first = jnp.arange(n_blocks, dtype=i32) * cpb
    block_e = jnp.minimum(jnp.sum((pend[None, :] <= block_first[:, None]).astype(i32), axis=1),
                          n_exp - 1)
    n_used = (pend[-1:] // cpb).astype(i32)

    onehot_pick = lambda onehot, table: jnp.sum(jnp.where(onehot, table, 0), axis=-1)

    is_e = block_e[:, None] == jnp.arange(n_exp, dtype=i32)[None, :]
    of_expert = lambda table_te: onehot_pick(is_e[:, None, :], table_te[None, :, :])
    incl_b, cnt_b, lexcl_b = of_expert(tile_incl), of_expert(tile_chunks), of_expert(local_excl)
    q = (block_first - onehot_pick(is_e, pstart[None, :]))[:, None] + jnp.arange(cpb, dtype=i32)
    tile_q = jnp.minimum(jnp.sum((incl_b[:, None, :] <= q[:, :, None]).astype(i32), axis=2), nt - 1)
    is_t = tile_q[:, :, None] == jnp.arange(nt, dtype=i32)[None, None, :]
    of_tile = lambda table_bt: onehot_pick(is_t, table_bt[:, None, :])
    local_chunk = of_tile(lexcl_b) + q - of_tile(incl_b - cnt_b)
    seg_b = onehot_pick(is_e, seg[None, :])
    block_rows = (CHUNK * jnp.clip(seg_b - q[:, 0], 0, cpb)).astype(i32)
    in_run = jnp.logical_and(q >= 0, q < seg_b[:, None])
    src_row = jnp.where(in_run, of_tile(tile_row0[None, :]) + CHUNK * local_chunk, 0)
    src_row = src_row.reshape(-1).astype(i32)

    c = jnp.arange(max_local, dtype=i32)
    e_c = jnp.minimum(jnp.sum((local_incl[:, None, :] <= c[None, :, None]).astype(i32), axis=2),
                      n_exp - 1)
    is_ec = e_c[:, :, None] == jnp.arange(n_exp, dtype=i32)[None, None, :]
    of_run = lambda table_te: onehot_pick(is_ec, table_te[:, None, :])
    global_chunk = of_run(base) + c[None, :] - of_run(local_excl)
    tile_src = jnp.where(c[None, :] < local_incl[:, -1:], CHUNK * global_chunk, 0).astype(i32)
    e_ids = jnp.arange(n_exp, dtype=i32)
    later = jnp.logical_and(seg[None, :] > 0, e_ids[None, :] > e_ids[:, None])
    next_e = jnp.min(jnp.where(later, e_ids[None, :], n_exp), axis=1)
    next_e = jnp.where(next_e < n_exp, next_e, -1).astype(i32)
    return block_e.astype(i32), n_used, src_row, tile_src, next_e, block_rows


def _chunk_gather(src_ref, hbm_ref, buf, sems, item, slot, n_chunks, *, wait):
    for c in range(n_chunks):
        row = 0 if wait else pl.multiple_of(src_ref[item * n_chunks + c], CHUNK)
        copy = pltpu.make_async_copy(hbm_ref.at[pl.ds(row, CHUNK)],
                                     buf.at[slot, pl.ds(c * CHUNK, CHUNK)], sems.at[slot])
        if wait:
            copy.wait()
        else:
            copy.start()


def _prefetched(gather, step, n_items, body):
    slot = step % 2

    @pl.when(jnp.logical_and(step == 0, n_items > 0))
    def _():
        gather(0, 0, wait=False)

    @pl.when(step + 1 < n_items)
    def _():
        gather(step + 1, 1 - slot, wait=False)

    body(slot, lambda: gather(step, slot, wait=True))


def _experts_kernel(block_e_ref, n_used_ref, src_ref, next_e_ref, block_rows_ref, xs_ref, wg_hbm,
                    wu_hbm, wd_hbm, y_hbm, xblk, sems, ybuf, ysems, wg_f, wu_f, wd_f, wsems, wg_b, wu_b, wd_b,
                    run_ref):
    rows = EXPERT_BLOCK
    n_blocks = y_hbm.shape[0] // rows
    n_used = n_used_ref[0]
    gather = functools.partial(_chunk_gather, src_ref, xs_ref, xblk, sems,
                               n_chunks=CHUNKS_PER_BLOCK)

    def weight_copies(expert, slot):
        return [pltpu.make_async_copy(hbm.at[expert], stage.at[slot], wsems.at[slot])
                for hbm, stage in ((wg_hbm, wg_f), (wu_hbm, wu_f), (wd_hbm, wd_f))]

    def y_copy(blk, slot):
        start = blk * rows if isinstance(blk, int) else pl.multiple_of(blk * rows, rows)
        return pltpu.make_async_copy(ybuf.at[slot], y_hbm.at[pl.ds(start, rows)], ysems.at[slot])

    def block(b, carry):
        e = block_e_ref[b]
        new_expert = jnp.logical_or(b == 0, e != block_e_ref[jnp.maximum(b - 1, 0)])

        @pl.when(jnp.logical_and(new_expert, b < n_used))
        def _():
            @pl.when(b == 0)
            def _():
                run_ref[0] = 0
                for copy in weight_copies(e, 0):
                    copy.start()

            @pl.when(b > 0)
            def _():
                run_ref[0] = run_ref[0] + 1

            slot = run_ref[0] % 2
            nxt = next_e_ref[e]

            @pl.when(nxt >= 0)
            def _():
                for copy in weight_copies(nxt, 1 - slot):
                    copy.start()

            for copy in weight_copies(e, slot):
                copy.wait()
            wg_b[...] = wg_f[slot].astype(BF16)
            wu_b[...] = wu_f[slot].astype(BF16)
            wd_b[...] = wd_f[slot].astype(BF16)

        def body(slot, wait_current):
            @pl.when(b >= 2)
            def _():
                y_copy(b - 2, slot).wait()

            @pl.when(b < n_used)
            def _():
                wait_current()

            valid = block_rows_ref[b]
            for m in range(EXPERT_ROW_STEP, rows + 1, EXPERT_ROW_STEP):
                @pl.when(jnp.logical_and(valid > m - EXPERT_ROW_STEP, valid <= m))
                def _():
                    x = xblk[slot, 0:m, :]
                    gate = jnp.dot(x, wg_b[...], preferred_element_type=F32)
                    up = jnp.dot(x, wu_b[...], preferred_element_type=F32)
                    hid = gate * (1.0 / (1.0 + jnp.exp(-gate))) * up
                    ybuf[slot, 0:m, :] = jnp.dot(hid.astype(BF16), wd_b[...],
                                                 preferred_element_type=F32).astype(BF16)
                    if m < rows:
                        ybuf[slot, m:rows, :] = jnp.zeros((rows - m, ybuf.shape[2]), ybuf.dtype)

            @pl.when(valid == 0)
            def _():
                ybuf[slot] = jnp.zeros(ybuf.shape[1:], ybuf.dtype)

            y_copy(b, slot).start()

        _prefetched(gather, b, n_used, body)
        return carry

    lax.fori_loop(0, n_blocks, block, 0)
    for blk in range(max(n_blocks - 2, 0), n_blocks):
        y_copy(blk, blk % 2).wait()


def _experts(block_e, n_used, src_row, next_e, block_rows, xs, w_gate, w_up, w_down):
    n_blocks = block_e.shape[0]
    _, d, d_exp = w_gate.shape
    blk = EXPERT_BLOCK
    any_spec = pl.BlockSpec(memory_space=pl.ANY)
    return pl.pallas_call(
        _experts_kernel,
        grid_spec=pltpu.PrefetchScalarGridSpec(
            num_scalar_prefetch=5,
            grid=(1,),
            in_specs=[any_spec] * 4,
            out_specs=any_spec,
            scratch_shapes=[pltpu.VMEM((2, blk, d), BF16), pltpu.SemaphoreType.DMA((2,)),
                            pltpu.VMEM((2, blk, d), BF16), pltpu.SemaphoreType.DMA((2,)),
                            pltpu.VMEM((2, d, d_exp), F32), pltpu.VMEM((2, d, d_exp), F32),
                            pltpu.VMEM((2, d_exp, d), F32), pltpu.SemaphoreType.DMA((2,)),
                            pltpu.VMEM((d, d_exp), BF16), pltpu.VMEM((d, d_exp), BF16),
                            pltpu.VMEM((d_exp, d), BF16), pltpu.SMEM((1,), jnp.int32)],
        ),
        out_shape=jax.ShapeDtypeStruct((n_blocks * blk, d), BF16),
        compiler_params=pltpu.CompilerParams(
            dimension_semantics=("arbitrary",), vmem_limit_bytes=VMEM_LIMIT),
        name="experts",
    )(block_e, n_used, src_row, next_e, block_rows, xs, w_gate, w_up, w_down)


def _combine_kernel(src_ref, h_ref, route_ref, gn_ref, ybuf_ref, o_ref, yloc, sems):
    tm = h_ref.shape[0]
    r_l = yloc.shape[1]
    gather = functools.partial(_chunk_gather, src_ref, ybuf_ref, yloc, sems,
                               n_chunks=r_l // CHUNK)

    def body(slot, wait_current):
        wait_current()
        y = yloc[slot]
        route = route_ref[...]
        l0 = route[:, R_POS0:R_POS0 + 1].astype(jnp.int32)
        l1 = route[:, R_POS1:R_POS1 + 1].astype(jnp.int32)
        srow = lax.broadcasted_iota(jnp.int32, (tm, r_l), 1)
        gates = jnp.where(srow == l0, route[:, R_G0:R_G0 + 1],
                          jnp.where(srow == l1, route[:, R_G1:R_G1 + 1], 0.0)).astype(BF16)
        f = jnp.dot(gates, y, preferred_element_type=F32)
        o_ref[...] = _rms(h_ref[...] + f, gn_ref[...])

    _prefetched(gather, pl.program_id(0), pl.num_programs(0), body)


def _combine(tile_src, h, route, norm_g, ybuf):
    t, d = h.shape
    tm = min(ROW_TILE, t)
    r_l = _local_rows(tm)
    return pl.pallas_call(
        _combine_kernel,
        grid_spec=pltpu.PrefetchScalarGridSpec(
            num_scalar_prefetch=1,
            grid=(t // tm,),
            in_specs=[pl.BlockSpec((tm, d), lambda i, src: (i, 0)),
                      pl.BlockSpec((tm, LANES), lambda i, src: (i, 0)),
                      pl.BlockSpec((1, d), lambda i, src: (0, 0)),
                      pl.BlockSpec(memory_space=pl.ANY)],
            out_specs=pl.BlockSpec((tm, d), lambda i, src: (i, 0)),
            scratch_shapes=[pltpu.VMEM((2, r_l, d), BF16),
                            pltpu.SemaphoreType.DMA((2,))],
        ),
        out_shape=jax.ShapeDtypeStruct((t, d), F32),
        compiler_params=pltpu.CompilerParams(
            dimension_semantics=("arbitrary",), vmem_limit_bytes=VMEM_LIMIT),
        name="combine",
    )(tile_src, h, route, norm_g, ybuf)


def kernel(x_prompt, x_sample, cache_k, cache_v, state_conv, norm_mix, w_in, conv_w, norm_out_attn,
           norm_out_conv, w_out, norm_ffn, w_router_group, b_router_group, w_router_expert,
           b_router_expert, w_gate, w_up, w_down, norm_final):
    n_seq, seq_len, d = x_prompt.shape
    db, ds, _ = x_sample.shape
    depth = w_in.shape[0]
    _, _, w_buf, n_heads, dh = cache_k.shape
    d_attn = n_heads * dh
    d_conv = d - d_attn
    assert depth == 1 and ds == 1 and dh == HEAD_DIM
    assert seq_len % (max(DILATIONS) * WIN_KEYS) == 0 and seq_len <= max(DILATIONS) * WIN_KEYS
    layer = 0
    tp, ts = n_seq * seq_len, db

    xp = x_prompt.reshape(tp, d)
    xs = x_sample.reshape(ts, d)
    row = lambda vec: vec.reshape(1, -1)
    w_in_b = w_in[layer].astype(BF16)
    w_out_b = w_out[layer].astype(BF16)
    g_mix, g_oa, g_oc, g_ffn = (row(norm_mix[layer]), row(norm_out_attn[layer]),
                                row(norm_out_conv[layer]), row(norm_ffn[layer]))
    st0, st1 = state_conv[layer, :, 0, :], state_conv[layer, :, 1, :]

    qp, kp, vp, kp_t, vp_t, ocp, conv_p = _mix_in_prompt(
        xp, g_mix, w_in_b, conv_w[layer], g_oc, seq_len=seq_len, d_attn=d_attn, d_conv=d_conv)
    qs, ks, vs, ocs, us = _mix_in_sample(
        xs, g_mix, w_in_b, conv_w[layer], g_oc, st0, st1, d_attn=d_attn, d_conv=d_conv)

    heads = lambda a: a.reshape(ts, 1, d_attn)
    positions_last = lambda c: jnp.transpose(c, (0, 2, 3, 1))
    attn_p, attn_s = _attention(qp, kp, vp, heads(qs), heads(ks), heads(vs),
                                positions_last(cache_k[layer]), positions_last(cache_v[layer]),
                                n_seq=n_seq, seq_len=seq_len)
    attn_s = attn_s.reshape(ts, d_attn)

    n_route = N_GROUPS + N_EXPERTS
    w_router = jnp.zeros((d, LANES), F32).at[:, :N_GROUPS].set(w_router_group[layer])
    w_router = w_router.at[:, N_GROUPS:n_route].set(w_router_expert[layer])
    b_router = jnp.zeros((1, LANES), F32).at[0, :N_GROUPS].set(b_router_group[layer])
    b_router = b_router.at[0, N_GROUPS:n_route].set(b_router_expert[layer])
    w_router_hi = w_router.astype(BF16)
    w_router_lo = (w_router - w_router_hi.astype(F32)).astype(BF16)
    mix_out = functools.partial(_mix_out, norm_ga=g_oa, w_out_bf16=w_out_b, norm_gf=g_ffn,
                                w_router=jnp.concatenate([w_router_hi, w_router_lo], axis=1),
                                b_router=b_router)
    h_s, route_s, xs_s, cnt_s = mix_out(xs, attn_s, ocs)
    assert cnt_s.shape[0] == 1
    h_p, route_p, xs_all, cnt_p = mix_out(xp, attn_p, ocp, tail=xs_s)

    tile_chunks = jnp.concatenate([cnt_p[:, 0, ROUTER_LANE0:n_route],
                                   cnt_s[:, 0, ROUTER_LANE0:n_route]], axis=0).astype(jnp.int32)
    ntp, nts = cnt_p.shape[0], cnt_s.shape[0]
    tm_p, tm_s = tp // ntp, ts // nts
    rl_p, rl_s = _local_rows(tm_p), _local_rows(tm_s)
    tile_row0 = jnp.arange(ntp + nts, dtype=jnp.int32) * rl_p
    total_chunks = ntp * _max_tile_chunks(tm_p) + nts * _max_tile_chunks(tm_s)
    n_blocks = -(-(total_chunks + N_EXPERTS * (CHUNKS_PER_BLOCK - 1)) // CHUNKS_PER_BLOCK)
    block_e, n_used, src_row, tile_src, next_e, block_rows = _sorted_layout(
        tile_chunks, tile_row0, rl_p // CHUNK, n_blocks)
    ybuf = _experts(block_e, n_used, src_row, next_e, block_rows, xs_all, w_gate[layer],
                    w_up[layer], w_down[layer])
    g_fin = row(norm_final)
    y_p = _combine(tile_src[:ntp].reshape(-1), h_p, route_p, g_fin, ybuf)
    y_s = _combine(tile_src[ntp:, :rl_s // CHUNK].reshape(-1), h_s, route_s, g_fin, ybuf)

    w_keep = min(max(DILATIONS) * WIN_KEYS, seq_len)
    kv5 = lambda a_t: jnp.transpose(a_t.reshape(n_seq, n_heads, dh, seq_len),
                                    (0, 3, 1, 2))[None, :, seq_len - w_keep:]
    conv_s = jnp.stack([st1, us], axis=1)[None]
    kvs = lambda a: a.reshape(1, ts, 1, n_heads, dh)
    return (y_p.reshape(n_seq, seq_len, d), y_s.reshape(db, ds, d), kv5(kp_t), kv5(vp_t),
            conv_p[None], kvs(ks), kvs(vs), conv_s)
```

```python
import functools

import jax
import jax.numpy as jnp
from jax import lax
from jax.experimental import pallas as pl
from jax.experimental.pallas import tpu as pltpu

HEAD_DIM = 64
WIN_KEYS = 128
DILATIONS = (1, 4, 16)
CONV_WIDTH = 3
N_GROUPS = 4
EXPERTS_PER_GROUP = 8
N_EXPERTS = N_GROUPS * EXPERTS_PER_GROUP
EPS = 1e-6
NEG = -1e30
LOG2_E = 1.4426950408889634

LANES = 128
ROW_TILE = 512
MIX_IN_TILE = 1024
EXPERT_BLOCK = 512
EXPERT_ROW_STEP = 128
ATTN_LAG = 3
ATTN_UNROLL = 8
VMEM_LIMIT = 56 * 1024 * 1024

F32 = jnp.float32
BF16 = jnp.bfloat16


def _rms(x, g):
    return x * lax.rsqrt(jnp.mean(x * x, axis=-1, keepdims=True) + EPS) * g


def _mix_in_kernel(*refs, d_attn, d_conv, sequential):
    if sequential:
        (x_ref, g_ref, w_ref, cw_ref, gc_ref,
         q_ref, k_ref, v_ref, kt_ref, vt_ref, oc_ref, st_ref, carry_ref) = refs
    else:
        (x_ref, g_ref, w_ref, cw_ref, gc_ref, st0_ref, st1_ref,
         q_ref, k_ref, v_ref, oc_ref, u_ref) = refs
    x = x_ref[...]
    xb = _rms(x, g_ref[...]).astype(BF16)

    def proj(lo, width):
        return jnp.dot(xb, w_ref[:, lo:lo + width], preferred_element_type=F32)

    q_ref[...] = proj(0, d_attn)
    k = proj(d_attn, d_attn)
    v = proj(2 * d_attn, d_attn)
    k_ref[...] = k
    v_ref[...] = v
    gate = proj(3 * d_attn, d_conv)
    u = proj(3 * d_attn + d_conv, d_conv) * proj(3 * d_attn + 2 * d_conv, d_conv)

    tm = x.shape[0]
    if sequential:
        kt_ref[...] = k.T
        vt_ref[...] = v.T

        @pl.when(pl.program_id(1) == 0)
        def _():
            carry_ref[...] = jnp.zeros_like(carry_ref)

        row = lax.broadcasted_iota(jnp.int32, u.shape, 0)
        prev1 = carry_ref[1:2, :]
        prev2 = carry_ref[0:1, :]
        u1 = jnp.where(row == 0, prev1, pltpu.roll(u, 1, axis=0))
        u2 = jnp.where(row == 0, prev2, jnp.where(row == 1, prev1, pltpu.roll(u, 2, axis=0)))
        carry_ref[0:2, :] = u[tm - 2:tm, :]
        st_ref[...] = u[tm - 2:tm, :]
    else:
        u_ref[...] = u
        u2 = st0_ref[...]
        u1 = st1_ref[...]
    z = u2 * cw_ref[0:1, :] + u1 * cw_ref[1:2, :] + u * cw_ref[2:3, :]
    oc_ref[...] = _rms(gate * z, gc_ref[...])


def _mix_in_call(kernel, grid, in_specs, out_specs, out_shape, scratch, args):
    return pl.pallas_call(
        kernel, grid=grid, in_specs=in_specs, out_specs=out_specs, out_shape=out_shape,
        scratch_shapes=scratch,
        compiler_params=pltpu.CompilerParams(
            dimension_semantics=("arbitrary",) * len(grid), vmem_limit_bytes=VMEM_LIMIT),
        name="mix_in",
    )(*args)


def _mix_in_prompt(x2d, norm_g, w_in_bf16, conv_w, norm_gc, *, seq_len, d_attn, d_conv):
    t, d = x2d.shape
    tm = min(MIX_IN_TILE, seq_len)
    n_seq, per = t // seq_len, seq_len // tm
    const = lambda b, s: (0, 0)
    row = lambda width: pl.BlockSpec((tm, width), lambda b, s: (b * per + s, 0))
    col = pl.BlockSpec((None, d_attn, tm), lambda b, s: (b, 0, s))
    f32 = lambda *shape: jax.ShapeDtypeStruct(shape, F32)
    return _mix_in_call(
        functools.partial(_mix_in_kernel, d_attn=d_attn, d_conv=d_conv, sequential=True),
        (n_seq, per),
        [row(d), pl.BlockSpec((1, d), const),
         pl.BlockSpec(w_in_bf16.shape, const, pipeline_mode=pl.Buffered(1)),
         pl.BlockSpec((CONV_WIDTH, d_conv), const), pl.BlockSpec((1, d_conv), const)],
        [row(d_attn)] * 3 + [col] * 2 + [row(d_conv),
                                         pl.BlockSpec((None, CONV_WIDTH - 1, d_conv),
                                                      lambda b, s: (b, 0, 0))],
        [f32(t, d_attn)] * 3 + [f32(n_seq, d_attn, seq_len)] * 2
        + [f32(t, d_conv), f32(n_seq, CONV_WIDTH - 1, d_conv)],
        [pltpu.VMEM((8, d_conv), F32)],
        (x2d, norm_g, w_in_bf16, conv_w, norm_gc))


def _mix_in_sample(x2d, norm_g, w_in_bf16, conv_w, norm_gc, st0, st1, *, d_attn, d_conv):
    t, d = x2d.shape
    full = lambda arr: pl.BlockSpec(arr.shape, lambda i: (0,) * arr.ndim)
    f32 = lambda *shape: jax.ShapeDtypeStruct(shape, F32)
    args = (x2d, norm_g, w_in_bf16, conv_w, norm_gc, st0, st1)
    outs = [f32(t, d_attn)] * 3 + [f32(t, d_conv)] * 2
    return _mix_in_call(
        functools.partial(_mix_in_kernel, d_attn=d_attn, d_conv=d_conv, sequential=False),
        (1,), [full(a) for a in args], [full(o) for o in outs], outs, [], args)


def _attn_prompt_kernel(q_ref, k_ref, v_ref, o_ref, m_s, l_s, a_s, *, seq_len):
    w = WIN_KEYS
    scale = HEAD_DIM ** -0.5 * LOG2_E
    r_i = lax.broadcasted_iota(jnp.int32, (2 * w, 2 * w), 0) & (w - 1)
    c_i = lax.broadcasted_iota(jnp.int32, (2 * w, 2 * w), 1)
    mask_cur = (lax.broadcasted_iota(jnp.int32, (2 * w, w), 1)
                <= lax.broadcasted_iota(jnp.int32, (2 * w, w), 0) & (w - 1))
    mask_both = jnp.logical_and(c_i >= r_i, c_i - w <= r_i)
    first_head = lax.broadcasted_iota(jnp.int32, (w, 2 * HEAD_DIM), 1) < HEAD_DIM
    dn_t = (((1,), (1,)), ((), ()))

    def rows(start, dil):
        if dil > 1:
            return pl.ds(start, w, stride=dil)
        return pl.ds(start if isinstance(start, int) else pl.multiple_of(start, w), w)

    def run_branch(dil, first, last):
        span = dil * w
        nb = seq_len // span

        def blocks(its, with_prev):
            mask = mask_both if with_prev else mask_cur

            def issue_scores(it):
                g = it % dil
                n = it // dil
                c = rows(g + n * span, dil)
                qb = (q_ref[c, :] * scale).astype(BF16)
                zero = jnp.zeros_like(qb)
                q = jnp.concatenate([jnp.where(first_head, qb, zero),
                                     jnp.where(first_head, zero, qb)], axis=0)
                k = k_ref[c, :].astype(BF16)
                v = v_ref[c, :].astype(BF16)
                if with_prev:
                    p = rows(g + (n - 1) * span, dil)
                    k = jnp.concatenate([k_ref[p, :].astype(BF16), k], axis=0)
                    v = jnp.concatenate([v_ref[p, :].astype(BF16), v], axis=0)
                return c, lax.dot_general(q, k, dn_t, preferred_element_type=F32), v

            def finish(c, s, v):
                s = jnp.where(mask, s, NEG)
                m = jnp.max(s, axis=-1, keepdims=True)
                p = jnp.exp2(s - m).astype(BF16)
                ones = jnp.ones((v.shape[0], 2 * HEAD_DIM), BF16)
                acc_l = jnp.dot(p, jnp.concatenate([v, ones], axis=1), preferred_element_type=F32)
                acc, l = acc_l[:, :2 * HEAD_DIM], acc_l[:, 2 * HEAD_DIM:]
                m_b = jnp.where(first_head, m[:w], m[w:])
                l_b = jnp.where(first_head, l[:w], l[w:])
                a_b = jnp.where(first_head, acc[:w], acc[w:])
                if not first:
                    m_o = m_s[c, :]
                    m_n = jnp.maximum(m_o, m_b)
                    w_o = jnp.exp2(m_o - m_n)
                    w_b = jnp.exp2(m_b - m_n)
                    l_b = w_o * l_s[c, :] + w_b * l_b
                    a_b = w_o * a_s[c, :] + w_b * a_b
                    m_b = m_n
                if last:
                    o_ref[c, :] = a_b / l_b
                else:
                    m_s[c, :] = m_b
                    l_s[c, :] = l_b
                    a_s[c, :] = a_b

            in_flight = []
            for i in range(len(its) + ATTN_LAG):
                if i < len(its):
                    in_flight.append(issue_scores(its[i]))
                if i >= ATTN_LAG:
                    finish(*in_flight.pop(0))

        def run(lo, hi, with_prev):
            u = ATTN_UNROLL
            trips = (hi - lo) // u

            def body(t, carry):
                blocks([lo + t * u + j for j in range(u)], with_prev)
                return carry

            if trips:
                lax.fori_loop(0, trips, body, 0)
            if lo + trips * u < hi:
                blocks(list(range(lo + trips * u, hi)), with_prev)

        run(0, dil, False)
        run(dil, dil * nb, True)

    order = sorted(DILATIONS, reverse=True)
    for i, dil in enumerate(order):
        run_branch(dil, i == 0, i == len(order) - 1)


def _attn_sample_kernel(q_ref, kn_ref, vn_ref, kt_ref, vt_ref, o_ref):
    n_heads, dh, w_buf = kt_ref.shape
    delta = w_buf - lax.broadcasted_iota(jnp.int32, (1, w_buf), 1)
    cnt = jnp.zeros((1, w_buf), F32)
    for dil in DILATIONS:
        assert dil & (dil - 1) == 0
        member = jnp.where(delta <= dil * WIN_KEYS, 1.0, 0.0)
        cnt = cnt + jnp.where((delta & (dil - 1)) == 0, member, 0.0)
    eye = (lax.broadcasted_iota(jnp.int32, (dh, dh), 0)
           == lax.broadcasted_iota(jnp.int32, (dh, dh), 1))
    to_col = lambda r: jnp.sum(jnp.where(eye, r, 0.0), axis=1, keepdims=True)
    to_row = lambda c: jnp.sum(jnp.where(eye, c, 0.0), axis=0, keepdims=True)
    outs = []
    for h in range(n_heads):
        sl = slice(h * dh, (h + 1) * dh)
        q = q_ref[:, sl] * (HEAD_DIM ** -0.5)
        s_self = jnp.sum(q * kn_ref[:, sl], axis=1, keepdims=True)
        s = jnp.sum(to_col(q) * kt_ref[h], axis=0, keepdims=True)
        s = jnp.where(cnt > 0.0, s, NEG)
        m = jnp.maximum(jnp.max(s, axis=1, keepdims=True), s_self)
        p = cnt * jnp.exp(s - m)
        p_self = len(DILATIONS) * jnp.exp(s_self - m)
        l = jnp.sum(p, axis=1, keepdims=True) + p_self
        acc = jnp.sum(p * vt_ref[h], axis=1, keepdims=True)
        outs.append((to_row(acc) + p_self * vn_ref[:, sl]) / l)
    o_ref[...] = jnp.concatenate(outs, axis=1)


def _attn_kernel(q_ref, k_ref, v_ref, qs_ref, kn_ref, vn_ref, kt_ref, vt_ref, o_ref, os_ref,
                 m_s, l_s, a_s, *, seq_len):
    _attn_sample_kernel(qs_ref, kn_ref, vn_ref, kt_ref, vt_ref, os_ref)
    _attn_prompt_kernel(q_ref, k_ref, v_ref, o_ref, m_s, l_s, a_s, seq_len=seq_len)


def _attention(q, k, v, qs, k_new, v_new, cache_kt, cache_vt, *, n_seq, seq_len):
    t, d_attn = q.shape
    db, n_heads, dh, w_buf = cache_kt.shape
    pair = 2 * HEAD_DIM
    pairs = d_attn // pair
    assert db == n_seq * pairs, "one sample sequence per prompt grid step"
    spec = pl.BlockSpec((seq_len, pair), lambda b, h: (b, h))
    head_spec = pl.BlockSpec((None, 1, d_attn), lambda b, h: (b * pairs + h, 0, 0))
    cache_spec = pl.BlockSpec((None, n_heads, dh, w_buf), lambda b, h: (b * pairs + h, 0, 0, 0))
    return pl.pallas_call(
        functools.partial(_attn_kernel, seq_len=seq_len),
        grid=(n_seq, pairs),
        in_specs=[spec] * 3 + [head_spec] * 3 + [cache_spec] * 2,
        out_specs=[spec, head_spec],
        out_shape=[jax.ShapeDtypeStruct((t, d_attn), F32),
                   jax.ShapeDtypeStruct((db, 1, d_attn), F32)],
        scratch_shapes=[pltpu.VMEM((seq_len, pair), F32)] * 3,
        compiler_params=pltpu.CompilerParams(
            dimension_semantics=("arbitrary", "arbitrary"), vmem_limit_bytes=VMEM_LIMIT),
        name="attention",
    )(q, k, v, qs, k_new, v_new, cache_kt, cache_vt)


R_E0, R_E1, R_G0, R_G1, R_POS0, R_POS1 = range(6)
ROUTER_LANE0 = N_GROUPS
CHUNK = 16
CHUNKS_PER_BLOCK = EXPERT_BLOCK // CHUNK


def _max_tile_chunks(tm):
    return (2 * tm + (CHUNK - 1) * N_EXPERTS) // CHUNK


def _local_rows(tm):
    return 2 * tm + N_EXPERTS * CHUNK


def _mix_out_kernel(*refs, n_tiles, has_tail):
    if has_tail:
        *tile_in, tail_ref, h_ref, route_ref, xs_ref, cnt_ref, tok_s, pos_s = refs
    else:
        *tile_in, h_ref, route_ref, xs_ref, cnt_ref, tok_s, pos_s = refs
    i = pl.program_id(0)
    route = lambda slot, *between: _route_tile(*tile_in, h_ref, route_ref, cnt_ref,
                                               tok_s.at[slot], pos_s.at[slot], *between)
    sort = lambda slot: _sort_tile(tok_s.at[slot], pos_s.at[slot], xs_ref)

    @pl.when(i == 0)
    def _():
        route(0)

    for parity in range(2):
        @pl.when(jnp.logical_and(jnp.logical_and(i >= 1, i < n_tiles), i % 2 == parity))
        def _():
            route(parity, lambda: sort(1 - parity))

    @pl.when(i == n_tiles)
    def _():
        sort((n_tiles - 1) % 2)

    if has_tail:
        @pl.when(i == n_tiles + 1)
        def _():
            rows = tail_ref.shape[0]
            xs_ref[0:rows, :] = tail_ref[...]
            xs_ref[rows:, :] = jnp.zeros((xs_ref.shape[0] - rows, xs_ref.shape[1]), xs_ref.dtype)


def _sort_tile(tok_ref, pos_ref, xs_ref):
    tm = tok_ref.shape[0]
    l1 = pos_ref[R_POS0:R_POS0 + 1, :].astype(jnp.int32)
    l2 = pos_ref[R_POS1:R_POS1 + 1, :].astype(jnp.int32)
    srow = lax.broadcasted_iota(jnp.int32, (xs_ref.shape[0], tm), 0)
    perm = jnp.where(srow == l1, 1.0, jnp.where(srow == l2, 1.0, 0.0)).astype(BF16)
    xs_ref[...] = jnp.dot(perm, tok_ref[...], preferred_element_type=F32).astype(BF16)


def _route_tile(x_ref, a_ref, oc_ref, ga_ref, wo_ref, gf_ref, wr_ref, br_ref,
                h_ref, route_ref, cnt_ref, tok_ref, pos_ref, after_projections=lambda: None):
    d_attn = a_ref.shape[1]
    tm, d = x_ref.shape
    a = _rms(a_ref[...], ga_ref[...]).astype(BF16)
    mix = jnp.dot(a, wo_ref[0:d_attn, :], preferred_element_type=F32)
    mix = mix + jnp.dot(oc_ref[...].astype(BF16), wo_ref[d_attn:, :], preferred_element_type=F32)
    h = x_ref[...] + mix
    h_ref[...] = h
    tok = _rms(h, gf_ref[...])

    tok_hi = tok.astype(BF16)
    tok_lo = (tok - tok_hi.astype(F32)).astype(BF16)
    hi_part = jnp.dot(tok_hi, wr_ref[...], preferred_element_type=F32)
    lo_part = jnp.dot(tok_lo, wr_ref[:, :LANES], preferred_element_type=F32)
    logits = hi_part[:, :LANES] + hi_part[:, LANES:] + lo_part + br_ref[...]
    after_projections()
    lane = lax.broadcasted_iota(jnp.int32, logits.shape, 1)
    big = jnp.int32(LANES)
    neg_inf = jnp.float32(-jnp.inf)

    def top1(vals):
        best = jnp.max(vals, axis=-1, keepdims=True)
        idx = jnp.min(jnp.where(vals == best, lane, big), axis=-1, keepdims=True)
        return best, idx

    is_group = lane < N_GROUPS
    lg = jnp.where(is_group, logits, neg_inf)
    mg, g_sel = top1(lg)
    p_group = 1.0 / jnp.sum(jnp.where(is_group, jnp.exp(lg - mg), 0.0), axis=-1, keepdims=True)

    lo = ROUTER_LANE0 + g_sel * EXPERTS_PER_GROUP
    in_group = jnp.logical_and(lane >= lo, lane < lo + EXPERTS_PER_GROUP)
    le = jnp.where(in_group, logits, neg_inf)
    v1, i1 = top1(le)
    v2, i2 = top1(jnp.where(lane == i1, neg_inf, le))
    e2 = jnp.exp(v2 - v1)
    gate1 = p_group / (1.0 + e2)
    gate2 = p_group * e2 / (1.0 + e2)

    oh1 = lane == i1
    oh2 = lane == i2
    both = jnp.where(jnp.logical_or(oh1, oh2), 1.0, 0.0)
    r_i = lax.broadcasted_iota(jnp.int32, (tm, tm), 0)
    c_i = lax.broadcasted_iota(jnp.int32, (tm, tm), 1)
    strict_lower = jnp.where(c_i < r_i, 1.0, 0.0).astype(BF16)
    before = jnp.dot(strict_lower, both.astype(BF16), preferred_element_type=F32)
    chunks = jnp.floor((jnp.sum(both, axis=0, keepdims=True) + (CHUNK - 1)) * (1.0 / CHUNK))
    u_r = lax.broadcasted_iota(jnp.int32, (LANES, LANES), 0)
    u_c = lax.broadcasted_iota(jnp.int32, (LANES, LANES), 1)
    strict_upper = jnp.where(u_r < u_c, 1.0, 0.0).astype(BF16)
    chunks8 = jnp.broadcast_to(chunks, (8, LANES))
    first_row = CHUNK * jnp.dot(chunks8.astype(BF16), strict_upper,
                                preferred_element_type=F32)[0:1, :]
    pos = first_row + before
    pos1 = jnp.sum(jnp.where(oh1, pos, 0.0), axis=-1, keepdims=True)
    pos2 = jnp.sum(jnp.where(oh2, pos, 0.0), axis=-1, keepdims=True)
    cnt_ref[...] = jnp.where(lax.broadcasted_iota(jnp.int32, (8, LANES), 0) == 0, chunks8, 0.0)

    rec = jnp.zeros(logits.shape, F32)
    for col, val in ((R_E0, (i1 - ROUTER_LANE0).astype(F32)), (R_E1, (i2 - ROUTER_LANE0).astype(F32)),
                     (R_G0, gate1), (R_G1, gate2), (R_POS0, pos1), (R_POS1, pos2)):
        rec = jnp.where(lane == col, val, rec)
    route_ref[...] = rec
    tok_ref[...] = tok_hi
    pos_ref[...] = rec.T[0:pos_ref.shape[0], :]


def _mix_out(x2d, attn, oconv, norm_ga, w_out_bf16, norm_gf, w_router, b_router, tail=None):
    t, d = x2d.shape
    d_attn, d_conv = attn.shape[1], oconv.shape[1]
    tm = min(ROW_TILE, t)
    nt = t // tm
    r_l = _local_rows(tm)
    has_tail = tail is not None
    tile = lambda i: jnp.minimum(i, nt - 1)
    sorted_block = lambda i: jnp.minimum(jnp.maximum(i - 1, 0), nt - 1 + has_tail)
    row = lambda width: pl.BlockSpec((tm, width), lambda i: (tile(i), 0))
    full = lambda arr: pl.BlockSpec(arr.shape, lambda i: (0, 0))
    args = [x2d, attn, oconv, norm_ga, w_out_bf16, norm_gf, w_router, b_router]
    in_specs = [row(d), row(d_attn), row(d_conv)] + [full(a) for a in args[3:]]
    if has_tail:
        assert tail.shape[0] <= r_l and tail.shape[1] == d
        args.append(tail)
        in_specs.append(full(tail))
    return pl.pallas_call(
        functools.partial(_mix_out_kernel, n_tiles=nt, has_tail=has_tail),
        grid=(nt + 1 + has_tail,),
        in_specs=in_specs,
        out_specs=[row(d), row(LANES), pl.BlockSpec((r_l, d), lambda i: (sorted_block(i), 0)),
                   pl.BlockSpec((None, 8, LANES), lambda i: (tile(i), 0, 0))],
        out_shape=[jax.ShapeDtypeStruct((t, d), F32), jax.ShapeDtypeStruct((t, LANES), F32),
                   jax.ShapeDtypeStruct(((nt + has_tail) * r_l, d), BF16),
                   jax.ShapeDtypeStruct((nt, 8, LANES), F32)],
        scratch_shapes=[pltpu.VMEM((2, tm, d), BF16), pltpu.VMEM((2, 8, tm), F32)],
        compiler_params=pltpu.CompilerParams(
            dimension_semantics=("arbitrary",), vmem_limit_bytes=VMEM_LIMIT),
        name="mix_out",
    )(*args)


def _sorted_layout(tile_chunks, tile_row0, max_local, n_blocks):
    nt, n_exp = tile_chunks.shape
    cpb = CHUNKS_PER_BLOCK
    i32 = jnp.int32
    seg = jnp.sum(tile_chunks, axis=0)
    padded = (seg + cpb - 1) // cpb * cpb
    pend = jnp.cumsum(padded)
    pstart = pend - padded
    tile_incl = jnp.cumsum(tile_chunks, axis=0)
    tile_excl = tile_incl - tile_chunks
    local_incl = jnp.cumsum(tile_chunks, axis=1)
    local_excl = local_incl - tile_chunks
    base = pstart[None, :] + tile_excl

    block_first = jnp.arange(n_blocks, dtype=i32) * cpb
    block_e = jnp.minimum(jnp.sum((pend[None, :] <= block_first[:, None]).astype(i32), axis=1),
                          n_exp - 1)
    n_used = (pend[-1:] // cpb).astype(i32)

    onehot_pick = lambda onehot, table: jnp.sum(jnp.where(onehot, table, 0), axis=-1)

    is_e = block_e[:, None] == jnp.arange(n_exp, dtype=i32)[None, :]
    of_expert = lambda table_te: onehot_pick(is_e[:, None, :], table_te[None, :, :])
    incl_b, cnt_b, lexcl_b = of_expert(tile_incl), of_expert(tile_chunks), of_expert(local_excl)
    q = (block_first - onehot_pick(is_e, pstart[None, :]))[:, None] + jnp.arange(cpb, dtype=i32)
    tile_q = jnp.minimum(jnp.sum((incl_b[:, None, :] <= q[:, :, None]).astype(i32), axis=2), nt - 1)
    is_t = tile_q[:, :, None] == jnp.arange(nt, dtype=i32)[None, None, :]
    of_tile = lambda table_bt: onehot_pick(is_t, table_bt[:, None, :])
    local_chunk = of_tile(lexcl_b) + q - of_tile(incl_b - cnt_b)
    seg_b = onehot_pick(is_e, seg[None, :])
    block_rows = (CHUNK * jnp.clip(seg_b - q[:, 0], 0, cpb)).astype(i32)
    in_run = jnp.logical_and(q >= 0, q < seg_b[:, None])
    src_row = jnp.where(in_run, of_tile(tile_row0[None, :]) + CHUNK * local_chunk, 0)
    src_row = src_row.reshape(-1).astype(i32)

    c = jnp.arange(max_local, dtype=i32)
    e_c = jnp.minimum(jnp.sum((local_incl[:, None, :] <= c[None, :, None]).astype(i32), axis=2),
                      n_exp - 1)
    is_ec = e_c[:, :, None] == jnp.arange(n_exp, dtype=i32)[None, None, :]
    of_run = lambda table_te: onehot_pick(is_ec, table_te[:, None, :])
    global_chunk = of_run(base) + c[None, :] - of_run(local_excl)
    tile_src = jnp.where(c[None, :] < local_incl[:, -1:], CHUNK * global_chunk, 0).astype(i32)
    e_ids = jnp.arange(n_exp, dtype=i32)
    later = jnp.logical_and(seg[None, :] > 0, e_ids[None, :] > e_ids[:, None])
    next_e = jnp.min(jnp.where(later, e_ids[None, :], n_exp), axis=1)
    next_e = jnp.where(next_e < n_exp, next_e, -1).astype(i32)
    return block_e.astype(i32), n_used, src_row, tile_src, next_e, block_rows


def _chunk_gather(src_ref, hbm_ref, buf, sems, item, slot, n_chunks, *, wait):
    for c in range(n_chunks):
        row = 0 if wait else pl.multiple_of(src_ref[item * n_chunks + c], CHUNK)
        copy = pltpu.make_async_copy(hbm_ref.at[pl.ds(row, CHUNK)],
                                     buf.at[slot, pl.ds(c * CHUNK, CHUNK)], sems.at[slot])
        if wait:
            copy.wait()
        else:
            copy.start()


def _prefetched(gather, step, n_items, body):
    slot = step % 2

    @pl.when(jnp.logical_and(step == 0, n_items > 0))
    def _():
        gather(0, 0, wait=False)

    @pl.when(step + 1 < n_items)
    def _():
        gather(step + 1, 1 - slot, wait=False)

    body(slot, lambda: gather(step, slot, wait=True))


def _experts_kernel(block_e_ref, n_used_ref, src_ref, next_e_ref, block_rows_ref, xs_ref, wg_hbm,
                    wu_hbm, wd_hbm, y_hbm, xblk, sems, ybuf, ysems, wg_f, wu_f, wd_f, wsems, wg_b, wu_b, wd_b,
                    run_ref):
    rows = EXPERT_BLOCK
    n_blocks = y_hbm.shape[0] // rows
    n_used = n_used_ref[0]
    gather = functools.partial(_chunk_gather, src_ref, xs_ref, xblk, sems,
                               n_chunks=CHUNKS_PER_BLOCK)

    def weight_copies(expert, slot):
        return [pltpu.make_async_copy(hbm.at[expert], stage.at[slot], wsems.at[slot])
                for hbm, stage in ((wg_hbm, wg_f), (wu_hbm, wu_f), (wd_hbm, wd_f))]

    def y_copy(blk, slot):
        start = blk * rows if isinstance(blk, int) else pl.multiple_of(blk * rows, rows)
        return pltpu.make_async_copy(ybuf.at[slot], y_hbm.at[pl.ds(start, rows)], ysems.at[slot])

    def block(b, carry):
        e = block_e_ref[b]
        new_expert = jnp.logical_or(b == 0, e != block_e_ref[jnp.maximum(b - 1, 0)])

        @pl.when(jnp.logical_and(new_expert, b < n_used))
        def _():
            @pl.when(b == 0)
            def _():
                run_ref[0] = 0
                for copy in weight_copies(e, 0):
                    copy.start()

            @pl.when(b > 0)
            def _():
                run_ref[0] = run_ref[0] + 1

            slot = run_ref[0] % 2
            nxt = next_e_ref[e]

            @pl.when(nxt >= 0)
            def _():
                for copy in weight_copies(nxt, 1 - slot):
                    copy.start()

            for copy in weight_copies(e, slot):
                copy.wait()
            wg_b[...] = wg_f[slot].astype(BF16)
            wu_b[...] = wu_f[slot].astype(BF16)
            wd_b[...] = wd_f[slot].astype(BF16)

        def body(slot, wait_current):
            @pl.when(b >= 2)
            def _():
                y_copy(b - 2, slot).wait()

            @pl.when(b < n_used)
            def _():
                wait_current()

            valid = block_rows_ref[b]
            for m in range(EXPERT_ROW_STEP, rows + 1, EXPERT_ROW_STEP):
                @pl.when(jnp.logical_and(valid > m - EXPERT_ROW_STEP, valid <= m))
                def _():
                    x = xblk[slot, 0:m, :]
                    gate = jnp.dot(x, wg_b[...], preferred_element_type=F32)
                    up = jnp.dot(x, wu_b[...], preferred_element_type=F32)
                    hid = gate * (1.0 / (1.0 + jnp.exp(-gate))) * up
                    ybuf[slot, 0:m, :] = jnp.dot(hid.astype(BF16), wd_b[...],
                                                 preferred_element_type=F32).astype(BF16)
                    if m < rows:
                        ybuf[slot, m:rows, :] = jnp.zeros((rows - m, ybuf.shape[2]), ybuf.dtype)

            @pl.when(valid == 0)
            def _():
                ybuf[slot] = jnp.zeros(ybuf.shape[1:], ybuf.dtype)

            y_copy(b, slot).start()

        _prefetched(gather, b, n_used, body)
        return carry

    lax.fori_loop(0, n_blocks, block, 0)
    for blk in range(max(n_blocks - 2, 0), n_blocks):
        y_copy(blk, blk % 2).wait()


def _experts(block_e, n_used, src_row, next_e, block_rows, xs, w_gate, w_up, w_down):
    n_blocks = block_e.shape[0]
    _, d, d_exp = w_gate.shape
    blk = EXPERT_BLOCK
    any_spec = pl.BlockSpec(memory_space=pl.ANY)
    return pl.pallas_call(
        _experts_kernel,
        grid_spec=pltpu.PrefetchScalarGridSpec(
            num_scalar_prefetch=5,
            grid=(1,),
            in_specs=[any_spec] * 4,
            out_specs=any_spec,
            scratch_shapes=[pltpu.VMEM((2, blk, d), BF16), pltpu.SemaphoreType.DMA((2,)),
                            pltpu.VMEM((2, blk, d), BF16), pltpu.SemaphoreType.DMA((2,)),
                            pltpu.VMEM((2, d, d_exp), F32), pltpu.VMEM((2, d, d_exp), F32),
                            pltpu.VMEM((2, d_exp, d), F32), pltpu.SemaphoreType.DMA((2,)),
                            pltpu.VMEM((d, d_exp), BF16), pltpu.VMEM((d, d_exp), BF16),
                            pltpu.VMEM((d_exp, d), BF16), pltpu.SMEM((1,), jnp.int32)],
        ),
        out_shape=jax.ShapeDtypeStruct((n_blocks * blk, d), BF16),
        compiler_params=pltpu.CompilerParams(
            dimension_semantics=("arbitrary",), vmem_limit_bytes=VMEM_LIMIT),
        name="experts",
    )(block_e, n_used, src_row, next_e, block_rows, xs, w_gate, w_up, w_down)


def _combine_kernel(src_ref, h_ref, route_ref, gn_ref, ybuf_ref, o_ref, yloc, sems):
    tm = h_ref.shape[0]
    r_l = yloc.shape[1]
    gather = functools.partial(_chunk_gather, src_ref, ybuf_ref, yloc, sems,
                               n_chunks=r_l // CHUNK)

    def body(slot, wait_current):
        wait_current()
        y = yloc[slot]
        route = route_ref[...]
        l0 = route[:, R_POS0:R_POS0 + 1].astype(jnp.int32)
        l1 = route[:, R_POS1:R_POS1 + 1].astype(jnp.int32)
        srow = lax.broadcasted_iota(jnp.int32, (tm, r_l), 1)
        gates = jnp.where(srow == l0, route[:, R_G0:R_G0 + 1],
                          jnp.where(srow == l1, route[:, R_G1:R_G1 + 1], 0.0)).astype(BF16)
        f = jnp.dot(gates, y, preferred_element_type=F32)
        o_ref[...] = _rms(h_ref[...] + f, gn_ref[...])

    _prefetched(gather, pl.program_id(0), pl.num_programs(0), body)


def _combine(tile_src, h, route, norm_g, ybuf):
    t, d = h.shape
    tm = min(ROW_TILE, t)
    r_l = _local_rows(tm)
    return pl.pallas_call(
        _combine_kernel,
        grid_spec=pltpu.PrefetchScalarGridSpec(
            num_scalar_prefetch=1,
            grid=(t // tm,),
            in_specs=[pl.BlockSpec((tm, d), lambda i, src: (i, 0)),
                      pl.BlockSpec((tm, LANES), lambda i, src: (i, 0)),
                      pl.BlockSpec((1, d), lambda i, src: (0, 0)),
                      pl.BlockSpec(memory_space=pl.ANY)],
            out_specs=pl.BlockSpec((tm, d), lambda i, src: (i, 0)),
            scratch_shapes=[pltpu.VMEM((2, r_l, d), BF16),
                            pltpu.SemaphoreType.DMA((2,))],
        ),
        out_shape=jax.ShapeDtypeStruct((t, d), F32),
        compiler_params=pltpu.CompilerParams(
            dimension_semantics=("arbitrary",), vmem_limit_bytes=VMEM_LIMIT),
        name="combine",
    )(tile_src, h, route, norm_g, ybuf)


def kernel(x_prompt, x_sample, cache_k, cache_v, state_conv, norm_mix, w_in, conv_w, norm_out_attn,
           norm_out_conv, w_out, norm_ffn, w_router_group, b_router_group, w_router_expert,
           b_router_expert, w_gate, w_up, w_down, norm_final):
    n_seq, seq_len, d = x_prompt.shape
    db, ds, _ = x_sample.shape
    depth = w_in.shape[0]
    _, _, w_buf, n_heads, dh = cache_k.shape
    d_attn = n_heads * dh
    d_conv = d - d_attn
    assert depth == 1 and ds == 1 and dh == HEAD_DIM
    assert seq_len % (max(DILATIONS) * WIN_KEYS) == 0 and seq_len <= max(DILATIONS) * WIN_KEYS
    layer = 0
    tp, ts = n_seq * seq_len, db

    xp = x_prompt.reshape(tp, d)
    xs = x_sample.reshape(ts, d)
    row = lambda vec: vec.reshape(1, -1)
    w_in_b = w_in[layer].astype(BF16)
    w_out_b = w_out[layer].astype(BF16)
    g_mix, g_oa, g_oc, g_ffn = (row(norm_mix[layer]), row(norm_out_attn[layer]),
                                row(norm_out_conv[layer]), row(norm_ffn[layer]))
    st0, st1 = state_conv[layer, :, 0, :], state_conv[layer, :, 1, :]

    qp, kp, vp, kp_t, vp_t, ocp, conv_p = _mix_in_prompt(
        xp, g_mix, w_in_b, conv_w[layer], g_oc, seq_len=seq_len, d_attn=d_attn, d_conv=d_conv)
    qs, ks, vs, ocs, us = _mix_in_sample(
        xs, g_mix, w_in_b, conv_w[layer], g_oc, st0, st1, d_attn=d_attn, d_conv=d_conv)

    heads = lambda a: a.reshape(ts, 1, d_attn)
    positions_last = lambda c: jnp.transpose(c, (0, 2, 3, 1))
    attn_p, attn_s = _attention(qp, kp, vp, heads(qs), heads(ks), heads(vs),
                                positions_last(cache_k[layer]), positions_last(cache_v[layer]),
                                n_seq=n_seq, seq_len=seq_len)
    attn_s = attn_s.reshape(ts, d_attn)

    n_route = N_GROUPS + N_EXPERTS
    w_router = jnp.zeros((d, LANES), F32).at[:, :N_GROUPS].set(w_router_group[layer])
    w_router = w_router.at[:, N_GROUPS:n_route].set(w_router_expert[layer])
    b_router = jnp.zeros((1, LANES), F32).at[0, :N_GROUPS].set(b_router_group[layer])
    b_router = b_router.at[0, N_GROUPS:n_route].set(b_router_expert[layer])
    w_router_hi = w_router.astype(BF16)
    w_router_lo = (w_router - w_router_hi.astype(F32)).astype(BF16)
    mix_out = functools.partial(_mix_out, norm_ga=g_oa, w_out_bf16=w_out_b, norm_gf=g_ffn,
                                w_router=jnp.concatenate([w_router_hi, w_router_lo], axis=1),
                                b_router=b_router)
    h_s, route_s, xs_s, cnt_s = mix_out(xs, attn_s, ocs)
    assert cnt_s.shape[0] == 1
    h_p, route_p, xs_all, cnt_p = mix_out(xp, attn_p, ocp, tail=xs_s)

    tile_chunks = jnp.concatenate([cnt_p[:, 0, ROUTER_LANE0:n_route],
                                   cnt_s[:, 0, ROUTER_LANE0:n_route]], axis=0).astype(jnp.int32)
    ntp, nts = cnt_p.shape[0], cnt_s.shape[0]
    tm_p, tm_s = tp // ntp, ts // nts
    rl_p, rl_s = _local_rows(tm_p), _local_rows(tm_s)
    tile_row0 = jnp.arange(ntp + nts, dtype=jnp.int32) * rl_p
    total_chunks = ntp * _max_tile_chunks(tm_p) + nts * _max_tile_chunks(tm_s)
    n_blocks = -(-(total_chunks + N_EXPERTS * (CHUNKS_PER_BLOCK - 1)) // CHUNKS_PER_BLOCK)
    block_e, n_used, src_row, tile_src, next_e, block_rows = _sorted_layout(
        tile_chunks, tile_row0, rl_p // CHUNK, n_blocks)
    ybuf = _experts(block_e, n_used, src_row, next_e, block_rows, xs_all, w_gate[layer],
                    w_up[layer], w_down[layer])
    g_fin = row(norm_final)
    y_p = _combine(tile_src[:ntp].reshape(-1), h_p, route_p, g_fin, ybuf)
    y_s = _combine(tile_src[ntp:, :rl_s // CHUNK].reshape(-1), h_s, route_s, g_fin, ybuf)

    w_keep = min(max(DILATIONS) * WIN_KEYS, seq_len)
    kv5 = lambda a_t: jnp.transpose(a_t.reshape(n_seq, n_heads, dh, seq_len),
                                    (0, 3, 1, 2))[None, :, seq_len - w_keep:]
    conv_s = jnp.stack([st1, us], axis=1)[None]
    kvs = lambda a: a.reshape(1, ts, 1, n_heads, dh)
    return (y_p.reshape(n_seq, seq_len, d), y_s.reshape(db, ds, d), kv5(kp_t), kv5(vp_t),
            conv_p[None], kvs(ks), kvs(vs), conv_s)
```

```python
import functools

import jax
import jax.numpy as jnp
from jax import lax
from jax.experimental import pallas as pl
from jax.experimental.pallas import tpu as pltpu

HEAD_DIM = 64
WIN_KEYS = 128
DILATIONS = (1, 4, 16)
CONV_WIDTH = 3
N_GROUPS = 4
EXPERTS_PER_GROUP = 8
N_EXPERTS = N_GROUPS * EXPERTS_PER_GROUP
EPS = 1e-6
NEG = -1e30
LOG2_E = 1.4426950408889634

LANES = 128
ROW_TILE = 512
MIX_IN_TILE = 1024
EXPERT_BLOCK = 512
EXPERT_ROW_STEP = 128
ATTN_LAG = 3
ATTN_UNROLL = 16
VMEM_LIMIT = 56 * 1024 * 1024

F32 = jnp.float32
BF16 = jnp.bfloat16


def _rms(x, g):
    return x * lax.rsqrt(jnp.mean(x * x, axis=-1, keepdims=True) + EPS) * g


def _mix_in_kernel(*refs, d_attn, d_conv, sequential):
    if sequential:
        (x_ref, g_ref, w_ref, cw_ref, gc_ref,
         q_ref, k_ref, v_ref, kt_ref, vt_ref, oc_ref, st_ref, carry_ref) = refs
    else:
        (x_ref, g_ref, w_ref, cw_ref, gc_ref, st0_ref, st1_ref,
         q_ref, k_ref, v_ref, oc_ref, u_ref) = refs
    x = x_ref[...]
    xb = _rms(x, g_ref[...]).astype(BF16)

    def proj(lo, width):
        return jnp.dot(xb, w_ref[:, lo:lo + width], preferred_element_type=F32)

    q_ref[...] = proj(0, d_attn)
    k = proj(d_attn, d_attn)
    v = proj(2 * d_attn, d_attn)
    k_ref[...] = k
    v_ref[...] = v
    gate = proj(3 * d_attn, d_conv)
    u = proj(3 * d_attn + d_conv, d_conv) * proj(3 * d_attn + 2 * d_conv, d_conv)

    tm = x.shape[0]
    if sequential:
        kt_ref[...] = k.T
        vt_ref[...] = v.T

        @pl.when(pl.program_id(1) == 0)
        def _():
            carry_ref[...] = jnp.zeros_like(carry_ref)

        row = lax.broadcasted_iota(jnp.int32, u.shape, 0)
        prev1 = carry_ref[1:2, :]
        prev2 = carry_ref[0:1, :]
        u1 = jnp.where(row == 0, prev1, pltpu.roll(u, 1, axis=0))
        u2 = jnp.where(row == 0, prev2, jnp.where(row == 1, prev1, pltpu.roll(u, 2, axis=0)))
        carry_ref[0:2, :] = u[tm - 2:tm, :]
        st_ref[...] = u[tm - 2:tm, :]
    else:
        u_ref[...] = u
        u2 = st0_ref[...]
        u1 = st1_ref[...]
    z = u2 * cw_ref[0:1, :] + u1 * cw_ref[1:2, :] + u * cw_ref[2:3, :]
    oc_ref[...] = _rms(gate * z, gc_ref[...])


def _mix_in_call(kernel, grid, in_specs, out_specs, out_shape, scratch, args):
    return pl.pallas_call(
        kernel, grid=grid, in_specs=in_specs, out_specs=out_specs, out_shape=out_shape,
        scratch_shapes=scratch,
        compiler_params=pltpu.CompilerParams(
            dimension_semantics=("arbitrary",) * len(grid), vmem_limit_bytes=VMEM_LIMIT),
        name="mix_in",
    )(*args)


def _mix_in_prompt(x2d, norm_g, w_in_bf16, conv_w, norm_gc, *, seq_len, d_attn, d_conv):
    t, d = x2d.shape
    tm = min(MIX_IN_TILE, seq_len)
    n_seq, per = t // seq_len, seq_len // tm
    const = lambda b, s: (0, 0)
    row = lambda width: pl.BlockSpec((tm, width), lambda b, s: (b * per + s, 0))
    col = pl.BlockSpec((None, d_attn, tm), lambda b, s: (b, 0, s))
    f32 = lambda *shape: jax.ShapeDtypeStruct(shape, F32)
    return _mix_in_call(
        functools.partial(_mix_in_kernel, d_attn=d_attn, d_conv=d_conv, sequential=True),
        (n_seq, per),
        [row(d), pl.BlockSpec((1, d), const),
         pl.BlockSpec(w_in_bf16.shape, const, pipeline_mode=pl.Buffered(1)),
         pl.BlockSpec((CONV_WIDTH, d_conv), const), pl.BlockSpec((1, d_conv), const)],
        [row(d_attn)] * 3 + [col] * 2 + [row(d_conv),
                                         pl.BlockSpec((None, CONV_WIDTH - 1, d_conv),
                                                      lambda b, s: (b, 0, 0))],
        [f32(t, d_attn)] * 3 + [f32(n_seq, d_attn, seq_len)] * 2
        + [f32(t, d_conv), f32(n_seq, CONV_WIDTH - 1, d_conv)],
        [pltpu.VMEM((8, d_conv), F32)],
        (x2d, norm_g, w_in_bf16, conv_w, norm_gc))


def _mix_in_sample(x2d, norm_g, w_in_bf16, conv_w, norm_gc, st0, st1, *, d_attn, d_conv):
    t, d = x2d.shape
    full = lambda arr: pl.BlockSpec(arr.shape, lambda i: (0,) * arr.ndim)
    f32 = lambda *shape: jax.ShapeDtypeStruct(shape, F32)
    args = (x2d, norm_g, w_in_bf16, conv_w, norm_gc, st0, st1)
    outs = [f32(t, d_attn)] * 3 + [f32(t, d_conv)] * 2
    return _mix_in_call(
        functools.partial(_mix_in_kernel, d_attn=d_attn, d_conv=d_conv, sequential=False),
        (1,), [full(a) for a in args], [full(o) for o in outs], outs, [], args)


def _attn_prompt_kernel(q_ref, k_ref, v_ref, o_ref, m_s, l_s, a_s, *, seq_len):
    w = WIN_KEYS
    scale = HEAD_DIM ** -0.5 * LOG2_E
    r_i = lax.broadcasted_iota(jnp.int32, (2 * w, 2 * w), 0) & (w - 1)
    c_i = lax.broadcasted_iota(jnp.int32, (2 * w, 2 * w), 1)
    mask_cur = (lax.broadcasted_iota(jnp.int32, (2 * w, w), 1)
                <= lax.broadcasted_iota(jnp.int32, (2 * w, w), 0) & (w - 1))
    mask_both = jnp.logical_and(c_i >= r_i, c_i - w <= r_i)
    first_head = lax.broadcasted_iota(jnp.int32, (w, 2 * HEAD_DIM), 1) < HEAD_DIM
    dn_t = (((1,), (1,)), ((), ()))

    def rows(start, dil):
        if dil > 1:
            return pl.ds(start, w, stride=dil)
        return pl.ds(start if isinstance(start, int) else pl.multiple_of(start, w), w)

    def run_branch(dil, first, last):
        span = dil * w
        nb = seq_len // span

        def blocks(its, with_prev):
            mask = mask_both if with_prev else mask_cur

            def issue_scores(it):
                g = it % dil
                n = it // dil
                c = rows(g + n * span, dil)
                qb = (q_ref[c, :] * scale).astype(BF16)
                zero = jnp.zeros_like(qb)
                q = jnp.concatenate([jnp.where(first_head, qb, zero),
                                     jnp.where(first_head, zero, qb)], axis=0)
                k = k_ref[c, :].astype(BF16)
                v = v_ref[c, :].astype(BF16)
                if with_prev:
                    p = rows(g + (n - 1) * span, dil)
                    k = jnp.concatenate([k_ref[p, :].astype(BF16), k], axis=0)
                    v = jnp.concatenate([v_ref[p, :].astype(BF16), v], axis=0)
                return c, lax.dot_general(q, k, dn_t, preferred_element_type=F32), v

            def finish(c, s, v):
                s = jnp.where(mask, s, NEG)
                m = jnp.max(s, axis=-1, keepdims=True)
                p = jnp.exp2(s - m).astype(BF16)
                ones = jnp.ones((v.shape[0], 2 * HEAD_DIM), BF16)
                acc_l = jnp.dot(p, jnp.concatenate([v, ones], axis=1), preferred_element_type=F32)
                acc, l = acc_l[:, :2 * HEAD_DIM], acc_l[:, 2 * HEAD_DIM:]
                m_b = jnp.where(first_head, m[:w], m[w:])
                l_b = jnp.where(first_head, l[:w], l[w:])
                a_b = jnp.where(first_head, acc[:w], acc[w:])
                if not first:
                    m_o = m_s[c, :]
                    m_n = jnp.maximum(m_o, m_b)
                    w_o = jnp.exp2(m_o - m_n)
                    w_b = jnp.exp2(m_b - m_n)
                    l_b = w_o * l_s[c, :] + w_b * l_b
                    a_b = w_o * a_s[c, :] + w_b * a_b
                    m_b = m_n
                if last:
                    o_ref[c, :] = a_b / l_b
                else:
                    m_s[c, :] = m_b
                    l_s[c, :] = l_b
                    a_s[c, :] = a_b

            in_flight = []
            for i in range(len(its) + ATTN_LAG):
                if i < len(its):
                    in_flight.append(issue_scores(its[i]))
                if i >= ATTN_LAG:
                    finish(*in_flight.pop(0))

        def run(lo, hi, with_prev):
            u = ATTN_UNROLL
            trips = (hi - lo) // u

            def body(t, carry):
                blocks([lo + t * u + j for j in range(u)], with_prev)
                return carry

            if trips:
                lax.fori_loop(0, trips, body, 0)
            if lo + trips * u < hi:
                blocks(list(range(lo + trips * u, hi)), with_prev)

        run(0, dil, False)
        run(dil, dil * nb, True)

    order = sorted(DILATIONS, reverse=True)
    for i, dil in enumerate(order):
        run_branch(dil, i == 0, i == len(order) - 1)


def _attn_sample_kernel(q_ref, kn_ref, vn_ref, kt_ref, vt_ref, o_ref):
    n_heads, dh, w_buf = kt_ref.shape
    delta = w_buf - lax.broadcasted_iota(jnp.int32, (1, w_buf), 1)
    cnt = jnp.zeros((1, w_buf), F32)
    for dil in DILATIONS:
        assert dil & (dil - 1) == 0
        member = jnp.where(delta <= dil * WIN_KEYS, 1.0, 0.0)
        cnt = cnt + jnp.where((delta & (dil - 1)) == 0, member, 0.0)
    eye = (lax.broadcasted_iota(jnp.int32, (dh, dh), 0)
           == lax.broadcasted_iota(jnp.int32, (dh, dh), 1))
    to_col = lambda r: jnp.sum(jnp.where(eye, r, 0.0), axis=1, keepdims=True)
    to_row = lambda c: jnp.sum(jnp.where(eye, c, 0.0), axis=0, keepdims=True)
    outs = []
    for h in range(n_heads):
        sl = slice(h * dh, (h + 1) * dh)
        q = q_ref[:, sl] * (HEAD_DIM ** -0.5)
        s_self = jnp.sum(q * kn_ref[:, sl], axis=1, keepdims=True)
        s = jnp.sum(to_col(q) * kt_ref[h], axis=0, keepdims=True)
        s = jnp.where(cnt > 0.0, s, NEG)
        m = jnp.maximum(jnp.max(s, axis=1, keepdims=True), s_self)
        p = cnt * jnp.exp(s - m)
        p_self = len(DILATIONS) * jnp.exp(s_self - m)
        l = jnp.sum(p, axis=1, keepdims=True) + p_self
        acc = jnp.sum(p * vt_ref[h], axis=1, keepdims=True)
        outs.append((to_row(acc) + p_self * vn_ref[:, sl]) / l)
    o_ref[...] = jnp.concatenate(outs, axis=1)


def _attn_kernel(q_ref, k_ref, v_ref, qs_ref, kn_ref, vn_ref, kt_ref, vt_ref, o_ref, os_ref,
                 m_s, l_s, a_s, *, seq_len):
    _attn_sample_kernel(qs_ref, kn_ref, vn_ref, kt_ref, vt_ref, os_ref)
    _attn_prompt_kernel(q_ref, k_ref, v_ref, o_ref, m_s, l_s, a_s, seq_len=seq_len)


def _attention(q, k, v, qs, k_new, v_new, cache_kt, cache_vt, *, n_seq, seq_len):
    t, d_attn = q.shape
    db, n_heads, dh, w_buf = cache_kt.shape
    pair = 2 * HEAD_DIM
    pairs = d_attn // pair
    assert db == n_seq * pairs, "one sample sequence per prompt grid step"
    spec = pl.BlockSpec((seq_len, pair), lambda b, h: (b, h))
    head_spec = pl.BlockSpec((None, 1, d_attn), lambda b, h: (b * pairs + h, 0, 0))
    cache_spec = pl.BlockSpec((None, n_heads, dh, w_buf), lambda b, h: (b * pairs + h, 0, 0, 0))
    return pl.pallas_call(
        functools.partial(_attn_kernel, seq_len=seq_len),
        grid=(n_seq, pairs),
        in_specs=[spec] * 3 + [head_spec] * 3 + [cache_spec] * 2,
        out_specs=[spec, head_spec],
        out_shape=[jax.ShapeDtypeStruct((t, d_attn), F32),
                   jax.ShapeDtypeStruct((db, 1, d_attn), F32)],
        scratch_shapes=[pltpu.VMEM((seq_len, pair), F32)] * 3,
        compiler_params=pltpu.CompilerParams(
            dimension_semantics=("arbitrary", "arbitrary"), vmem_limit_bytes=VMEM_LIMIT),
        name="attention",
    )(q, k, v, qs, k_new, v_new, cache_kt, cache_vt)


R_E0, R_E1, R_G0, R_G1, R_POS0, R_POS1 = range(6)
ROUTER_LANE0 = N_GROUPS
CHUNK = 16
CHUNKS_PER_BLOCK = EXPERT_BLOCK // CHUNK


def _max_tile_chunks(tm):
    return (2 * tm + (CHUNK - 1) * N_EXPERTS) // CHUNK


def _local_rows(tm):
    return 2 * tm + N_EXPERTS * CHUNK


def _mix_out_kernel(*refs, n_tiles, has_tail):
    if has_tail:
        *tile_in, tail_ref, h_ref, route_ref, xs_ref, cnt_ref, tok_s, pos_s = refs
    else:
        *tile_in, h_ref, route_ref, xs_ref, cnt_ref, tok_s, pos_s = refs
    i = pl.program_id(0)
    route = lambda slot, *between: _route_tile(*tile_in, h_ref, route_ref, cnt_ref,
                                               tok_s.at[slot], pos_s.at[slot], *between)
    sort = lambda slot: _sort_tile(tok_s.at[slot], pos_s.at[slot], xs_ref)

    @pl.when(i == 0)
    def _():
        route(0)

    for parity in range(2):
        @pl.when(jnp.logical_and(jnp.logical_and(i >= 1, i < n_tiles), i % 2 == parity))
        def _():
            route(parity, lambda: sort(1 - parity))

    @pl.when(i == n_tiles)
    def _():
        sort((n_tiles - 1) % 2)

    if has_tail:
        @pl.when(i == n_tiles + 1)
        def _():
            rows = tail_ref.shape[0]
            xs_ref[0:rows, :] = tail_ref[...]
            xs_ref[rows:, :] = jnp.zeros((xs_ref.shape[0] - rows, xs_ref.shape[1]), xs_ref.dtype)


def _sort_tile(tok_ref, pos_ref, xs_ref):
    tm = tok_ref.shape[0]
    l1 = pos_ref[R_POS0:R_POS0 + 1, :].astype(jnp.int32)
    l2 = pos_ref[R_POS1:R_POS1 + 1, :].astype(jnp.int32)
    srow = lax.broadcasted_iota(jnp.int32, (xs_ref.shape[0], tm), 0)
    perm = jnp.where(srow == l1, 1.0, jnp.where(srow == l2, 1.0, 0.0)).astype(BF16)
    xs_ref[...] = jnp.dot(perm, tok_ref[...], preferred_element_type=F32).astype(BF16)


def _route_tile(x_ref, a_ref, oc_ref, ga_ref, wo_ref, gf_ref, wr_ref, br_ref,
                h_ref, route_ref, cnt_ref, tok_ref, pos_ref, after_projections=lambda: None):
    d_attn = a_ref.shape[1]
    tm, d = x_ref.shape
    a = _rms(a_ref[...], ga_ref[...]).astype(BF16)
    mix = jnp.dot(a, wo_ref[0:d_attn, :], preferred_element_type=F32)
    mix = mix + jnp.dot(oc_ref[...].astype(BF16), wo_ref[d_attn:, :], preferred_element_type=F32)
    h = x_ref[...] + mix
    h_ref[...] = h
    tok = _rms(h, gf_ref[...])

    tok_hi = tok.astype(BF16)
    tok_lo = (tok - tok_hi.astype(F32)).astype(BF16)
    hi_part = jnp.dot(tok_hi, wr_ref[...], preferred_element_type=F32)
    lo_part = jnp.dot(tok_lo, wr_ref[:, :LANES], preferred_element_type=F32)
    logits = hi_part[:, :LANES] + hi_part[:, LANES:] + lo_part + br_ref[...]
    after_projections()
    lane = lax.broadcasted_iota(jnp.int32, logits.shape, 1)
    big = jnp.int32(LANES)
    neg_inf = jnp.float32(-jnp.inf)

    def top1(vals):
        best = jnp.max(vals, axis=-1, keepdims=True)
        idx = jnp.min(jnp.where(vals == best, lane, big), axis=-1, keepdims=True)
        return best, idx

    is_group = lane < N_GROUPS
    lg = jnp.where(is_group, logits, neg_inf)
    mg, g_sel = top1(lg)
    p_group = 1.0 / jnp.sum(jnp.where(is_group, jnp.exp(lg - mg), 0.0), axis=-1, keepdims=True)

    lo = ROUTER_LANE0 + g_sel * EXPERTS_PER_GROUP
    in_group = jnp.logical_and(lane >= lo, lane < lo + EXPERTS_PER_GROUP)
    le = jnp.where(in_group, logits, neg_inf)
    v1, i1 = top1(le)
    v2, i2 = top1(jnp.where(lane == i1, neg_inf, le))
    e2 = jnp.exp(v2 - v1)
    gate1 = p_group / (1.0 + e2)
    gate2 = p_group * e2 / (1.0 + e2)

    oh1 = lane == i1
    oh2 = lane == i2
    both = jnp.where(jnp.logical_or(oh1, oh2), 1.0, 0.0)
    r_i = lax.broadcasted_iota(jnp.int32, (tm, tm), 0)
    c_i = lax.broadcasted_iota(jnp.int32, (tm, tm), 1)
    strict_lower = jnp.where(c_i < r_i, 1.0, 0.0).astype(BF16)
    before = jnp.dot(strict_lower, both.astype(BF16), preferred_element_type=F32)
    chunks = jnp.floor((jnp.sum(both, axis=0, keepdims=True) + (CHUNK - 1)) * (1.0 / CHUNK))
    u_r = lax.broadcasted_iota(jnp.int32, (LANES, LANES), 0)
    u_c = lax.broadcasted_iota(jnp.int32, (LANES, LANES), 1)
    strict_upper = jnp.where(u_r < u_c, 1.0, 0.0).astype(BF16)
    chunks8 = jnp.broadcast_to(chunks, (8, LANES))
    first_row = CHUNK * jnp.dot(chunks8.astype(BF16), strict_upper,
                                preferred_element_type=F32)[0:1, :]
    pos = first_row + before
    pos1 = jnp.sum(jnp.where(oh1, pos, 0.0), axis=-1, keepdims=True)
    pos2 = jnp.sum(jnp.where(oh2, pos, 0.0), axis=-1, keepdims=True)
    cnt_ref[...] = jnp.where(lax.broadcasted_iota(jnp.int32, (8, LANES), 0) == 0, chunks8, 0.0)

    rec = jnp.zeros(logits.shape, F32)
    for col, val in ((R_E0, (i1 - ROUTER_LANE0).astype(F32)), (R_E1, (i2 - ROUTER_LANE0).astype(F32)),
                     (R_G0, gate1), (R_G1, gate2), (R_POS0, pos1), (R_POS1, pos2)):
        rec = jnp.where(lane == col, val, rec)
    route_ref[...] = rec
    tok_ref[...] = tok_hi
    pos_ref[...] = rec.T[0:pos_ref.shape[0], :]


def _mix_out(x2d, attn, oconv, norm_ga, w_out_bf16, norm_gf, w_router, b_router, tail=None):
    t, d = x2d.shape
    d_attn, d_conv = attn.shape[1], oconv.shape[1]
    tm = min(ROW_TILE, t)
    nt = t // tm
    r_l = _local_rows(tm)
    has_tail = tail is not None
    tile = lambda i: jnp.minimum(i, nt - 1)
    sorted_block = lambda i: jnp.minimum(jnp.maximum(i - 1, 0), nt - 1 + has_tail)
    row = lambda width: pl.BlockSpec((tm, width), lambda i: (tile(i), 0))
    full = lambda arr: pl.BlockSpec(arr.shape, lambda i: (0, 0))
    args = [x2d, attn, oconv, norm_ga, w_out_bf16, norm_gf, w_router, b_router]
    in_specs = [row(d), row(d_attn), row(d_conv)] + [full(a) for a in args[3:]]
    if has_tail:
        assert tail.shape[0] <= r_l and tail.shape[1] == d
        args.append(tail)
        in_specs.append(full(tail))
    return pl.pallas_call(
        functools.partial(_mix_out_kernel, n_tiles=nt, has_tail=has_tail),
        grid=(nt + 1 + has_tail,),
        in_specs=in_specs,
        out_specs=[row(d), row(LANES), pl.BlockSpec((r_l, d), lambda i: (sorted_block(i), 0)),
                   pl.BlockSpec((None, 8, LANES), lambda i: (tile(i), 0, 0))],
        out_shape=[jax.ShapeDtypeStruct((t, d), F32), jax.ShapeDtypeStruct((t, LANES), F32),
                   jax.ShapeDtypeStruct(((nt + has_tail) * r_l, d), BF16),
                   jax.ShapeDtypeStruct((nt, 8, LANES), F32)],
        scratch_shapes=[pltpu.VMEM((2, tm, d), BF16), pltpu.VMEM((2, 8, tm), F32)],
        compiler_params=pltpu.CompilerParams(
            dimension_semantics=("arbitrary",), vmem_limit_bytes=VMEM_LIMIT),
        name="mix_out",
    )(*args)


def _sorted_layout(tile_chunks, tile_row0, max_local, n_blocks):
    nt, n_exp = tile_chunks.shape
    cpb = CHUNKS_PER_BLOCK
    i32 = jnp.int32
    seg = jnp.sum(tile_chunks, axis=0)
    padded = (seg + cpb - 1) // cpb * cpb
    pend = jnp.cumsum(padded)
    pstart = pend - padded
    tile_incl = jnp.cumsum(tile_chunks, axis=0)
    tile_excl = tile_incl - tile_chunks
    local_incl = jnp.cumsum(tile_chunks, axis=1)
    local_excl = local_incl - tile_chunks
    base = pstart[None, :] + tile_excl

    block_first = jnp.arange(n_blocks, dtype=i32) * cpb
    block_e = jnp.minimum(jnp.sum((pend[None, :] <= block_first[:, None]).astype(i32), axis=1),
                          n_exp - 1)
    n_used = (pend[-1:] // cpb).astype(i32)

    onehot_pick = lambda onehot, table: jnp.sum(jnp.where(onehot, table, 0), axis=-1)

    is_e = block_e[:, None] == jnp.arange(n_exp, dtype=i32)[None, :]
    of_expert = lambda table_te: onehot_pick(is_e[:, None, :], table_te[None, :, :])
    incl_b, cnt_b, lexcl_b = of_expert(tile_incl), of_expert(tile_chunks), of_expert(local_excl)
    q = (block_first - onehot_pick(is_e, pstart[None, :]))[:, None] + jnp.arange(cpb, dtype=i32)
    tile_q = jnp.minimum(jnp.sum((incl_b[:, None, :] <= q[:, :, None]).astype(i32), axis=2), nt - 1)
    is_t = tile_q[:, :, None] == jnp.arange(nt, dtype=i32)[None, None, :]
    of_tile = lambda table_bt: onehot_pick(is_t, table_bt[:, None, :])
    local_chunk = of_tile(lexcl_b) + q - of_tile(incl_b - cnt_b)
    seg_b = onehot_pick(is_e, seg[None, :])
    block_rows = (CHUNK * jnp.clip(seg_b - q[:, 0], 0, cpb)).astype(i32)
    in_run = jnp.logical_and(q >= 0, q < seg_b[:, None])
    src_row = jnp.where(in_run, of_tile(tile_row0[None, :]) + CHUNK * local_chunk, 0)
    src_row = src_row.reshape(-1).astype(i32)

    c = jnp.arange(max_local, dtype=i32)
    e_c = jnp.minimum(jnp.sum((local_incl[:, None, :] <= c[None, :, None]).astype(i32), axis=2),
                      n_exp - 1)
    is_ec = e_c[:, :, None] == jnp.arange(n_exp, dtype=i32)[None, None, :]
    of_run = lambda table_te: onehot_pick(is_ec, table_te[:, None, :])
    global_chunk = of_run(base) + c[None, :] - of_run(local_excl)
    tile_src = jnp.where(c[None, :] < local_incl[:, -1:], CHUNK * global_chunk, 0).astype(i32)
    e_ids = jnp.arange(n_exp, dtype=i32)
    later = jnp.logical_and(seg[None, :] > 0, e_ids[None, :] > e_ids[:, None])
    next_e = jnp.min(jnp.where(later, e_ids[None, :], n_exp), axis=1)
    next_e = jnp.where(next_e < n_exp, next_e, -1).astype(i32)
    return block_e.astype(i32), n_used, src_row, tile_src, next_e, block_rows


def _chunk_gather(src_ref, hbm_ref, buf, sems, item, slot, n_chunks, *, wait):
    for c in range(n_chunks):
        row = 0 if wait else pl.multiple_of(src_ref[item * n_chunks + c], CHUNK)
        copy = pltpu.make_async_copy(hbm_ref.at[pl.ds(row, CHUNK)],
                                     buf.at[slot, pl.ds(c * CHUNK, CHUNK)], sems.at[slot])
        if wait:
            copy.wait()
        else:
            copy.start()


def _prefetched(gather, step, n_items, body):
    slot = step % 2

    @pl.when(jnp.logical_and(step == 0, n_items > 0))
    def _():
        gather(0, 0, wait=False)

    @pl.when(step + 1 < n_items)
    def _():
        gather(step + 1, 1 - slot, wait=False)

    body(slot, lambda: gather(step, slot, wait=True))


def _experts_kernel(block_e_ref, n_used_ref, src_ref, next_e_ref, block_rows_ref, xs_ref, wg_hbm,
                    wu_hbm, wd_hbm, y_hbm, xblk, sems, ybuf, ysems, wg_f, wu_f, wd_f, wsems, wg_b, wu_b, wd_b,
                    run_ref):
    rows = EXPERT_BLOCK
    n_blocks = y_hbm.shape[0] // rows
    n_used = n_used_ref[0]
    gather = functools.partial(_chunk_gather, src_ref, xs_ref, xblk, sems,
                               n_chunks=CHUNKS_PER_BLOCK)

    def weight_copies(expert, slot):
        return [pltpu.make_async_copy(hbm.at[expert], stage.at[slot], wsems.at[slot])
                for hbm, stage in ((wg_hbm, wg_f), (wu_hbm, wu_f), (wd_hbm, wd_f))]

    def y_copy(blk, slot):
        start = blk * rows if isinstance(blk, int) else pl.multiple_of(blk * rows, rows)
        return pltpu.make_async_copy(ybuf.at[slot], y_hbm.at[pl.ds(start, rows)], ysems.at[slot])

    def block(b, carry):
        e = block_e_ref[b]
        new_expert = jnp.logical_or(b == 0, e != block_e_ref[jnp.maximum(b - 1, 0)])

        @pl.when(jnp.logical_and(new_expert, b < n_used))
        def _():
            @pl.when(b == 0)
            def _():
                run_ref[0] = 0
                for copy in weight_copies(e, 0):
                    copy.start()

            @pl.when(b > 0)
            def _():
                run_ref[0] = run_ref[0] + 1

            slot = run_ref[0] % 2
            nxt = next_e_ref[e]

            @pl.when(nxt >= 0)
            def _():
                for copy in weight_copies(nxt, 1 - slot):
                    copy.start()

            for copy in weight_copies(e, slot):
                copy.wait()
            wg_b[...] = wg_f[slot].astype(BF16)
            wu_b[...] = wu_f[slot].astype(BF16)
            wd_b[...] = wd_f[slot].astype(BF16)

        def body(slot, wait_current):
            @pl.when(b >= 2)
            def _():
                y_copy(b - 2, slot).wait()

            @pl.when(b < n_used)
            def _():
                wait_current()

            valid = block_rows_ref[b]
            for m in range(EXPERT_ROW_STEP, rows + 1, EXPERT_ROW_STEP):
                @pl.when(jnp.logical_and(valid > m - EXPERT_ROW_STEP, valid <= m))
                def _():
                    x = xblk[slot, 0:m, :]
                    gate = jnp.dot(x, wg_b[...], preferred_element_type=F32)
                    up = jnp.dot(x, wu_b[...], preferred_element_type=F32)
                    hid = gate * (1.0 / (1.0 + jnp.exp(-gate))) * up
                    ybuf[slot, 0:m, :] = jnp.dot(hid.astype(BF16), wd_b[...],
                                                 preferred_element_type=F32).astype(BF16)
                    if m < rows:
                        ybuf[slot, m:rows, :] = jnp.zeros((rows - m, ybuf.shape[2]), ybuf.dtype)

            @pl.when(valid == 0)
            def _():
                ybuf[slot] = jnp.zeros(ybuf.shape[1:], ybuf.dtype)

            y_copy(b, slot).start()

        _prefetched(gather, b, n_used, body)
        return carry

    lax.fori_loop(0, n_blocks, block, 0)
    for blk in range(max(n_blocks - 2, 0), n_blocks):
        y_copy(blk, blk % 2).wait()


def _experts(block_e, n_used, src_row, next_e, block_rows, xs, w_gate, w_up, w_down):
    n_blocks = block_e.shape[0]
    _, d, d_exp = w_gate.shape
    blk = EXPERT_BLOCK
    any_spec = pl.BlockSpec(memory_space=pl.ANY)
    return pl.pallas_call(
        _experts_kernel,
        grid_spec=pltpu.PrefetchScalarGridSpec(
            num_scalar_prefetch=5,
            grid=(1,),
            in_specs=[any_spec] * 4,
            out_specs=any_spec,
            scratch_shapes=[pltpu.VMEM((2, blk, d), BF16), pltpu.SemaphoreType.DMA((2,)),
                            pltpu.VMEM((2, blk, d), BF16), pltpu.SemaphoreType.DMA((2,)),
                            pltpu.VMEM((2, d, d_exp), F32), pltpu.VMEM((2, d, d_exp), F32),
                            pltpu.VMEM((2, d_exp, d), F32), pltpu.SemaphoreType.DMA((2,)),
                            pltpu.VMEM((d, d_exp), BF16), pltpu.VMEM((d, d_exp), BF16),
                            pltpu.VMEM((d_exp, d), BF16), pltpu.SMEM((1,), jnp.int32)],
        ),
        out_shape=jax.ShapeDtypeStruct((n_blocks * blk, d), BF16),
        compiler_params=pltpu.CompilerParams(
            dimension_semantics=("arbitrary",), vmem_limit_bytes=VMEM_LIMIT),
        name="experts",
    )(block_e, n_used, src_row, next_e, block_rows, xs, w_gate, w_up, w_down)


def _combine_kernel(src_ref, h_ref, route_ref, gn_ref, ybuf_ref, o_ref, yloc, sems):
    tm = h_ref.shape[0]
    r_l = yloc.shape[1]
    gather = functools.partial(_chunk_gather, src_ref, ybuf_ref, yloc, sems,
                               n_chunks=r_l // CHUNK)

    def body(slot, wait_current):
        wait_current()
        y = yloc[slot]
        route = route_ref[...]
        l0 = route[:, R_POS0:R_POS0 + 1].astype(jnp.int32)
        l1 = route[:, R_POS1:R_POS1 + 1].astype(jnp.int32)
        srow = lax.broadcasted_iota(jnp.int32, (tm, r_l), 1)
        gates = jnp.where(srow == l0, route[:, R_G0:R_G0 + 1],
                          jnp.where(srow == l1, route[:, R_G1:R_G1 + 1], 0.0)).astype(BF16)
        f = jnp.dot(gates, y, preferred_element_type=F32)
        o_ref[...] = _rms(h_ref[...] + f, gn_ref[...])

    _prefetched(gather, pl.program_id(0), pl.num_programs(0), body)


def _combine(tile_src, h, route, norm_g, ybuf):
    t, d = h.shape
    tm = min(ROW_TILE, t)
    r_l = _local_rows(tm)
    return pl.pallas_call(
        _combine_kernel,
        grid_spec=pltpu.PrefetchScalarGridSpec(
            num_scalar_prefetch=1,
            grid=(t // tm,),
            in_specs=[pl.BlockSpec((tm, d), lambda i, src: (i, 0)),
                      pl.BlockSpec((tm, LANES), lambda i, src: (i, 0)),
                      pl.BlockSpec((1, d), lambda i, src: (0, 0)),
                      pl.BlockSpec(memory_space=pl.ANY)],
            out_specs=pl.BlockSpec((tm, d), lambda i, src: (i, 0)),
            scratch_shapes=[pltpu.VMEM((2, r_l, d), BF16),
                            pltpu.SemaphoreType.DMA((2,))],
        ),
        out_shape=jax.ShapeDtypeStruct((t, d), F32),
        compiler_params=pltpu.CompilerParams(
            dimension_semantics=("arbitrary",), vmem_limit_bytes=VMEM_LIMIT),
        name="combine",
    )(tile_src, h, route, norm_g, ybuf)


def kernel(x_prompt, x_sample, cache_k, cache_v, state_conv, norm_mix, w_in, conv_w, norm_out_attn,
           norm_out_conv, w_out, norm_ffn, w_router_group, b_router_group, w_router_expert,
           b_router_expert, w_gate, w_up, w_down, norm_final):
    n_seq, seq_len, d = x_prompt.shape
    db, ds, _ = x_sample.shape
    depth = w_in.shape[0]
    _, _, w_buf, n_heads, dh = cache_k.shape
    d_attn = n_heads * dh
    d_conv = d - d_attn
    assert depth == 1 and ds == 1 and dh == HEAD_DIM
    assert seq_len % (max(DILATIONS) * WIN_KEYS) == 0 and seq_len <= max(DILATIONS) * WIN_KEYS
    layer = 0
    tp, ts = n_seq * seq_len, db

    xp = x_prompt.reshape(tp, d)
    xs = x_sample.reshape(ts, d)
    row = lambda vec: vec.reshape(1, -1)
    w_in_b = w_in[layer].astype(BF16)
    w_out_b = w_out[layer].astype(BF16)
    g_mix, g_oa, g_oc, g_ffn = (row(norm_mix[layer]), row(norm_out_attn[layer]),
                                row(norm_out_conv[layer]), row(norm_ffn[layer]))
    st0, st1 = state_conv[layer, :, 0, :], state_conv[layer, :, 1, :]

    qp, kp, vp, kp_t, vp_t, ocp, conv_p = _mix_in_prompt(
        xp, g_mix, w_in_b, conv_w[layer], g_oc, seq_len=seq_len, d_attn=d_attn, d_conv=d_conv)
    qs, ks, vs, ocs, us = _mix_in_sample(
        xs, g_mix, w_in_b, conv_w[layer], g_oc, st0, st1, d_attn=d_attn, d_conv=d_conv)

    heads = lambda a: a.reshape(ts, 1, d_attn)
    positions_last = lambda c: jnp.transpose(c, (0, 2, 3, 1))
    attn_p, attn_s = _attention(qp, kp, vp, heads(qs), heads(ks), heads(vs),
                                positions_last(cache_k[layer]), positions_last(cache_v[layer]),
                                n_seq=n_seq, seq_len=seq_len)
    attn_s = attn_s.reshape(ts, d_attn)

    n_route = N_GROUPS + N_EXPERTS
    w_router = jnp.zeros((d, LANES), F32).at[:, :N_GROUPS].set(w_router_group[layer])
    w_router = w_router.at[:, N_GROUPS:n_route].set(w_router_expert[layer])
    b_router = jnp.zeros((1, LANES), F32).at[0, :N_GROUPS].set(b_router_group[layer])
    b_router = b_router.at[0, N_GROUPS:n_route].set(b_router_expert[layer])
    w_router_hi = w_router.astype(BF16)
    w_router_lo = (w_router - w_router_hi.astype(F32)).astype(BF16)
    mix_out = functools.partial(_mix_out, norm_ga=g_oa, w_out_bf16=w_out_b, norm_gf=g_ffn,
                                w_router=jnp.concatenate([w_router_hi, w_router_lo], axis=1),
                                b_router=b_router)
    h_s, route_s, xs_s, cnt_s = mix_out(xs, attn_s, ocs)
    assert cnt_s.shape[0] == 1
    h_p, route_p, xs_all, cnt_p = mix_out(xp, attn_p, ocp, tail=xs_s)

    tile_chunks = jnp.concatenate([cnt_p[:, 0, ROUTER_LANE0:n_route],
                                   cnt_s[:, 0, ROUTER_LANE0:n_route]], axis=0).astype(jnp.int32)
    ntp, nts = cnt_p.shape[0], cnt_s.shape[0]
    tm_p, tm_s = tp // ntp, ts // nts
    rl_p, rl_s = _local_rows(tm_p), _local_rows(tm_s)
    tile_row0 = jnp.arange(ntp + nts, dtype=jnp.int32) * rl_p
    total_chunks = ntp * _max_tile_chunks(tm_p) + nts * _max_tile_chunks(tm_s)
    n_blocks = -(-(total_chunks + N_EXPERTS * (CHUNKS_PER_BLOCK - 1)) // CHUNKS_PER_BLOCK)
    block_e, n_used, src_row, tile_src, next_e, block_rows = _sorted_layout(
        tile_chunks, tile_row0, rl_p // CHUNK, n_blocks)
    ybuf = _experts(block_e, n_used, src_row, next_e, block_rows, xs_all, w_gate[layer],
                    w_up[layer], w_down[layer])
    g_fin = row(norm_final)
    y_p = _combine(tile_src[:ntp].reshape(-1), h_p, route_p, g_fin, ybuf)
    y_s = _combine(tile_src[ntp:, :rl_s // CHUNK].reshape(-1), h_s, route_s, g_fin, ybuf)

    w_keep = min(max(DILATIONS) * WIN_KEYS, seq_len)
    kv5 = lambda a_t: jnp.transpose(a_t.reshape(n_seq, n_heads, dh, seq_len),
                                    (0, 3, 1, 2))[None, :, seq_len - w_keep:]
    conv_s = jnp.stack([st1, us], axis=1)[None]
    kvs = lambda a: a.reshape(1, ts, 1, n_heads, dh)
    return (y_p.reshape(n_seq, seq_len, d), y_s.reshape(db, ds, d), kv5(kp_t), kv5(vp_t),
            conv_p[None], kvs(ks), kvs(vs), conv_s)
```

```python
import functools

import jax
import jax.numpy as jnp
from jax import lax
from jax.experimental import pallas as pl
from jax.experimental.pallas import tpu as pltpu

HEAD_DIM = 64
WIN_KEYS = 128
DILATIONS = (1, 4, 16)
CONV_WIDTH = 3
N_GROUPS = 4
EXPERTS_PER_GROUP = 8
N_EXPERTS = N_GROUPS * EXPERTS_PER_GROUP
EPS = 1e-6
NEG = -1e30
LOG2_E = 1.4426950408889634

LANES = 128
ROW_TILE = 512
MIX_IN_TILE = 1024
EXPERT_BLOCK = 768
EXPERT_ROW_STEP = 128
ATTN_LAG = 3
ATTN_UNROLL = 16
VMEM_LIMIT = 56 * 1024 * 1024

F32 = jnp.float32
BF16 = jnp.bfloat16


def _rms(x, g):
    return x * lax.rsqrt(jnp.mean(x * x, axis=-1, keepdims=True) + EPS) * g


def _mix_in_kernel(*refs, d_attn, d_conv, sequential):
    if sequential:
        (x_ref, g_ref, w_ref, cw_ref, gc_ref,
         q_ref, k_ref, v_ref, kt_ref, vt_ref, oc_ref, st_ref, carry_ref) = refs
    else:
        (x_ref, g_ref, w_ref, cw_ref, gc_ref, st0_ref, st1_ref,
         q_ref, k_ref, v_ref, oc_ref, u_ref) = refs
    x = x_ref[...]
    xb = _rms(x, g_ref[...]).astype(BF16)

    def proj(lo, width):
        return jnp.dot(xb, w_ref[:, lo:lo + width], preferred_element_type=F32)

    q_ref[...] = proj(0, d_attn)
    k = proj(d_attn, d_attn)
    v = proj(2 * d_attn, d_attn)
    k_ref[...] = k
    v_ref[...] = v
    gate = proj(3 * d_attn, d_conv)
    u = proj(3 * d_attn + d_conv, d_conv) * proj(3 * d_attn + 2 * d_conv, d_conv)

    tm = x.shape[0]
    if sequential:
        kt_ref[...] = k.T
        vt_ref[...] = v.T

        @pl.when(pl.program_id(1) == 0)
        def _():
            carry_ref[...] = jnp.zeros_like(carry_ref)

        row = lax.broadcasted_iota(jnp.int32, u.shape, 0)
        prev1 = carry_ref[1:2, :]
        prev2 = carry_ref[0:1, :]
        u1 = jnp.where(row == 0, prev1, pltpu.roll(u, 1, axis=0))
        u2 = jnp.where(row == 0, prev2, jnp.where(row == 1, prev1, pltpu.roll(u, 2, axis=0)))
        carry_ref[0:2, :] = u[tm - 2:tm, :]
        st_ref[...] = u[tm - 2:tm, :]
    else:
        u_ref[...] = u
        u2 = st0_ref[...]
        u1 = st1_ref[...]
    z = u2 * cw_ref[0:1, :] + u1 * cw_ref[1:2, :] + u * cw_ref[2:3, :]
    oc_ref[...] = _rms(gate * z, gc_ref[...])


def _mix_in_call(kernel, grid, in_specs, out_specs, out_shape, scratch, args):
    return pl.pallas_call(
        kernel, grid=grid, in_specs=in_specs, out_specs=out_specs, out_shape=out_shape,
        scratch_shapes=scratch,
        compiler_params=pltpu.CompilerParams(
            dimension_semantics=("arbitrary",) * len(grid), vmem_limit_bytes=VMEM_LIMIT),
        name="mix_in",
    )(*args)


def _mix_in_prompt(x2d, norm_g, w_in_bf16, conv_w, norm_gc, *, seq_len, d_attn, d_conv):
    t, d = x2d.shape
    tm = min(MIX_IN_TILE, seq_len)
    n_seq, per = t // seq_len, seq_len // tm
    const = lambda b, s: (0, 0)
    row = lambda width: pl.BlockSpec((tm, width), lambda b, s: (b * per + s, 0))
    col = pl.BlockSpec((None, d_attn, tm), lambda b, s: (b, 0, s))
    f32 = lambda *shape: jax.ShapeDtypeStruct(shape, F32)
    return _mix_in_call(
        functools.partial(_mix_in_kernel, d_attn=d_attn, d_conv=d_conv, sequential=True),
        (n_seq, per),
        [row(d), pl.BlockSpec((1, d), const),
         pl.BlockSpec(w_in_bf16.shape, const, pipeline_mode=pl.Buffered(1)),
         pl.BlockSpec((CONV_WIDTH, d_conv), const), pl.BlockSpec((1, d_conv), const)],
        [row(d_attn)] * 3 + [col] * 2 + [row(d_conv),
                                         pl.BlockSpec((None, CONV_WIDTH - 1, d_conv),
                                                      lambda b, s: (b, 0, 0))],
        [f32(t, d_attn)] * 3 + [f32(n_seq, d_attn, seq_len)] * 2
        + [f32(t, d_conv), f32(n_seq, CONV_WIDTH - 1, d_conv)],
        [pltpu.VMEM((8, d_conv), F32)],
        (x2d, norm_g, w_in_bf16, conv_w, norm_gc))


def _mix_in_sample(x2d, norm_g, w_in_bf16, conv_w, norm_gc, st0, st1, *, d_attn, d_conv):
    t, d = x2d.shape
    full = lambda arr: pl.BlockSpec(arr.shape, lambda i: (0,) * arr.ndim)
    f32 = lambda *shape: jax.ShapeDtypeStruct(shape, F32)
    args = (x2d, norm_g, w_in_bf16, conv_w, norm_gc, st0, st1)
    outs = [f32(t, d_attn)] * 3 + [f32(t, d_conv)] * 2
    return _mix_in_call(
        functools.partial(_mix_in_kernel, d_attn=d_attn, d_conv=d_conv, sequential=False),
        (1,), [full(a) for a in args], [full(o) for o in outs], outs, [], args)


def _attn_prompt_kernel(q_ref, k_ref, v_ref, o_ref, m_s, l_s, a_s, *, seq_len):
    w = WIN_KEYS
    scale = HEAD_DIM ** -0.5 * LOG2_E
    r_i = lax.broadcasted_iota(jnp.int32, (2 * w, 2 * w), 0) & (w - 1)
    c_i = lax.broadcasted_iota(jnp.int32, (2 * w, 2 * w), 1)
    mask_cur = (lax.broadcasted_iota(jnp.int32, (2 * w, w), 1)
                <= lax.broadcasted_iota(jnp.int32, (2 * w, w), 0) & (w - 1))
    mask_both = jnp.logical_and(c_i >= r_i, c_i - w <= r_i)
    first_head = lax.broadcasted_iota(jnp.int32, (w, 2 * HEAD_DIM), 1) < HEAD_DIM
    dn_t = (((1,), (1,)), ((), ()))

    def rows(start, dil):
        if dil > 1:
            return pl.ds(start, w, stride=dil)
        return pl.ds(start if isinstance(start, int) else pl.multiple_of(start, w), w)

    def run_branch(dil, first, last):
        span = dil * w
        nb = seq_len // span

        def blocks(its, with_prev):
            mask = mask_both if with_prev else mask_cur

            def issue_scores(it):
                g = it % dil
                n = it // dil
                c = rows(g + n * span, dil)
                qb = (q_ref[c, :] * scale).astype(BF16)
                zero = jnp.zeros_like(qb)
                q = jnp.concatenate([jnp.where(first_head, qb, zero),
                                     jnp.where(first_head, zero, qb)], axis=0)
                k = k_ref[c, :].astype(BF16)
                v = v_ref[c, :].astype(BF16)
                if with_prev:
                    p = rows(g + (n - 1) * span, dil)
                    k = jnp.concatenate([k_ref[p, :].astype(BF16), k], axis=0)
                    v = jnp.concatenate([v_ref[p, :].astype(BF16), v], axis=0)
                return c, lax.dot_general(q, k, dn_t, preferred_element_type=F32), v

            def finish(c, s, v):
                s = jnp.where(mask, s, NEG)
                m = jnp.max(s, axis=-1, keepdims=True)
                p = jnp.exp2(s - m).astype(BF16)
                ones = jnp.ones((v.shape[0], 2 * HEAD_DIM), BF16)
                acc_l = jnp.dot(p, jnp.concatenate([v, ones], axis=1), preferred_element_type=F32)
                acc, l = acc_l[:, :2 * HEAD_DIM], acc_l[:, 2 * HEAD_DIM:]
                m_b = jnp.where(first_head, m[:w], m[w:])
                l_b = jnp.where(first_head, l[:w], l[w:])
                a_b = jnp.where(first_head, acc[:w], acc[w:])
                if not first:
                    m_o = m_s[c, :]
                    m_n = jnp.maximum(m_o, m_b)
                    w_o = jnp.exp2(m_o - m_n)
                    w_b = jnp.exp2(m_b - m_n)
                    l_b = w_o * l_s[c, :] + w_b * l_b
                    a_b = w_o * a_s[c, :] + w_b * a_b
                    m_b = m_n
                if last:
                    o_ref[c, :] = a_b / l_b
                else:
                    m_s[c, :] = m_b
                    l_s[c, :] = l_b
                    a_s[c, :] = a_b

            in_flight = []
            for i in range(len(its) + ATTN_LAG):
                if i < len(its):
                    in_flight.append(issue_scores(its[i]))
                if i >= ATTN_LAG:
                    finish(*in_flight.pop(0))

        def run(lo, hi, with_prev):
            u = ATTN_UNROLL
            trips = (hi - lo) // u

            def body(t, carry):
                blocks([lo + t * u + j for j in range(u)], with_prev)
                return carry

            if trips:
                lax.fori_loop(0, trips, body, 0)
            if lo + trips * u < hi:
                blocks(list(range(lo + trips * u, hi)), with_prev)

        run(0, dil, False)
        run(dil, dil * nb, True)

    order = sorted(DILATIONS, reverse=True)
    for i, dil in enumerate(order):
        run_branch(dil, i == 0, i == len(order) - 1)


def _attn_sample_kernel(q_ref, kn_ref, vn_ref, kt_ref, vt_ref, o_ref):
    n_heads, dh, w_buf = kt_ref.shape
    delta = w_buf - lax.broadcasted_iota(jnp.int32, (1, w_buf), 1)
    cnt = jnp.zeros((1, w_buf), F32)
    for dil in DILATIONS:
        assert dil & (dil - 1) == 0
        member = jnp.where(delta <= dil * WIN_KEYS, 1.0, 0.0)
        cnt = cnt + jnp.where((delta & (dil - 1)) == 0, member, 0.0)
    eye = (lax.broadcasted_iota(jnp.int32, (dh, dh), 0)
           == lax.broadcasted_iota(jnp.int32, (dh, dh), 1))
    to_col = lambda r: jnp.sum(jnp.where(eye, r, 0.0), axis=1, keepdims=True)
    to_row = lambda c: jnp.sum(jnp.where(eye, c, 0.0), axis=0, keepdims=True)
    outs = []
    for h in range(n_heads):
        sl = slice(h * dh, (h + 1) * dh)
        q = q_ref[:, sl] * (HEAD_DIM ** -0.5)
        s_self = jnp.sum(q * kn_ref[:, sl], axis=1, keepdims=True)
        s = jnp.sum(to_col(q) * kt_ref[h], axis=0, keepdims=True)
        s = jnp.where(cnt > 0.0, s, NEG)
        m = jnp.maximum(jnp.max(s, axis=1, keepdims=True), s_self)
        p = cnt * jnp.exp(s - m)
        p_self = len(DILATIONS) * jnp.exp(s_self - m)
        l = jnp.sum(p, axis=1, keepdims=True) + p_self
        acc = jnp.sum(p * vt_ref[h], axis=1, keepdims=True)
        outs.append((to_row(acc) + p_self * vn_ref[:, sl]) / l)
    o_ref[...] = jnp.concatenate(outs, axis=1)


def _attn_kernel(q_ref, k_ref, v_ref, qs_ref, kn_ref, vn_ref, kt_ref, vt_ref, o_ref, os_ref,
                 m_s, l_s, a_s, *, seq_len):
    _attn_sample_kernel(qs_ref, kn_ref, vn_ref, kt_ref, vt_ref, os_ref)
    _attn_prompt_kernel(q_ref, k_ref, v_ref, o_ref, m_s, l_s, a_s, seq_len=seq_len)


def _attention(q, k, v, qs, k_new, v_new, cache_kt, cache_vt, *, n_seq, seq_len):
    t, d_attn = q.shape
    db, n_heads, dh, w_buf = cache_kt.shape
    pair = 2 * HEAD_DIM
    pairs = d_attn // pair
    assert db == n_seq * pairs, "one sample sequence per prompt grid step"
    spec = pl.BlockSpec((seq_len, pair), lambda b, h: (b, h))
    head_spec = pl.BlockSpec((None, 1, d_attn), lambda b, h: (b * pairs + h, 0, 0))
    cache_spec = pl.BlockSpec((None, n_heads, dh, w_buf), lambda b, h: (b * pairs + h, 0, 0, 0))
    return pl.pallas_call(
        functools.partial(_attn_kernel, seq_len=seq_len),
        grid=(n_seq, pairs),
        in_specs=[spec] * 3 + [head_spec] * 3 + [cache_spec] * 2,
        out_specs=[spec, head_spec],
        out_shape=[jax.ShapeDtypeStruct((t, d_attn), F32),
                   jax.ShapeDtypeStruct((db, 1, d_attn), F32)],
        scratch_shapes=[pltpu.VMEM((seq_len, pair), F32)] * 3,
        compiler_params=pltpu.CompilerParams(
            dimension_semantics=("arbitrary", "arbitrary"), vmem_limit_bytes=VMEM_LIMIT),
        name="attention",
    )(q, k, v, qs, k_new, v_new, cache_kt, cache_vt)


R_E0, R_E1, R_G0, R_G1, R_POS0, R_POS1 = range(6)
ROUTER_LANE0 = N_GROUPS
CHUNK = 16
CHUNKS_PER_BLOCK = EXPERT_BLOCK // CHUNK


def _max_tile_chunks(tm):
    return (2 * tm + (CHUNK - 1) * N_EXPERTS) // CHUNK


def _local_rows(tm):
    return 2 * tm + N_EXPERTS * CHUNK


def _mix_out_kernel(*refs, n_tiles, has_tail):
    if has_tail:
        *tile_in, tail_ref, h_ref, route_ref, xs_ref, cnt_ref, tok_s, pos_s = refs
    else:
        *tile_in, h_ref, route_ref, xs_ref, cnt_ref, tok_s, pos_s = refs
    i = pl.program_id(0)
    route = lambda slot, *between: _route_tile(*tile_in, h_ref, route_ref, cnt_ref,
                                               tok_s.at[slot], pos_s.at[slot], *between)
    sort = lambda slot: _sort_tile(tok_s.at[slot], pos_s.at[slot], xs_ref)

    @pl.when(i == 0)
    def _():
        route(0)

    for parity in range(2):
        @pl.when(jnp.logical_and(jnp.logical_and(i >= 1, i < n_tiles), i % 2 == parity))
        def _():
            route(parity, lambda: sort(1 - parity))

    @pl.when(i == n_tiles)
    def _():
        sort((n_tiles - 1) % 2)

    if has_tail:
        @pl.when(i == n_tiles + 1)
        def _():
            rows = tail_ref.shape[0]
            xs_ref[0:rows, :] = tail_ref[...]
            xs_ref[rows:, :] = jnp.zeros((xs_ref.shape[0] - rows, xs_ref.shape[1]), xs_ref.dtype)


def _sort_tile(tok_ref, pos_ref, xs_ref):
    tm = tok_ref.shape[0]
    l1 = pos_ref[R_POS0:R_POS0 + 1, :].astype(jnp.int32)
    l2 = pos_ref[R_POS1:R_POS1 + 1, :].astype(jnp.int32)
    srow = lax.broadcasted_iota(jnp.int32, (xs_ref.shape[0], tm), 0)
    perm = jnp.where(srow == l1, 1.0, jnp.where(srow == l2, 1.0, 0.0)).astype(BF16)
    xs_ref[...] = jnp.dot(perm, tok_ref[...], preferred_element_type=F32).astype(BF16)


def _route_tile(x_ref, a_ref, oc_ref, ga_ref, wo_ref, gf_ref, wr_ref, br_ref,
                h_ref, route_ref, cnt_ref, tok_ref, pos_ref, after_projections=lambda: None):
    d_attn = a_ref.shape[1]
    tm, d = x_ref.shape
    a = _rms(a_ref[...], ga_ref[...]).astype(BF16)
    mix = jnp.dot(a, wo_ref[0:d_attn, :], preferred_element_type=F32)
    mix = mix + jnp.dot(oc_ref[...].astype(BF16), wo_ref[d_attn:, :], preferred_element_type=F32)
    h = x_ref[...] + mix
    h_ref[...] = h
    tok = _rms(h, gf_ref[...])

    tok_hi = tok.astype(BF16)
    tok_lo = (tok - tok_hi.astype(F32)).astype(BF16)
    hi_part = jnp.dot(tok_hi, wr_ref[...], preferred_element_type=F32)
    lo_part = jnp.dot(tok_lo, wr_ref[:, :LANES], preferred_element_type=F32)
    logits = hi_part[:, :LANES] + hi_part[:, LANES:] + lo_part + br_ref[...]
    after_projections()
    lane = lax.broadcasted_iota(jnp.int32, logits.shape, 1)
    big = jnp.int32(LANES)
    neg_inf = jnp.float32(-jnp.inf)

    def top1(vals):
        best = jnp.max(vals, axis=-1, keepdims=True)
        idx = jnp.min(jnp.where(vals == best, lane, big), axis=-1, keepdims=True)
        return best, idx

    is_group = lane < N_GROUPS
    lg = jnp.where(is_group, logits, neg_inf)
    mg, g_sel = top1(lg)
    p_group = 1.0 / jnp.sum(jnp.where(is_group, jnp.exp(lg - mg), 0.0), axis=-1, keepdims=True)

    lo = ROUTER_LANE0 + g_sel * EXPERTS_PER_GROUP
    in_group = jnp.logical_and(lane >= lo, lane < lo + EXPERTS_PER_GROUP)
    le = jnp.where(in_group, logits, neg_inf)
    v1, i1 = top1(le)
    v2, i2 = top1(jnp.where(lane == i1, neg_inf, le))
    e2 = jnp.exp(v2 - v1)
    gate1 = p_group / (1.0 + e2)
    gate2 = p_group * e2 / (1.0 + e2)

    oh1 = lane == i1
    oh2 = lane == i2
    both = jnp.where(jnp.logical_or(oh1, oh2), 1.0, 0.0)
    r_i = lax.broadcasted_iota(jnp.int32, (tm, tm), 0)
    c_i = lax.broadcasted_iota(jnp.int32, (tm, tm), 1)
    strict_lower = jnp.where(c_i < r_i, 1.0, 0.0).astype(BF16)
    before = jnp.dot(strict_lower, both.astype(BF16), preferred_element_type=F32)
    chunks = jnp.floor((jnp.sum(both, axis=0, keepdims=True) + (CHUNK - 1)) * (1.0 / CHUNK))
    u_r = lax.broadcasted_iota(jnp.int32, (LANES, LANES), 0)
    u_c = lax.broadcasted_iota(jnp.int32, (LANES, LANES), 1)
    strict_upper = jnp.where(u_r < u_c, 1.0, 0.0).astype(BF16)
    chunks8 = jnp.broadcast_to(chunks, (8, LANES))
    first_row = CHUNK * jnp.dot(chunks8.astype(BF16), strict_upper,
                                preferred_element_type=F32)[0:1, :]
    pos = first_row + before
    pos1 = jnp.sum(jnp.where(oh1, pos, 0.0), axis=-1, keepdims=True)
    pos2 = jnp.sum(jnp.where(oh2, pos, 0.0), axis=-1, keepdims=True)
    cnt_ref[...] = jnp.where(lax.broadcasted_iota(jnp.int32, (8, LANES), 0) == 0, chunks8, 0.0)

    rec = jnp.zeros(logits.shape, F32)
    for col, val in ((R_E0, (i1 - ROUTER_LANE0).astype(F32)), (R_E1, (i2 - ROUTER_LANE0).astype(F32)),
                     (R_G0, gate1), (R_G1, gate2), (R_POS0, pos1), (R_POS1, pos2)):
        rec = jnp.where(lane == col, val, rec)
    route_ref[...] = rec
    tok_ref[...] = tok_hi
    pos_ref[...] = rec.T[0:pos_ref.shape[0], :]


def _mix_out(x2d, attn, oconv, norm_ga, w_out_bf16, norm_gf, w_router, b_router, tail=None):
    t, d = x2d.shape
    d_attn, d_conv = attn.shape[1], oconv.shape[1]
    tm = min(ROW_TILE, t)
    nt = t // tm
    r_l = _local_rows(tm)
    has_tail = tail is not None
    tile = lambda i: jnp.minimum(i, nt - 1)
    sorted_block = lambda i: jnp.minimum(jnp.maximum(i - 1, 0), nt - 1 + has_tail)
    row = lambda width: pl.BlockSpec((tm, width), lambda i: (tile(i), 0))
    full = lambda arr: pl.BlockSpec(arr.shape, lambda i: (0, 0))
    args = [x2d, attn, oconv, norm_ga, w_out_bf16, norm_gf, w_router, b_router]
    in_specs = [row(d), row(d_attn), row(d_conv)] + [full(a) for a in args[3:]]
    if has_tail:
        assert tail.shape[0] <= r_l and tail.shape[1] == d
        args.append(tail)
        in_specs.append(full(tail))
    return pl.pallas_call(
        functools.partial(_mix_out_kernel, n_tiles=nt, has_tail=has_tail),
        grid=(nt + 1 + has_tail,),
        in_specs=in_specs,
        out_specs=[row(d), row(LANES), pl.BlockSpec((r_l, d), lambda i: (sorted_block(i), 0)),
                   pl.BlockSpec((None, 8, LANES), lambda i: (tile(i), 0, 0))],
        out_shape=[jax.ShapeDtypeStruct((t, d), F32), jax.ShapeDtypeStruct((t, LANES), F32),
                   jax.ShapeDtypeStruct(((nt + has_tail) * r_l, d), BF16),
                   jax.ShapeDtypeStruct((nt, 8, LANES), F32)],
        scratch_shapes=[pltpu.VMEM((2, tm, d), BF16), pltpu.VMEM((2, 8, tm), F32)],
        compiler_params=pltpu.CompilerParams(
            dimension_semantics=("arbitrary",), vmem_limit_bytes=VMEM_LIMIT),
        name="mix_out",
    )(*args)


def _sorted_layout(tile_chunks, tile_row0, max_local, n_blocks):
    nt, n_exp = tile_chunks.shape
    cpb = CHUNKS_PER_BLOCK
    i32 = jnp.int32
    seg = jnp.sum(tile_chunks, axis=0)
    padded = (seg + cpb - 1) // cpb * cpb
    pend = jnp.cumsum(padded)
    pstart = pend - padded
    tile_incl = jnp.cumsum(tile_chunks, axis=0)
    tile_excl = tile_incl - tile_chunks
    local_incl = jnp.cumsum(tile_chunks, axis=1)
    local_excl = local_incl - tile_chunks
    base = pstart[None, :] + tile_excl

    block_first = jnp.arange(n_blocks, dtype=i32) * cpb
    block_e = jnp.minimum(jnp.sum((pend[None, :] <= block_first[:, None]).astype(i32), axis=1),
                          n_exp - 1)
    n_used = (pend[-1:] // cpb).astype(i32)

    onehot_pick = lambda onehot, table: jnp.sum(jnp.where(onehot, table, 0), axis=-1)

    is_e = block_e[:, None] == jnp.arange(n_exp, dtype=i32)[None, :]
    of_expert = lambda table_te: onehot_pick(is_e[:, None, :], table_te[None, :, :])
    incl_b, cnt_b, lexcl_b = of_expert(tile_incl), of_expert(tile_chunks), of_expert(local_excl)
    q = (block_first - onehot_pick(is_e, pstart[None, :]))[:, None] + jnp.arange(cpb, dtype=i32)
    tile_q = jnp.minimum(jnp.sum((incl_b[:, None, :] <= q[:, :, None]).astype(i32), axis=2), nt - 1)
    is_t = tile_q[:, :, None] == jnp.arange(nt, dtype=i32)[None, None, :]
    of_tile = lambda table_bt: onehot_pick(is_t, table_bt[:, None, :])
    local_chunk = of_tile(lexcl_b) + q - of_tile(incl_b - cnt_b)
    seg_b = onehot_pick(is_e, seg[None, :])
    block_rows = (CHUNK * jnp.clip(seg_b - q[:, 0], 0, cpb)).astype(i32)
    in_run = jnp.logical_and(q >= 0, q < seg_b[:, None])
    src_row = jnp.where(in_run, of_tile(tile_row0[None, :]) + CHUNK * local_chunk, 0)
    src_row = src_row.reshape(-1).astype(i32)

    c = jnp.arange(max_local, dtype=i32)
    e_c = jnp.minimum(jnp.sum((local_incl[:, None, :] <= c[None, :, None]).astype(i32), axis=2),
                      n_exp - 1)
    is_ec = e_c[:, :, None] == jnp.arange(n_exp, dtype=i32)[None, None, :]
    of_run = lambda table_te: onehot_pick(is_ec, table_te[:, None, :])
    global_chunk = of_run(base) + c[None, :] - of_run(local_excl)
    tile_src = jnp.where(c[None, :] < local_incl[:, -1:], CHUNK * global_chunk, 0).astype(i32)
    e_ids = jnp.arange(n_exp, dtype=i32)
    later = jnp.logical_and(seg[None, :] > 0, e_ids[None, :] > e_ids[:, None])
    next_e = jnp.min(jnp.where(later, e_ids[None, :], n_exp), axis=1)
    next_e = jnp.where(next_e < n_exp, next_e, -1).astype(i32)
    return block_e.astype(i32), n_used, src_row, tile_src, next_e, block_rows


def _chunk_gather(src_ref, hbm_ref, buf, sems, item, slot, n_chunks, *, wait):
    for c in range(n_chunks):
        row = 0 if wait else pl.multiple_of(src_ref[item * n_chunks + c], CHUNK)
        copy = pltpu.make_async_copy(hbm_ref.at[pl.ds(row, CHUNK)],
                                     buf.at[slot, pl.ds(c * CHUNK, CHUNK)], sems.at[slot])
        if wait:
            copy.wait()
        else:
            copy.start()


def _prefetched(gather, step, n_items, body):
    slot = step % 2

    @pl.when(jnp.logical_and(step == 0, n_items > 0))
    def _():
        gather(0, 0, wait=False)

    @pl.when(step + 1 < n_items)
    def _():
        gather(step + 1, 1 - slot, wait=False)

    body(slot, lambda: gather(step, slot, wait=True))


def _experts_kernel(block_e_ref, n_used_ref, src_ref, next_e_ref, block_rows_ref, xs_ref, wg_hbm,
                    wu_hbm, wd_hbm, y_hbm, xblk, sems, ybuf, ysems, wg_f, wu_f, wd_f, wsems, wg_b, wu_b, wd_b,
                    run_ref):
    rows = EXPERT_BLOCK
    n_blocks = y_hbm.shape[0] // rows
    n_used = n_used_ref[0]
    gather = functools.partial(_chunk_gather, src_ref, xs_ref, xblk, sems,
                               n_chunks=CHUNKS_PER_BLOCK)

    def weight_copies(expert, slot):
        return [pltpu.make_async_copy(hbm.at[expert], stage.at[slot], wsems.at[slot])
                for hbm, stage in ((wg_hbm, wg_f), (wu_hbm, wu_f), (wd_hbm, wd_f))]

    def y_copy(blk, slot):
        start = blk * rows if isinstance(blk, int) else pl.multiple_of(blk * rows, rows)
        return pltpu.make_async_copy(ybuf.at[slot], y_hbm.at[pl.ds(start, rows)], ysems.at[slot])

    def block(b, carry):
        e = block_e_ref[b]
        new_expert = jnp.logical_or(b == 0, e != block_e_ref[jnp.maximum(b - 1, 0)])

        @pl.when(jnp.logical_and(new_expert, b < n_used))
        def _():
            @pl.when(b == 0)
            def _():
                run_ref[0] = 0
                for copy in weight_copies(e, 0):
                    copy.start()

            @pl.when(b > 0)
            def _():
                run_ref[0] = run_ref[0] + 1

            slot = run_ref[0] % 2
            nxt = next_e_ref[e]

            @pl.when(nxt >= 0)
            def _():
                for copy in weight_copies(nxt, 1 - slot):
                    copy.start()

            for copy in weight_copies(e, slot):
                copy.wait()
            wg_b[...] = wg_f[slot].astype(BF16)
            wu_b[...] = wu_f[slot].astype(BF16)
            wd_b[...] = wd_f[slot].astype(BF16)

        def body(slot, wait_current):
            @pl.when(b >= 2)
            def _():
                y_copy(b - 2, slot).wait()

            @pl.when(b < n_used)
            def _():
                wait_current()

            valid = block_rows_ref[b]
            for m in range(EXPERT_ROW_STEP, rows + 1, EXPERT_ROW_STEP):
                @pl.when(jnp.logical_and(valid > m - EXPERT_ROW_STEP, valid <= m))
                def _():
                    x = xblk[slot, 0:m, :]
                    gate = jnp.dot(x, wg_b[...], preferred_element_type=F32)
                    up = jnp.dot(x, wu_b[...], preferred_element_type=F32)
                    hid = gate * (1.0 / (1.0 + jnp.exp(-gate))) * up
                    ybuf[slot, 0:m, :] = jnp.dot(hid.astype(BF16), wd_b[...],
                                                 preferred_element_type=F32).astype(BF16)
                    if m < rows:
                        ybuf[slot, m:rows, :] = jnp.zeros((rows - m, ybuf.shape[2]), ybuf.dtype)

            @pl.when(valid == 0)
            def _():
                ybuf[slot] = jnp.zeros(ybuf.shape[1:], ybuf.dtype)

            y_copy(b, slot).start()

        _prefetched(gather, b, n_used, body)
        return carry

    lax.fori_loop(0, n_blocks, block, 0)
    for blk in range(max(n_blocks - 2, 0), n_blocks):
        y_copy(blk, blk % 2).wait()


def _experts(block_e, n_used, src_row, next_e, block_rows, xs, w_gate, w_up, w_down):
    n_blocks = block_e.shape[0]
    _, d, d_exp = w_gate.shape
    blk = EXPERT_BLOCK
    any_spec = pl.BlockSpec(memory_space=pl.ANY)
    return pl.pallas_call(
        _experts_kernel,
        grid_spec=pltpu.PrefetchScalarGridSpec(
            num_scalar_prefetch=5,
            grid=(1,),
            in_specs=[any_spec] * 4,
            out_specs=any_spec,
            scratch_shapes=[pltpu.VMEM((2, blk, d), BF16), pltpu.SemaphoreType.DMA((2,)),
                            pltpu.VMEM((2, blk, d), BF16), pltpu.SemaphoreType.DMA((2,)),
                            pltpu.VMEM((2, d, d_exp), F32), pltpu.VMEM((2, d, d_exp), F32),
                            pltpu.VMEM((2, d_exp, d), F32), pltpu.SemaphoreType.DMA((2,)),
                            pltpu.VMEM((d, d_exp), BF16), pltpu.VMEM((d, d_exp), BF16),
                            pltpu.VMEM((d_exp, d), BF16), pltpu.SMEM((1,), jnp.int32)],
        ),
        out_shape=jax.ShapeDtypeStruct((n_blocks * blk, d), BF16),
        compiler_params=pltpu.CompilerParams(
            dimension_semantics=("arbitrary",), vmem_limit_bytes=VMEM_LIMIT),
        name="experts",
    )(block_e, n_used, src_row, next_e, block_rows, xs, w_gate, w_up, w_down)


def _combine_kernel(src_ref, h_ref, route_ref, gn_ref, ybuf_ref, o_ref, yloc, sems):
    tm = h_ref.shape[0]
    r_l = yloc.shape[1]
    gather = functools.partial(_chunk_gather, src_ref, ybuf_ref, yloc, sems,
                               n_chunks=r_l // CHUNK)

    def body(slot, wait_current):
        wait_current()
        y = yloc[slot]
        route = route_ref[...]
        l0 = route[:, R_POS0:R_POS0 + 1].astype(jnp.int32)
        l1 = route[:, R_POS1:R_POS1 + 1].astype(jnp.int32)
        srow = lax.broadcasted_iota(jnp.int32, (tm, r_l), 1)
        gates = jnp.where(srow == l0, route[:, R_G0:R_G0 + 1],
                          jnp.where(srow == l1, route[:, R_G1:R_G1 + 1], 0.0)).astype(BF16)
        f = jnp.dot(gates, y, preferred_element_type=F32)
        o_ref[...] = _rms(h_ref[...] + f, gn_ref[...])

    _prefetched(gather, pl.program_id(0), pl.num_programs(0), body)


def _combine(tile_src, h, route, norm_g, ybuf):
    t, d = h.shape
    tm = min(ROW_TILE, t)
    r_l = _local_rows(tm)
    return pl.pallas_call(
        _combine_kernel,
        grid_spec=pltpu.PrefetchScalarGridSpec(
            num_scalar_prefetch=1,
            grid=(t // tm,),
            in_specs=[pl.BlockSpec((tm, d), lambda i, src: (i, 0)),
                      pl.BlockSpec((tm, LANES), lambda i, src: (i, 0)),
                      pl.BlockSpec((1, d), lambda i, src: (0, 0)),
                      pl.BlockSpec(memory_space=pl.ANY)],
            out_specs=pl.BlockSpec((tm, d), lambda i, src: (i, 0)),
            scratch_shapes=[pltpu.VMEM((2, r_l, d), BF16),
                            pltpu.SemaphoreType.DMA((2,))],
        ),
        out_shape=jax.ShapeDtypeStruct((t, d), F32),
        compiler_params=pltpu.CompilerParams(
            dimension_semantics=("arbitrary",), vmem_limit_bytes=VMEM_LIMIT),
        name="combine",
    )(tile_src, h, route, norm_g, ybuf)


def kernel(x_prompt, x_sample, cache_k, cache_v, state_conv, norm_mix, w_in, conv_w, norm_out_attn,
           norm_out_conv, w_out, norm_ffn, w_router_group, b_router_group, w_router_expert,
           b_router_expert, w_gate, w_up, w_down, norm_final):
    n_seq, seq_len, d = x_prompt.shape
    db, ds, _ = x_sample.shape
    depth = w_in.shape[0]
    _, _, w_buf, n_heads, dh = cache_k.shape
    d_attn = n_heads * dh
    d_conv = d - d_attn
    assert depth == 1 and ds == 1 and dh == HEAD_DIM
    assert seq_len % (max(DILATIONS) * WIN_KEYS) == 0 and seq_len <= max(DILATIONS) * WIN_KEYS
    layer = 0
    tp, ts = n_seq * seq_len, db

    xp = x_prompt.reshape(tp, d)
    xs = x_sample.reshape(ts, d)
    row = lambda vec: vec.reshape(1, -1)
    w_in_b = w_in[layer].astype(BF16)
    w_out_b = w_out[layer].astype(BF16)
    g_mix, g_oa, g_oc, g_ffn = (row(norm_mix[layer]), row(norm_out_attn[layer]),
                                row(norm_out_conv[layer]), row(norm_ffn[layer]))
    st0, st1 = state_conv[layer, :, 0, :], state_conv[layer, :, 1, :]

    qp, kp, vp, kp_t, vp_t, ocp, conv_p = _mix_in_prompt(
        xp, g_mix, w_in_b, conv_w[layer], g_oc, seq_len=seq_len, d_attn=d_attn, d_conv=d_conv)
    qs, ks, vs, ocs, us = _mix_in_sample(
        xs, g_mix, w_in_b, conv_w[layer], g_oc, st0, st1, d_attn=d_attn, d_conv=d_conv)

    heads = lambda a: a.reshape(ts, 1, d_attn)
    positions_last = lambda c: jnp.transpose(c, (0, 2, 3, 1))
    attn_p, attn_s = _attention(qp, kp, vp, heads(qs), heads(ks), heads(vs),
                                positions_last(cache_k[layer]), positions_last(cache_v[layer]),
                                n_seq=n_seq, seq_len=seq_len)
    attn_s = attn_s.reshape(ts, d_attn)

    n_route = N_GROUPS + N_EXPERTS
    w_router = jnp.zeros((d, LANES), F32).at[:, :N_GROUPS].set(w_router_group[layer])
    w_router = w_router.at[:, N_GROUPS:n_route].set(w_router_expert[layer])
    b_router = jnp.zeros((1, LANES), F32).at[0, :N_GROUPS].set(b_router_group[layer])
    b_router = b_router.at[0, N_GROUPS:n_route].set(b_router_expert[layer])
    w_router_hi = w_router.astype(BF16)
    w_router_lo = (w_router - w_router_hi.astype(F32)).astype(BF16)
    mix_out = functools.partial(_mix_out, norm_ga=g_oa, w_out_bf16=w_out_b, norm_gf=g_ffn,
                                w_router=jnp.concatenate([w_router_hi, w_router_lo], axis=1),
                                b_router=b_router)
    h_s, route_s, xs_s, cnt_s = mix_out(xs, attn_s, ocs)
    assert cnt_s.shape[0] == 1
    h_p, route_p, xs_all, cnt_p = mix_out(xp, attn_p, ocp, tail=xs_s)

    tile_chunks = jnp.concatenate([cnt_p[:, 0, ROUTER_LANE0:n_route],
                                   cnt_s[:, 0, ROUTER_LANE0:n_route]], axis=0).astype(jnp.int32)
    ntp, nts = cnt_p.shape[0], cnt_s.shape[0]
    tm_p, tm_s = tp // ntp, ts // nts
    rl_p, rl_s = _local_rows(tm_p), _local_rows(tm_s)
    tile_row0 = jnp.arange(ntp + nts, dtype=jnp.int32) * rl_p
    total_chunks = ntp * _max_tile_chunks(tm_p) + nts * _max_tile_chunks(tm_s)
    n_blocks = -(-(total_chunks + N_EXPERTS * (CHUNKS_PER_BLOCK - 1)) // CHUNKS_PER_BLOCK)
    block_e, n_used, src_row, tile_src, next_e, block_rows = _sorted_layout(
        tile_chunks, tile_row0, rl_p // CHUNK, n_blocks)
    ybuf = _experts(block_e, n_used, src_row, next_e, block_rows, xs_all, w_gate[layer],
                    w_up[layer], w_down[layer])
    g_fin = row(norm_final)
    y_p = _combine(tile_src[:ntp].reshape(-1), h_p, route_p, g_fin, ybuf)
    y_s = _combine(tile_src[ntp:, :rl_s // CHUNK].reshape(-1), h_s, route_s, g_fin, ybuf)

    w_keep = min(max(DILATIONS) * WIN_KEYS, seq_len)
    kv5 = lambda a_t: jnp.transpose(a_t.reshape(n_seq, n_heads, dh, seq_len),
                                    (0, 3, 1, 2))[None, :, seq_len - w_keep:]
    conv_s = jnp.stack([st1, us], axis=1)[None]
    kvs = lambda a: a.reshape(1, ts, 1, n_heads, dh)
    return (y_p.reshape(n_seq, seq_len, d), y_s.reshape(db, ds, d), kv5(kp_t), kv5(vp_t),
            conv_p[None], kvs(ks), kvs(vs), conv_s)
```

```python
import functools

import jax
import jax.numpy as jnp
from jax import lax
from jax.experimental import pallas as pl
from jax.experimental.pallas import tpu as pltpu

HEAD_DIM = 64
WIN_KEYS = 128
DILATIONS = (1, 4, 16)
CONV_WIDTH = 3
N_GROUPS = 4
EXPERTS_PER_GROUP = 8
N_EXPERTS = N_GROUPS * EXPERTS_PER_GROUP
EPS = 1e-6
NEG = -1e30
LOG2_E = 1.4426950408889634

LANES = 128
ROW_TILE = 512
MIX_IN_TILE = 1024
EXPERT_BLOCK = 512
EXPERT_ROW_STEP = 128
ATTN_LAG = 3
ATTN_UNROLL = 16
VMEM_LIMIT = 56 * 1024 * 1024

F32 = jnp.float32
BF16 = jnp.bfloat16


def _rms(x, g):
    return x * lax.rsqrt(jnp.mean(x * x, axis=-1, keepdims=True) + EPS) * g


def _mix_in_kernel(*refs, d_attn, d_conv, sequential):
    if sequential:
        (x_ref, g_ref, w_ref, cw_ref, gc_ref,
         q_ref, k_ref, v_ref, kt_ref, vt_ref, oc_ref, st_ref, carry_ref) = refs
    else:
        (x_ref, g_ref, w_ref, cw_ref, gc_ref, st0_ref, st1_ref,
         q_ref, k_ref, v_ref, oc_ref, u_ref) = refs
    x = x_ref[...]
    xb = _rms(x, g_ref[...]).astype(BF16)

    def proj(lo, width):
        return jnp.dot(xb, w_ref[:, lo:lo + width], preferred_element_type=F32)

    q_ref[...] = proj(0, d_attn)
    k = proj(d_attn, d_attn)
    v = proj(2 * d_attn, d_attn)
    k_ref[...] = k
    v_ref[...] = v
    gate = proj(3 * d_attn, d_conv)
    u = proj(3 * d_attn + d_conv, d_conv) * proj(3 * d_attn + 2 * d_conv, d_conv)

    tm = x.shape[0]
    if sequential:
        kt_ref[...] = k.T
        vt_ref[...] = v.T

        @pl.when(pl.program_id(1) == 0)
        def _():
            carry_ref[...] = jnp.zeros_like(carry_ref)

        row = lax.broadcasted_iota(jnp.int32, u.shape, 0)
        prev1 = carry_ref[1:2, :]
        prev2 = carry_ref[0:1, :]
        u1 = jnp.where(row == 0, prev1, pltpu.roll(u, 1, axis=0))
        u2 = jnp.where(row == 0, prev2, jnp.where(row == 1, prev1, pltpu.roll(u, 2, axis=0)))
        carry_ref[0:2, :] = u[tm - 2:tm, :]
        st_ref[...] = u[tm - 2:tm, :]
    else:
        u_ref[...] = u
        u2 = st0_ref[...]
        u1 = st1_ref[...]
    z = u2 * cw_ref[0:1, :] + u1 * cw_ref[1:2, :] + u * cw_ref[2:3, :]
    oc_ref[...] = _rms(gate * z, gc_ref[...])


def _mix_in_call(kernel, grid, in_specs, out_specs, out_shape, scratch, args):
    return pl.pallas_call(
        kernel, grid=grid, in_specs=in_specs, out_specs=out_specs, out_shape=out_shape,
        scratch_shapes=scratch,
        compiler_params=pltpu.CompilerParams(
            dimension_semantics=("arbitrary",) * len(grid), vmem_limit_bytes=VMEM_LIMIT),
        name="mix_in",
    )(*args)


def _mix_in_prompt(x2d, norm_g, w_in_bf16, conv_w, norm_gc, *, seq_len, d_attn, d_conv):
    t, d = x2d.shape
    tm = min(MIX_IN_TILE, seq_len)
    n_seq, per = t // seq_len, seq_len // tm
    const = lambda b, s: (0, 0)
    row = lambda width: pl.BlockSpec((tm, width), lambda b, s: (b * per + s, 0))
    col = pl.BlockSpec((None, d_attn, tm), lambda b, s: (b, 0, s))
    f32 = lambda *shape: jax.ShapeDtypeStruct(shape, F32)
    return _mix_in_call(
        functools.partial(_mix_in_kernel, d_attn=d_attn, d_conv=d_conv, sequential=True),
        (n_seq, per),
        [row(d), pl.BlockSpec((1, d), const),
         pl.BlockSpec(w_in_bf16.shape, const, pipeline_mode=pl.Buffered(1)),
         pl.BlockSpec((CONV_WIDTH, d_conv), const), pl.BlockSpec((1, d_conv), const)],
        [row(d_attn)] * 3 + [col] * 2 + [row(d_conv),
                                         pl.BlockSpec((None, CONV_WIDTH - 1, d_conv),
                                                      lambda b, s: (b, 0, 0))],
        [f32(t, d_attn)] * 3 + [f32(n_seq, d_attn, seq_len)] * 2
        + [f32(t, d_conv), f32(n_seq, CONV_WIDTH - 1, d_conv)],
        [pltpu.VMEM((8, d_conv), F32)],
        (x2d, norm_g, w_in_bf16, conv_w, norm_gc))


def _mix_in_sample(x2d, norm_g, w_in_bf16, conv_w, norm_gc, st0, st1, *, d_attn, d_conv):
    t, d = x2d.shape
    full = lambda arr: pl.BlockSpec(arr.shape, lambda i: (0,) * arr.ndim)
    f32 = lambda *shape: jax.ShapeDtypeStruct(shape, F32)
    args = (x2d, norm_g, w_in_bf16, conv_w, norm_gc, st0, st1)
    outs = [f32(t, d_attn)] * 3 + [f32(t, d_conv)] * 2
    return _mix_in_call(
        functools.partial(_mix_in_kernel, d_attn=d_attn, d_conv=d_conv, sequential=False),
        (1,), [full(a) for a in args], [full(o) for o in outs], outs, [], args)


def _attn_prompt_kernel(q_ref, k_ref, v_ref, o_ref, m_s, l_s, a_s, *, seq_len):
    w = WIN_KEYS
    scale = HEAD_DIM ** -0.5 * LOG2_E
    r_i = lax.broadcasted_iota(jnp.int32, (2 * w, 2 * w), 0) & (w - 1)
    c_i = lax.broadcasted_iota(jnp.int32, (2 * w, 2 * w), 1)
    mask_cur = (lax.broadcasted_iota(jnp.int32, (2 * w, w), 1)
                <= lax.broadcasted_iota(jnp.int32, (2 * w, w), 0) & (w - 1))
    mask_both = jnp.logical_and(c_i >= r_i, c_i - w <= r_i)
    first_head = lax.broadcasted_iota(jnp.int32, (w, 2 * HEAD_DIM), 1) < HEAD_DIM
    dn_t = (((1,), (1,)), ((), ()))

    def rows(start, dil):
        if dil > 1:
            return pl.ds(start, w, stride=dil)
        return pl.ds(start if isinstance(start, int) else pl.multiple_of(start, w), w)

    def run_branch(dil, first, last):
        span = dil * w
        nb = seq_len // span

        def blocks(its, with_prev):
            mask = mask_both if with_prev else mask_cur

            def issue_scores(it):
                g = it % dil
                n = it // dil
                c = rows(g + n * span, dil)
                qb = (q_ref[c, :] * scale).astype(BF16)
                zero = jnp.zeros_like(qb)
                q = jnp.concatenate([jnp.where(first_head, qb, zero),
                                     jnp.where(first_head, zero, qb)], axis=0)
                k = k_ref[c, :].astype(BF16)
                v = v_ref[c, :].astype(BF16)
                if with_prev:
                    p = rows(g + (n - 1) * span, dil)
                    k = jnp.concatenate([k_ref[p, :].astype(BF16), k], axis=0)
                    v = jnp.concatenate([v_ref[p, :].astype(BF16), v], axis=0)
                return c, lax.dot_general(q, k, dn_t, preferred_element_type=F32), v

            def finish(c, s, v):
                s = jnp.where(mask, s, NEG)
                m = jnp.max(s, axis=-1, keepdims=True)
                p = jnp.exp2(s - m).astype(BF16)
                ones = jnp.ones((v.shape[0], 2 * HEAD_DIM), BF16)
                acc_l = jnp.dot(p, jnp.concatenate([v, ones], axis=1), preferred_element_type=F32)
                acc, l = acc_l[:, :2 * HEAD_DIM], acc_l[:, 2 * HEAD_DIM:]
                m_b = jnp.where(first_head, m[:w], m[w:])
                l_b = jnp.where(first_head, l[:w], l[w:])
                a_b = jnp.where(first_head, acc[:w], acc[w:])
                if not first:
                    m_o = m_s[c, :]
                    m_n = jnp.maximum(m_o, m_b)
                    w_o = jnp.exp2(m_o - m_n)
                    w_b = jnp.exp2(m_b - m_n)
                    l_b = w_o * l_s[c, :] + w_b * l_b
                    a_b = w_o * a_s[c, :] + w_b * a_b
                    m_b = m_n
                if last:
                    o_ref[c, :] = a_b / l_b
                else:
                    m_s[c, :] = m_b
                    l_s[c, :] = l_b
                    a_s[c, :] = a_b

            in_flight = []
            for i in range(len(its) + ATTN_LAG):
                if i < len(its):
                    in_flight.append(issue_scores(its[i]))
                if i >= ATTN_LAG:
                    finish(*in_flight.pop(0))

        def run(lo, hi, with_prev):
            u = ATTN_UNROLL
            trips = (hi - lo) // u

            def body(t, carry):
                blocks([lo + t * u + j for j in range(u)], with_prev)
                return carry

            if trips:
                lax.fori_loop(0, trips, body, 0)
            if lo + trips * u < hi:
                blocks(list(range(lo + trips * u, hi)), with_prev)

        run(0, dil, False)
        run(dil, dil * nb, True)

    order = sorted(DILATIONS, reverse=True)
    for i, dil in enumerate(order):
        run_branch(dil, i == 0, i == len(order) - 1)


def _attn_sample_kernel(q_ref, kn_ref, vn_ref, kt_ref, vt_ref, o_ref):
    n_heads, dh, w_buf = kt_ref.shape
    delta = w_buf - lax.broadcasted_iota(jnp.int32, (1, w_buf), 1)
    cnt = jnp.zeros((1, w_buf), F32)
    for dil in DILATIONS:
        assert dil & (dil - 1) == 0
        member = jnp.where(delta <= dil * WIN_KEYS, 1.0, 0.0)
        cnt = cnt + jnp.where((delta & (dil - 1)) == 0, member, 0.0)
    eye = (lax.broadcasted_iota(jnp.int32, (dh, dh), 0)
           == lax.broadcasted_iota(jnp.int32, (dh, dh), 1))
    to_col = lambda r: jnp.sum(jnp.where(eye, r, 0.0), axis=1, keepdims=True)
    to_row = lambda c: jnp.sum(jnp.where(eye, c, 0.0), axis=0, keepdims=True)
    outs = []
    for h in range(n_heads):
        sl = slice(h * dh, (h + 1) * dh)
        q = q_ref[:, sl] * (HEAD_DIM ** -0.5)
        s_self = jnp.sum(q * kn_ref[:, sl], axis=1, keepdims=True)
        s = jnp.sum(to_col(q) * kt_ref[h], axis=0, keepdims=True)
        s = jnp.where(cnt > 0.0, s, NEG)
        m = jnp.maximum(jnp.max(s, axis=1, keepdims=True), s_self)
        p = cnt * jnp.exp(s - m)
        p_self = len(DILATIONS) * jnp.exp(s_self - m)
        l = jnp.sum(p, axis=1, keepdims=True) + p_self
        acc = jnp.sum(p * vt_ref[h], axis=1, keepdims=True)
        outs.append((to_row(acc) + p_self * vn_ref[:, sl]) / l)
    o_ref[...] = jnp.concatenate(outs, axis=1)


def _attn_kernel(q_ref, k_ref, v_ref, qs_ref, kn_ref, vn_ref, kt_ref, vt_ref, o_ref, os_ref,
                 m_s, l_s, a_s, *, seq_len):
    _attn_sample_kernel(qs_ref, kn_ref, vn_ref, kt_ref, vt_ref, os_ref)
    _attn_prompt_kernel(q_ref, k_ref, v_ref, o_ref, m_s, l_s, a_s, seq_len=seq_len)


def _attention(q, k, v, qs, k_new, v_new, cache_kt, cache_vt, *, n_seq, seq_len):
    t, d_attn = q.shape
    db, n_heads, dh, w_buf = cache_kt.shape
    pair = 2 * HEAD_DIM
    pairs = d_attn // pair
    assert db == n_seq * pairs, "one sample sequence per prompt grid step"
    spec = pl.BlockSpec((seq_len, pair), lambda b, h: (b, h))
    head_spec = pl.BlockSpec((None, 1, d_attn), lambda b, h: (b * pairs + h, 0, 0))
    cache_spec = pl.BlockSpec((None, n_heads, dh, w_buf), lambda b, h: (b * pairs + h, 0, 0, 0))
    return pl.pallas_call(
        functools.partial(_attn_kernel, seq_len=seq_len),
        grid=(n_seq, pairs),
        in_specs=[spec] * 3 + [head_spec] * 3 + [cache_spec] * 2,
        out_specs=[spec, head_spec],
        out_shape=[jax.ShapeDtypeStruct((t, d_attn), F32),
                   jax.ShapeDtypeStruct((db, 1, d_attn), F32)],
        scratch_shapes=[pltpu.VMEM((seq_len, pair), F32)] * 3,
        compiler_params=pltpu.CompilerParams(
            dimension_semantics=("arbitrary", "arbitrary"), vmem_limit_bytes=VMEM_LIMIT),
        name="attention",
    )(q, k, v, qs, k_new, v_new, cache_kt, cache_vt)


R_E0, R_E1, R_G0, R_G1, R_POS0, R_POS1 = range(6)
ROUTER_LANE0 = N_GROUPS
CHUNK = 16
CHUNKS_PER_BLOCK = EXPERT_BLOCK // CHUNK


def _max_tile_chunks(tm):
    return (2 * tm + (CHUNK - 1) * N_EXPERTS) // CHUNK


def _local_rows(tm):
    return 2 * tm + N_EXPERTS * CHUNK


def _mix_out_kernel(*refs, n_tiles, has_tail):
    if has_tail:
        *tile_in, tail_ref, h_ref, route_ref, xs_ref, cnt_ref, tok_s, pos_s = refs
    else:
        *tile_in, h_ref, route_ref, xs_ref, cnt_ref, tok_s, pos_s = refs
    i = pl.program_id(0)
    route = lambda slot, *between: _route_tile(*tile_in, h_ref, route_ref, cnt_ref,
                                               tok_s.at[slot], pos_s.at[slot], *between)
    sort = lambda slot: _sort_tile(tok_s.at[slot], pos_s.at[slot], xs_ref)

    @pl.when(i == 0)
    def _():
        route(0)

    for parity in range(2):
        @pl.when(jnp.logical_and(jnp.logical_and(i >= 1, i < n_tiles), i % 2 == parity))
        def _():
            route(parity, lambda: sort(1 - parity))

    @pl.when(i == n_tiles)
    def _():
        sort((n_tiles - 1) % 2)

    if has_tail:
        @pl.when(i == n_tiles + 1)
        def _():
            rows = tail_ref.shape[0]
            xs_ref[0:rows, :] = tail_ref[...]
            xs_ref[rows:, :] = jnp.zeros((xs_ref.shape[0] - rows, xs_ref.shape[1]), xs_ref.dtype)


def _sort_tile(tok_ref, pos_ref, xs_ref):
    tm = tok_ref.shape[0]
    l1 = pos_ref[R_POS0:R_POS0 + 1, :].astype(jnp.int32)
    l2 = pos_ref[R_POS1:R_POS1 + 1, :].astype(jnp.int32)
    srow = lax.broadcasted_iota(jnp.int32, (xs_ref.shape[0], tm), 0)
    perm = jnp.where(srow == l1, 1.0, jnp.where(srow == l2, 1.0, 0.0)).astype(BF16)
    xs_ref[...] = jnp.dot(perm, tok_ref[...], preferred_element_type=F32).astype(BF16)


def _route_tile(x_ref, a_ref, oc_ref, ga_ref, wo_ref, gf_ref, wr_ref, br_ref,
                h_ref, route_ref, cnt_ref, tok_ref, pos_ref, after_projections=lambda: None):
    d_attn = a_ref.shape[1]
    tm, d = x_ref.shape
    a = _rms(a_ref[...], ga_ref[...]).astype(BF16)
    mix = jnp.dot(a, wo_ref[0:d_attn, :], preferred_element_type=F32)
    mix = mix + jnp.dot(oc_ref[...].astype(BF16), wo_ref[d_attn:, :], preferred_element_type=F32)
    h = x_ref[...] + mix
    h_ref[...] = h
    tok = _rms(h, gf_ref[...])

    tok_hi = tok.astype(BF16)
    tok_lo = (tok - tok_hi.astype(F32)).astype(BF16)
    hi_part = jnp.dot(tok_hi, wr_ref[...], preferred_element_type=F32)
    lo_part = jnp.dot(tok_lo, wr_ref[:, :LANES], preferred_element_type=F32)
    logits = hi_part[:, :LANES] + hi_part[:, LANES:] + lo_part + br_ref[...]
    after_projections()
    lane = lax.broadcasted_iota(jnp.int32, logits.shape, 1)
    big = jnp.int32(LANES)
    neg_inf = jnp.float32(-jnp.inf)

    def top1(vals):
        best = jnp.max(vals, axis=-1, keepdims=True)
        idx = jnp.min(jnp.where(vals == best, lane, big), axis=-1, keepdims=True)
        return best, idx

    is_group = lane < N_GROUPS
    lg = jnp.where(is_group, logits, neg_inf)
    mg, g_sel = top1(lg)
    p_group = 1.0 / jnp.sum(jnp.where(is_group, jnp.exp(lg - mg), 0.0), axis=-1, keepdims=True)

    lo = ROUTER_LANE0 + g_sel * EXPERTS_PER_GROUP
    in_group = jnp.logical_and(lane >= lo, lane < lo + EXPERTS_PER_GROUP)
    le = jnp.where(in_group, logits, neg_inf)
    v1, i1 = top1(le)
    v2, i2 = top1(jnp.where(lane == i1, neg_inf, le))
    e2 = jnp.exp(v2 - v1)
    gate1 = p_group / (1.0 + e2)
    gate2 = p_group * e2 / (1.0 + e2)

    oh1 = lane == i1
    oh2 = lane == i2
    both = jnp.where(jnp.logical_or(oh1, oh2), 1.0, 0.0)
    r_i = lax.broadcasted_iota(jnp.int32, (tm, tm), 0)
    c_i = lax.broadcasted_iota(jnp.int32, (tm, tm), 1)
    strict_lower = jnp.where(c_i < r_i, 1.0, 0.0).astype(BF16)
    before = jnp.dot(strict_lower, both.astype(BF16), preferred_element_type=F32)
    chunks = jnp.floor((jnp.sum(both, axis=0, keepdims=True) + (CHUNK - 1)) * (1.0 / CHUNK))
    u_r = lax.broadcasted_iota(jnp.int32, (LANES, LANES), 0)
    u_c = lax.broadcasted_iota(jnp.int32, (LANES, LANES), 1)
    strict_upper = jnp.where(u_r < u_c, 1.0, 0.0).astype(BF16)
    chunks8 = jnp.broadcast_to(chunks, (8, LANES))
    first_row = CHUNK * jnp.dot(chunks8.astype(BF16), strict_upper,
                                preferred_element_type=F32)[0:1, :]
    pos = first_row + before
    pos1 = jnp.sum(jnp.where(oh1, pos, 0.0), axis=-1, keepdims=True)
    pos2 = jnp.sum(jnp.where(oh2, pos, 0.0), axis=-1, keepdims=True)
    cnt_ref[...] = jnp.where(lax.broadcasted_iota(jnp.int32, (8, LANES), 0) == 0, chunks8, 0.0)

    rec = jnp.zeros(logits.shape, F32)
    for col, val in ((R_E0, (i1 - ROUTER_LANE0).astype(F32)), (R_E1, (i2 - ROUTER_LANE0).astype(F32)),
                     (R_G0, gate1), (R_G1, gate2), (R_POS0, pos1), (R_POS1, pos2)):
        rec = jnp.where(lane == col, val, rec)
    route_ref[...] = rec
    tok_ref[...] = tok_hi
    pos_ref[...] = rec.T[0:pos_ref.shape[0], :]


def _mix_out(x2d, attn, oconv, norm_ga, w_out_bf16, norm_gf, w_router, b_router, tail=None):
    t, d = x2d.shape
    d_attn, d_conv = attn.shape[1], oconv.shape[1]
    tm = min(ROW_TILE, t)
    nt = t // tm
    r_l = _local_rows(tm)
    has_tail = tail is not None
    tile = lambda i: jnp.minimum(i, nt - 1)
    sorted_block = lambda i: jnp.minimum(jnp.maximum(i - 1, 0), nt - 1 + has_tail)
    row = lambda width: pl.BlockSpec((tm, width), lambda i: (tile(i), 0))
    full = lambda arr: pl.BlockSpec(arr.shape, lambda i: (0, 0))
    args = [x2d, attn, oconv, norm_ga, w_out_bf16, norm_gf, w_router, b_router]
    in_specs = [row(d), row(d_attn), row(d_conv)] + [full(a) for a in args[3:]]
    if has_tail:
        assert tail.shape[0] <= r_l and tail.shape[1] == d
        args.append(tail)
        in_specs.append(full(tail))
    return pl.pallas_call(
        functools.partial(_mix_out_kernel, n_tiles=nt, has_tail=has_tail),
        grid=(nt + 1 + has_tail,),
        in_specs=in_specs,
        out_specs=[row(d), row(LANES), pl.BlockSpec((r_l, d), lambda i: (sorted_block(i), 0)),
                   pl.BlockSpec((None, 8, LANES), lambda i: (tile(i), 0, 0))],
        out_shape=[jax.ShapeDtypeStruct((t, d), F32), jax.ShapeDtypeStruct((t, LANES), F32),
                   jax.ShapeDtypeStruct(((nt + has_tail) * r_l, d), BF16),
                   jax.ShapeDtypeStruct((nt, 8, LANES), F32)],
        scratch_shapes=[pltpu.VMEM((2, tm, d), BF16), pltpu.VMEM((2, 8, tm), F32)],
        compiler_params=pltpu.CompilerParams(
            dimension_semantics=("arbitrary",), vmem_limit_bytes=VMEM_LIMIT),
        name="mix_out",
    )(*args)


def _sorted_layout(tile_chunks, tile_row0, max_local, n_blocks):
    nt, n_exp = tile_chunks.shape
    cpb = CHUNKS_PER_BLOCK
    i32 = jnp.int32
    seg = jnp.sum(tile_chunks, axis=0)
    padded = (seg + cpb - 1) // cpb * cpb
    pend = jnp.cumsum(padded)
    pstart = pend - padded
    tile_incl = jnp.cumsum(tile_chunks, axis=0)
    tile_excl = tile_incl - tile_chunks
    local_incl = jnp.cumsum(tile_chunks, axis=1)
    local_excl = local_incl - tile_chunks
    base = pstart[None, :] + tile_excl

    block_first = jnp.arange(n_blocks, dtype=i32) * cpb
    block_e = jnp.minimum(jnp.sum((pend[None, :] <= block_first[:, None]).astype(i32), axis=1),
                          n_exp - 1)
    n_used = (pend[-1:] // cpb).astype(i32)

    onehot_pick = lambda onehot, table: jnp.sum(jnp.where(onehot, table, 0), axis=-1)

    is_e = block_e[:, None] == jnp.arange(n_exp, dtype=i32)[None, :]
    of_expert = lambda table_te: onehot_pick(is_e[:, None, :], table_te[None, :, :])
    incl_b, cnt_b, lexcl_b = of_expert(tile_incl), of_expert(tile_chunks), of_expert(local_excl)
    q = (block_first - onehot_pick(is_e, pstart[None, :]))[:, None] + jnp.arange(cpb, dtype=i32)
    tile_q = jnp.minimum(jnp.sum((incl_b[:, None, :] <= q[:, :, None]).astype(i32), axis=2), nt - 1)
    is_t = tile_q[:, :, None] == jnp.arange(nt, dtype=i32)[None, None, :]
    of_tile = lambda table_bt: onehot_pick(is_t, table_bt[:, None, :])
    local_chunk = of_tile(lexcl_b) + q - of_tile(incl_b - cnt_b)
    seg_b = onehot_pick(is_e, seg[None, :])
    block_rows = (CHUNK * jnp.clip(seg_b - q[:, 0], 0, cpb)).astype(i32)
    in_run = jnp.logical_and(q >= 0, q < seg_b[:, None])
    src_row = jnp.where(in_run, of_tile(tile_row0[None, :]) + CHUNK * local_chunk, 0)
    src_row = src_row.reshape(-1).astype(i32)

    c = jnp.arange(max_local, dtype=i32)
    e_c = jnp.minimum(jnp.sum((local_incl[:, None, :] <= c[None, :, None]).astype(i32), axis=2),
                      n_exp - 1)
    is_ec = e_c[:, :, None] == jnp.arange(n_exp, dtype=i32)[None, None, :]
    of_run = lambda table_te: onehot_pick(is_ec, table_te[:, None, :])
    global_chunk = of_run(base) + c[None, :] - of_run(local_excl)
    tile_src = jnp.where(c[None, :] < local_incl[:, -1:], CHUNK * global_chunk, 0).astype(i32)
    e_ids = jnp.arange(n_exp, dtype=i32)
    later = jnp.logical_and(seg[None, :] > 0, e_ids[None, :] > e_ids[:, None])
    next_e = jnp.min(jnp.where(later, e_ids[None, :], n_exp), axis=1)
    next_e = jnp.where(next_e < n_exp, next_e, -1).astype(i32)
    return block_e.astype(i32), n_used, src_row, tile_src, next_e, block_rows


def _chunk_gather(src_ref, hbm_ref, buf, sems, item, slot, n_chunks, *, wait, priorities=(0, 1)):
    for c in range(n_chunks):
        row = 0 if wait else pl.multiple_of(src_ref[item * n_chunks + c], CHUNK)
        copy = pltpu.make_async_copy(hbm_ref.at[pl.ds(row, CHUNK)],
                                     buf.at[slot, pl.ds(c * CHUNK, CHUNK)], sems.at[slot])
        if wait:
            copy.wait()
        else:
            copy.start(priority=priorities[c % len(priorities)])


def _prefetched(gather, step, n_items, body):
    slot = step % 2

    @pl.when(jnp.logical_and(step == 0, n_items > 0))
    def _():
        gather(0, 0, wait=False)

    @pl.when(step + 1 < n_items)
    def _():
        gather(step + 1, 1 - slot, wait=False)

    body(slot, lambda: gather(step, slot, wait=True))


def _experts_kernel(block_e_ref, n_used_ref, src_ref, next_e_ref, block_rows_ref, xs_ref, wg_hbm,
                    wu_hbm, wd_hbm, y_hbm, xblk, sems, ybuf, ysems, wg_f, wu_f, wd_f, wsems, wg_b, wu_b, wd_b,
                    run_ref):
    rows = EXPERT_BLOCK
    n_blocks = y_hbm.shape[0] // rows
    n_used = n_used_ref[0]
    gather = functools.partial(_chunk_gather, src_ref, xs_ref, xblk, sems,
                               n_chunks=CHUNKS_PER_BLOCK, priorities=(0,))

    def weight_copies(expert, slot):
        return [pltpu.make_async_copy(hbm.at[expert], stage.at[slot], wsems.at[slot])
                for hbm, stage in ((wg_hbm, wg_f), (wu_hbm, wu_f), (wd_hbm, wd_f))]

    def y_copy(blk, slot):
        start = blk * rows if isinstance(blk, int) else pl.multiple_of(blk * rows, rows)
        return pltpu.make_async_copy(ybuf.at[slot], y_hbm.at[pl.ds(start, rows)], ysems.at[slot])

    def block(b, carry):
        e = block_e_ref[b]
        new_expert = jnp.logical_or(b == 0, e != block_e_ref[jnp.maximum(b - 1, 0)])

        @pl.when(jnp.logical_and(new_expert, b < n_used))
        def _():
            @pl.when(b == 0)
            def _():
                run_ref[0] = 0
                for copy in weight_copies(e, 0):
                    copy.start(priority=1)

            @pl.when(b > 0)
            def _():
                run_ref[0] = run_ref[0] + 1

            slot = run_ref[0] % 2
            nxt = next_e_ref[e]

            @pl.when(nxt >= 0)
            def _():
                for copy in weight_copies(nxt, 1 - slot):
                    copy.start(priority=1)

            for copy in weight_copies(e, slot):
                copy.wait()
            wg_b[...] = wg_f[slot].astype(BF16)
            wu_b[...] = wu_f[slot].astype(BF16)
            wd_b[...] = wd_f[slot].astype(BF16)

        def body(slot, wait_current):
            @pl.when(b >= 2)
            def _():
                y_copy(b - 2, slot).wait()

            @pl.when(b < n_used)
            def _():
                wait_current()

            valid = block_rows_ref[b]
            for m in range(EXPERT_ROW_STEP, rows + 1, EXPERT_ROW_STEP):
                @pl.when(jnp.logical_and(valid > m - EXPERT_ROW_STEP, valid <= m))
                def _():
                    x = xblk[slot, 0:m, :]
                    gate = jnp.dot(x, wg_b[...], preferred_element_type=F32)
                    up = jnp.dot(x, wu_b[...], preferred_element_type=F32)
                    hid = gate * (1.0 / (1.0 + jnp.exp(-gate))) * up
                    ybuf[slot, 0:m, :] = jnp.dot(hid.astype(BF16), wd_b[...],
                                                 preferred_element_type=F32).astype(BF16)
                    if m < rows:
                        ybuf[slot, m:rows, :] = jnp.zeros((rows - m, ybuf.shape[2]), ybuf.dtype)

            @pl.when(valid == 0)
            def _():
                ybuf[slot] = jnp.zeros(ybuf.shape[1:], ybuf.dtype)

            y_copy(b, slot).start()

        _prefetched(gather, b, n_used, body)
        return carry

    lax.fori_loop(0, n_blocks, block, 0)
    for blk in range(max(n_blocks - 2, 0), n_blocks):
        y_copy(blk, blk % 2).wait()


def _experts(block_e, n_used, src_row, next_e, block_rows, xs, w_gate, w_up, w_down):
    n_blocks = block_e.shape[0]
    _, d, d_exp = w_gate.shape
    blk = EXPERT_BLOCK
    any_spec = pl.BlockSpec(memory_space=pl.ANY)
    return pl.pallas_call(
        _experts_kernel,
        grid_spec=pltpu.PrefetchScalarGridSpec(
            num_scalar_prefetch=5,
            grid=(1,),
            in_specs=[any_spec] * 4,
            out_specs=any_spec,
            scratch_shapes=[pltpu.VMEM((2, blk, d), BF16), pltpu.SemaphoreType.DMA((2,)),
                            pltpu.VMEM((2, blk, d), BF16), pltpu.SemaphoreType.DMA((2,)),
                            pltpu.VMEM((2, d, d_exp), F32), pltpu.VMEM((2, d, d_exp), F32),
                            pltpu.VMEM((2, d_exp, d), F32), pltpu.SemaphoreType.DMA((2,)),
                            pltpu.VMEM((d, d_exp), BF16), pltpu.VMEM((d, d_exp), BF16),
                            pltpu.VMEM((d_exp, d), BF16), pltpu.SMEM((1,), jnp.int32)],
        ),
        out_shape=jax.ShapeDtypeStruct((n_blocks * blk, d), BF16),
        compiler_params=pltpu.CompilerParams(
            dimension_semantics=("arbitrary",), vmem_limit_bytes=VMEM_LIMIT),
        name="experts",
    )(block_e, n_used, src_row, next_e, block_rows, xs, w_gate, w_up, w_down)


def _combine_kernel(src_ref, h_ref, route_ref, gn_ref, ybuf_ref, o_ref, yloc, sems):
    tm = h_ref.shape[0]
    r_l = yloc.shape[1]
    gather = functools.partial(_chunk_gather, src_ref, ybuf_ref, yloc, sems,
                               n_chunks=r_l // CHUNK)

    def body(slot, wait_current):
        wait_current()
        y = yloc[slot]
        route = route_ref[...]
        l0 = route[:, R_POS0:R_POS0 + 1].astype(jnp.int32)
        l1 = route[:, R_POS1:R_POS1 + 1].astype(jnp.int32)
        srow = lax.broadcasted_iota(jnp.int32, (tm, r_l), 1)
        gates = jnp.where(srow == l0, route[:, R_G0:R_G0 + 1],
                          jnp.where(srow == l1, route[:, R_G1:R_G1 + 1], 0.0)).astype(BF16)
        f = jnp.dot(gates, y, preferred_element_type=F32)
        o_ref[...] = _rms(h_ref[...] + f, gn_ref[...])

    _prefetched(gather, pl.program_id(0), pl.num_programs(0), body)


def _combine(tile_src, h, route, norm_g, ybuf):
    t, d = h.shape
    tm = min(ROW_TILE, t)
    r_l = _local_rows(tm)
    return pl.pallas_call(
        _combine_kernel,
        grid_spec=pltpu.PrefetchScalarGridSpec(
            num_scalar_prefetch=1,
            grid=(t // tm,),
            in_specs=[pl.BlockSpec((tm, d), lambda i, src: (i, 0)),
                      pl.BlockSpec((tm, LANES), lambda i, src: (i, 0)),
                      pl.BlockSpec((1, d), lambda i, src: (0, 0)),
                      pl.BlockSpec(memory_space=pl.ANY)],
            out_specs=pl.BlockSpec((tm, d), lambda i, src: (i, 0)),
            scratch_shapes=[pltpu.VMEM((2, r_l, d), BF16),
                            pltpu.SemaphoreType.DMA((2,))],
        ),
        out_shape=jax.ShapeDtypeStruct((t, d), F32),
        compiler_params=pltpu.CompilerParams(
            dimension_semantics=("arbitrary",), vmem_limit_bytes=VMEM_LIMIT),
        name="combine",
    )(tile_src, h, route, norm_g, ybuf)


def kernel(x_prompt, x_sample, cache_k, cache_v, state_conv, norm_mix, w_in, conv_w, norm_out_attn,
           norm_out_conv, w_out, norm_ffn, w_router_group, b_router_group, w_router_expert,
           b_router_expert, w_gate, w_up, w_down, norm_final):
    n_seq, seq_len, d = x_prompt.shape
    db, ds, _ = x_sample.shape
    depth = w_in.shape[0]
    _, _, w_buf, n_heads, dh = cache_k.shape
    d_attn = n_heads * dh
    d_conv = d - d_attn
    assert depth == 1 and ds == 1 and dh == HEAD_DIM
    assert seq_len % (max(DILATIONS) * WIN_KEYS) == 0 and seq_len <= max(DILATIONS) * WIN_KEYS
    layer = 0
    tp, ts = n_seq * seq_len, db

    xp = x_prompt.reshape(tp, d)
    xs = x_sample.reshape(ts, d)
    row = lambda vec: vec.reshape(1, -1)
    w_in_b = w_in[layer].astype(BF16)
    w_out_b = w_out[layer].astype(BF16)
    g_mix, g_oa, g_oc, g_ffn = (row(norm_mix[layer]), row(norm_out_attn[layer]),
                                row(norm_out_conv[layer]), row(norm_ffn[layer]))
    st0, st1 = state_conv[layer, :, 0, :], state_conv[layer, :, 1, :]

    qp, kp, vp, kp_t, vp_t, ocp, conv_p = _mix_in_prompt(
        xp, g_mix, w_in_b, conv_w[layer], g_oc, seq_len=seq_len, d_attn=d_attn, d_conv=d_conv)
    qs, ks, vs, ocs, us = _mix_in_sample(
        xs, g_mix, w_in_b, conv_w[layer], g_oc, st0, st1, d_attn=d_attn, d_conv=d_conv)

    heads = lambda a: a.reshape(ts, 1, d_attn)
    positions_last = lambda c: jnp.transpose(c, (0, 2, 3, 1))
    attn_p, attn_s = _attention(qp, kp, vp, heads(qs), heads(ks), heads(vs),
                                positions_last(cache_k[layer]), positions_last(cache_v[layer]),
                                n_seq=n_seq, seq_len=seq_len)
    attn_s = attn_s.reshape(ts, d_attn)

    n_route = N_GROUPS + N_EXPERTS
    w_router = jnp.zeros((d, LANES), F32).at[:, :N_GROUPS].set(w_router_group[layer])
    w_router = w_router.at[:, N_GROUPS:n_route].set(w_router_expert[layer])
    b_router = jnp.zeros((1, LANES), F32).at[0, :N_GROUPS].set(b_router_group[layer])
    b_router = b_router.at[0, N_GROUPS:n_route].set(b_router_expert[layer])
    w_router_hi = w_router.astype(BF16)
    w_router_lo = (w_router - w_router_hi.astype(F32)).astype(BF16)
    mix_out = functools.partial(_mix_out, norm_ga=g_oa, w_out_bf16=w_out_b, norm_gf=g_ffn,
                                w_router=jnp.concatenate([w_router_hi, w_router_lo], axis=1),
                                b_router=b_router)
    h_s, route_s, xs_s, cnt_s = mix_out(xs, attn_s, ocs)
    assert cnt_s.shape[0] == 1
    h_p, route_p, xs_all, cnt_p = mix_out(xp, attn_p, ocp, tail=xs_s)

    tile_chunks = jnp.concatenate([cnt_p[:, 0, ROUTER_LANE0:n_route],
                                   cnt_s[:, 0, ROUTER_LANE0:n_route]], axis=0).astype(jnp.int32)
    ntp, nts = cnt_p.shape[0], cnt_s.shape[0]
    tm_p, tm_s = tp // ntp, ts // nts
    rl_p, rl_s = _local_rows(tm_p), _local_rows(tm_s)
    tile_row0 = jnp.arange(ntp + nts, dtype=jnp.int32) * rl_p
    total_chunks = ntp * _max_tile_chunks(tm_p) + nts * _max_tile_chunks(tm_s)
    n_blocks = -(-(total_chunks + N_EXPERTS * (CHUNKS_PER_BLOCK - 1)) // CHUNKS_PER_BLOCK)
    block_e, n_used, src_row, tile_src, next_e, block_rows = _sorted_layout(
        tile_chunks, tile_row0, rl_p // CHUNK, n_blocks)
    ybuf = _experts(block_e, n_used, src_row, next_e, block_rows, xs_all, w_gate[layer],
                    w_up[layer], w_down[layer])
    g_fin = row(norm_final)
    y_p = _combine(tile_src[:ntp].reshape(-1), h_p, route_p, g_fin, ybuf)
    y_s = _combine(tile_src[ntp:, :rl_s // CHUNK].reshape(-1), h_s, route_s, g_fin, ybuf)

    w_keep = min(max(DILATIONS) * WIN_KEYS, seq_len)
    kv5 = lambda a_t: jnp.transpose(a_t.reshape(n_seq, n_heads, dh, seq_len),
                                    (0, 3, 1, 2))[None, :, seq_len - w_keep:]
    conv_s = jnp.stack([st1, us], axis=1)[None]
    kvs = lambda a: a.reshape(1, ts, 1, n_heads, dh)
    return (y_p.reshape(n_seq, seq_len, d), y_s.reshape(db, ds, d), kv5(kp_t), kv5(vp_t),
            conv_p[None], kvs(ks), kvs(vs), conv_s)
```

```python
import functools

import jax
import jax.numpy as jnp
from jax import lax
from jax.experimental import pallas as pl
from jax.experimental.pallas import tpu as pltpu

HEAD_DIM = 64
WIN_KEYS = 128
DILATIONS = (1, 4, 16)
CONV_WIDTH = 3
N_GROUPS = 4
EXPERTS_PER_GROUP = 8
N_EXPERTS = N_GROUPS * EXPERTS_PER_GROUP
EPS = 1e-6
NEG = -1e30
LOG2_E = 1.4426950408889634

LANES = 128
ROW_TILE = 512
MIX_IN_TILE = 1024
EXPERT_BLOCK = 512
EXPERT_ROW_STEP = 128
ATTN_LAG = 3
ATTN_UNROLL = 16
VMEM_LIMIT = 56 * 1024 * 1024

F32 = jnp.float32
BF16 = jnp.bfloat16


def _rms(x, g):
    return x * lax.rsqrt(jnp.mean(x * x, axis=-1, keepdims=True) + EPS) * g


def _mix_in_kernel(*refs, d_attn, d_conv, sequential):
    if sequential:
        (x_ref, g_ref, w_ref, cw_ref, gc_ref,
         q_ref, k_ref, v_ref, kt_ref, vt_ref, oc_ref, st_ref, carry_ref) = refs
    else:
        (x_ref, g_ref, w_ref, cw_ref, gc_ref, st0_ref, st1_ref,
         q_ref, k_ref, v_ref, oc_ref, u_ref) = refs
    x = x_ref[...]
    xb = _rms(x, g_ref[...]).astype(BF16)

    def proj(lo, width):
        return jnp.dot(xb, w_ref[:, lo:lo + width], preferred_element_type=F32)

    q_ref[...] = proj(0, d_attn)
    k = proj(d_attn, d_attn)
    v = proj(2 * d_attn, d_attn)
    k_ref[...] = k
    v_ref[...] = v
    gate = proj(3 * d_attn, d_conv)
    u = proj(3 * d_attn + d_conv, d_conv) * proj(3 * d_attn + 2 * d_conv, d_conv)

    tm = x.shape[0]
    if sequential:
        kt_ref[...] = k.T
        vt_ref[...] = v.T

        @pl.when(pl.program_id(1) == 0)
        def _():
            carry_ref[...] = jnp.zeros_like(carry_ref)

        row = lax.broadcasted_iota(jnp.int32, u.shape, 0)
        prev1 = carry_ref[1:2, :]
        prev2 = carry_ref[0:1, :]
        u1 = jnp.where(row == 0, prev1, pltpu.roll(u, 1, axis=0))
        u2 = jnp.where(row == 0, prev2, jnp.where(row == 1, prev1, pltpu.roll(u, 2, axis=0)))
        carry_ref[0:2, :] = u[tm - 2:tm, :]
        st_ref[...] = u[tm - 2:tm, :]
    else:
        u_ref[...] = u
        u2 = st0_ref[...]
        u1 = st1_ref[...]
    z = u2 * cw_ref[0:1, :] + u1 * cw_ref[1:2, :] + u * cw_ref[2:3, :]
    oc_ref[...] = _rms(gate * z, gc_ref[...])


def _mix_in_call(kernel, grid, in_specs, out_specs, out_shape, scratch, args):
    return pl.pallas_call(
        kernel, grid=grid, in_specs=in_specs, out_specs=out_specs, out_shape=out_shape,
        scratch_shapes=scratch,
        compiler_params=pltpu.CompilerParams(
            dimension_semantics=("arbitrary",) * len(grid), vmem_limit_bytes=VMEM_LIMIT),
        name="mix_in",
    )(*args)


def _mix_in_prompt(x2d, norm_g, w_in_bf16, conv_w, norm_gc, *, seq_len, d_attn, d_conv):
    t, d = x2d.shape
    tm = min(MIX_IN_TILE, seq_len)
    n_seq, per = t // seq_len, seq_len // tm
    const = lambda b, s: (0, 0)
    row = lambda width: pl.BlockSpec((tm, width), lambda b, s: (b * per + s, 0))
    col = pl.BlockSpec((None, d_attn, tm), lambda b, s: (b, 0, s))
    f32 = lambda *shape: jax.ShapeDtypeStruct(shape, F32)
    return _mix_in_call(
        functools.partial(_mix_in_kernel, d_attn=d_attn, d_conv=d_conv, sequential=True),
        (n_seq, per),
        [row(d), pl.BlockSpec((1, d), const),
         pl.BlockSpec(w_in_bf16.shape, const, pipeline_mode=pl.Buffered(1)),
         pl.BlockSpec((CONV_WIDTH, d_conv), const), pl.BlockSpec((1, d_conv), const)],
        [row(d_attn)] * 3 + [col] * 2 + [row(d_conv),
                                         pl.BlockSpec((None, CONV_WIDTH - 1, d_conv),
                                                      lambda b, s: (b, 0, 0))],
        [f32(t, d_attn)] * 3 + [f32(n_seq, d_attn, seq_len)] * 2
        + [f32(t, d_conv), f32(n_seq, CONV_WIDTH - 1, d_conv)],
        [pltpu.VMEM((8, d_conv), F32)],
        (x2d, norm_g, w_in_bf16, conv_w, norm_gc))


def _mix_in_sample(x2d, norm_g, w_in_bf16, conv_w, norm_gc, st0, st1, *, d_attn, d_conv):
    t, d = x2d.shape
    full = lambda arr: pl.BlockSpec(arr.shape, lambda i: (0,) * arr.ndim)
    f32 = lambda *shape: jax.ShapeDtypeStruct(shape, F32)
    args = (x2d, norm_g, w_in_bf16, conv_w, norm_gc, st0, st1)
    outs = [f32(t, d_attn)] * 3 + [f32(t, d_conv)] * 2
    return _mix_in_call(
        functools.partial(_mix_in_kernel, d_attn=d_attn, d_conv=d_conv, sequential=False),
        (1,), [full(a) for a in args], [full(o) for o in outs], outs, [], args)


def _attn_prompt_kernel(q_ref, k_ref, v_ref, o_ref, m_s, l_s, a_s, *, seq_len):
    w = WIN_KEYS
    scale = HEAD_DIM ** -0.5 * LOG2_E
    r_i = lax.broadcasted_iota(jnp.int32, (2 * w, 2 * w), 0) & (w - 1)
    c_i = lax.broadcasted_iota(jnp.int32, (2 * w, 2 * w), 1)
    mask_cur = (lax.broadcasted_iota(jnp.int32, (2 * w, w), 1)
                <= lax.broadcasted_iota(jnp.int32, (2 * w, w), 0) & (w - 1))
    mask_both = jnp.logical_and(c_i >= r_i, c_i - w <= r_i)
    first_head = lax.broadcasted_iota(jnp.int32, (w, 2 * HEAD_DIM), 1) < HEAD_DIM
    dn_t = (((1,), (1,)), ((), ()))

    def rows(start, dil):
        if dil > 1:
            return pl.ds(start, w, stride=dil)
        return pl.ds(start if isinstance(start, int) else pl.multiple_of(start, w), w)

    def run_branch(dil, first, last):
        span = dil * w
        nb = seq_len // span

        def blocks(its, with_prev):
            mask = mask_both if with_prev else mask_cur

            def issue_scores(it):
                g = it % dil
                n = it // dil
                c = rows(g + n * span, dil)
                qb = (q_ref[c, :] * scale).astype(BF16)
                zero = jnp.zeros_like(qb)
                q = jnp.concatenate([jnp.where(first_head, qb, zero),
                                     jnp.where(first_head, zero, qb)], axis=0)
                k = k_ref[c, :].astype(BF16)
                v = v_ref[c, :].astype(BF16)
                if with_prev:
                    p = rows(g + (n - 1) * span, dil)
                    k = jnp.concatenate([k_ref[p, :].astype(BF16), k], axis=0)
                    v = jnp.concatenate([v_ref[p, :].astype(BF16), v], axis=0)
                return c, lax.dot_general(q, k, dn_t, preferred_element_type=F32), v

            def finish(c, s, v):
                s = jnp.where(mask, s, NEG)
                m = jnp.max(s, axis=-1, keepdims=True)
                p = jnp.exp2(s - m).astype(BF16)
                ones = jnp.ones((v.shape[0], 2 * HEAD_DIM), BF16)
                acc_l = jnp.dot(p, jnp.concatenate([v, ones], axis=1), preferred_element_type=F32)
                acc, l = acc_l[:, :2 * HEAD_DIM], acc_l[:, 2 * HEAD_DIM:]
                m_b = jnp.where(first_head, m[:w], m[w:])
                l_b = jnp.where(first_head, l[:w], l[w:])
                a_b = jnp.where(first_head, acc[:w], acc[w:])
                if not first:
                    m_o = m_s[c, :]
                    m_n = jnp.maximum(m_o, m_b)
                    w_o = jnp.exp2(m_o - m_n)
                    w_b = jnp.exp2(m_b - m_n)
                    l_b = w_o * l_s[c, :] + w_b * l_b
                    a_b = w_o * a_s[c, :] + w_b * a_b
                    m_b = m_n
                if last:
                    o_ref[c, :] = a_b / l_b
                else:
                    m_s[c, :] = m_b
                    l_s[c, :] = l_b
                    a_s[c, :] = a_b

            in_flight = []
            for i in range(len(its) + ATTN_LAG):
                if i < len(its):
                    in_flight.append(issue_scores(its[i]))
                if i >= ATTN_LAG:
                    finish(*in_flight.pop(0))

        def run(lo, hi, with_prev):
            u = ATTN_UNROLL
            trips = (hi - lo) // u

            def body(t, carry):
                blocks([lo + t * u + j for j in range(u)], with_prev)
                return carry

            if trips:
                lax.fori_loop(0, trips, body, 0)
            if lo + trips * u < hi:
                blocks(list(range(lo + trips * u, hi)), with_prev)

        run(0, dil, False)
        run(dil, dil * nb, True)

    order = sorted(DILATIONS, reverse=True)
    for i, dil in enumerate(order):
        run_branch(dil, i == 0, i == len(order) - 1)


def _attn_sample_kernel(q_ref, kn_ref, vn_ref, kt_ref, vt_ref, o_ref):
    n_heads, dh, w_buf = kt_ref.shape
    delta = w_buf - lax.broadcasted_iota(jnp.int32, (1, w_buf), 1)
    cnt = jnp.zeros((1, w_buf), F32)
    for dil in DILATIONS:
        assert dil & (dil - 1) == 0
        member = jnp.where(delta <= dil * WIN_KEYS, 1.0, 0.0)
        cnt = cnt + jnp.where((delta & (dil - 1)) == 0, member, 0.0)
    eye = (lax.broadcasted_iota(jnp.int32, (dh, dh), 0)
           == lax.broadcasted_iota(jnp.int32, (dh, dh), 1))
    to_col = lambda r: jnp.sum(jnp.where(eye, r, 0.0), axis=1, keepdims=True)
    to_row = lambda c: jnp.sum(jnp.where(eye, c, 0.0), axis=0, keepdims=True)
    outs = []
    for h in range(n_heads):
        sl = slice(h * dh, (h + 1) * dh)
        q = q_ref[:, sl] * (HEAD_DIM ** -0.5)
        s_self = jnp.sum(q * kn_ref[:, sl], axis=1, keepdims=True)
        s = jnp.sum(to_col(q) * kt_ref[h], axis=0, keepdims=True)
        s = jnp.where(cnt > 0.0, s, NEG)
        m = jnp.maximum(jnp.max(s, axis=1, keepdims=True), s_self)
        p = cnt * jnp.exp(s - m)
        p_self = len(DILATIONS) * jnp.exp(s_self - m)
        l = jnp.sum(p, axis=1, keepdims=True) + p_self
        acc = jnp.sum(p * vt_ref[h], axis=1, keepdims=True)
        outs.append((to_row(acc) + p_self * vn_ref[:, sl]) / l)
    o_ref[...] = jnp.concatenate(outs, axis=1)


def _attn_kernel(q_ref, k_ref, v_ref, qs_ref, kn_ref, vn_ref, kt_ref, vt_ref, o_ref, os_ref,
                 m_s, l_s, a_s, *, seq_len):
    _attn_sample_kernel(qs_ref, kn_ref, vn_ref, kt_ref, vt_ref, os_ref)
    _attn_prompt_kernel(q_ref, k_ref, v_ref, o_ref, m_s, l_s, a_s, seq_len=seq_len)


def _attention(q, k, v, qs, k_new, v_new, cache_kt, cache_vt, *, n_seq, seq_len):
    t, d_attn = q.shape
    db, n_heads, dh, w_buf = cache_kt.shape
    pair = 2 * HEAD_DIM
    pairs = d_attn // pair
    assert db == n_seq * pairs, "one sample sequence per prompt grid step"
    spec = pl.BlockSpec((seq_len, pair), lambda b, h: (b, h))
    head_spec = pl.BlockSpec((None, 1, d_attn), lambda b, h: (b * pairs + h, 0, 0))
    cache_spec = pl.BlockSpec((None, n_heads, dh, w_buf), lambda b, h: (b * pairs + h, 0, 0, 0))
    return pl.pallas_call(
        functools.partial(_attn_kernel, seq_len=seq_len),
        grid=(n_seq, pairs),
        in_specs=[spec] * 3 + [head_spec] * 3 + [cache_spec] * 2,
        out_specs=[spec, head_spec],
        out_shape=[jax.ShapeDtypeStruct((t, d_attn), F32),
                   jax.ShapeDtypeStruct((db, 1, d_attn), F32)],
        scratch_shapes=[pltpu.VMEM((seq_len, pair), F32)] * 3,
        compiler_params=pltpu.CompilerParams(
            dimension_semantics=("arbitrary", "arbitrary"), vmem_limit_bytes=VMEM_LIMIT),
        name="attention",
    )(q, k, v, qs, k_new, v_new, cache_kt, cache_vt)


R_E0, R_E1, R_G0, R_G1, R_POS0, R_POS1 = range(6)
ROUTER_LANE0 = N_GROUPS
CHUNK = 16
CHUNKS_PER_BLOCK = EXPERT_BLOCK // CHUNK


def _max_tile_chunks(tm):
    return (2 * tm + (CHUNK - 1) * N_EXPERTS) // CHUNK


def _local_rows(tm):
    return 2 * tm + N_EXPERTS * CHUNK


def _mix_out_kernel(*refs, n_tiles, has_tail):
    if has_tail:
        *tile_in, tail_ref, h_ref, route_ref, xs_ref, cnt_ref, tok_s, pos_s = refs
    else:
        *tile_in, h_ref, route_ref, xs_ref, cnt_ref, tok_s, pos_s = refs
    i = pl.program_id(0)
    route = lambda slot, *between: _route_tile(*tile_in, h_ref, route_ref, cnt_ref,
                                               tok_s.at[slot], pos_s.at[slot], *between)
    sort = lambda slot: _sort_tile(tok_s.at[slot], pos_s.at[slot], xs_ref)

    @pl.when(i == 0)
    def _():
        route(0)

    for parity in range(2):
        @pl.when(jnp.logical_and(jnp.logical_and(i >= 1, i < n_tiles), i % 2 == parity))
        def _():
            route(parity, lambda: sort(1 - parity))

    @pl.when(i == n_tiles)
    def _():
        sort((n_tiles - 1) % 2)

    if has_tail:
        @pl.when(i == n_tiles + 1)
        def _():
            rows = tail_ref.shape[0]
            xs_ref[0:rows, :] = tail_ref[...]
            xs_ref[rows:, :] = jnp.zeros((xs_ref.shape[0] - rows, xs_ref.shape[1]), xs_ref.dtype)


def _sort_tile(tok_ref, pos_ref, xs_ref):
    tm = tok_ref.shape[0]
    l1 = pos_ref[R_POS0:R_POS0 + 1, :].astype(jnp.int32)
    l2 = pos_ref[R_POS1:R_POS1 + 1, :].astype(jnp.int32)
    srow = lax.broadcasted_iota(jnp.int32, (xs_ref.shape[0], tm), 0)
    perm = jnp.where(srow == l1, 1.0, jnp.where(srow == l2, 1.0, 0.0)).astype(BF16)
    xs_ref[...] = jnp.dot(perm, tok_ref[...], preferred_element_type=F32).astype(BF16)


def _route_tile(x_ref, a_ref, oc_ref, ga_ref, wo_ref, gf_ref, wr_ref, br_ref,
                h_ref, route_ref, cnt_ref, tok_ref, pos_ref, after_projections=lambda: None):
    d_attn = a_ref.shape[1]
    tm, d = x_ref.shape
    a = _rms(a_ref[...], ga_ref[...]).astype(BF16)
    mix = jnp.dot(a, wo_ref[0:d_attn, :], preferred_element_type=F32)
    mix = mix + jnp.dot(oc_ref[...].astype(BF16), wo_ref[d_attn:, :], preferred_element_type=F32)
    h = x_ref[...] + mix
    h_ref[...] = h
    tok = _rms(h, gf_ref[...])

    tok_hi = tok.astype(BF16)
    tok_lo = (tok - tok_hi.astype(F32)).astype(BF16)
    hi_part = jnp.dot(tok_hi, wr_ref[...], preferred_element_type=F32)
    lo_part = jnp.dot(tok_lo, wr_ref[:, :LANES], preferred_element_type=F32)
    logits = hi_part[:, :LANES] + hi_part[:, LANES:] + lo_part + br_ref[...]
    after_projections()
    lane = lax.broadcasted_iota(jnp.int32, logits.shape, 1)
    big = jnp.int32(LANES)
    neg_inf = jnp.float32(-jnp.inf)

    def top1(vals):
        best = jnp.max(vals, axis=-1, keepdims=True)
        idx = jnp.min(jnp.where(vals == best, lane, big), axis=-1, keepdims=True)
        return best, idx

    is_group = lane < N_GROUPS
    lg = jnp.where(is_group, logits, neg_inf)
    mg, g_sel = top1(lg)
    p_group = 1.0 / jnp.sum(jnp.where(is_group, jnp.exp(lg - mg), 0.0), axis=-1, keepdims=True)

    lo = ROUTER_LANE0 + g_sel * EXPERTS_PER_GROUP
    in_group = jnp.logical_and(lane >= lo, lane < lo + EXPERTS_PER_GROUP)
    le = jnp.where(in_group, logits, neg_inf)
    v1, i1 = top1(le)
    v2, i2 = top1(jnp.where(lane == i1, neg_inf, le))
    e2 = jnp.exp(v2 - v1)
    gate1 = p_group / (1.0 + e2)
    gate2 = p_group * e2 / (1.0 + e2)

    oh1 = lane == i1
    oh2 = lane == i2
    both = jnp.where(jnp.logical_or(oh1, oh2), 1.0, 0.0)
    r_i = lax.broadcasted_iota(jnp.int32, (tm, tm), 0)
    c_i = lax.broadcasted_iota(jnp.int32, (tm, tm), 1)
    strict_lower = jnp.where(c_i < r_i, 1.0, 0.0).astype(BF16)
    before = jnp.dot(strict_lower, both.astype(BF16), preferred_element_type=F32)
    chunks = jnp.floor((jnp.sum(both, axis=0, keepdims=True) + (CHUNK - 1)) * (1.0 / CHUNK))
    u_r = lax.broadcasted_iota(jnp.int32, (LANES, LANES), 0)
    u_c = lax.broadcasted_iota(jnp.int32, (LANES, LANES), 1)
    strict_upper = jnp.where(u_r < u_c, 1.0, 0.0).astype(BF16)
    chunks8 = jnp.broadcast_to(chunks, (8, LANES))
    first_row = CHUNK * jnp.dot(chunks8.astype(BF16), strict_upper,
                                preferred_element_type=F32)[0:1, :]
    pos = first_row + before
    pos1 = jnp.sum(jnp.where(oh1, pos, 0.0), axis=-1, keepdims=True)
    pos2 = jnp.sum(jnp.where(oh2, pos, 0.0), axis=-1, keepdims=True)
    cnt_ref[...] = jnp.where(lax.broadcasted_iota(jnp.int32, (8, LANES), 0) == 0, chunks8, 0.0)

    rec = jnp.zeros(logits.shape, F32)
    for col, val in ((R_E0, (i1 - ROUTER_LANE0).astype(F32)), (R_E1, (i2 - ROUTER_LANE0).astype(F32)),
                     (R_G0, gate1), (R_G1, gate2), (R_POS0, pos1), (R_POS1, pos2)):
        rec = jnp.where(lane == col, val, rec)
    route_ref[...] = rec
    tok_ref[...] = tok_hi
    pos_ref[...] = rec.T[0:pos_ref.shape[0], :]


def _mix_out(x2d, attn, oconv, norm_ga, w_out_bf16, norm_gf, w_router, b_router, tail=None):
    t, d = x2d.shape
    d_attn, d_conv = attn.shape[1], oconv.shape[1]
    tm = min(ROW_TILE, t)
    nt = t // tm
    r_l = _local_rows(tm)
    has_tail = tail is not None
    tile = lambda i: jnp.minimum(i, nt - 1)
    sorted_block = lambda i: jnp.minimum(jnp.maximum(i - 1, 0), nt - 1 + has_tail)
    row = lambda width: pl.BlockSpec((tm, width), lambda i: (tile(i), 0))
    full = lambda arr: pl.BlockSpec(arr.shape, lambda i: (0, 0))
    args = [x2d, attn, oconv, norm_ga, w_out_bf16, norm_gf, w_router, b_router]
    in_specs = [row(d), row(d_attn), row(d_conv)] + [full(a) for a in args[3:]]
    if has_tail:
        assert tail.shape[0] <= r_l and tail.shape[1] == d
        args.append(tail)
        in_specs.append(full(tail))
    return pl.pallas_call(
        functools.partial(_mix_out_kernel, n_tiles=nt, has_tail=has_tail),
        grid=(nt + 1 + has_tail,),
        in_specs=in_specs,
        out_specs=[row(d), row(LANES), pl.BlockSpec((r_l, d), lambda i: (sorted_block(i), 0)),
                   pl.BlockSpec((None, 8, LANES), lambda i: (tile(i), 0, 0))],
        out_shape=[jax.ShapeDtypeStruct((t, d), F32), jax.ShapeDtypeStruct((t, LANES), F32),
                   jax.ShapeDtypeStruct(((nt + has_tail) * r_l, d), BF16),
                   jax.ShapeDtypeStruct((nt, 8, LANES), F32)],
        scratch_shapes=[pltpu.VMEM((2, tm, d), BF16), pltpu.VMEM((2, 8, tm), F32)],
        compiler_params=pltpu.CompilerParams(
            dimension_semantics=("arbitrary",), vmem_limit_bytes=VMEM_LIMIT),
        name="mix_out",
    )(*args)


def _sorted_layout(tile_chunks, tile_row0, max_local, n_blocks):
    nt, n_exp = tile_chunks.shape
    cpb = CHUNKS_PER_BLOCK
    i32 = jnp.int32
    seg = jnp.sum(tile_chunks, axis=0)
    padded = (seg + cpb - 1) // cpb * cpb
    pend = jnp.cumsum(padded)
    pstart = pend - padded
    tile_incl = jnp.cumsum(tile_chunks, axis=0)
    tile_excl = tile_incl - tile_chunks
    local_incl = jnp.cumsum(tile_chunks, axis=1)
    local_excl = local_incl - tile_chunks
    base = pstart[None, :] + tile_excl

    block_first = jnp.arange(n_blocks, dtype=i32) * cpb
    block_e = jnp.minimum(jnp.sum((pend[None, :] <= block_first[:, None]).astype(i32), axis=1),
                          n_exp - 1)
    n_used = (pend[-1:] // cpb).astype(i32)

    onehot_pick = lambda onehot, table: jnp.sum(jnp.where(onehot, table, 0), axis=-1)

    is_e = block_e[:, None] == jnp.arange(n_exp, dtype=i32)[None, :]
    of_expert = lambda table_te: onehot_pick(is_e[:, None, :], table_te[None, :, :])
    incl_b, cnt_b, lexcl_b = of_expert(tile_incl), of_expert(tile_chunks), of_expert(local_excl)
    q = (block_first - onehot_pick(is_e, pstart[None, :]))[:, None] + jnp.arange(cpb, dtype=i32)
    tile_q = jnp.minimum(jnp.sum((incl_b[:, None, :] <= q[:, :, None]).astype(i32), axis=2), nt - 1)
    is_t = tile_q[:, :, None] == jnp.arange(nt, dtype=i32)[None, None, :]
    of_tile = lambda table_bt: onehot_pick(is_t, table_bt[:, None, :])
    local_chunk = of_tile(lexcl_b) + q - of_tile(incl_b - cnt_b)
    seg_b = onehot_pick(is_e, seg[None, :])
    block_rows = (CHUNK * jnp.clip(seg_b - q[:, 0], 0, cpb)).astype(i32)
    in_run = jnp.logical_and(q >= 0, q < seg_b[:, None])
    src_row = jnp.where(in_run, of_tile(tile_row0[None, :]) + CHUNK * local_chunk, 0)
    src_row = src_row.reshape(-1).astype(i32)

    c = jnp.arange(max_local, dtype=i32)
    e_c = jnp.minimum(jnp.sum((local_incl[:, None, :] <= c[None, :, None]).astype(i32), axis=2),
                      n_exp - 1)
    is_ec = e_c[:, :, None] == jnp.arange(n_exp, dtype=i32)[None, None, :]
    of_run = lambda table_te: onehot_pick(is_ec, table_te[:, None, :])
    global_chunk = of_run(base) + c[None, :] - of_run(local_excl)
    tile_src = jnp.where(c[None, :] < local_incl[:, -1:], CHUNK * global_chunk, 0).astype(i32)
    e_ids = jnp.arange(n_exp, dtype=i32)
    later = jnp.logical_and(seg[None, :] > 0, e_ids[None, :] > e_ids[:, None])
    next_e = jnp.min(jnp.where(later, e_ids[None, :], n_exp), axis=1)
    next_e = jnp.where(next_e < n_exp, next_e, -1).astype(i32)
    return block_e.astype(i32), n_used, src_row, tile_src, next_e, block_rows


def _chunk_gather(src_ref, hbm_ref, buf, sems, item, slot, n_chunks, *, wait, priorities=(0, 1)):
    for c in range(n_chunks):
        row = 0 if wait else pl.multiple_of(src_ref[item * n_chunks + c], CHUNK)
        copy = pltpu.make_async_copy(hbm_ref.at[pl.ds(row, CHUNK)],
                                     buf.at[slot, pl.ds(c * CHUNK, CHUNK)], sems.at[slot])
        if wait:
            copy.wait()
        else:
            copy.start(priority=priorities[c % len(priorities)])


def _prefetched(gather, step, n_items, body):
    slot = step % 2

    @pl.when(jnp.logical_and(step == 0, n_items > 0))
    def _():
        gather(0, 0, wait=False)

    @pl.when(step + 1 < n_items)
    def _():
        gather(step + 1, 1 - slot, wait=False)

    body(slot, lambda: gather(step, slot, wait=True))


def _experts_kernel(block_e_ref, n_used_ref, src_ref, next_e_ref, block_rows_ref, xs_ref, wg_hbm,
                    wu_hbm, wd_hbm, y_hbm, xblk, sems, ybuf, ysems, wg_f, wu_f, wd_f, wsems, wg_b, wu_b, wd_b,
                    run_ref):
    rows = EXPERT_BLOCK
    n_blocks = y_hbm.shape[0] // rows
    n_used = n_used_ref[0]
    gather = functools.partial(_chunk_gather, src_ref, xs_ref, xblk, sems,
                               n_chunks=CHUNKS_PER_BLOCK, priorities=(0,))

    def weight_copies(expert, slot):
        return [pltpu.make_async_copy(hbm.at[expert], stage.at[slot], wsems.at[slot])
                for hbm, stage in ((wg_hbm, wg_f), (wu_hbm, wu_f), (wd_hbm, wd_f))]

    def y_copy(blk, slot):
        start = blk * rows if isinstance(blk, int) else pl.multiple_of(blk * rows, rows)
        return pltpu.make_async_copy(ybuf.at[slot], y_hbm.at[pl.ds(start, rows)], ysems.at[slot])

    def block(b, carry):
        e = block_e_ref[b]
        new_expert = jnp.logical_or(b == 0, e != block_e_ref[jnp.maximum(b - 1, 0)])

        @pl.when(jnp.logical_and(new_expert, b < n_used))
        def _():
            @pl.when(b == 0)
            def _():
                run_ref[0] = 0
                for copy in weight_copies(e, 0):
                    copy.start(priority=1)

            @pl.when(b > 0)
            def _():
                run_ref[0] = run_ref[0] + 1

            slot = run_ref[0] % 2
            nxt = next_e_ref[e]

            @pl.when(nxt >= 0)
            def _():
                for copy in weight_copies(nxt, 1 - slot):
                    copy.start(priority=1)

            for copy in weight_copies(e, slot):
                copy.wait()
            wg_b[...] = wg_f[slot].astype(BF16)
            wu_b[...] = wu_f[slot].astype(BF16)
            wd_b[...] = wd_f[slot].astype(BF16)

        def body(slot, wait_current):
            @pl.when(b >= 2)
            def _():
                y_copy(b - 2, slot).wait()

            @pl.when(b < n_used)
            def _():
                wait_current()

            valid = block_rows_ref[b]
            for m in range(EXPERT_ROW_STEP, rows + 1, EXPERT_ROW_STEP):
                @pl.when(jnp.logical_and(valid > m - EXPERT_ROW_STEP, valid <= m))
                def _():
                    x = xblk[slot, 0:m, :]
                    gate = jnp.dot(x, wg_b[...], preferred_element_type=F32)
                    up = jnp.dot(x, wu_b[...], preferred_element_type=F32)
                    hid = gate * (1.0 / (1.0 + jnp.exp(-gate))) * up
                    ybuf[slot, 0:m, :] = jnp.dot(hid.astype(BF16), wd_b[...],
                                                 preferred_element_type=F32).astype(BF16)
                    if m < rows:
                        ybuf[slot, m:rows, :] = jnp.zeros((rows - m, ybuf.shape[2]), ybuf.dtype)

            @pl.when(valid == 0)
            def _():
                ybuf[slot] = jnp.zeros(ybuf.shape[1:], ybuf.dtype)

            y_copy(b, slot).start(priority=1)

        _prefetched(gather, b, n_used, body)
        return carry

    lax.fori_loop(0, n_blocks, block, 0)
    for blk in range(max(n_blocks - 2, 0), n_blocks):
        y_copy(blk, blk % 2).wait()


def _experts(block_e, n_used, src_row, next_e, block_rows, xs, w_gate, w_up, w_down):
    n_blocks = block_e.shape[0]
    _, d, d_exp = w_gate.shape
    blk = EXPERT_BLOCK
    any_spec = pl.BlockSpec(memory_space=pl.ANY)
    return pl.pallas_call(
        _experts_kernel,
        grid_spec=pltpu.PrefetchScalarGridSpec(
            num_scalar_prefetch=5,
            grid=(1,),
            in_specs=[any_spec] * 4,
            out_specs=any_spec,
            scratch_shapes=[pltpu.VMEM((2, blk, d), BF16), pltpu.SemaphoreType.DMA((2,)),
                            pltpu.VMEM((2, blk, d), BF16), pltpu.SemaphoreType.DMA((2,)),
                            pltpu.VMEM((2, d, d_exp), F32), pltpu.VMEM((2, d, d_exp), F32),
                            pltpu.VMEM((2, d_exp, d), F32), pltpu.SemaphoreType.DMA((2,)),
                            pltpu.VMEM((d, d_exp), BF16), pltpu.VMEM((d, d_exp), BF16),
                            pltpu.VMEM((d_exp, d), BF16), pltpu.SMEM((1,), jnp.int32)],
        ),
        out_shape=jax.ShapeDtypeStruct((n_blocks * blk, d), BF16),
        compiler_params=pltpu.CompilerParams(
            dimension_semantics=("arbitrary",), vmem_limit_bytes=VMEM_LIMIT),
        name="experts",
    )(block_e, n_used, src_row, next_e, block_rows, xs, w_gate, w_up, w_down)


def _combine_kernel(src_ref, h_ref, route_ref, gn_ref, ybuf_ref, o_ref, yloc, sems):
    tm = h_ref.shape[0]
    r_l = yloc.shape[1]
    gather = functools.partial(_chunk_gather, src_ref, ybuf_ref, yloc, sems,
                               n_chunks=r_l // CHUNK)

    def body(slot, wait_current):
        wait_current()
        y = yloc[slot]
        route = route_ref[...]
        l0 = route[:, R_POS0:R_POS0 + 1].astype(jnp.int32)
        l1 = route[:, R_POS1:R_POS1 + 1].astype(jnp.int32)
        srow = lax.broadcasted_iota(jnp.int32, (tm, r_l), 1)
        gates = jnp.where(srow == l0, route[:, R_G0:R_G0 + 1],
                          jnp.where(srow == l1, route[:, R_G1:R_G1 + 1], 0.0)).astype(BF16)
        f = jnp.dot(gates, y, preferred_element_type=F32)
        o_ref[...] = _rms(h_ref[...] + f, gn_ref[...])

    _prefetched(gather, pl.program_id(0), pl.num_programs(0), body)


def _combine(tile_src, h, route, norm_g, ybuf):
    t, d = h.shape
    tm = min(ROW_TILE, t)
    r_l = _local_rows(tm)
    return pl.pallas_call(
        _combine_kernel,
        grid_spec=pltpu.PrefetchScalarGridSpec(
            num_scalar_prefetch=1,
            grid=(t // tm,),
            in_specs=[pl.BlockSpec((tm, d), lambda i, src: (i, 0)),
                      pl.BlockSpec((tm, LANES), lambda i, src: (i, 0)),
                      pl.BlockSpec((1, d), lambda i, src: (0, 0)),
                      pl.BlockSpec(memory_space=pl.ANY)],
            out_specs=pl.BlockSpec((tm, d), lambda i, src: (i, 0)),
            scratch_shapes=[pltpu.VMEM((2, r_l, d), BF16),
                            pltpu.SemaphoreType.DMA((2,))],
        ),
        out_shape=jax.ShapeDtypeStruct((t, d), F32),
        compiler_params=pltpu.CompilerParams(
            dimension_semantics=("arbitrary",), vmem_limit_bytes=VMEM_LIMIT),
        name="combine",
    )(tile_src, h, route, norm_g, ybuf)


def kernel(x_prompt, x_sample, cache_k, cache_v, state_conv, norm_mix, w_in, conv_w, norm_out_attn,
           norm_out_conv, w_out, norm_ffn, w_router_group, b_router_group, w_router_expert,
           b_router_expert, w_gate, w_up, w_down, norm_final):
    n_seq, seq_len, d = x_prompt.shape
    db, ds, _ = x_sample.shape
    depth = w_in.shape[0]
    _, _, w_buf, n_heads, dh = cache_k.shape
    d_attn = n_heads * dh
    d_conv = d - d_attn
    assert depth == 1 and ds == 1 and dh == HEAD_DIM
    assert seq_len % (max(DILATIONS) * WIN_KEYS) == 0 and seq_len <= max(DILATIONS) * WIN_KEYS
    layer = 0
    tp, ts = n_seq * seq_len, db

    xp = x_prompt.reshape(tp, d)
    xs = x_sample.reshape(ts, d)
    row = lambda vec: vec.reshape(1, -1)
    w_in_b = w_in[layer].astype(BF16)
    w_out_b = w_out[layer].astype(BF16)
    g_mix, g_oa, g_oc, g_ffn = (row(norm_mix[layer]), row(norm_out_attn[layer]),
                                row(norm_out_conv[layer]), row(norm_ffn[layer]))
    st0, st1 = state_conv[layer, :, 0, :], state_conv[layer, :, 1, :]

    qp, kp, vp, kp_t, vp_t, ocp, conv_p = _mix_in_prompt(
        xp, g_mix, w_in_b, conv_w[layer], g_oc, seq_len=seq_len, d_attn=d_attn, d_conv=d_conv)
    qs, ks, vs, ocs, us = _mix_in_sample(
        xs, g_mix, w_in_b, conv_w[layer], g_oc, st0, st1, d_attn=d_attn, d_conv=d_conv)

    heads = lambda a: a.reshape(ts, 1, d_attn)
    positions_last = lambda c: jnp.transpose(c, (0, 2, 3, 1))
    attn_p, attn_s = _attention(qp, kp, vp, heads(qs), heads(ks), heads(vs),
                                positions_last(cache_k[layer]), positions_last(cache_v[layer]),
                                n_seq=n_seq, seq_len=seq_len)
    attn_s = attn_s.reshape(ts, d_attn)

    n_route = N_GROUPS + N_EXPERTS
    w_router = jnp.zeros((d, LANES), F32).at[:, :N_GROUPS].set(w_router_group[layer])
    w_router = w_router.at[:, N_GROUPS:n_route].set(w_router_expert[layer])
    b_router = jnp.zeros((1, LANES), F32).at[0, :N_GROUPS].set(b_router_group[layer])
    b_router = b_router.at[0, N_GROUPS:n_route].set(b_router_expert[layer])
    w_router_hi = w_router.astype(BF16)
    w_router_lo = (w_router - w_router_hi.astype(F32)).astype(BF16)
    mix_out = functools.partial(_mix_out, norm_ga=g_oa, w_out_bf16=w_out_b, norm_gf=g_ffn,
                                w_router=jnp.concatenate([w_router_hi, w_router_lo], axis=1),
                                b_router=b_router)
    h_s, route_s, xs_s, cnt_s = mix_out(xs, attn_s, ocs)
    assert cnt_s.shape[0] == 1
    h_p, route_p, xs_all, cnt_p = mix_out(xp, attn_p, ocp, tail=xs_s)

    tile_chunks = jnp.concatenate([cnt_p[:, 0, ROUTER_LANE0:n_route],
                                   cnt_s[:, 0, ROUTER_LANE0:n_route]], axis=0).astype(jnp.int32)
    ntp, nts = cnt_p.shape[0], cnt_s.shape[0]
    tm_p, tm_s = tp // ntp, ts // nts
    rl_p, rl_s = _local_rows(tm_p), _local_rows(tm_s)
    tile_row0 = jnp.arange(ntp + nts, dtype=jnp.int32) * rl_p
    total_chunks = ntp * _max_tile_chunks(tm_p) + nts * _max_tile_chunks(tm_s)
    n_blocks = -(-(total_chunks + N_EXPERTS * (CHUNKS_PER_BLOCK - 1)) // CHUNKS_PER_BLOCK)
    block_e, n_used, src_row, tile_src, next_e, block_rows = _sorted_layout(
        tile_chunks, tile_row0, rl_p // CHUNK, n_blocks)
    ybuf = _experts(block_e, n_used, src_row, next_e, block_rows, xs_all, w_gate[layer],
                    w_up[layer], w_down[layer])
    g_fin = row(norm_final)
    y_p = _combine(tile_src[:ntp].reshape(-1), h_p, route_p, g_fin, ybuf)
    y_s = _combine(tile_src[ntp:, :rl_s // CHUNK].reshape(-1), h_s, route_s, g_fin, ybuf)

    w_keep = min(max(DILATIONS) * WIN_KEYS, seq_len)
    kv5 = lambda a_t: jnp.transpose(a_t.reshape(n_seq, n_heads, dh, seq_len),
                                    (0, 3, 1, 2))[None, :, seq_len - w_keep:]
    conv_s = jnp.stack([st1, us], axis=1)[None]
    kvs = lambda a: a.reshape(1, ts, 1, n_heads, dh)
    return (y_p.reshape(n_seq, seq_len, d), y_s.reshape(db, ds, d), kv5(kp_t), kv5(vp_t),
            conv_p[None], kvs(ks), kvs(vs), conv_s)
```
